```python
import math
import jax, jax.numpy as jnp
from jax import lax
import numpy as np

D_MODEL = 1024
BATCH = 8
SEQ = 4096
DEPTH = 2

CHUNK = 64
N_MIXERS = 2
N_RWKV = (DEPTH + 1) // 2
N_SSM = DEPTH // 2
RWKV_HEAD = 64
RWKV_HEADS = D_MODEL // RWKV_HEAD
DECAY_LORA = 64
AAA_LORA = 64
GATE_LORA = 160
GN_EPS = 64e-5
SSM_GROUP = 16
SSM_GROUPS = D_MODEL // SSM_GROUP
SSM_STATE = 64
D_FF = 4 * D_MODEL
DN_ALPHA = (2.0 * DEPTH) ** 0.25
DN_BETA = (8.0 * DEPTH) ** -0.25
LN_EPS = 1e-5

kernel_name = "rwkv7_s5_interleaved_deepnorm_trunk"


def layer_norm(x, g, b):
    xf = x.astype(jnp.float32)
    mu = jnp.mean(xf, axis=-1, keepdims=True)
    var = jnp.mean(jnp.square(xf - mu), axis=-1, keepdims=True)
    return ((xf - mu) * lax.rsqrt(var + LN_EPS) * g + b).astype(x.dtype)


def cmul(ar, ai, br, bi):
    return ar * br - ai * bi, ar * bi + ai * br


def rwkv7_time_mix(x, mu, w0, w1, w2, a0, a1, a2, g1, g2, k_k, k_a, r_k,
                   wr, wk, wv, wo, lnx_g, lnx_b):
    bsz, seq, d = x.shape
    f32 = jnp.float32
    xx = jnp.pad(x, ((0, 0), (1, 0), (0, 0)))[:, :-1] - x
    xr = x + xx * mu[0]
    xw = x + xx * mu[1]
    xk = x + xx * mu[2]
    xv = x + xx * mu[3]
    xa = x + xx * mu[4]
    xg = x + xx * mu[5]
    r = xr @ wr
    w_pre = (w0 + jnp.tanh(xw @ w1) @ w2).astype(f32)
    decay = jnp.exp(-jnp.exp(-jax.nn.softplus(-w_pre) - 0.5))
    k = xk @ wk
    v = xv @ wv
    a = jax.nn.sigmoid(a0 + (xa @ a1) @ a2)
    g = jax.nn.sigmoid(xg @ g1) @ g2

    def heads(t):
        return t.astype(f32).reshape(bsz, seq, RWKV_HEADS, RWKV_HEAD)

    kk = heads(k * k_k)
    kk = kk / jnp.maximum(jnp.sqrt(jnp.sum(kk * kk, axis=-1, keepdims=True)), 1e-12)
    k = k * (1.0 + (a - 1.0) * k_a)
    rh, kh, vh, ah, wh = heads(r), heads(k), heads(v), heads(a), heads(decay)
    seq_in = tuple(jnp.moveaxis(t, 1, 0) for t in (rh, wh, kh, vh, -kk, kk * ah))

    def step(state, inp):
        r_t, w_t, k_t, v_t, a_t, b_t = inp
        sa = jnp.einsum('bhvk,bhk->bhv', state, a_t)
        state = (state * w_t[:, :, None, :] + sa[..., None] * b_t[:, :, None, :]
                 + v_t[..., None] * k_t[:, :, None, :])
        return state, jnp.einsum('bhvk,bhk->bhv', state, r_t)

    s0 = jnp.zeros((bsz, RWKV_HEADS, RWKV_HEAD, RWKV_HEAD), f32)
    _, o = lax.scan(step, s0, seq_in)
    o = jnp.moveaxis(o, 0, 1)
    om = jnp.mean(o, axis=-1, keepdims=True)
    ov = jnp.mean(jnp.square(o - om), axis=-1, keepdims=True)
    o = ((o - om) * lax.rsqrt(ov + GN_EPS)).reshape(bsz, seq, d) * lnx_g + lnx_b
    bonus = jnp.sum(rh * kh * r_k.astype(f32), axis=-1, keepdims=True) * vh
    o = o + bonus.reshape(bsz, seq, d)
    return (o.astype(x.dtype) * g) @ wo


def s5_mix(x, a_re, a_im, log_dt, b_re, b_im, c_re, c_im, d_skip, w_glu):
    bsz, seq, d = x.shape
    f32 = jnp.float32
    xf = x.astype(f32)
    dt = jnp.exp(log_dt.astype(f32))[:, None]
    lam_re = jnp.minimum(a_re.astype(f32), -1e-4)
    lam_im = a_im.astype(f32)
    mag = jnp.exp(dt * lam_re)
    abar_re = mag * jnp.cos(dt * lam_im)
    abar_im = mag * jnp.sin(dt * lam_im)
    den = lam_re * lam_re + lam_im * lam_im
    nr, ni = abar_re - 1.0, abar_im
    coef_re = (nr * lam_re + ni * lam_im) / den
    coef_im = (ni * lam_re - nr * lam_im) / den
    bbar_re, bbar_im = cmul(coef_re[..., None], coef_im[..., None],
                            b_re.astype(f32), b_im.astype(f32))
    cr_w, ci_w = c_re.astype(f32), c_im.astype(f32)

    n_chunks = seq // CHUNK
    u = xf.reshape(bsz, n_chunks, CHUNK, SSM_GROUPS, SSM_GROUP).transpose(1, 2, 0, 3, 4)
    a_ch_re = jnp.broadcast_to(abar_re, (CHUNK, 1, SSM_GROUPS, SSM_STATE))
    a_ch_im = jnp.broadcast_to(abar_im, (CHUNK, 1, SSM_GROUPS, SSM_STATE))

    def combine(e1, e2):
        a1r, a1i, b1r, b1i = e1
        a2r, a2i, b2r, b2i = e2
        ar, ai = cmul(a2r, a2i, a1r, a1i)
        br, bi = cmul(a2r, a2i, b1r, b1i)
        return ar, ai, br + b2r, bi + b2i

    def chunk_step(h, u_c):
        h_re, h_im = h
        bu_re = jnp.einsum('tbgc,gpc->tbgp', u_c, bbar_re)
        bu_im = jnp.einsum('tbgc,gpc->tbgp', u_c, bbar_im)
        pr, pi, sr, si = lax.associative_scan(combine, (a_ch_re, a_ch_im, bu_re, bu_im), axis=0)
        carry_re, carry_im = cmul(pr, pi, h_re[None], h_im[None])
        s_re = sr + carry_re
        s_im = si + carry_im
        y = (jnp.einsum('tbgp,gcp->tbgc', s_re, cr_w)
             - jnp.einsum('tbgp,gcp->tbgc', s_im, ci_w))
        return (s_re[-1], s_im[-1]), y

    h0 = jnp.zeros((bsz, SSM_GROUPS, SSM_STATE), f32)
    _, y = lax.scan(chunk_step, (h0, h0), u)
    y = y.transpose(2, 0, 1, 3, 4).reshape(bsz, seq, d) + xf * d_skip.astype(f32)
    y = jax.nn.gelu(y).astype(x.dtype)
    z = y @ w_glu
    return z[..., :d] * jax.nn.sigmoid(z[..., d:])


def sq_relu_mlp(x, w1, w2):
    return jnp.square(jax.nn.relu(x @ w1)) @ w2


def _fwd_setup_inputs(seed: int = 0) -> dict:
    key = jax.random.key(seed)
    ks = list(jax.random.split(key, 40))
    f32 = jnp.float32

    def nrm(i, shape, scale):
        return scale * jax.random.normal(ks[i], shape, f32)

    def uni(i, shape, lo, hi):
        return jax.random.uniform(ks[i], shape, f32, lo, hi)

    C, H, N = D_MODEL, RWKV_HEADS, RWKV_HEAD
    G, P, S = SSM_GROUPS, SSM_STATE, SSM_GROUP
    nr_, ns_ = N_RWKV, N_SSM
    glu = nrm(30, (ns_, C, 2 * C), C ** -0.5)
    glu = glu * jnp.concatenate([jnp.full((C,), DN_BETA, f32), jnp.ones((C,), f32)])
    a_im = jnp.pi * jnp.arange(P, dtype=f32)[None, None, :] + nrm(22, (ns_, G, P), 0.01)
    return {
        "x": nrm(0, (BATCH, SEQ, C), 1.0),
        "ln_g": 1.0 + nrm(1, (2 * DEPTH, C), 0.02),
        "ln_b": nrm(2, (2 * DEPTH, C), 0.02),
        "rw_mu": uni(3, (nr_, 6, C), 0.0, 1.0),
        "rw_w0": uni(4, (nr_, C), -6.0, -1.0),
        "rw_w1": nrm(5, (nr_, C, DECAY_LORA), C ** -0.5),
        "rw_w2": nrm(6, (nr_, DECAY_LORA, C), 0.1 * DECAY_LORA ** -0.5),
        "rw_a0": nrm(7, (nr_, C), 0.1),
        "rw_a1": nrm(8, (nr_, C, AAA_LORA), C ** -0.5),
        "rw_a2": nrm(9, (nr_, AAA_LORA, C), 0.1 * AAA_LORA ** -0.5),
        "rw_g1": nrm(10, (nr_, C, GATE_LORA), C ** -0.5),
        "rw_g2": nrm(11, (nr_, GATE_LORA, C), GATE_LORA ** -0.5),
        "rw_k_k": 0.85 + nrm(12, (nr_, C), 0.02),
        "rw_k_a": 1.0 + nrm(13, (nr_, C), 0.02),
        "rw_r_k": -0.04 + nrm(14, (nr_, H, N), 0.02),
        "rw_wr": nrm(15, (nr_, C, C), C ** -0.5),
        "rw_wk": nrm(16, (nr_, C, C), C ** -0.5),
        "rw_wv": nrm(17, (nr_, C, C), C ** -0.5),
        "rw_wo": nrm(18, (nr_, C, C), DN_BETA * C ** -0.5),
        "rw_lnx_g": 1.0 + nrm(19, (nr_, C), 0.02),
        "rw_lnx_b": nrm(20, (nr_, C), 0.02),
        "s5_a_re": -0.5 + nrm(21, (ns_, G, P), 0.01),
        "s5_a_im": a_im,
        "s5_log_dt": uni(23, (ns_, G), math.log(1e-3), math.log(1e-1)),
        "s5_b_re": nrm(24, (ns_, G, P, S), (2.0 * S) ** -0.5),
        "s5_b_im": nrm(25, (ns_, G, P, S), (2.0 * S) ** -0.5),
        "s5_c_re": nrm(26, (ns_, G, S, P), (2.0 * P) ** -0.5),
        "s5_c_im": nrm(27, (ns_, G, S, P), (2.0 * P) ** -0.5),
        "s5_d": nrm(28, (ns_, C), 1.0),
        "s5_w_glu": glu,
        "mlp_w1": nrm(31, (DEPTH, C, D_FF), C ** -0.5),
        "mlp_w2": nrm(32, (DEPTH, D_FF, C), DN_BETA * D_FF ** -0.5),
    }


def _fwd_reference(x, ln_g, ln_b, rw_mu, rw_w0, rw_w1, rw_w2, rw_a0, rw_a1, rw_a2, rw_g1, rw_g2,
              rw_k_k, rw_k_a, rw_r_k, rw_wr, rw_wk, rw_wv, rw_wo, rw_lnx_g, rw_lnx_b,
              s5_a_re, s5_a_im, s5_log_dt, s5_b_re, s5_b_im, s5_c_re, s5_c_im, s5_d, s5_w_glu,
              mlp_w1, mlp_w2):
    h = x
    for i in range(DEPTH):
        j = i // N_MIXERS
        if i % N_MIXERS == 0:
            mix = rwkv7_time_mix(h, rw_mu[j], rw_w0[j], rw_w1[j], rw_w2[j], rw_a0[j], rw_a1[j],
                                 rw_a2[j], rw_g1[j], rw_g2[j], rw_k_k[j], rw_k_a[j], rw_r_k[j],
                                 rw_wr[j], rw_wk[j], rw_wv[j], rw_wo[j], rw_lnx_g[j], rw_lnx_b[j])
        else:
            mix = s5_mix(h, s5_a_re[j], s5_a_im[j], s5_log_dt[j], s5_b_re[j], s5_b_im[j],
                         s5_c_re[j], s5_c_im[j], s5_d[j], s5_w_glu[j])
        h = layer_norm(DN_ALPHA * h + mix, ln_g[2 * i], ln_b[2 * i])
        h = layer_norm(DN_ALPHA * h + sq_relu_mlp(h, mlp_w1[i], mlp_w2[i]),
                       ln_g[2 * i + 1], ln_b[2 * i + 1])
    return h


import jax as _jax
import jax.numpy as _jnp

TWIN_FORMAT = 'train_step'
FWD_PARAMS = ['x', 'ln_g', 'ln_b', 'rw_mu', 'rw_w0', 'rw_w1', 'rw_w2', 'rw_a0', 'rw_a1', 'rw_a2', 'rw_g1', 'rw_g2', 'rw_k_k', 'rw_k_a', 'rw_r_k', 'rw_wr', 'rw_wk', 'rw_wv', 'rw_wo', 'rw_lnx_g', 'rw_lnx_b', 's5_a_re', 's5_a_im', 's5_log_dt', 's5_b_re', 's5_b_im', 's5_c_re', 's5_c_im', 's5_d', 's5_w_glu', 'mlp_w1', 'mlp_w2']
TWIN_WEIGHTS = ['ln_g', 'ln_b', 'rw_mu', 'rw_w0', 'rw_w1', 'rw_w2', 'rw_a0', 'rw_a1', 'rw_a2', 'rw_g1', 'rw_g2', 'rw_k_k', 'rw_k_a', 'rw_r_k', 'rw_wr', 'rw_wk', 'rw_wv', 'rw_wo', 'rw_lnx_g', 'rw_lnx_b', 's5_a_re', 's5_a_im', 's5_log_dt', 's5_b_re', 's5_b_im', 's5_c_re', 's5_c_im', 's5_d', 's5_w_glu', 'mlp_w1', 'mlp_w2']
TWIN_DIFF_INPUT = 'x'
TWIN_INPUTS = ['x', 'ln_g', 'ln_b', 'rw_mu', 'rw_w0', 'rw_w1', 'rw_w2', 'rw_a0', 'rw_a1', 'rw_a2', 'rw_g1', 'rw_g2', 'rw_k_k', 'rw_k_a', 'rw_r_k', 'rw_wr', 'rw_wk', 'rw_wv', 'rw_wo', 'rw_lnx_g', 'rw_lnx_b', 's5_a_re', 's5_a_im', 's5_log_dt', 's5_b_re', 's5_b_im', 's5_c_re', 's5_c_im', 's5_d', 's5_w_glu', 'mlp_w1', 'mlp_w2', 'loss_target', 'm_ln_g', 'm_ln_b', 'm_rw_mu', 'm_rw_w0', 'm_rw_w1', 'm_rw_w2', 'm_rw_a0', 'm_rw_a1', 'm_rw_a2', 'm_rw_g1', 'm_rw_g2', 'm_rw_k_k', 'm_rw_k_a', 'm_rw_r_k', 'm_rw_wr', 'm_rw_wk', 'm_rw_wv', 'm_rw_wo', 'm_rw_lnx_g', 'm_rw_lnx_b', 'm_s5_a_re', 'm_s5_a_im', 'm_s5_log_dt', 'm_s5_b_re', 'm_s5_b_im', 'm_s5_c_re', 'm_s5_c_im', 'm_s5_d', 'm_s5_w_glu', 'm_mlp_w1', 'm_mlp_w2', 'v_ln_g', 'v_ln_b', 'v_rw_mu', 'v_rw_w0', 'v_rw_w1', 'v_rw_w2', 'v_rw_a0', 'v_rw_a1', 'v_rw_a2', 'v_rw_g1', 'v_rw_g2', 'v_rw_k_k', 'v_rw_k_a', 'v_rw_r_k', 'v_rw_wr', 'v_rw_wk', 'v_rw_wv', 'v_rw_wo', 'v_rw_lnx_g', 'v_rw_lnx_b', 'v_s5_a_re', 'v_s5_a_im', 'v_s5_log_dt', 'v_s5_b_re', 'v_s5_b_im', 'v_s5_c_re', 'v_s5_c_im', 'v_s5_d', 'v_s5_w_glu', 'v_mlp_w1', 'v_mlp_w2']
TWIN_OUTPUTS = ['loss', 'grad_x', 'grad_ln_g', 'grad_ln_b', 'grad_rw_mu', 'grad_rw_w0', 'grad_rw_w1', 'grad_rw_w2', 'grad_rw_a0', 'grad_rw_a1', 'grad_rw_a2', 'grad_rw_g1', 'grad_rw_g2', 'grad_rw_k_k', 'grad_rw_k_a', 'grad_rw_r_k', 'grad_rw_wr', 'grad_rw_wk', 'grad_rw_wv', 'grad_rw_wo', 'grad_rw_lnx_g', 'grad_rw_lnx_b', 'grad_s5_a_re', 'grad_s5_a_im', 'grad_s5_log_dt', 'grad_s5_b_re', 'grad_s5_b_im', 'grad_s5_c_re', 'grad_s5_c_im', 'grad_s5_d', 'grad_s5_w_glu', 'grad_mlp_w1', 'grad_mlp_w2', 'delta_ln_g', 'delta_ln_b', 'delta_rw_mu', 'delta_rw_w0', 'delta_rw_w1', 'delta_rw_w2', 'delta_rw_a0', 'delta_rw_a1', 'delta_rw_a2', 'delta_rw_g1', 'delta_rw_g2', 'delta_rw_k_k', 'delta_rw_k_a', 'delta_rw_r_k', 'delta_rw_wr', 'delta_rw_wk', 'delta_rw_wv', 'delta_rw_wo', 'delta_rw_lnx_g', 'delta_rw_lnx_b', 'delta_s5_a_re', 'delta_s5_a_im', 'delta_s5_log_dt', 'delta_s5_b_re', 'delta_s5_b_im', 'delta_s5_c_re', 'delta_s5_c_im', 'delta_s5_d', 'delta_s5_w_glu', 'delta_mlp_w1', 'delta_mlp_w2', 'new_m_ln_g', 'new_m_ln_b', 'new_m_rw_mu', 'new_m_rw_w0', 'new_m_rw_w1', 'new_m_rw_w2', 'new_m_rw_a0', 'new_m_rw_a1', 'new_m_rw_a2', 'new_m_rw_g1', 'new_m_rw_g2', 'new_m_rw_k_k', 'new_m_rw_k_a', 'new_m_rw_r_k', 'new_m_rw_wr', 'new_m_rw_wk', 'new_m_rw_wv', 'new_m_rw_wo', 'new_m_rw_lnx_g', 'new_m_rw_lnx_b', 'new_m_s5_a_re', 'new_m_s5_a_im', 'new_m_s5_log_dt', 'new_m_s5_b_re', 'new_m_s5_b_im', 'new_m_s5_c_re', 'new_m_s5_c_im', 'new_m_s5_d', 'new_m_s5_w_glu', 'new_m_mlp_w1', 'new_m_mlp_w2', 'new_v_ln_g', 'new_v_ln_b', 'new_v_rw_mu', 'new_v_rw_w0', 'new_v_rw_w1', 'new_v_rw_w2', 'new_v_rw_a0', 'new_v_rw_a1', 'new_v_rw_a2', 'new_v_rw_g1', 'new_v_rw_g2', 'new_v_rw_k_k', 'new_v_rw_k_a', 'new_v_rw_r_k', 'new_v_rw_wr', 'new_v_rw_wk', 'new_v_rw_wv', 'new_v_rw_wo', 'new_v_rw_lnx_g', 'new_v_rw_lnx_b', 'new_v_s5_a_re', 'new_v_s5_a_im', 'new_v_s5_log_dt', 'new_v_s5_b_re', 'new_v_s5_b_im', 'new_v_s5_c_re', 'new_v_s5_c_im', 'new_v_s5_d', 'new_v_s5_w_glu', 'new_v_mlp_w1', 'new_v_mlp_w2']
TWIN_LEAF_KINDS = {'loss': 'loss', 'grad_x': 'grad_x', 'grad_ln_g': 'grad_w', 'grad_ln_b': 'grad_w', 'grad_rw_mu': 'grad_w', 'grad_rw_w0': 'grad_w', 'grad_rw_w1': 'grad_w', 'grad_rw_w2': 'grad_w', 'grad_rw_a0': 'grad_w', 'grad_rw_a1': 'grad_w', 'grad_rw_a2': 'grad_w', 'grad_rw_g1': 'grad_w', 'grad_rw_g2': 'grad_w', 'grad_rw_k_k': 'grad_w', 'grad_rw_k_a': 'grad_w', 'grad_rw_r_k': 'grad_w', 'grad_rw_wr': 'grad_w', 'grad_rw_wk': 'grad_w', 'grad_rw_wv': 'grad_w', 'grad_rw_wo': 'grad_w', 'grad_rw_lnx_g': 'grad_w', 'grad_rw_lnx_b': 'grad_w', 'grad_s5_a_re': 'grad_w', 'grad_s5_a_im': 'grad_w', 'grad_s5_log_dt': 'grad_w', 'grad_s5_b_re': 'grad_w', 'grad_s5_b_im': 'grad_w', 'grad_s5_c_re': 'grad_w', 'grad_s5_c_im': 'grad_w', 'grad_s5_d': 'grad_w', 'grad_s5_w_glu': 'grad_w', 'grad_mlp_w1': 'grad_w', 'grad_mlp_w2': 'grad_w', 'delta_ln_g': 'delta_w', 'delta_ln_b': 'delta_w', 'delta_rw_mu': 'delta_w', 'delta_rw_w0': 'delta_w', 'delta_rw_w1': 'delta_w', 'delta_rw_w2': 'delta_w', 'delta_rw_a0': 'delta_w', 'delta_rw_a1': 'delta_w', 'delta_rw_a2': 'delta_w', 'delta_rw_g1': 'delta_w', 'delta_rw_g2': 'delta_w', 'delta_rw_k_k': 'delta_w', 'delta_rw_k_a': 'delta_w', 'delta_rw_r_k': 'delta_w', 'delta_rw_wr': 'delta_w', 'delta_rw_wk': 'delta_w', 'delta_rw_wv': 'delta_w', 'delta_rw_wo': 'delta_w', 'delta_rw_lnx_g': 'delta_w', 'delta_rw_lnx_b': 'delta_w', 'delta_s5_a_re': 'delta_w', 'delta_s5_a_im': 'delta_w', 'delta_s5_log_dt': 'delta_w', 'delta_s5_b_re': 'delta_w', 'delta_s5_b_im': 'delta_w', 'delta_s5_c_re': 'delta_w', 'delta_s5_c_im': 'delta_w', 'delta_s5_d': 'delta_w', 'delta_s5_w_glu': 'delta_w', 'delta_mlp_w1': 'delta_w', 'delta_mlp_w2': 'delta_w', 'new_m_ln_g': 'new_m', 'new_m_ln_b': 'new_m', 'new_m_rw_mu': 'new_m', 'new_m_rw_w0': 'new_m', 'new_m_rw_w1': 'new_m', 'new_m_rw_w2': 'new_m', 'new_m_rw_a0': 'new_m', 'new_m_rw_a1': 'new_m', 'new_m_rw_a2': 'new_m', 'new_m_rw_g1': 'new_m', 'new_m_rw_g2': 'new_m', 'new_m_rw_k_k': 'new_m', 'new_m_rw_k_a': 'new_m', 'new_m_rw_r_k': 'new_m', 'new_m_rw_wr': 'new_m', 'new_m_rw_wk': 'new_m', 'new_m_rw_wv': 'new_m', 'new_m_rw_wo': 'new_m', 'new_m_rw_lnx_g': 'new_m', 'new_m_rw_lnx_b': 'new_m', 'new_m_s5_a_re': 'new_m', 'new_m_s5_a_im': 'new_m', 'new_m_s5_log_dt': 'new_m', 'new_m_s5_b_re': 'new_m', 'new_m_s5_b_im': 'new_m', 'new_m_s5_c_re': 'new_m', 'new_m_s5_c_im': 'new_m', 'new_m_s5_d': 'new_m', 'new_m_s5_w_glu': 'new_m', 'new_m_mlp_w1': 'new_m', 'new_m_mlp_w2': 'new_m', 'new_v_ln_g': 'new_v', 'new_v_ln_b': 'new_v', 'new_v_rw_mu': 'new_v', 'new_v_rw_w0': 'new_v', 'new_v_rw_w1': 'new_v', 'new_v_rw_w2': 'new_v', 'new_v_rw_a0': 'new_v', 'new_v_rw_a1': 'new_v', 'new_v_rw_a2': 'new_v', 'new_v_rw_g1': 'new_v', 'new_v_rw_g2': 'new_v', 'new_v_rw_k_k': 'new_v', 'new_v_rw_k_a': 'new_v', 'new_v_rw_r_k': 'new_v', 'new_v_rw_wr': 'new_v', 'new_v_rw_wk': 'new_v', 'new_v_rw_wv': 'new_v', 'new_v_rw_wo': 'new_v', 'new_v_rw_lnx_g': 'new_v', 'new_v_rw_lnx_b': 'new_v', 'new_v_s5_a_re': 'new_v', 'new_v_s5_a_im': 'new_v', 'new_v_s5_log_dt': 'new_v', 'new_v_s5_b_re': 'new_v', 'new_v_s5_b_im': 'new_v', 'new_v_s5_c_re': 'new_v', 'new_v_s5_c_im': 'new_v', 'new_v_s5_d': 'new_v', 'new_v_s5_w_glu': 'new_v', 'new_v_mlp_w1': 'new_v', 'new_v_mlp_w2': 'new_v'}


def _forward(args):
    return _fwd_reference(*[args[k] for k in FWD_PARAMS])


def _output_shape():
    def fwd():
        inp = _fwd_setup_inputs(0)
        return _fwd_reference(*[inp[k] for k in FWD_PARAMS])
    out = _jax.eval_shape(fwd)
    return out.shape, out.dtype

N_MICROBATCH = 1
ADAM_LR = 0.001
ADAM_B1 = 0.9
ADAM_B2 = 0.999
ADAM_EPS = 1e-08
ADAM_WD = 0.01
ADAM_STEP = 10
PER_EXAMPLE_BATCH_AXIS = {'x': 0, 'loss_target': 0}
SHARED_INPUTS = []
_WEIGHT_DTYPES = {'ln_g': _jnp.float32, 'ln_b': _jnp.float32, 'rw_mu': _jnp.float32, 'rw_w0': _jnp.float32, 'rw_w1': _jnp.float32, 'rw_w2': _jnp.float32, 'rw_a0': _jnp.float32, 'rw_a1': _jnp.float32, 'rw_a2': _jnp.float32, 'rw_g1': _jnp.float32, 'rw_g2': _jnp.float32, 'rw_k_k': _jnp.float32, 'rw_k_a': _jnp.float32, 'rw_r_k': _jnp.float32, 'rw_wr': _jnp.float32, 'rw_wk': _jnp.float32, 'rw_wv': _jnp.float32, 'rw_wo': _jnp.float32, 'rw_lnx_g': _jnp.float32, 'rw_lnx_b': _jnp.float32, 's5_a_re': _jnp.float32, 's5_a_im': _jnp.float32, 's5_log_dt': _jnp.float32, 's5_b_re': _jnp.float32, 's5_b_im': _jnp.float32, 's5_c_re': _jnp.float32, 's5_c_im': _jnp.float32, 's5_d': _jnp.float32, 's5_w_glu': _jnp.float32, 'mlp_w1': _jnp.float32, 'mlp_w2': _jnp.float32}
MOMENT_SCALE = {'ln_g': 1.608856e+01, 'ln_b': 3.865760e+00, 'rw_mu': 4.145040e-02, 'rw_w0': 1.156324e-02, 'rw_w1': 6.260522e-04, 'rw_w2': 1.524222e-03, 'rw_a0': 1.512021e-02, 'rw_a1': 5.637443e-03, 'rw_a2': 1.421171e-02, 'rw_g1': 2.946198e-02, 'rw_g2': 3.454726e-02, 'rw_k_k': 4.139226e-02, 'rw_k_a': 3.718239e-02, 'rw_r_k': 8.028356e-02, 'rw_wr': 3.521449e-02, 'rw_wk': 3.796386e-02, 'rw_wv': 3.395549e-02, 'rw_wo': 6.721780e-02, 'rw_lnx_g': 3.399640e-02, 'rw_lnx_b': 1.080031e-01, 's5_a_re': 2.282119e-03, 's5_a_im': 1.357330e-03, 's5_log_dt': 1.135149e+00, 's5_b_re': 9.817728e-04, 's5_b_im': 9.629690e-04, 's5_c_re': 2.196442e-03, 's5_c_im': 1.915261e-03, 's5_d': 4.038435e-02, 's5_w_glu': 5.692793e-02, 'mlp_w1': 4.219523e-02, 'mlp_w2': 2.258109e-01}


def _to_microbatches(a, axis):
    t = _jnp.moveaxis(a, axis, 0)
    t = t.reshape((N_MICROBATCH, t.shape[0] // N_MICROBATCH) + t.shape[1:])
    return _jnp.moveaxis(t, 1, axis + 1)


def setup_inputs(seed: int = 0) -> dict:
    inp = _fwd_setup_inputs(seed)
    key = _jax.random.fold_in(_jax.random.key(seed), 7919)
    shape, _ = _output_shape()
    out = dict(inp)
    out["loss_target"] = _jax.random.normal(_jax.random.fold_in(key, 0), shape, _jnp.float32)
    for i, name in enumerate(TWIN_WEIGHTS):
        w = inp[name].astype(_jnp.float32)
        if MOMENT_SCALE is None:
            s = _jnp.sqrt(_jnp.mean(_jnp.square(w)) + 1e-30)
        else:
            s = MOMENT_SCALE[name]
        km, kv = _jax.random.split(_jax.random.fold_in(key, i + 1))
        out[name] = w
        out["m_" + name] = s * _jax.random.normal(km, w.shape, _jnp.float32)
        out["v_" + name] = (s * s) * _jax.random.uniform(kv, w.shape, _jnp.float32, 0.5, 1.5)
    if N_MICROBATCH > 1:
        for name, axis in PER_EXAMPLE_BATCH_AXIS.items():
            out[name] = _to_microbatches(out[name], axis)
    return {'x': out['x'], 'ln_g': out['ln_g'], 'ln_b': out['ln_b'], 'rw_mu': out['rw_mu'], 'rw_w0': out['rw_w0'], 'rw_w1': out['rw_w1'], 'rw_w2': out['rw_w2'], 'rw_a0': out['rw_a0'], 'rw_a1': out['rw_a1'], 'rw_a2': out['rw_a2'], 'rw_g1': out['rw_g1'], 'rw_g2': out['rw_g2'], 'rw_k_k': out['rw_k_k'], 'rw_k_a': out['rw_k_a'], 'rw_r_k': out['rw_r_k'], 'rw_wr': out['rw_wr'], 'rw_wk': out['rw_wk'], 'rw_wv': out['rw_wv'], 'rw_wo': out['rw_wo'], 'rw_lnx_g': out['rw_lnx_g'], 'rw_lnx_b': out['rw_lnx_b'], 's5_a_re': out['s5_a_re'], 's5_a_im': out['s5_a_im'], 's5_log_dt': out['s5_log_dt'], 's5_b_re': out['s5_b_re'], 's5_b_im': out['s5_b_im'], 's5_c_re': out['s5_c_re'], 's5_c_im': out['s5_c_im'], 's5_d': out['s5_d'], 's5_w_glu': out['s5_w_glu'], 'mlp_w1': out['mlp_w1'], 'mlp_w2': out['mlp_w2'], 'loss_target': out['loss_target'], 'm_ln_g': out['m_ln_g'], 'm_ln_b': out['m_ln_b'], 'm_rw_mu': out['m_rw_mu'], 'm_rw_w0': out['m_rw_w0'], 'm_rw_w1': out['m_rw_w1'], 'm_rw_w2': out['m_rw_w2'], 'm_rw_a0': out['m_rw_a0'], 'm_rw_a1': out['m_rw_a1'], 'm_rw_a2': out['m_rw_a2'], 'm_rw_g1': out['m_rw_g1'], 'm_rw_g2': out['m_rw_g2'], 'm_rw_k_k': out['m_rw_k_k'], 'm_rw_k_a': out['m_rw_k_a'], 'm_rw_r_k': out['m_rw_r_k'], 'm_rw_wr': out['m_rw_wr'], 'm_rw_wk': out['m_rw_wk'], 'm_rw_wv': out['m_rw_wv'], 'm_rw_wo': out['m_rw_wo'], 'm_rw_lnx_g': out['m_rw_lnx_g'], 'm_rw_lnx_b': out['m_rw_lnx_b'], 'm_s5_a_re': out['m_s5_a_re'], 'm_s5_a_im': out['m_s5_a_im'], 'm_s5_log_dt': out['m_s5_log_dt'], 'm_s5_b_re': out['m_s5_b_re'], 'm_s5_b_im': out['m_s5_b_im'], 'm_s5_c_re': out['m_s5_c_re'], 'm_s5_c_im': out['m_s5_c_im'], 'm_s5_d': out['m_s5_d'], 'm_s5_w_glu': out['m_s5_w_glu'], 'm_mlp_w1': out['m_mlp_w1'], 'm_mlp_w2': out['m_mlp_w2'], 'v_ln_g': out['v_ln_g'], 'v_ln_b': out['v_ln_b'], 'v_rw_mu': out['v_rw_mu'], 'v_rw_w0': out['v_rw_w0'], 'v_rw_w1': out['v_rw_w1'], 'v_rw_w2': out['v_rw_w2'], 'v_rw_a0': out['v_rw_a0'], 'v_rw_a1': out['v_rw_a1'], 'v_rw_a2': out['v_rw_a2'], 'v_rw_g1': out['v_rw_g1'], 'v_rw_g2': out['v_rw_g2'], 'v_rw_k_k': out['v_rw_k_k'], 'v_rw_k_a': out['v_rw_k_a'], 'v_rw_r_k': out['v_rw_r_k'], 'v_rw_wr': out['v_rw_wr'], 'v_rw_wk': out['v_rw_wk'], 'v_rw_wv': out['v_rw_wv'], 'v_rw_wo': out['v_rw_wo'], 'v_rw_lnx_g': out['v_rw_lnx_g'], 'v_rw_lnx_b': out['v_rw_lnx_b'], 'v_s5_a_re': out['v_s5_a_re'], 'v_s5_a_im': out['v_s5_a_im'], 'v_s5_log_dt': out['v_s5_log_dt'], 'v_s5_b_re': out['v_s5_b_re'], 'v_s5_b_im': out['v_s5_b_im'], 'v_s5_c_re': out['v_s5_c_re'], 'v_s5_c_im': out['v_s5_c_im'], 'v_s5_d': out['v_s5_d'], 'v_s5_w_glu': out['v_s5_w_glu'], 'v_mlp_w1': out['v_mlp_w1'], 'v_mlp_w2': out['v_mlp_w2']}


def _loss(weights, diff, rest, loss_target):
    with _jax.named_scope("forward"):
        args = {**rest, TWIN_DIFF_INPUT: diff, **{k: w.astype(_WEIGHT_DTYPES[k]) for k, w in weights.items()}}
        y = _forward(args)
    with _jax.named_scope("loss_head"):
        err = _jnp.square(y.astype(_jnp.float32) - loss_target)
        return 0.5 * _jnp.sum(_jnp.mean(err, axis=-1)) if err.ndim else 0.5 * err


def _adamw(w, g, m, v):
    m = ADAM_B1 * m + (1.0 - ADAM_B1) * g
    v = ADAM_B2 * v + (1.0 - ADAM_B2) * _jnp.square(g)
    m_hat = m / (1.0 - ADAM_B1 ** ADAM_STEP)
    v_hat = v / (1.0 - ADAM_B2 ** ADAM_STEP)
    delta = -ADAM_LR * (m_hat / (_jnp.sqrt(v_hat) + ADAM_EPS) + ADAM_WD * w)
    return delta, m, v


def reference(x, ln_g, ln_b, rw_mu, rw_w0, rw_w1, rw_w2, rw_a0, rw_a1, rw_a2, rw_g1, rw_g2, rw_k_k, rw_k_a, rw_r_k, rw_wr, rw_wk, rw_wv, rw_wo, rw_lnx_g, rw_lnx_b, s5_a_re, s5_a_im, s5_log_dt, s5_b_re, s5_b_im, s5_c_re, s5_c_im, s5_d, s5_w_glu, mlp_w1, mlp_w2, loss_target, m_ln_g, m_ln_b, m_rw_mu, m_rw_w0, m_rw_w1, m_rw_w2, m_rw_a0, m_rw_a1, m_rw_a2, m_rw_g1, m_rw_g2, m_rw_k_k, m_rw_k_a, m_rw_r_k, m_rw_wr, m_rw_wk, m_rw_wv, m_rw_wo, m_rw_lnx_g, m_rw_lnx_b, m_s5_a_re, m_s5_a_im, m_s5_log_dt, m_s5_b_re, m_s5_b_im, m_s5_c_re, m_s5_c_im, m_s5_d, m_s5_w_glu, m_mlp_w1, m_mlp_w2, v_ln_g, v_ln_b, v_rw_mu, v_rw_w0, v_rw_w1, v_rw_w2, v_rw_a0, v_rw_a1, v_rw_a2, v_rw_g1, v_rw_g2, v_rw_k_k, v_rw_k_a, v_rw_r_k, v_rw_wr, v_rw_wk, v_rw_wv, v_rw_wo, v_rw_lnx_g, v_rw_lnx_b, v_s5_a_re, v_s5_a_im, v_s5_log_dt, v_s5_b_re, v_s5_b_im, v_s5_c_re, v_s5_c_im, v_s5_d, v_s5_w_glu, v_mlp_w1, v_mlp_w2):
    given = dict(x=x, ln_g=ln_g, ln_b=ln_b, rw_mu=rw_mu, rw_w0=rw_w0, rw_w1=rw_w1, rw_w2=rw_w2, rw_a0=rw_a0, rw_a1=rw_a1, rw_a2=rw_a2, rw_g1=rw_g1, rw_g2=rw_g2, rw_k_k=rw_k_k, rw_k_a=rw_k_a, rw_r_k=rw_r_k, rw_wr=rw_wr, rw_wk=rw_wk, rw_wv=rw_wv, rw_wo=rw_wo, rw_lnx_g=rw_lnx_g, rw_lnx_b=rw_lnx_b, s5_a_re=s5_a_re, s5_a_im=s5_a_im, s5_log_dt=s5_log_dt, s5_b_re=s5_b_re, s5_b_im=s5_b_im, s5_c_re=s5_c_re, s5_c_im=s5_c_im, s5_d=s5_d, s5_w_glu=s5_w_glu, mlp_w1=mlp_w1, mlp_w2=mlp_w2, loss_target=loss_target, m_ln_g=m_ln_g, m_ln_b=m_ln_b, m_rw_mu=m_rw_mu, m_rw_w0=m_rw_w0, m_rw_w1=m_rw_w1, m_rw_w2=m_rw_w2, m_rw_a0=m_rw_a0, m_rw_a1=m_rw_a1, m_rw_a2=m_rw_a2, m_rw_g1=m_rw_g1, m_rw_g2=m_rw_g2, m_rw_k_k=m_rw_k_k, m_rw_k_a=m_rw_k_a, m_rw_r_k=m_rw_r_k, m_rw_wr=m_rw_wr, m_rw_wk=m_rw_wk, m_rw_wv=m_rw_wv, m_rw_wo=m_rw_wo, m_rw_lnx_g=m_rw_lnx_g, m_rw_lnx_b=m_rw_lnx_b, m_s5_a_re=m_s5_a_re, m_s5_a_im=m_s5_a_im, m_s5_log_dt=m_s5_log_dt, m_s5_b_re=m_s5_b_re, m_s5_b_im=m_s5_b_im, m_s5_c_re=m_s5_c_re, m_s5_c_im=m_s5_c_im, m_s5_d=m_s5_d, m_s5_w_glu=m_s5_w_glu, m_mlp_w1=m_mlp_w1, m_mlp_w2=m_mlp_w2, v_ln_g=v_ln_g, v_ln_b=v_ln_b, v_rw_mu=v_rw_mu, v_rw_w0=v_rw_w0, v_rw_w1=v_rw_w1, v_rw_w2=v_rw_w2, v_rw_a0=v_rw_a0, v_rw_a1=v_rw_a1, v_rw_a2=v_rw_a2, v_rw_g1=v_rw_g1, v_rw_g2=v_rw_g2, v_rw_k_k=v_rw_k_k, v_rw_k_a=v_rw_k_a, v_rw_r_k=v_rw_r_k, v_rw_wr=v_rw_wr, v_rw_wk=v_rw_wk, v_rw_wv=v_rw_wv, v_rw_wo=v_rw_wo, v_rw_lnx_g=v_rw_lnx_g, v_rw_lnx_b=v_rw_lnx_b, v_s5_a_re=v_s5_a_re, v_s5_a_im=v_s5_a_im, v_s5_log_dt=v_s5_log_dt, v_s5_b_re=v_s5_b_re, v_s5_b_im=v_s5_b_im, v_s5_c_re=v_s5_c_re, v_s5_c_im=v_s5_c_im, v_s5_d=v_s5_d, v_s5_w_glu=v_s5_w_glu, v_mlp_w1=v_mlp_w1, v_mlp_w2=v_mlp_w2)
    weights = {n: given[n] for n in TWIN_WEIGHTS}
    shared = {n: given[n] for n in SHARED_INPUTS}
    per_example = {n: given[n] for n in ['x']}
    grad_fn = _jax.value_and_grad(_loss, argnums=(0, 1))

    def one_microbatch(ex, loss_target):
        ex = dict(ex)
        diff = ex.pop(TWIN_DIFF_INPUT)
        return grad_fn(weights, diff, {**shared, **ex}, loss_target)

    if N_MICROBATCH == 1:
        loss, (grad_w, grad_x) = one_microbatch(per_example, given["loss_target"])
    else:
        def body(carry, xs):
            loss_sum, grad_sum = carry
            l_k, (gw_k, gx_k) = one_microbatch(xs[0], xs[1])
            with _jax.named_scope("update"):
                return (loss_sum + l_k, _jax.tree.map(_jnp.add, grad_sum, gw_k)), gx_k

        init = (_jnp.zeros((), _jnp.float32), _jax.tree.map(_jnp.zeros_like, weights))
        (loss, grad_w), grad_x = _jax.lax.scan(body, init, (per_example, given["loss_target"]))
    with _jax.named_scope("update"):
        delta_w, new_m, new_v = {}, {}, {}
        for n in TWIN_WEIGHTS:
            delta_w[n], new_m[n], new_v[n] = _adamw(weights[n], grad_w[n], given["m_" + n], given["v_" + n])
    return (loss, grad_x, *[grad_w[n] for n in TWIN_WEIGHTS], *[delta_w[n] for n in TWIN_WEIGHTS],
            *[new_m[n] for n in TWIN_WEIGHTS], *[new_v[n] for n in TWIN_WEIGHTS])
```

```python
import functools
import math

import jax
import jax.numpy as jnp
from jax import lax
from jax.experimental import pallas as pl
from jax.experimental.pallas import tpu as pltpu

F32 = jnp.float32
BF16 = jnp.bfloat16
MESH = pl.DeviceIdType.MESH

HEAD = 64
SSM_GROUP = 16
SSM_STATE = 64
GN_EPS = 64e-5
LN_EPS = 1e-5
DEPTH = 2
DN_ALPHA = (2.0 * DEPTH) ** 0.25
ADAM_LR, ADAM_B1, ADAM_B2, ADAM_EPS, ADAM_WD, ADAM_STEP = 0.001, 0.9, 0.999, 1e-08, 0.01, 10
REC_CHUNK = 64
V7X_VMEM_BYTES = 64 * 2 ** 20
VMEM_LIMIT = V7X_VMEM_BYTES - 8 * 2 ** 20

WEIGHTS = ['ln_g', 'ln_b', 'rw_mu', 'rw_w0', 'rw_w1', 'rw_w2', 'rw_a0', 'rw_a1', 'rw_a2', 'rw_g1', 'rw_g2',
           'rw_k_k', 'rw_k_a', 'rw_r_k', 'rw_wr', 'rw_wk', 'rw_wv', 'rw_wo', 'rw_lnx_g', 'rw_lnx_b',
           's5_a_re', 's5_a_im', 's5_log_dt', 's5_b_re', 's5_b_im', 's5_c_re', 's5_c_im', 's5_d', 's5_w_glu',
           'mlp_w1', 'mlp_w2']
SHARD_AXIS = {'rw_mu': 2, 'rw_w1': 1, 'rw_w2': 2, 'rw_a1': 1, 'rw_a2': 2, 'rw_g1': 1, 'rw_g2': 2,
              'rw_wr': 1, 'rw_wk': 1, 'rw_wv': 1, 'rw_wo': 1, 's5_d': 1, 's5_w_glu': 2, 'mlp_w1': 2, 'mlp_w2': 1}
REPLICATED = [n for n in WEIGHTS if n not in SHARD_AXIS]
BIG_EARLY = ['rw_wr', 'rw_wk', 'rw_wv', 'rw_wo', 's5_w_glu']
BIG_LATE = ['mlp_w1', 'mlp_w2']
BIG = BIG_EARLY + BIG_LATE
BIG_READY = ['rw_wo', 's5_w_glu', 'mlp_w1', 'mlp_w2']


def _sds(shape, dtype=F32):
    return jax.ShapeDtypeStruct(tuple(shape), dtype)


def _cparams(sem=None, **kw):
    if sem is not None:
        kw["dimension_semantics"] = sem
    return pltpu.CompilerParams(vmem_limit_bytes=VMEM_LIMIT, **kw)


def _mm_products(a, b, g):
    gb = g.astype(BF16)
    da = lax.dot_general(gb, b.astype(BF16), (((1,), (1,)), ((), ())), preferred_element_type=F32)
    db = lax.dot_general(a.astype(BF16), gb, (((0,), (0,)), ((), ())), preferred_element_type=F32)
    return da, db


@jax.custom_vjp
def _mm_plain(a, b):
    return jnp.dot(a.astype(BF16), b.astype(BF16), preferred_element_type=F32)


def _mm_plain_bwd(res, g):
    da, db = _mm_products(*res, g)
    return da.astype(res[0].dtype), db.astype(res[1].dtype)


_mm_plain.defvjp(lambda a, b: (_mm_plain(a, b), (a, b)), _mm_plain_bwd)


@jax.custom_vjp
def _mm_proxy(a, b, z):
    return jnp.dot(a.astype(BF16), b.astype(BF16), preferred_element_type=F32)


def _mm_proxy_bwd(res, g):
    da, db = _mm_products(*res, g)
    return da.astype(res[0].dtype), jnp.zeros_like(res[1]), db


_mm_proxy.defvjp(lambda a, b, z: (_mm_proxy(a, b, z), (a, b)), _mm_proxy_bwd)


def mm(a, b, z=None):
    return _mm_plain(a, b) if z is None else _mm_proxy(a, b, z)


def _split3(x):
    hi = x.astype(BF16)
    r1 = x - hi.astype(F32)
    mid = r1.astype(BF16)
    lo = (r1 - mid.astype(F32)).astype(BF16)
    return hi, mid, lo


def _head_sum_impl(x):
    c = x.shape[1]
    lanes = 128
    sel = (lax.broadcasted_iota(jnp.int32, (c, lanes), 0) // HEAD
           == lax.broadcasted_iota(jnp.int32, (c, lanes), 1)).astype(BF16)
    s = sum(jnp.dot(p, sel, preferred_element_type=F32) for p in _split3(x))
    return sum(lax.dot_general(p, sel, (((1,), (1,)), ((), ())), preferred_element_type=F32) for p in _split3(s))


@jax.custom_vjp
def head_sum(x):
    return _head_sum_impl(x)


head_sum.defvjp(lambda x: (_head_sum_impl(x), None), lambda _, g: (_head_sum_impl(g),))


def _ln(x, g, b):
    mu = jnp.mean(x, axis=-1, keepdims=True)
    xc = x - mu
    var = jnp.mean(xc * xc, axis=-1, keepdims=True)
    return xc * lax.rsqrt(var + LN_EPS) * g + b


def _f_proj(acts, params, proxies):
    x, xp = acts
    mu, w = params
    return (mm(x + (xp - x) * mu, w, proxies[1]),)


def _f_lora(acts, params, proxies):
    x, xp, kraw = acts
    mu_w, mu_a, mu_g, w0, w1, w2, a0, a1, a2, g1, g2, k_k, k_a = params
    xx = xp - x
    w_pre = w0 + mm(jnp.tanh(mm(x + xx * mu_w, w1)), w2)
    z = -w_pre
    softplus = jnp.maximum(z, 0.0) + jnp.log(1.0 + jnp.exp(-jnp.abs(z)))
    log_decay = -jnp.exp(-softplus - 0.5)
    a = jax.nn.sigmoid(a0 + mm(mm(x + xx * mu_a, a1), a2))
    g = mm(jax.nn.sigmoid(mm(x + xx * mu_g, g1)), g2)
    kk = kraw * k_k
    kkn = kk / jnp.maximum(jnp.sqrt(head_sum(kk * kk)), 1e-12)
    k2 = kraw * (1.0 + (a - 1.0) * k_a)
    return log_decay, k2, -kkn, kkn * a, g


def _f_post(acts, params, proxies):
    o, r, k2, v, g, x = acts
    lnx_g, lnx_b, r_k, wo, ln_g, ln_b = params
    om = head_sum(o) * (1.0 / HEAD)
    oc = o - om
    ov = head_sum(oc * oc) * (1.0 / HEAD)
    on = oc * lax.rsqrt(ov + GN_EPS) * lnx_g + lnx_b
    bonus = head_sum(r * k2 * r_k) * v
    y = mm((on + bonus) * g, wo, proxies[3])
    return (_ln(DN_ALPHA * x + y, ln_g, ln_b),)


def _f_glu(acts, params, proxies):
    ys, h = acts
    d, wv0, wv1, wg0, wg1, ln_g, ln_b = params
    y = jax.nn.gelu(ys + h * d)
    mix = jnp.concatenate([mm(y, wv0, proxies[1]) * jax.nn.sigmoid(mm(y, wg0, proxies[3])),
                           mm(y, wv1, proxies[2]) * jax.nn.sigmoid(mm(y, wg1, proxies[4]))], axis=1)
    return (_ln(DN_ALPHA * h + mix, ln_g, ln_b),)


def _f_zoh(a_re, a_im, log_dt, b_re_t, b_im_t):
    dt = jnp.exp(log_dt)
    lam_re = jnp.minimum(a_re, -1e-4)
    lam_im = a_im
    mag = jnp.exp(dt * lam_re)
    abar_re = mag * jnp.cos(dt * lam_im)
    abar_im = mag * jnp.sin(dt * lam_im)
    den = lam_re * lam_re + lam_im * lam_im
    nr, ni = abar_re - 1.0, abar_im
    coef_re = ((nr * lam_re + ni * lam_im) / den)[:, None, :]
    coef_im = ((ni * lam_re - nr * lam_im) / den)[:, None, :]
    return (abar_re, abar_im, coef_re * b_re_t - coef_im * b_im_t, coef_re * b_im_t + coef_im * b_re_t)


def _bdot16_raw(a, b, ca, cb):
    return lax.dot_general(a.astype(BF16), b.astype(BF16), (((ca,), (cb,)), ((0,), (0,))), preferred_element_type=F32)


@functools.partial(jax.custom_vjp, nondiff_argnums=(2, 3))
def _bdot16(a, b, ca, cb):
    return _bdot16_raw(a, b, ca, cb)


def _bdot16_bwd(ca, cb, res, g):
    a, b = res
    if (ca, cb) == (2, 1):
        return _bdot16_raw(g, b, 2, 2), _bdot16_raw(a, g, 1, 1)
    if (ca, cb) == (2, 2):
        return _bdot16_raw(g, b, 2, 1), _bdot16_raw(g, a, 1, 1)
    assert (ca, cb) == (1, 1)
    return _bdot16_raw(b, g, 2, 2), _bdot16_raw(a, g, 2, 1)


_bdot16.defvjp(lambda a, b, ca, cb: (_bdot16_raw(a, b, ca, cb), (a, b)), _bdot16_bwd)

def _time_sums(x, suffix):
    hg, ln, _ = x.shape
    row = lax.broadcasted_iota(jnp.int32, (hg, ln, ln), 1)
    col = lax.broadcasted_iota(jnp.int32, (hg, ln, ln), 2)
    tri = ((row <= col) if suffix else (row >= col)).astype(BF16)
    return sum(lax.dot_general(tri, p, (((2,), (1,)), ((0,), (0,))), preferred_element_type=F32) for p in _split3(x))


@jax.custom_vjp
def _time_cumsum(x):
    return _time_sums(x, False)


_time_cumsum.defvjp(lambda x: (_time_sums(x, False), None), lambda _, g: (_time_sums(g, True),))

_dot_score = _bdot16
_dot_inverse = _bdot16
_dot_value = _bdot16


def _rec_chunk(s0, r, lw, k, v, a, b):
    hg, ln, _ = r.shape
    row = lax.broadcasted_iota(jnp.int32, (hg, ln, ln), 1)
    col = lax.broadcasted_iota(jnp.int32, (hg, ln, ln), 2)
    incl, strict = row >= col, row > col
    cum = _time_cumsum(lw)
    total = jnp.sum(lw, axis=1, keepdims=True)
    e_cum, e_inv, e_prev, e_tail = jnp.exp(cum), jnp.exp(-cum), jnp.exp(cum - lw), jnp.exp(total - cum)
    rt, at, bt, kt = r * e_cum, a * e_prev, b * e_inv, k * e_inv
    aab = jnp.where(strict, _dot_score(at, bt, 2, 2), 0.0)
    aak = jnp.where(strict, _dot_score(at, kt, 2, 2), 0.0)
    arb = jnp.where(incl, _dot_score(rt, bt, 2, 2), 0.0)
    ark = jnp.where(incl, _dot_score(rt, kt, 2, 2), 0.0)
    p = (row == col).astype(F32) + aab
    m = aab
    for _ in range(int(math.log2(ln)) - 1):
        m = _dot_inverse(m, m, 2, 1)
        p = p + _dot_inverse(p, m, 2, 1)
    u = _dot_inverse(p, _dot_value(at, s0, 2, 2) + _dot_value(aak, v, 2, 1), 2, 1)
    o = _dot_value(rt, s0, 2, 2) + _dot_value(arb, u, 2, 1) + _dot_value(ark, v, 2, 1)
    s1 = s0 * jnp.exp(total) + _dot_value(u, b * e_tail, 1, 1) + _dot_value(v, k * e_tail, 1, 1)
    return o, s1


def _full_spec(shape):
    nd = len(shape)
    return pl.BlockSpec(tuple(shape), lambda *_: (0,) * nd)


def _stage_fwd(name, f, acts, params, out_dims, tb):
    t = acts[0].shape[0]
    na, npar = len(acts), len(params)

    def body(*refs):
        outs = f(tuple(r[...] for r in refs[:na]), tuple(r[...] for r in refs[na:na + npar]), (None,) * npar)
        for r, val in zip(refs[na + npar:], outs):
            r[...] = val

    return pl.pallas_call(
        body, name=name, grid=(t // tb,),
        in_specs=[pl.BlockSpec((tb, a.shape[1]), lambda i: (i, 0)) for a in acts] + [_full_spec(p.shape) for p in params],
        out_specs=[pl.BlockSpec((tb, d), lambda i: (i, 0)) for d in out_dims],
        out_shape=[_sds((t, d)) for d in out_dims],
        compiler_params=_cparams(("arbitrary",)),
    )(*acts, *params)


def _stage_bwd(name, f, acts, params, couts, tb, proxied=()):
    t = acts[0].shape[0]
    groups = [c if isinstance(c, tuple) else (c,) for c in couts]
    couts = [term for grp in groups for term in grp]
    na, npar, nc = len(acts), len(params), len(couts)
    steps = t // tb

    def f_diff(act_vals, diff_vals, param_vals):
        real = tuple(param_vals[i] if i in proxied else diff_vals[i] for i in range(npar))
        proxies = tuple(diff_vals[i] if i in proxied else None for i in range(npar))
        return f(act_vals, real, proxies)

    def body(*refs):
        a_refs, p_hbm, c_refs = refs[:na], refs[na:na + npar], refs[na + npar:na + npar + nc]
        o = na + npar + nc
        da_refs, dp_hbm = refs[o:o + na], refs[o + na:o + na + npar]
        p_buf, acc = refs[o + na + npar:o + na + 2 * npar], refs[o + na + 2 * npar:]
        i = pl.program_id(0)

        @pl.when(i == 0)
        def _():
            for src, dst in zip(p_hbm, p_buf):
                pltpu.sync_copy(src, dst)
            for r in acc:
                r[...] = jnp.zeros_like(r)

        param_vals = tuple(r[...] for r in p_buf)
        diff_vals = tuple(jnp.zeros(v.shape, F32) if i in proxied else v for i, v in enumerate(param_vals))
        _, vjp = jax.vjp(functools.partial(f_diff, param_vals=param_vals), tuple(r[...] for r in a_refs), diff_vals)
        terms = iter(c_refs)
        d_acts, d_params = vjp(tuple(functools.reduce(jnp.add, [next(terms)[...] for _ in grp]) for grp in groups))
        for r, val in zip(da_refs, d_acts):
            r[...] = val
        for r, val in zip(acc, d_params):
            r[...] += val

        @pl.when(i == steps - 1)
        def _():
            for src, dst in zip(acc, dp_hbm):
                pltpu.sync_copy(src, dst)

    hbm = pl.BlockSpec(memory_space=pltpu.HBM)
    outs = pl.pallas_call(
        body, name=name, grid=(steps,),
        in_specs=[pl.BlockSpec((tb, a.shape[1]), lambda i: (i, 0)) for a in acts] + [hbm] * npar
        + [pl.BlockSpec((tb, c.shape[1]), lambda i: (i, 0)) for c in couts],
        out_specs=[pl.BlockSpec((tb, a.shape[1]), lambda i: (i, 0)) for a in acts] + [hbm] * npar,
        out_shape=[_sds(a.shape) for a in acts] + [_sds(p.shape) for p in params],
        scratch_shapes=[pltpu.VMEM(p.shape, p.dtype) for p in params] + [pltpu.VMEM(p.shape, F32) for p in params],
        compiler_params=_cparams(("arbitrary",)),
    )(*acts, *params, *couts)
    return outs[:na], outs[na:]


def _tiled_matmul(name, a, b, mode, grid, a_spec, b_spec, o_spec, out_shape):
    nk = grid[2]
    dims = {"nn": ((1,), (0,)), "nt": ((1,), (1,)), "tn": ((0,), (0,))}[mode]

    def body(a_ref, b_ref, o_ref, acc):
        kk = pl.program_id(2)

        @pl.when(kk == 0)
        def _():
            acc[...] = jnp.zeros_like(acc)

        acc[...] += lax.dot_general(a_ref[...].astype(BF16), b_ref[...].astype(BF16), (dims, ((), ())),
                                    preferred_element_type=F32)

        @pl.when(kk == nk - 1)
        def _():
            o_ref[...] = acc[...]

    return pl.pallas_call(
        body, name=name, grid=grid, in_specs=[a_spec, b_spec], out_specs=o_spec, out_shape=_sds(out_shape),
        scratch_shapes=[pltpu.VMEM(o_spec.block_shape, F32)],
        compiler_params=_cparams(("parallel", "parallel", "arbitrary")),
    )(a, b)


def _mlp_weight_grad(name, a, b, layer, layers, split, into=None, tile=512):
    t, m = a.shape
    n = b.shape[1]
    tk = min(tile, t)
    tile = 2 * tile
    if split == "n":
        tm, tn = min(tile, m), min(tile, n // 4)
        per = n // 4 // tn
        shape = (4, layers, m, n // 4)
        o_idx = lambda i, j, k: (j // per, layer, i, j % per)
    else:
        tm, tn = min(tile, m // 4), min(tile, n)
        per = m // 4 // tm
        shape = (4, layers, m // 4, n)
        o_idx = lambda i, j, k: (i // per, layer, i % per, j)
    nk = t // tk

    def body(a_ref, b_ref, *rest):
        o_ref, acc = rest[-2:]
        kk = pl.program_id(2)

        @pl.when(kk == 0)
        def _():
            acc[...] = jnp.zeros_like(acc)

        acc[...] += lax.dot_general(a_ref[...].astype(BF16), b_ref[...].astype(BF16), (((0,), (0,)), ((), ())),
                                    preferred_element_type=F32)

        @pl.when(kk == nk - 1)
        def _():
            o_ref[...] = acc[...]

    in_specs = [pl.BlockSpec((tk, tm), lambda i, j, k: (k, i)), pl.BlockSpec((tk, tn), lambda i, j, k: (k, j))]
    operands = [a, b]
    aliases = {}
    if into is not None:
        in_specs.append(pl.BlockSpec(memory_space=pl.ANY))
        operands.append(into)
        aliases = {2: 0}
    return pl.pallas_call(
        body, name=name, grid=(m // tm, n // tn, nk), in_specs=in_specs,
        out_specs=pl.BlockSpec((None, None, tm, tn), o_idx), out_shape=_sds(shape), input_output_aliases=aliases,
        scratch_shapes=[pltpu.VMEM((tm, tn), F32)],
        compiler_params=_cparams(("parallel", "parallel", "arbitrary")),
    )(*operands)


S5_PACK = 8


def _s5_wide_from_narrow(name, x, wc, mode, tm):
    t, c = x.shape
    kb, nb = S5_PACK * SSM_GROUP, S5_PACK * SSM_STATE
    nsb = c // kb
    wide = 2 * nsb * nb
    b_spec = (pl.BlockSpec((kb, nb), lambda i, j, k: (0, j)) if mode == "nn" else pl.BlockSpec((nb, kb), lambda i, j, k: (j, 0)))
    return _tiled_matmul(name, x, wc, mode, (t // tm, wide // nb, 1), pl.BlockSpec((tm, kb), lambda i, j, k: (i, j % nsb)),
                         b_spec, pl.BlockSpec((tm, nb), lambda i, j, k: (i, j)), (t, wide))


def _s5_narrow_from_wide(name, s, wc, mode, tm):
    t, wide = s.shape
    kb, nb = S5_PACK * SSM_GROUP, S5_PACK * SSM_STATE
    nsb = wide // (2 * nb)
    b_spec = (pl.BlockSpec((nb, kb), lambda i, j, k: (k * nsb + j, 0)) if mode == "nn"
              else pl.BlockSpec((kb, nb), lambda i, j, k: (0, k * nsb + j)))
    return _tiled_matmul(name, s, wc, mode, (t // tm, nsb, 2), pl.BlockSpec((tm, nb), lambda i, j, k: (i, k * nsb + j)),
                         b_spec, pl.BlockSpec((tm, kb), lambda i, j, k: (i, j)), (t, nsb * kb))


def _s5_weight_grad(name, x, s, wide_rows, tk):
    t, c = x.shape
    wide = s.shape[1]
    kb, nb = S5_PACK * SSM_GROUP, S5_PACK * SSM_STATE
    nsb = c // kb
    x_spec = pl.BlockSpec((tk, kb), lambda i, j, k: (k, j % nsb))
    s_spec = pl.BlockSpec((tk, nb), lambda i, j, k: (k, j))
    if wide_rows:
        return _tiled_matmul(name, s, x, "tn", (1, wide // nb, t // tk), s_spec, x_spec,
                             pl.BlockSpec((nb, kb), lambda i, j, k: (j, 0)), (wide, kb))
    return _tiled_matmul(name, x, s, "tn", (1, wide // nb, t // tk), x_spec, s_spec,
                         pl.BlockSpec((kb, nb), lambda i, j, k: (0, j)), (kb, wide))


def _mlp_fwd(name, h, w1, w2, layer, ln_g, ln_b, tb):
    t, c = h.shape
    nj, fc = w1.shape[0], w1.shape[3]

    def body(h_ref, w1_ref, w2_ref, g_ref, b_ref, out_ref, s_ref, acc):
        j = pl.program_id(1)

        @pl.when(j == 0)
        def _():
            acc[...] = jnp.zeros_like(acc)

        hid = jnp.dot(h_ref[...].astype(BF16), w1_ref[...].astype(BF16), preferred_element_type=F32)
        act = jnp.square(jnp.maximum(hid, 0.0))
        acc[...] += jnp.dot(act.astype(BF16), w2_ref[...].astype(BF16), preferred_element_type=F32)

        @pl.when(j == nj - 1)
        def _():
            s = DN_ALPHA * h_ref[...] + acc[...]
            s_ref[...] = s
            out_ref[...] = _ln(s, g_ref[...], b_ref[...])

    row = pl.BlockSpec((tb, c), lambda i, j: (i, 0))
    vec = pl.BlockSpec((1, c), lambda i, j: (0, 0))
    return pl.pallas_call(
        body, name=name, grid=(t // tb, nj),
        in_specs=[row, pl.BlockSpec((None, None, c, fc), lambda i, j: (j, layer, 0, 0)),
                  pl.BlockSpec((None, None, fc, c), lambda i, j: (j, layer, 0, 0)), vec, vec],
        out_specs=[row, row], out_shape=[_sds((t, c)), _sds((t, c))],
        scratch_shapes=[pltpu.VMEM((tb, c), F32)],
        compiler_params=_cparams(("parallel", "arbitrary")),
    )(h, w1, w2, ln_g, ln_b)


def _mlp_bwd(name, h, s, dout, w1, w2, layer, ln_g, ln_b, tb):
    t, c = h.shape
    nj, fc = w1.shape[0], w1.shape[3]
    ff = nj * fc
    ni = t // tb
    nt = (((1,), (1,)), ((), ()))
    douts = dout if isinstance(dout, tuple) else (dout,)
    nd = len(douts)

    def body(h_ref, s_ref, *rest):
        dout_refs = rest[:nd]
        (w1_ref, w2_ref, g_ref, b_ref, dh_ref, ds_ref, dhid_ref, act_ref, dg_ref, db_ref,
         ds_scr, dh_acc, dg_acc, db_acc) = rest[nd:]
        i, j = pl.program_id(0), pl.program_id(1)

        @pl.when((i == 0) & (j == 0))
        def _():
            dg_acc[...] = jnp.zeros_like(dg_acc)
            db_acc[...] = jnp.zeros_like(db_acc)

        @pl.when(j == 0)
        def _():
            _, vjp = jax.vjp(_ln, s_ref[...], g_ref[...], b_ref[...])
            ds, dg, db = vjp(functools.reduce(jnp.add, [r[...] for r in dout_refs]))
            ds_scr[...] = ds
            ds_ref[...] = ds.astype(BF16)
            dh_acc[...] = DN_ALPHA * ds
            dg_acc[...] += dg
            db_acc[...] += db

        w1b, w2b = w1_ref[...].astype(BF16), w2_ref[...].astype(BF16)
        hid = jnp.dot(h_ref[...].astype(BF16), w1b, preferred_element_type=F32)
        rl = jnp.maximum(hid, 0.0)
        dact = lax.dot_general(ds_scr[...].astype(BF16), w2b, nt, preferred_element_type=F32)
        dhid = (dact * 2.0 * rl).astype(BF16)
        dh_acc[...] += lax.dot_general(dhid, w1b, nt, preferred_element_type=F32)
        dhid_ref[...] = dhid
        act_ref[...] = (rl * rl).astype(BF16)

        @pl.when(j == nj - 1)
        def _():
            dh_ref[...] = dh_acc[...]

        @pl.when((i == ni - 1) & (j == nj - 1))
        def _():
            dg_ref[...] = dg_acc[...]
            db_ref[...] = db_acc[...]

    row = pl.BlockSpec((tb, c), lambda i, j: (i, 0))
    vec = pl.BlockSpec((1, c), lambda i, j: (0, 0))
    wide = pl.BlockSpec((tb, fc), lambda i, j: (i, j))
    return pl.pallas_call(
        body, name=name, grid=(ni, nj),
        in_specs=[row, row] + [row] * nd + [pl.BlockSpec((None, None, c, fc), lambda i, j: (j, layer, 0, 0)),
                                            pl.BlockSpec((None, None, fc, c), lambda i, j: (j, layer, 0, 0)), vec, vec],
        out_specs=[row, row, wide, wide, vec, vec],
        out_shape=[_sds((t, c)), _sds((t, c), BF16), _sds((t, ff), BF16), _sds((t, ff), BF16), _sds((1, c)), _sds((1, c))],
        scratch_shapes=[pltpu.VMEM((tb, c), F32), pltpu.VMEM((tb, c), F32), pltpu.VMEM((1, c), F32), pltpu.VMEM((1, c), F32)],
        compiler_params=_cparams(("arbitrary", "arbitrary")),
    )(h, s, *douts, w1, w2, ln_g, ln_b)


def _rec_fwd(r, lw, k, v, a, b, hg, shards):
    nh, t, n = r.shape
    ln = REC_CHUNK
    nck = t // ln
    ngrp = nh // hg
    nsh = len(shards)
    steps = ngrp * nck

    def body(r_ref, lw_ref, k_ref, v_ref, a_ref, b_ref, *rest):
        src, (o_ref, s0_ref), dst = rest[:nsh], rest[nsh:nsh + 2], rest[nsh + 2:2 * nsh + 2]
        state, sems = rest[2 * nsh + 2], rest[2 * nsh + 3:]
        step = pl.program_id(0) * nck + pl.program_id(1)
        start, forward, finish = _gather_big_phases(src, dst, sems)
        pl.when(step == 0)(start)

        @pl.when(pl.program_id(1) == 0)
        def _():
            state[...] = jnp.zeros_like(state)

        s0 = state[...]
        s0_ref[...] = s0
        o, s1 = _rec_chunk(s0, r_ref[...], lw_ref[...], k_ref[...], v_ref[...], a_ref[...], b_ref[...])
        o_ref[...] = o
        state[...] = s1
        pl.when(step == steps // 2)(forward)
        pl.when(step == steps - 1)(finish)

    seq = pl.BlockSpec((hg, ln, n), lambda g, i: (g, i, 0))
    outs = pl.pallas_call(
        body, name="rec_fwd", grid=(ngrp, nck), in_specs=[seq] * 6 + [_HBM] * nsh,
        out_specs=[seq, pl.BlockSpec((None, hg, n, n), lambda g, i: (i, g, 0, 0))] + [_HBM] * nsh,
        out_shape=[_sds((nh, t, n)), _sds((nck, nh, n, n))] + [_sds((4,) + s.shape, s.dtype) for s in shards],
        scratch_shapes=[pltpu.VMEM((hg, n, n), F32)] + _gather_big_sems(nsh),
        compiler_params=_cparams(("arbitrary", "arbitrary")),
    )(r, lw, k, v, a, b, *shards)
    return outs[0], outs[1], outs[2:]


def _rec_bwd(r, lw, k, v, a, b, s0s, do, hg, chip_sums):
    nh, t, n = r.shape
    ln = REC_CHUNK
    nck = t // ln
    ngrp = nh // hg
    nsum = len(chip_sums)
    steps = ngrp * nck

    def body(r_ref, lw_ref, k_ref, v_ref, a_ref, b_ref, s0_ref, do_ref, *rest):
        src, grad_refs, land = rest[:nsum], rest[nsum:nsum + 6], rest[nsum + 6:2 * nsum + 6]
        dstate, sems = rest[2 * nsum + 6], rest[2 * nsum + 7:]
        step = pl.program_id(0) * nck + pl.program_id(1)
        start, finish = _scatter_big_phases(src, land, sems)
        pl.when(step == 0)(start)

        @pl.when(pl.program_id(1) == 0)
        def _():
            dstate[...] = jnp.zeros_like(dstate)

        _, vjp = jax.vjp(_rec_chunk, s0_ref[...], r_ref[...], lw_ref[...], k_ref[...], v_ref[...], a_ref[...], b_ref[...])
        ds0, *grads = vjp((do_ref[...], dstate[...]))
        dstate[...] = ds0
        for ref, val in zip(grad_refs, grads):
            ref[...] = val
        pl.when(step == steps - 1)(finish)

    seq = pl.BlockSpec((hg, ln, n), lambda g, i: (g, nck - 1 - i, 0))
    outs = pl.pallas_call(
        body, name="rec_bwd", grid=(ngrp, nck),
        in_specs=[seq] * 6 + [pl.BlockSpec((None, hg, n, n), lambda g, i: (nck - 1 - i, g, 0, 0)), seq] + [_HBM] * nsum,
        out_specs=[seq] * 6 + [_HBM] * nsum,
        out_shape=[_sds((nh, t, n))] * 6 + [_sds((3,) + s.shape[1:], s.dtype) for s in chip_sums],
        scratch_shapes=[pltpu.VMEM((hg, n, n), F32)] + _scatter_big_sems(nsum),
        compiler_params=_cparams(("arbitrary", "arbitrary")),
    )(r, lw, k, v, a, b, s0s, do, *chip_sums)
    return outs[:6], outs[6:]


def _scan_fwd(bu, abar, tb):
    t, w2 = bu.shape
    w = w2 // 2

    def body(bu_ref, a_ref, s_ref, h_scr, rows):
        @pl.when(pl.program_id(0) == 0)
        def _():
            h_scr[...] = jnp.zeros_like(h_scr)

        ar, ai = a_ref[:, :w], a_ref[:, w:]

        def step(i, carry):
            hr, hi = carry
            nr = ar * hr - ai * hi + bu_ref[pl.ds(i, 1), :w]
            ni = ar * hi + ai * hr + bu_ref[pl.ds(i, 1), w:]
            rows[pl.ds(i, 1), :w] = nr
            rows[pl.ds(i, 1), w:] = ni
            return nr, ni

        hr, hi = lax.fori_loop(0, tb, step, (h_scr[:, :w], h_scr[:, w:]))
        h_scr[:, :w] = hr
        h_scr[:, w:] = hi
        s_ref[...] = rows[...].astype(BF16)

    return pl.pallas_call(
        body, name="s5_scan_fwd", grid=(t // tb,),
        in_specs=[pl.BlockSpec((tb, w2), lambda i: (i, 0)), pl.BlockSpec((1, w2), lambda i: (0, 0))],
        out_specs=pl.BlockSpec((tb, w2), lambda i: (i, 0)), out_shape=_sds((t, w2), BF16),
        scratch_shapes=[pltpu.VMEM((1, w2), F32), pltpu.VMEM((tb, w2), F32)],
        compiler_params=_cparams(("arbitrary",)),
    )(bu, abar)


def _scan_bwd(ds, s, abar, tb):
    t, w2 = ds.shape
    w = w2 // 2
    nb = t // tb
    pack = 16
    per = tb // pack

    def body(ds_ref, s_ref, sprev_ref, a_ref, dbu_ref, da_ref, g_scr, da_acc, rows):
        i = pl.program_id(0)

        @pl.when(i == 0)
        def _():
            g_scr[...] = jnp.zeros_like(g_scr)
            da_acc[...] = jnp.zeros_like(da_acc)

        ar, ai = a_ref[:, :w], a_ref[:, w:]

        def step(n, carry):
            gr, gi = carry
            row = tb - 1 - n
            nr = ds_ref[pl.ds(row, 1), :w] + ar * gr + ai * gi
            ni = ds_ref[pl.ds(row, 1), w:] + ar * gi - ai * gr
            rows[pl.ds(row, 1), :w] = nr
            rows[pl.ds(row, 1), w:] = ni
            return nr, ni

        gr, gi = lax.fori_loop(0, tb, step, (g_scr[:, :w], g_scr[:, w:]))
        g_scr[:, :w] = gr
        g_scr[:, w:] = gi
        last = (lax.broadcasted_iota(jnp.int32, (pack, w2), 0) == pack - 1) & (i < nb - 1)
        before = jnp.sum(jnp.where(last, sprev_ref[...].astype(F32), 0.0), axis=0, keepdims=True)
        rid = lax.broadcasted_iota(jnp.int32, (tb, w2), 0)
        sp = jnp.where(rid == 0, before, pltpu.roll(s_ref[...].astype(F32), 1, 0))
        g = rows[...]
        dbu_ref[...] = g.astype(BF16)
        spr, spi, g_r, g_i = sp[:, :w], sp[:, w:], g[:, :w], g[:, w:]
        da_acc[:, :w] += jnp.sum(spr * g_r + spi * g_i, axis=0, keepdims=True)
        da_acc[:, w:] += jnp.sum(spr * g_i - spi * g_r, axis=0, keepdims=True)

        @pl.when(i == nb - 1)
        def _():
            da_ref[...] = da_acc[...]

    blk = pl.BlockSpec((tb, w2), lambda i: (nb - 1 - i, 0))
    prev = pl.BlockSpec((pack, w2), lambda i: (jnp.maximum((nb - 1 - i) * per - 1, 0), 0))
    return pl.pallas_call(
        body, name="s5_scan_bwd", grid=(nb,),
        in_specs=[blk, blk, prev, pl.BlockSpec((1, w2), lambda i: (0, 0))],
        out_specs=[blk, pl.BlockSpec((1, w2), lambda i: (0, 0))],
        out_shape=[_sds((t, w2), BF16), _sds((1, w2))],
        scratch_shapes=[pltpu.VMEM((1, w2), F32), pltpu.VMEM((1, w2), F32), pltpu.VMEM((tb, w2), F32)],
        compiler_params=_cparams(("arbitrary",)),
    )(ds, s, s, abar)


def _zoh_fwd(a_re, a_im, log_dt, b_re_t, b_im_t):
    def body(*refs):
        for r, val in zip(refs[5:], _f_zoh(*(x[...] for x in refs[:5]))):
            r[...] = val

    return pl.pallas_call(body, name="s5_zoh_fwd", out_shape=[_sds(a_re.shape)] * 2 + [_sds(b_re_t.shape)] * 2,
                          compiler_params=_cparams())(a_re, a_im, log_dt, b_re_t, b_im_t)


def _zoh_bwd(a_re, a_im, log_dt, b_re_t, b_im_t, couts):
    def body(*refs):
        _, vjp = jax.vjp(_f_zoh, *(x[...] for x in refs[:5]))
        for r, val in zip(refs[9:], vjp(tuple(x[...] for x in refs[5:9]))):
            r[...] = val

    ins = (a_re, a_im, log_dt, b_re_t, b_im_t)
    return pl.pallas_call(body, name="s5_zoh_bwd", out_shape=[_sds(x.shape) for x in ins],
                          compiler_params=_cparams())(*ins, *couts)


def _loss_head(h, target, tb):
    t, c = h.shape
    nb = t // tb

    def body(h_ref, t_ref, loss_ref, dh_ref, acc):
        i = pl.program_id(0)

        @pl.when(i == 0)
        def _():
            acc[...] = jnp.zeros_like(acc)

        d = h_ref[...] - t_ref[...]
        dh_ref[...] = d * (1.0 / c)
        acc[...] += 0.5 * jnp.sum(jnp.mean(d * d, axis=-1, keepdims=True), axis=0, keepdims=True)

        @pl.when(i == nb - 1)
        def _():
            loss_ref[...] = jnp.broadcast_to(acc[...], loss_ref.shape)

    row = pl.BlockSpec((tb, c), lambda i: (i, 0))
    return pl.pallas_call(
        body, name="loss_head", grid=(nb,), in_specs=[row, row],
        out_specs=[pl.BlockSpec((8, 128), lambda i: (0, 0)), row], out_shape=[_sds((8, 128)), _sds((t, c))],
        scratch_shapes=[pltpu.VMEM((1, 1), F32)], compiler_params=_cparams(("arbitrary",)),
    )(h, target)


def _rows_tile(rows):
    for cand in (512, 256, 128, 64, 32, 16, 8):
        if rows % cand == 0:
            return cand
    return rows


def _addn(name, arrs):
    rows, cols = arrs[0].shape
    tb = _rows_tile(rows)

    def body(*refs):
        acc = refs[0][...]
        for r in refs[1:-1]:
            acc = acc + r[...]
        refs[-1][...] = acc

    blk = pl.BlockSpec((tb, cols), lambda i: (i, 0))
    return pl.pallas_call(body, name=name, grid=(rows // tb,), in_specs=[blk] * len(arrs), out_specs=blk,
                          out_shape=_sds((rows, cols)), compiler_params=_cparams(("parallel",)))(*arrs)


def _adamw_math(w, g, m, v):
    m = ADAM_B1 * m + (1.0 - ADAM_B1) * g
    v = ADAM_B2 * v + (1.0 - ADAM_B2) * jnp.square(g)
    m_hat = m / (1.0 - ADAM_B1 ** ADAM_STEP)
    v_hat = v / (1.0 - ADAM_B2 ** ADAM_STEP)
    delta = -ADAM_LR * (m_hat / (jnp.sqrt(v_hat) + ADAM_EPS) + ADAM_WD * w)
    return delta, m, v


def _adamw(name, parts, w, m, v):
    rows, cols = w.shape
    tb = _rows_tile(rows)
    npart = len(parts)

    def body(*refs):
        g = refs[0][...]
        for r in refs[1:npart]:
            g = g + r[...]
        w_ref, m_ref, v_ref = refs[npart:npart + 3]
        g_out, d_out, m_out, v_out = refs[npart + 3:]
        delta, mn, vn = _adamw_math(w_ref[...], g, m_ref[...], v_ref[...])
        g_out[...] = g
        d_out[...] = delta
        m_out[...] = mn
        v_out[...] = vn

    blk = pl.BlockSpec((tb, cols), lambda i: (i, 0))
    return pl.pallas_call(body, name=name, grid=(rows // tb,), in_specs=[blk] * (npart + 3), out_specs=[blk] * 4,
                          out_shape=[_sds((rows, cols))] * 4, compiler_params=_cparams(("parallel",)))(*parts, w, m, v)


def _heads(a):
    t, c = a.shape
    return a.reshape(t, c // HEAD, HEAD).transpose(1, 0, 2)


def _unheads(a):
    nh, t, n = a.shape
    return a.transpose(1, 0, 2).reshape(t, nh * n)


def _shift_down(a):
    return jnp.concatenate([jnp.zeros_like(a[:1]), a[:-1]], axis=0)


def _shift_up(a):
    return jnp.concatenate([a[1:], jnp.zeros_like(a[:1])], axis=0)


def _s5_pack_mask(g):
    return (jnp.arange(g)[None, :] % S5_PACK == jnp.arange(S5_PACK)[:, None]).astype(F32)


def _compact_b(bbar_t):
    g, s, p = bbar_t.shape
    return (_s5_pack_mask(g)[:, None, :, None] * bbar_t.transpose(1, 0, 2)[None]).reshape(S5_PACK * s, g * p)


def _compact_b_t(dense, g):
    s, p = dense.shape[0] // S5_PACK, dense.shape[1] // g
    return jnp.sum(dense.reshape(S5_PACK, s, g, p) * _s5_pack_mask(g)[:, None, :, None], axis=0).transpose(1, 0, 2)


def _compact_c(c_w):
    g, s, p = c_w.shape
    return (c_w.transpose(0, 2, 1)[:, :, None, :] * _s5_pack_mask(g).T[:, None, :, None]).reshape(g * p, S5_PACK * s)


def _compact_c_t(dense, g):
    p, s = dense.shape[0] // g, dense.shape[1] // S5_PACK
    return jnp.sum(dense.reshape(g, p, S5_PACK, s) * _s5_pack_mask(g).T[:, None, :, None], axis=2).transpose(0, 2, 1)


def _local_step(x, target, fw, core):
    t, c = x.shape
    nh = c // HEAD
    ng = c // SSM_GROUP
    tb = min(256, t)
    tbb = min(128, t)
    tbm = min(512, t)
    tbmb = min(512, t)
    tbs = min(64, t)
    tb5 = min(1024, t)
    tk5 = min(2048, t)
    hg = min(8, nh)
    mu = [fw['rw_mu'][i:i + 1] for i in range(6)]
    ln_g = [fw['ln_g'][i:i + 1] for i in range(4)]
    ln_b = [fw['ln_b'][i:i + 1] for i in range(4)]
    grads = {}

    xp = _shift_down(x)
    proj_params = {n: (mu[i], fw['rw_w' + n]) for n, i in (('r', 0), ('k', 2), ('v', 3))}
    raw = {n: _stage_fwd("proj_" + n, _f_proj, (x, xp), proj_params[n], (c,), tb)[0] for n in 'rkv'}
    lora_params = (mu[1], mu[4], mu[5], fw['rw_w0'], fw['rw_w1'], fw['rw_w2'], fw['rw_a0'], fw['rw_a1'], fw['rw_a2'],
                   fw['rw_g1'], fw['rw_g2'], fw['rw_k_k'], fw['rw_k_a'])
    lw, k2, an, bb, gate = _stage_fwd("lora", _f_lora, (x, xp, raw['k']), lora_params, (c,) * 5, tb)
    rec_in = tuple(_heads(a) for a in (raw['r'], lw, k2, raw['v'], an, bb))
    o_h, s0s, (mlp_w1, mlp_w2) = _rec_fwd(*rec_in, hg, fw['mlp_shards'])
    fw = dict(fw, mlp_w1=mlp_w1.reshape(4, DEPTH, c, -1), mlp_w2=mlp_w2.reshape(4, DEPTH, -1, c))
    o = _unheads(o_h)
    post_params = (fw['rw_lnx_g'], fw['rw_lnx_b'], fw['rw_r_k'], fw['rw_wo'], ln_g[0], ln_b[0])
    post_acts = (o, raw['r'], k2, raw['v'], gate, x)
    h1, = _stage_fwd("post", _f_post, post_acts, post_params, (c,), tb)
    h2, s_mlp0 = _mlp_fwd("mlp0_fwd", h1, fw['mlp_w1'], fw['mlp_w2'], 0, ln_g[1], ln_b[1], tbm)

    a_re, a_im, log_dt = fw['s5_a_re'], fw['s5_a_im'], fw['s5_log_dt']
    b_re_t, b_im_t = fw['s5_b_re'].transpose(0, 2, 1), fw['s5_b_im'].transpose(0, 2, 1)
    abar_re, abar_im, bbar_re_t, bbar_im_t = _zoh_fwd(a_re, a_im, log_dt, b_re_t, b_im_t)
    abar = jnp.concatenate([abar_re.reshape(1, -1), abar_im.reshape(1, -1)], axis=1)
    bc = jnp.concatenate([_compact_b(bbar_re_t), _compact_b(bbar_im_t)], axis=1).astype(BF16)
    cc = jnp.concatenate([_compact_c(fw['s5_c_re']), -_compact_c(fw['s5_c_im'])], axis=0).astype(BF16)
    bu = _s5_wide_from_narrow("s5_bu", h2, bc, "nn", tb5)
    st = _scan_fwd(bu, abar, tbs)
    ys = _s5_narrow_from_wide("s5_y", st, cc, "nn", tb5)
    glu_params = (fw['s5_d'], *fw['s5_w_glu'], ln_g[2], ln_b[2])
    h3, = _stage_fwd("glu", _f_glu, (ys, h2), glu_params, (c,), tb)
    h4, s_mlp1 = _mlp_fwd("mlp1_fwd", h3, fw['mlp_w1'], fw['mlp_w2'], 1, ln_g[3], ln_b[3], tbm)

    loss_blk, dh4 = _loss_head(h4, target, tb)

    dln_g, dln_b = [None] * 4, [None] * 4
    dh3, ds1, dhid1, act1, dln_g[3], dln_b[3] = _mlp_bwd("mlp1_bwd", h3, s_mlp1, dh4, fw['mlp_w1'], fw['mlp_w2'], 1,
                                                         ln_g[3], ln_b[3], tbmb)
    dw1 = _mlp_weight_grad("mlp1_dw1", h3, dhid1, 1, DEPTH, "n")
    dw2 = _mlp_weight_grad("mlp1_dw2", act1, ds1, 1, DEPTH, "m")
    (dys, dh2_glu), (grads['s5_d'], *dglu, dln_g[2], dln_b[2]) = _stage_bwd(
        "glu_bwd", _f_glu, (ys, h2), glu_params, (dh3,), tbb, proxied=(1, 2, 3, 4))
    grads['s5_w_glu'] = jnp.stack(dglu)
    dst = _s5_wide_from_narrow("s5_dst", dys, cc, "nt", tb5)
    dcc = _s5_weight_grad("s5_dcc", dys, st, True, tk5)
    dbu, dabar = _scan_bwd(dst, st, abar, tbs)
    dh2_bu = _s5_narrow_from_wide("s5_dh", dbu, bc, "nt", tb5)
    dbc = _s5_weight_grad("s5_dbc", h2, dbu, False, tk5)
    gp = ng * SSM_STATE
    grads['s5_c_re'] = _compact_c_t(dcc[:gp], ng)
    grads['s5_c_im'] = -_compact_c_t(dcc[gp:], ng)
    zoh_couts = (dabar[:, :gp].reshape(ng, SSM_STATE), dabar[:, gp:].reshape(ng, SSM_STATE),
                 _compact_b_t(dbc[:, :gp], ng), _compact_b_t(dbc[:, gp:], ng))
    grads['s5_a_re'], grads['s5_a_im'], grads['s5_log_dt'], db_re_t, db_im_t = _zoh_bwd(
        a_re, a_im, log_dt, b_re_t, b_im_t, zoh_couts)
    grads['s5_b_re'], grads['s5_b_im'] = db_re_t.transpose(0, 2, 1), db_im_t.transpose(0, 2, 1)
    dh2 = (dh2_glu, dh2_bu)

    dh1, ds0, dhid0, act0, dln_g[1], dln_b[1] = _mlp_bwd("mlp0_bwd", h1, s_mlp0, dh2, fw['mlp_w1'], fw['mlp_w2'], 0,
                                                         ln_g[1], ln_b[1], tbmb)
    grads['mlp_w1'] = _mlp_weight_grad("mlp0_dw1", h1, dhid0, 0, DEPTH, "n", into=dw1)
    grads['mlp_w2'] = _mlp_weight_grad("mlp0_dw2", act0, ds0, 0, DEPTH, "m", into=dw2)
    (do, dr_p, dk2_p, dv_p, dgate, dx_post), post_g = _stage_bwd("post_bwd", _f_post, post_acts, post_params, (dh1,), tbb,
                                                                 proxied=(3,))
    grads['rw_lnx_g'], grads['rw_lnx_b'], grads['rw_r_k'], grads['rw_wo'], dln_g[0], dln_b[0] = post_g
    ready_sums = _chip_sums("a", [grads[n] for n in BIG_READY], core)
    rec_g, ready_lands = _rec_bwd(*rec_in, s0s, _heads(do), hg, ready_sums)
    reduced = dict(zip(BIG_READY, zip(ready_sums, ready_lands)))
    dr_r, dlw, dk2_r, dv_r, dan, dbb = (_unheads(a) for a in rec_g)
    dk2 = (dk2_p, dk2_r)
    (dx_l, dxp_l, dkraw_l), lora_g = _stage_bwd("lora_bwd", _f_lora, (x, xp, raw['k']), lora_params,
                                                (dlw, dk2, dan, dbb, dgate), tbb)
    (dmu_w, dmu_a, dmu_g, grads['rw_w0'], grads['rw_w1'], grads['rw_w2'], grads['rw_a0'], grads['rw_a1'], grads['rw_a2'],
     grads['rw_g1'], grads['rw_g2'], grads['rw_k_k'], grads['rw_k_a']) = lora_g
    dproj = {'r': (dr_p, dr_r), 'k': dkraw_l, 'v': (dv_p, dv_r)}
    dxs, dxps, dmu = [dx_post, dx_l], [dxp_l], {}
    for n in 'rkv':
        (dx_n, dxp_n), (dmu[n], grads['rw_w' + n]) = _stage_bwd("proj_bwd_" + n, _f_proj, (x, xp), proj_params[n],
                                                                 (dproj[n],), tb, proxied=(1,))
        dxs.append(dx_n)
        dxps.append(dxp_n)
    grads['rw_mu'] = jnp.concatenate([dmu['r'], dmu_w, dmu['k'], dmu['v'], dmu_a, dmu_g], axis=0)
    grads['ln_g'] = jnp.concatenate(dln_g, axis=0)
    grads['ln_b'] = jnp.concatenate(dln_b, axis=0)
    grad_x = _addn("grad_x", dxs + [_shift_up(_addn("dxp_sum", dxps))])
    late = [n for n in BIG if n not in BIG_READY]
    late_sums = _chip_sums("b", [grads[n] for n in late], core)
    reduced.update(zip(late, zip(late_sums, _scatter_big(late_sums))))
    return loss_blk, grad_x, grads, reduced


def _position():
    return lax.axis_index("x"), lax.axis_index("y"), lax.axis_index("c")


def _other_chips(x, y):
    return [(1 - x, y), (x, 1 - y), (1 - x, 1 - y)]


def _chip_slice(ref, axis, q, size):
    idx = [slice(None)] * len(ref.shape)
    idx[axis] = pl.ds(pl.multiple_of(q * size, size), size)
    return ref.at[tuple(idx)]


_HBM = pl.BlockSpec(memory_space=pltpu.HBM)


def _gather_small_phases(src, dst, axes, sems):
    n = len(src)
    send_sems, recv_sems, own_sems = sems
    x, y, c = _position()
    chips = _other_chips(x, y)
    sizes = [src[a].shape[axes[a]] for a in range(n)]

    def copy(a, k, q):
        return pltpu.make_async_remote_copy(
            src_ref=src[a], dst_ref=_chip_slice(dst[a], axes[a], q, sizes[a]), send_sem=send_sems.at[a, k],
            recv_sem=recv_sems.at[a, k], device_id=(*chips[k], c), device_id_type=MESH)

    def own(a):
        return pltpu.make_async_copy(src[a], _chip_slice(dst[a], axes[a], 2 * x + y, sizes[a]), own_sems.at[a])

    def start():
        for a in range(n):
            own(a).start()
            for k in range(3):
                copy(a, k, 2 * x + y).start()

    def finish():
        for a in range(n):
            for k, (cx, cy) in enumerate(chips):
                copy(a, k, 2 * cx + cy).wait_recv()
        for a in range(n):
            for k in range(3):
                copy(a, k, 2 * x + y).wait_send()
            own(a).wait()

    return start, finish


def _gather_early(big, small, axes):
    nb, ns = len(big), len(small)
    full_shapes = [tuple(s * 4 if i == ax else s for i, s in enumerate(a.shape)) for a, ax in zip(small, axes)]

    def body(*refs):
        src_b, src_s = refs[:nb], refs[nb:nb + ns]
        dst_b, dst_s = refs[nb + ns:2 * nb + ns], refs[2 * nb + ns:2 * (nb + ns)]
        sems = refs[2 * (nb + ns):]
        small_start, small_finish = _gather_small_phases(src_s, dst_s, axes, sems[5:])
        small_start()
        for phase in _gather_big_phases(src_b, dst_b, sems[:5]):
            phase()
        small_finish()

    outs = pl.pallas_call(
        body, name="gather_early", in_specs=[_HBM] * (nb + ns), out_specs=[_HBM] * (nb + ns),
        out_shape=[_sds((4,) + a.shape, a.dtype) for a in big] + [_sds(s, a.dtype) for s, a in zip(full_shapes, small)],
        scratch_shapes=_gather_big_sems(nb) + [pltpu.SemaphoreType.DMA((ns, 3)), pltpu.SemaphoreType.DMA((ns, 3)),
                                               pltpu.SemaphoreType.DMA((ns,))],
        compiler_params=_cparams(),
    )(*big, *small)
    return outs[:nb], outs[nb:]


def _scatter_pieces(fulls, axes):
    n = len(fulls)
    sizes = [a.shape[ax] // 4 for a, ax in zip(fulls, axes)]
    shard_shapes = [tuple(sz if i == ax else s for i, s in enumerate(a.shape)) for a, ax, sz in zip(fulls, axes, sizes)]

    def body(*refs):
        src, land = refs[:n], refs[n:2 * n]
        send_sems, recv_sems = refs[2 * n:]
        x, y, c = _position()
        chips = _other_chips(x, y)

        def copy(a, k):
            cx, cy = chips[k]
            return pltpu.make_async_remote_copy(
                src_ref=_chip_slice(src[a], axes[a], 2 * cx + cy, sizes[a]), dst_ref=land[a].at[k],
                send_sem=send_sems.at[a, k], recv_sem=recv_sems.at[a, k], device_id=(cx, cy, c), device_id_type=MESH)

        for a in range(n):
            for k in range(3):
                copy(a, k).start()
        for a in range(n):
            for k in range(3):
                copy(a, k).wait_recv()
        for a in range(n):
            for k in range(3):
                copy(a, k).wait_send()

    return pl.pallas_call(
        body, name="scatter_grads", in_specs=[_HBM] * n, out_specs=[_HBM] * n,
        out_shape=[_sds((3,) + s) for s in shard_shapes],
        scratch_shapes=[pltpu.SemaphoreType.DMA((n, 3)), pltpu.SemaphoreType.DMA((n, 3))],
        compiler_params=_cparams(),
    )(*fulls)


def _sibling_swap(name, arrs):
    n = len(arrs)

    def body(*refs):
        src, dst = refs[:n], refs[n:2 * n]
        send_sems, recv_sems = refs[2 * n:]
        x, y, c = _position()
        copies = [pltpu.make_async_remote_copy(src_ref=src[a], dst_ref=dst[a], send_sem=send_sems.at[a], recv_sem=recv_sems.at[a],
                                               device_id=(x, y, 1 - c), device_id_type=MESH) for a in range(n)]
        for cp in copies:
            cp.start()
        for cp in copies:
            cp.wait_recv()
        for cp in copies:
            cp.wait_send()

    return pl.pallas_call(
        body, name=name, in_specs=[_HBM] * n, out_specs=[_HBM] * n, out_shape=[_sds(a.shape) for a in arrs],
        scratch_shapes=[pltpu.SemaphoreType.DMA((n,)), pltpu.SemaphoreType.DMA((n,))],
        compiler_params=_cparams(),
    )(*arrs)


def _sum4(name, own, land):
    rows, cols = own.shape
    tb = _rows_tile(rows)

    def body(o_ref, l0, l1, l2, out_ref):
        out_ref[...] = ((o_ref[...] + l0[...]) + l1[...]) + l2[...]

    blk = pl.BlockSpec((tb, cols), lambda i: (i, 0))
    lands = [pl.BlockSpec((None, tb, cols), functools.partial(lambda k, i: (k, i, 0), k)) for k in range(3)]
    return pl.pallas_call(body, name=name, grid=(rows // tb,), in_specs=[blk] + lands, out_specs=blk,
                          out_shape=_sds((rows, cols)), compiler_params=_cparams(("parallel",)))(own, land, land, land)


def _allreduce_adamw_small(g, w, m, v):
    rows, lanes = g.shape

    def body(g_ref, w_ref, m_ref, v_ref, gs_ref, d_ref, mn_ref, vn_ref, land, send_sems, recv_sems):
        x, y, c = _position()
        me = 4 * x + 2 * y + c
        masks = [(bx, by, bc) for bx in (0, 1) for by in (0, 1) for bc in (0, 1)][1:]

        def peer(mask):
            return (x ^ mask[0], y ^ mask[1], c ^ mask[2])

        def copy(j, slot):
            return pltpu.make_async_remote_copy(src_ref=g_ref, dst_ref=land.at[slot], send_sem=send_sems.at[j],
                                                recv_sem=recv_sems.at[j], device_id=peer(masks[j]), device_id_type=MESH)

        for j in range(7):
            copy(j, me).start()
        land[me] = g_ref[...]
        for j in range(7):
            px, py, pc = peer(masks[j])
            copy(j, 4 * px + 2 * py + pc).wait_recv()
        for j in range(7):
            copy(j, me).wait_send()
        total = land[0]
        for dev in range(1, 8):
            total = total + land[dev]
        delta, mn, vn = _adamw_math(w_ref[...], total, m_ref[...], v_ref[...])
        gs_ref[...] = total
        d_ref[...] = delta
        mn_ref[...] = mn
        vn_ref[...] = vn

    vmem = pl.BlockSpec(memory_space=pltpu.VMEM)
    return pl.pallas_call(
        body, name="allreduce_adamw_small", in_specs=[vmem] * 4, out_specs=[vmem] * 4, out_shape=[_sds((rows, lanes))] * 4,
        scratch_shapes=[pltpu.VMEM((8, rows, lanes), F32), pltpu.SemaphoreType.DMA((7,)), pltpu.SemaphoreType.DMA((7,))],
        compiler_params=_cparams(),
    )(g, w, m, v)


def _row_half(ref, c):
    r2 = ref.shape[-2] // 2
    lead = (slice(None),) * (len(ref.shape) - 2)
    return ref.at[(*lead, pl.ds(pl.multiple_of(c * r2, r2), r2), slice(None))]


def _gather_big_phases(src, dst, sems):
    n = len(src)
    ici_send, ici_recv, d2d_send, d2d_recv, own_sems = sems
    x, y, c = _position()
    me = 2 * x + y
    chips = _other_chips(x, y)
    ids = [2 * cx + cy for cx, cy in chips]

    def ici(a, k, q):
        return pltpu.make_async_remote_copy(
            src_ref=_row_half(src[a], c), dst_ref=_row_half(dst[a].at[q], c), send_sem=ici_send.at[a, k],
            recv_sem=ici_recv.at[a, k], device_id=(*chips[k], c), device_id_type=MESH)

    def d2d(a, k, half):
        where = _row_half(dst[a].at[ids[k]], half)
        return pltpu.make_async_remote_copy(src_ref=where, dst_ref=where, send_sem=d2d_send.at[a, k], recv_sem=d2d_recv.at[a, k],
                                            device_id=(x, y, 1 - c), device_id_type=MESH)

    def own(a):
        return pltpu.make_async_copy(src[a], dst[a].at[me], own_sems.at[a])

    def start():
        for a in range(n):
            own(a).start()
            for k in range(3):
                ici(a, k, me).start()

    def forward():
        for a in range(n):
            for k in range(3):
                ici(a, k, ids[k]).wait_recv()
                d2d(a, k, c).start()

    def finish():
        for a in range(n):
            for k in range(3):
                d2d(a, k, 1 - c).wait_recv()
        for a in range(n):
            for k in range(3):
                ici(a, k, me).wait_send()
                d2d(a, k, c).wait_send()
            own(a).wait()

    return start, forward, finish


def _gather_big_sems(n):
    return [pltpu.SemaphoreType.DMA((n, 3))] * 4 + [pltpu.SemaphoreType.DMA((n,))]


def _chip_sums(tag, grads, core):
    views = [g.reshape(4, -1, g.shape[-1]) for g in grads]
    others = _sibling_halves("sibling_halves_" + tag, views)
    return [_half_add(f"half_add_{tag}{i}", v, o, core) for i, (v, o) in enumerate(zip(views, others))]


def _sibling_halves(name, views):
    n = len(views)

    def body(*refs):
        src, dst = refs[:n], refs[n:2 * n]
        send_sems, recv_sems = refs[2 * n:]
        x, y, c = _position()
        copies = [pltpu.make_async_remote_copy(src_ref=_row_half(src[a], 1 - c), dst_ref=dst[a], send_sem=send_sems.at[a],
                                               recv_sem=recv_sems.at[a], device_id=(x, y, 1 - c), device_id_type=MESH)
                  for a in range(n)]
        for cp in copies:
            cp.start()
        for cp in copies:
            cp.wait_recv()
        for cp in copies:
            cp.wait_send()

    return pl.pallas_call(
        body, name=name, in_specs=[_HBM] * n, out_specs=[_HBM] * n,
        out_shape=[_sds((4, v.shape[1] // 2, v.shape[2])) for v in views],
        scratch_shapes=[pltpu.SemaphoreType.DMA((n,)), pltpu.SemaphoreType.DMA((n,))], compiler_params=_cparams(),
    )(*views)


def _rows_tile_capped(rows, cap=256):
    return min(_rows_tile(rows), cap)


def _half_add(name, view, other, core):
    _, r, k = view.shape
    r2 = r // 2
    tr = _rows_tile_capped(r2)
    per = r2 // tr

    def body(c_ref, v_ref, o_ref, out_ref):
        out_ref[...] = (v_ref[...] + o_ref[...]).astype(BF16)

    blk = pl.BlockSpec((None, tr, k), lambda q, i, c: (q, i, 0))
    return pl.pallas_call(
        body, name=name,
        grid_spec=pltpu.PrefetchScalarGridSpec(
            num_scalar_prefetch=1, grid=(4, per),
            in_specs=[pl.BlockSpec((None, tr, k), lambda q, i, c: (q, c[0] * per + i, 0)), blk], out_specs=blk),
        out_shape=_sds((4, r2, k), BF16), compiler_params=_cparams(("parallel", "parallel")),
    )(core, view, other)


def _scatter_big_phases(src, land, sems):
    n = len(src)
    send_sems, recv_sems = sems
    x, y, c = _position()
    chips = _other_chips(x, y)

    def copy(a, k):
        cx, cy = chips[k]
        return pltpu.make_async_remote_copy(src_ref=src[a].at[2 * cx + cy], dst_ref=land[a].at[k], send_sem=send_sems.at[a, k],
                                            recv_sem=recv_sems.at[a, k], device_id=(cx, cy, c), device_id_type=MESH)

    def start():
        for a in range(n):
            for k in range(3):
                copy(a, k).start()

    def finish():
        for a in range(n):
            for k in range(3):
                copy(a, k).wait_recv()
        for a in range(n):
            for k in range(3):
                copy(a, k).wait_send()

    return start, finish


def _scatter_big_sems(n):
    return [pltpu.SemaphoreType.DMA((n, 3)), pltpu.SemaphoreType.DMA((n, 3))]


def _scatter_big(sums):
    n = len(sums)

    def body(*refs):
        for phase in _scatter_big_phases(refs[:n], refs[n:2 * n], refs[2 * n:]):
            phase()

    return pl.pallas_call(
        body, name="scatter_big", in_specs=[_HBM] * n, out_specs=[_HBM] * n,
        out_shape=[_sds((3,) + s.shape[1:], s.dtype) for s in sums], scratch_shapes=_scatter_big_sems(n), compiler_params=_cparams(),
    )(*sums)


def _sum4_big(name, sums, land, chip):
    _, r2, k = sums.shape
    tr = _rows_tile_capped(r2)

    def body(q_ref, s_ref, l0, l1, l2, out_ref):
        out_ref[...] = ((s_ref[...].astype(F32) + l0[...].astype(F32)) + l1[...].astype(F32)) + l2[...].astype(F32)

    lands = [pl.BlockSpec((None, tr, k), functools.partial(lambda j, i, q: (j, i, 0), j)) for j in range(3)]
    return pl.pallas_call(
        body, name=name,
        grid_spec=pltpu.PrefetchScalarGridSpec(
            num_scalar_prefetch=1, grid=(r2 // tr,),
            in_specs=[pl.BlockSpec((None, tr, k), lambda i, q: (q[0], i, 0))] + lands,
            out_specs=pl.BlockSpec((tr, k), lambda i, q: (i, 0))),
        out_shape=_sds((r2, k)), compiler_params=_cparams(("parallel",)),
    )(chip, sums, land, land, land)


def _adamw_halves(name, mine, theirs, w, m, v, core):
    r, k = w.shape
    r2 = r // 2
    tr = _rows_tile_capped(r2)
    per = r2 // tr

    def body(c_ref, mine_ref, theirs_ref, w_ref, m_ref, v_ref, g_out, d_out, m_out, v_out):
        g = jnp.where(pl.program_id(0) == c_ref[0], mine_ref[...], theirs_ref[...])
        delta, mn, vn = _adamw_math(w_ref[...], g, m_ref[...], v_ref[...])
        g_out[...] = g
        d_out[...] = delta
        m_out[...] = mn
        v_out[...] = vn

    half = pl.BlockSpec((tr, k), lambda h, i, c: (i, 0))
    full = pl.BlockSpec((tr, k), lambda h, i, c: (h * per + i, 0))
    return pl.pallas_call(
        body, name=name,
        grid_spec=pltpu.PrefetchScalarGridSpec(num_scalar_prefetch=1, grid=(2, per), in_specs=[half, half, full, full, full],
                                               out_specs=[full] * 4),
        out_shape=[_sds((r, k))] * 4, compiler_params=_cparams(("parallel", "parallel")),
    )(core, mine, theirs, w, m, v)


def _drops_layer_axis(name):
    return not (name.startswith('mlp') or name == 's5_d')


def _work(name, arr):
    return arr.reshape(arr.shape[1:]) if _drops_layer_axis(name) else arr


def _work_axis(name):
    return SHARD_AXIS[name] - (1 if _drops_layer_axis(name) else 0)


def _as2d(a):
    return a.reshape(-1, a.shape[-1])


def _replicated_2d(name, arr):
    if name in ('ln_g', 'ln_b'):
        return arr
    if name == 'rw_r_k':
        return arr.reshape(1, -1)
    if name == 's5_log_dt':
        return arr.reshape(-1, 1)
    if name.startswith('s5_'):
        return arr.reshape(arr.shape[1:])
    return arr


def _pack(arrs):
    flat = []
    for a in arrs:
        f = a.reshape(-1)
        flat.append(jnp.pad(f, (0, -f.shape[0] % 128)))
    f = jnp.concatenate(flat)
    f = jnp.pad(f, (0, -f.shape[0] % 1024))
    return f.reshape(-1, 128)


def _unpack(packed, shapes):
    flat = packed.reshape(-1)
    out, at = [], 0
    for s in shapes:
        size = math.prod(s)
        out.append(flat[at:at + size].reshape(s))
        at += size + (-size % 128)
    return out


def kernel(x, ln_g, ln_b, rw_mu, rw_w0, rw_w1, rw_w2, rw_a0, rw_a1, rw_a2, rw_g1, rw_g2, rw_k_k, rw_k_a, rw_r_k, rw_wr, rw_wk, rw_wv, rw_wo, rw_lnx_g, rw_lnx_b, s5_a_re, s5_a_im, s5_log_dt, s5_b_re, s5_b_im, s5_c_re, s5_c_im, s5_d, s5_w_glu, mlp_w1, mlp_w2, loss_target, m_ln_g, m_ln_b, m_rw_mu, m_rw_w0, m_rw_w1, m_rw_w2, m_rw_a0, m_rw_a1, m_rw_a2, m_rw_g1, m_rw_g2, m_rw_k_k, m_rw_k_a, m_rw_r_k, m_rw_wr, m_rw_wk, m_rw_wv, m_rw_wo, m_rw_lnx_g, m_rw_lnx_b, m_s5_a_re, m_s5_a_im, m_s5_log_dt, m_s5_b_re, m_s5_b_im, m_s5_c_re, m_s5_c_im, m_s5_d, m_s5_w_glu, m_mlp_w1, m_mlp_w2, v_ln_g, v_ln_b, v_rw_mu, v_rw_w0, v_rw_w1, v_rw_w2, v_rw_a0, v_rw_a1, v_rw_a2, v_rw_g1, v_rw_g2, v_rw_k_k, v_rw_k_a, v_rw_r_k, v_rw_wr, v_rw_wk, v_rw_wv, v_rw_wo, v_rw_lnx_g, v_rw_lnx_b, v_s5_a_re, v_s5_a_im, v_s5_log_dt, v_s5_b_re, v_s5_b_im, v_s5_c_re, v_s5_c_im, v_s5_d, v_s5_w_glu, v_mlp_w1, v_mlp_w2):
    d = dict(locals())
    x_pos, y_pos, c_pos = _position()
    chip = 2 * x_pos + y_pos
    chip_arr = jnp.reshape(chip, (1,)).astype(jnp.int32)
    core_arr = jnp.reshape(c_pos, (1,)).astype(jnp.int32)

    small = [n for n in SHARD_AXIS if n not in BIG]
    axes = [_work_axis(n) for n in small]
    big_views, small_fulls = _gather_early([_as2d(d[n]).astype(BF16) for n in BIG_EARLY], [_work(n, d[n]) for n in small], axes)
    views = dict(zip(BIG_EARLY, big_views))
    fw = dict(zip(small, small_fulls))
    c_model = d['x'].shape[-1]
    for n in ('rw_wr', 'rw_wk', 'rw_wv', 'rw_wo'):
        fw[n] = views[n].reshape(c_model, c_model)
    fw['s5_w_glu'] = tuple(views['s5_w_glu'][q] for q in range(4))
    fw['mlp_shards'] = [_as2d(d[n]).astype(BF16) for n in BIG_LATE]
    for n in REPLICATED:
        fw[n] = _replicated_2d(n, d[n])

    loss_blk, grad_x, grads, reduced = _local_step(d['x'][0], d['loss_target'][0], fw, core_arr)
    loss = lax.psum(loss_blk[0, 0], ('x', 'y', 'c'))
    out = {}

    mine = [_sum4_big("sum4_" + n, *reduced[n], chip_arr) for n in BIG]
    theirs = _sibling_swap("swap_big", mine)
    for n, a, b in zip(BIG, mine, theirs):
        res = _adamw_halves("adamw_" + n, a, b, _as2d(d[n]), _as2d(d['m_' + n]), _as2d(d['v_' + n]), core_arr)
        out[n] = [r.reshape(d[n].shape) for r in res]

    pieces = [grads[n] for n in small]
    lands = _scatter_pieces(pieces, axes)
    mine = []
    for n, g, ax, land in zip(small, pieces, axes, lands):
        size = g.shape[ax] // 4
        mine.append(_sum4("sum4_" + n, lax.dynamic_slice_in_dim(g, chip * size, size, ax), land))
    theirs = _sibling_swap("swap_small", mine)
    for n, a, b in zip(small, mine, theirs):
        res = _adamw("adamw_" + n, (a, b), _as2d(d[n]), _as2d(d['m_' + n]), _as2d(d['v_' + n]))
        out[n] = [r.reshape(d[n].shape) for r in res]

    rep_shapes = [d[n].shape for n in REPLICATED]
    packs = [_pack([grads[n] for n in REPLICATED])] + [_pack([d[p + n] for n in REPLICATED]) for p in ('', 'm_', 'v_')]
    res = [_unpack(p, rep_shapes) for p in _allreduce_adamw_small(*packs)]
    for i, n in enumerate(REPLICATED):
        out[n] = [r[i] for r in res]

    grad_x = grad_x.reshape(d['x'].shape)
    return (loss, grad_x, *[out[n][0] for n in WEIGHTS], *[out[n][1] for n in WEIGHTS],
            *[out[n][2] for n in WEIGHTS], *[out[n][3] for n in WEIGHTS])
```

```python
import functools
import math

import jax
import jax.numpy as jnp
from jax import lax
from jax.experimental import pallas as pl
from jax.experimental.pallas import tpu as pltpu

F32 = jnp.float32
BF16 = jnp.bfloat16
MESH = pl.DeviceIdType.MESH

HEAD = 64
SSM_GROUP = 16
SSM_STATE = 64
GN_EPS = 64e-5
LN_EPS = 1e-5
DEPTH = 2
DN_ALPHA = (2.0 * DEPTH) ** 0.25
ADAM_LR, ADAM_B1, ADAM_B2, ADAM_EPS, ADAM_WD, ADAM_STEP = 0.001, 0.9, 0.999, 1e-08, 0.01, 10
REC_CHUNK = 64
V7X_VMEM_BYTES = 64 * 2 ** 20
VMEM_LIMIT = V7X_VMEM_BYTES - 8 * 2 ** 20

WEIGHTS = ['ln_g', 'ln_b', 'rw_mu', 'rw_w0', 'rw_w1', 'rw_w2', 'rw_a0', 'rw_a1', 'rw_a2', 'rw_g1', 'rw_g2',
           'rw_k_k', 'rw_k_a', 'rw_r_k', 'rw_wr', 'rw_wk', 'rw_wv', 'rw_wo', 'rw_lnx_g', 'rw_lnx_b',
           's5_a_re', 's5_a_im', 's5_log_dt', 's5_b_re', 's5_b_im', 's5_c_re', 's5_c_im', 's5_d', 's5_w_glu',
           'mlp_w1', 'mlp_w2']
SHARD_AXIS = {'rw_mu': 2, 'rw_w1': 1, 'rw_w2': 2, 'rw_a1': 1, 'rw_a2': 2, 'rw_g1': 1, 'rw_g2': 2,
              'rw_wr': 1, 'rw_wk': 1, 'rw_wv': 1, 'rw_wo': 1, 's5_d': 1, 's5_w_glu': 2, 'mlp_w1': 2, 'mlp_w2': 1}
REPLICATED = [n for n in WEIGHTS if n not in SHARD_AXIS]
BIG_EARLY = ['rw_wr', 'rw_wk', 'rw_wv']
BIG_LATE = ['rw_wo', 's5_w_glu', 'mlp_w1', 'mlp_w2']
BIG = BIG_EARLY + BIG_LATE
BIG_READY = ['rw_wo', 's5_w_glu', 'mlp_w1', 'mlp_w2']


def _sds(shape, dtype=F32):
    return jax.ShapeDtypeStruct(tuple(shape), dtype)


def _cparams(sem=None, **kw):
    if sem is not None:
        kw["dimension_semantics"] = sem
    return pltpu.CompilerParams(vmem_limit_bytes=VMEM_LIMIT, **kw)


def _mm_products(a, b, g):
    gb = g.astype(BF16)
    da = lax.dot_general(gb, b.astype(BF16), (((1,), (1,)), ((), ())), preferred_element_type=F32)
    db = lax.dot_general(a.astype(BF16), gb, (((0,), (0,)), ((), ())), preferred_element_type=F32)
    return da, db


@jax.custom_vjp
def _mm_plain(a, b):
    return jnp.dot(a.astype(BF16), b.astype(BF16), preferred_element_type=F32)


def _mm_plain_bwd(res, g):
    da, db = _mm_products(*res, g)
    return da.astype(res[0].dtype), db.astype(res[1].dtype)


_mm_plain.defvjp(lambda a, b: (_mm_plain(a, b), (a, b)), _mm_plain_bwd)


@jax.custom_vjp
def _mm_proxy(a, b, z):
    return jnp.dot(a.astype(BF16), b.astype(BF16), preferred_element_type=F32)


def _mm_proxy_bwd(res, g):
    da, db = _mm_products(*res, g)
    return da.astype(res[0].dtype), jnp.zeros_like(res[1]), db


_mm_proxy.defvjp(lambda a, b, z: (_mm_proxy(a, b, z), (a, b)), _mm_proxy_bwd)


def mm(a, b, z=None):
    return _mm_plain(a, b) if z is None else _mm_proxy(a, b, z)


def _split3(x):
    hi = x.astype(BF16)
    r1 = x - hi.astype(F32)
    mid = r1.astype(BF16)
    lo = (r1 - mid.astype(F32)).astype(BF16)
    return hi, mid, lo


def _head_sum_impl(x):
    c = x.shape[1]
    lanes = 128
    sel = (lax.broadcasted_iota(jnp.int32, (c, lanes), 0) // HEAD
           == lax.broadcasted_iota(jnp.int32, (c, lanes), 1)).astype(BF16)
    s = sum(jnp.dot(p, sel, preferred_element_type=F32) for p in _split3(x))
    return sum(lax.dot_general(p, sel, (((1,), (1,)), ((), ())), preferred_element_type=F32) for p in _split3(s))


@jax.custom_vjp
def head_sum(x):
    return _head_sum_impl(x)


head_sum.defvjp(lambda x: (_head_sum_impl(x), None), lambda _, g: (_head_sum_impl(g),))


def _ln(x, g, b):
    mu = jnp.mean(x, axis=-1, keepdims=True)
    xc = x - mu
    var = jnp.mean(xc * xc, axis=-1, keepdims=True)
    return xc * lax.rsqrt(var + LN_EPS) * g + b


def _f_proj(acts, params, proxies):
    x, xp = acts
    mu, w = params
    return (mm(x + (xp - x) * mu, w, proxies[1]),)


def _f_lora(acts, params, proxies):
    x, xp, kraw = acts
    mu_w, mu_a, mu_g, w0, w1, w2, a0, a1, a2, g1, g2, k_k, k_a = params
    xx = xp - x
    w_pre = w0 + mm(jnp.tanh(mm(x + xx * mu_w, w1)), w2)
    z = -w_pre
    softplus = jnp.maximum(z, 0.0) + jnp.log(1.0 + jnp.exp(-jnp.abs(z)))
    log_decay = -jnp.exp(-softplus - 0.5)
    a = jax.nn.sigmoid(a0 + mm(mm(x + xx * mu_a, a1), a2))
    g = mm(jax.nn.sigmoid(mm(x + xx * mu_g, g1)), g2)
    kk = kraw * k_k
    kkn = kk / jnp.maximum(jnp.sqrt(head_sum(kk * kk)), 1e-12)
    k2 = kraw * (1.0 + (a - 1.0) * k_a)
    return log_decay, k2, -kkn, kkn * a, g


def _f_post(acts, params, proxies):
    o, r, k2, v, g, x = acts
    lnx_g, lnx_b, r_k, wo, ln_g, ln_b = params
    om = head_sum(o) * (1.0 / HEAD)
    oc = o - om
    ov = head_sum(oc * oc) * (1.0 / HEAD)
    on = oc * lax.rsqrt(ov + GN_EPS) * lnx_g + lnx_b
    bonus = head_sum(r * k2 * r_k) * v
    y = mm((on + bonus) * g, wo, proxies[3])
    return (_ln(DN_ALPHA * x + y, ln_g, ln_b),)


def _f_glu(acts, params, proxies):
    ys, h = acts
    d, wv0, wv1, wg0, wg1, ln_g, ln_b = params
    y = jax.nn.gelu(ys + h * d)
    mix = jnp.concatenate([mm(y, wv0, proxies[1]) * jax.nn.sigmoid(mm(y, wg0, proxies[3])),
                           mm(y, wv1, proxies[2]) * jax.nn.sigmoid(mm(y, wg1, proxies[4]))], axis=1)
    return (_ln(DN_ALPHA * h + mix, ln_g, ln_b),)


def _f_zoh(a_re, a_im, log_dt, b_re_t, b_im_t):
    dt = jnp.exp(log_dt)
    lam_re = jnp.minimum(a_re, -1e-4)
    lam_im = a_im
    mag = jnp.exp(dt * lam_re)
    abar_re = mag * jnp.cos(dt * lam_im)
    abar_im = mag * jnp.sin(dt * lam_im)
    den = lam_re * lam_re + lam_im * lam_im
    nr, ni = abar_re - 1.0, abar_im
    coef_re = ((nr * lam_re + ni * lam_im) / den)[:, None, :]
    coef_im = ((ni * lam_re - nr * lam_im) / den)[:, None, :]
    return (abar_re, abar_im, coef_re * b_re_t - coef_im * b_im_t, coef_re * b_im_t + coef_im * b_re_t)


def _bdot16_raw(a, b, ca, cb):
    return lax.dot_general(a.astype(BF16), b.astype(BF16), (((ca,), (cb,)), ((0,), (0,))), preferred_element_type=F32)


@functools.partial(jax.custom_vjp, nondiff_argnums=(2, 3))
def _bdot16(a, b, ca, cb):
    return _bdot16_raw(a, b, ca, cb)


def _bdot16_bwd(ca, cb, res, g):
    a, b = res
    if (ca, cb) == (2, 1):
        return _bdot16_raw(g, b, 2, 2), _bdot16_raw(a, g, 1, 1)
    if (ca, cb) == (2, 2):
        return _bdot16_raw(g, b, 2, 1), _bdot16_raw(g, a, 1, 1)
    assert (ca, cb) == (1, 1)
    return _bdot16_raw(b, g, 2, 2), _bdot16_raw(a, g, 2, 1)


_bdot16.defvjp(lambda a, b, ca, cb: (_bdot16_raw(a, b, ca, cb), (a, b)), _bdot16_bwd)

def _time_sums(x, suffix):
    hg, ln, _ = x.shape
    row = lax.broadcasted_iota(jnp.int32, (hg, ln, ln), 1)
    col = lax.broadcasted_iota(jnp.int32, (hg, ln, ln), 2)
    tri = ((row <= col) if suffix else (row >= col)).astype(BF16)
    return sum(lax.dot_general(tri, p, (((2,), (1,)), ((0,), (0,))), preferred_element_type=F32) for p in _split3(x))


@jax.custom_vjp
def _time_cumsum(x):
    return _time_sums(x, False)


_time_cumsum.defvjp(lambda x: (_time_sums(x, False), None), lambda _, g: (_time_sums(g, True),))

_dot_score = _bdot16
_dot_inverse = _bdot16
_dot_value = _bdot16


def _rec_chunk(s0, r, lw, k, v, a, b):
    hg, ln, _ = r.shape
    row = lax.broadcasted_iota(jnp.int32, (hg, ln, ln), 1)
    col = lax.broadcasted_iota(jnp.int32, (hg, ln, ln), 2)
    incl, strict = row >= col, row > col
    cum = _time_cumsum(lw)
    total = jnp.sum(lw, axis=1, keepdims=True)
    e_cum, e_inv, e_prev, e_tail = jnp.exp(cum), jnp.exp(-cum), jnp.exp(cum - lw), jnp.exp(total - cum)
    rt, at, bt, kt = r * e_cum, a * e_prev, b * e_inv, k * e_inv
    aab = jnp.where(strict, _dot_score(at, bt, 2, 2), 0.0)
    aak = jnp.where(strict, _dot_score(at, kt, 2, 2), 0.0)
    arb = jnp.where(incl, _dot_score(rt, bt, 2, 2), 0.0)
    ark = jnp.where(incl, _dot_score(rt, kt, 2, 2), 0.0)
    p = (row == col).astype(F32) + aab
    m = aab
    for _ in range(int(math.log2(ln)) - 1):
        m = _dot_inverse(m, m, 2, 1)
        p = p + _dot_inverse(p, m, 2, 1)
    u = _dot_inverse(p, _dot_value(at, s0, 2, 2) + _dot_value(aak, v, 2, 1), 2, 1)
    o = _dot_value(rt, s0, 2, 2) + _dot_value(arb, u, 2, 1) + _dot_value(ark, v, 2, 1)
    s1 = s0 * jnp.exp(total) + _dot_value(u, b * e_tail, 1, 1) + _dot_value(v, k * e_tail, 1, 1)
    return o, s1


def _full_spec(shape):
    nd = len(shape)
    return pl.BlockSpec(tuple(shape), lambda *_: (0,) * nd)


def _stage_fwd(name, f, acts, params, out_dims, tb):
    t = acts[0].shape[0]
    na, npar = len(acts), len(params)

    def body(*refs):
        outs = f(tuple(r[...] for r in refs[:na]), tuple(r[...] for r in refs[na:na + npar]), (None,) * npar)
        for r, val in zip(refs[na + npar:], outs):
            r[...] = val

    return pl.pallas_call(
        body, name=name, grid=(t // tb,),
        in_specs=[pl.BlockSpec((tb, a.shape[1]), lambda i: (i, 0)) for a in acts] + [_full_spec(p.shape) for p in params],
        out_specs=[pl.BlockSpec((tb, d), lambda i: (i, 0)) for d in out_dims],
        out_shape=[_sds((t, d)) for d in out_dims],
        compiler_params=_cparams(("arbitrary",)),
    )(*acts, *params)


def _stage_bwd(name, f, acts, params, couts, tb, proxied=()):
    t = acts[0].shape[0]
    groups = [c if isinstance(c, tuple) else (c,) for c in couts]
    couts = [term for grp in groups for term in grp]
    na, npar, nc = len(acts), len(params), len(couts)
    steps = t // tb

    def f_diff(act_vals, diff_vals, param_vals):
        real = tuple(param_vals[i] if i in proxied else diff_vals[i] for i in range(npar))
        proxies = tuple(diff_vals[i] if i in proxied else None for i in range(npar))
        return f(act_vals, real, proxies)

    def body(*refs):
        a_refs, p_hbm, c_refs = refs[:na], refs[na:na + npar], refs[na + npar:na + npar + nc]
        o = na + npar + nc
        da_refs, dp_hbm = refs[o:o + na], refs[o + na:o + na + npar]
        p_buf, acc = refs[o + na + npar:o + na + 2 * npar], refs[o + na + 2 * npar:]
        i = pl.program_id(0)

        @pl.when(i == 0)
        def _():
            for src, dst in zip(p_hbm, p_buf):
                pltpu.sync_copy(src, dst)
            for r in acc:
                r[...] = jnp.zeros_like(r)

        param_vals = tuple(r[...] for r in p_buf)
        diff_vals = tuple(jnp.zeros(v.shape, F32) if i in proxied else v for i, v in enumerate(param_vals))
        _, vjp = jax.vjp(functools.partial(f_diff, param_vals=param_vals), tuple(r[...] for r in a_refs), diff_vals)
        terms = iter(c_refs)
        d_acts, d_params = vjp(tuple(functools.reduce(jnp.add, [next(terms)[...] for _ in grp]) for grp in groups))
        for r, val in zip(da_refs, d_acts):
            r[...] = val
        for r, val in zip(acc, d_params):
            r[...] += val

        @pl.when(i == steps - 1)
        def _():
            for src, dst in zip(acc, dp_hbm):
                pltpu.sync_copy(src, dst)

    hbm = pl.BlockSpec(memory_space=pltpu.HBM)
    outs = pl.pallas_call(
        body, name=name, grid=(steps,),
        in_specs=[pl.BlockSpec((tb, a.shape[1]), lambda i: (i, 0)) for a in acts] + [hbm] * npar
        + [pl.BlockSpec((tb, c.shape[1]), lambda i: (i, 0)) for c in couts],
        out_specs=[pl.BlockSpec((tb, a.shape[1]), lambda i: (i, 0)) for a in acts] + [hbm] * npar,
        out_shape=[_sds(a.shape) for a in acts] + [_sds(p.shape) for p in params],
        scratch_shapes=[pltpu.VMEM(p.shape, p.dtype) for p in params] + [pltpu.VMEM(p.shape, F32) for p in params],
        compiler_params=_cparams(("arbitrary",)),
    )(*acts, *params, *couts)
    return outs[:na], outs[na:]


def _tiled_matmul(name, a, b, mode, grid, a_spec, b_spec, o_spec, out_shape):
    nk = grid[2]
    dims = {"nn": ((1,), (0,)), "nt": ((1,), (1,)), "tn": ((0,), (0,))}[mode]

    def body(a_ref, b_ref, o_ref, acc):
        kk = pl.program_id(2)

        @pl.when(kk == 0)
        def _():
            acc[...] = jnp.zeros_like(acc)

        acc[...] += lax.dot_general(a_ref[...].astype(BF16), b_ref[...].astype(BF16), (dims, ((), ())),
                                    preferred_element_type=F32)

        @pl.when(kk == nk - 1)
        def _():
            o_ref[...] = acc[...]

    return pl.pallas_call(
        body, name=name, grid=grid, in_specs=[a_spec, b_spec], out_specs=o_spec, out_shape=_sds(out_shape),
        scratch_shapes=[pltpu.VMEM(o_spec.block_shape, F32)],
        compiler_params=_cparams(("parallel", "parallel", "arbitrary")),
    )(a, b)


def _mlp_weight_grad(name, a, b, layer, layers, split, into=None, tile=512):
    t, m = a.shape
    n = b.shape[1]
    tk = min(tile, t)
    tile = 2 * tile
    if split == "n":
        tm, tn = min(tile, m), min(tile, n // 4)
        per = n // 4 // tn
        shape = (4, layers, m, n // 4)
        o_idx = lambda i, j, k: (j // per, layer, i, j % per)
    else:
        tm, tn = min(tile, m // 4), min(tile, n)
        per = m // 4 // tm
        shape = (4, layers, m // 4, n)
        o_idx = lambda i, j, k: (i // per, layer, i % per, j)
    nk = t // tk

    def body(a_ref, b_ref, *rest):
        o_ref, acc = rest[-2:]
        kk = pl.program_id(2)

        @pl.when(kk == 0)
        def _():
            acc[...] = jnp.zeros_like(acc)

        acc[...] += lax.dot_general(a_ref[...].astype(BF16), b_ref[...].astype(BF16), (((0,), (0,)), ((), ())),
                                    preferred_element_type=F32)

        @pl.when(kk == nk - 1)
        def _():
            o_ref[...] = acc[...]

    in_specs = [pl.BlockSpec((tk, tm), lambda i, j, k: (k, i)), pl.BlockSpec((tk, tn), lambda i, j, k: (k, j))]
    operands = [a, b]
    aliases = {}
    if into is not None:
        in_specs.append(pl.BlockSpec(memory_space=pl.ANY))
        operands.append(into)
        aliases = {2: 0}
    return pl.pallas_call(
        body, name=name, grid=(m // tm, n // tn, nk), in_specs=in_specs,
        out_specs=pl.BlockSpec((None, None, tm, tn), o_idx), out_shape=_sds(shape), input_output_aliases=aliases,
        scratch_shapes=[pltpu.VMEM((tm, tn), F32)],
        compiler_params=_cparams(("parallel", "parallel", "arbitrary")),
    )(*operands)


S5_PACK = 8


def _s5_wide_from_narrow(name, x, wc, mode, tm):
    t, c = x.shape
    kb, nb = S5_PACK * SSM_GROUP, S5_PACK * SSM_STATE
    nsb = c // kb
    wide = 2 * nsb * nb
    b_spec = (pl.BlockSpec((kb, nb), lambda i, j, k: (0, j)) if mode == "nn" else pl.BlockSpec((nb, kb), lambda i, j, k: (j, 0)))
    return _tiled_matmul(name, x, wc, mode, (t // tm, wide // nb, 1), pl.BlockSpec((tm, kb), lambda i, j, k: (i, j % nsb)),
                         b_spec, pl.BlockSpec((tm, nb), lambda i, j, k: (i, j)), (t, wide))


def _s5_narrow_from_wide(name, s, wc, mode, tm):
    t, wide = s.shape
    kb, nb = S5_PACK * SSM_GROUP, S5_PACK * SSM_STATE
    nsb = wide // (2 * nb)
    b_spec = (pl.BlockSpec((nb, kb), lambda i, j, k: (k * nsb + j, 0)) if mode == "nn"
              else pl.BlockSpec((kb, nb), lambda i, j, k: (0, k * nsb + j)))
    return _tiled_matmul(name, s, wc, mode, (t // tm, nsb, 2), pl.BlockSpec((tm, nb), lambda i, j, k: (i, k * nsb + j)),
                         b_spec, pl.BlockSpec((tm, kb), lambda i, j, k: (i, j)), (t, nsb * kb))


def _s5_weight_grad(name, x, s, wide_rows, tk):
    t, c = x.shape
    wide = s.shape[1]
    kb, nb = S5_PACK * SSM_GROUP, S5_PACK * SSM_STATE
    nsb = c // kb
    x_spec = pl.BlockSpec((tk, kb), lambda i, j, k: (k, j % nsb))
    s_spec = pl.BlockSpec((tk, nb), lambda i, j, k: (k, j))
    if wide_rows:
        return _tiled_matmul(name, s, x, "tn", (1, wide // nb, t // tk), s_spec, x_spec,
                             pl.BlockSpec((nb, kb), lambda i, j, k: (j, 0)), (wide, kb))
    return _tiled_matmul(name, x, s, "tn", (1, wide // nb, t // tk), x_spec, s_spec,
                         pl.BlockSpec((kb, nb), lambda i, j, k: (0, j)), (kb, wide))


def _mlp_fwd(name, h, w1, w2, layer, ln_g, ln_b, tb):
    t, c = h.shape
    nj, fc = w1.shape[0], w1.shape[3]

    def body(h_ref, w1_ref, w2_ref, g_ref, b_ref, out_ref, s_ref, acc):
        j = pl.program_id(1)

        @pl.when(j == 0)
        def _():
            acc[...] = jnp.zeros_like(acc)

        hid = jnp.dot(h_ref[...].astype(BF16), w1_ref[...].astype(BF16), preferred_element_type=F32)
        act = jnp.square(jnp.maximum(hid, 0.0))
        acc[...] += jnp.dot(act.astype(BF16), w2_ref[...].astype(BF16), preferred_element_type=F32)

        @pl.when(j == nj - 1)
        def _():
            s = DN_ALPHA * h_ref[...] + acc[...]
            s_ref[...] = s
            out_ref[...] = _ln(s, g_ref[...], b_ref[...])

    row = pl.BlockSpec((tb, c), lambda i, j: (i, 0))
    vec = pl.BlockSpec((1, c), lambda i, j: (0, 0))
    return pl.pallas_call(
        body, name=name, grid=(t // tb, nj),
        in_specs=[row, pl.BlockSpec((None, None, c, fc), lambda i, j: (j, layer, 0, 0)),
                  pl.BlockSpec((None, None, fc, c), lambda i, j: (j, layer, 0, 0)), vec, vec],
        out_specs=[row, row], out_shape=[_sds((t, c)), _sds((t, c))],
        scratch_shapes=[pltpu.VMEM((tb, c), F32)],
        compiler_params=_cparams(("parallel", "arbitrary")),
    )(h, w1, w2, ln_g, ln_b)


def _mlp_bwd(name, h, s, dout, w1, w2, layer, ln_g, ln_b, tb):
    t, c = h.shape
    nj, fc = w1.shape[0], w1.shape[3]
    ff = nj * fc
    ni = t // tb
    nt = (((1,), (1,)), ((), ()))
    douts = dout if isinstance(dout, tuple) else (dout,)
    nd = len(douts)

    def body(h_ref, s_ref, *rest):
        dout_refs = rest[:nd]
        (w1_ref, w2_ref, g_ref, b_ref, dh_ref, ds_ref, dhid_ref, act_ref, dg_ref, db_ref,
         ds_scr, dh_acc, dg_acc, db_acc) = rest[nd:]
        i, j = pl.program_id(0), pl.program_id(1)

        @pl.when((i == 0) & (j == 0))
        def _():
            dg_acc[...] = jnp.zeros_like(dg_acc)
            db_acc[...] = jnp.zeros_like(db_acc)

        @pl.when(j == 0)
        def _():
            _, vjp = jax.vjp(_ln, s_ref[...], g_ref[...], b_ref[...])
            ds, dg, db = vjp(functools.reduce(jnp.add, [r[...] for r in dout_refs]))
            ds_scr[...] = ds
            ds_ref[...] = ds.astype(BF16)
            dh_acc[...] = DN_ALPHA * ds
            dg_acc[...] += dg
            db_acc[...] += db

        w1b, w2b = w1_ref[...].astype(BF16), w2_ref[...].astype(BF16)
        hid = jnp.dot(h_ref[...].astype(BF16), w1b, preferred_element_type=F32)
        rl = jnp.maximum(hid, 0.0)
        dact = lax.dot_general(ds_scr[...].astype(BF16), w2b, nt, preferred_element_type=F32)
        dhid = (dact * 2.0 * rl).astype(BF16)
        dh_acc[...] += lax.dot_general(dhid, w1b, nt, preferred_element_type=F32)
        dhid_ref[...] = dhid
        act_ref[...] = (rl * rl).astype(BF16)

        @pl.when(j == nj - 1)
        def _():
            dh_ref[...] = dh_acc[...]

        @pl.when((i == ni - 1) & (j == nj - 1))
        def _():
            dg_ref[...] = dg_acc[...]
            db_ref[...] = db_acc[...]

    row = pl.BlockSpec((tb, c), lambda i, j: (i, 0))
    vec = pl.BlockSpec((1, c), lambda i, j: (0, 0))
    wide = pl.BlockSpec((tb, fc), lambda i, j: (i, j))
    return pl.pallas_call(
        body, name=name, grid=(ni, nj),
        in_specs=[row, row] + [row] * nd + [pl.BlockSpec((None, None, c, fc), lambda i, j: (j, layer, 0, 0)),
                                            pl.BlockSpec((None, None, fc, c), lambda i, j: (j, layer, 0, 0)), vec, vec],
        out_specs=[row, row, wide, wide, vec, vec],
        out_shape=[_sds((t, c)), _sds((t, c), BF16), _sds((t, ff), BF16), _sds((t, ff), BF16), _sds((1, c)), _sds((1, c))],
        scratch_shapes=[pltpu.VMEM((tb, c), F32), pltpu.VMEM((tb, c), F32), pltpu.VMEM((1, c), F32), pltpu.VMEM((1, c), F32)],
        compiler_params=_cparams(("arbitrary", "arbitrary")),
    )(h, s, *douts, w1, w2, ln_g, ln_b)


def _load_heads(ref, hg):
    return jnp.stack([ref[:, h * HEAD:(h + 1) * HEAD] for h in range(hg)])


def _store_heads(ref, val):
    for h in range(val.shape[0]):
        ref[:, h * HEAD:(h + 1) * HEAD] = val[h]


def _rec_fwd(r, lw, k, v, a, b, hg, shards):
    t, c = r.shape
    n = HEAD
    nh = c // n
    ln = REC_CHUNK
    nck = t // ln
    ngrp = nh // hg
    nsh = len(shards)
    steps = ngrp * nck

    def body(r_ref, lw_ref, k_ref, v_ref, a_ref, b_ref, *rest):
        src, (o_ref, s0_ref), dst = rest[:nsh], rest[nsh:nsh + 2], rest[nsh + 2:2 * nsh + 2]
        state, sems = rest[2 * nsh + 2], rest[2 * nsh + 3:]
        step = pl.program_id(0) * nck + pl.program_id(1)
        start, forward, finish = _gather_big_phases(src, dst, sems)
        pl.when(step == 0)(start)

        @pl.when(pl.program_id(1) == 0)
        def _():
            state[...] = jnp.zeros_like(state)

        s0 = state[...]
        s0_ref[...] = s0
        o, s1 = _rec_chunk(s0, *(_load_heads(x, hg) for x in (r_ref, lw_ref, k_ref, v_ref, a_ref, b_ref)))
        _store_heads(o_ref, o)
        state[...] = s1
        pl.when(step == steps // 2)(forward)
        pl.when(step == steps - 1)(finish)

    seq = pl.BlockSpec((ln, hg * n), lambda g, i: (i, g))
    outs = pl.pallas_call(
        body, name="rec_fwd", grid=(ngrp, nck), in_specs=[seq] * 6 + [_HBM] * nsh,
        out_specs=[seq, pl.BlockSpec((None, hg, n, n), lambda g, i: (i, g, 0, 0))] + [_HBM] * nsh,
        out_shape=[_sds((t, c)), _sds((nck, nh, n, n))] + [_sds((4,) + s.shape, s.dtype) for s in shards],
        scratch_shapes=[pltpu.VMEM((hg, n, n), F32)] + _gather_big_sems(nsh),
        compiler_params=_cparams(("arbitrary", "arbitrary")),
    )(r, lw, k, v, a, b, *shards)
    return outs[0], outs[1], outs[2:]


def _rec_bwd(r, lw, k, v, a, b, s0s, do, hg, chip_sums):
    t, c = r.shape
    n = HEAD
    nh = c // n
    ln = REC_CHUNK
    nck = t // ln
    ngrp = nh // hg
    nsum = len(chip_sums)
    steps = ngrp * nck

    def body(r_ref, lw_ref, k_ref, v_ref, a_ref, b_ref, s0_ref, do_ref, *rest):
        src, grad_refs, land = rest[:nsum], rest[nsum:nsum + 6], rest[nsum + 6:2 * nsum + 6]
        dstate, sems = rest[2 * nsum + 6], rest[2 * nsum + 7:]
        step = pl.program_id(0) * nck + pl.program_id(1)
        start, finish = _scatter_big_phases(src, land, sems)
        pl.when(step == 0)(start)

        @pl.when(pl.program_id(1) == 0)
        def _():
            dstate[...] = jnp.zeros_like(dstate)

        _, vjp = jax.vjp(_rec_chunk, s0_ref[...], *(_load_heads(x, hg) for x in (r_ref, lw_ref, k_ref, v_ref, a_ref, b_ref)))
        ds0, *grads = vjp((_load_heads(do_ref, hg), dstate[...]))
        dstate[...] = ds0
        for ref, val in zip(grad_refs, grads):
            _store_heads(ref, val)
        pl.when(step == steps - 1)(finish)

    seq = pl.BlockSpec((ln, hg * n), lambda g, i: (nck - 1 - i, g))
    outs = pl.pallas_call(
        body, name="rec_bwd", grid=(ngrp, nck),
        in_specs=[seq] * 6 + [pl.BlockSpec((None, hg, n, n), lambda g, i: (nck - 1 - i, g, 0, 0)), seq] + [_HBM] * nsum,
        out_specs=[seq] * 6 + [_HBM] * nsum,
        out_shape=[_sds((t, c))] * 6 + [_sds((3,) + s.shape[1:], s.dtype) for s in chip_sums],
        scratch_shapes=[pltpu.VMEM((hg, n, n), F32)] + _scatter_big_sems(nsum),
        compiler_params=_cparams(("arbitrary", "arbitrary")),
    )(r, lw, k, v, a, b, s0s, do, *chip_sums)
    return outs[:6], outs[6:]


def _scan_fwd(bu, abar, tb):
    t, w2 = bu.shape
    w = w2 // 2

    def body(bu_ref, a_ref, s_ref, h_scr, rows):
        @pl.when(pl.program_id(0) == 0)
        def _():
            h_scr[...] = jnp.zeros_like(h_scr)

        ar, ai = a_ref[:, :w], a_ref[:, w:]

        def step(i, carry):
            hr, hi = carry
            nr = ar * hr - ai * hi + bu_ref[pl.ds(i, 1), :w]
            ni = ar * hi + ai * hr + bu_ref[pl.ds(i, 1), w:]
            rows[pl.ds(i, 1), :w] = nr
            rows[pl.ds(i, 1), w:] = ni
            return nr, ni

        hr, hi = lax.fori_loop(0, tb, step, (h_scr[:, :w], h_scr[:, w:]))
        h_scr[:, :w] = hr
        h_scr[:, w:] = hi
        s_ref[...] = rows[...].astype(BF16)

    return pl.pallas_call(
        body, name="s5_scan_fwd", grid=(t // tb,),
        in_specs=[pl.BlockSpec((tb, w2), lambda i: (i, 0)), pl.BlockSpec((1, w2), lambda i: (0, 0))],
        out_specs=pl.BlockSpec((tb, w2), lambda i: (i, 0)), out_shape=_sds((t, w2), BF16),
        scratch_shapes=[pltpu.VMEM((1, w2), F32), pltpu.VMEM((tb, w2), F32)],
        compiler_params=_cparams(("arbitrary",)),
    )(bu, abar)


def _scan_bwd(ds, s, abar, tb):
    t, w2 = ds.shape
    w = w2 // 2
    nb = t // tb
    pack = 16
    per = tb // pack

    def body(ds_ref, s_ref, sprev_ref, a_ref, dbu_ref, da_ref, g_scr, da_acc, rows):
        i = pl.program_id(0)

        @pl.when(i == 0)
        def _():
            g_scr[...] = jnp.zeros_like(g_scr)
            da_acc[...] = jnp.zeros_like(da_acc)

        ar, ai = a_ref[:, :w], a_ref[:, w:]

        def step(n, carry):
            gr, gi = carry
            row = tb - 1 - n
            nr = ds_ref[pl.ds(row, 1), :w] + ar * gr + ai * gi
            ni = ds_ref[pl.ds(row, 1), w:] + ar * gi - ai * gr
            rows[pl.ds(row, 1), :w] = nr
            rows[pl.ds(row, 1), w:] = ni
            return nr, ni

        gr, gi = lax.fori_loop(0, tb, step, (g_scr[:, :w], g_scr[:, w:]))
        g_scr[:, :w] = gr
        g_scr[:, w:] = gi
        last = (lax.broadcasted_iota(jnp.int32, (pack, w2), 0) == pack - 1) & (i < nb - 1)
        before = jnp.sum(jnp.where(last, sprev_ref[...].astype(F32), 0.0), axis=0, keepdims=True)
        rid = lax.broadcasted_iota(jnp.int32, (tb, w2), 0)
        sp = jnp.where(rid == 0, before, pltpu.roll(s_ref[...].astype(F32), 1, 0))
        g = rows[...]
        dbu_ref[...] = g.astype(BF16)
        spr, spi, g_r, g_i = sp[:, :w], sp[:, w:], g[:, :w], g[:, w:]
        da_acc[:, :w] += jnp.sum(spr * g_r + spi * g_i, axis=0, keepdims=True)
        da_acc[:, w:] += jnp.sum(spr * g_i - spi * g_r, axis=0, keepdims=True)

        @pl.when(i == nb - 1)
        def _():
            da_ref[...] = da_acc[...]

    blk = pl.BlockSpec((tb, w2), lambda i: (nb - 1 - i, 0))
    prev = pl.BlockSpec((pack, w2), lambda i: (jnp.maximum((nb - 1 - i) * per - 1, 0), 0))
    return pl.pallas_call(
        body, name="s5_scan_bwd", grid=(nb,),
        in_specs=[blk, blk, prev, pl.BlockSpec((1, w2), lambda i: (0, 0))],
        out_specs=[blk, pl.BlockSpec((1, w2), lambda i: (0, 0))],
        out_shape=[_sds((t, w2), BF16), _sds((1, w2))],
        scratch_shapes=[pltpu.VMEM((1, w2), F32), pltpu.VMEM((1, w2), F32), pltpu.VMEM((tb, w2), F32)],
        compiler_params=_cparams(("arbitrary",)),
    )(ds, s, s, abar)


def _zoh_fwd(a_re, a_im, log_dt, b_re_t, b_im_t):
    def body(*refs):
        for r, val in zip(refs[5:], _f_zoh(*(x[...] for x in refs[:5]))):
            r[...] = val

    return pl.pallas_call(body, name="s5_zoh_fwd", out_shape=[_sds(a_re.shape)] * 2 + [_sds(b_re_t.shape)] * 2,
                          compiler_params=_cparams())(a_re, a_im, log_dt, b_re_t, b_im_t)


def _zoh_bwd(a_re, a_im, log_dt, b_re_t, b_im_t, couts):
    def body(*refs):
        _, vjp = jax.vjp(_f_zoh, *(x[...] for x in refs[:5]))
        for r, val in zip(refs[9:], vjp(tuple(x[...] for x in refs[5:9]))):
            r[...] = val

    ins = (a_re, a_im, log_dt, b_re_t, b_im_t)
    return pl.pallas_call(body, name="s5_zoh_bwd", out_shape=[_sds(x.shape) for x in ins],
                          compiler_params=_cparams())(*ins, *couts)


def _loss_head(h, target, tb):
    t, c = h.shape
    nb = t // tb

    def body(h_ref, t_ref, loss_ref, dh_ref, acc):
        i = pl.program_id(0)

        @pl.when(i == 0)
        def _():
            acc[...] = jnp.zeros_like(acc)

        d = h_ref[...] - t_ref[...]
        dh_ref[...] = d * (1.0 / c)
        acc[...] += 0.5 * jnp.sum(jnp.mean(d * d, axis=-1, keepdims=True), axis=0, keepdims=True)

        @pl.when(i == nb - 1)
        def _():
            loss_ref[...] = jnp.broadcast_to(acc[...], loss_ref.shape)

    row = pl.BlockSpec((tb, c), lambda i: (i, 0))
    return pl.pallas_call(
        body, name="loss_head", grid=(nb,), in_specs=[row, row],
        out_specs=[pl.BlockSpec((8, 128), lambda i: (0, 0)), row], out_shape=[_sds((8, 128)), _sds((t, c))],
        scratch_shapes=[pltpu.VMEM((1, 1), F32)], compiler_params=_cparams(("arbitrary",)),
    )(h, target)


def _rows_tile(rows):
    for cand in (512, 256, 128, 64, 32, 16, 8):
        if rows % cand == 0:
            return cand
    return rows


def _addn(name, arrs):
    rows, cols = arrs[0].shape
    tb = _rows_tile(rows)

    def body(*refs):
        acc = refs[0][...]
        for r in refs[1:-1]:
            acc = acc + r[...]
        refs[-1][...] = acc

    blk = pl.BlockSpec((tb, cols), lambda i: (i, 0))
    return pl.pallas_call(body, name=name, grid=(rows // tb,), in_specs=[blk] * len(arrs), out_specs=blk,
                          out_shape=_sds((rows, cols)), compiler_params=_cparams(("parallel",)))(*arrs)


def _adamw_math(w, g, m, v):
    m = ADAM_B1 * m + (1.0 - ADAM_B1) * g
    v = ADAM_B2 * v + (1.0 - ADAM_B2) * jnp.square(g)
    m_hat = m / (1.0 - ADAM_B1 ** ADAM_STEP)
    v_hat = v / (1.0 - ADAM_B2 ** ADAM_STEP)
    delta = -ADAM_LR * (m_hat / (jnp.sqrt(v_hat) + ADAM_EPS) + ADAM_WD * w)
    return delta, m, v


def _adamw(name, parts, w, m, v):
    rows, cols = w.shape
    tb = _rows_tile(rows)
    npart = len(parts)

    def body(*refs):
        g = refs[0][...]
        for r in refs[1:npart]:
            g = g + r[...]
        w_ref, m_ref, v_ref = refs[npart:npart + 3]
        g_out, d_out, m_out, v_out = refs[npart + 3:]
        delta, mn, vn = _adamw_math(w_ref[...], g, m_ref[...], v_ref[...])
        g_out[...] = g
        d_out[...] = delta
        m_out[...] = mn
        v_out[...] = vn

    blk = pl.BlockSpec((tb, cols), lambda i: (i, 0))
    return pl.pallas_call(body, name=name, grid=(rows // tb,), in_specs=[blk] * (npart + 3), out_specs=[blk] * 4,
                          out_shape=[_sds((rows, cols))] * 4, compiler_params=_cparams(("parallel",)))(*parts, w, m, v)


def _shift_down(a):
    return jnp.concatenate([jnp.zeros_like(a[:1]), a[:-1]], axis=0)


def _shift_up(a):
    return jnp.concatenate([a[1:], jnp.zeros_like(a[:1])], axis=0)


def _s5_pack_mask(g):
    return (jnp.arange(g)[None, :] % S5_PACK == jnp.arange(S5_PACK)[:, None]).astype(F32)


def _compact_b(bbar_t):
    g, s, p = bbar_t.shape
    return (_s5_pack_mask(g)[:, None, :, None] * bbar_t.transpose(1, 0, 2)[None]).reshape(S5_PACK * s, g * p)


def _compact_b_t(dense, g):
    s, p = dense.shape[0] // S5_PACK, dense.shape[1] // g
    return jnp.sum(dense.reshape(S5_PACK, s, g, p) * _s5_pack_mask(g)[:, None, :, None], axis=0).transpose(1, 0, 2)


def _compact_c(c_w):
    g, s, p = c_w.shape
    return (c_w.transpose(0, 2, 1)[:, :, None, :] * _s5_pack_mask(g).T[:, None, :, None]).reshape(g * p, S5_PACK * s)


def _compact_c_t(dense, g):
    p, s = dense.shape[0] // g, dense.shape[1] // S5_PACK
    return jnp.sum(dense.reshape(g, p, S5_PACK, s) * _s5_pack_mask(g).T[:, None, :, None], axis=2).transpose(0, 2, 1)


def _local_step(x, target, fw, core):
    t, c = x.shape
    nh = c // HEAD
    ng = c // SSM_GROUP
    tb = min(256, t)
    tbb = min(128, t)
    tbm = min(512, t)
    tbmb = min(512, t)
    tbs = min(64, t)
    tb5 = min(1024, t)
    tk5 = min(2048, t)
    hg = min(16, nh)
    mu = [fw['rw_mu'][i:i + 1] for i in range(6)]
    ln_g = [fw['ln_g'][i:i + 1] for i in range(4)]
    ln_b = [fw['ln_b'][i:i + 1] for i in range(4)]
    grads = {}

    xp = _shift_down(x)
    proj_params = {n: (mu[i], fw['rw_w' + n]) for n, i in (('r', 0), ('k', 2), ('v', 3))}
    raw = {n: _stage_fwd("proj_" + n, _f_proj, (x, xp), proj_params[n], (c,), tb)[0] for n in 'rkv'}
    lora_params = (mu[1], mu[4], mu[5], fw['rw_w0'], fw['rw_w1'], fw['rw_w2'], fw['rw_a0'], fw['rw_a1'], fw['rw_a2'],
                   fw['rw_g1'], fw['rw_g2'], fw['rw_k_k'], fw['rw_k_a'])
    lw, k2, an, bb, gate = _stage_fwd("lora", _f_lora, (x, xp, raw['k']), lora_params, (c,) * 5, tb)
    rec_in = (raw['r'], lw, k2, raw['v'], an, bb)
    o, s0s, late = _rec_fwd(*rec_in, hg, fw['late_shards'])
    late = dict(zip(BIG_LATE, late))
    fw = dict(fw, rw_wo=late['rw_wo'].reshape(c, c), s5_w_glu=tuple(late['s5_w_glu'][q] for q in range(4)),
              mlp_w1=late['mlp_w1'].reshape(4, DEPTH, c, -1), mlp_w2=late['mlp_w2'].reshape(4, DEPTH, -1, c))
    post_params = (fw['rw_lnx_g'], fw['rw_lnx_b'], fw['rw_r_k'], fw['rw_wo'], ln_g[0], ln_b[0])
    post_acts = (o, raw['r'], k2, raw['v'], gate, x)
    h1, = _stage_fwd("post", _f_post, post_acts, post_params, (c,), tb)
    h2, s_mlp0 = _mlp_fwd("mlp0_fwd", h1, fw['mlp_w1'], fw['mlp_w2'], 0, ln_g[1], ln_b[1], tbm)

    a_re, a_im, log_dt = fw['s5_a_re'], fw['s5_a_im'], fw['s5_log_dt']
    b_re_t, b_im_t = fw['s5_b_re'].transpose(0, 2, 1), fw['s5_b_im'].transpose(0, 2, 1)
    abar_re, abar_im, bbar_re_t, bbar_im_t = _zoh_fwd(a_re, a_im, log_dt, b_re_t, b_im_t)
    abar = jnp.concatenate([abar_re.reshape(1, -1), abar_im.reshape(1, -1)], axis=1)
    bc = jnp.concatenate([_compact_b(bbar_re_t), _compact_b(bbar_im_t)], axis=1).astype(BF16)
    cc = jnp.concatenate([_compact_c(fw['s5_c_re']), -_compact_c(fw['s5_c_im'])], axis=0).astype(BF16)
    bu = _s5_wide_from_narrow("s5_bu", h2, bc, "nn", tb5)
    st = _scan_fwd(bu, abar, tbs)
    ys = _s5_narrow_from_wide("s5_y", st, cc, "nn", tb5)
    glu_params = (fw['s5_d'], *fw['s5_w_glu'], ln_g[2], ln_b[2])
    h3, = _stage_fwd("glu", _f_glu, (ys, h2), glu_params, (c,), tb)
    h4, s_mlp1 = _mlp_fwd("mlp1_fwd", h3, fw['mlp_w1'], fw['mlp_w2'], 1, ln_g[3], ln_b[3], tbm)

    loss_blk, dh4 = _loss_head(h4, target, tb)

    dln_g, dln_b = [None] * 4, [None] * 4
    dh3, ds1, dhid1, act1, dln_g[3], dln_b[3] = _mlp_bwd("mlp1_bwd", h3, s_mlp1, dh4, fw['mlp_w1'], fw['mlp_w2'], 1,
                                                         ln_g[3], ln_b[3], tbmb)
    dw1 = _mlp_weight_grad("mlp1_dw1", h3, dhid1, 1, DEPTH, "n")
    dw2 = _mlp_weight_grad("mlp1_dw2", act1, ds1, 1, DEPTH, "m")
    (dys, dh2_glu), (grads['s5_d'], *dglu, dln_g[2], dln_b[2]) = _stage_bwd(
        "glu_bwd", _f_glu, (ys, h2), glu_params, (dh3,), tbb, proxied=(1, 2, 3, 4))
    grads['s5_w_glu'] = jnp.stack(dglu)
    dst = _s5_wide_from_narrow("s5_dst", dys, cc, "nt", tb5)
    dcc = _s5_weight_grad("s5_dcc", dys, st, True, tk5)
    dbu, dabar = _scan_bwd(dst, st, abar, tbs)
    dh2_bu = _s5_narrow_from_wide("s5_dh", dbu, bc, "nt", tb5)
    dbc = _s5_weight_grad("s5_dbc", h2, dbu, False, tk5)
    gp = ng * SSM_STATE
    grads['s5_c_re'] = _compact_c_t(dcc[:gp], ng)
    grads['s5_c_im'] = -_compact_c_t(dcc[gp:], ng)
    zoh_couts = (dabar[:, :gp].reshape(ng, SSM_STATE), dabar[:, gp:].reshape(ng, SSM_STATE),
                 _compact_b_t(dbc[:, :gp], ng), _compact_b_t(dbc[:, gp:], ng))
    grads['s5_a_re'], grads['s5_a_im'], grads['s5_log_dt'], db_re_t, db_im_t = _zoh_bwd(
        a_re, a_im, log_dt, b_re_t, b_im_t, zoh_couts)
    grads['s5_b_re'], grads['s5_b_im'] = db_re_t.transpose(0, 2, 1), db_im_t.transpose(0, 2, 1)
    dh2 = (dh2_glu, dh2_bu)

    dh1, ds0, dhid0, act0, dln_g[1], dln_b[1] = _mlp_bwd("mlp0_bwd", h1, s_mlp0, dh2, fw['mlp_w1'], fw['mlp_w2'], 0,
                                                         ln_g[1], ln_b[1], tbmb)
    grads['mlp_w1'] = _mlp_weight_grad("mlp0_dw1", h1, dhid0, 0, DEPTH, "n", into=dw1)
    grads['mlp_w2'] = _mlp_weight_grad("mlp0_dw2", act0, ds0, 0, DEPTH, "m", into=dw2)
    (do, dr_p, dk2_p, dv_p, dgate, dx_post), post_g = _stage_bwd("post_bwd", _f_post, post_acts, post_params, (dh1,), tbb,
                                                                 proxied=(3,))
    grads['rw_lnx_g'], grads['rw_lnx_b'], grads['rw_r_k'], grads['rw_wo'], dln_g[0], dln_b[0] = post_g
    ready_sums = _chip_sums("a", [grads[n] for n in BIG_READY], core)
    rec_g, ready_lands = _rec_bwd(*rec_in, s0s, do, hg, ready_sums)
    reduced = dict(zip(BIG_READY, zip(ready_sums, ready_lands)))
    dr_r, dlw, dk2_r, dv_r, dan, dbb = rec_g
    dk2 = (dk2_p, dk2_r)
    (dx_l, dxp_l, dkraw_l), lora_g = _stage_bwd("lora_bwd", _f_lora, (x, xp, raw['k']), lora_params,
                                                (dlw, dk2, dan, dbb, dgate), tbb)
    (dmu_w, dmu_a, dmu_g, grads['rw_w0'], grads['rw_w1'], grads['rw_w2'], grads['rw_a0'], grads['rw_a1'], grads['rw_a2'],
     grads['rw_g1'], grads['rw_g2'], grads['rw_k_k'], grads['rw_k_a']) = lora_g
    dproj = {'r': (dr_p, dr_r), 'k': dkraw_l, 'v': (dv_p, dv_r)}
    dxs, dxps, dmu = [dx_post, dx_l], [dxp_l], {}
    for n in 'rkv':
        (dx_n, dxp_n), (dmu[n], grads['rw_w' + n]) = _stage_bwd("proj_bwd_" + n, _f_proj, (x, xp), proj_params[n],
                                                                 (dproj[n],), tb, proxied=(1,))
        dxs.append(dx_n)
        dxps.append(dxp_n)
    grads['rw_mu'] = jnp.concatenate([dmu['r'], dmu_w, dmu['k'], dmu['v'], dmu_a, dmu_g], axis=0)
    grads['ln_g'] = jnp.concatenate(dln_g, axis=0)
    grads['ln_b'] = jnp.concatenate(dln_b, axis=0)
    grad_x = _addn("grad_x", dxs + [_shift_up(_addn("dxp_sum", dxps))])
    late = [n for n in BIG if n not in BIG_READY]
    late_sums = _chip_sums("b", [grads[n] for n in late], core)
    reduced.update(zip(late, zip(late_sums, _scatter_big(late_sums))))
    return loss_blk, grad_x, grads, reduced


def _position():
    return lax.axis_index("x"), lax.axis_index("y"), lax.axis_index("c")


def _other_chips(x, y):
    return [(1 - x, y), (x, 1 - y), (1 - x, 1 - y)]


def _chip_slice(ref, axis, q, size):
    idx = [slice(None)] * len(ref.shape)
    idx[axis] = pl.ds(pl.multiple_of(q * size, size), size)
    return ref.at[tuple(idx)]


_HBM = pl.BlockSpec(memory_space=pltpu.HBM)


def _gather_small_phases(src, dst, axes, sems):
    n = len(src)
    send_sems, recv_sems, own_sems = sems
    x, y, c = _position()
    chips = _other_chips(x, y)
    sizes = [src[a].shape[axes[a]] for a in range(n)]

    def copy(a, k, q):
        return pltpu.make_async_remote_copy(
            src_ref=src[a], dst_ref=_chip_slice(dst[a], axes[a], q, sizes[a]), send_sem=send_sems.at[a, k],
            recv_sem=recv_sems.at[a, k], device_id=(*chips[k], c), device_id_type=MESH)

    def own(a):
        return pltpu.make_async_copy(src[a], _chip_slice(dst[a], axes[a], 2 * x + y, sizes[a]), own_sems.at[a])

    def start():
        for a in range(n):
            own(a).start()
            for k in range(3):
                copy(a, k, 2 * x + y).start()

    def finish():
        for a in range(n):
            for k, (cx, cy) in enumerate(chips):
                copy(a, k, 2 * cx + cy).wait_recv()
        for a in range(n):
            for k in range(3):
                copy(a, k, 2 * x + y).wait_send()
            own(a).wait()

    return start, finish


def _gather_early(big, small, axes):
    nb, ns = len(big), len(small)
    full_shapes = [tuple(s * 4 if i == ax else s for i, s in enumerate(a.shape)) for a, ax in zip(small, axes)]

    def body(*refs):
        src_b, src_s = refs[:nb], refs[nb:nb + ns]
        dst_b, dst_s = refs[nb + ns:2 * nb + ns], refs[2 * nb + ns:2 * (nb + ns)]
        sems = refs[2 * (nb + ns):]
        small_start, small_finish = _gather_small_phases(src_s, dst_s, axes, sems[5:])
        small_start()
        for phase in _gather_big_phases(src_b, dst_b, sems[:5]):
            phase()
        small_finish()

    outs = pl.pallas_call(
        body, name="gather_early", in_specs=[_HBM] * (nb + ns), out_specs=[_HBM] * (nb + ns),
        out_shape=[_sds((4,) + a.shape, a.dtype) for a in big] + [_sds(s, a.dtype) for s, a in zip(full_shapes, small)],
        scratch_shapes=_gather_big_sems(nb) + [pltpu.SemaphoreType.DMA((ns, 3)), pltpu.SemaphoreType.DMA((ns, 3)),
                                               pltpu.SemaphoreType.DMA((ns,))],
        compiler_params=_cparams(),
    )(*big, *small)
    return outs[:nb], outs[nb:]


def _scatter_pieces(fulls, axes):
    n = len(fulls)
    sizes = [a.shape[ax] // 4 for a, ax in zip(fulls, axes)]
    shard_shapes = [tuple(sz if i == ax else s for i, s in enumerate(a.shape)) for a, ax, sz in zip(fulls, axes, sizes)]

    def body(*refs):
        src, land = refs[:n], refs[n:2 * n]
        send_sems, recv_sems = refs[2 * n:]
        x, y, c = _position()
        chips = _other_chips(x, y)

        def copy(a, k):
            cx, cy = chips[k]
            return pltpu.make_async_remote_copy(
                src_ref=_chip_slice(src[a], axes[a], 2 * cx + cy, sizes[a]), dst_ref=land[a].at[k],
                send_sem=send_sems.at[a, k], recv_sem=recv_sems.at[a, k], device_id=(cx, cy, c), device_id_type=MESH)

        for a in range(n):
            for k in range(3):
                copy(a, k).start()
        for a in range(n):
            for k in range(3):
                copy(a, k).wait_recv()
        for a in range(n):
            for k in range(3):
                copy(a, k).wait_send()

    return pl.pallas_call(
        body, name="scatter_grads", in_specs=[_HBM] * n, out_specs=[_HBM] * n,
        out_shape=[_sds((3,) + s) for s in shard_shapes],
        scratch_shapes=[pltpu.SemaphoreType.DMA((n, 3)), pltpu.SemaphoreType.DMA((n, 3))],
        compiler_params=_cparams(),
    )(*fulls)


def _sibling_swap(name, arrs):
    n = len(arrs)

    def body(*refs):
        src, dst = refs[:n], refs[n:2 * n]
        send_sems, recv_sems = refs[2 * n:]
        x, y, c = _position()
        copies = [pltpu.make_async_remote_copy(src_ref=src[a], dst_ref=dst[a], send_sem=send_sems.at[a], recv_sem=recv_sems.at[a],
                                               device_id=(x, y, 1 - c), device_id_type=MESH) for a in range(n)]
        for cp in copies:
            cp.start()
        for cp in copies:
            cp.wait_recv()
        for cp in copies:
            cp.wait_send()

    return pl.pallas_call(
        body, name=name, in_specs=[_HBM] * n, out_specs=[_HBM] * n, out_shape=[_sds(a.shape) for a in arrs],
        scratch_shapes=[pltpu.SemaphoreType.DMA((n,)), pltpu.SemaphoreType.DMA((n,))],
        compiler_params=_cparams(),
    )(*arrs)


def _sum4(name, own, land):
    rows, cols = own.shape
    tb = _rows_tile(rows)

    def body(o_ref, l0, l1, l2, out_ref):
        out_ref[...] = ((o_ref[...] + l0[...]) + l1[...]) + l2[...]

    blk = pl.BlockSpec((tb, cols), lambda i: (i, 0))
    lands = [pl.BlockSpec((None, tb, cols), functools.partial(lambda k, i: (k, i, 0), k)) for k in range(3)]
    return pl.pallas_call(body, name=name, grid=(rows // tb,), in_specs=[blk] + lands, out_specs=blk,
                          out_shape=_sds((rows, cols)), compiler_params=_cparams(("parallel",)))(own, land, land, land)


def _allreduce_adamw_small(g, w, m, v):
    rows, lanes = g.shape

    def body(g_ref, w_ref, m_ref, v_ref, gs_ref, d_ref, mn_ref, vn_ref, land, send_sems, recv_sems):
        x, y, c = _position()
        me = 4 * x + 2 * y + c
        masks = [(bx, by, bc) for bx in (0, 1) for by in (0, 1) for bc in (0, 1)][1:]

        def peer(mask):
            return (x ^ mask[0], y ^ mask[1], c ^ mask[2])

        def copy(j, slot):
            return pltpu.make_async_remote_copy(src_ref=g_ref, dst_ref=land.at[slot], send_sem=send_sems.at[j],
                                                recv_sem=recv_sems.at[j], device_id=peer(masks[j]), device_id_type=MESH)

        for j in range(7):
            copy(j, me).start()
        land[me] = g_ref[...]
        for j in range(7):
            px, py, pc = peer(masks[j])
            copy(j, 4 * px + 2 * py + pc).wait_recv()
        for j in range(7):
            copy(j, me).wait_send()
        total = land[0]
        for dev in range(1, 8):
            total = total + land[dev]
        delta, mn, vn = _adamw_math(w_ref[...], total, m_ref[...], v_ref[...])
        gs_ref[...] = total
        d_ref[...] = delta
        mn_ref[...] = mn
        vn_ref[...] = vn

    vmem = pl.BlockSpec(memory_space=pltpu.VMEM)
    return pl.pallas_call(
        body, name="allreduce_adamw_small", in_specs=[vmem] * 4, out_specs=[vmem] * 4, out_shape=[_sds((rows, lanes))] * 4,
        scratch_shapes=[pltpu.VMEM((8, rows, lanes), F32), pltpu.SemaphoreType.DMA((7,)), pltpu.SemaphoreType.DMA((7,))],
        compiler_params=_cparams(),
    )(g, w, m, v)


def _row_half(ref, c):
    r2 = ref.shape[-2] // 2
    lead = (slice(None),) * (len(ref.shape) - 2)
    return ref.at[(*lead, pl.ds(pl.multiple_of(c * r2, r2), r2), slice(None))]


def _gather_big_phases(src, dst, sems):
    n = len(src)
    ici_send, ici_recv, d2d_send, d2d_recv, own_sems = sems
    x, y, c = _position()
    me = 2 * x + y
    chips = _other_chips(x, y)
    ids = [2 * cx + cy for cx, cy in chips]

    def ici(a, k, q):
        return pltpu.make_async_remote_copy(
            src_ref=_row_half(src[a], c), dst_ref=_row_half(dst[a].at[q], c), send_sem=ici_send.at[a, k],
            recv_sem=ici_recv.at[a, k], device_id=(*chips[k], c), device_id_type=MESH)

    def d2d(a, k, half):
        where = _row_half(dst[a].at[ids[k]], half)
        return pltpu.make_async_remote_copy(src_ref=where, dst_ref=where, send_sem=d2d_send.at[a, k], recv_sem=d2d_recv.at[a, k],
                                            device_id=(x, y, 1 - c), device_id_type=MESH)

    def own(a):
        return pltpu.make_async_copy(src[a], dst[a].at[me], own_sems.at[a])

    def start():
        for a in range(n):
            own(a).start()
            for k in range(3):
                ici(a, k, me).start()

    def forward():
        for a in range(n):
            for k in range(3):
                ici(a, k, ids[k]).wait_recv()
                d2d(a, k, c).start()

    def finish():
        for a in range(n):
            for k in range(3):
                d2d(a, k, 1 - c).wait_recv()
        for a in range(n):
            for k in range(3):
                ici(a, k, me).wait_send()
                d2d(a, k, c).wait_send()
            own(a).wait()

    return start, forward, finish


def _gather_big_sems(n):
    return [pltpu.SemaphoreType.DMA((n, 3))] * 4 + [pltpu.SemaphoreType.DMA((n,))]


def _chip_sums(tag, grads, core):
    views = [g.reshape(4, -1, g.shape[-1]) for g in grads]
    others = _sibling_halves("sibling_halves_" + tag, views)
    return [_half_add(f"half_add_{tag}{i}", v, o, core) for i, (v, o) in enumerate(zip(views, others))]


def _sibling_halves(name, views):
    n = len(views)

    def body(*refs):
        src, dst = refs[:n], refs[n:2 * n]
        send_sems, recv_sems = refs[2 * n:]
        x, y, c = _position()
        copies = [pltpu.make_async_remote_copy(src_ref=_row_half(src[a], 1 - c), dst_ref=dst[a], send_sem=send_sems.at[a],
                                               recv_sem=recv_sems.at[a], device_id=(x, y, 1 - c), device_id_type=MESH)
                  for a in range(n)]
        for cp in copies:
            cp.start()
        for cp in copies:
            cp.wait_recv()
        for cp in copies:
            cp.wait_send()

    return pl.pallas_call(
        body, name=name, in_specs=[_HBM] * n, out_specs=[_HBM] * n,
        out_shape=[_sds((4, v.shape[1] // 2, v.shape[2])) for v in views],
        scratch_shapes=[pltpu.SemaphoreType.DMA((n,)), pltpu.SemaphoreType.DMA((n,))], compiler_params=_cparams(),
    )(*views)


def _rows_tile_capped(rows, cap=256):
    return min(_rows_tile(rows), cap)


def _half_add(name, view, other, core):
    _, r, k = view.shape
    r2 = r // 2
    tr = _rows_tile_capped(r2)
    per = r2 // tr

    def body(c_ref, v_ref, o_ref, out_ref):
        out_ref[...] = (v_ref[...] + o_ref[...]).astype(BF16)

    blk = pl.BlockSpec((None, tr, k), lambda q, i, c: (q, i, 0))
    return pl.pallas_call(
        body, name=name,
        grid_spec=pltpu.PrefetchScalarGridSpec(
            num_scalar_prefetch=1, grid=(4, per),
            in_specs=[pl.BlockSpec((None, tr, k), lambda q, i, c: (q, c[0] * per + i, 0)), blk], out_specs=blk),
        out_shape=_sds((4, r2, k), BF16), compiler_params=_cparams(("parallel", "parallel")),
    )(core, view, other)


def _scatter_big_phases(src, land, sems):
    n = len(src)
    send_sems, recv_sems = sems
    x, y, c = _position()
    chips = _other_chips(x, y)

    def copy(a, k):
        cx, cy = chips[k]
        return pltpu.make_async_remote_copy(src_ref=src[a].at[2 * cx + cy], dst_ref=land[a].at[k], send_sem=send_sems.at[a, k],
                                            recv_sem=recv_sems.at[a, k], device_id=(cx, cy, c), device_id_type=MESH)

    def start():
        for a in range(n):
            for k in range(3):
                copy(a, k).start()

    def finish():
        for a in range(n):
            for k in range(3):
                copy(a, k).wait_recv()
        for a in range(n):
            for k in range(3):
                copy(a, k).wait_send()

    return start, finish


def _scatter_big_sems(n):
    return [pltpu.SemaphoreType.DMA((n, 3)), pltpu.SemaphoreType.DMA((n, 3))]


def _scatter_big(sums):
    n = len(sums)

    def body(*refs):
        for phase in _scatter_big_phases(refs[:n], refs[n:2 * n], refs[2 * n:]):
            phase()

    return pl.pallas_call(
        body, name="scatter_big", in_specs=[_HBM] * n, out_specs=[_HBM] * n,
        out_shape=[_sds((3,) + s.shape[1:], s.dtype) for s in sums], scratch_shapes=_scatter_big_sems(n), compiler_params=_cparams(),
    )(*sums)


def _sum4_big(name, sums, land, chip):
    _, r2, k = sums.shape
    tr = _rows_tile_capped(r2)

    def body(q_ref, s_ref, l0, l1, l2, out_ref):
        out_ref[...] = ((s_ref[...].astype(F32) + l0[...].astype(F32)) + l1[...].astype(F32)) + l2[...].astype(F32)

    lands = [pl.BlockSpec((None, tr, k), functools.partial(lambda j, i, q: (j, i, 0), j)) for j in range(3)]
    return pl.pallas_call(
        body, name=name,
        grid_spec=pltpu.PrefetchScalarGridSpec(
            num_scalar_prefetch=1, grid=(r2 // tr,),
            in_specs=[pl.BlockSpec((None, tr, k), lambda i, q: (q[0], i, 0))] + lands,
            out_specs=pl.BlockSpec((tr, k), lambda i, q: (i, 0))),
        out_shape=_sds((r2, k)), compiler_params=_cparams(("parallel",)),
    )(chip, sums, land, land, land)


def _adamw_halves(name, mine, theirs, w, m, v, core):
    r, k = w.shape
    r2 = r // 2
    tr = _rows_tile_capped(r2, 512)
    per = r2 // tr

    def body(c_ref, mine_ref, theirs_ref, w_ref, m_ref, v_ref, g_out, d_out, m_out, v_out):
        g = jnp.where(pl.program_id(0) == c_ref[0], mine_ref[...], theirs_ref[...])
        delta, mn, vn = _adamw_math(w_ref[...], g, m_ref[...], v_ref[...])
        g_out[...] = g
        d_out[...] = delta
        m_out[...] = mn
        v_out[...] = vn

    half = pl.BlockSpec((tr, k), lambda h, i, c: (i, 0))
    full = pl.BlockSpec((tr, k), lambda h, i, c: (h * per + i, 0))
    return pl.pallas_call(
        body, name=name,
        grid_spec=pltpu.PrefetchScalarGridSpec(num_scalar_prefetch=1, grid=(2, per), in_specs=[half, half, full, full, full],
                                               out_specs=[full] * 4),
        out_shape=[_sds((r, k))] * 4, compiler_params=_cparams(("parallel", "parallel")),
    )(core, mine, theirs, w, m, v)


def _drops_layer_axis(name):
    return not (name.startswith('mlp') or name == 's5_d')


def _work(name, arr):
    return arr.reshape(arr.shape[1:]) if _drops_layer_axis(name) else arr


def _work_axis(name):
    return SHARD_AXIS[name] - (1 if _drops_layer_axis(name) else 0)


def _as2d(a):
    return a.reshape(-1, a.shape[-1])


def _replicated_2d(name, arr):
    if name in ('ln_g', 'ln_b'):
        return arr
    if name == 'rw_r_k':
        return arr.reshape(1, -1)
    if name == 's5_log_dt':
        return arr.reshape(-1, 1)
    if name.startswith('s5_'):
        return arr.reshape(arr.shape[1:])
    return arr


def _pack(arrs):
    flat = []
    for a in arrs:
        f = a.reshape(-1)
        flat.append(jnp.pad(f, (0, -f.shape[0] % 128)))
    f = jnp.concatenate(flat)
    f = jnp.pad(f, (0, -f.shape[0] % 1024))
    return f.reshape(-1, 128)


def _unpack(packed, shapes):
    flat = packed.reshape(-1)
    out, at = [], 0
    for s in shapes:
        size = math.prod(s)
        out.append(flat[at:at + size].reshape(s))
        at += size + (-size % 128)
    return out


def kernel(x, ln_g, ln_b, rw_mu, rw_w0, rw_w1, rw_w2, rw_a0, rw_a1, rw_a2, rw_g1, rw_g2, rw_k_k, rw_k_a, rw_r_k, rw_wr, rw_wk, rw_wv, rw_wo, rw_lnx_g, rw_lnx_b, s5_a_re, s5_a_im, s5_log_dt, s5_b_re, s5_b_im, s5_c_re, s5_c_im, s5_d, s5_w_glu, mlp_w1, mlp_w2, loss_target, m_ln_g, m_ln_b, m_rw_mu, m_rw_w0, m_rw_w1, m_rw_w2, m_rw_a0, m_rw_a1, m_rw_a2, m_rw_g1, m_rw_g2, m_rw_k_k, m_rw_k_a, m_rw_r_k, m_rw_wr, m_rw_wk, m_rw_wv, m_rw_wo, m_rw_lnx_g, m_rw_lnx_b, m_s5_a_re, m_s5_a_im, m_s5_log_dt, m_s5_b_re, m_s5_b_im, m_s5_c_re, m_s5_c_im, m_s5_d, m_s5_w_glu, m_mlp_w1, m_mlp_w2, v_ln_g, v_ln_b, v_rw_mu, v_rw_w0, v_rw_w1, v_rw_w2, v_rw_a0, v_rw_a1, v_rw_a2, v_rw_g1, v_rw_g2, v_rw_k_k, v_rw_k_a, v_rw_r_k, v_rw_wr, v_rw_wk, v_rw_wv, v_rw_wo, v_rw_lnx_g, v_rw_lnx_b, v_s5_a_re, v_s5_a_im, v_s5_log_dt, v_s5_b_re, v_s5_b_im, v_s5_c_re, v_s5_c_im, v_s5_d, v_s5_w_glu, v_mlp_w1, v_mlp_w2):
    d = dict(locals())
    x_pos, y_pos, c_pos = _position()
    chip = 2 * x_pos + y_pos
    chip_arr = jnp.reshape(chip, (1,)).astype(jnp.int32)
    core_arr = jnp.reshape(c_pos, (1,)).astype(jnp.int32)

    small = [n for n in SHARD_AXIS if n not in BIG]
    axes = [_work_axis(n) for n in small]
    big_views, small_fulls = _gather_early([_as2d(d[n]).astype(BF16) for n in BIG_EARLY], [_work(n, d[n]) for n in small], axes)
    views = dict(zip(BIG_EARLY, big_views))
    fw = dict(zip(small, small_fulls))
    c_model = d['x'].shape[-1]
    for n in BIG_EARLY:
        fw[n] = views[n].reshape(c_model, c_model)
    fw['late_shards'] = [_as2d(d[n]).astype(BF16) for n in BIG_LATE]
    for n in REPLICATED:
        fw[n] = _replicated_2d(n, d[n])

    loss_blk, grad_x, grads, reduced = _local_step(d['x'][0], d['loss_target'][0], fw, core_arr)
    loss = lax.psum(loss_blk[0, 0], ('x', 'y', 'c'))
    out = {}

    mine = [_sum4_big("sum4_" + n, *reduced[n], chip_arr) for n in BIG]
    theirs = _sibling_swap("swap_big", mine)
    for n, a, b in zip(BIG, mine, theirs):
        res = _adamw_halves("adamw_" + n, a, b, _as2d(d[n]), _as2d(d['m_' + n]), _as2d(d['v_' + n]), core_arr)
        out[n] = [r.reshape(d[n].shape) for r in res]

    pieces = [grads[n] for n in small]
    lands = _scatter_pieces(pieces, axes)
    mine = []
    for n, g, ax, land in zip(small, pieces, axes, lands):
        size = g.shape[ax] // 4
        mine.append(_sum4("sum4_" + n, lax.dynamic_slice_in_dim(g, chip * size, size, ax), land))
    theirs = _sibling_swap("swap_small", mine)
    for n, a, b in zip(small, mine, theirs):
        res = _adamw("adamw_" + n, (a, b), _as2d(d[n]), _as2d(d['m_' + n]), _as2d(d['v_' + n]))
        out[n] = [r.reshape(d[n].shape) for r in res]

    rep_shapes = [d[n].shape for n in REPLICATED]
    packs = [_pack([grads[n] for n in REPLICATED])] + [_pack([d[p + n] for n in REPLICATED]) for p in ('', 'm_', 'v_')]
    res = [_unpack(p, rep_shapes) for p in _allreduce_adamw_small(*packs)]
    for i, n in enumerate(REPLICATED):
        out[n] = [r[i] for r in res]

    grad_x = grad_x.reshape(d['x'].shape)
    return (loss, grad_x, *[out[n][0] for n in WEIGHTS], *[out[n][1] for n in WEIGHTS],
            *[out[n][2] for n in WEIGHTS], *[out[n][3] for n in WEIGHTS])
```

```python
import functools
import math

import jax
import jax.numpy as jnp
from jax import lax
from jax.experimental import pallas as pl
from jax.experimental.pallas import tpu as pltpu

F32 = jnp.float32
BF16 = jnp.bfloat16
MESH = pl.DeviceIdType.MESH

HEAD = 64
SSM_GROUP = 16
SSM_STATE = 64
GN_EPS = 64e-5
LN_EPS = 1e-5
DEPTH = 2
DN_ALPHA = (2.0 * DEPTH) ** 0.25
ADAM_LR, ADAM_B1, ADAM_B2, ADAM_EPS, ADAM_WD, ADAM_STEP = 0.001, 0.9, 0.999, 1e-08, 0.01, 10
REC_CHUNK = 64
V7X_VMEM_BYTES = 64 * 2 ** 20
VMEM_LIMIT = V7X_VMEM_BYTES - 8 * 2 ** 20

WEIGHTS = ['ln_g', 'ln_b', 'rw_mu', 'rw_w0', 'rw_w1', 'rw_w2', 'rw_a0', 'rw_a1', 'rw_a2', 'rw_g1', 'rw_g2',
           'rw_k_k', 'rw_k_a', 'rw_r_k', 'rw_wr', 'rw_wk', 'rw_wv', 'rw_wo', 'rw_lnx_g', 'rw_lnx_b',
           's5_a_re', 's5_a_im', 's5_log_dt', 's5_b_re', 's5_b_im', 's5_c_re', 's5_c_im', 's5_d', 's5_w_glu',
           'mlp_w1', 'mlp_w2']
SHARD_AXIS = {'rw_mu': 2, 'rw_w1': 1, 'rw_w2': 2, 'rw_a1': 1, 'rw_a2': 2, 'rw_g1': 1, 'rw_g2': 2,
              'rw_wr': 1, 'rw_wk': 1, 'rw_wv': 1, 'rw_wo': 1, 's5_d': 1, 's5_w_glu': 2, 'mlp_w1': 2, 'mlp_w2': 1}
REPLICATED = [n for n in WEIGHTS if n not in SHARD_AXIS]
BIG_EARLY = ['rw_wr', 'rw_wk', 'rw_wv']
BIG_LATE = ['rw_wo', 's5_w_glu', 'mlp_w1', 'mlp_w2']
BIG = BIG_EARLY + BIG_LATE
BIG_READY = ['rw_wo', 's5_w_glu', 'mlp_w1', 'mlp_w2']


def _sds(shape, dtype=F32):
    return jax.ShapeDtypeStruct(tuple(shape), dtype)


def _cparams(sem=None, **kw):
    if sem is not None:
        kw["dimension_semantics"] = sem
    return pltpu.CompilerParams(vmem_limit_bytes=VMEM_LIMIT, **kw)


def _mm_products(a, b, g):
    gb = g.astype(BF16)
    da = lax.dot_general(gb, b.astype(BF16), (((1,), (1,)), ((), ())), preferred_element_type=F32)
    db = lax.dot_general(a.astype(BF16), gb, (((0,), (0,)), ((), ())), preferred_element_type=F32)
    return da, db


@jax.custom_vjp
def _mm_plain(a, b):
    return jnp.dot(a.astype(BF16), b.astype(BF16), preferred_element_type=F32)


def _mm_plain_bwd(res, g):
    da, db = _mm_products(*res, g)
    return da.astype(res[0].dtype), db.astype(res[1].dtype)


_mm_plain.defvjp(lambda a, b: (_mm_plain(a, b), (a, b)), _mm_plain_bwd)


@jax.custom_vjp
def _mm_proxy(a, b, z):
    return jnp.dot(a.astype(BF16), b.astype(BF16), preferred_element_type=F32)


def _mm_proxy_bwd(res, g):
    da, db = _mm_products(*res, g)
    return da.astype(res[0].dtype), jnp.zeros_like(res[1]), db


_mm_proxy.defvjp(lambda a, b, z: (_mm_proxy(a, b, z), (a, b)), _mm_proxy_bwd)


def mm(a, b, z=None):
    return _mm_plain(a, b) if z is None else _mm_proxy(a, b, z)


def _split3(x):
    hi = x.astype(BF16)
    r1 = x - hi.astype(F32)
    mid = r1.astype(BF16)
    lo = (r1 - mid.astype(F32)).astype(BF16)
    return hi, mid, lo


def _head_sum_impl(x):
    c = x.shape[1]
    lanes = 128
    sel = (lax.broadcasted_iota(jnp.int32, (c, lanes), 0) // HEAD
           == lax.broadcasted_iota(jnp.int32, (c, lanes), 1)).astype(BF16)
    s = sum(jnp.dot(p, sel, preferred_element_type=F32) for p in _split3(x))
    return sum(lax.dot_general(p, sel, (((1,), (1,)), ((), ())), preferred_element_type=F32) for p in _split3(s))


@jax.custom_vjp
def head_sum(x):
    return _head_sum_impl(x)


head_sum.defvjp(lambda x: (_head_sum_impl(x), None), lambda _, g: (_head_sum_impl(g),))


def _ln(x, g, b):
    mu = jnp.mean(x, axis=-1, keepdims=True)
    xc = x - mu
    var = jnp.mean(xc * xc, axis=-1, keepdims=True)
    return xc * lax.rsqrt(var + LN_EPS) * g + b


def _f_proj(acts, params, proxies):
    x, xp = acts
    mu, w = params
    return (mm(x + (xp - x) * mu, w, proxies[1]),)


def _f_lora(acts, params, proxies):
    x, xp, kraw = acts
    mu_w, mu_a, mu_g, w0, w1, w2, a0, a1, a2, g1, g2, k_k, k_a = params
    xx = xp - x
    w_pre = w0 + mm(jnp.tanh(mm(x + xx * mu_w, w1)), w2)
    z = -w_pre
    softplus = jnp.maximum(z, 0.0) + jnp.log(1.0 + jnp.exp(-jnp.abs(z)))
    log_decay = -jnp.exp(-softplus - 0.5)
    a = jax.nn.sigmoid(a0 + mm(mm(x + xx * mu_a, a1), a2))
    g = mm(jax.nn.sigmoid(mm(x + xx * mu_g, g1)), g2)
    kk = kraw * k_k
    kkn = kk / jnp.maximum(jnp.sqrt(head_sum(kk * kk)), 1e-12)
    k2 = kraw * (1.0 + (a - 1.0) * k_a)
    return log_decay, k2, -kkn, kkn * a, g


def _f_post(acts, params, proxies):
    o, r, k2, v, g, x = acts
    lnx_g, lnx_b, r_k, wo, ln_g, ln_b = params
    om = head_sum(o) * (1.0 / HEAD)
    oc = o - om
    ov = head_sum(oc * oc) * (1.0 / HEAD)
    on = oc * lax.rsqrt(ov + GN_EPS) * lnx_g + lnx_b
    bonus = head_sum(r * k2 * r_k) * v
    y = mm((on + bonus) * g, wo, proxies[3])
    return (_ln(DN_ALPHA * x + y, ln_g, ln_b),)


def _f_glu(acts, params, proxies):
    ys, h = acts
    d, wv0, wv1, wg0, wg1, ln_g, ln_b = params
    y = jax.nn.gelu(ys + h * d)
    mix = jnp.concatenate([mm(y, wv0, proxies[1]) * jax.nn.sigmoid(mm(y, wg0, proxies[3])),
                           mm(y, wv1, proxies[2]) * jax.nn.sigmoid(mm(y, wg1, proxies[4]))], axis=1)
    return (_ln(DN_ALPHA * h + mix, ln_g, ln_b),)


def _f_zoh(a_re, a_im, log_dt, b_re_t, b_im_t):
    dt = jnp.exp(log_dt)
    lam_re = jnp.minimum(a_re, -1e-4)
    lam_im = a_im
    mag = jnp.exp(dt * lam_re)
    abar_re = mag * jnp.cos(dt * lam_im)
    abar_im = mag * jnp.sin(dt * lam_im)
    den = lam_re * lam_re + lam_im * lam_im
    nr, ni = abar_re - 1.0, abar_im
    coef_re = ((nr * lam_re + ni * lam_im) / den)[:, None, :]
    coef_im = ((ni * lam_re - nr * lam_im) / den)[:, None, :]
    return (abar_re, abar_im, coef_re * b_re_t - coef_im * b_im_t, coef_re * b_im_t + coef_im * b_re_t)


def _bdot16_raw(a, b, ca, cb):
    return lax.dot_general(a.astype(BF16), b.astype(BF16), (((ca,), (cb,)), ((0,), (0,))), preferred_element_type=F32)


@functools.partial(jax.custom_vjp, nondiff_argnums=(2, 3))
def _bdot16(a, b, ca, cb):
    return _bdot16_raw(a, b, ca, cb)


def _bdot16_bwd(ca, cb, res, g):
    a, b = res
    if (ca, cb) == (2, 1):
        return _bdot16_raw(g, b, 2, 2), _bdot16_raw(a, g, 1, 1)
    if (ca, cb) == (2, 2):
        return _bdot16_raw(g, b, 2, 1), _bdot16_raw(g, a, 1, 1)
    assert (ca, cb) == (1, 1)
    return _bdot16_raw(b, g, 2, 2), _bdot16_raw(a, g, 2, 1)


_bdot16.defvjp(lambda a, b, ca, cb: (_bdot16_raw(a, b, ca, cb), (a, b)), _bdot16_bwd)

def _time_sums(x, suffix):
    hg, ln, _ = x.shape
    row = lax.broadcasted_iota(jnp.int32, (hg, ln, ln), 1)
    col = lax.broadcasted_iota(jnp.int32, (hg, ln, ln), 2)
    tri = ((row <= col) if suffix else (row >= col)).astype(BF16)
    return sum(lax.dot_general(tri, p, (((2,), (1,)), ((0,), (0,))), preferred_element_type=F32) for p in _split3(x))


@jax.custom_vjp
def _time_cumsum(x):
    return _time_sums(x, False)


_time_cumsum.defvjp(lambda x: (_time_sums(x, False), None), lambda _, g: (_time_sums(g, True),))

_dot_score = _bdot16
_dot_inverse = _bdot16
_dot_value = _bdot16


def _rec_chunk(s0, r, lw, k, v, a, b):
    hg, ln, _ = r.shape
    row = lax.broadcasted_iota(jnp.int32, (hg, ln, ln), 1)
    col = lax.broadcasted_iota(jnp.int32, (hg, ln, ln), 2)
    incl, strict = row >= col, row > col
    cum = _time_cumsum(lw)
    total = jnp.sum(lw, axis=1, keepdims=True)
    e_cum, e_inv, e_prev, e_tail = jnp.exp(cum), jnp.exp(-cum), jnp.exp(cum - lw), jnp.exp(total - cum)
    rt, at, bt, kt = r * e_cum, a * e_prev, b * e_inv, k * e_inv
    aab = jnp.where(strict, _dot_score(at, bt, 2, 2), 0.0)
    aak = jnp.where(strict, _dot_score(at, kt, 2, 2), 0.0)
    arb = jnp.where(incl, _dot_score(rt, bt, 2, 2), 0.0)
    ark = jnp.where(incl, _dot_score(rt, kt, 2, 2), 0.0)
    p = (row == col).astype(F32) + aab
    m = aab
    for _ in range(int(math.log2(ln)) - 1):
        m = _dot_inverse(m, m, 2, 1)
        p = p + _dot_inverse(p, m, 2, 1)
    u = _dot_inverse(p, _dot_value(at, s0, 2, 2) + _dot_value(aak, v, 2, 1), 2, 1)
    o = _dot_value(rt, s0, 2, 2) + _dot_value(arb, u, 2, 1) + _dot_value(ark, v, 2, 1)
    s1 = s0 * jnp.exp(total) + _dot_value(u, b * e_tail, 1, 1) + _dot_value(v, k * e_tail, 1, 1)
    return o, s1


def _full_spec(shape):
    nd = len(shape)
    return pl.BlockSpec(tuple(shape), lambda *_: (0,) * nd)


def _stage_fwd(name, f, acts, params, out_dims, tb):
    t = acts[0].shape[0]
    na, npar = len(acts), len(params)

    def body(*refs):
        outs = f(tuple(r[...] for r in refs[:na]), tuple(r[...] for r in refs[na:na + npar]), (None,) * npar)
        for r, val in zip(refs[na + npar:], outs):
            r[...] = val

    return pl.pallas_call(
        body, name=name, grid=(t // tb,),
        in_specs=[pl.BlockSpec((tb, a.shape[1]), lambda i: (i, 0)) for a in acts] + [_full_spec(p.shape) for p in params],
        out_specs=[pl.BlockSpec((tb, d), lambda i: (i, 0)) for d in out_dims],
        out_shape=[_sds((t, d)) for d in out_dims],
        compiler_params=_cparams(("arbitrary",)),
    )(*acts, *params)


def _stage_bwd(name, f, acts, params, couts, tb, proxied=()):
    t = acts[0].shape[0]
    groups = [c if isinstance(c, tuple) else (c,) for c in couts]
    couts = [term for grp in groups for term in grp]
    na, npar, nc = len(acts), len(params), len(couts)
    steps = t // tb

    def f_diff(act_vals, diff_vals, param_vals):
        real = tuple(param_vals[i] if i in proxied else diff_vals[i] for i in range(npar))
        proxies = tuple(diff_vals[i] if i in proxied else None for i in range(npar))
        return f(act_vals, real, proxies)

    def body(*refs):
        a_refs, p_hbm, c_refs = refs[:na], refs[na:na + npar], refs[na + npar:na + npar + nc]
        o = na + npar + nc
        da_refs, dp_hbm = refs[o:o + na], refs[o + na:o + na + npar]
        p_buf, acc = refs[o + na + npar:o + na + 2 * npar], refs[o + na + 2 * npar:]
        i = pl.program_id(0)

        @pl.when(i == 0)
        def _():
            for src, dst in zip(p_hbm, p_buf):
                pltpu.sync_copy(src, dst)
            for r in acc:
                r[...] = jnp.zeros_like(r)

        param_vals = tuple(r[...] for r in p_buf)
        diff_vals = tuple(jnp.zeros(v.shape, F32) if i in proxied else v for i, v in enumerate(param_vals))
        _, vjp = jax.vjp(functools.partial(f_diff, param_vals=param_vals), tuple(r[...] for r in a_refs), diff_vals)
        terms = iter(c_refs)
        d_acts, d_params = vjp(tuple(functools.reduce(jnp.add, [next(terms)[...] for _ in grp]) for grp in groups))
        for r, val in zip(da_refs, d_acts):
            r[...] = val
        for r, val in zip(acc, d_params):
            r[...] += val

        @pl.when(i == steps - 1)
        def _():
            for src, dst in zip(acc, dp_hbm):
                pltpu.sync_copy(src, dst)

    hbm = pl.BlockSpec(memory_space=pltpu.HBM)
    outs = pl.pallas_call(
        body, name=name, grid=(steps,),
        in_specs=[pl.BlockSpec((tb, a.shape[1]), lambda i: (i, 0)) for a in acts] + [hbm] * npar
        + [pl.BlockSpec((tb, c.shape[1]), lambda i: (i, 0)) for c in couts],
        out_specs=[pl.BlockSpec((tb, a.shape[1]), lambda i: (i, 0)) for a in acts] + [hbm] * npar,
        out_shape=[_sds(a.shape) for a in acts] + [_sds(p.shape) for p in params],
        scratch_shapes=[pltpu.VMEM(p.shape, p.dtype) for p in params] + [pltpu.VMEM(p.shape, F32) for p in params],
        compiler_params=_cparams(("arbitrary",)),
    )(*acts, *params, *couts)
    return outs[:na], outs[na:]


def _tiled_matmul(name, a, b, mode, grid, a_spec, b_spec, o_spec, out_shape):
    nk = grid[2]
    dims = {"nn": ((1,), (0,)), "nt": ((1,), (1,)), "tn": ((0,), (0,))}[mode]

    def body(a_ref, b_ref, o_ref, acc):
        kk = pl.program_id(2)

        @pl.when(kk == 0)
        def _():
            acc[...] = jnp.zeros_like(acc)

        acc[...] += lax.dot_general(a_ref[...].astype(BF16), b_ref[...].astype(BF16), (dims, ((), ())),
                                    preferred_element_type=F32)

        @pl.when(kk == nk - 1)
        def _():
            o_ref[...] = acc[...]

    return pl.pallas_call(
        body, name=name, grid=grid, in_specs=[a_spec, b_spec], out_specs=o_spec, out_shape=_sds(out_shape),
        scratch_shapes=[pltpu.VMEM(o_spec.block_shape, F32)],
        compiler_params=_cparams(("parallel", "parallel", "arbitrary")),
    )(a, b)


def _mlp_weight_grad(name, a, b, layer, layers, split, into=None, tile=512):
    t, m = a.shape
    n = b.shape[1]
    tk = min(tile, t)
    tile = 2 * tile
    if split == "n":
        tm, tn = min(tile, m), min(tile, n // 4)
        per = n // 4 // tn
        shape = (4, layers, m, n // 4)
        o_idx = lambda i, j, k: (j // per, layer, i, j % per)
    else:
        tm, tn = min(tile, m // 4), min(tile, n)
        per = m // 4 // tm
        shape = (4, layers, m // 4, n)
        o_idx = lambda i, j, k: (i // per, layer, i % per, j)
    nk = t // tk

    def body(a_ref, b_ref, *rest):
        o_ref, acc = rest[-2:]
        kk = pl.program_id(2)

        @pl.when(kk == 0)
        def _():
            acc[...] = jnp.zeros_like(acc)

        acc[...] += lax.dot_general(a_ref[...].astype(BF16), b_ref[...].astype(BF16), (((0,), (0,)), ((), ())),
                                    preferred_element_type=F32)

        @pl.when(kk == nk - 1)
        def _():
            o_ref[...] = acc[...]

    in_specs = [pl.BlockSpec((tk, tm), lambda i, j, k: (k, i)), pl.BlockSpec((tk, tn), lambda i, j, k: (k, j))]
    operands = [a, b]
    aliases = {}
    if into is not None:
        in_specs.append(pl.BlockSpec(memory_space=pl.ANY))
        operands.append(into)
        aliases = {2: 0}
    return pl.pallas_call(
        body, name=name, grid=(m // tm, n // tn, nk), in_specs=in_specs,
        out_specs=pl.BlockSpec((None, None, tm, tn), o_idx), out_shape=_sds(shape), input_output_aliases=aliases,
        scratch_shapes=[pltpu.VMEM((tm, tn), F32)],
        compiler_params=_cparams(("parallel", "parallel", "arbitrary")),
    )(*operands)


S5_PACK = 8


def _s5_weight_grad(name, x, s, wide_rows, tk):
    t, c = x.shape
    wide = s.shape[1]
    kb, nb = S5_PACK * SSM_GROUP, S5_PACK * SSM_STATE
    nsb = c // kb
    x_spec = pl.BlockSpec((tk, kb), lambda i, j, k: (k, j % nsb))
    s_spec = pl.BlockSpec((tk, nb), lambda i, j, k: (k, j))
    if wide_rows:
        return _tiled_matmul(name, s, x, "tn", (1, wide // nb, t // tk), s_spec, x_spec,
                             pl.BlockSpec((nb, kb), lambda i, j, k: (j, 0)), (wide, kb))
    return _tiled_matmul(name, x, s, "tn", (1, wide // nb, t // tk), x_spec, s_spec,
                         pl.BlockSpec((kb, nb), lambda i, j, k: (0, j)), (kb, wide))


def _mlp_fwd(name, h, w1, w2, layer, ln_g, ln_b, tb):
    t, c = h.shape
    nj, fc = w1.shape[0], w1.shape[3]

    def body(h_ref, w1_ref, w2_ref, g_ref, b_ref, out_ref, s_ref, acc):
        j = pl.program_id(1)

        @pl.when(j == 0)
        def _():
            acc[...] = jnp.zeros_like(acc)

        hid = jnp.dot(h_ref[...].astype(BF16), w1_ref[...].astype(BF16), preferred_element_type=F32)
        act = jnp.square(jnp.maximum(hid, 0.0))
        acc[...] += jnp.dot(act.astype(BF16), w2_ref[...].astype(BF16), preferred_element_type=F32)

        @pl.when(j == nj - 1)
        def _():
            s = DN_ALPHA * h_ref[...] + acc[...]
            s_ref[...] = s
            out_ref[...] = _ln(s, g_ref[...], b_ref[...])

    row = pl.BlockSpec((tb, c), lambda i, j: (i, 0))
    vec = pl.BlockSpec((1, c), lambda i, j: (0, 0))
    return pl.pallas_call(
        body, name=name, grid=(t // tb, nj),
        in_specs=[row, pl.BlockSpec((None, None, c, fc), lambda i, j: (j, layer, 0, 0)),
                  pl.BlockSpec((None, None, fc, c), lambda i, j: (j, layer, 0, 0)), vec, vec],
        out_specs=[row, row], out_shape=[_sds((t, c)), _sds((t, c))],
        scratch_shapes=[pltpu.VMEM((tb, c), F32)],
        compiler_params=_cparams(("parallel", "arbitrary")),
    )(h, w1, w2, ln_g, ln_b)


def _mlp_bwd(name, h, s, dout, w1, w2, layer, ln_g, ln_b, tb):
    t, c = h.shape
    nj, fc = w1.shape[0], w1.shape[3]
    ff = nj * fc
    ni = t // tb
    nt = (((1,), (1,)), ((), ()))
    douts = dout if isinstance(dout, tuple) else (dout,)
    nd = len(douts)

    def body(h_ref, s_ref, *rest):
        dout_refs = rest[:nd]
        (w1_ref, w2_ref, g_ref, b_ref, dh_ref, ds_ref, dhid_ref, act_ref, dg_ref, db_ref,
         ds_scr, dh_acc, dg_acc, db_acc) = rest[nd:]
        i, j = pl.program_id(0), pl.program_id(1)

        @pl.when((i == 0) & (j == 0))
        def _():
            dg_acc[...] = jnp.zeros_like(dg_acc)
            db_acc[...] = jnp.zeros_like(db_acc)

        @pl.when(j == 0)
        def _():
            _, vjp = jax.vjp(_ln, s_ref[...], g_ref[...], b_ref[...])
            ds, dg, db = vjp(functools.reduce(jnp.add, [r[...] for r in dout_refs]))
            ds_scr[...] = ds
            ds_ref[...] = ds.astype(BF16)
            dh_acc[...] = DN_ALPHA * ds
            dg_acc[...] += dg
            db_acc[...] += db

        w1b, w2b = w1_ref[...].astype(BF16), w2_ref[...].astype(BF16)
        hid = jnp.dot(h_ref[...].astype(BF16), w1b, preferred_element_type=F32)
        rl = jnp.maximum(hid, 0.0)
        dact = lax.dot_general(ds_scr[...].astype(BF16), w2b, nt, preferred_element_type=F32)
        dhid = (dact * 2.0 * rl).astype(BF16)
        dh_acc[...] += lax.dot_general(dhid, w1b, nt, preferred_element_type=F32)
        dhid_ref[...] = dhid
        act_ref[...] = (rl * rl).astype(BF16)

        @pl.when(j == nj - 1)
        def _():
            dh_ref[...] = dh_acc[...]

        @pl.when((i == ni - 1) & (j == nj - 1))
        def _():
            dg_ref[...] = dg_acc[...]
            db_ref[...] = db_acc[...]

    row = pl.BlockSpec((tb, c), lambda i, j: (i, 0))
    vec = pl.BlockSpec((1, c), lambda i, j: (0, 0))
    wide = pl.BlockSpec((tb, fc), lambda i, j: (i, j))
    return pl.pallas_call(
        body, name=name, grid=(ni, nj),
        in_specs=[row, row] + [row] * nd + [pl.BlockSpec((None, None, c, fc), lambda i, j: (j, layer, 0, 0)),
                                            pl.BlockSpec((None, None, fc, c), lambda i, j: (j, layer, 0, 0)), vec, vec],
        out_specs=[row, row, wide, wide, vec, vec],
        out_shape=[_sds((t, c)), _sds((t, c), BF16), _sds((t, ff), BF16), _sds((t, ff), BF16), _sds((1, c)), _sds((1, c))],
        scratch_shapes=[pltpu.VMEM((tb, c), F32), pltpu.VMEM((tb, c), F32), pltpu.VMEM((1, c), F32), pltpu.VMEM((1, c), F32)],
        compiler_params=_cparams(("arbitrary", "arbitrary")),
    )(h, s, *douts, w1, w2, ln_g, ln_b)


def _load_heads(ref, hg):
    return jnp.stack([ref[:, h * HEAD:(h + 1) * HEAD] for h in range(hg)])


def _store_heads(ref, val):
    for h in range(val.shape[0]):
        ref[:, h * HEAD:(h + 1) * HEAD] = val[h]


def _rec_fwd(r, lw, k, v, a, b, hg, shards):
    t, c = r.shape
    n = HEAD
    nh = c // n
    ln = REC_CHUNK
    nck = t // ln
    ngrp = nh // hg
    nsh = len(shards)
    steps = ngrp * nck

    def body(r_ref, lw_ref, k_ref, v_ref, a_ref, b_ref, *rest):
        src, (o_ref, s0_ref), dst = rest[:nsh], rest[nsh:nsh + 2], rest[nsh + 2:2 * nsh + 2]
        state, sems = rest[2 * nsh + 2], rest[2 * nsh + 3:]
        step = pl.program_id(0) * nck + pl.program_id(1)
        start, forward, finish = _gather_big_phases(src, dst, sems)
        pl.when(step == 0)(start)

        @pl.when(pl.program_id(1) == 0)
        def _():
            state[...] = jnp.zeros_like(state)

        s0 = state[...]
        s0_ref[...] = s0
        o, s1 = _rec_chunk(s0, *(_load_heads(x, hg) for x in (r_ref, lw_ref, k_ref, v_ref, a_ref, b_ref)))
        _store_heads(o_ref, o)
        state[...] = s1
        pl.when(step == steps // 2)(forward)
        pl.when(step == steps - 1)(finish)

    seq = pl.BlockSpec((ln, hg * n), lambda g, i: (i, g))
    outs = pl.pallas_call(
        body, name="rec_fwd", grid=(ngrp, nck), in_specs=[seq] * 6 + [_HBM] * nsh,
        out_specs=[seq, pl.BlockSpec((None, hg, n, n), lambda g, i: (i, g, 0, 0))] + [_HBM] * nsh,
        out_shape=[_sds((t, c)), _sds((nck, nh, n, n))] + [_sds((4,) + s.shape, s.dtype) for s in shards],
        scratch_shapes=[pltpu.VMEM((hg, n, n), F32)] + _gather_big_sems(nsh),
        compiler_params=_cparams(("arbitrary", "arbitrary")),
    )(r, lw, k, v, a, b, *shards)
    return outs[0], outs[1], outs[2:]


def _rec_bwd(r, lw, k, v, a, b, s0s, do, hg, chip_sums):
    t, c = r.shape
    n = HEAD
    nh = c // n
    ln = REC_CHUNK
    nck = t // ln
    ngrp = nh // hg
    nsum = len(chip_sums)
    steps = ngrp * nck

    def body(r_ref, lw_ref, k_ref, v_ref, a_ref, b_ref, s0_ref, do_ref, *rest):
        src, grad_refs, land = rest[:nsum], rest[nsum:nsum + 6], rest[nsum + 6:2 * nsum + 6]
        dstate, sems = rest[2 * nsum + 6], rest[2 * nsum + 7:]
        step = pl.program_id(0) * nck + pl.program_id(1)
        start, finish = _scatter_big_phases(src, land, sems)
        pl.when(step == 0)(start)

        @pl.when(pl.program_id(1) == 0)
        def _():
            dstate[...] = jnp.zeros_like(dstate)

        _, vjp = jax.vjp(_rec_chunk, s0_ref[...], *(_load_heads(x, hg) for x in (r_ref, lw_ref, k_ref, v_ref, a_ref, b_ref)))
        ds0, *grads = vjp((_load_heads(do_ref, hg), dstate[...]))
        dstate[...] = ds0
        for ref, val in zip(grad_refs, grads):
            _store_heads(ref, val)
        pl.when(step == steps - 1)(finish)

    seq = pl.BlockSpec((ln, hg * n), lambda g, i: (nck - 1 - i, g))
    outs = pl.pallas_call(
        body, name="rec_bwd", grid=(ngrp, nck),
        in_specs=[seq] * 6 + [pl.BlockSpec((None, hg, n, n), lambda g, i: (nck - 1 - i, g, 0, 0)), seq] + [_HBM] * nsum,
        out_specs=[seq] * 6 + [_HBM] * nsum,
        out_shape=[_sds((t, c))] * 6 + [_sds((3,) + s.shape[1:], s.dtype) for s in chip_sums],
        scratch_shapes=[pltpu.VMEM((hg, n, n), F32)] + _scatter_big_sems(nsum),
        compiler_params=_cparams(("arbitrary", "arbitrary")),
    )(r, lw, k, v, a, b, s0s, do, *chip_sums)
    return outs[:6], outs[6:]


def _s5_blocks(c):
    kb, nb = S5_PACK * SSM_GROUP, S5_PACK * SSM_STATE
    return kb, nb, c // kb


def _s5_fwd(h, bc, abar, cc, tb):
    t, c = h.shape
    w2 = bc.shape[1]
    w = w2 // 2
    kb, nb, nsb = _s5_blocks(c)

    def body(h_ref, bc_ref, a_ref, cc_ref, s_ref, y_ref, carry, rows):
        @pl.when(pl.program_id(0) == 0)
        def _():
            carry[...] = jnp.zeros_like(carry)

        for j in range(w2 // nb):
            ch = (j % nsb) * kb
            rows[:, j * nb:(j + 1) * nb] = jnp.dot(h_ref[:, ch:ch + kb].astype(BF16), bc_ref[:, j * nb:(j + 1) * nb],
                                                   preferred_element_type=F32)
        ar, ai = a_ref[:, :w], a_ref[:, w:]

        def step(i, state):
            hr, hi = state
            nr = ar * hr - ai * hi + rows[pl.ds(i, 1), :w]
            ni = ar * hi + ai * hr + rows[pl.ds(i, 1), w:]
            rows[pl.ds(i, 1), :w] = nr
            rows[pl.ds(i, 1), w:] = ni
            return nr, ni

        hr, hi = lax.fori_loop(0, tb, step, (carry[:, :w], carry[:, w:]))
        carry[:, :w] = hr
        carry[:, w:] = hi
        s_ref[...] = rows[...].astype(BF16)
        for j in range(nsb):
            re, im = j * nb, w + j * nb
            y_ref[:, j * kb:(j + 1) * kb] = (
                jnp.dot(s_ref[:, re:re + nb], cc_ref[re:re + nb, :], preferred_element_type=F32)
                + jnp.dot(s_ref[:, im:im + nb], cc_ref[im:im + nb, :], preferred_element_type=F32))

    return pl.pallas_call(
        body, name="s5_fwd", grid=(t // tb,),
        in_specs=[pl.BlockSpec((tb, c), lambda i: (i, 0)), _full_spec(bc.shape), _full_spec(abar.shape), _full_spec(cc.shape)],
        out_specs=[pl.BlockSpec((tb, w2), lambda i: (i, 0)), pl.BlockSpec((tb, c), lambda i: (i, 0))],
        out_shape=[_sds((t, w2), BF16), _sds((t, c))],
        scratch_shapes=[pltpu.VMEM((1, w2), F32), pltpu.VMEM((tb, w2), F32)],
        compiler_params=_cparams(("arbitrary",)),
    )(h, bc, abar, cc)


def _s5_bwd(dy, s, abar, cc, bc, tb):
    t, c = dy.shape
    w2 = s.shape[1]
    w = w2 // 2
    kb, nb, nsb = _s5_blocks(c)
    nblk = t // tb
    pack = 16
    per = tb // pack
    nt = (((1,), (1,)), ((), ()))

    def body(dy_ref, s_ref, sprev_ref, a_ref, cc_ref, bc_ref, dbu_ref, dh_ref, da_ref, carry, da_acc, rows):
        i = pl.program_id(0)

        @pl.when(i == 0)
        def _():
            carry[...] = jnp.zeros_like(carry)
            da_acc[...] = jnp.zeros_like(da_acc)

        for j in range(w2 // nb):
            ch = (j % nsb) * kb
            rows[:, j * nb:(j + 1) * nb] = lax.dot_general(dy_ref[:, ch:ch + kb].astype(BF16), cc_ref[j * nb:(j + 1) * nb, :], nt,
                                                           preferred_element_type=F32)
        ar, ai = a_ref[:, :w], a_ref[:, w:]

        def step(n, state):
            gr, gi = state
            row = tb - 1 - n
            nr = rows[pl.ds(row, 1), :w] + ar * gr + ai * gi
            ni = rows[pl.ds(row, 1), w:] + ar * gi - ai * gr
            rows[pl.ds(row, 1), :w] = nr
            rows[pl.ds(row, 1), w:] = ni
            return nr, ni

        gr, gi = lax.fori_loop(0, tb, step, (carry[:, :w], carry[:, w:]))
        carry[:, :w] = gr
        carry[:, w:] = gi
        last = (lax.broadcasted_iota(jnp.int32, (pack, w2), 0) == pack - 1) & (i < nblk - 1)
        before = jnp.sum(jnp.where(last, sprev_ref[...].astype(F32), 0.0), axis=0, keepdims=True)
        rid = lax.broadcasted_iota(jnp.int32, (tb, w2), 0)
        sp = jnp.where(rid == 0, before, pltpu.roll(s_ref[...].astype(F32), 1, 0))
        g = rows[...]
        dbu_ref[...] = g.astype(BF16)
        spr, spi, g_r, g_i = sp[:, :w], sp[:, w:], g[:, :w], g[:, w:]
        da_acc[:, :w] += jnp.sum(spr * g_r + spi * g_i, axis=0, keepdims=True)
        da_acc[:, w:] += jnp.sum(spr * g_i - spi * g_r, axis=0, keepdims=True)
        for j in range(nsb):
            re, im = j * nb, w + j * nb
            dh_ref[:, j * kb:(j + 1) * kb] = (
                lax.dot_general(dbu_ref[:, re:re + nb], bc_ref[:, re:re + nb], nt, preferred_element_type=F32)
                + lax.dot_general(dbu_ref[:, im:im + nb], bc_ref[:, im:im + nb], nt, preferred_element_type=F32))

        @pl.when(i == nblk - 1)
        def _():
            da_ref[...] = da_acc[...]

    wide = pl.BlockSpec((tb, w2), lambda i: (nblk - 1 - i, 0))
    narrow = pl.BlockSpec((tb, c), lambda i: (nblk - 1 - i, 0))
    prev = pl.BlockSpec((pack, w2), lambda i: (jnp.maximum((nblk - 1 - i) * per - 1, 0), 0))
    return pl.pallas_call(
        body, name="s5_bwd", grid=(nblk,),
        in_specs=[narrow, wide, prev, _full_spec(abar.shape), _full_spec(cc.shape), _full_spec(bc.shape)],
        out_specs=[wide, narrow, pl.BlockSpec((1, w2), lambda i: (0, 0))],
        out_shape=[_sds((t, w2), BF16), _sds((t, c)), _sds((1, w2))],
        scratch_shapes=[pltpu.VMEM((1, w2), F32), pltpu.VMEM((1, w2), F32), pltpu.VMEM((tb, w2), F32)],
        compiler_params=_cparams(("arbitrary",)),
    )(dy, s, s, abar, cc, bc)


def _zoh_fwd(a_re, a_im, log_dt, b_re_t, b_im_t):
    def body(*refs):
        for r, val in zip(refs[5:], _f_zoh(*(x[...] for x in refs[:5]))):
            r[...] = val

    return pl.pallas_call(body, name="s5_zoh_fwd", out_shape=[_sds(a_re.shape)] * 2 + [_sds(b_re_t.shape)] * 2,
                          compiler_params=_cparams())(a_re, a_im, log_dt, b_re_t, b_im_t)


def _zoh_bwd(a_re, a_im, log_dt, b_re_t, b_im_t, couts):
    def body(*refs):
        _, vjp = jax.vjp(_f_zoh, *(x[...] for x in refs[:5]))
        for r, val in zip(refs[9:], vjp(tuple(x[...] for x in refs[5:9]))):
            r[...] = val

    ins = (a_re, a_im, log_dt, b_re_t, b_im_t)
    return pl.pallas_call(body, name="s5_zoh_bwd", out_shape=[_sds(x.shape) for x in ins],
                          compiler_params=_cparams())(*ins, *couts)


def _loss_head(h, target, tb):
    t, c = h.shape
    nb = t // tb

    def body(h_ref, t_ref, loss_ref, dh_ref, acc):
        i = pl.program_id(0)

        @pl.when(i == 0)
        def _():
            acc[...] = jnp.zeros_like(acc)

        d = h_ref[...] - t_ref[...]
        dh_ref[...] = d * (1.0 / c)
        acc[...] += 0.5 * jnp.sum(jnp.mean(d * d, axis=-1, keepdims=True), axis=0, keepdims=True)

        @pl.when(i == nb - 1)
        def _():
            loss_ref[...] = jnp.broadcast_to(acc[...], loss_ref.shape)

    row = pl.BlockSpec((tb, c), lambda i: (i, 0))
    return pl.pallas_call(
        body, name="loss_head", grid=(nb,), in_specs=[row, row],
        out_specs=[pl.BlockSpec((8, 128), lambda i: (0, 0)), row], out_shape=[_sds((8, 128)), _sds((t, c))],
        scratch_shapes=[pltpu.VMEM((1, 1), F32)], compiler_params=_cparams(("arbitrary",)),
    )(h, target)


def _rows_tile(rows):
    for cand in (512, 256, 128, 64, 32, 16, 8):
        if rows % cand == 0:
            return cand
    return rows


def _addn(name, arrs):
    rows, cols = arrs[0].shape
    tb = _rows_tile(rows)

    def body(*refs):
        acc = refs[0][...]
        for r in refs[1:-1]:
            acc = acc + r[...]
        refs[-1][...] = acc

    blk = pl.BlockSpec((tb, cols), lambda i: (i, 0))
    return pl.pallas_call(body, name=name, grid=(rows // tb,), in_specs=[blk] * len(arrs), out_specs=blk,
                          out_shape=_sds((rows, cols)), compiler_params=_cparams(("parallel",)))(*arrs)


def _adamw_math(w, g, m, v):
    m = ADAM_B1 * m + (1.0 - ADAM_B1) * g
    v = ADAM_B2 * v + (1.0 - ADAM_B2) * jnp.square(g)
    m_hat = m / (1.0 - ADAM_B1 ** ADAM_STEP)
    v_hat = v / (1.0 - ADAM_B2 ** ADAM_STEP)
    delta = -ADAM_LR * (m_hat / (jnp.sqrt(v_hat) + ADAM_EPS) + ADAM_WD * w)
    return delta, m, v


def _adamw(name, parts, w, m, v):
    rows, cols = w.shape
    tb = _rows_tile(rows)
    npart = len(parts)

    def body(*refs):
        g = refs[0][...]
        for r in refs[1:npart]:
            g = g + r[...]
        w_ref, m_ref, v_ref = refs[npart:npart + 3]
        g_out, d_out, m_out, v_out = refs[npart + 3:]
        delta, mn, vn = _adamw_math(w_ref[...], g, m_ref[...], v_ref[...])
        g_out[...] = g
        d_out[...] = delta
        m_out[...] = mn
        v_out[...] = vn

    blk = pl.BlockSpec((tb, cols), lambda i: (i, 0))
    return pl.pallas_call(body, name=name, grid=(rows // tb,), in_specs=[blk] * (npart + 3), out_specs=[blk] * 4,
                          out_shape=[_sds((rows, cols))] * 4, compiler_params=_cparams(("parallel",)))(*parts, w, m, v)


def _shift_down(a):
    return jnp.concatenate([jnp.zeros_like(a[:1]), a[:-1]], axis=0)


def _shift_up(a):
    return jnp.concatenate([a[1:], jnp.zeros_like(a[:1])], axis=0)


def _s5_pack_mask(g):
    return (jnp.arange(g)[None, :] % S5_PACK == jnp.arange(S5_PACK)[:, None]).astype(F32)


def _compact_b(bbar_t):
    g, s, p = bbar_t.shape
    return (_s5_pack_mask(g)[:, None, :, None] * bbar_t.transpose(1, 0, 2)[None]).reshape(S5_PACK * s, g * p)


def _compact_b_t(dense, g):
    s, p = dense.shape[0] // S5_PACK, dense.shape[1] // g
    return jnp.sum(dense.reshape(S5_PACK, s, g, p) * _s5_pack_mask(g)[:, None, :, None], axis=0).transpose(1, 0, 2)


def _compact_c(c_w):
    g, s, p = c_w.shape
    return (c_w.transpose(0, 2, 1)[:, :, None, :] * _s5_pack_mask(g).T[:, None, :, None]).reshape(g * p, S5_PACK * s)


def _compact_c_t(dense, g):
    p, s = dense.shape[0] // g, dense.shape[1] // S5_PACK
    return jnp.sum(dense.reshape(g, p, S5_PACK, s) * _s5_pack_mask(g).T[:, None, :, None], axis=2).transpose(0, 2, 1)


def _local_step(x, target, fw, core):
    t, c = x.shape
    nh = c // HEAD
    ng = c // SSM_GROUP
    tb = min(256, t)
    tbb = min(128, t)
    tbm = min(512, t)
    tbmb = min(512, t)
    tbs = min(256, t)
    tk5 = min(2048, t)
    hg = min(16, nh)
    mu = [fw['rw_mu'][i:i + 1] for i in range(6)]
    ln_g = [fw['ln_g'][i:i + 1] for i in range(4)]
    ln_b = [fw['ln_b'][i:i + 1] for i in range(4)]
    grads = {}

    xp = _shift_down(x)
    proj_params = {n: (mu[i], fw['rw_w' + n]) for n, i in (('r', 0), ('k', 2), ('v', 3))}
    raw = {n: _stage_fwd("proj_" + n, _f_proj, (x, xp), proj_params[n], (c,), tb)[0] for n in 'rkv'}
    lora_params = (mu[1], mu[4], mu[5], fw['rw_w0'], fw['rw_w1'], fw['rw_w2'], fw['rw_a0'], fw['rw_a1'], fw['rw_a2'],
                   fw['rw_g1'], fw['rw_g2'], fw['rw_k_k'], fw['rw_k_a'])
    lw, k2, an, bb, gate = _stage_fwd("lora", _f_lora, (x, xp, raw['k']), lora_params, (c,) * 5, tb)
    rec_in = (raw['r'], lw, k2, raw['v'], an, bb)
    o, s0s, late = _rec_fwd(*rec_in, hg, fw['late_shards'])
    late = dict(zip(BIG_LATE, late))
    fw = dict(fw, rw_wo=late['rw_wo'].reshape(c, c), s5_w_glu=tuple(late['s5_w_glu'][q] for q in range(4)),
              mlp_w1=late['mlp_w1'].reshape(4, DEPTH, c, -1), mlp_w2=late['mlp_w2'].reshape(4, DEPTH, -1, c))
    post_params = (fw['rw_lnx_g'], fw['rw_lnx_b'], fw['rw_r_k'], fw['rw_wo'], ln_g[0], ln_b[0])
    post_acts = (o, raw['r'], k2, raw['v'], gate, x)
    h1, = _stage_fwd("post", _f_post, post_acts, post_params, (c,), tb)
    h2, s_mlp0 = _mlp_fwd("mlp0_fwd", h1, fw['mlp_w1'], fw['mlp_w2'], 0, ln_g[1], ln_b[1], tbm)

    a_re, a_im, log_dt = fw['s5_a_re'], fw['s5_a_im'], fw['s5_log_dt']
    b_re_t, b_im_t = fw['s5_b_re'].transpose(0, 2, 1), fw['s5_b_im'].transpose(0, 2, 1)
    abar_re, abar_im, bbar_re_t, bbar_im_t = _zoh_fwd(a_re, a_im, log_dt, b_re_t, b_im_t)
    abar = jnp.concatenate([abar_re.reshape(1, -1), abar_im.reshape(1, -1)], axis=1)
    bc = jnp.concatenate([_compact_b(bbar_re_t), _compact_b(bbar_im_t)], axis=1).astype(BF16)
    cc = jnp.concatenate([_compact_c(fw['s5_c_re']), -_compact_c(fw['s5_c_im'])], axis=0).astype(BF16)
    st, ys = _s5_fwd(h2, bc, abar, cc, tbs)
    glu_params = (fw['s5_d'], *fw['s5_w_glu'], ln_g[2], ln_b[2])
    h3, = _stage_fwd("glu", _f_glu, (ys, h2), glu_params, (c,), tb)
    h4, s_mlp1 = _mlp_fwd("mlp1_fwd", h3, fw['mlp_w1'], fw['mlp_w2'], 1, ln_g[3], ln_b[3], tbm)

    loss_blk, dh4 = _loss_head(h4, target, tb)

    dln_g, dln_b = [None] * 4, [None] * 4
    dh3, ds1, dhid1, act1, dln_g[3], dln_b[3] = _mlp_bwd("mlp1_bwd", h3, s_mlp1, dh4, fw['mlp_w1'], fw['mlp_w2'], 1,
                                                         ln_g[3], ln_b[3], tbmb)
    dw1 = _mlp_weight_grad("mlp1_dw1", h3, dhid1, 1, DEPTH, "n")
    dw2 = _mlp_weight_grad("mlp1_dw2", act1, ds1, 1, DEPTH, "m")
    (dys, dh2_glu), (grads['s5_d'], *dglu, dln_g[2], dln_b[2]) = _stage_bwd(
        "glu_bwd", _f_glu, (ys, h2), glu_params, (dh3,), tbb, proxied=(1, 2, 3, 4))
    grads['s5_w_glu'] = jnp.stack(dglu)
    dcc = _s5_weight_grad("s5_dcc", dys, st, True, tk5)
    dbu, dh2_bu, dabar = _s5_bwd(dys, st, abar, cc, bc, tbs)
    dbc = _s5_weight_grad("s5_dbc", h2, dbu, False, tk5)
    gp = ng * SSM_STATE
    grads['s5_c_re'] = _compact_c_t(dcc[:gp], ng)
    grads['s5_c_im'] = -_compact_c_t(dcc[gp:], ng)
    zoh_couts = (dabar[:, :gp].reshape(ng, SSM_STATE), dabar[:, gp:].reshape(ng, SSM_STATE),
                 _compact_b_t(dbc[:, :gp], ng), _compact_b_t(dbc[:, gp:], ng))
    grads['s5_a_re'], grads['s5_a_im'], grads['s5_log_dt'], db_re_t, db_im_t = _zoh_bwd(
        a_re, a_im, log_dt, b_re_t, b_im_t, zoh_couts)
    grads['s5_b_re'], grads['s5_b_im'] = db_re_t.transpose(0, 2, 1), db_im_t.transpose(0, 2, 1)
    dh2 = (dh2_glu, dh2_bu)

    dh1, ds0, dhid0, act0, dln_g[1], dln_b[1] = _mlp_bwd("mlp0_bwd", h1, s_mlp0, dh2, fw['mlp_w1'], fw['mlp_w2'], 0,
                                                         ln_g[1], ln_b[1], tbmb)
    grads['mlp_w1'] = _mlp_weight_grad("mlp0_dw1", h1, dhid0, 0, DEPTH, "n", into=dw1)
    grads['mlp_w2'] = _mlp_weight_grad("mlp0_dw2", act0, ds0, 0, DEPTH, "m", into=dw2)
    (do, dr_p, dk2_p, dv_p, dgate, dx_post), post_g = _stage_bwd("post_bwd", _f_post, post_acts, post_params, (dh1,), tbb,
                                                                 proxied=(3,))
    grads['rw_lnx_g'], grads['rw_lnx_b'], grads['rw_r_k'], grads['rw_wo'], dln_g[0], dln_b[0] = post_g
    ready_sums = _chip_sums("a", [grads[n] for n in BIG_READY], core)
    rec_g, ready_lands = _rec_bwd(*rec_in, s0s, do, hg, ready_sums)
    reduced = dict(zip(BIG_READY, zip(ready_sums, ready_lands)))
    dr_r, dlw, dk2_r, dv_r, dan, dbb = rec_g
    dk2 = (dk2_p, dk2_r)
    (dx_l, dxp_l, dkraw_l), lora_g = _stage_bwd("lora_bwd", _f_lora, (x, xp, raw['k']), lora_params,
                                                (dlw, dk2, dan, dbb, dgate), tbb)
    (dmu_w, dmu_a, dmu_g, grads['rw_w0'], grads['rw_w1'], grads['rw_w2'], grads['rw_a0'], grads['rw_a1'], grads['rw_a2'],
     grads['rw_g1'], grads['rw_g2'], grads['rw_k_k'], grads['rw_k_a']) = lora_g
    dproj = {'r': (dr_p, dr_r), 'k': dkraw_l, 'v': (dv_p, dv_r)}
    dxs, dxps, dmu = [dx_post, dx_l], [dxp_l], {}
    for n in 'rkv':
        (dx_n, dxp_n), (dmu[n], grads['rw_w' + n]) = _stage_bwd("proj_bwd_" + n, _f_proj, (x, xp), proj_params[n],
                                                                 (dproj[n],), tb, proxied=(1,))
        dxs.append(dx_n)
        dxps.append(dxp_n)
    grads['rw_mu'] = jnp.concatenate([dmu['r'], dmu_w, dmu['k'], dmu['v'], dmu_a, dmu_g], axis=0)
    grads['ln_g'] = jnp.concatenate(dln_g, axis=0)
    grads['ln_b'] = jnp.concatenate(dln_b, axis=0)
    grad_x = _addn("grad_x", dxs + [_shift_up(_addn("dxp_sum", dxps))])
    late = [n for n in BIG if n not in BIG_READY]
    late_sums = _chip_sums("b", [grads[n] for n in late], core)
    reduced.update(zip(late, zip(late_sums, _scatter_big(late_sums))))
    return loss_blk, grad_x, grads, reduced


def _position():
    return lax.axis_index("x"), lax.axis_index("y"), lax.axis_index("c")


def _other_chips(x, y):
    return [(1 - x, y), (x, 1 - y), (1 - x, 1 - y)]


def _chip_slice(ref, axis, q, size):
    idx = [slice(None)] * len(ref.shape)
    idx[axis] = pl.ds(pl.multiple_of(q * size, size), size)
    return ref.at[tuple(idx)]


_HBM = pl.BlockSpec(memory_space=pltpu.HBM)


def _gather_small_phases(src, dst, axes, sems):
    n = len(src)
    send_sems, recv_sems, own_sems = sems
    x, y, c = _position()
    chips = _other_chips(x, y)
    sizes = [src[a].shape[axes[a]] for a in range(n)]

    def copy(a, k, q):
        return pltpu.make_async_remote_copy(
            src_ref=src[a], dst_ref=_chip_slice(dst[a], axes[a], q, sizes[a]), send_sem=send_sems.at[a, k],
            recv_sem=recv_sems.at[a, k], device_id=(*chips[k], c), device_id_type=MESH)

    def own(a):
        return pltpu.make_async_copy(src[a], _chip_slice(dst[a], axes[a], 2 * x + y, sizes[a]), own_sems.at[a])

    def start():
        for a in range(n):
            own(a).start()
            for k in range(3):
                copy(a, k, 2 * x + y).start()

    def finish():
        for a in range(n):
            for k, (cx, cy) in enumerate(chips):
                copy(a, k, 2 * cx + cy).wait_recv()
        for a in range(n):
            for k in range(3):
                copy(a, k, 2 * x + y).wait_send()
            own(a).wait()

    return start, finish


def _gather_early(big, small, axes):
    nb, ns = len(big), len(small)
    full_shapes = [tuple(s * 4 if i == ax else s for i, s in enumerate(a.shape)) for a, ax in zip(small, axes)]

    def body(*refs):
        src_b, src_s = refs[:nb], refs[nb:nb + ns]
        dst_b, dst_s = refs[nb + ns:2 * nb + ns], refs[2 * nb + ns:2 * (nb + ns)]
        sems = refs[2 * (nb + ns):]
        small_start, small_finish = _gather_small_phases(src_s, dst_s, axes, sems[5:])
        small_start()
        for phase in _gather_big_phases(src_b, dst_b, sems[:5]):
            phase()
        small_finish()

    outs = pl.pallas_call(
        body, name="gather_early", in_specs=[_HBM] * (nb + ns), out_specs=[_HBM] * (nb + ns),
        out_shape=[_sds((4,) + a.shape, a.dtype) for a in big] + [_sds(s, a.dtype) for s, a in zip(full_shapes, small)],
        scratch_shapes=_gather_big_sems(nb) + [pltpu.SemaphoreType.DMA((ns, 3)), pltpu.SemaphoreType.DMA((ns, 3)),
                                               pltpu.SemaphoreType.DMA((ns,))],
        compiler_params=_cparams(),
    )(*big, *small)
    return outs[:nb], outs[nb:]


def _scatter_pieces(fulls, axes):
    n = len(fulls)
    sizes = [a.shape[ax] // 4 for a, ax in zip(fulls, axes)]
    shard_shapes = [tuple(sz if i == ax else s for i, s in enumerate(a.shape)) for a, ax, sz in zip(fulls, axes, sizes)]

    def body(*refs):
        src, land = refs[:n], refs[n:2 * n]
        send_sems, recv_sems = refs[2 * n:]
        x, y, c = _position()
        chips = _other_chips(x, y)

        def copy(a, k):
            cx, cy = chips[k]
            return pltpu.make_async_remote_copy(
                src_ref=_chip_slice(src[a], axes[a], 2 * cx + cy, sizes[a]), dst_ref=land[a].at[k],
                send_sem=send_sems.at[a, k], recv_sem=recv_sems.at[a, k], device_id=(cx, cy, c), device_id_type=MESH)

        for a in range(n):
            for k in range(3):
                copy(a, k).start()
        for a in range(n):
            for k in range(3):
                copy(a, k).wait_recv()
        for a in range(n):
            for k in range(3):
                copy(a, k).wait_send()

    return pl.pallas_call(
        body, name="scatter_grads", in_specs=[_HBM] * n, out_specs=[_HBM] * n,
        out_shape=[_sds((3,) + s) for s in shard_shapes],
        scratch_shapes=[pltpu.SemaphoreType.DMA((n, 3)), pltpu.SemaphoreType.DMA((n, 3))],
        compiler_params=_cparams(),
    )(*fulls)


def _sibling_swap(name, arrs):
    n = len(arrs)

    def body(*refs):
        src, dst = refs[:n], refs[n:2 * n]
        send_sems, recv_sems = refs[2 * n:]
        x, y, c = _position()
        copies = [pltpu.make_async_remote_copy(src_ref=src[a], dst_ref=dst[a], send_sem=send_sems.at[a], recv_sem=recv_sems.at[a],
                                               device_id=(x, y, 1 - c), device_id_type=MESH) for a in range(n)]
        for cp in copies:
            cp.start()
        for cp in copies:
            cp.wait_recv()
        for cp in copies:
            cp.wait_send()

    return pl.pallas_call(
        body, name=name, in_specs=[_HBM] * n, out_specs=[_HBM] * n, out_shape=[_sds(a.shape) for a in arrs],
        scratch_shapes=[pltpu.SemaphoreType.DMA((n,)), pltpu.SemaphoreType.DMA((n,))],
        compiler_params=_cparams(),
    )(*arrs)


def _sum4(name, own, land):
    rows, cols = own.shape
    tb = _rows_tile(rows)

    def body(o_ref, l0, l1, l2, out_ref):
        out_ref[...] = ((o_ref[...] + l0[...]) + l1[...]) + l2[...]

    blk = pl.BlockSpec((tb, cols), lambda i: (i, 0))
    lands = [pl.BlockSpec((None, tb, cols), functools.partial(lambda k, i: (k, i, 0), k)) for k in range(3)]
    return pl.pallas_call(body, name=name, grid=(rows // tb,), in_specs=[blk] + lands, out_specs=blk,
                          out_shape=_sds((rows, cols)), compiler_params=_cparams(("parallel",)))(own, land, land, land)


def _allreduce_adamw_small(g, w, m, v):
    rows, lanes = g.shape

    def body(g_ref, w_ref, m_ref, v_ref, gs_ref, d_ref, mn_ref, vn_ref, land, send_sems, recv_sems):
        x, y, c = _position()
        me = 4 * x + 2 * y + c
        masks = [(bx, by, bc) for bx in (0, 1) for by in (0, 1) for bc in (0, 1)][1:]

        def peer(mask):
            return (x ^ mask[0], y ^ mask[1], c ^ mask[2])

        def copy(j, slot):
            return pltpu.make_async_remote_copy(src_ref=g_ref, dst_ref=land.at[slot], send_sem=send_sems.at[j],
                                                recv_sem=recv_sems.at[j], device_id=peer(masks[j]), device_id_type=MESH)

        for j in range(7):
            copy(j, me).start()
        land[me] = g_ref[...]
        for j in range(7):
            px, py, pc = peer(masks[j])
            copy(j, 4 * px + 2 * py + pc).wait_recv()
        for j in range(7):
            copy(j, me).wait_send()
        total = land[0]
        for dev in range(1, 8):
            total = total + land[dev]
        delta, mn, vn = _adamw_math(w_ref[...], total, m_ref[...], v_ref[...])
        gs_ref[...] = total
        d_ref[...] = delta
        mn_ref[...] = mn
        vn_ref[...] = vn

    vmem = pl.BlockSpec(memory_space=pltpu.VMEM)
    return pl.pallas_call(
        body, name="allreduce_adamw_small", in_specs=[vmem] * 4, out_specs=[vmem] * 4, out_shape=[_sds((rows, lanes))] * 4,
        scratch_shapes=[pltpu.VMEM((8, rows, lanes), F32), pltpu.SemaphoreType.DMA((7,)), pltpu.SemaphoreType.DMA((7,))],
        compiler_params=_cparams(),
    )(g, w, m, v)


def _row_half(ref, c):
    r2 = ref.shape[-2] // 2
    lead = (slice(None),) * (len(ref.shape) - 2)
    return ref.at[(*lead, pl.ds(pl.multiple_of(c * r2, r2), r2), slice(None))]


def _gather_big_phases(src, dst, sems):
    n = len(src)
    ici_send, ici_recv, d2d_send, d2d_recv, own_sems = sems
    x, y, c = _position()
    me = 2 * x + y
    chips = _other_chips(x, y)
    ids = [2 * cx + cy for cx, cy in chips]

    def ici(a, k, q):
        return pltpu.make_async_remote_copy(
            src_ref=_row_half(src[a], c), dst_ref=_row_half(dst[a].at[q], c), send_sem=ici_send.at[a, k],
            recv_sem=ici_recv.at[a, k], device_id=(*chips[k], c), device_id_type=MESH)

    def d2d(a, k, half):
        where = _row_half(dst[a].at[ids[k]], half)
        return pltpu.make_async_remote_copy(src_ref=where, dst_ref=where, send_sem=d2d_send.at[a, k], recv_sem=d2d_recv.at[a, k],
                                            device_id=(x, y, 1 - c), device_id_type=MESH)

    def own(a):
        return pltpu.make_async_copy(src[a], dst[a].at[me], own_sems.at[a])

    def start():
        for a in range(n):
            own(a).start()
            for k in range(3):
                ici(a, k, me).start()

    def forward():
        for a in range(n):
            for k in range(3):
                ici(a, k, ids[k]).wait_recv()
                d2d(a, k, c).start()

    def finish():
        for a in range(n):
            for k in range(3):
                d2d(a, k, 1 - c).wait_recv()
        for a in range(n):
            for k in range(3):
                ici(a, k, me).wait_send()
                d2d(a, k, c).wait_send()
            own(a).wait()

    return start, forward, finish


def _gather_big_sems(n):
    return [pltpu.SemaphoreType.DMA((n, 3))] * 4 + [pltpu.SemaphoreType.DMA((n,))]


def _chip_sums(tag, grads, core):
    views = [g.reshape(4, -1, g.shape[-1]) for g in grads]
    others = _sibling_halves("sibling_halves_" + tag, views)
    return [_half_add(f"half_add_{tag}{i}", v, o, core) for i, (v, o) in enumerate(zip(views, others))]


def _sibling_halves(name, views):
    n = len(views)

    def body(*refs):
        src, dst = refs[:n], refs[n:2 * n]
        send_sems, recv_sems = refs[2 * n:]
        x, y, c = _position()
        copies = [pltpu.make_async_remote_copy(src_ref=_row_half(src[a], 1 - c), dst_ref=dst[a], send_sem=send_sems.at[a],
                                               recv_sem=recv_sems.at[a], device_id=(x, y, 1 - c), device_id_type=MESH)
                  for a in range(n)]
        for cp in copies:
            cp.start()
        for cp in copies:
            cp.wait_recv()
        for cp in copies:
            cp.wait_send()

    return pl.pallas_call(
        body, name=name, in_specs=[_HBM] * n, out_specs=[_HBM] * n,
        out_shape=[_sds((4, v.shape[1] // 2, v.shape[2])) for v in views],
        scratch_shapes=[pltpu.SemaphoreType.DMA((n,)), pltpu.SemaphoreType.DMA((n,))], compiler_params=_cparams(),
    )(*views)


def _rows_tile_capped(rows, cap=256):
    return min(_rows_tile(rows), cap)


def _half_add(name, view, other, core):
    _, r, k = view.shape
    r2 = r // 2
    tr = _rows_tile_capped(r2)
    per = r2 // tr

    def body(c_ref, v_ref, o_ref, out_ref):
        out_ref[...] = (v_ref[...] + o_ref[...]).astype(BF16)

    blk = pl.BlockSpec((None, tr, k), lambda q, i, c: (q, i, 0))
    return pl.pallas_call(
        body, name=name,
        grid_spec=pltpu.PrefetchScalarGridSpec(
            num_scalar_prefetch=1, grid=(4, per),
            in_specs=[pl.BlockSpec((None, tr, k), lambda q, i, c: (q, c[0] * per + i, 0)), blk], out_specs=blk),
        out_shape=_sds((4, r2, k), BF16), compiler_params=_cparams(("parallel", "parallel")),
    )(core, view, other)


def _scatter_big_phases(src, land, sems):
    n = len(src)
    send_sems, recv_sems = sems
    x, y, c = _position()
    chips = _other_chips(x, y)

    def copy(a, k):
        cx, cy = chips[k]
        return pltpu.make_async_remote_copy(src_ref=src[a].at[2 * cx + cy], dst_ref=land[a].at[k], send_sem=send_sems.at[a, k],
                                            recv_sem=recv_sems.at[a, k], device_id=(cx, cy, c), device_id_type=MESH)

    def start():
        for a in range(n):
            for k in range(3):
                copy(a, k).start()

    def finish():
        for a in range(n):
            for k in range(3):
                copy(a, k).wait_recv()
        for a in range(n):
            for k in range(3):
                copy(a, k).wait_send()

    return start, finish


def _scatter_big_sems(n):
    return [pltpu.SemaphoreType.DMA((n, 3)), pltpu.SemaphoreType.DMA((n, 3))]


def _scatter_big(sums):
    n = len(sums)

    def body(*refs):
        for phase in _scatter_big_phases(refs[:n], refs[n:2 * n], refs[2 * n:]):
            phase()

    return pl.pallas_call(
        body, name="scatter_big", in_specs=[_HBM] * n, out_specs=[_HBM] * n,
        out_shape=[_sds((3,) + s.shape[1:], s.dtype) for s in sums], scratch_shapes=_scatter_big_sems(n), compiler_params=_cparams(),
    )(*sums)


def _sum4_big(name, sums, land, chip):
    _, r2, k = sums.shape
    tr = _rows_tile_capped(r2)

    def body(q_ref, s_ref, l0, l1, l2, out_ref):
        out_ref[...] = ((s_ref[...].astype(F32) + l0[...].astype(F32)) + l1[...].astype(F32)) + l2[...].astype(F32)

    lands = [pl.BlockSpec((None, tr, k), functools.partial(lambda j, i, q: (j, i, 0), j)) for j in range(3)]
    return pl.pallas_call(
        body, name=name,
        grid_spec=pltpu.PrefetchScalarGridSpec(
            num_scalar_prefetch=1, grid=(r2 // tr,),
            in_specs=[pl.BlockSpec((None, tr, k), lambda i, q: (q[0], i, 0))] + lands,
            out_specs=pl.BlockSpec((tr, k), lambda i, q: (i, 0))),
        out_shape=_sds((r2, k)), compiler_params=_cparams(("parallel",)),
    )(chip, sums, land, land, land)


def _adamw_halves(name, mine, theirs, w, m, v, core):
    r, k = w.shape
    r2 = r // 2
    tr = _rows_tile_capped(r2, 512)
    per = r2 // tr

    def body(c_ref, mine_ref, theirs_ref, w_ref, m_ref, v_ref, g_out, d_out, m_out, v_out):
        g = jnp.where(pl.program_id(0) == c_ref[0], mine_ref[...], theirs_ref[...])
        delta, mn, vn = _adamw_math(w_ref[...], g, m_ref[...], v_ref[...])
        g_out[...] = g
        d_out[...] = delta
        m_out[...] = mn
        v_out[...] = vn

    half = pl.BlockSpec((tr, k), lambda h, i, c: (i, 0))
    full = pl.BlockSpec((tr, k), lambda h, i, c: (h * per + i, 0))
    return pl.pallas_call(
        body, name=name,
        grid_spec=pltpu.PrefetchScalarGridSpec(num_scalar_prefetch=1, grid=(2, per), in_specs=[half, half, full, full, full],
                                               out_specs=[full] * 4),
        out_shape=[_sds((r, k))] * 4, compiler_params=_cparams(("parallel", "parallel")),
    )(core, mine, theirs, w, m, v)


def _drops_layer_axis(name):
    return not (name.startswith('mlp') or name == 's5_d')


def _work(name, arr):
    return arr.reshape(arr.shape[1:]) if _drops_layer_axis(name) else arr


def _work_axis(name):
    return SHARD_AXIS[name] - (1 if _drops_layer_axis(name) else 0)


def _as2d(a):
    return a.reshape(-1, a.shape[-1])


def _replicated_2d(name, arr):
    if name in ('ln_g', 'ln_b'):
        return arr
    if name == 'rw_r_k':
        return arr.reshape(1, -1)
    if name == 's5_log_dt':
        return arr.reshape(-1, 1)
    if name.startswith('s5_'):
        return arr.reshape(arr.shape[1:])
    return arr


def _pack(arrs):
    flat = []
    for a in arrs:
        f = a.reshape(-1)
        flat.append(jnp.pad(f, (0, -f.shape[0] % 128)))
    f = jnp.concatenate(flat)
    f = jnp.pad(f, (0, -f.shape[0] % 1024))
    return f.reshape(-1, 128)


def _unpack(packed, shapes):
    flat = packed.reshape(-1)
    out, at = [], 0
    for s in shapes:
        size = math.prod(s)
        out.append(flat[at:at + size].reshape(s))
        at += size + (-size % 128)
    return out


def kernel(x, ln_g, ln_b, rw_mu, rw_w0, rw_w1, rw_w2, rw_a0, rw_a1, rw_a2, rw_g1, rw_g2, rw_k_k, rw_k_a, rw_r_k, rw_wr, rw_wk, rw_wv, rw_wo, rw_lnx_g, rw_lnx_b, s5_a_re, s5_a_im, s5_log_dt, s5_b_re, s5_b_im, s5_c_re, s5_c_im, s5_d, s5_w_glu, mlp_w1, mlp_w2, loss_target, m_ln_g, m_ln_b, m_rw_mu, m_rw_w0, m_rw_w1, m_rw_w2, m_rw_a0, m_rw_a1, m_rw_a2, m_rw_g1, m_rw_g2, m_rw_k_k, m_rw_k_a, m_rw_r_k, m_rw_wr, m_rw_wk, m_rw_wv, m_rw_wo, m_rw_lnx_g, m_rw_lnx_b, m_s5_a_re, m_s5_a_im, m_s5_log_dt, m_s5_b_re, m_s5_b_im, m_s5_c_re, m_s5_c_im, m_s5_d, m_s5_w_glu, m_mlp_w1, m_mlp_w2, v_ln_g, v_ln_b, v_rw_mu, v_rw_w0, v_rw_w1, v_rw_w2, v_rw_a0, v_rw_a1, v_rw_a2, v_rw_g1, v_rw_g2, v_rw_k_k, v_rw_k_a, v_rw_r_k, v_rw_wr, v_rw_wk, v_rw_wv, v_rw_wo, v_rw_lnx_g, v_rw_lnx_b, v_s5_a_re, v_s5_a_im, v_s5_log_dt, v_s5_b_re, v_s5_b_im, v_s5_c_re, v_s5_c_im, v_s5_d, v_s5_w_glu, v_mlp_w1, v_mlp_w2):
    d = dict(locals())
    x_pos, y_pos, c_pos = _position()
    chip = 2 * x_pos + y_pos
    chip_arr = jnp.reshape(chip, (1,)).astype(jnp.int32)
    core_arr = jnp.reshape(c_pos, (1,)).astype(jnp.int32)

    small = [n for n in SHARD_AXIS if n not in BIG]
    axes = [_work_axis(n) for n in small]
    big_views, small_fulls = _gather_early([_as2d(d[n]).astype(BF16) for n in BIG_EARLY], [_work(n, d[n]) for n in small], axes)
    views = dict(zip(BIG_EARLY, big_views))
    fw = dict(zip(small, small_fulls))
    c_model = d['x'].shape[-1]
    for n in BIG_EARLY:
        fw[n] = views[n].reshape(c_model, c_model)
    fw['late_shards'] = [_as2d(d[n]).astype(BF16) for n in BIG_LATE]
    for n in REPLICATED:
        fw[n] = _replicated_2d(n, d[n])

    loss_blk, grad_x, grads, reduced = _local_step(d['x'][0], d['loss_target'][0], fw, core_arr)
    loss = lax.psum(loss_blk[0, 0], ('x', 'y', 'c'))
    out = {}

    mine = [_sum4_big("sum4_" + n, *reduced[n], chip_arr) for n in BIG]
    theirs = _sibling_swap("swap_big", mine)
    for n, a, b in zip(BIG, mine, theirs):
        res = _adamw_halves("adamw_" + n, a, b, _as2d(d[n]), _as2d(d['m_' + n]), _as2d(d['v_' + n]), core_arr)
        out[n] = [r.reshape(d[n].shape) for r in res]

    pieces = [grads[n] for n in small]
    lands = _scatter_pieces(pieces, axes)
    mine = []
    for n, g, ax, land in zip(small, pieces, axes, lands):
        size = g.shape[ax] // 4
        mine.append(_sum4("sum4_" + n, lax.dynamic_slice_in_dim(g, chip * size, size, ax), land))
    theirs = _sibling_swap("swap_small", mine)
    for n, a, b in zip(small, mine, theirs):
        res = _adamw("adamw_" + n, (a, b), _as2d(d[n]), _as2d(d['m_' + n]), _as2d(d['v_' + n]))
        out[n] = [r.reshape(d[n].shape) for r in res]

    rep_shapes = [d[n].shape for n in REPLICATED]
    packs = [_pack([grads[n] for n in REPLICATED])] + [_pack([d[p + n] for n in REPLICATED]) for p in ('', 'm_', 'v_')]
    res = [_unpack(p, rep_shapes) for p in _allreduce_adamw_small(*packs)]
    for i, n in enumerate(REPLICATED):
        out[n] = [r[i] for r in res]

    grad_x = grad_x.reshape(d['x'].shape)
    return (loss, grad_x, *[out[n][0] for n in WEIGHTS], *[out[n][1] for n in WEIGHTS],
            *[out[n][2] for n in WEIGHTS], *[out[n][3] for n in WEIGHTS])
```

```python
import functools
import math

import jax
import jax.numpy as jnp
from jax import lax
from jax.experimental import pallas as pl
from jax.experimental.pallas import tpu as pltpu

F32 = jnp.float32
BF16 = jnp.bfloat16
MESH = pl.DeviceIdType.MESH

HEAD = 64
SSM_GROUP = 16
SSM_STATE = 64
GN_EPS = 64e-5
LN_EPS = 1e-5
DEPTH = 2
DN_ALPHA = (2.0 * DEPTH) ** 0.25
ADAM_LR, ADAM_B1, ADAM_B2, ADAM_EPS, ADAM_WD, ADAM_STEP = 0.001, 0.9, 0.999, 1e-08, 0.01, 10
REC_CHUNK = 64
V7X_VMEM_BYTES = 64 * 2 ** 20
VMEM_LIMIT = V7X_VMEM_BYTES - 8 * 2 ** 20

WEIGHTS = ['ln_g', 'ln_b', 'rw_mu', 'rw_w0', 'rw_w1', 'rw_w2', 'rw_a0', 'rw_a1', 'rw_a2', 'rw_g1', 'rw_g2',
           'rw_k_k', 'rw_k_a', 'rw_r_k', 'rw_wr', 'rw_wk', 'rw_wv', 'rw_wo', 'rw_lnx_g', 'rw_lnx_b',
           's5_a_re', 's5_a_im', 's5_log_dt', 's5_b_re', 's5_b_im', 's5_c_re', 's5_c_im', 's5_d', 's5_w_glu',
           'mlp_w1', 'mlp_w2']
SHARD_AXIS = {'rw_mu': 2, 'rw_w1': 1, 'rw_w2': 2, 'rw_a1': 1, 'rw_a2': 2, 'rw_g1': 1, 'rw_g2': 2,
              'rw_wr': 1, 'rw_wk': 1, 'rw_wv': 1, 'rw_wo': 1, 's5_d': 1, 's5_w_glu': 2, 'mlp_w1': 2, 'mlp_w2': 1}
REPLICATED = [n for n in WEIGHTS if n not in SHARD_AXIS]
BIG_EARLY = ['rw_wr', 'rw_wk', 'rw_wv']
BIG_LATE = ['rw_wo', 's5_w_glu', 'mlp_w1', 'mlp_w2']
BIG = BIG_EARLY + BIG_LATE
BIG_READY = ['rw_wo', 's5_w_glu', 'mlp_w1', 'mlp_w2']


def _sds(shape, dtype=F32):
    return jax.ShapeDtypeStruct(tuple(shape), dtype)


def _cparams(sem=None, **kw):
    if sem is not None:
        kw["dimension_semantics"] = sem
    return pltpu.CompilerParams(vmem_limit_bytes=VMEM_LIMIT, **kw)


def _mm_products(a, b, g):
    gb = g.astype(BF16)
    da = lax.dot_general(gb, b.astype(BF16), (((1,), (1,)), ((), ())), preferred_element_type=F32)
    db = lax.dot_general(a.astype(BF16), gb, (((0,), (0,)), ((), ())), preferred_element_type=F32)
    return da, db


@jax.custom_vjp
def _mm_plain(a, b):
    return jnp.dot(a.astype(BF16), b.astype(BF16), preferred_element_type=F32)


def _mm_plain_bwd(res, g):
    da, db = _mm_products(*res, g)
    return da.astype(res[0].dtype), db.astype(res[1].dtype)


_mm_plain.defvjp(lambda a, b: (_mm_plain(a, b), (a, b)), _mm_plain_bwd)


@jax.custom_vjp
def _mm_proxy(a, b, z):
    return jnp.dot(a.astype(BF16), b.astype(BF16), preferred_element_type=F32)


def _mm_proxy_bwd(res, g):
    da, db = _mm_products(*res, g)
    return da.astype(res[0].dtype), jnp.zeros_like(res[1]), db


_mm_proxy.defvjp(lambda a, b, z: (_mm_proxy(a, b, z), (a, b)), _mm_proxy_bwd)


def mm(a, b, z=None):
    return _mm_plain(a, b) if z is None else _mm_proxy(a, b, z)


def _split3(x):
    hi = x.astype(BF16)
    r1 = x - hi.astype(F32)
    mid = r1.astype(BF16)
    lo = (r1 - mid.astype(F32)).astype(BF16)
    return hi, mid, lo


def _head_sum_impl(x):
    c = x.shape[1]
    lanes = 128
    sel = (lax.broadcasted_iota(jnp.int32, (c, lanes), 0) // HEAD
           == lax.broadcasted_iota(jnp.int32, (c, lanes), 1)).astype(BF16)
    s = sum(jnp.dot(p, sel, preferred_element_type=F32) for p in _split3(x))
    return sum(lax.dot_general(p, sel, (((1,), (1,)), ((), ())), preferred_element_type=F32) for p in _split3(s))


@jax.custom_vjp
def head_sum(x):
    return _head_sum_impl(x)


head_sum.defvjp(lambda x: (_head_sum_impl(x), None), lambda _, g: (_head_sum_impl(g),))


def _ln(x, g, b):
    mu = jnp.mean(x, axis=-1, keepdims=True)
    xc = x - mu
    var = jnp.mean(xc * xc, axis=-1, keepdims=True)
    return xc * lax.rsqrt(var + LN_EPS) * g + b


def _f_proj(acts, params, proxies):
    x, xp = acts
    mu, w = params
    return (mm(x + (xp - x) * mu, w, proxies[1]),)


def _f_lora(acts, params, proxies):
    x, xp, kraw = acts
    mu_w, mu_a, mu_g, w0, w1, w2, a0, a1, a2, g1, g2, k_k, k_a = params
    xx = xp - x
    w_pre = w0 + mm(jnp.tanh(mm(x + xx * mu_w, w1)), w2)
    z = -w_pre
    softplus = jnp.maximum(z, 0.0) + jnp.log(1.0 + jnp.exp(-jnp.abs(z)))
    log_decay = -jnp.exp(-softplus - 0.5)
    a = jax.nn.sigmoid(a0 + mm(mm(x + xx * mu_a, a1), a2))
    g = mm(jax.nn.sigmoid(mm(x + xx * mu_g, g1)), g2)
    kk = kraw * k_k
    kkn = kk / jnp.maximum(jnp.sqrt(head_sum(kk * kk)), 1e-12)
    k2 = kraw * (1.0 + (a - 1.0) * k_a)
    return log_decay, k2, -kkn, kkn * a, g


def _f_post(acts, params, proxies):
    o, r, k2, v, g, x = acts
    lnx_g, lnx_b, r_k, wo, ln_g, ln_b = params
    om = head_sum(o) * (1.0 / HEAD)
    oc = o - om
    ov = head_sum(oc * oc) * (1.0 / HEAD)
    on = oc * lax.rsqrt(ov + GN_EPS) * lnx_g + lnx_b
    bonus = head_sum(r * k2 * r_k) * v
    y = mm((on + bonus) * g, wo, proxies[3])
    return (_ln(DN_ALPHA * x + y, ln_g, ln_b),)


def _f_glu(acts, params, proxies):
    ys, h = acts
    d, wv0, wv1, wg0, wg1, ln_g, ln_b = params
    y = jax.nn.gelu(ys + h * d)
    mix = jnp.concatenate([mm(y, wv0, proxies[1]) * jax.nn.sigmoid(mm(y, wg0, proxies[3])),
                           mm(y, wv1, proxies[2]) * jax.nn.sigmoid(mm(y, wg1, proxies[4]))], axis=1)
    return (_ln(DN_ALPHA * h + mix, ln_g, ln_b),)


def _f_zoh(a_re, a_im, log_dt, b_re_t, b_im_t):
    dt = jnp.exp(log_dt)
    lam_re = jnp.minimum(a_re, -1e-4)
    lam_im = a_im
    mag = jnp.exp(dt * lam_re)
    abar_re = mag * jnp.cos(dt * lam_im)
    abar_im = mag * jnp.sin(dt * lam_im)
    den = lam_re * lam_re + lam_im * lam_im
    nr, ni = abar_re - 1.0, abar_im
    coef_re = ((nr * lam_re + ni * lam_im) / den)[:, None, :]
    coef_im = ((ni * lam_re - nr * lam_im) / den)[:, None, :]
    return (abar_re, abar_im, coef_re * b_re_t - coef_im * b_im_t, coef_re * b_im_t + coef_im * b_re_t)


def _bdot16_raw(a, b, ca, cb):
    return lax.dot_general(a.astype(BF16), b.astype(BF16), (((ca,), (cb,)), ((0,), (0,))), preferred_element_type=F32)


@functools.partial(jax.custom_vjp, nondiff_argnums=(2, 3))
def _bdot16(a, b, ca, cb):
    return _bdot16_raw(a, b, ca, cb)


def _bdot16_bwd(ca, cb, res, g):
    a, b = res
    if (ca, cb) == (2, 1):
        return _bdot16_raw(g, b, 2, 2), _bdot16_raw(a, g, 1, 1)
    if (ca, cb) == (2, 2):
        return _bdot16_raw(g, b, 2, 1), _bdot16_raw(g, a, 1, 1)
    assert (ca, cb) == (1, 1)
    return _bdot16_raw(b, g, 2, 2), _bdot16_raw(a, g, 2, 1)


_bdot16.defvjp(lambda a, b, ca, cb: (_bdot16_raw(a, b, ca, cb), (a, b)), _bdot16_bwd)

def _time_sums(x, suffix):
    hg, ln, _ = x.shape
    row = lax.broadcasted_iota(jnp.int32, (hg, ln, ln), 1)
    col = lax.broadcasted_iota(jnp.int32, (hg, ln, ln), 2)
    tri = ((row <= col) if suffix else (row >= col)).astype(BF16)
    return sum(lax.dot_general(tri, p, (((2,), (1,)), ((0,), (0,))), preferred_element_type=F32) for p in _split3(x))


@jax.custom_vjp
def _time_cumsum(x):
    return _time_sums(x, False)


_time_cumsum.defvjp(lambda x: (_time_sums(x, False), None), lambda _, g: (_time_sums(g, True),))

_dot_score = _bdot16
_dot_inverse = _bdot16
_dot_value = _bdot16


def _rec_chunk(s0, r, lw, k, v, a, b):
    hg, ln, _ = r.shape
    row = lax.broadcasted_iota(jnp.int32, (hg, ln, ln), 1)
    col = lax.broadcasted_iota(jnp.int32, (hg, ln, ln), 2)
    incl, strict = row >= col, row > col
    cum = _time_cumsum(lw)
    total = jnp.sum(lw, axis=1, keepdims=True)
    e_cum, e_inv, e_prev, e_tail = jnp.exp(cum), jnp.exp(-cum), jnp.exp(cum - lw), jnp.exp(total - cum)
    rt, at, bt, kt = r * e_cum, a * e_prev, b * e_inv, k * e_inv
    aab = jnp.where(strict, _dot_score(at, bt, 2, 2), 0.0)
    aak = jnp.where(strict, _dot_score(at, kt, 2, 2), 0.0)
    arb = jnp.where(incl, _dot_score(rt, bt, 2, 2), 0.0)
    ark = jnp.where(incl, _dot_score(rt, kt, 2, 2), 0.0)
    p = (row == col).astype(F32) + aab
    m = aab
    for _ in range(int(math.log2(ln)) - 1):
        m = _dot_inverse(m, m, 2, 1)
        p = p + _dot_inverse(p, m, 2, 1)
    u = _dot_inverse(p, _dot_value(at, s0, 2, 2) + _dot_value(aak, v, 2, 1), 2, 1)
    o = _dot_value(rt, s0, 2, 2) + _dot_value(arb, u, 2, 1) + _dot_value(ark, v, 2, 1)
    s1 = s0 * jnp.exp(total) + _dot_value(u, b * e_tail, 1, 1) + _dot_value(v, k * e_tail, 1, 1)
    return o, s1


def _full_spec(shape):
    nd = len(shape)
    return pl.BlockSpec(tuple(shape), lambda *_: (0,) * nd)


def _stage_fwd(name, f, acts, params, out_dims, tb):
    t = acts[0].shape[0]
    na, npar = len(acts), len(params)

    def body(*refs):
        outs = f(tuple(r[...] for r in refs[:na]), tuple(r[...] for r in refs[na:na + npar]), (None,) * npar)
        for r, val in zip(refs[na + npar:], outs):
            r[...] = val

    return pl.pallas_call(
        body, name=name, grid=(t // tb,),
        in_specs=[pl.BlockSpec((tb, a.shape[1]), lambda i: (i, 0)) for a in acts] + [_full_spec(p.shape) for p in params],
        out_specs=[pl.BlockSpec((tb, d), lambda i: (i, 0)) for d in out_dims],
        out_shape=[_sds((t, d)) for d in out_dims],
        compiler_params=_cparams(("arbitrary",)),
    )(*acts, *params)


def _stage_bwd(name, f, acts, params, couts, tb, proxied=()):
    t = acts[0].shape[0]
    groups = [c if isinstance(c, tuple) else (c,) for c in couts]
    couts = [term for grp in groups for term in grp]
    na, npar, nc = len(acts), len(params), len(couts)
    steps = t // tb

    def f_diff(act_vals, diff_vals, param_vals):
        real = tuple(param_vals[i] if i in proxied else diff_vals[i] for i in range(npar))
        proxies = tuple(diff_vals[i] if i in proxied else None for i in range(npar))
        return f(act_vals, real, proxies)

    def body(*refs):
        a_refs, p_hbm, c_refs = refs[:na], refs[na:na + npar], refs[na + npar:na + npar + nc]
        o = na + npar + nc
        da_refs, dp_hbm = refs[o:o + na], refs[o + na:o + na + npar]
        p_buf, acc = refs[o + na + npar:o + na + 2 * npar], refs[o + na + 2 * npar:]
        i = pl.program_id(0)

        @pl.when(i == 0)
        def _():
            for src, dst in zip(p_hbm, p_buf):
                pltpu.sync_copy(src, dst)
            for r in acc:
                r[...] = jnp.zeros_like(r)

        param_vals = tuple(r[...] for r in p_buf)
        diff_vals = tuple(jnp.zeros(v.shape, F32) if i in proxied else v for i, v in enumerate(param_vals))
        _, vjp = jax.vjp(functools.partial(f_diff, param_vals=param_vals), tuple(r[...] for r in a_refs), diff_vals)
        terms = iter(c_refs)
        d_acts, d_params = vjp(tuple(functools.reduce(jnp.add, [next(terms)[...] for _ in grp]) for grp in groups))
        for r, val in zip(da_refs, d_acts):
            r[...] = val
        for r, val in zip(acc, d_params):
            r[...] += val

        @pl.when(i == steps - 1)
        def _():
            for src, dst in zip(acc, dp_hbm):
                pltpu.sync_copy(src, dst)

    hbm = pl.BlockSpec(memory_space=pltpu.HBM)
    outs = pl.pallas_call(
        body, name=name, grid=(steps,),
        in_specs=[pl.BlockSpec((tb, a.shape[1]), lambda i: (i, 0)) for a in acts] + [hbm] * npar
        + [pl.BlockSpec((tb, c.shape[1]), lambda i: (i, 0)) for c in couts],
        out_specs=[pl.BlockSpec((tb, a.shape[1]), lambda i: (i, 0)) for a in acts] + [hbm] * npar,
        out_shape=[_sds(a.shape) for a in acts] + [_sds(p.shape) for p in params],
        scratch_shapes=[pltpu.VMEM(p.shape, p.dtype) for p in params] + [pltpu.VMEM(p.shape, F32) for p in params],
        compiler_params=_cparams(("arbitrary",)),
    )(*acts, *params, *couts)
    return outs[:na], outs[na:]


def _tiled_matmul(name, a, b, mode, grid, a_spec, b_spec, o_spec, out_shape):
    nk = grid[2]
    dims = {"nn": ((1,), (0,)), "nt": ((1,), (1,)), "tn": ((0,), (0,))}[mode]

    def body(a_ref, b_ref, o_ref, acc):
        kk = pl.program_id(2)

        @pl.when(kk == 0)
        def _():
            acc[...] = jnp.zeros_like(acc)

        acc[...] += lax.dot_general(a_ref[...].astype(BF16), b_ref[...].astype(BF16), (dims, ((), ())),
                                    preferred_element_type=F32)

        @pl.when(kk == nk - 1)
        def _():
            o_ref[...] = acc[...]

    return pl.pallas_call(
        body, name=name, grid=grid, in_specs=[a_spec, b_spec], out_specs=o_spec, out_shape=_sds(out_shape),
        scratch_shapes=[pltpu.VMEM(o_spec.block_shape, F32)],
        compiler_params=_cparams(("parallel", "parallel", "arbitrary")),
    )(a, b)


def _mlp_weight_grad(name, a, b, layer, layers, split, into=None, tile=512):
    t, m = a.shape
    n = b.shape[1]
    tk = min(tile, t)
    tile = 2 * tile
    if split == "n":
        tm, tn = min(tile, m), min(tile, n // 4)
        per = n // 4 // tn
        shape = (4, layers, m, n // 4)
        o_idx = lambda i, j, k: (j // per, layer, i, j % per)
    else:
        tm, tn = min(tile, m // 4), min(tile, n)
        per = m // 4 // tm
        shape = (4, layers, m // 4, n)
        o_idx = lambda i, j, k: (i // per, layer, i % per, j)
    nk = t // tk

    def body(a_ref, b_ref, *rest):
        o_ref, acc = rest[-2:]
        kk = pl.program_id(2)

        @pl.when(kk == 0)
        def _():
            acc[...] = jnp.zeros_like(acc)

        acc[...] += lax.dot_general(a_ref[...].astype(BF16), b_ref[...].astype(BF16), (((0,), (0,)), ((), ())),
                                    preferred_element_type=F32)

        @pl.when(kk == nk - 1)
        def _():
            o_ref[...] = acc[...]

    in_specs = [pl.BlockSpec((tk, tm), lambda i, j, k: (k, i)), pl.BlockSpec((tk, tn), lambda i, j, k: (k, j))]
    operands = [a, b]
    aliases = {}
    if into is not None:
        in_specs.append(pl.BlockSpec(memory_space=pl.ANY))
        operands.append(into)
        aliases = {2: 0}
    return pl.pallas_call(
        body, name=name, grid=(m // tm, n // tn, nk), in_specs=in_specs,
        out_specs=pl.BlockSpec((None, None, tm, tn), o_idx), out_shape=_sds(shape), input_output_aliases=aliases,
        scratch_shapes=[pltpu.VMEM((tm, tn), F32)],
        compiler_params=_cparams(("parallel", "parallel", "arbitrary")),
    )(*operands)


S5_PACK = 8


def _s5_weight_grad(name, x, s, wide_rows, tk):
    t, c = x.shape
    wide = s.shape[1]
    kb, nb = S5_PACK * SSM_GROUP, S5_PACK * SSM_STATE
    nsb = c // kb
    x_spec = pl.BlockSpec((tk, kb), lambda i, j, k: (k, j % nsb))
    s_spec = pl.BlockSpec((tk, nb), lambda i, j, k: (k, j))
    if wide_rows:
        return _tiled_matmul(name, s, x, "tn", (1, wide // nb, t // tk), s_spec, x_spec,
                             pl.BlockSpec((nb, kb), lambda i, j, k: (j, 0)), (wide, kb))
    return _tiled_matmul(name, x, s, "tn", (1, wide // nb, t // tk), x_spec, s_spec,
                         pl.BlockSpec((kb, nb), lambda i, j, k: (0, j)), (kb, wide))


def _mlp_fwd(name, h, w1, w2, layer, ln_g, ln_b, tb, shards=()):
    t, c = h.shape
    nj, fc = w1.shape[0], w1.shape[3]
    nsh = len(shards)
    steps = (t // tb) * nj

    def body(h_ref, w1_ref, w2_ref, g_ref, b_ref, *rest):
        src, (out_ref, s_ref), dst = rest[:nsh], rest[nsh:nsh + 2], rest[nsh + 2:2 * nsh + 2]
        acc, sems = rest[2 * nsh + 2], rest[2 * nsh + 3:]
        j = pl.program_id(1)
        step = pl.program_id(0) * nj + j
        if nsh:
            start, forward, finish = _gather_big_phases(src, dst, sems)
            pl.when(step == 0)(start)

        @pl.when(j == 0)
        def _():
            acc[...] = jnp.zeros_like(acc)

        hid = jnp.dot(h_ref[...].astype(BF16), w1_ref[...].astype(BF16), preferred_element_type=F32)
        act = jnp.square(jnp.maximum(hid, 0.0))
        acc[...] += jnp.dot(act.astype(BF16), w2_ref[...].astype(BF16), preferred_element_type=F32)

        @pl.when(j == nj - 1)
        def _():
            s = DN_ALPHA * h_ref[...] + acc[...]
            s_ref[...] = s
            out_ref[...] = _ln(s, g_ref[...], b_ref[...])

        if nsh:
            pl.when(step == steps // 2)(forward)
            pl.when(step == steps - 1)(finish)

    row = pl.BlockSpec((tb, c), lambda i, j: (i, 0))
    vec = pl.BlockSpec((1, c), lambda i, j: (0, 0))
    outs = pl.pallas_call(
        body, name=name, grid=(t // tb, nj),
        in_specs=[row, pl.BlockSpec((None, None, c, fc), lambda i, j: (j, layer, 0, 0)),
                  pl.BlockSpec((None, None, fc, c), lambda i, j: (j, layer, 0, 0)), vec, vec] + [_HBM] * nsh,
        out_specs=[row, row] + [_HBM] * nsh,
        out_shape=[_sds((t, c)), _sds((t, c))] + [_sds((4,) + a.shape, a.dtype) for a in shards],
        scratch_shapes=[pltpu.VMEM((tb, c), F32)] + (_gather_big_sems(nsh) if nsh else []),
        compiler_params=_cparams(("arbitrary", "arbitrary")),
    )(h, w1, w2, ln_g, ln_b, *shards)
    return outs[0], outs[1], outs[2:]


def _mlp_bwd(name, h, s, dout, w1, w2, layer, ln_g, ln_b, tb):
    t, c = h.shape
    nj, fc = w1.shape[0], w1.shape[3]
    ff = nj * fc
    ni = t // tb
    nt = (((1,), (1,)), ((), ()))
    douts = dout if isinstance(dout, tuple) else (dout,)
    nd = len(douts)

    def body(h_ref, s_ref, *rest):
        dout_refs = rest[:nd]
        (w1_ref, w2_ref, g_ref, b_ref, dh_ref, ds_ref, dhid_ref, act_ref, dg_ref, db_ref,
         ds_scr, dh_acc, dg_acc, db_acc) = rest[nd:]
        i, j = pl.program_id(0), pl.program_id(1)

        @pl.when((i == 0) & (j == 0))
        def _():
            dg_acc[...] = jnp.zeros_like(dg_acc)
            db_acc[...] = jnp.zeros_like(db_acc)

        @pl.when(j == 0)
        def _():
            _, vjp = jax.vjp(_ln, s_ref[...], g_ref[...], b_ref[...])
            ds, dg, db = vjp(functools.reduce(jnp.add, [r[...] for r in dout_refs]))
            ds_scr[...] = ds
            ds_ref[...] = ds.astype(BF16)
            dh_acc[...] = DN_ALPHA * ds
            dg_acc[...] += dg
            db_acc[...] += db

        w1b, w2b = w1_ref[...].astype(BF16), w2_ref[...].astype(BF16)
        hid = jnp.dot(h_ref[...].astype(BF16), w1b, preferred_element_type=F32)
        rl = jnp.maximum(hid, 0.0)
        dact = lax.dot_general(ds_scr[...].astype(BF16), w2b, nt, preferred_element_type=F32)
        dhid = (dact * 2.0 * rl).astype(BF16)
        dh_acc[...] += lax.dot_general(dhid, w1b, nt, preferred_element_type=F32)
        dhid_ref[...] = dhid
        act_ref[...] = (rl * rl).astype(BF16)

        @pl.when(j == nj - 1)
        def _():
            dh_ref[...] = dh_acc[...]

        @pl.when((i == ni - 1) & (j == nj - 1))
        def _():
            dg_ref[...] = dg_acc[...]
            db_ref[...] = db_acc[...]

    row = pl.BlockSpec((tb, c), lambda i, j: (i, 0))
    vec = pl.BlockSpec((1, c), lambda i, j: (0, 0))
    wide = pl.BlockSpec((tb, fc), lambda i, j: (i, j))
    return pl.pallas_call(
        body, name=name, grid=(ni, nj),
        in_specs=[row, row] + [row] * nd + [pl.BlockSpec((None, None, c, fc), lambda i, j: (j, layer, 0, 0)),
                                            pl.BlockSpec((None, None, fc, c), lambda i, j: (j, layer, 0, 0)), vec, vec],
        out_specs=[row, row, wide, wide, vec, vec],
        out_shape=[_sds((t, c)), _sds((t, c), BF16), _sds((t, ff), BF16), _sds((t, ff), BF16), _sds((1, c)), _sds((1, c))],
        scratch_shapes=[pltpu.VMEM((tb, c), F32), pltpu.VMEM((tb, c), F32), pltpu.VMEM((1, c), F32), pltpu.VMEM((1, c), F32)],
        compiler_params=_cparams(("arbitrary", "arbitrary")),
    )(h, s, *douts, w1, w2, ln_g, ln_b)


def _load_heads(ref, hg):
    return jnp.stack([ref[:, h * HEAD:(h + 1) * HEAD] for h in range(hg)])


def _store_heads(ref, val):
    for h in range(val.shape[0]):
        ref[:, h * HEAD:(h + 1) * HEAD] = val[h]


def _rec_fwd(r, lw, k, v, a, b, hg, shards):
    t, c = r.shape
    n = HEAD
    nh = c // n
    ln = REC_CHUNK
    nck = t // ln
    ngrp = nh // hg
    nsh = len(shards)
    steps = ngrp * nck

    def body(r_ref, lw_ref, k_ref, v_ref, a_ref, b_ref, *rest):
        src, (o_ref, s0_ref), dst = rest[:nsh], rest[nsh:nsh + 2], rest[nsh + 2:2 * nsh + 2]
        state, sems = rest[2 * nsh + 2], rest[2 * nsh + 3:]
        step = pl.program_id(0) * nck + pl.program_id(1)
        start, forward, finish = _gather_big_phases(src, dst, sems)
        pl.when(step == 0)(start)

        @pl.when(pl.program_id(1) == 0)
        def _():
            state[...] = jnp.zeros_like(state)

        s0 = state[...]
        s0_ref[...] = s0
        o, s1 = _rec_chunk(s0, *(_load_heads(x, hg) for x in (r_ref, lw_ref, k_ref, v_ref, a_ref, b_ref)))
        _store_heads(o_ref, o)
        state[...] = s1
        pl.when(step == steps // 2)(forward)
        pl.when(step == steps - 1)(finish)

    seq = pl.BlockSpec((ln, hg * n), lambda g, i: (i, g))
    outs = pl.pallas_call(
        body, name="rec_fwd", grid=(ngrp, nck), in_specs=[seq] * 6 + [_HBM] * nsh,
        out_specs=[seq, pl.BlockSpec((None, hg, n, n), lambda g, i: (i, g, 0, 0))] + [_HBM] * nsh,
        out_shape=[_sds((t, c)), _sds((nck, nh, n, n))] + [_sds((4,) + s.shape, s.dtype) for s in shards],
        scratch_shapes=[pltpu.VMEM((hg, n, n), F32)] + _gather_big_sems(nsh),
        compiler_params=_cparams(("arbitrary", "arbitrary")),
    )(r, lw, k, v, a, b, *shards)
    return outs[0], outs[1], outs[2:]


def _rec_bwd(r, lw, k, v, a, b, s0s, do, hg, chip_sums):
    t, c = r.shape
    n = HEAD
    nh = c // n
    ln = REC_CHUNK
    nck = t // ln
    ngrp = nh // hg
    nsum = len(chip_sums)
    steps = ngrp * nck

    def body(r_ref, lw_ref, k_ref, v_ref, a_ref, b_ref, s0_ref, do_ref, *rest):
        src, grad_refs, land = rest[:nsum], rest[nsum:nsum + 6], rest[nsum + 6:2 * nsum + 6]
        dstate, sems = rest[2 * nsum + 6], rest[2 * nsum + 7:]
        step = pl.program_id(0) * nck + pl.program_id(1)
        start, finish = _scatter_big_phases(src, land, sems)
        pl.when(step == 0)(start)

        @pl.when(pl.program_id(1) == 0)
        def _():
            dstate[...] = jnp.zeros_like(dstate)

        _, vjp = jax.vjp(_rec_chunk, s0_ref[...], *(_load_heads(x, hg) for x in (r_ref, lw_ref, k_ref, v_ref, a_ref, b_ref)))
        ds0, *grads = vjp((_load_heads(do_ref, hg), dstate[...]))
        dstate[...] = ds0
        for ref, val in zip(grad_refs, grads):
            _store_heads(ref, val)
        pl.when(step == steps - 1)(finish)

    seq = pl.BlockSpec((ln, hg * n), lambda g, i: (nck - 1 - i, g))
    outs = pl.pallas_call(
        body, name="rec_bwd", grid=(ngrp, nck),
        in_specs=[seq] * 6 + [pl.BlockSpec((None, hg, n, n), lambda g, i: (nck - 1 - i, g, 0, 0)), seq] + [_HBM] * nsum,
        out_specs=[seq] * 6 + [_HBM] * nsum,
        out_shape=[_sds((t, c))] * 6 + [_sds((3,) + s.shape[1:], s.dtype) for s in chip_sums],
        scratch_shapes=[pltpu.VMEM((hg, n, n), F32)] + _scatter_big_sems(nsum),
        compiler_params=_cparams(("arbitrary", "arbitrary")),
    )(r, lw, k, v, a, b, s0s, do, *chip_sums)
    return outs[:6], outs[6:]


def _s5_blocks(c):
    kb, nb = S5_PACK * SSM_GROUP, S5_PACK * SSM_STATE
    return kb, nb, c // kb


def _s5_fwd(h, bc, abar, cc, tb):
    t, c = h.shape
    w2 = bc.shape[1]
    w = w2 // 2
    kb, nb, nsb = _s5_blocks(c)

    def body(h_ref, bc_ref, a_ref, cc_ref, s_ref, y_ref, carry, rows):
        @pl.when(pl.program_id(0) == 0)
        def _():
            carry[...] = jnp.zeros_like(carry)

        for j in range(w2 // nb):
            ch = (j % nsb) * kb
            rows[:, j * nb:(j + 1) * nb] = jnp.dot(h_ref[:, ch:ch + kb].astype(BF16), bc_ref[:, j * nb:(j + 1) * nb],
                                                   preferred_element_type=F32)
        ar, ai = a_ref[:, :w], a_ref[:, w:]

        def step(i, state):
            hr, hi = state
            nr = ar * hr - ai * hi + rows[pl.ds(i, 1), :w]
            ni = ar * hi + ai * hr + rows[pl.ds(i, 1), w:]
            rows[pl.ds(i, 1), :w] = nr
            rows[pl.ds(i, 1), w:] = ni
            return nr, ni

        hr, hi = lax.fori_loop(0, tb, step, (carry[:, :w], carry[:, w:]))
        carry[:, :w] = hr
        carry[:, w:] = hi
        s_ref[...] = rows[...].astype(BF16)
        for j in range(nsb):
            re, im = j * nb, w + j * nb
            y_ref[:, j * kb:(j + 1) * kb] = (
                jnp.dot(s_ref[:, re:re + nb], cc_ref[re:re + nb, :], preferred_element_type=F32)
                + jnp.dot(s_ref[:, im:im + nb], cc_ref[im:im + nb, :], preferred_element_type=F32))

    return pl.pallas_call(
        body, name="s5_fwd", grid=(t // tb,),
        in_specs=[pl.BlockSpec((tb, c), lambda i: (i, 0)), _full_spec(bc.shape), _full_spec(abar.shape), _full_spec(cc.shape)],
        out_specs=[pl.BlockSpec((tb, w2), lambda i: (i, 0)), pl.BlockSpec((tb, c), lambda i: (i, 0))],
        out_shape=[_sds((t, w2), BF16), _sds((t, c))],
        scratch_shapes=[pltpu.VMEM((1, w2), F32), pltpu.VMEM((tb, w2), F32)],
        compiler_params=_cparams(("arbitrary",)),
    )(h, bc, abar, cc)


def _s5_bwd(dy, s, abar, cc, bc, tb):
    t, c = dy.shape
    w2 = s.shape[1]
    w = w2 // 2
    kb, nb, nsb = _s5_blocks(c)
    nblk = t // tb
    pack = 16
    per = tb // pack
    nt = (((1,), (1,)), ((), ()))

    def body(dy_ref, s_ref, sprev_ref, a_ref, cc_ref, bc_ref, dbu_ref, dh_ref, da_ref, carry, da_acc, rows):
        i = pl.program_id(0)

        @pl.when(i == 0)
        def _():
            carry[...] = jnp.zeros_like(carry)
            da_acc[...] = jnp.zeros_like(da_acc)

        for j in range(w2 // nb):
            ch = (j % nsb) * kb
            rows[:, j * nb:(j + 1) * nb] = lax.dot_general(dy_ref[:, ch:ch + kb].astype(BF16), cc_ref[j * nb:(j + 1) * nb, :], nt,
                                                           preferred_element_type=F32)
        ar, ai = a_ref[:, :w], a_ref[:, w:]

        def step(n, state):
            gr, gi = state
            row = tb - 1 - n
            nr = rows[pl.ds(row, 1), :w] + ar * gr + ai * gi
            ni = rows[pl.ds(row, 1), w:] + ar * gi - ai * gr
            rows[pl.ds(row, 1), :w] = nr
            rows[pl.ds(row, 1), w:] = ni
            return nr, ni

        gr, gi = lax.fori_loop(0, tb, step, (carry[:, :w], carry[:, w:]))
        carry[:, :w] = gr
        carry[:, w:] = gi
        last = (lax.broadcasted_iota(jnp.int32, (pack, w2), 0) == pack - 1) & (i < nblk - 1)
        before = jnp.sum(jnp.where(last, sprev_ref[...].astype(F32), 0.0), axis=0, keepdims=True)
        rid = lax.broadcasted_iota(jnp.int32, (tb, w2), 0)
        sp = jnp.where(rid == 0, before, pltpu.roll(s_ref[...].astype(F32), 1, 0))
        g = rows[...]
        dbu_ref[...] = g.astype(BF16)
        spr, spi, g_r, g_i = sp[:, :w], sp[:, w:], g[:, :w], g[:, w:]
        da_acc[:, :w] += jnp.sum(spr * g_r + spi * g_i, axis=0, keepdims=True)
        da_acc[:, w:] += jnp.sum(spr * g_i - spi * g_r, axis=0, keepdims=True)
        for j in range(nsb):
            re, im = j * nb, w + j * nb
            dh_ref[:, j * kb:(j + 1) * kb] = (
                lax.dot_general(dbu_ref[:, re:re + nb], bc_ref[:, re:re + nb], nt, preferred_element_type=F32)
                + lax.dot_general(dbu_ref[:, im:im + nb], bc_ref[:, im:im + nb], nt, preferred_element_type=F32))

        @pl.when(i == nblk - 1)
        def _():
            da_ref[...] = da_acc[...]

    wide = pl.BlockSpec((tb, w2), lambda i: (nblk - 1 - i, 0))
    narrow = pl.BlockSpec((tb, c), lambda i: (nblk - 1 - i, 0))
    prev = pl.BlockSpec((pack, w2), lambda i: (jnp.maximum((nblk - 1 - i) * per - 1, 0), 0))
    return pl.pallas_call(
        body, name="s5_bwd", grid=(nblk,),
        in_specs=[narrow, wide, prev, _full_spec(abar.shape), _full_spec(cc.shape), _full_spec(bc.shape)],
        out_specs=[wide, narrow, pl.BlockSpec((1, w2), lambda i: (0, 0))],
        out_shape=[_sds((t, w2), BF16), _sds((t, c)), _sds((1, w2))],
        scratch_shapes=[pltpu.VMEM((1, w2), F32), pltpu.VMEM((1, w2), F32), pltpu.VMEM((tb, w2), F32)],
        compiler_params=_cparams(("arbitrary",)),
    )(dy, s, s, abar, cc, bc)


def _zoh_fwd(a_re, a_im, log_dt, b_re_t, b_im_t):
    def body(*refs):
        for r, val in zip(refs[5:], _f_zoh(*(x[...] for x in refs[:5]))):
            r[...] = val

    return pl.pallas_call(body, name="s5_zoh_fwd", out_shape=[_sds(a_re.shape)] * 2 + [_sds(b_re_t.shape)] * 2,
                          compiler_params=_cparams())(a_re, a_im, log_dt, b_re_t, b_im_t)


def _zoh_bwd(a_re, a_im, log_dt, b_re_t, b_im_t, couts):
    def body(*refs):
        _, vjp = jax.vjp(_f_zoh, *(x[...] for x in refs[:5]))
        for r, val in zip(refs[9:], vjp(tuple(x[...] for x in refs[5:9]))):
            r[...] = val

    ins = (a_re, a_im, log_dt, b_re_t, b_im_t)
    return pl.pallas_call(body, name="s5_zoh_bwd", out_shape=[_sds(x.shape) for x in ins],
                          compiler_params=_cparams())(*ins, *couts)


def _loss_head(h, target, tb):
    t, c = h.shape
    nb = t // tb

    def body(h_ref, t_ref, loss_ref, dh_ref, acc):
        i = pl.program_id(0)

        @pl.when(i == 0)
        def _():
            acc[...] = jnp.zeros_like(acc)

        d = h_ref[...] - t_ref[...]
        dh_ref[...] = d * (1.0 / c)
        acc[...] += 0.5 * jnp.sum(jnp.mean(d * d, axis=-1, keepdims=True), axis=0, keepdims=True)

        @pl.when(i == nb - 1)
        def _():
            loss_ref[...] = jnp.broadcast_to(acc[...], loss_ref.shape)

    row = pl.BlockSpec((tb, c), lambda i: (i, 0))
    return pl.pallas_call(
        body, name="loss_head", grid=(nb,), in_specs=[row, row],
        out_specs=[pl.BlockSpec((8, 128), lambda i: (0, 0)), row], out_shape=[_sds((8, 128)), _sds((t, c))],
        scratch_shapes=[pltpu.VMEM((1, 1), F32)], compiler_params=_cparams(("arbitrary",)),
    )(h, target)


def _rows_tile(rows):
    for cand in (512, 256, 128, 64, 32, 16, 8):
        if rows % cand == 0:
            return cand
    return rows


def _addn(name, arrs):
    rows, cols = arrs[0].shape
    tb = _rows_tile(rows)

    def body(*refs):
        acc = refs[0][...]
        for r in refs[1:-1]:
            acc = acc + r[...]
        refs[-1][...] = acc

    blk = pl.BlockSpec((tb, cols), lambda i: (i, 0))
    return pl.pallas_call(body, name=name, grid=(rows // tb,), in_specs=[blk] * len(arrs), out_specs=blk,
                          out_shape=_sds((rows, cols)), compiler_params=_cparams(("parallel",)))(*arrs)


def _adamw_math(w, g, m, v):
    m = ADAM_B1 * m + (1.0 - ADAM_B1) * g
    v = ADAM_B2 * v + (1.0 - ADAM_B2) * jnp.square(g)
    m_hat = m / (1.0 - ADAM_B1 ** ADAM_STEP)
    v_hat = v / (1.0 - ADAM_B2 ** ADAM_STEP)
    delta = -ADAM_LR * (m_hat / (jnp.sqrt(v_hat) + ADAM_EPS) + ADAM_WD * w)
    return delta, m, v


def _adamw(name, parts, w, m, v):
    rows, cols = w.shape
    tb = _rows_tile(rows)
    npart = len(parts)

    def body(*refs):
        g = refs[0][...]
        for r in refs[1:npart]:
            g = g + r[...]
        w_ref, m_ref, v_ref = refs[npart:npart + 3]
        g_out, d_out, m_out, v_out = refs[npart + 3:]
        delta, mn, vn = _adamw_math(w_ref[...], g, m_ref[...], v_ref[...])
        g_out[...] = g
        d_out[...] = delta
        m_out[...] = mn
        v_out[...] = vn

    blk = pl.BlockSpec((tb, cols), lambda i: (i, 0))
    return pl.pallas_call(body, name=name, grid=(rows // tb,), in_specs=[blk] * (npart + 3), out_specs=[blk] * 4,
                          out_shape=[_sds((rows, cols))] * 4, compiler_params=_cparams(("parallel",)))(*parts, w, m, v)


def _shift_down(a):
    return jnp.concatenate([jnp.zeros_like(a[:1]), a[:-1]], axis=0)


def _shift_up(a):
    return jnp.concatenate([a[1:], jnp.zeros_like(a[:1])], axis=0)


def _s5_pack_mask(g):
    return (jnp.arange(g)[None, :] % S5_PACK == jnp.arange(S5_PACK)[:, None]).astype(F32)


def _compact_b(bbar_t):
    g, s, p = bbar_t.shape
    return (_s5_pack_mask(g)[:, None, :, None] * bbar_t.transpose(1, 0, 2)[None]).reshape(S5_PACK * s, g * p)


def _compact_b_t(dense, g):
    s, p = dense.shape[0] // S5_PACK, dense.shape[1] // g
    return jnp.sum(dense.reshape(S5_PACK, s, g, p) * _s5_pack_mask(g)[:, None, :, None], axis=0).transpose(1, 0, 2)


def _compact_c(c_w):
    g, s, p = c_w.shape
    return (c_w.transpose(0, 2, 1)[:, :, None, :] * _s5_pack_mask(g).T[:, None, :, None]).reshape(g * p, S5_PACK * s)


def _compact_c_t(dense, g):
    p, s = dense.shape[0] // g, dense.shape[1] // S5_PACK
    return jnp.sum(dense.reshape(g, p, S5_PACK, s) * _s5_pack_mask(g).T[:, None, :, None], axis=2).transpose(0, 2, 1)


def _local_step(x, target, fw, core):
    t, c = x.shape
    nh = c // HEAD
    ng = c // SSM_GROUP
    tb = min(256, t)
    tbb = min(128, t)
    tbm = min(512, t)
    tbmb = min(512, t)
    tbs = min(256, t)
    tk5 = min(2048, t)
    hg = min(16, nh)
    mu = [fw['rw_mu'][i:i + 1] for i in range(6)]
    ln_g = [fw['ln_g'][i:i + 1] for i in range(4)]
    ln_b = [fw['ln_b'][i:i + 1] for i in range(4)]
    grads = {}

    xp = _shift_down(x)
    proj_params = {n: (mu[i], fw['rw_w' + n]) for n, i in (('r', 0), ('k', 2), ('v', 3))}
    raw = {n: _stage_fwd("proj_" + n, _f_proj, (x, xp), proj_params[n], (c,), tb)[0] for n in 'rkv'}
    lora_params = (mu[1], mu[4], mu[5], fw['rw_w0'], fw['rw_w1'], fw['rw_w2'], fw['rw_a0'], fw['rw_a1'], fw['rw_a2'],
                   fw['rw_g1'], fw['rw_g2'], fw['rw_k_k'], fw['rw_k_a'])
    lw, k2, an, bb, gate = _stage_fwd("lora", _f_lora, (x, xp, raw['k']), lora_params, (c,) * 5, tb)
    rec_in = (raw['r'], lw, k2, raw['v'], an, bb)
    o, s0s, (wo_view, w1_l0, w2_l0) = _rec_fwd(*rec_in, hg, fw['late_a'])
    fw = dict(fw, rw_wo=wo_view.reshape(c, c))
    mlp_w = [(w1_l0.reshape(4, 1, c, -1), w2_l0.reshape(4, 1, -1, c)), None]
    post_params = (fw['rw_lnx_g'], fw['rw_lnx_b'], fw['rw_r_k'], fw['rw_wo'], ln_g[0], ln_b[0])
    post_acts = (o, raw['r'], k2, raw['v'], gate, x)
    h1, = _stage_fwd("post", _f_post, post_acts, post_params, (c,), tb)
    h2, s_mlp0, (glu_view, w1_l1, w2_l1) = _mlp_fwd("mlp0_fwd", h1, *mlp_w[0], 0, ln_g[1], ln_b[1], tbm, fw['late_b'])
    mlp_w[1] = (w1_l1.reshape(4, 1, c, -1), w2_l1.reshape(4, 1, -1, c))
    fw = dict(fw, s5_w_glu=tuple(glu_view[q] for q in range(4)))

    a_re, a_im, log_dt = fw['s5_a_re'], fw['s5_a_im'], fw['s5_log_dt']
    b_re_t, b_im_t = fw['s5_b_re'].transpose(0, 2, 1), fw['s5_b_im'].transpose(0, 2, 1)
    abar_re, abar_im, bbar_re_t, bbar_im_t = _zoh_fwd(a_re, a_im, log_dt, b_re_t, b_im_t)
    abar = jnp.concatenate([abar_re.reshape(1, -1), abar_im.reshape(1, -1)], axis=1)
    bc = jnp.concatenate([_compact_b(bbar_re_t), _compact_b(bbar_im_t)], axis=1).astype(BF16)
    cc = jnp.concatenate([_compact_c(fw['s5_c_re']), -_compact_c(fw['s5_c_im'])], axis=0).astype(BF16)
    st, ys = _s5_fwd(h2, bc, abar, cc, tbs)
    glu_params = (fw['s5_d'], *fw['s5_w_glu'], ln_g[2], ln_b[2])
    h3, = _stage_fwd("glu", _f_glu, (ys, h2), glu_params, (c,), tb)
    h4, s_mlp1, _ = _mlp_fwd("mlp1_fwd", h3, *mlp_w[1], 0, ln_g[3], ln_b[3], tbm)

    loss_blk, dh4 = _loss_head(h4, target, tb)

    dln_g, dln_b = [None] * 4, [None] * 4
    dh3, ds1, dhid1, act1, dln_g[3], dln_b[3] = _mlp_bwd("mlp1_bwd", h3, s_mlp1, dh4, *mlp_w[1], 0,
                                                         ln_g[3], ln_b[3], tbmb)
    dw1 = _mlp_weight_grad("mlp1_dw1", h3, dhid1, 1, DEPTH, "n")
    dw2 = _mlp_weight_grad("mlp1_dw2", act1, ds1, 1, DEPTH, "m")
    (dys, dh2_glu), (grads['s5_d'], *dglu, dln_g[2], dln_b[2]) = _stage_bwd(
        "glu_bwd", _f_glu, (ys, h2), glu_params, (dh3,), tbb, proxied=(1, 2, 3, 4))
    grads['s5_w_glu'] = jnp.stack(dglu)
    dcc = _s5_weight_grad("s5_dcc", dys, st, True, tk5)
    dbu, dh2_bu, dabar = _s5_bwd(dys, st, abar, cc, bc, tbs)
    dbc = _s5_weight_grad("s5_dbc", h2, dbu, False, tk5)
    gp = ng * SSM_STATE
    grads['s5_c_re'] = _compact_c_t(dcc[:gp], ng)
    grads['s5_c_im'] = -_compact_c_t(dcc[gp:], ng)
    zoh_couts = (dabar[:, :gp].reshape(ng, SSM_STATE), dabar[:, gp:].reshape(ng, SSM_STATE),
                 _compact_b_t(dbc[:, :gp], ng), _compact_b_t(dbc[:, gp:], ng))
    grads['s5_a_re'], grads['s5_a_im'], grads['s5_log_dt'], db_re_t, db_im_t = _zoh_bwd(
        a_re, a_im, log_dt, b_re_t, b_im_t, zoh_couts)
    grads['s5_b_re'], grads['s5_b_im'] = db_re_t.transpose(0, 2, 1), db_im_t.transpose(0, 2, 1)
    dh2 = (dh2_glu, dh2_bu)

    dh1, ds0, dhid0, act0, dln_g[1], dln_b[1] = _mlp_bwd("mlp0_bwd", h1, s_mlp0, dh2, *mlp_w[0], 0,
                                                         ln_g[1], ln_b[1], tbmb)
    grads['mlp_w1'] = _mlp_weight_grad("mlp0_dw1", h1, dhid0, 0, DEPTH, "n", into=dw1)
    grads['mlp_w2'] = _mlp_weight_grad("mlp0_dw2", act0, ds0, 0, DEPTH, "m", into=dw2)
    (do, dr_p, dk2_p, dv_p, dgate, dx_post), post_g = _stage_bwd("post_bwd", _f_post, post_acts, post_params, (dh1,), tbb,
                                                                 proxied=(3,))
    grads['rw_lnx_g'], grads['rw_lnx_b'], grads['rw_r_k'], grads['rw_wo'], dln_g[0], dln_b[0] = post_g
    ready_sums = _chip_sums("a", [grads[n] for n in BIG_READY], core)
    rec_g, ready_lands = _rec_bwd(*rec_in, s0s, do, hg, ready_sums)
    reduced = dict(zip(BIG_READY, zip(ready_sums, ready_lands)))
    dr_r, dlw, dk2_r, dv_r, dan, dbb = rec_g
    dk2 = (dk2_p, dk2_r)
    (dx_l, dxp_l, dkraw_l), lora_g = _stage_bwd("lora_bwd", _f_lora, (x, xp, raw['k']), lora_params,
                                                (dlw, dk2, dan, dbb, dgate), tbb)
    (dmu_w, dmu_a, dmu_g, grads['rw_w0'], grads['rw_w1'], grads['rw_w2'], grads['rw_a0'], grads['rw_a1'], grads['rw_a2'],
     grads['rw_g1'], grads['rw_g2'], grads['rw_k_k'], grads['rw_k_a']) = lora_g
    dproj = {'r': (dr_p, dr_r), 'k': dkraw_l, 'v': (dv_p, dv_r)}
    dxs, dxps, dmu = [dx_post, dx_l], [dxp_l], {}
    for n in 'rkv':
        (dx_n, dxp_n), (dmu[n], grads['rw_w' + n]) = _stage_bwd("proj_bwd_" + n, _f_proj, (x, xp), proj_params[n],
                                                                 (dproj[n],), tb, proxied=(1,))
        dxs.append(dx_n)
        dxps.append(dxp_n)
    grads['rw_mu'] = jnp.concatenate([dmu['r'], dmu_w, dmu['k'], dmu['v'], dmu_a, dmu_g], axis=0)
    grads['ln_g'] = jnp.concatenate(dln_g, axis=0)
    grads['ln_b'] = jnp.concatenate(dln_b, axis=0)
    grad_x = _addn("grad_x", dxs + [_shift_up(_addn("dxp_sum", dxps))])
    late = [n for n in BIG if n not in BIG_READY]
    late_sums = dict(zip(late, _chip_sums("b", [grads[n] for n in late], core)))
    return loss_blk, grad_x, grads, reduced, late_sums


def _position():
    return lax.axis_index("x"), lax.axis_index("y"), lax.axis_index("c")


def _other_chips(x, y):
    return [(1 - x, y), (x, 1 - y), (1 - x, 1 - y)]


def _chip_slice(ref, axis, q, size):
    idx = [slice(None)] * len(ref.shape)
    idx[axis] = pl.ds(pl.multiple_of(q * size, size), size)
    return ref.at[tuple(idx)]


_HBM = pl.BlockSpec(memory_space=pltpu.HBM)


def _gather_small_phases(src, dst, axes, sems):
    n = len(src)
    send_sems, recv_sems, own_sems = sems
    x, y, c = _position()
    chips = _other_chips(x, y)
    sizes = [src[a].shape[axes[a]] for a in range(n)]

    def copy(a, k, q):
        return pltpu.make_async_remote_copy(
            src_ref=src[a], dst_ref=_chip_slice(dst[a], axes[a], q, sizes[a]), send_sem=send_sems.at[a, k],
            recv_sem=recv_sems.at[a, k], device_id=(*chips[k], c), device_id_type=MESH)

    def own(a):
        return pltpu.make_async_copy(src[a], _chip_slice(dst[a], axes[a], 2 * x + y, sizes[a]), own_sems.at[a])

    def start():
        for a in range(n):
            own(a).start()
            for k in range(3):
                copy(a, k, 2 * x + y).start()

    def finish():
        for a in range(n):
            for k, (cx, cy) in enumerate(chips):
                copy(a, k, 2 * cx + cy).wait_recv()
        for a in range(n):
            for k in range(3):
                copy(a, k, 2 * x + y).wait_send()
            own(a).wait()

    return start, finish


def _gather_early(big, small, axes):
    nb, ns = len(big), len(small)
    full_shapes = [tuple(s * 4 if i == ax else s for i, s in enumerate(a.shape)) for a, ax in zip(small, axes)]

    def body(*refs):
        src_b, src_s = refs[:nb], refs[nb:nb + ns]
        dst_b, dst_s = refs[nb + ns:2 * nb + ns], refs[2 * nb + ns:2 * (nb + ns)]
        sems = refs[2 * (nb + ns):]
        small_start, small_finish = _gather_small_phases(src_s, dst_s, axes, sems[5:])
        small_start()
        for phase in _gather_big_phases(src_b, dst_b, sems[:5]):
            phase()
        small_finish()

    outs = pl.pallas_call(
        body, name="gather_early", in_specs=[_HBM] * (nb + ns), out_specs=[_HBM] * (nb + ns),
        out_shape=[_sds((4,) + a.shape, a.dtype) for a in big] + [_sds(s, a.dtype) for s, a in zip(full_shapes, small)],
        scratch_shapes=_gather_big_sems(nb) + [pltpu.SemaphoreType.DMA((ns, 3)), pltpu.SemaphoreType.DMA((ns, 3)),
                                               pltpu.SemaphoreType.DMA((ns,))],
        compiler_params=_cparams(),
    )(*big, *small)
    return outs[:nb], outs[nb:]


def _scatter_pieces(fulls, axes, sums):
    n, nsum = len(fulls), len(sums)
    sizes = [a.shape[ax] // 4 for a, ax in zip(fulls, axes)]
    shard_shapes = [tuple(sz if i == ax else s for i, s in enumerate(a.shape)) for a, ax, sz in zip(fulls, axes, sizes)]

    def body(*refs):
        src, big_src = refs[:n], refs[n:n + nsum]
        land, big_land = refs[n + nsum:2 * n + nsum], refs[2 * n + nsum:2 * (n + nsum)]
        send_sems, recv_sems = refs[2 * (n + nsum):2 * (n + nsum) + 2]
        big_start, big_finish = _scatter_big_phases(big_src, big_land, refs[2 * (n + nsum) + 2:])
        big_start()
        x, y, c = _position()
        chips = _other_chips(x, y)

        def copy(a, k):
            cx, cy = chips[k]
            return pltpu.make_async_remote_copy(
                src_ref=_chip_slice(src[a], axes[a], 2 * cx + cy, sizes[a]), dst_ref=land[a].at[k],
                send_sem=send_sems.at[a, k], recv_sem=recv_sems.at[a, k], device_id=(cx, cy, c), device_id_type=MESH)

        for a in range(n):
            for k in range(3):
                copy(a, k).start()
        for a in range(n):
            for k in range(3):
                copy(a, k).wait_recv()
        for a in range(n):
            for k in range(3):
                copy(a, k).wait_send()
        big_finish()

    outs = pl.pallas_call(
        body, name="scatter_grads", in_specs=[_HBM] * (n + nsum), out_specs=[_HBM] * (n + nsum),
        out_shape=[_sds((3,) + s) for s in shard_shapes] + [_sds((3,) + s.shape[1:], s.dtype) for s in sums],
        scratch_shapes=[pltpu.SemaphoreType.DMA((n, 3)), pltpu.SemaphoreType.DMA((n, 3))] + _scatter_big_sems(nsum),
        compiler_params=_cparams(),
    )(*fulls, *sums)
    return outs[:n], outs[n:]


def _sibling_swap(name, arrs):
    n = len(arrs)

    def body(*refs):
        src, dst = refs[:n], refs[n:2 * n]
        send_sems, recv_sems = refs[2 * n:]
        x, y, c = _position()
        copies = [pltpu.make_async_remote_copy(src_ref=src[a], dst_ref=dst[a], send_sem=send_sems.at[a], recv_sem=recv_sems.at[a],
                                               device_id=(x, y, 1 - c), device_id_type=MESH) for a in range(n)]
        for cp in copies:
            cp.start()
        for cp in copies:
            cp.wait_recv()
        for cp in copies:
            cp.wait_send()

    return pl.pallas_call(
        body, name=name, in_specs=[_HBM] * n, out_specs=[_HBM] * n, out_shape=[_sds(a.shape) for a in arrs],
        scratch_shapes=[pltpu.SemaphoreType.DMA((n,)), pltpu.SemaphoreType.DMA((n,))],
        compiler_params=_cparams(),
    )(*arrs)


def _sum4(name, own, land):
    rows, cols = own.shape
    tb = _rows_tile(rows)

    def body(o_ref, l0, l1, l2, out_ref):
        out_ref[...] = ((o_ref[...] + l0[...]) + l1[...]) + l2[...]

    blk = pl.BlockSpec((tb, cols), lambda i: (i, 0))
    lands = [pl.BlockSpec((None, tb, cols), functools.partial(lambda k, i: (k, i, 0), k)) for k in range(3)]
    return pl.pallas_call(body, name=name, grid=(rows // tb,), in_specs=[blk] + lands, out_specs=blk,
                          out_shape=_sds((rows, cols)), compiler_params=_cparams(("parallel",)))(own, land, land, land)


def _allreduce_adamw_small(g, w, m, v):
    rows, lanes = g.shape

    def body(g_ref, w_ref, m_ref, v_ref, gs_ref, d_ref, mn_ref, vn_ref, land, send_sems, recv_sems):
        x, y, c = _position()
        me = 4 * x + 2 * y + c
        masks = [(bx, by, bc) for bx in (0, 1) for by in (0, 1) for bc in (0, 1)][1:]

        def peer(mask):
            return (x ^ mask[0], y ^ mask[1], c ^ mask[2])

        def copy(j, slot):
            return pltpu.make_async_remote_copy(src_ref=g_ref, dst_ref=land.at[slot], send_sem=send_sems.at[j],
                                                recv_sem=recv_sems.at[j], device_id=peer(masks[j]), device_id_type=MESH)

        for j in range(7):
            copy(j, me).start()
        land[me] = g_ref[...]
        for j in range(7):
            px, py, pc = peer(masks[j])
            copy(j, 4 * px + 2 * py + pc).wait_recv()
        for j in range(7):
            copy(j, me).wait_send()
        total = land[0]
        for dev in range(1, 8):
            total = total + land[dev]
        delta, mn, vn = _adamw_math(w_ref[...], total, m_ref[...], v_ref[...])
        gs_ref[...] = total
        d_ref[...] = delta
        mn_ref[...] = mn
        vn_ref[...] = vn

    vmem = pl.BlockSpec(memory_space=pltpu.VMEM)
    return pl.pallas_call(
        body, name="allreduce_adamw_small", in_specs=[vmem] * 4, out_specs=[vmem] * 4, out_shape=[_sds((rows, lanes))] * 4,
        scratch_shapes=[pltpu.VMEM((8, rows, lanes), F32), pltpu.SemaphoreType.DMA((7,)), pltpu.SemaphoreType.DMA((7,))],
        compiler_params=_cparams(),
    )(g, w, m, v)


def _row_half(ref, c):
    r2 = ref.shape[-2] // 2
    lead = (slice(None),) * (len(ref.shape) - 2)
    return ref.at[(*lead, pl.ds(pl.multiple_of(c * r2, r2), r2), slice(None))]


def _gather_big_phases(src, dst, sems):
    n = len(src)
    ici_send, ici_recv, d2d_send, d2d_recv, own_sems = sems
    x, y, c = _position()
    me = 2 * x + y
    chips = _other_chips(x, y)
    ids = [2 * cx + cy for cx, cy in chips]

    def ici(a, k, q):
        return pltpu.make_async_remote_copy(
            src_ref=_row_half(src[a], c), dst_ref=_row_half(dst[a].at[q], c), send_sem=ici_send.at[a, k],
            recv_sem=ici_recv.at[a, k], device_id=(*chips[k], c), device_id_type=MESH)

    def d2d(a, k, half):
        where = _row_half(dst[a].at[ids[k]], half)
        return pltpu.make_async_remote_copy(src_ref=where, dst_ref=where, send_sem=d2d_send.at[a, k], recv_sem=d2d_recv.at[a, k],
                                            device_id=(x, y, 1 - c), device_id_type=MESH)

    def own(a):
        return pltpu.make_async_copy(src[a], dst[a].at[me], own_sems.at[a])

    def start():
        for a in range(n):
            own(a).start()
            for k in range(3):
                ici(a, k, me).start()

    def forward():
        for a in range(n):
            for k in range(3):
                ici(a, k, ids[k]).wait_recv()
                d2d(a, k, c).start()

    def finish():
        for a in range(n):
            for k in range(3):
                d2d(a, k, 1 - c).wait_recv()
        for a in range(n):
            for k in range(3):
                ici(a, k, me).wait_send()
                d2d(a, k, c).wait_send()
            own(a).wait()

    return start, forward, finish


def _gather_big_sems(n):
    return [pltpu.SemaphoreType.DMA((n, 3))] * 4 + [pltpu.SemaphoreType.DMA((n,))]


def _chip_sums(tag, grads, core):
    views = [g.reshape(4, -1, g.shape[-1]) for g in grads]
    others = _sibling_halves("sibling_halves_" + tag, views)
    return [_half_add(f"half_add_{tag}{i}", v, o, core) for i, (v, o) in enumerate(zip(views, others))]


def _sibling_halves(name, views):
    n = len(views)

    def body(*refs):
        src, dst = refs[:n], refs[n:2 * n]
        send_sems, recv_sems = refs[2 * n:]
        x, y, c = _position()
        copies = [pltpu.make_async_remote_copy(src_ref=_row_half(src[a], 1 - c), dst_ref=dst[a], send_sem=send_sems.at[a],
                                               recv_sem=recv_sems.at[a], device_id=(x, y, 1 - c), device_id_type=MESH)
                  for a in range(n)]
        for cp in copies:
            cp.start()
        for cp in copies:
            cp.wait_recv()
        for cp in copies:
            cp.wait_send()

    return pl.pallas_call(
        body, name=name, in_specs=[_HBM] * n, out_specs=[_HBM] * n,
        out_shape=[_sds((4, v.shape[1] // 2, v.shape[2])) for v in views],
        scratch_shapes=[pltpu.SemaphoreType.DMA((n,)), pltpu.SemaphoreType.DMA((n,))], compiler_params=_cparams(),
    )(*views)


def _rows_tile_capped(rows, cap=256):
    return min(_rows_tile(rows), cap)


def _half_add(name, view, other, core):
    _, r, k = view.shape
    r2 = r // 2
    tr = _rows_tile_capped(r2)
    per = r2 // tr

    def body(c_ref, v_ref, o_ref, out_ref):
        out_ref[...] = (v_ref[...] + o_ref[...]).astype(BF16)

    blk = pl.BlockSpec((None, tr, k), lambda q, i, c: (q, i, 0))
    return pl.pallas_call(
        body, name=name,
        grid_spec=pltpu.PrefetchScalarGridSpec(
            num_scalar_prefetch=1, grid=(4, per),
            in_specs=[pl.BlockSpec((None, tr, k), lambda q, i, c: (q, c[0] * per + i, 0)), blk], out_specs=blk),
        out_shape=_sds((4, r2, k), BF16), compiler_params=_cparams(("parallel", "parallel")),
    )(core, view, other)


def _scatter_big_phases(src, land, sems):
    n = len(src)
    send_sems, recv_sems = sems
    x, y, c = _position()
    chips = _other_chips(x, y)

    def copy(a, k):
        cx, cy = chips[k]
        return pltpu.make_async_remote_copy(src_ref=src[a].at[2 * cx + cy], dst_ref=land[a].at[k], send_sem=send_sems.at[a, k],
                                            recv_sem=recv_sems.at[a, k], device_id=(cx, cy, c), device_id_type=MESH)

    def start():
        for a in range(n):
            for k in range(3):
                copy(a, k).start()

    def finish():
        for a in range(n):
            for k in range(3):
                copy(a, k).wait_recv()
        for a in range(n):
            for k in range(3):
                copy(a, k).wait_send()

    return start, finish


def _scatter_big_sems(n):
    return [pltpu.SemaphoreType.DMA((n, 3)), pltpu.SemaphoreType.DMA((n, 3))]


def _sum4_big(name, sums, land, chip):
    _, r2, k = sums.shape
    tr = _rows_tile_capped(r2)

    def body(q_ref, s_ref, l0, l1, l2, out_ref):
        out_ref[...] = ((s_ref[...].astype(F32) + l0[...].astype(F32)) + l1[...].astype(F32)) + l2[...].astype(F32)

    lands = [pl.BlockSpec((None, tr, k), functools.partial(lambda j, i, q: (j, i, 0), j)) for j in range(3)]
    return pl.pallas_call(
        body, name=name,
        grid_spec=pltpu.PrefetchScalarGridSpec(
            num_scalar_prefetch=1, grid=(r2 // tr,),
            in_specs=[pl.BlockSpec((None, tr, k), lambda i, q: (q[0], i, 0))] + lands,
            out_specs=pl.BlockSpec((tr, k), lambda i, q: (i, 0))),
        out_shape=_sds((r2, k)), compiler_params=_cparams(("parallel",)),
    )(chip, sums, land, land, land)


def _adamw_halves(name, mine, theirs, w, m, v, core):
    r, k = w.shape
    r2 = r // 2
    tr = _rows_tile_capped(r2, 512)
    per = r2 // tr

    def body(c_ref, mine_ref, theirs_ref, w_ref, m_ref, v_ref, g_out, d_out, m_out, v_out):
        g = jnp.where(pl.program_id(0) == c_ref[0], mine_ref[...], theirs_ref[...])
        delta, mn, vn = _adamw_math(w_ref[...], g, m_ref[...], v_ref[...])
        g_out[...] = g
        d_out[...] = delta
        m_out[...] = mn
        v_out[...] = vn

    half = pl.BlockSpec((tr, k), lambda h, i, c: (i, 0))
    full = pl.BlockSpec((tr, k), lambda h, i, c: (h * per + i, 0))
    return pl.pallas_call(
        body, name=name,
        grid_spec=pltpu.PrefetchScalarGridSpec(num_scalar_prefetch=1, grid=(2, per), in_specs=[half, half, full, full, full],
                                               out_specs=[full] * 4),
        out_shape=[_sds((r, k))] * 4, compiler_params=_cparams(("parallel", "parallel")),
    )(core, mine, theirs, w, m, v)


def _drops_layer_axis(name):
    return not (name.startswith('mlp') or name == 's5_d')


def _work(name, arr):
    return arr.reshape(arr.shape[1:]) if _drops_layer_axis(name) else arr


def _work_axis(name):
    return SHARD_AXIS[name] - (1 if _drops_layer_axis(name) else 0)


def _as2d(a):
    return a.reshape(-1, a.shape[-1])


def _replicated_2d(name, arr):
    if name in ('ln_g', 'ln_b'):
        return arr
    if name == 'rw_r_k':
        return arr.reshape(1, -1)
    if name == 's5_log_dt':
        return arr.reshape(-1, 1)
    if name.startswith('s5_'):
        return arr.reshape(arr.shape[1:])
    return arr


def _pack(arrs):
    flat = []
    for a in arrs:
        f = a.reshape(-1)
        flat.append(jnp.pad(f, (0, -f.shape[0] % 128)))
    f = jnp.concatenate(flat)
    f = jnp.pad(f, (0, -f.shape[0] % 1024))
    return f.reshape(-1, 128)


def _unpack(packed, shapes):
    flat = packed.reshape(-1)
    out, at = [], 0
    for s in shapes:
        size = math.prod(s)
        out.append(flat[at:at + size].reshape(s))
        at += size + (-size % 128)
    return out


def kernel(x, ln_g, ln_b, rw_mu, rw_w0, rw_w1, rw_w2, rw_a0, rw_a1, rw_a2, rw_g1, rw_g2, rw_k_k, rw_k_a, rw_r_k, rw_wr, rw_wk, rw_wv, rw_wo, rw_lnx_g, rw_lnx_b, s5_a_re, s5_a_im, s5_log_dt, s5_b_re, s5_b_im, s5_c_re, s5_c_im, s5_d, s5_w_glu, mlp_w1, mlp_w2, loss_target, m_ln_g, m_ln_b, m_rw_mu, m_rw_w0, m_rw_w1, m_rw_w2, m_rw_a0, m_rw_a1, m_rw_a2, m_rw_g1, m_rw_g2, m_rw_k_k, m_rw_k_a, m_rw_r_k, m_rw_wr, m_rw_wk, m_rw_wv, m_rw_wo, m_rw_lnx_g, m_rw_lnx_b, m_s5_a_re, m_s5_a_im, m_s5_log_dt, m_s5_b_re, m_s5_b_im, m_s5_c_re, m_s5_c_im, m_s5_d, m_s5_w_glu, m_mlp_w1, m_mlp_w2, v_ln_g, v_ln_b, v_rw_mu, v_rw_w0, v_rw_w1, v_rw_w2, v_rw_a0, v_rw_a1, v_rw_a2, v_rw_g1, v_rw_g2, v_rw_k_k, v_rw_k_a, v_rw_r_k, v_rw_wr, v_rw_wk, v_rw_wv, v_rw_wo, v_rw_lnx_g, v_rw_lnx_b, v_s5_a_re, v_s5_a_im, v_s5_log_dt, v_s5_b_re, v_s5_b_im, v_s5_c_re, v_s5_c_im, v_s5_d, v_s5_w_glu, v_mlp_w1, v_mlp_w2):
    d = dict(locals())
    x_pos, y_pos, c_pos = _position()
    chip = 2 * x_pos + y_pos
    chip_arr = jnp.reshape(chip, (1,)).astype(jnp.int32)
    core_arr = jnp.reshape(c_pos, (1,)).astype(jnp.int32)

    small = [n for n in SHARD_AXIS if n not in BIG]
    axes = [_work_axis(n) for n in small]
    big_views, small_fulls = _gather_early([_as2d(d[n]).astype(BF16) for n in BIG_EARLY], [_work(n, d[n]) for n in small], axes)
    views = dict(zip(BIG_EARLY, big_views))
    fw = dict(zip(small, small_fulls))
    c_model = d['x'].shape[-1]
    for n in BIG_EARLY:
        fw[n] = views[n].reshape(c_model, c_model)
    w1_layers, w2_layers = d['mlp_w1'].astype(BF16), d['mlp_w2'].astype(BF16)
    fw['late_a'] = [_as2d(d['rw_wo']).astype(BF16), w1_layers[0], w2_layers[0]]
    fw['late_b'] = [_as2d(d['s5_w_glu']).astype(BF16), w1_layers[1], w2_layers[1]]
    for n in REPLICATED:
        fw[n] = _replicated_2d(n, d[n])

    loss_blk, grad_x, grads, reduced, late_sums = _local_step(d['x'][0], d['loss_target'][0], fw, core_arr)
    loss = lax.psum(loss_blk[0, 0], ('x', 'y', 'c'))
    out = {}

    pieces = [grads[n] for n in small]
    lands, late_lands = _scatter_pieces(pieces, axes, list(late_sums.values()))
    reduced.update(zip(late_sums, zip(late_sums.values(), late_lands)))
    mine = [_sum4_big("sum4_" + n, *reduced[n], chip_arr) for n in BIG]
    for n, g, ax, land in zip(small, pieces, axes, lands):
        size = g.shape[ax] // 4
        mine.append(_sum4("sum4_" + n, lax.dynamic_slice_in_dim(g, chip * size, size, ax), land))
    theirs = _sibling_swap("swap_sums", mine)
    for n, a, b in zip(BIG + small, mine, theirs):
        w2d, m2d, v2d = _as2d(d[n]), _as2d(d['m_' + n]), _as2d(d['v_' + n])
        res = (_adamw_halves("adamw_" + n, a, b, w2d, m2d, v2d, core_arr) if n in BIG
               else _adamw("adamw_" + n, (a, b), w2d, m2d, v2d))
        out[n] = [r.reshape(d[n].shape) for r in res]

    rep_shapes = [d[n].shape for n in REPLICATED]
    packs = [_pack([grads[n] for n in REPLICATED])] + [_pack([d[p + n] for n in REPLICATED]) for p in ('', 'm_', 'v_')]
    res = [_unpack(p, rep_shapes) for p in _allreduce_adamw_small(*packs)]
    for i, n in enumerate(REPLICATED):
        out[n] = [r[i] for r in res]

    grad_x = grad_x.reshape(d['x'].shape)
    return (loss, grad_x, *[out[n][0] for n in WEIGHTS], *[out[n][1] for n in WEIGHTS],
            *[out[n][2] for n in WEIGHTS], *[out[n][3] for n in WEIGHTS])
```

```python
import functools
import math

import jax
import jax.numpy as jnp
from jax import lax
from jax.experimental import pallas as pl
from jax.experimental.pallas import tpu as pltpu

F32 = jnp.float32
BF16 = jnp.bfloat16
MESH = pl.DeviceIdType.MESH

HEAD = 64
SSM_GROUP = 16
SSM_STATE = 64
GN_EPS = 64e-5
LN_EPS = 1e-5
DEPTH = 2
DN_ALPHA = (2.0 * DEPTH) ** 0.25
ADAM_LR, ADAM_B1, ADAM_B2, ADAM_EPS, ADAM_WD, ADAM_STEP = 0.001, 0.9, 0.999, 1e-08, 0.01, 10
REC_CHUNK = 64
V7X_VMEM_BYTES = 64 * 2 ** 20
VMEM_LIMIT = V7X_VMEM_BYTES - 8 * 2 ** 20

WEIGHTS = ['ln_g', 'ln_b', 'rw_mu', 'rw_w0', 'rw_w1', 'rw_w2', 'rw_a0', 'rw_a1', 'rw_a2', 'rw_g1', 'rw_g2',
           'rw_k_k', 'rw_k_a', 'rw_r_k', 'rw_wr', 'rw_wk', 'rw_wv', 'rw_wo', 'rw_lnx_g', 'rw_lnx_b',
           's5_a_re', 's5_a_im', 's5_log_dt', 's5_b_re', 's5_b_im', 's5_c_re', 's5_c_im', 's5_d', 's5_w_glu',
           'mlp_w1', 'mlp_w2']
SHARD_AXIS = {'rw_mu': 2, 'rw_w1': 1, 'rw_w2': 2, 'rw_a1': 1, 'rw_a2': 2, 'rw_g1': 1, 'rw_g2': 2,
              'rw_wr': 1, 'rw_wk': 1, 'rw_wv': 1, 'rw_wo': 1, 's5_d': 1, 's5_w_glu': 2, 'mlp_w1': 2, 'mlp_w2': 1}
REPLICATED = [n for n in WEIGHTS if n not in SHARD_AXIS]
BIG_EARLY = ['rw_wr', 'rw_wk', 'rw_wv']
BIG_LATE = ['rw_wo', 's5_w_glu', 'mlp_w1', 'mlp_w2']
BIG = BIG_EARLY + BIG_LATE
BIG_READY = ['s5_w_glu', 'mlp_w1', 'mlp_w2']


def _sds(shape, dtype=F32):
    return jax.ShapeDtypeStruct(tuple(shape), dtype)


def _cparams(sem=None, **kw):
    if sem is not None:
        kw["dimension_semantics"] = sem
    return pltpu.CompilerParams(vmem_limit_bytes=VMEM_LIMIT, **kw)


def _mm_products(a, b, g):
    gb = g.astype(BF16)
    da = lax.dot_general(gb, b.astype(BF16), (((1,), (1,)), ((), ())), preferred_element_type=F32)
    db = lax.dot_general(a.astype(BF16), gb, (((0,), (0,)), ((), ())), preferred_element_type=F32)
    return da, db


@jax.custom_vjp
def _mm_plain(a, b):
    return jnp.dot(a.astype(BF16), b.astype(BF16), preferred_element_type=F32)


def _mm_plain_bwd(res, g):
    da, db = _mm_products(*res, g)
    return da.astype(res[0].dtype), db.astype(res[1].dtype)


_mm_plain.defvjp(lambda a, b: (_mm_plain(a, b), (a, b)), _mm_plain_bwd)


@jax.custom_vjp
def _mm_proxy(a, b, z):
    return jnp.dot(a.astype(BF16), b.astype(BF16), preferred_element_type=F32)


def _mm_proxy_bwd(res, g):
    da, db = _mm_products(*res, g)
    return da.astype(res[0].dtype), jnp.zeros_like(res[1]), db


_mm_proxy.defvjp(lambda a, b, z: (_mm_proxy(a, b, z), (a, b)), _mm_proxy_bwd)


def mm(a, b, z=None):
    return _mm_plain(a, b) if z is None else _mm_proxy(a, b, z)


def _split3(x):
    hi = x.astype(BF16)
    r1 = x - hi.astype(F32)
    mid = r1.astype(BF16)
    lo = (r1 - mid.astype(F32)).astype(BF16)
    return hi, mid, lo


def _head_sum_impl(x):
    c = x.shape[1]
    lanes = 128
    sel = (lax.broadcasted_iota(jnp.int32, (c, lanes), 0) // HEAD
           == lax.broadcasted_iota(jnp.int32, (c, lanes), 1)).astype(BF16)
    s = sum(jnp.dot(p, sel, preferred_element_type=F32) for p in _split3(x))
    return sum(lax.dot_general(p, sel, (((1,), (1,)), ((), ())), preferred_element_type=F32) for p in _split3(s))


@jax.custom_vjp
def head_sum(x):
    return _head_sum_impl(x)


head_sum.defvjp(lambda x: (_head_sum_impl(x), None), lambda _, g: (_head_sum_impl(g),))


def _ln(x, g, b):
    mu = jnp.mean(x, axis=-1, keepdims=True)
    xc = x - mu
    var = jnp.mean(xc * xc, axis=-1, keepdims=True)
    return xc * lax.rsqrt(var + LN_EPS) * g + b


def _f_proj(acts, params, proxies):
    x, xp = acts
    mu, w = params
    return (mm(x + (xp - x) * mu, w, proxies[1]),)


def _f_lora(acts, params, proxies):
    x, xp, kraw = acts
    mu_w, mu_a, mu_g, w0, w1, w2, a0, a1, a2, g1, g2, k_k, k_a = params
    xx = xp - x
    w_pre = w0 + mm(jnp.tanh(mm(x + xx * mu_w, w1)), w2)
    z = -w_pre
    softplus = jnp.maximum(z, 0.0) + jnp.log(1.0 + jnp.exp(-jnp.abs(z)))
    log_decay = -jnp.exp(-softplus - 0.5)
    a = jax.nn.sigmoid(a0 + mm(mm(x + xx * mu_a, a1), a2))
    g = mm(jax.nn.sigmoid(mm(x + xx * mu_g, g1)), g2)
    kk = kraw * k_k
    kkn = kk / jnp.maximum(jnp.sqrt(head_sum(kk * kk)), 1e-12)
    k2 = kraw * (1.0 + (a - 1.0) * k_a)
    return log_decay, k2, -kkn, kkn * a, g


def _f_post(acts, params, proxies):
    o, r, k2, v, g, x = acts
    lnx_g, lnx_b, r_k, wo, ln_g, ln_b = params
    om = head_sum(o) * (1.0 / HEAD)
    oc = o - om
    ov = head_sum(oc * oc) * (1.0 / HEAD)
    on = oc * lax.rsqrt(ov + GN_EPS) * lnx_g + lnx_b
    bonus = head_sum(r * k2 * r_k) * v
    y = mm((on + bonus) * g, wo, proxies[3])
    return (_ln(DN_ALPHA * x + y, ln_g, ln_b),)


def _f_glu(acts, params, proxies):
    ys, h = acts
    d, wv0, wv1, wg0, wg1, ln_g, ln_b = params
    y = jax.nn.gelu(ys + h * d)
    mix = jnp.concatenate([mm(y, wv0, proxies[1]) * jax.nn.sigmoid(mm(y, wg0, proxies[3])),
                           mm(y, wv1, proxies[2]) * jax.nn.sigmoid(mm(y, wg1, proxies[4]))], axis=1)
    return (_ln(DN_ALPHA * h + mix, ln_g, ln_b),)


def _f_zoh(a_re, a_im, log_dt, b_re_t, b_im_t):
    dt = jnp.exp(log_dt)
    lam_re = jnp.minimum(a_re, -1e-4)
    lam_im = a_im
    mag = jnp.exp(dt * lam_re)
    abar_re = mag * jnp.cos(dt * lam_im)
    abar_im = mag * jnp.sin(dt * lam_im)
    den = lam_re * lam_re + lam_im * lam_im
    nr, ni = abar_re - 1.0, abar_im
    coef_re = ((nr * lam_re + ni * lam_im) / den)[:, None, :]
    coef_im = ((ni * lam_re - nr * lam_im) / den)[:, None, :]
    return (abar_re, abar_im, coef_re * b_re_t - coef_im * b_im_t, coef_re * b_im_t + coef_im * b_re_t)


def _bdot16_raw(a, b, ca, cb):
    return lax.dot_general(a.astype(BF16), b.astype(BF16), (((ca,), (cb,)), ((0,), (0,))), preferred_element_type=F32)


@functools.partial(jax.custom_vjp, nondiff_argnums=(2, 3))
def _bdot16(a, b, ca, cb):
    return _bdot16_raw(a, b, ca, cb)


def _bdot16_bwd(ca, cb, res, g):
    a, b = res
    if (ca, cb) == (2, 1):
        return _bdot16_raw(g, b, 2, 2), _bdot16_raw(a, g, 1, 1)
    if (ca, cb) == (2, 2):
        return _bdot16_raw(g, b, 2, 1), _bdot16_raw(g, a, 1, 1)
    assert (ca, cb) == (1, 1)
    return _bdot16_raw(b, g, 2, 2), _bdot16_raw(a, g, 2, 1)


_bdot16.defvjp(lambda a, b, ca, cb: (_bdot16_raw(a, b, ca, cb), (a, b)), _bdot16_bwd)

def _time_sums(x, suffix):
    hg, ln, _ = x.shape
    row = lax.broadcasted_iota(jnp.int32, (hg, ln, ln), 1)
    col = lax.broadcasted_iota(jnp.int32, (hg, ln, ln), 2)
    tri = ((row <= col) if suffix else (row >= col)).astype(BF16)
    return sum(lax.dot_general(tri, p, (((2,), (1,)), ((0,), (0,))), preferred_element_type=F32) for p in _split3(x))


@jax.custom_vjp
def _time_cumsum(x):
    return _time_sums(x, False)


_time_cumsum.defvjp(lambda x: (_time_sums(x, False), None), lambda _, g: (_time_sums(g, True),))

_dot_score = _bdot16
_dot_inverse = _bdot16
_dot_value = _bdot16


def _rec_chunk(s0, r, lw, k, v, a, b):
    hg, ln, _ = r.shape
    row = lax.broadcasted_iota(jnp.int32, (hg, ln, ln), 1)
    col = lax.broadcasted_iota(jnp.int32, (hg, ln, ln), 2)
    incl, strict = row >= col, row > col
    cum = _time_cumsum(lw)
    total = jnp.sum(lw, axis=1, keepdims=True)
    e_cum, e_inv, e_prev, e_tail = jnp.exp(cum), jnp.exp(-cum), jnp.exp(cum - lw), jnp.exp(total - cum)
    rt, at, bt, kt = r * e_cum, a * e_prev, b * e_inv, k * e_inv
    aab = jnp.where(strict, _dot_score(at, bt, 2, 2), 0.0)
    aak = jnp.where(strict, _dot_score(at, kt, 2, 2), 0.0)
    arb = jnp.where(incl, _dot_score(rt, bt, 2, 2), 0.0)
    ark = jnp.where(incl, _dot_score(rt, kt, 2, 2), 0.0)
    p = (row == col).astype(F32) + aab
    m = aab
    for _ in range(int(math.log2(ln)) - 1):
        m = _dot_inverse(m, m, 2, 1)
        p = p + _dot_inverse(p, m, 2, 1)
    u = _dot_inverse(p, _dot_value(at, s0, 2, 2) + _dot_value(aak, v, 2, 1), 2, 1)
    o = _dot_value(rt, s0, 2, 2) + _dot_value(arb, u, 2, 1) + _dot_value(ark, v, 2, 1)
    s1 = s0 * jnp.exp(total) + _dot_value(u, b * e_tail, 1, 1) + _dot_value(v, k * e_tail, 1, 1)
    return o, s1


def _full_spec(shape):
    nd = len(shape)
    return pl.BlockSpec(tuple(shape), lambda *_: (0,) * nd)


def _stage_fwd(name, f, acts, params, out_dims, tb):
    t = acts[0].shape[0]
    na, npar = len(acts), len(params)

    def body(*refs):
        outs = f(tuple(r[...] for r in refs[:na]), tuple(r[...] for r in refs[na:na + npar]), (None,) * npar)
        for r, val in zip(refs[na + npar:], outs):
            r[...] = val

    return pl.pallas_call(
        body, name=name, grid=(t // tb,),
        in_specs=[pl.BlockSpec((tb, a.shape[1]), lambda i: (i, 0)) for a in acts] + [_full_spec(p.shape) for p in params],
        out_specs=[pl.BlockSpec((tb, d), lambda i: (i, 0)) for d in out_dims],
        out_shape=[_sds((t, d)) for d in out_dims],
        compiler_params=_cparams(("arbitrary",)),
    )(*acts, *params)


def _stage_bwd(name, f, acts, params, couts, tb, proxied=(), halves_of=()):
    nh = len(halves_of)
    t = acts[0].shape[0]
    groups = [c if isinstance(c, tuple) else (c,) for c in couts]
    couts = [term for grp in groups for term in grp]
    na, npar, nc = len(acts), len(params), len(couts)
    steps = t // tb

    def f_diff(act_vals, diff_vals, param_vals):
        real = tuple(param_vals[i] if i in proxied else diff_vals[i] for i in range(npar))
        proxies = tuple(diff_vals[i] if i in proxied else None for i in range(npar))
        return f(act_vals, real, proxies)

    def body(*refs):
        a_refs, p_hbm, c_refs = refs[:na], refs[na:na + npar], refs[na + npar:na + npar + nc]
        o = na + npar + nc
        half_src, o = refs[o:o + nh], o + nh
        da_refs, dp_hbm, half_dst = refs[o:o + na], refs[o + na:o + na + npar], refs[o + na + npar:o + na + npar + nh]
        o = o + na + npar + nh
        p_buf, acc, half_sems = refs[o:o + npar], refs[o + npar:o + 2 * npar], refs[o + 2 * npar:]
        i = pl.program_id(0)
        if nh:
            half_start, half_finish = _sibling_halves_phases(half_src, half_dst, half_sems)
            pl.when(i == 0)(half_start)

        @pl.when(i == 0)
        def _():
            for src, dst in zip(p_hbm, p_buf):
                pltpu.sync_copy(src, dst)
            for r in acc:
                r[...] = jnp.zeros_like(r)

        param_vals = tuple(r[...] for r in p_buf)
        diff_vals = tuple(jnp.zeros(v.shape, F32) if i in proxied else v for i, v in enumerate(param_vals))
        _, vjp = jax.vjp(functools.partial(f_diff, param_vals=param_vals), tuple(r[...] for r in a_refs), diff_vals)
        terms = iter(c_refs)
        d_acts, d_params = vjp(tuple(functools.reduce(jnp.add, [next(terms)[...] for _ in grp]) for grp in groups))
        for r, val in zip(da_refs, d_acts):
            r[...] = val
        for r, val in zip(acc, d_params):
            r[...] += val

        @pl.when(i == steps - 1)
        def _():
            for src, dst in zip(acc, dp_hbm):
                pltpu.sync_copy(src, dst)

        if nh:
            pl.when(i == steps - 1)(half_finish)

    hbm = pl.BlockSpec(memory_space=pltpu.HBM)
    outs = pl.pallas_call(
        body, name=name, grid=(steps,),
        in_specs=[pl.BlockSpec((tb, a.shape[1]), lambda i: (i, 0)) for a in acts] + [hbm] * npar
        + [pl.BlockSpec((tb, c.shape[1]), lambda i: (i, 0)) for c in couts] + [hbm] * nh,
        out_specs=[pl.BlockSpec((tb, a.shape[1]), lambda i: (i, 0)) for a in acts] + [hbm] * (npar + nh),
        out_shape=[_sds(a.shape) for a in acts] + [_sds(p.shape) for p in params] + _sibling_halves_shapes(halves_of),
        scratch_shapes=[pltpu.VMEM(p.shape, p.dtype) for p in params] + [pltpu.VMEM(p.shape, F32) for p in params]
        + (_sibling_halves_sems(nh) if nh else []),
        compiler_params=_cparams(("arbitrary",)),
    )(*acts, *params, *couts, *halves_of)
    if nh:
        return outs[:na], outs[na:na + npar], outs[na + npar:]
    return outs[:na], outs[na:]


def _tiled_matmul(name, a, b, mode, grid, a_spec, b_spec, o_spec, out_shape):
    nk = grid[2]
    dims = {"nn": ((1,), (0,)), "nt": ((1,), (1,)), "tn": ((0,), (0,))}[mode]

    def body(a_ref, b_ref, o_ref, acc):
        kk = pl.program_id(2)

        @pl.when(kk == 0)
        def _():
            acc[...] = jnp.zeros_like(acc)

        acc[...] += lax.dot_general(a_ref[...].astype(BF16), b_ref[...].astype(BF16), (dims, ((), ())),
                                    preferred_element_type=F32)

        @pl.when(kk == nk - 1)
        def _():
            o_ref[...] = acc[...]

    return pl.pallas_call(
        body, name=name, grid=grid, in_specs=[a_spec, b_spec], out_specs=o_spec, out_shape=_sds(out_shape),
        scratch_shapes=[pltpu.VMEM(o_spec.block_shape, F32)],
        compiler_params=_cparams(("parallel", "parallel", "arbitrary")),
    )(a, b)


def _mlp_weight_grad(name, a, b, layer, layers, split, into=None, tile=512):
    t, m = a.shape
    n = b.shape[1]
    tk = min(tile, t)
    tile = 2 * tile
    if split == "n":
        tm, tn = min(tile, m), min(tile, n // 4)
        per = n // 4 // tn
        shape = (4, layers, m, n // 4)
        o_idx = lambda i, j, k: (j // per, layer, i, j % per)
    else:
        tm, tn = min(tile, m // 4), min(tile, n)
        per = m // 4 // tm
        shape = (4, layers, m // 4, n)
        o_idx = lambda i, j, k: (i // per, layer, i % per, j)
    nk = t // tk

    def body(a_ref, b_ref, *rest):
        o_ref, acc = rest[-2:]
        kk = pl.program_id(2)

        @pl.when(kk == 0)
        def _():
            acc[...] = jnp.zeros_like(acc)

        acc[...] += lax.dot_general(a_ref[...].astype(BF16), b_ref[...].astype(BF16), (((0,), (0,)), ((), ())),
                                    preferred_element_type=F32)

        @pl.when(kk == nk - 1)
        def _():
            o_ref[...] = acc[...]

    in_specs = [pl.BlockSpec((tk, tm), lambda i, j, k: (k, i)), pl.BlockSpec((tk, tn), lambda i, j, k: (k, j))]
    operands = [a, b]
    aliases = {}
    if into is not None:
        in_specs.append(pl.BlockSpec(memory_space=pl.ANY))
        operands.append(into)
        aliases = {2: 0}
    return pl.pallas_call(
        body, name=name, grid=(m // tm, n // tn, nk), in_specs=in_specs,
        out_specs=pl.BlockSpec((None, None, tm, tn), o_idx), out_shape=_sds(shape), input_output_aliases=aliases,
        scratch_shapes=[pltpu.VMEM((tm, tn), F32)],
        compiler_params=_cparams(("parallel", "parallel", "arbitrary")),
    )(*operands)


S5_PACK = 8


def _s5_weight_grad(name, x, s, wide_rows, tk):
    t, c = x.shape
    wide = s.shape[1]
    kb, nb = S5_PACK * SSM_GROUP, S5_PACK * SSM_STATE
    nsb = c // kb
    x_spec = pl.BlockSpec((tk, kb), lambda i, j, k: (k, j % nsb))
    s_spec = pl.BlockSpec((tk, nb), lambda i, j, k: (k, j))
    if wide_rows:
        return _tiled_matmul(name, s, x, "tn", (1, wide // nb, t // tk), s_spec, x_spec,
                             pl.BlockSpec((nb, kb), lambda i, j, k: (j, 0)), (wide, kb))
    return _tiled_matmul(name, x, s, "tn", (1, wide // nb, t // tk), x_spec, s_spec,
                         pl.BlockSpec((kb, nb), lambda i, j, k: (0, j)), (kb, wide))


def _mlp_fwd(name, h, w1, w2, layer, ln_g, ln_b, tb, shards=()):
    t, c = h.shape
    nj, fc = w1.shape[0], w1.shape[3]
    nsh = len(shards)
    steps = (t // tb) * nj

    def body(h_ref, w1_ref, w2_ref, g_ref, b_ref, *rest):
        src, (out_ref, s_ref), dst = rest[:nsh], rest[nsh:nsh + 2], rest[nsh + 2:2 * nsh + 2]
        acc, sems = rest[2 * nsh + 2], rest[2 * nsh + 3:]
        j = pl.program_id(1)
        step = pl.program_id(0) * nj + j
        if nsh:
            start, forward, finish = _gather_big_phases(src, dst, sems)
            pl.when(step == 0)(start)

        @pl.when(j == 0)
        def _():
            acc[...] = jnp.zeros_like(acc)

        hid = jnp.dot(h_ref[...].astype(BF16), w1_ref[...].astype(BF16), preferred_element_type=F32)
        act = jnp.square(jnp.maximum(hid, 0.0))
        acc[...] += jnp.dot(act.astype(BF16), w2_ref[...].astype(BF16), preferred_element_type=F32)

        @pl.when(j == nj - 1)
        def _():
            s = DN_ALPHA * h_ref[...] + acc[...]
            s_ref[...] = s
            out_ref[...] = _ln(s, g_ref[...], b_ref[...])

        if nsh:
            pl.when(step == steps // 2)(forward)
            pl.when(step == steps - 1)(finish)

    row = pl.BlockSpec((tb, c), lambda i, j: (i, 0))
    vec = pl.BlockSpec((1, c), lambda i, j: (0, 0))
    outs = pl.pallas_call(
        body, name=name, grid=(t // tb, nj),
        in_specs=[row, pl.BlockSpec((None, None, c, fc), lambda i, j: (j, layer, 0, 0)),
                  pl.BlockSpec((None, None, fc, c), lambda i, j: (j, layer, 0, 0)), vec, vec] + [_HBM] * nsh,
        out_specs=[row, row] + [_HBM] * nsh,
        out_shape=[_sds((t, c)), _sds((t, c))] + [_sds((4,) + a.shape, a.dtype) for a in shards],
        scratch_shapes=[pltpu.VMEM((tb, c), F32)] + (_gather_big_sems(nsh) if nsh else []),
        compiler_params=_cparams(("arbitrary", "arbitrary")),
    )(h, w1, w2, ln_g, ln_b, *shards)
    return outs[0], outs[1], outs[2:]


def _mlp_bwd(name, h, s, dout, w1, w2, layer, ln_g, ln_b, tb):
    t, c = h.shape
    nj, fc = w1.shape[0], w1.shape[3]
    ff = nj * fc
    ni = t // tb
    nt = (((1,), (1,)), ((), ()))
    douts = dout if isinstance(dout, tuple) else (dout,)
    nd = len(douts)

    def body(h_ref, s_ref, *rest):
        dout_refs = rest[:nd]
        (w1_ref, w2_ref, g_ref, b_ref, dh_ref, ds_ref, dhid_ref, act_ref, dg_ref, db_ref,
         ds_scr, dh_acc, dg_acc, db_acc) = rest[nd:]
        i, j = pl.program_id(0), pl.program_id(1)

        @pl.when((i == 0) & (j == 0))
        def _():
            dg_acc[...] = jnp.zeros_like(dg_acc)
            db_acc[...] = jnp.zeros_like(db_acc)

        @pl.when(j == 0)
        def _():
            _, vjp = jax.vjp(_ln, s_ref[...], g_ref[...], b_ref[...])
            ds, dg, db = vjp(functools.reduce(jnp.add, [r[...] for r in dout_refs]))
            ds_scr[...] = ds
            ds_ref[...] = ds.astype(BF16)
            dh_acc[...] = DN_ALPHA * ds
            dg_acc[...] += dg
            db_acc[...] += db

        w1b, w2b = w1_ref[...].astype(BF16), w2_ref[...].astype(BF16)
        hid = jnp.dot(h_ref[...].astype(BF16), w1b, preferred_element_type=F32)
        rl = jnp.maximum(hid, 0.0)
        dact = lax.dot_general(ds_scr[...].astype(BF16), w2b, nt, preferred_element_type=F32)
        dhid = (dact * 2.0 * rl).astype(BF16)
        dh_acc[...] += lax.dot_general(dhid, w1b, nt, preferred_element_type=F32)
        dhid_ref[...] = dhid
        act_ref[...] = (rl * rl).astype(BF16)

        @pl.when(j == nj - 1)
        def _():
            dh_ref[...] = dh_acc[...]

        @pl.when((i == ni - 1) & (j == nj - 1))
        def _():
            dg_ref[...] = dg_acc[...]
            db_ref[...] = db_acc[...]

    row = pl.BlockSpec((tb, c), lambda i, j: (i, 0))
    vec = pl.BlockSpec((1, c), lambda i, j: (0, 0))
    wide = pl.BlockSpec((tb, fc), lambda i, j: (i, j))
    return pl.pallas_call(
        body, name=name, grid=(ni, nj),
        in_specs=[row, row] + [row] * nd + [pl.BlockSpec((None, None, c, fc), lambda i, j: (j, layer, 0, 0)),
                                            pl.BlockSpec((None, None, fc, c), lambda i, j: (j, layer, 0, 0)), vec, vec],
        out_specs=[row, row, wide, wide, vec, vec],
        out_shape=[_sds((t, c)), _sds((t, c), BF16), _sds((t, ff), BF16), _sds((t, ff), BF16), _sds((1, c)), _sds((1, c))],
        scratch_shapes=[pltpu.VMEM((tb, c), F32), pltpu.VMEM((tb, c), F32), pltpu.VMEM((1, c), F32), pltpu.VMEM((1, c), F32)],
        compiler_params=_cparams(("arbitrary", "arbitrary")),
    )(h, s, *douts, w1, w2, ln_g, ln_b)


def _load_heads(ref, hg):
    return jnp.stack([ref[:, h * HEAD:(h + 1) * HEAD] for h in range(hg)])


def _store_heads(ref, val):
    for h in range(val.shape[0]):
        ref[:, h * HEAD:(h + 1) * HEAD] = val[h]


def _rec_fwd(r, lw, k, v, a, b, hg, shards):
    t, c = r.shape
    n = HEAD
    nh = c // n
    ln = REC_CHUNK
    nck = t // ln
    ngrp = nh // hg
    nsh = len(shards)
    steps = ngrp * nck

    def body(r_ref, lw_ref, k_ref, v_ref, a_ref, b_ref, *rest):
        src, (o_ref, s0_ref), dst = rest[:nsh], rest[nsh:nsh + 2], rest[nsh + 2:2 * nsh + 2]
        state, sems = rest[2 * nsh + 2], rest[2 * nsh + 3:]
        step = pl.program_id(0) * nck + pl.program_id(1)
        start, forward, finish = _gather_big_phases(src, dst, sems)
        pl.when(step == 0)(start)

        @pl.when(pl.program_id(1) == 0)
        def _():
            state[...] = jnp.zeros_like(state)

        s0 = state[...]
        s0_ref[...] = s0
        o, s1 = _rec_chunk(s0, *(_load_heads(x, hg) for x in (r_ref, lw_ref, k_ref, v_ref, a_ref, b_ref)))
        _store_heads(o_ref, o)
        state[...] = s1
        pl.when(step == steps // 2)(forward)
        pl.when(step == steps - 1)(finish)

    seq = pl.BlockSpec((ln, hg * n), lambda g, i: (i, g))
    outs = pl.pallas_call(
        body, name="rec_fwd", grid=(ngrp, nck), in_specs=[seq] * 6 + [_HBM] * nsh,
        out_specs=[seq, pl.BlockSpec((None, hg, n, n), lambda g, i: (i, g, 0, 0))] + [_HBM] * nsh,
        out_shape=[_sds((t, c)), _sds((nck, nh, n, n))] + [_sds((4,) + s.shape, s.dtype) for s in shards],
        scratch_shapes=[pltpu.VMEM((hg, n, n), F32)] + _gather_big_sems(nsh),
        compiler_params=_cparams(("arbitrary", "arbitrary")),
    )(r, lw, k, v, a, b, *shards)
    return outs[0], outs[1], outs[2:]


def _rec_bwd(r, lw, k, v, a, b, s0s, do, hg, chip_sums):
    t, c = r.shape
    n = HEAD
    nh = c // n
    ln = REC_CHUNK
    nck = t // ln
    ngrp = nh // hg
    nsum = len(chip_sums)
    steps = ngrp * nck

    def body(r_ref, lw_ref, k_ref, v_ref, a_ref, b_ref, s0_ref, do_ref, *rest):
        src, grad_refs, land = rest[:nsum], rest[nsum:nsum + 6], rest[nsum + 6:2 * nsum + 6]
        dstate, sems = rest[2 * nsum + 6], rest[2 * nsum + 7:]
        step = pl.program_id(0) * nck + pl.program_id(1)
        start, finish = _scatter_big_phases(src, land, sems)
        pl.when(step == 0)(start)

        @pl.when(pl.program_id(1) == 0)
        def _():
            dstate[...] = jnp.zeros_like(dstate)

        _, vjp = jax.vjp(_rec_chunk, s0_ref[...], *(_load_heads(x, hg) for x in (r_ref, lw_ref, k_ref, v_ref, a_ref, b_ref)))
        ds0, *grads = vjp((_load_heads(do_ref, hg), dstate[...]))
        dstate[...] = ds0
        for ref, val in zip(grad_refs, grads):
            _store_heads(ref, val)
        pl.when(step == steps - 1)(finish)

    seq = pl.BlockSpec((ln, hg * n), lambda g, i: (nck - 1 - i, g))
    outs = pl.pallas_call(
        body, name="rec_bwd", grid=(ngrp, nck),
        in_specs=[seq] * 6 + [pl.BlockSpec((None, hg, n, n), lambda g, i: (nck - 1 - i, g, 0, 0)), seq] + [_HBM] * nsum,
        out_specs=[seq] * 6 + [_HBM] * nsum,
        out_shape=[_sds((t, c))] * 6 + [_sds((3,) + s.shape[1:], s.dtype) for s in chip_sums],
        scratch_shapes=[pltpu.VMEM((hg, n, n), F32)] + _scatter_big_sems(nsum),
        compiler_params=_cparams(("arbitrary", "arbitrary")),
    )(r, lw, k, v, a, b, s0s, do, *chip_sums)
    return outs[:6], outs[6:]


def _s5_blocks(c):
    kb, nb = S5_PACK * SSM_GROUP, S5_PACK * SSM_STATE
    return kb, nb, c // kb


def _s5_fwd(h, bc, abar, cc, tb, shards=()):
    t, c = h.shape
    w2 = bc.shape[1]
    w = w2 // 2
    kb, nb, nsb = _s5_blocks(c)

    nsh = len(shards)
    steps = t // tb

    def body(h_ref, bc_ref, a_ref, cc_ref, *rest):
        src, (s_ref, y_ref), dst = rest[:nsh], rest[nsh:nsh + 2], rest[nsh + 2:2 * nsh + 2]
        carry, rows, sems = rest[2 * nsh + 2], rest[2 * nsh + 3], rest[2 * nsh + 4:]
        if nsh:
            start, forward, finish = _gather_big_phases(src, dst, sems)
            pl.when(pl.program_id(0) == 0)(start)

        @pl.when(pl.program_id(0) == 0)
        def _():
            carry[...] = jnp.zeros_like(carry)

        for j in range(w2 // nb):
            ch = (j % nsb) * kb
            rows[:, j * nb:(j + 1) * nb] = jnp.dot(h_ref[:, ch:ch + kb].astype(BF16), bc_ref[:, j * nb:(j + 1) * nb],
                                                   preferred_element_type=F32)
        ar, ai = a_ref[:, :w], a_ref[:, w:]

        def step(i, state):
            hr, hi = state
            nr = ar * hr - ai * hi + rows[pl.ds(i, 1), :w]
            ni = ar * hi + ai * hr + rows[pl.ds(i, 1), w:]
            rows[pl.ds(i, 1), :w] = nr
            rows[pl.ds(i, 1), w:] = ni
            return nr, ni

        hr, hi = lax.fori_loop(0, tb, step, (carry[:, :w], carry[:, w:]))
        carry[:, :w] = hr
        carry[:, w:] = hi
        s_ref[...] = rows[...].astype(BF16)
        for j in range(nsb):
            re, im = j * nb, w + j * nb
            y_ref[:, j * kb:(j + 1) * kb] = (
                jnp.dot(s_ref[:, re:re + nb], cc_ref[re:re + nb, :], preferred_element_type=F32)
                + jnp.dot(s_ref[:, im:im + nb], cc_ref[im:im + nb, :], preferred_element_type=F32))
        if nsh:
            pl.when(pl.program_id(0) == steps // 2)(forward)
            pl.when(pl.program_id(0) == steps - 1)(finish)

    outs = pl.pallas_call(
        body, name="s5_fwd", grid=(steps,),
        in_specs=[pl.BlockSpec((tb, c), lambda i: (i, 0)), _full_spec(bc.shape), _full_spec(abar.shape), _full_spec(cc.shape)]
        + [_HBM] * nsh,
        out_specs=[pl.BlockSpec((tb, w2), lambda i: (i, 0)), pl.BlockSpec((tb, c), lambda i: (i, 0))] + [_HBM] * nsh,
        out_shape=[_sds((t, w2), BF16), _sds((t, c))] + [_sds((4,) + a.shape, a.dtype) for a in shards],
        scratch_shapes=[pltpu.VMEM((1, w2), F32), pltpu.VMEM((tb, w2), F32)] + (_gather_big_sems(nsh) if nsh else []),
        compiler_params=_cparams(("arbitrary",)),
    )(h, bc, abar, cc, *shards)
    return outs[0], outs[1], outs[2:]


def _s5_bwd(dy, s, abar, cc, bc, tb):
    t, c = dy.shape
    w2 = s.shape[1]
    w = w2 // 2
    kb, nb, nsb = _s5_blocks(c)
    nblk = t // tb
    pack = 16
    per = tb // pack
    nt = (((1,), (1,)), ((), ()))

    def body(dy_ref, s_ref, sprev_ref, a_ref, cc_ref, bc_ref, dbu_ref, dh_ref, da_ref, carry, da_acc, rows):
        i = pl.program_id(0)

        @pl.when(i == 0)
        def _():
            carry[...] = jnp.zeros_like(carry)
            da_acc[...] = jnp.zeros_like(da_acc)

        for j in range(w2 // nb):
            ch = (j % nsb) * kb
            rows[:, j * nb:(j + 1) * nb] = lax.dot_general(dy_ref[:, ch:ch + kb].astype(BF16), cc_ref[j * nb:(j + 1) * nb, :], nt,
                                                           preferred_element_type=F32)
        ar, ai = a_ref[:, :w], a_ref[:, w:]

        def step(n, state):
            gr, gi = state
            row = tb - 1 - n
            nr = rows[pl.ds(row, 1), :w] + ar * gr + ai * gi
            ni = rows[pl.ds(row, 1), w:] + ar * gi - ai * gr
            rows[pl.ds(row, 1), :w] = nr
            rows[pl.ds(row, 1), w:] = ni
            return nr, ni

        gr, gi = lax.fori_loop(0, tb, step, (carry[:, :w], carry[:, w:]))
        carry[:, :w] = gr
        carry[:, w:] = gi
        last = (lax.broadcasted_iota(jnp.int32, (pack, w2), 0) == pack - 1) & (i < nblk - 1)
        before = jnp.sum(jnp.where(last, sprev_ref[...].astype(F32), 0.0), axis=0, keepdims=True)
        rid = lax.broadcasted_iota(jnp.int32, (tb, w2), 0)
        sp = jnp.where(rid == 0, before, pltpu.roll(s_ref[...].astype(F32), 1, 0))
        g = rows[...]
        dbu_ref[...] = g.astype(BF16)
        spr, spi, g_r, g_i = sp[:, :w], sp[:, w:], g[:, :w], g[:, w:]
        da_acc[:, :w] += jnp.sum(spr * g_r + spi * g_i, axis=0, keepdims=True)
        da_acc[:, w:] += jnp.sum(spr * g_i - spi * g_r, axis=0, keepdims=True)
        for j in range(nsb):
            re, im = j * nb, w + j * nb
            dh_ref[:, j * kb:(j + 1) * kb] = (
                lax.dot_general(dbu_ref[:, re:re + nb], bc_ref[:, re:re + nb], nt, preferred_element_type=F32)
                + lax.dot_general(dbu_ref[:, im:im + nb], bc_ref[:, im:im + nb], nt, preferred_element_type=F32))

        @pl.when(i == nblk - 1)
        def _():
            da_ref[...] = da_acc[...]

    wide = pl.BlockSpec((tb, w2), lambda i: (nblk - 1 - i, 0))
    narrow = pl.BlockSpec((tb, c), lambda i: (nblk - 1 - i, 0))
    prev = pl.BlockSpec((pack, w2), lambda i: (jnp.maximum((nblk - 1 - i) * per - 1, 0), 0))
    return pl.pallas_call(
        body, name="s5_bwd", grid=(nblk,),
        in_specs=[narrow, wide, prev, _full_spec(abar.shape), _full_spec(cc.shape), _full_spec(bc.shape)],
        out_specs=[wide, narrow, pl.BlockSpec((1, w2), lambda i: (0, 0))],
        out_shape=[_sds((t, w2), BF16), _sds((t, c)), _sds((1, w2))],
        scratch_shapes=[pltpu.VMEM((1, w2), F32), pltpu.VMEM((1, w2), F32), pltpu.VMEM((tb, w2), F32)],
        compiler_params=_cparams(("arbitrary",)),
    )(dy, s, s, abar, cc, bc)


def _zoh_fwd(a_re, a_im, log_dt, b_re_t, b_im_t):
    def body(*refs):
        for r, val in zip(refs[5:], _f_zoh(*(x[...] for x in refs[:5]))):
            r[...] = val

    return pl.pallas_call(body, name="s5_zoh_fwd", out_shape=[_sds(a_re.shape)] * 2 + [_sds(b_re_t.shape)] * 2,
                          compiler_params=_cparams())(a_re, a_im, log_dt, b_re_t, b_im_t)


def _zoh_bwd(a_re, a_im, log_dt, b_re_t, b_im_t, couts):
    def body(*refs):
        _, vjp = jax.vjp(_f_zoh, *(x[...] for x in refs[:5]))
        for r, val in zip(refs[9:], vjp(tuple(x[...] for x in refs[5:9]))):
            r[...] = val

    ins = (a_re, a_im, log_dt, b_re_t, b_im_t)
    return pl.pallas_call(body, name="s5_zoh_bwd", out_shape=[_sds(x.shape) for x in ins],
                          compiler_params=_cparams())(*ins, *couts)


def _loss_head(h, target, tb):
    t, c = h.shape
    nb = t // tb

    def body(h_ref, t_ref, loss_ref, dh_ref, acc):
        i = pl.program_id(0)

        @pl.when(i == 0)
        def _():
            acc[...] = jnp.zeros_like(acc)

        d = h_ref[...] - t_ref[...]
        dh_ref[...] = d * (1.0 / c)
        acc[...] += 0.5 * jnp.sum(jnp.mean(d * d, axis=-1, keepdims=True), axis=0, keepdims=True)

        @pl.when(i == nb - 1)
        def _():
            loss_ref[...] = jnp.broadcast_to(acc[...], loss_ref.shape)

    row = pl.BlockSpec((tb, c), lambda i: (i, 0))
    return pl.pallas_call(
        body, name="loss_head", grid=(nb,), in_specs=[row, row],
        out_specs=[pl.BlockSpec((8, 128), lambda i: (0, 0)), row], out_shape=[_sds((8, 128)), _sds((t, c))],
        scratch_shapes=[pltpu.VMEM((1, 1), F32)], compiler_params=_cparams(("arbitrary",)),
    )(h, target)


def _rows_tile(rows):
    for cand in (512, 256, 128, 64, 32, 16, 8):
        if rows % cand == 0:
            return cand
    return rows


def _addn(name, arrs):
    rows, cols = arrs[0].shape
    tb = _rows_tile(rows)

    def body(*refs):
        acc = refs[0][...]
        for r in refs[1:-1]:
            acc = acc + r[...]
        refs[-1][...] = acc

    blk = pl.BlockSpec((tb, cols), lambda i: (i, 0))
    return pl.pallas_call(body, name=name, grid=(rows // tb,), in_specs=[blk] * len(arrs), out_specs=blk,
                          out_shape=_sds((rows, cols)), compiler_params=_cparams(("parallel",)))(*arrs)


def _adamw_math(w, g, m, v):
    m = ADAM_B1 * m + (1.0 - ADAM_B1) * g
    v = ADAM_B2 * v + (1.0 - ADAM_B2) * jnp.square(g)
    m_hat = m / (1.0 - ADAM_B1 ** ADAM_STEP)
    v_hat = v / (1.0 - ADAM_B2 ** ADAM_STEP)
    delta = -ADAM_LR * (m_hat / (jnp.sqrt(v_hat) + ADAM_EPS) + ADAM_WD * w)
    return delta, m, v


def _adamw(name, parts, w, m, v):
    rows, cols = w.shape
    tb = _rows_tile(rows)
    npart = len(parts)

    def body(*refs):
        g = refs[0][...]
        for r in refs[1:npart]:
            g = g + r[...]
        w_ref, m_ref, v_ref = refs[npart:npart + 3]
        g_out, d_out, m_out, v_out = refs[npart + 3:]
        delta, mn, vn = _adamw_math(w_ref[...], g, m_ref[...], v_ref[...])
        g_out[...] = g
        d_out[...] = delta
        m_out[...] = mn
        v_out[...] = vn

    blk = pl.BlockSpec((tb, cols), lambda i: (i, 0))
    return pl.pallas_call(body, name=name, grid=(rows // tb,), in_specs=[blk] * (npart + 3), out_specs=[blk] * 4,
                          out_shape=[_sds((rows, cols))] * 4, compiler_params=_cparams(("parallel",)))(*parts, w, m, v)


def _shift_down(a):
    return jnp.concatenate([jnp.zeros_like(a[:1]), a[:-1]], axis=0)


def _shift_up(a):
    return jnp.concatenate([a[1:], jnp.zeros_like(a[:1])], axis=0)


def _s5_pack_mask(g):
    return (jnp.arange(g)[None, :] % S5_PACK == jnp.arange(S5_PACK)[:, None]).astype(F32)


def _compact_b(bbar_t):
    g, s, p = bbar_t.shape
    return (_s5_pack_mask(g)[:, None, :, None] * bbar_t.transpose(1, 0, 2)[None]).reshape(S5_PACK * s, g * p)


def _compact_b_t(dense, g):
    s, p = dense.shape[0] // S5_PACK, dense.shape[1] // g
    return jnp.sum(dense.reshape(S5_PACK, s, g, p) * _s5_pack_mask(g)[:, None, :, None], axis=0).transpose(1, 0, 2)


def _compact_c(c_w):
    g, s, p = c_w.shape
    return (c_w.transpose(0, 2, 1)[:, :, None, :] * _s5_pack_mask(g).T[:, None, :, None]).reshape(g * p, S5_PACK * s)


def _compact_c_t(dense, g):
    p, s = dense.shape[0] // g, dense.shape[1] // S5_PACK
    return jnp.sum(dense.reshape(g, p, S5_PACK, s) * _s5_pack_mask(g).T[:, None, :, None], axis=2).transpose(0, 2, 1)


def _local_step(x, target, fw, core):
    t, c = x.shape
    nh = c // HEAD
    ng = c // SSM_GROUP
    tb = min(256, t)
    tbb = min(128, t)
    tbm = min(512, t)
    tbmb = min(512, t)
    tbs = min(256, t)
    tk5 = min(2048, t)
    hg = min(16, nh)
    mu = [fw['rw_mu'][i:i + 1] for i in range(6)]
    ln_g = [fw['ln_g'][i:i + 1] for i in range(4)]
    ln_b = [fw['ln_b'][i:i + 1] for i in range(4)]
    grads = {}

    xp = _shift_down(x)
    proj_params = {n: (mu[i], fw['rw_w' + n]) for n, i in (('r', 0), ('k', 2), ('v', 3))}
    raw = {n: _stage_fwd("proj_" + n, _f_proj, (x, xp), proj_params[n], (c,), tb)[0] for n in 'rkv'}
    lora_params = (mu[1], mu[4], mu[5], fw['rw_w0'], fw['rw_w1'], fw['rw_w2'], fw['rw_a0'], fw['rw_a1'], fw['rw_a2'],
                   fw['rw_g1'], fw['rw_g2'], fw['rw_k_k'], fw['rw_k_a'])
    lw, k2, an, bb, gate = _stage_fwd("lora", _f_lora, (x, xp, raw['k']), lora_params, (c,) * 5, tb)
    rec_in = (raw['r'], lw, k2, raw['v'], an, bb)
    o, s0s, (wo_view, w1_l0, w2_l0) = _rec_fwd(*rec_in, hg, fw['late_a'])
    fw = dict(fw, rw_wo=wo_view.reshape(c, c))
    mlp_w = [(w1_l0.reshape(4, 1, c, -1), w2_l0.reshape(4, 1, -1, c)), None]
    post_params = (fw['rw_lnx_g'], fw['rw_lnx_b'], fw['rw_r_k'], fw['rw_wo'], ln_g[0], ln_b[0])
    post_acts = (o, raw['r'], k2, raw['v'], gate, x)
    h1, = _stage_fwd("post", _f_post, post_acts, post_params, (c,), tb)
    h2, s_mlp0, (w1_l1,) = _mlp_fwd("mlp0_fwd", h1, *mlp_w[0], 0, ln_g[1], ln_b[1], tbm, fw['late_b'])

    a_re, a_im, log_dt = fw['s5_a_re'], fw['s5_a_im'], fw['s5_log_dt']
    b_re_t, b_im_t = fw['s5_b_re'].transpose(0, 2, 1), fw['s5_b_im'].transpose(0, 2, 1)
    abar_re, abar_im, bbar_re_t, bbar_im_t = _zoh_fwd(a_re, a_im, log_dt, b_re_t, b_im_t)
    abar = jnp.concatenate([abar_re.reshape(1, -1), abar_im.reshape(1, -1)], axis=1)
    bc = jnp.concatenate([_compact_b(bbar_re_t), _compact_b(bbar_im_t)], axis=1).astype(BF16)
    cc = jnp.concatenate([_compact_c(fw['s5_c_re']), -_compact_c(fw['s5_c_im'])], axis=0).astype(BF16)
    st, ys, (glu_view, w2_l1) = _s5_fwd(h2, bc, abar, cc, tbs, fw['late_c'])
    mlp_w[1] = (w1_l1.reshape(4, 1, c, -1), w2_l1.reshape(4, 1, -1, c))
    fw = dict(fw, s5_w_glu=tuple(glu_view[q] for q in range(4)))
    glu_params = (fw['s5_d'], *fw['s5_w_glu'], ln_g[2], ln_b[2])
    h3, = _stage_fwd("glu", _f_glu, (ys, h2), glu_params, (c,), tb)
    h4, s_mlp1, _ = _mlp_fwd("mlp1_fwd", h3, *mlp_w[1], 0, ln_g[3], ln_b[3], tbm)

    loss_blk, dh4 = _loss_head(h4, target, tb)

    dln_g, dln_b = [None] * 4, [None] * 4
    dh3, ds1, dhid1, act1, dln_g[3], dln_b[3] = _mlp_bwd("mlp1_bwd", h3, s_mlp1, dh4, *mlp_w[1], 0,
                                                         ln_g[3], ln_b[3], tbmb)
    dw1 = _mlp_weight_grad("mlp1_dw1", h3, dhid1, 1, DEPTH, "n")
    dw2 = _mlp_weight_grad("mlp1_dw2", act1, ds1, 1, DEPTH, "m")
    (dys, dh2_glu), (grads['s5_d'], *dglu, dln_g[2], dln_b[2]) = _stage_bwd(
        "glu_bwd", _f_glu, (ys, h2), glu_params, (dh3,), tbb, proxied=(1, 2, 3, 4))
    grads['s5_w_glu'] = jnp.stack(dglu)
    dcc = _s5_weight_grad("s5_dcc", dys, st, True, tk5)
    dbu, dh2_bu, dabar = _s5_bwd(dys, st, abar, cc, bc, tbs)
    dbc = _s5_weight_grad("s5_dbc", h2, dbu, False, tk5)
    gp = ng * SSM_STATE
    grads['s5_c_re'] = _compact_c_t(dcc[:gp], ng)
    grads['s5_c_im'] = -_compact_c_t(dcc[gp:], ng)
    zoh_couts = (dabar[:, :gp].reshape(ng, SSM_STATE), dabar[:, gp:].reshape(ng, SSM_STATE),
                 _compact_b_t(dbc[:, :gp], ng), _compact_b_t(dbc[:, gp:], ng))
    grads['s5_a_re'], grads['s5_a_im'], grads['s5_log_dt'], db_re_t, db_im_t = _zoh_bwd(
        a_re, a_im, log_dt, b_re_t, b_im_t, zoh_couts)
    grads['s5_b_re'], grads['s5_b_im'] = db_re_t.transpose(0, 2, 1), db_im_t.transpose(0, 2, 1)
    dh2 = (dh2_glu, dh2_bu)

    dh1, ds0, dhid0, act0, dln_g[1], dln_b[1] = _mlp_bwd("mlp0_bwd", h1, s_mlp0, dh2, *mlp_w[0], 0,
                                                         ln_g[1], ln_b[1], tbmb)
    grads['mlp_w1'] = _mlp_weight_grad("mlp0_dw1", h1, dhid0, 0, DEPTH, "n", into=dw1)
    grads['mlp_w2'] = _mlp_weight_grad("mlp0_dw2", act0, ds0, 0, DEPTH, "m", into=dw2)
    ready_views = [grads[n].reshape(4, -1, grads[n].shape[-1]) for n in BIG_READY]
    (do, dr_p, dk2_p, dv_p, dgate, dx_post), post_g, ready_others = _stage_bwd(
        "post_bwd", _f_post, post_acts, post_params, (dh1,), tbb, proxied=(3,), halves_of=ready_views)
    grads['rw_lnx_g'], grads['rw_lnx_b'], grads['rw_r_k'], grads['rw_wo'], dln_g[0], dln_b[0] = post_g
    ready_sums = [_half_add(f"half_add_a{i}", v, o, core) for i, (v, o) in enumerate(zip(ready_views, ready_others))]
    rec_g, ready_lands = _rec_bwd(*rec_in, s0s, do, hg, ready_sums)
    reduced = dict(zip(BIG_READY, zip(ready_sums, ready_lands)))
    dr_r, dlw, dk2_r, dv_r, dan, dbb = rec_g
    dk2 = (dk2_p, dk2_r)
    (dx_l, dxp_l, dkraw_l), lora_g = _stage_bwd("lora_bwd", _f_lora, (x, xp, raw['k']), lora_params,
                                                (dlw, dk2, dan, dbb, dgate), tbb)
    (dmu_w, dmu_a, dmu_g, grads['rw_w0'], grads['rw_w1'], grads['rw_w2'], grads['rw_a0'], grads['rw_a1'], grads['rw_a2'],
     grads['rw_g1'], grads['rw_g2'], grads['rw_k_k'], grads['rw_k_a']) = lora_g
    dproj = {'r': (dr_p, dr_r), 'k': dkraw_l, 'v': (dv_p, dv_r)}
    dxs, dxps, dmu = [dx_post, dx_l], [dxp_l], {}
    for n in 'rkv':
        (dx_n, dxp_n), (dmu[n], grads['rw_w' + n]) = _stage_bwd("proj_bwd_" + n, _f_proj, (x, xp), proj_params[n],
                                                                 (dproj[n],), tb, proxied=(1,))
        dxs.append(dx_n)
        dxps.append(dxp_n)
    grads['rw_mu'] = jnp.concatenate([dmu['r'], dmu_w, dmu['k'], dmu['v'], dmu_a, dmu_g], axis=0)
    grads['ln_g'] = jnp.concatenate(dln_g, axis=0)
    grads['ln_b'] = jnp.concatenate(dln_b, axis=0)
    grad_x = _addn("grad_x", dxs + [_shift_up(_addn("dxp_sum", dxps))])
    late = [n for n in BIG if n not in BIG_READY]
    late_sums = dict(zip(late, _chip_sums("b", [grads[n] for n in late], core)))
    return loss_blk, grad_x, grads, reduced, late_sums


def _position():
    return lax.axis_index("x"), lax.axis_index("y"), lax.axis_index("c")


def _other_chips(x, y):
    return [(1 - x, y), (x, 1 - y), (1 - x, 1 - y)]


def _chip_slice(ref, axis, q, size):
    idx = [slice(None)] * len(ref.shape)
    idx[axis] = pl.ds(pl.multiple_of(q * size, size), size)
    return ref.at[tuple(idx)]


_HBM = pl.BlockSpec(memory_space=pltpu.HBM)


def _gather_small_phases(src, dst, axes, sems):
    n = len(src)
    send_sems, recv_sems, own_sems = sems
    x, y, c = _position()
    chips = _other_chips(x, y)
    sizes = [src[a].shape[axes[a]] for a in range(n)]

    def copy(a, k, q):
        return pltpu.make_async_remote_copy(
            src_ref=src[a], dst_ref=_chip_slice(dst[a], axes[a], q, sizes[a]), send_sem=send_sems.at[a, k],
            recv_sem=recv_sems.at[a, k], device_id=(*chips[k], c), device_id_type=MESH)

    def own(a):
        return pltpu.make_async_copy(src[a], _chip_slice(dst[a], axes[a], 2 * x + y, sizes[a]), own_sems.at[a])

    def start():
        for a in range(n):
            own(a).start()
            for k in range(3):
                copy(a, k, 2 * x + y).start()

    def finish():
        for a in range(n):
            for k, (cx, cy) in enumerate(chips):
                copy(a, k, 2 * cx + cy).wait_recv()
        for a in range(n):
            for k in range(3):
                copy(a, k, 2 * x + y).wait_send()
            own(a).wait()

    return start, finish


def _gather_early(big, small, axes):
    nb, ns = len(big), len(small)
    full_shapes = [tuple(s * 4 if i == ax else s for i, s in enumerate(a.shape)) for a, ax in zip(small, axes)]

    def body(*refs):
        src_b, src_s = refs[:nb], refs[nb:nb + ns]
        dst_b, dst_s = refs[nb + ns:2 * nb + ns], refs[2 * nb + ns:2 * (nb + ns)]
        sems = refs[2 * (nb + ns):]
        small_start, small_finish = _gather_small_phases(src_s, dst_s, axes, sems[5:])
        small_start()
        for phase in _gather_big_phases(src_b, dst_b, sems[:5]):
            phase()
        small_finish()

    outs = pl.pallas_call(
        body, name="gather_early", in_specs=[_HBM] * (nb + ns), out_specs=[_HBM] * (nb + ns),
        out_shape=[_sds((4,) + a.shape, a.dtype) for a in big] + [_sds(s, a.dtype) for s, a in zip(full_shapes, small)],
        scratch_shapes=_gather_big_sems(nb) + [pltpu.SemaphoreType.DMA((ns, 3)), pltpu.SemaphoreType.DMA((ns, 3)),
                                               pltpu.SemaphoreType.DMA((ns,))],
        compiler_params=_cparams(),
    )(*big, *small)
    return outs[:nb], outs[nb:]


def _scatter_pieces(fulls, axes, sums):
    n, nsum = len(fulls), len(sums)
    sizes = [a.shape[ax] // 4 for a, ax in zip(fulls, axes)]
    shard_shapes = [tuple(sz if i == ax else s for i, s in enumerate(a.shape)) for a, ax, sz in zip(fulls, axes, sizes)]

    def body(*refs):
        src, big_src = refs[:n], refs[n:n + nsum]
        land, big_land = refs[n + nsum:2 * n + nsum], refs[2 * n + nsum:2 * (n + nsum)]
        send_sems, recv_sems = refs[2 * (n + nsum):2 * (n + nsum) + 2]
        big_start, big_finish = _scatter_big_phases(big_src, big_land, refs[2 * (n + nsum) + 2:])
        big_start()
        x, y, c = _position()
        chips = _other_chips(x, y)

        def copy(a, k):
            cx, cy = chips[k]
            return pltpu.make_async_remote_copy(
                src_ref=_chip_slice(src[a], axes[a], 2 * cx + cy, sizes[a]), dst_ref=land[a].at[k],
                send_sem=send_sems.at[a, k], recv_sem=recv_sems.at[a, k], device_id=(cx, cy, c), device_id_type=MESH)

        for a in range(n):
            for k in range(3):
                copy(a, k).start()
        for a in range(n):
            for k in range(3):
                copy(a, k).wait_recv()
        for a in range(n):
            for k in range(3):
                copy(a, k).wait_send()
        big_finish()

    outs = pl.pallas_call(
        body, name="scatter_grads", in_specs=[_HBM] * (n + nsum), out_specs=[_HBM] * (n + nsum),
        out_shape=[_sds((3,) + s) for s in shard_shapes] + [_sds((3,) + s.shape[1:], s.dtype) for s in sums],
        scratch_shapes=[pltpu.SemaphoreType.DMA((n, 3)), pltpu.SemaphoreType.DMA((n, 3))] + _scatter_big_sems(nsum),
        compiler_params=_cparams(),
    )(*fulls, *sums)
    return outs[:n], outs[n:]


def _sibling_swap(name, arrs):
    n = len(arrs)

    def body(*refs):
        src, dst = refs[:n], refs[n:2 * n]
        send_sems, recv_sems = refs[2 * n:]
        x, y, c = _position()
        copies = [pltpu.make_async_remote_copy(src_ref=src[a], dst_ref=dst[a], send_sem=send_sems.at[a], recv_sem=recv_sems.at[a],
                                               device_id=(x, y, 1 - c), device_id_type=MESH) for a in range(n)]
        for cp in copies:
            cp.start()
        for cp in copies:
            cp.wait_recv()
        for cp in copies:
            cp.wait_send()

    return pl.pallas_call(
        body, name=name, in_specs=[_HBM] * n, out_specs=[_HBM] * n, out_shape=[_sds(a.shape) for a in arrs],
        scratch_shapes=[pltpu.SemaphoreType.DMA((n,)), pltpu.SemaphoreType.DMA((n,))],
        compiler_params=_cparams(),
    )(*arrs)


def _sum4(name, own, land):
    rows, cols = own.shape
    tb = _rows_tile(rows)

    def body(o_ref, l0, l1, l2, out_ref):
        out_ref[...] = ((o_ref[...] + l0[...]) + l1[...]) + l2[...]

    blk = pl.BlockSpec((tb, cols), lambda i: (i, 0))
    lands = [pl.BlockSpec((None, tb, cols), functools.partial(lambda k, i: (k, i, 0), k)) for k in range(3)]
    return pl.pallas_call(body, name=name, grid=(rows // tb,), in_specs=[blk] + lands, out_specs=blk,
                          out_shape=_sds((rows, cols)), compiler_params=_cparams(("parallel",)))(own, land, land, land)


def _allreduce_adamw_small(g, w, m, v):
    rows, lanes = g.shape

    def body(g_ref, w_ref, m_ref, v_ref, gs_ref, d_ref, mn_ref, vn_ref, land, send_sems, recv_sems):
        x, y, c = _position()
        me = 4 * x + 2 * y + c
        masks = [(bx, by, bc) for bx in (0, 1) for by in (0, 1) for bc in (0, 1)][1:]

        def peer(mask):
            return (x ^ mask[0], y ^ mask[1], c ^ mask[2])

        def copy(j, slot):
            return pltpu.make_async_remote_copy(src_ref=g_ref, dst_ref=land.at[slot], send_sem=send_sems.at[j],
                                                recv_sem=recv_sems.at[j], device_id=peer(masks[j]), device_id_type=MESH)

        for j in range(7):
            copy(j, me).start()
        land[me] = g_ref[...]
        for j in range(7):
            px, py, pc = peer(masks[j])
            copy(j, 4 * px + 2 * py + pc).wait_recv()
        for j in range(7):
            copy(j, me).wait_send()
        total = land[0]
        for dev in range(1, 8):
            total = total + land[dev]
        delta, mn, vn = _adamw_math(w_ref[...], total, m_ref[...], v_ref[...])
        gs_ref[...] = total
        d_ref[...] = delta
        mn_ref[...] = mn
        vn_ref[...] = vn

    vmem = pl.BlockSpec(memory_space=pltpu.VMEM)
    return pl.pallas_call(
        body, name="allreduce_adamw_small", in_specs=[vmem] * 4, out_specs=[vmem] * 4, out_shape=[_sds((rows, lanes))] * 4,
        scratch_shapes=[pltpu.VMEM((8, rows, lanes), F32), pltpu.SemaphoreType.DMA((7,)), pltpu.SemaphoreType.DMA((7,))],
        compiler_params=_cparams(),
    )(g, w, m, v)


def _row_half(ref, c):
    r2 = ref.shape[-2] // 2
    lead = (slice(None),) * (len(ref.shape) - 2)
    return ref.at[(*lead, pl.ds(pl.multiple_of(c * r2, r2), r2), slice(None))]


def _gather_big_phases(src, dst, sems):
    n = len(src)
    ici_send, ici_recv, d2d_send, d2d_recv, own_sems = sems
    x, y, c = _position()
    me = 2 * x + y
    chips = _other_chips(x, y)
    ids = [2 * cx + cy for cx, cy in chips]

    def ici(a, k, q):
        return pltpu.make_async_remote_copy(
            src_ref=_row_half(src[a], c), dst_ref=_row_half(dst[a].at[q], c), send_sem=ici_send.at[a, k],
            recv_sem=ici_recv.at[a, k], device_id=(*chips[k], c), device_id_type=MESH)

    def d2d(a, k, half):
        where = _row_half(dst[a].at[ids[k]], half)
        return pltpu.make_async_remote_copy(src_ref=where, dst_ref=where, send_sem=d2d_send.at[a, k], recv_sem=d2d_recv.at[a, k],
                                            device_id=(x, y, 1 - c), device_id_type=MESH)

    def own(a):
        return pltpu.make_async_copy(src[a], dst[a].at[me], own_sems.at[a])

    def start():
        for a in range(n):
            own(a).start()
            for k in range(3):
                ici(a, k, me).start()

    def forward():
        for a in range(n):
            for k in range(3):
                ici(a, k, ids[k]).wait_recv()
                d2d(a, k, c).start()

    def finish():
        for a in range(n):
            for k in range(3):
                d2d(a, k, 1 - c).wait_recv()
        for a in range(n):
            for k in range(3):
                ici(a, k, me).wait_send()
                d2d(a, k, c).wait_send()
            own(a).wait()

    return start, forward, finish


def _gather_big_sems(n):
    return [pltpu.SemaphoreType.DMA((n, 3))] * 4 + [pltpu.SemaphoreType.DMA((n,))]


def _chip_sums(tag, grads, core):
    views = [g.reshape(4, -1, g.shape[-1]) for g in grads]
    others = _sibling_halves("sibling_halves_" + tag, views)
    return [_half_add(f"half_add_{tag}{i}", v, o, core) for i, (v, o) in enumerate(zip(views, others))]


def _sibling_halves_phases(src, dst, sems):
    n = len(src)
    send_sems, recv_sems = sems
    x, y, c = _position()

    def copy(a):
        return pltpu.make_async_remote_copy(src_ref=_row_half(src[a], 1 - c), dst_ref=dst[a], send_sem=send_sems.at[a],
                                            recv_sem=recv_sems.at[a], device_id=(x, y, 1 - c), device_id_type=MESH)

    def start():
        for a in range(n):
            copy(a).start()

    def finish():
        for a in range(n):
            copy(a).wait_recv()
        for a in range(n):
            copy(a).wait_send()

    return start, finish


def _sibling_halves_sems(n):
    return [pltpu.SemaphoreType.DMA((n,)), pltpu.SemaphoreType.DMA((n,))]


def _sibling_halves_shapes(views):
    return [_sds((4, v.shape[1] // 2, v.shape[2])) for v in views]


def _sibling_halves(name, views):
    n = len(views)

    def body(*refs):
        for phase in _sibling_halves_phases(refs[:n], refs[n:2 * n], refs[2 * n:]):
            phase()

    return pl.pallas_call(
        body, name=name, in_specs=[_HBM] * n, out_specs=[_HBM] * n, out_shape=_sibling_halves_shapes(views),
        scratch_shapes=_sibling_halves_sems(n), compiler_params=_cparams(),
    )(*views)


def _rows_tile_capped(rows, cap=256):
    return min(_rows_tile(rows), cap)


def _half_add(name, view, other, core):
    _, r, k = view.shape
    r2 = r // 2
    tr = _rows_tile_capped(r2)
    per = r2 // tr

    def body(c_ref, v_ref, o_ref, out_ref):
        out_ref[...] = (v_ref[...] + o_ref[...]).astype(BF16)

    blk = pl.BlockSpec((None, tr, k), lambda q, i, c: (q, i, 0))
    return pl.pallas_call(
        body, name=name,
        grid_spec=pltpu.PrefetchScalarGridSpec(
            num_scalar_prefetch=1, grid=(4, per),
            in_specs=[pl.BlockSpec((None, tr, k), lambda q, i, c: (q, c[0] * per + i, 0)), blk], out_specs=blk),
        out_shape=_sds((4, r2, k), BF16), compiler_params=_cparams(("parallel", "parallel")),
    )(core, view, other)


def _scatter_big_phases(src, land, sems):
    n = len(src)
    send_sems, recv_sems = sems
    x, y, c = _position()
    chips = _other_chips(x, y)

    def copy(a, k):
        cx, cy = chips[k]
        return pltpu.make_async_remote_copy(src_ref=src[a].at[2 * cx + cy], dst_ref=land[a].at[k], send_sem=send_sems.at[a, k],
                                            recv_sem=recv_sems.at[a, k], device_id=(cx, cy, c), device_id_type=MESH)

    def start():
        for a in range(n):
            for k in range(3):
                copy(a, k).start()

    def finish():
        for a in range(n):
            for k in range(3):
                copy(a, k).wait_recv()
        for a in range(n):
            for k in range(3):
                copy(a, k).wait_send()

    return start, finish


def _scatter_big_sems(n):
    return [pltpu.SemaphoreType.DMA((n, 3)), pltpu.SemaphoreType.DMA((n, 3))]


def _sum4_big(name, sums, land, chip):
    _, r2, k = sums.shape
    tr = _rows_tile_capped(r2)

    def body(q_ref, s_ref, l0, l1, l2, out_ref):
        out_ref[...] = ((s_ref[...].astype(F32) + l0[...].astype(F32)) + l1[...].astype(F32)) + l2[...].astype(F32)

    lands = [pl.BlockSpec((None, tr, k), functools.partial(lambda j, i, q: (j, i, 0), j)) for j in range(3)]
    return pl.pallas_call(
        body, name=name,
        grid_spec=pltpu.PrefetchScalarGridSpec(
            num_scalar_prefetch=1, grid=(r2 // tr,),
            in_specs=[pl.BlockSpec((None, tr, k), lambda i, q: (q[0], i, 0))] + lands,
            out_specs=pl.BlockSpec((tr, k), lambda i, q: (i, 0))),
        out_shape=_sds((r2, k)), compiler_params=_cparams(("parallel",)),
    )(chip, sums, land, land, land)


def _adamw_halves(name, mine, theirs, w, m, v, core):
    r, k = w.shape
    r2 = r // 2
    tr = _rows_tile_capped(r2, 512)
    per = r2 // tr

    def body(c_ref, mine_ref, theirs_ref, w_ref, m_ref, v_ref, g_out, d_out, m_out, v_out):
        g = jnp.where(pl.program_id(0) == c_ref[0], mine_ref[...], theirs_ref[...])
        delta, mn, vn = _adamw_math(w_ref[...], g, m_ref[...], v_ref[...])
        g_out[...] = g
        d_out[...] = delta
        m_out[...] = mn
        v_out[...] = vn

    half = pl.BlockSpec((tr, k), lambda h, i, c: (i, 0))
    full = pl.BlockSpec((tr, k), lambda h, i, c: (h * per + i, 0))
    return pl.pallas_call(
        body, name=name,
        grid_spec=pltpu.PrefetchScalarGridSpec(num_scalar_prefetch=1, grid=(2, per), in_specs=[half, half, full, full, full],
                                               out_specs=[full] * 4),
        out_shape=[_sds((r, k))] * 4, compiler_params=_cparams(("parallel", "parallel")),
    )(core, mine, theirs, w, m, v)


def _drops_layer_axis(name):
    return not (name.startswith('mlp') or name == 's5_d')


def _work(name, arr):
    return arr.reshape(arr.shape[1:]) if _drops_layer_axis(name) else arr


def _work_axis(name):
    return SHARD_AXIS[name] - (1 if _drops_layer_axis(name) else 0)


def _as2d(a):
    return a.reshape(-1, a.shape[-1])


def _replicated_2d(name, arr):
    if name in ('ln_g', 'ln_b'):
        return arr
    if name == 'rw_r_k':
        return arr.reshape(1, -1)
    if name == 's5_log_dt':
        return arr.reshape(-1, 1)
    if name.startswith('s5_'):
        return arr.reshape(arr.shape[1:])
    return arr


def _pack(arrs):
    flat = []
    for a in arrs:
        f = a.reshape(-1)
        flat.append(jnp.pad(f, (0, -f.shape[0] % 128)))
    f = jnp.concatenate(flat)
    f = jnp.pad(f, (0, -f.shape[0] % 1024))
    return f.reshape(-1, 128)


def _unpack(packed, shapes):
    flat = packed.reshape(-1)
    out, at = [], 0
    for s in shapes:
        size = math.prod(s)
        out.append(flat[at:at + size].reshape(s))
        at += size + (-size % 128)
    return out


def kernel(x, ln_g, ln_b, rw_mu, rw_w0, rw_w1, rw_w2, rw_a0, rw_a1, rw_a2, rw_g1, rw_g2, rw_k_k, rw_k_a, rw_r_k, rw_wr, rw_wk, rw_wv, rw_wo, rw_lnx_g, rw_lnx_b, s5_a_re, s5_a_im, s5_log_dt, s5_b_re, s5_b_im, s5_c_re, s5_c_im, s5_d, s5_w_glu, mlp_w1, mlp_w2, loss_target, m_ln_g, m_ln_b, m_rw_mu, m_rw_w0, m_rw_w1, m_rw_w2, m_rw_a0, m_rw_a1, m_rw_a2, m_rw_g1, m_rw_g2, m_rw_k_k, m_rw_k_a, m_rw_r_k, m_rw_wr, m_rw_wk, m_rw_wv, m_rw_wo, m_rw_lnx_g, m_rw_lnx_b, m_s5_a_re, m_s5_a_im, m_s5_log_dt, m_s5_b_re, m_s5_b_im, m_s5_c_re, m_s5_c_im, m_s5_d, m_s5_w_glu, m_mlp_w1, m_mlp_w2, v_ln_g, v_ln_b, v_rw_mu, v_rw_w0, v_rw_w1, v_rw_w2, v_rw_a0, v_rw_a1, v_rw_a2, v_rw_g1, v_rw_g2, v_rw_k_k, v_rw_k_a, v_rw_r_k, v_rw_wr, v_rw_wk, v_rw_wv, v_rw_wo, v_rw_lnx_g, v_rw_lnx_b, v_s5_a_re, v_s5_a_im, v_s5_log_dt, v_s5_b_re, v_s5_b_im, v_s5_c_re, v_s5_c_im, v_s5_d, v_s5_w_glu, v_mlp_w1, v_mlp_w2):
    d = dict(locals())
    x_pos, y_pos, c_pos = _position()
    chip = 2 * x_pos + y_pos
    chip_arr = jnp.reshape(chip, (1,)).astype(jnp.int32)
    core_arr = jnp.reshape(c_pos, (1,)).astype(jnp.int32)

    small = [n for n in SHARD_AXIS if n not in BIG]
    axes = [_work_axis(n) for n in small]
    big_views, small_fulls = _gather_early([_as2d(d[n]).astype(BF16) for n in BIG_EARLY], [_work(n, d[n]) for n in small], axes)
    views = dict(zip(BIG_EARLY, big_views))
    fw = dict(zip(small, small_fulls))
    c_model = d['x'].shape[-1]
    for n in BIG_EARLY:
        fw[n] = views[n].reshape(c_model, c_model)
    w1_layers, w2_layers = d['mlp_w1'].astype(BF16), d['mlp_w2'].astype(BF16)
    fw['late_a'] = [_as2d(d['rw_wo']).astype(BF16), w1_layers[0], w2_layers[0]]
    fw['late_b'] = [w1_layers[1]]
    fw['late_c'] = [_as2d(d['s5_w_glu']).astype(BF16), w2_layers[1]]
    for n in REPLICATED:
        fw[n] = _replicated_2d(n, d[n])

    loss_blk, grad_x, grads, reduced, late_sums = _local_step(d['x'][0], d['loss_target'][0], fw, core_arr)
    loss = lax.psum(loss_blk[0, 0], ('x', 'y', 'c'))
    out = {}

    pieces = [grads[n] for n in small]
    lands, late_lands = _scatter_pieces(pieces, axes, list(late_sums.values()))
    reduced.update(zip(late_sums, zip(late_sums.values(), late_lands)))
    mine = [_sum4_big("sum4_" + n, *reduced[n], chip_arr) for n in BIG]
    for n, g, ax, land in zip(small, pieces, axes, lands):
        size = g.shape[ax] // 4
        mine.append(_sum4("sum4_" + n, lax.dynamic_slice_in_dim(g, chip * size, size, ax), land))
    theirs = _sibling_swap("swap_sums", mine)
    for n, a, b in zip(BIG + small, mine, theirs):
        w2d, m2d, v2d = _as2d(d[n]), _as2d(d['m_' + n]), _as2d(d['v_' + n])
        res = (_adamw_halves("adamw_" + n, a, b, w2d, m2d, v2d, core_arr) if n in BIG
               else _adamw("adamw_" + n, (a, b), w2d, m2d, v2d))
        out[n] = [r.reshape(d[n].shape) for r in res]

    rep_shapes = [d[n].shape for n in REPLICATED]
    packs = [_pack([grads[n] for n in REPLICATED])] + [_pack([d[p + n] for n in REPLICATED]) for p in ('', 'm_', 'v_')]
    res = [_unpack(p, rep_shapes) for p in _allreduce_adamw_small(*packs)]
    for i, n in enumerate(REPLICATED):
        out[n] = [r[i] for r in res]

    grad_x = grad_x.reshape(d['x'].shape)
    return (loss, grad_x, *[out[n][0] for n in WEIGHTS], *[out[n][1] for n in WEIGHTS],
            *[out[n][2] for n in WEIGHTS], *[out[n][3] for n in WEIGHTS])
```

```python
import functools
import math

import jax
import jax.numpy as jnp
from jax import lax
from jax.experimental import pallas as pl
from jax.experimental.pallas import tpu as pltpu

F32 = jnp.float32
BF16 = jnp.bfloat16
MESH = pl.DeviceIdType.MESH

HEAD = 64
SSM_GROUP = 16
SSM_STATE = 64
GN_EPS = 64e-5
LN_EPS = 1e-5
DEPTH = 2
DN_ALPHA = (2.0 * DEPTH) ** 0.25
ADAM_LR, ADAM_B1, ADAM_B2, ADAM_EPS, ADAM_WD, ADAM_STEP = 0.001, 0.9, 0.999, 1e-08, 0.01, 10
REC_CHUNK = 64
V7X_VMEM_BYTES = 64 * 2 ** 20
VMEM_LIMIT = V7X_VMEM_BYTES - 8 * 2 ** 20

WEIGHTS = ['ln_g', 'ln_b', 'rw_mu', 'rw_w0', 'rw_w1', 'rw_w2', 'rw_a0', 'rw_a1', 'rw_a2', 'rw_g1', 'rw_g2',
           'rw_k_k', 'rw_k_a', 'rw_r_k', 'rw_wr', 'rw_wk', 'rw_wv', 'rw_wo', 'rw_lnx_g', 'rw_lnx_b',
           's5_a_re', 's5_a_im', 's5_log_dt', 's5_b_re', 's5_b_im', 's5_c_re', 's5_c_im', 's5_d', 's5_w_glu',
           'mlp_w1', 'mlp_w2']
SHARD_AXIS = {'rw_mu': 2, 'rw_w1': 1, 'rw_w2': 2, 'rw_a1': 1, 'rw_a2': 2, 'rw_g1': 1, 'rw_g2': 2,
              'rw_wr': 1, 'rw_wk': 1, 'rw_wv': 1, 'rw_wo': 1, 's5_d': 1, 's5_w_glu': 2, 'mlp_w1': 2, 'mlp_w2': 1}
REPLICATED = [n for n in WEIGHTS if n not in SHARD_AXIS]
BIG_EARLY = ['rw_wr', 'rw_wk', 'rw_wv']
BIG_LATE = ['rw_wo', 's5_w_glu', 'mlp_w1', 'mlp_w2']
BIG = BIG_EARLY + BIG_LATE
BIG_READY = ['s5_w_glu', 'mlp_w1', 'mlp_w2']


def _sds(shape, dtype=F32):
    return jax.ShapeDtypeStruct(tuple(shape), dtype)


def _cparams(sem=None, **kw):
    if sem is not None:
        kw["dimension_semantics"] = sem
    return pltpu.CompilerParams(vmem_limit_bytes=VMEM_LIMIT, **kw)


def _mm_products(a, b, g):
    gb = g.astype(BF16)
    da = lax.dot_general(gb, b.astype(BF16), (((1,), (1,)), ((), ())), preferred_element_type=F32)
    db = lax.dot_general(a.astype(BF16), gb, (((0,), (0,)), ((), ())), preferred_element_type=F32)
    return da, db


@jax.custom_vjp
def _mm_plain(a, b):
    return jnp.dot(a.astype(BF16), b.astype(BF16), preferred_element_type=F32)


def _mm_plain_bwd(res, g):
    da, db = _mm_products(*res, g)
    return da.astype(res[0].dtype), db.astype(res[1].dtype)


_mm_plain.defvjp(lambda a, b: (_mm_plain(a, b), (a, b)), _mm_plain_bwd)


@jax.custom_vjp
def _mm_proxy(a, b, z):
    return jnp.dot(a.astype(BF16), b.astype(BF16), preferred_element_type=F32)


def _mm_proxy_bwd(res, g):
    da, db = _mm_products(*res, g)
    return da.astype(res[0].dtype), jnp.zeros_like(res[1]), db


_mm_proxy.defvjp(lambda a, b, z: (_mm_proxy(a, b, z), (a, b)), _mm_proxy_bwd)


def mm(a, b, z=None):
    return _mm_plain(a, b) if z is None else _mm_proxy(a, b, z)


def _split3(x):
    hi = x.astype(BF16)
    r1 = x - hi.astype(F32)
    mid = r1.astype(BF16)
    lo = (r1 - mid.astype(F32)).astype(BF16)
    return hi, mid, lo


def _head_sum_impl(x):
    c = x.shape[1]
    lanes = 128
    sel = (lax.broadcasted_iota(jnp.int32, (c, lanes), 0) // HEAD
           == lax.broadcasted_iota(jnp.int32, (c, lanes), 1)).astype(BF16)
    s = sum(jnp.dot(p, sel, preferred_element_type=F32) for p in _split3(x))
    return sum(lax.dot_general(p, sel, (((1,), (1,)), ((), ())), preferred_element_type=F32) for p in _split3(s))


@jax.custom_vjp
def head_sum(x):
    return _head_sum_impl(x)


head_sum.defvjp(lambda x: (_head_sum_impl(x), None), lambda _, g: (_head_sum_impl(g),))


def _ln(x, g, b):
    mu = jnp.mean(x, axis=-1, keepdims=True)
    xc = x - mu
    var = jnp.mean(xc * xc, axis=-1, keepdims=True)
    return xc * lax.rsqrt(var + LN_EPS) * g + b


def _f_proj(acts, params, proxies):
    x, xp = acts
    mu, w = params
    return (mm(x + (xp - x) * mu, w, proxies[1]),)


def _f_lora(acts, params, proxies):
    x, xp, kraw = acts
    mu_w, mu_a, mu_g, w0, w1, w2, a0, a1, a2, g1, g2, k_k, k_a = params
    xx = xp - x
    w_pre = w0 + mm(jnp.tanh(mm(x + xx * mu_w, w1)), w2)
    z = -w_pre
    softplus = jnp.maximum(z, 0.0) + jnp.log(1.0 + jnp.exp(-jnp.abs(z)))
    log_decay = -jnp.exp(-softplus - 0.5)
    a = jax.nn.sigmoid(a0 + mm(mm(x + xx * mu_a, a1), a2))
    g = mm(jax.nn.sigmoid(mm(x + xx * mu_g, g1)), g2)
    kk = kraw * k_k
    kkn = kk / jnp.maximum(jnp.sqrt(head_sum(kk * kk)), 1e-12)
    k2 = kraw * (1.0 + (a - 1.0) * k_a)
    return log_decay, k2, -kkn, kkn * a, g


def _f_post(acts, params, proxies):
    o, r, k2, v, g, x = acts
    lnx_g, lnx_b, r_k, wo, ln_g, ln_b = params
    om = head_sum(o) * (1.0 / HEAD)
    oc = o - om
    ov = head_sum(oc * oc) * (1.0 / HEAD)
    on = oc * lax.rsqrt(ov + GN_EPS) * lnx_g + lnx_b
    bonus = head_sum(r * k2 * r_k) * v
    y = mm((on + bonus) * g, wo, proxies[3])
    return (_ln(DN_ALPHA * x + y, ln_g, ln_b),)


def _f_glu(acts, params, proxies):
    ys, h = acts
    d, wv0, wv1, wg0, wg1, ln_g, ln_b = params
    y = jax.nn.gelu(ys + h * d)
    mix = jnp.concatenate([mm(y, wv0, proxies[1]) * jax.nn.sigmoid(mm(y, wg0, proxies[3])),
                           mm(y, wv1, proxies[2]) * jax.nn.sigmoid(mm(y, wg1, proxies[4]))], axis=1)
    return (_ln(DN_ALPHA * h + mix, ln_g, ln_b),)


def _f_zoh(a_re, a_im, log_dt, b_re_t, b_im_t):
    dt = jnp.exp(log_dt)
    lam_re = jnp.minimum(a_re, -1e-4)
    lam_im = a_im
    mag = jnp.exp(dt * lam_re)
    abar_re = mag * jnp.cos(dt * lam_im)
    abar_im = mag * jnp.sin(dt * lam_im)
    den = lam_re * lam_re + lam_im * lam_im
    nr, ni = abar_re - 1.0, abar_im
    coef_re = ((nr * lam_re + ni * lam_im) / den)[:, None, :]
    coef_im = ((ni * lam_re - nr * lam_im) / den)[:, None, :]
    return (abar_re, abar_im, coef_re * b_re_t - coef_im * b_im_t, coef_re * b_im_t + coef_im * b_re_t)


def _bdot16_raw(a, b, ca, cb):
    return lax.dot_general(a.astype(BF16), b.astype(BF16), (((ca,), (cb,)), ((0,), (0,))), preferred_element_type=F32)


@functools.partial(jax.custom_vjp, nondiff_argnums=(2, 3))
def _bdot16(a, b, ca, cb):
    return _bdot16_raw(a, b, ca, cb)


def _bdot16_bwd(ca, cb, res, g):
    a, b = res
    if (ca, cb) == (2, 1):
        return _bdot16_raw(g, b, 2, 2), _bdot16_raw(a, g, 1, 1)
    if (ca, cb) == (2, 2):
        return _bdot16_raw(g, b, 2, 1), _bdot16_raw(g, a, 1, 1)
    assert (ca, cb) == (1, 1)
    return _bdot16_raw(b, g, 2, 2), _bdot16_raw(a, g, 2, 1)


_bdot16.defvjp(lambda a, b, ca, cb: (_bdot16_raw(a, b, ca, cb), (a, b)), _bdot16_bwd)

def _time_sums(x, suffix):
    hg, ln, _ = x.shape
    row = lax.broadcasted_iota(jnp.int32, (hg, ln, ln), 1)
    col = lax.broadcasted_iota(jnp.int32, (hg, ln, ln), 2)
    tri = ((row <= col) if suffix else (row >= col)).astype(BF16)
    return sum(lax.dot_general(tri, p, (((2,), (1,)), ((0,), (0,))), preferred_element_type=F32) for p in _split3(x))


@jax.custom_vjp
def _time_cumsum(x):
    return _time_sums(x, False)


_time_cumsum.defvjp(lambda x: (_time_sums(x, False), None), lambda _, g: (_time_sums(g, True),))

_dot_score = _bdot16
_dot_inverse = _bdot16
_dot_value = _bdot16


def _rec_chunk(s0, r, lw, k, v, a, b):
    hg, ln, _ = r.shape
    row = lax.broadcasted_iota(jnp.int32, (hg, ln, ln), 1)
    col = lax.broadcasted_iota(jnp.int32, (hg, ln, ln), 2)
    incl, strict = row >= col, row > col
    cum = _time_cumsum(lw)
    total = jnp.sum(lw, axis=1, keepdims=True)
    e_cum, e_inv, e_prev, e_tail = jnp.exp(cum), jnp.exp(-cum), jnp.exp(cum - lw), jnp.exp(total - cum)
    rt, at, bt, kt = r * e_cum, a * e_prev, b * e_inv, k * e_inv
    ar = jnp.concatenate([at, rt], axis=1)
    on_b, on_k = _dot_score(ar, bt, 2, 2), _dot_score(ar, kt, 2, 2)
    aab, arb = jnp.where(strict, on_b[:, :ln], 0.0), jnp.where(incl, on_b[:, ln:], 0.0)
    aak, ark = jnp.where(strict, on_k[:, :ln], 0.0), jnp.where(incl, on_k[:, ln:], 0.0)
    p = (row == col).astype(F32) + aab
    m = aab
    for _ in range(int(math.log2(ln)) - 1):
        m = _dot_inverse(m, m, 2, 1)
        p = p + _dot_inverse(p, m, 2, 1)
    from_state = _dot_value(ar, s0, 2, 2)
    from_v = _dot_value(jnp.concatenate([aak, ark], axis=1), v, 2, 1)
    u = _dot_inverse(p, from_state[:, :ln] + from_v[:, :ln], 2, 1)
    o = from_state[:, ln:] + from_v[:, ln:] + _dot_value(arb, u, 2, 1)
    s1 = s0 * jnp.exp(total) + _dot_value(jnp.concatenate([u, v], axis=1),
                                          jnp.concatenate([b * e_tail, k * e_tail], axis=1), 1, 1)
    return o, s1


def _full_spec(shape):
    nd = len(shape)
    return pl.BlockSpec(tuple(shape), lambda *_: (0,) * nd)


def _stage_fwd(name, f, acts, params, out_dims, tb):
    t = acts[0].shape[0]
    na, npar = len(acts), len(params)

    def body(*refs):
        outs = f(tuple(r[...] for r in refs[:na]), tuple(r[...] for r in refs[na:na + npar]), (None,) * npar)
        for r, val in zip(refs[na + npar:], outs):
            r[...] = val

    return pl.pallas_call(
        body, name=name, grid=(t // tb,),
        in_specs=[pl.BlockSpec((tb, a.shape[1]), lambda i: (i, 0)) for a in acts] + [_full_spec(p.shape) for p in params],
        out_specs=[pl.BlockSpec((tb, d), lambda i: (i, 0)) for d in out_dims],
        out_shape=[_sds((t, d)) for d in out_dims],
        compiler_params=_cparams(("arbitrary",)),
    )(*acts, *params)


def _stage_bwd(name, f, acts, params, couts, tb, proxied=(), halves_of=()):
    nh = len(halves_of)
    t = acts[0].shape[0]
    groups = [c if isinstance(c, tuple) else (c,) for c in couts]
    couts = [term for grp in groups for term in grp]
    na, npar, nc = len(acts), len(params), len(couts)
    steps = t // tb

    def f_diff(act_vals, diff_vals, param_vals):
        real = tuple(param_vals[i] if i in proxied else diff_vals[i] for i in range(npar))
        proxies = tuple(diff_vals[i] if i in proxied else None for i in range(npar))
        return f(act_vals, real, proxies)

    def body(*refs):
        a_refs, p_hbm, c_refs = refs[:na], refs[na:na + npar], refs[na + npar:na + npar + nc]
        o = na + npar + nc
        half_src, o = refs[o:o + nh], o + nh
        da_refs, dp_hbm, half_dst = refs[o:o + na], refs[o + na:o + na + npar], refs[o + na + npar:o + na + npar + nh]
        o = o + na + npar + nh
        p_buf, acc, half_sems = refs[o:o + npar], refs[o + npar:o + 2 * npar], refs[o + 2 * npar:]
        i = pl.program_id(0)
        if nh:
            half_start, half_finish = _sibling_halves_phases(half_src, half_dst, half_sems)
            pl.when(i == 0)(half_start)

        @pl.when(i == 0)
        def _():
            for src, dst in zip(p_hbm, p_buf):
                pltpu.sync_copy(src, dst)
            for r in acc:
                r[...] = jnp.zeros_like(r)

        param_vals = tuple(r[...] for r in p_buf)
        diff_vals = tuple(jnp.zeros(v.shape, F32) if i in proxied else v for i, v in enumerate(param_vals))
        _, vjp = jax.vjp(functools.partial(f_diff, param_vals=param_vals), tuple(r[...] for r in a_refs), diff_vals)
        terms = iter(c_refs)
        d_acts, d_params = vjp(tuple(functools.reduce(jnp.add, [next(terms)[...] for _ in grp]) for grp in groups))
        for r, val in zip(da_refs, d_acts):
            r[...] = val
        for r, val in zip(acc, d_params):
            r[...] += val

        @pl.when(i == steps - 1)
        def _():
            for src, dst in zip(acc, dp_hbm):
                pltpu.sync_copy(src, dst)

        if nh:
            pl.when(i == steps - 1)(half_finish)

    hbm = pl.BlockSpec(memory_space=pltpu.HBM)
    outs = pl.pallas_call(
        body, name=name, grid=(steps,),
        in_specs=[pl.BlockSpec((tb, a.shape[1]), lambda i: (i, 0)) for a in acts] + [hbm] * npar
        + [pl.BlockSpec((tb, c.shape[1]), lambda i: (i, 0)) for c in couts] + [hbm] * nh,
        out_specs=[pl.BlockSpec((tb, a.shape[1]), lambda i: (i, 0)) for a in acts] + [hbm] * (npar + nh),
        out_shape=[_sds(a.shape) for a in acts] + [_sds(p.shape) for p in params] + _sibling_halves_shapes(halves_of),
        scratch_shapes=[pltpu.VMEM(p.shape, p.dtype) for p in params] + [pltpu.VMEM(p.shape, F32) for p in params]
        + (_sibling_halves_sems(nh) if nh else []),
        compiler_params=_cparams(("arbitrary",)),
    )(*acts, *params, *couts, *halves_of)
    if nh:
        return outs[:na], outs[na:na + npar], outs[na + npar:]
    return outs[:na], outs[na:]


def _tiled_matmul(name, a, b, mode, grid, a_spec, b_spec, o_spec, out_shape):
    nk = grid[2]
    dims = {"nn": ((1,), (0,)), "nt": ((1,), (1,)), "tn": ((0,), (0,))}[mode]

    def body(a_ref, b_ref, o_ref, acc):
        kk = pl.program_id(2)

        @pl.when(kk == 0)
        def _():
            acc[...] = jnp.zeros_like(acc)

        acc[...] += lax.dot_general(a_ref[...].astype(BF16), b_ref[...].astype(BF16), (dims, ((), ())),
                                    preferred_element_type=F32)

        @pl.when(kk == nk - 1)
        def _():
            o_ref[...] = acc[...]

    return pl.pallas_call(
        body, name=name, grid=grid, in_specs=[a_spec, b_spec], out_specs=o_spec, out_shape=_sds(out_shape),
        scratch_shapes=[pltpu.VMEM(o_spec.block_shape, F32)],
        compiler_params=_cparams(("parallel", "parallel", "arbitrary")),
    )(a, b)


def _mlp_weight_grad(name, a, b, layer, layers, split, into=None, tile=512):
    t, m = a.shape
    n = b.shape[1]
    tk = min(tile, t)
    tile = 2 * tile
    if split == "n":
        tm, tn = min(tile, m), min(tile, n // 4)
        per = n // 4 // tn
        shape = (4, layers, m, n // 4)
        o_idx = lambda i, j, k: (j // per, layer, i, j % per)
    else:
        tm, tn = min(tile, m // 4), min(tile, n)
        per = m // 4 // tm
        shape = (4, layers, m // 4, n)
        o_idx = lambda i, j, k: (i // per, layer, i % per, j)
    nk = t // tk

    def body(a_ref, b_ref, *rest):
        o_ref, acc = rest[-2:]
        kk = pl.program_id(2)

        @pl.when(kk == 0)
        def _():
            acc[...] = jnp.zeros_like(acc)

        acc[...] += lax.dot_general(a_ref[...].astype(BF16), b_ref[...].astype(BF16), (((0,), (0,)), ((), ())),
                                    preferred_element_type=F32)

        @pl.when(kk == nk - 1)
        def _():
            o_ref[...] = acc[...]

    in_specs = [pl.BlockSpec((tk, tm), lambda i, j, k: (k, i)), pl.BlockSpec((tk, tn), lambda i, j, k: (k, j))]
    operands = [a, b]
    aliases = {}
    if into is not None:
        in_specs.append(pl.BlockSpec(memory_space=pl.ANY))
        operands.append(into)
        aliases = {2: 0}
    return pl.pallas_call(
        body, name=name, grid=(m // tm, n // tn, nk), in_specs=in_specs,
        out_specs=pl.BlockSpec((None, None, tm, tn), o_idx), out_shape=_sds(shape), input_output_aliases=aliases,
        scratch_shapes=[pltpu.VMEM((tm, tn), F32)],
        compiler_params=_cparams(("parallel", "parallel", "arbitrary")),
    )(*operands)


S5_PACK = 8


def _s5_weight_grad(name, x, s, wide_rows, tk):
    t, c = x.shape
    wide = s.shape[1]
    kb, nb = S5_PACK * SSM_GROUP, S5_PACK * SSM_STATE
    nsb = c // kb
    x_spec = pl.BlockSpec((tk, kb), lambda i, j, k: (k, j % nsb))
    s_spec = pl.BlockSpec((tk, nb), lambda i, j, k: (k, j))
    if wide_rows:
        return _tiled_matmul(name, s, x, "tn", (1, wide // nb, t // tk), s_spec, x_spec,
                             pl.BlockSpec((nb, kb), lambda i, j, k: (j, 0)), (wide, kb))
    return _tiled_matmul(name, x, s, "tn", (1, wide // nb, t // tk), x_spec, s_spec,
                         pl.BlockSpec((kb, nb), lambda i, j, k: (0, j)), (kb, wide))


def _mlp_fwd(name, h, w1, w2, layer, ln_g, ln_b, tb, shards=()):
    t, c = h.shape
    nj, fc = w1.shape[0], w1.shape[3]
    nsh = len(shards)
    steps = (t // tb) * nj

    def body(h_ref, w1_ref, w2_ref, g_ref, b_ref, *rest):
        src, (out_ref, s_ref), dst = rest[:nsh], rest[nsh:nsh + 2], rest[nsh + 2:2 * nsh + 2]
        acc, sems = rest[2 * nsh + 2], rest[2 * nsh + 3:]
        j = pl.program_id(1)
        step = pl.program_id(0) * nj + j
        if nsh:
            start, forward, finish = _gather_big_phases(src, dst, sems)
            pl.when(step == 0)(start)

        @pl.when(j == 0)
        def _():
            acc[...] = jnp.zeros_like(acc)

        hid = jnp.dot(h_ref[...].astype(BF16), w1_ref[...].astype(BF16), preferred_element_type=F32)
        act = jnp.square(jnp.maximum(hid, 0.0))
        acc[...] += jnp.dot(act.astype(BF16), w2_ref[...].astype(BF16), preferred_element_type=F32)

        @pl.when(j == nj - 1)
        def _():
            s = DN_ALPHA * h_ref[...] + acc[...]
            s_ref[...] = s
            out_ref[...] = _ln(s, g_ref[...], b_ref[...])

        if nsh:
            pl.when(step == steps // 2)(forward)
            pl.when(step == steps - 1)(finish)

    row = pl.BlockSpec((tb, c), lambda i, j: (i, 0))
    vec = pl.BlockSpec((1, c), lambda i, j: (0, 0))
    outs = pl.pallas_call(
        body, name=name, grid=(t // tb, nj),
        in_specs=[row, pl.BlockSpec((None, None, c, fc), lambda i, j: (j, layer, 0, 0)),
                  pl.BlockSpec((None, None, fc, c), lambda i, j: (j, layer, 0, 0)), vec, vec] + [_HBM] * nsh,
        out_specs=[row, row] + [_HBM] * nsh,
        out_shape=[_sds((t, c)), _sds((t, c))] + [_sds((4,) + a.shape, a.dtype) for a in shards],
        scratch_shapes=[pltpu.VMEM((tb, c), F32)] + (_gather_big_sems(nsh) if nsh else []),
        compiler_params=_cparams(("arbitrary", "arbitrary")),
    )(h, w1, w2, ln_g, ln_b, *shards)
    return outs[0], outs[1], outs[2:]


def _mlp_bwd(name, h, s, dout, w1, w2, layer, ln_g, ln_b, tb):
    t, c = h.shape
    nj, fc = w1.shape[0], w1.shape[3]
    ff = nj * fc
    ni = t // tb
    nt = (((1,), (1,)), ((), ()))
    douts = dout if isinstance(dout, tuple) else (dout,)
    nd = len(douts)

    def body(h_ref, s_ref, *rest):
        dout_refs = rest[:nd]
        (w1_ref, w2_ref, g_ref, b_ref, dh_ref, ds_ref, dhid_ref, act_ref, dg_ref, db_ref,
         ds_scr, dh_acc, dg_acc, db_acc) = rest[nd:]
        i, j = pl.program_id(0), pl.program_id(1)

        @pl.when((i == 0) & (j == 0))
        def _():
            dg_acc[...] = jnp.zeros_like(dg_acc)
            db_acc[...] = jnp.zeros_like(db_acc)

        @pl.when(j == 0)
        def _():
            _, vjp = jax.vjp(_ln, s_ref[...], g_ref[...], b_ref[...])
            ds, dg, db = vjp(functools.reduce(jnp.add, [r[...] for r in dout_refs]))
            ds_scr[...] = ds
            ds_ref[...] = ds.astype(BF16)
            dh_acc[...] = DN_ALPHA * ds
            dg_acc[...] += dg
            db_acc[...] += db

        w1b, w2b = w1_ref[...].astype(BF16), w2_ref[...].astype(BF16)
        hid = jnp.dot(h_ref[...].astype(BF16), w1b, preferred_element_type=F32)
        rl = jnp.maximum(hid, 0.0)
        dact = lax.dot_general(ds_scr[...].astype(BF16), w2b, nt, preferred_element_type=F32)
        dhid = (dact * 2.0 * rl).astype(BF16)
        dh_acc[...] += lax.dot_general(dhid, w1b, nt, preferred_element_type=F32)
        dhid_ref[...] = dhid
        act_ref[...] = (rl * rl).astype(BF16)

        @pl.when(j == nj - 1)
        def _():
            dh_ref[...] = dh_acc[...]

        @pl.when((i == ni - 1) & (j == nj - 1))
        def _():
            dg_ref[...] = dg_acc[...]
            db_ref[...] = db_acc[...]

    row = pl.BlockSpec((tb, c), lambda i, j: (i, 0))
    vec = pl.BlockSpec((1, c), lambda i, j: (0, 0))
    wide = pl.BlockSpec((tb, fc), lambda i, j: (i, j))
    return pl.pallas_call(
        body, name=name, grid=(ni, nj),
        in_specs=[row, row] + [row] * nd + [pl.BlockSpec((None, None, c, fc), lambda i, j: (j, layer, 0, 0)),
                                            pl.BlockSpec((None, None, fc, c), lambda i, j: (j, layer, 0, 0)), vec, vec],
        out_specs=[row, row, wide, wide, vec, vec],
        out_shape=[_sds((t, c)), _sds((t, c), BF16), _sds((t, ff), BF16), _sds((t, ff), BF16), _sds((1, c)), _sds((1, c))],
        scratch_shapes=[pltpu.VMEM((tb, c), F32), pltpu.VMEM((tb, c), F32), pltpu.VMEM((1, c), F32), pltpu.VMEM((1, c), F32)],
        compiler_params=_cparams(("arbitrary", "arbitrary")),
    )(h, s, *douts, w1, w2, ln_g, ln_b)


def _load_heads(ref, hg):
    return jnp.stack([ref[:, h * HEAD:(h + 1) * HEAD] for h in range(hg)])


def _store_heads(ref, val):
    for h in range(val.shape[0]):
        ref[:, h * HEAD:(h + 1) * HEAD] = val[h]


def _rec_fwd(r, lw, k, v, a, b, hg, shards):
    t, c = r.shape
    n = HEAD
    nh = c // n
    ln = REC_CHUNK
    nck = t // ln
    ngrp = nh // hg
    nsh = len(shards)
    steps = ngrp * nck

    def body(r_ref, lw_ref, k_ref, v_ref, a_ref, b_ref, *rest):
        src, (o_ref, s0_ref), dst = rest[:nsh], rest[nsh:nsh + 2], rest[nsh + 2:2 * nsh + 2]
        state, sems = rest[2 * nsh + 2], rest[2 * nsh + 3:]
        step = pl.program_id(0) * nck + pl.program_id(1)
        start, forward, finish = _gather_big_phases(src, dst, sems)
        pl.when(step == 0)(start)

        @pl.when(pl.program_id(1) == 0)
        def _():
            state[...] = jnp.zeros_like(state)

        s0 = state[...]
        s0_ref[...] = s0
        o, s1 = _rec_chunk(s0, *(_load_heads(x, hg) for x in (r_ref, lw_ref, k_ref, v_ref, a_ref, b_ref)))
        _store_heads(o_ref, o)
        state[...] = s1
        pl.when(step == steps // 2)(forward)
        pl.when(step == steps - 1)(finish)

    seq = pl.BlockSpec((ln, hg * n), lambda g, i: (i, g))
    outs = pl.pallas_call(
        body, name="rec_fwd", grid=(ngrp, nck), in_specs=[seq] * 6 + [_HBM] * nsh,
        out_specs=[seq, pl.BlockSpec((None, hg, n, n), lambda g, i: (i, g, 0, 0))] + [_HBM] * nsh,
        out_shape=[_sds((t, c)), _sds((nck, nh, n, n))] + [_sds((4,) + s.shape, s.dtype) for s in shards],
        scratch_shapes=[pltpu.VMEM((hg, n, n), F32)] + _gather_big_sems(nsh),
        compiler_params=_cparams(("arbitrary", "arbitrary")),
    )(r, lw, k, v, a, b, *shards)
    return outs[0], outs[1], outs[2:]


def _rec_bwd(r, lw, k, v, a, b, s0s, do, hg, chip_sums):
    t, c = r.shape
    n = HEAD
    nh = c // n
    ln = REC_CHUNK
    nck = t // ln
    ngrp = nh // hg
    nsum = len(chip_sums)
    steps = ngrp * nck

    def body(r_ref, lw_ref, k_ref, v_ref, a_ref, b_ref, s0_ref, do_ref, *rest):
        src, grad_refs, land = rest[:nsum], rest[nsum:nsum + 6], rest[nsum + 6:2 * nsum + 6]
        dstate, sems = rest[2 * nsum + 6], rest[2 * nsum + 7:]
        step = pl.program_id(0) * nck + pl.program_id(1)
        start, finish = _scatter_big_phases(src, land, sems)
        pl.when(step == 0)(start)

        @pl.when(pl.program_id(1) == 0)
        def _():
            dstate[...] = jnp.zeros_like(dstate)

        _, vjp = jax.vjp(_rec_chunk, s0_ref[...], *(_load_heads(x, hg) for x in (r_ref, lw_ref, k_ref, v_ref, a_ref, b_ref)))
        ds0, *grads = vjp((_load_heads(do_ref, hg), dstate[...]))
        dstate[...] = ds0
        for ref, val in zip(grad_refs, grads):
            _store_heads(ref, val)
        pl.when(step == steps - 1)(finish)

    seq = pl.BlockSpec((ln, hg * n), lambda g, i: (nck - 1 - i, g))
    outs = pl.pallas_call(
        body, name="rec_bwd", grid=(ngrp, nck),
        in_specs=[seq] * 6 + [pl.BlockSpec((None, hg, n, n), lambda g, i: (nck - 1 - i, g, 0, 0)), seq] + [_HBM] * nsum,
        out_specs=[seq] * 6 + [_HBM] * nsum,
        out_shape=[_sds((t, c))] * 6 + [_sds((3,) + s.shape[1:], s.dtype) for s in chip_sums],
        scratch_shapes=[pltpu.VMEM((hg, n, n), F32)] + _scatter_big_sems(nsum),
        compiler_params=_cparams(("arbitrary", "arbitrary")),
    )(r, lw, k, v, a, b, s0s, do, *chip_sums)
    return outs[:6], outs[6:]


def _s5_blocks(c):
    kb, nb = S5_PACK * SSM_GROUP, S5_PACK * SSM_STATE
    return kb, nb, c // kb


def _s5_fwd(h, bc, abar, cc, tb, shards=()):
    t, c = h.shape
    w2 = bc.shape[1]
    w = w2 // 2
    kb, nb, nsb = _s5_blocks(c)

    nsh = len(shards)
    steps = t // tb

    def body(h_ref, bc_ref, a_ref, cc_ref, *rest):
        src, (s_ref, y_ref), dst = rest[:nsh], rest[nsh:nsh + 2], rest[nsh + 2:2 * nsh + 2]
        carry, rows, sems = rest[2 * nsh + 2], rest[2 * nsh + 3], rest[2 * nsh + 4:]
        if nsh:
            start, forward, finish = _gather_big_phases(src, dst, sems)
            pl.when(pl.program_id(0) == 0)(start)

        @pl.when(pl.program_id(0) == 0)
        def _():
            carry[...] = jnp.zeros_like(carry)

        for j in range(w2 // nb):
            ch = (j % nsb) * kb
            rows[:, j * nb:(j + 1) * nb] = jnp.dot(h_ref[:, ch:ch + kb].astype(BF16), bc_ref[:, j * nb:(j + 1) * nb],
                                                   preferred_element_type=F32)
        ar, ai = a_ref[:, :w], a_ref[:, w:]

        def step(i, state):
            hr, hi = state
            nr = ar * hr - ai * hi + rows[pl.ds(i, 1), :w]
            ni = ar * hi + ai * hr + rows[pl.ds(i, 1), w:]
            rows[pl.ds(i, 1), :w] = nr
            rows[pl.ds(i, 1), w:] = ni
            return nr, ni

        hr, hi = lax.fori_loop(0, tb, step, (carry[:, :w], carry[:, w:]))
        carry[:, :w] = hr
        carry[:, w:] = hi
        s_ref[...] = rows[...].astype(BF16)
        for j in range(nsb):
            re, im = j * nb, w + j * nb
            y_ref[:, j * kb:(j + 1) * kb] = (
                jnp.dot(s_ref[:, re:re + nb], cc_ref[re:re + nb, :], preferred_element_type=F32)
                + jnp.dot(s_ref[:, im:im + nb], cc_ref[im:im + nb, :], preferred_element_type=F32))
        if nsh:
            pl.when(pl.program_id(0) == steps // 2)(forward)
            pl.when(pl.program_id(0) == steps - 1)(finish)

    outs = pl.pallas_call(
        body, name="s5_fwd", grid=(steps,),
        in_specs=[pl.BlockSpec((tb, c), lambda i: (i, 0)), _full_spec(bc.shape), _full_spec(abar.shape), _full_spec(cc.shape)]
        + [_HBM] * nsh,
        out_specs=[pl.BlockSpec((tb, w2), lambda i: (i, 0)), pl.BlockSpec((tb, c), lambda i: (i, 0))] + [_HBM] * nsh,
        out_shape=[_sds((t, w2), BF16), _sds((t, c))] + [_sds((4,) + a.shape, a.dtype) for a in shards],
        scratch_shapes=[pltpu.VMEM((1, w2), F32), pltpu.VMEM((tb, w2), F32)] + (_gather_big_sems(nsh) if nsh else []),
        compiler_params=_cparams(("arbitrary",)),
    )(h, bc, abar, cc, *shards)
    return outs[0], outs[1], outs[2:]


def _s5_bwd(dy, s, abar, cc, bc, tb):
    t, c = dy.shape
    w2 = s.shape[1]
    w = w2 // 2
    kb, nb, nsb = _s5_blocks(c)
    nblk = t // tb
    pack = 16
    per = tb // pack
    nt = (((1,), (1,)), ((), ()))

    def body(dy_ref, s_ref, sprev_ref, a_ref, cc_ref, bc_ref, dbu_ref, dh_ref, da_ref, carry, da_acc, rows):
        i = pl.program_id(0)

        @pl.when(i == 0)
        def _():
            carry[...] = jnp.zeros_like(carry)
            da_acc[...] = jnp.zeros_like(da_acc)

        for j in range(w2 // nb):
            ch = (j % nsb) * kb
            rows[:, j * nb:(j + 1) * nb] = lax.dot_general(dy_ref[:, ch:ch + kb].astype(BF16), cc_ref[j * nb:(j + 1) * nb, :], nt,
                                                           preferred_element_type=F32)
        ar, ai = a_ref[:, :w], a_ref[:, w:]

        def step(n, state):
            gr, gi = state
            row = tb - 1 - n
            nr = rows[pl.ds(row, 1), :w] + ar * gr + ai * gi
            ni = rows[pl.ds(row, 1), w:] + ar * gi - ai * gr
            rows[pl.ds(row, 1), :w] = nr
            rows[pl.ds(row, 1), w:] = ni
            return nr, ni

        gr, gi = lax.fori_loop(0, tb, step, (carry[:, :w], carry[:, w:]))
        carry[:, :w] = gr
        carry[:, w:] = gi
        last = (lax.broadcasted_iota(jnp.int32, (pack, w2), 0) == pack - 1) & (i < nblk - 1)
        before = jnp.sum(jnp.where(last, sprev_ref[...].astype(F32), 0.0), axis=0, keepdims=True)
        rid = lax.broadcasted_iota(jnp.int32, (tb, w2), 0)
        sp = jnp.where(rid == 0, before, pltpu.roll(s_ref[...].astype(F32), 1, 0))
        g = rows[...]
        dbu_ref[...] = g.astype(BF16)
        spr, spi, g_r, g_i = sp[:, :w], sp[:, w:], g[:, :w], g[:, w:]
        da_acc[:, :w] += jnp.sum(spr * g_r + spi * g_i, axis=0, keepdims=True)
        da_acc[:, w:] += jnp.sum(spr * g_i - spi * g_r, axis=0, keepdims=True)
        for j in range(nsb):
            re, im = j * nb, w + j * nb
            dh_ref[:, j * kb:(j + 1) * kb] = (
                lax.dot_general(dbu_ref[:, re:re + nb], bc_ref[:, re:re + nb], nt, preferred_element_type=F32)
                + lax.dot_general(dbu_ref[:, im:im + nb], bc_ref[:, im:im + nb], nt, preferred_element_type=F32))

        @pl.when(i == nblk - 1)
        def _():
            da_ref[...] = da_acc[...]

    wide = pl.BlockSpec((tb, w2), lambda i: (nblk - 1 - i, 0))
    narrow = pl.BlockSpec((tb, c), lambda i: (nblk - 1 - i, 0))
    prev = pl.BlockSpec((pack, w2), lambda i: (jnp.maximum((nblk - 1 - i) * per - 1, 0), 0))
    return pl.pallas_call(
        body, name="s5_bwd", grid=(nblk,),
        in_specs=[narrow, wide, prev, _full_spec(abar.shape), _full_spec(cc.shape), _full_spec(bc.shape)],
        out_specs=[wide, narrow, pl.BlockSpec((1, w2), lambda i: (0, 0))],
        out_shape=[_sds((t, w2), BF16), _sds((t, c)), _sds((1, w2))],
        scratch_shapes=[pltpu.VMEM((1, w2), F32), pltpu.VMEM((1, w2), F32), pltpu.VMEM((tb, w2), F32)],
        compiler_params=_cparams(("arbitrary",)),
    )(dy, s, s, abar, cc, bc)


def _zoh_fwd(a_re, a_im, log_dt, b_re_t, b_im_t):
    def body(*refs):
        for r, val in zip(refs[5:], _f_zoh(*(x[...] for x in refs[:5]))):
            r[...] = val

    return pl.pallas_call(body, name="s5_zoh_fwd", out_shape=[_sds(a_re.shape)] * 2 + [_sds(b_re_t.shape)] * 2,
                          compiler_params=_cparams())(a_re, a_im, log_dt, b_re_t, b_im_t)


def _zoh_bwd(a_re, a_im, log_dt, b_re_t, b_im_t, couts):
    def body(*refs):
        _, vjp = jax.vjp(_f_zoh, *(x[...] for x in refs[:5]))
        for r, val in zip(refs[9:], vjp(tuple(x[...] for x in refs[5:9]))):
            r[...] = val

    ins = (a_re, a_im, log_dt, b_re_t, b_im_t)
    return pl.pallas_call(body, name="s5_zoh_bwd", out_shape=[_sds(x.shape) for x in ins],
                          compiler_params=_cparams())(*ins, *couts)


def _loss_head(h, target, tb):
    t, c = h.shape
    nb = t // tb

    def body(h_ref, t_ref, loss_ref, dh_ref, acc):
        i = pl.program_id(0)

        @pl.when(i == 0)
        def _():
            acc[...] = jnp.zeros_like(acc)

        d = h_ref[...] - t_ref[...]
        dh_ref[...] = d * (1.0 / c)
        acc[...] += 0.5 * jnp.sum(jnp.mean(d * d, axis=-1, keepdims=True), axis=0, keepdims=True)

        @pl.when(i == nb - 1)
        def _():
            loss_ref[...] = jnp.broadcast_to(acc[...], loss_ref.shape)

    row = pl.BlockSpec((tb, c), lambda i: (i, 0))
    return pl.pallas_call(
        body, name="loss_head", grid=(nb,), in_specs=[row, row],
        out_specs=[pl.BlockSpec((8, 128), lambda i: (0, 0)), row], out_shape=[_sds((8, 128)), _sds((t, c))],
        scratch_shapes=[pltpu.VMEM((1, 1), F32)], compiler_params=_cparams(("arbitrary",)),
    )(h, target)


def _rows_tile(rows):
    for cand in (512, 256, 128, 64, 32, 16, 8):
        if rows % cand == 0:
            return cand
    return rows


def _addn(name, arrs):
    rows, cols = arrs[0].shape
    tb = _rows_tile(rows)

    def body(*refs):
        acc = refs[0][...]
        for r in refs[1:-1]:
            acc = acc + r[...]
        refs[-1][...] = acc

    blk = pl.BlockSpec((tb, cols), lambda i: (i, 0))
    return pl.pallas_call(body, name=name, grid=(rows // tb,), in_specs=[blk] * len(arrs), out_specs=blk,
                          out_shape=_sds((rows, cols)), compiler_params=_cparams(("parallel",)))(*arrs)


def _adamw_math(w, g, m, v):
    m = ADAM_B1 * m + (1.0 - ADAM_B1) * g
    v = ADAM_B2 * v + (1.0 - ADAM_B2) * jnp.square(g)
    m_hat = m / (1.0 - ADAM_B1 ** ADAM_STEP)
    v_hat = v / (1.0 - ADAM_B2 ** ADAM_STEP)
    delta = -ADAM_LR * (m_hat / (jnp.sqrt(v_hat) + ADAM_EPS) + ADAM_WD * w)
    return delta, m, v


def _adamw(name, parts, w, m, v):
    rows, cols = w.shape
    tb = _rows_tile(rows)
    npart = len(parts)

    def body(*refs):
        g = refs[0][...]
        for r in refs[1:npart]:
            g = g + r[...]
        w_ref, m_ref, v_ref = refs[npart:npart + 3]
        g_out, d_out, m_out, v_out = refs[npart + 3:]
        delta, mn, vn = _adamw_math(w_ref[...], g, m_ref[...], v_ref[...])
        g_out[...] = g
        d_out[...] = delta
        m_out[...] = mn
        v_out[...] = vn

    blk = pl.BlockSpec((tb, cols), lambda i: (i, 0))
    return pl.pallas_call(body, name=name, grid=(rows // tb,), in_specs=[blk] * (npart + 3), out_specs=[blk] * 4,
                          out_shape=[_sds((rows, cols))] * 4, compiler_params=_cparams(("parallel",)))(*parts, w, m, v)


def _shift_down(a):
    return jnp.concatenate([jnp.zeros_like(a[:1]), a[:-1]], axis=0)


def _shift_up(a):
    return jnp.concatenate([a[1:], jnp.zeros_like(a[:1])], axis=0)


def _s5_pack_mask(g):
    return (jnp.arange(g)[None, :] % S5_PACK == jnp.arange(S5_PACK)[:, None]).astype(F32)


def _compact_b(bbar_t):
    g, s, p = bbar_t.shape
    return (_s5_pack_mask(g)[:, None, :, None] * bbar_t.transpose(1, 0, 2)[None]).reshape(S5_PACK * s, g * p)


def _compact_b_t(dense, g):
    s, p = dense.shape[0] // S5_PACK, dense.shape[1] // g
    return jnp.sum(dense.reshape(S5_PACK, s, g, p) * _s5_pack_mask(g)[:, None, :, None], axis=0).transpose(1, 0, 2)


def _compact_c(c_w):
    g, s, p = c_w.shape
    return (c_w.transpose(0, 2, 1)[:, :, None, :] * _s5_pack_mask(g).T[:, None, :, None]).reshape(g * p, S5_PACK * s)


def _compact_c_t(dense, g):
    p, s = dense.shape[0] // g, dense.shape[1] // S5_PACK
    return jnp.sum(dense.reshape(g, p, S5_PACK, s) * _s5_pack_mask(g).T[:, None, :, None], axis=2).transpose(0, 2, 1)


def _local_step(x, target, fw, core):
    t, c = x.shape
    nh = c // HEAD
    ng = c // SSM_GROUP
    tb = min(256, t)
    tbb = min(128, t)
    tbm = min(512, t)
    tbmb = min(512, t)
    tbs = min(256, t)
    tk5 = min(2048, t)
    hg = min(16, nh)
    mu = [fw['rw_mu'][i:i + 1] for i in range(6)]
    ln_g = [fw['ln_g'][i:i + 1] for i in range(4)]
    ln_b = [fw['ln_b'][i:i + 1] for i in range(4)]
    grads = {}

    xp = _shift_down(x)
    proj_params = {n: (mu[i], fw['rw_w' + n]) for n, i in (('r', 0), ('k', 2), ('v', 3))}
    raw = {n: _stage_fwd("proj_" + n, _f_proj, (x, xp), proj_params[n], (c,), tb)[0] for n in 'rkv'}
    lora_params = (mu[1], mu[4], mu[5], fw['rw_w0'], fw['rw_w1'], fw['rw_w2'], fw['rw_a0'], fw['rw_a1'], fw['rw_a2'],
                   fw['rw_g1'], fw['rw_g2'], fw['rw_k_k'], fw['rw_k_a'])
    lw, k2, an, bb, gate = _stage_fwd("lora", _f_lora, (x, xp, raw['k']), lora_params, (c,) * 5, tb)
    rec_in = (raw['r'], lw, k2, raw['v'], an, bb)
    o, s0s, (wo_view, w1_l0, w2_l0) = _rec_fwd(*rec_in, hg, fw['late_a'])
    fw = dict(fw, rw_wo=wo_view.reshape(c, c))
    mlp_w = [(w1_l0.reshape(4, 1, c, -1), w2_l0.reshape(4, 1, -1, c)), None]
    post_params = (fw['rw_lnx_g'], fw['rw_lnx_b'], fw['rw_r_k'], fw['rw_wo'], ln_g[0], ln_b[0])
    post_acts = (o, raw['r'], k2, raw['v'], gate, x)
    h1, = _stage_fwd("post", _f_post, post_acts, post_params, (c,), tb)
    h2, s_mlp0, (w1_l1,) = _mlp_fwd("mlp0_fwd", h1, *mlp_w[0], 0, ln_g[1], ln_b[1], tbm, fw['late_b'])

    a_re, a_im, log_dt = fw['s5_a_re'], fw['s5_a_im'], fw['s5_log_dt']
    b_re_t, b_im_t = fw['s5_b_re'].transpose(0, 2, 1), fw['s5_b_im'].transpose(0, 2, 1)
    abar_re, abar_im, bbar_re_t, bbar_im_t = _zoh_fwd(a_re, a_im, log_dt, b_re_t, b_im_t)
    abar = jnp.concatenate([abar_re.reshape(1, -1), abar_im.reshape(1, -1)], axis=1)
    bc = jnp.concatenate([_compact_b(bbar_re_t), _compact_b(bbar_im_t)], axis=1).astype(BF16)
    cc = jnp.concatenate([_compact_c(fw['s5_c_re']), -_compact_c(fw['s5_c_im'])], axis=0).astype(BF16)
    st, ys, (glu_view, w2_l1) = _s5_fwd(h2, bc, abar, cc, tbs, fw['late_c'])
    mlp_w[1] = (w1_l1.reshape(4, 1, c, -1), w2_l1.reshape(4, 1, -1, c))
    fw = dict(fw, s5_w_glu=tuple(glu_view[q] for q in range(4)))
    glu_params = (fw['s5_d'], *fw['s5_w_glu'], ln_g[2], ln_b[2])
    h3, = _stage_fwd("glu", _f_glu, (ys, h2), glu_params, (c,), tb)
    h4, s_mlp1, _ = _mlp_fwd("mlp1_fwd", h3, *mlp_w[1], 0, ln_g[3], ln_b[3], tbm)

    loss_blk, dh4 = _loss_head(h4, target, tb)

    dln_g, dln_b = [None] * 4, [None] * 4
    dh3, ds1, dhid1, act1, dln_g[3], dln_b[3] = _mlp_bwd("mlp1_bwd", h3, s_mlp1, dh4, *mlp_w[1], 0,
                                                         ln_g[3], ln_b[3], tbmb)
    dw1 = _mlp_weight_grad("mlp1_dw1", h3, dhid1, 1, DEPTH, "n")
    dw2 = _mlp_weight_grad("mlp1_dw2", act1, ds1, 1, DEPTH, "m")
    (dys, dh2_glu), (grads['s5_d'], *dglu, dln_g[2], dln_b[2]) = _stage_bwd(
        "glu_bwd", _f_glu, (ys, h2), glu_params, (dh3,), tbb, proxied=(1, 2, 3, 4))
    grads['s5_w_glu'] = jnp.stack(dglu)
    dcc = _s5_weight_grad("s5_dcc", dys, st, True, tk5)
    dbu, dh2_bu, dabar = _s5_bwd(dys, st, abar, cc, bc, tbs)
    dbc = _s5_weight_grad("s5_dbc", h2, dbu, False, tk5)
    gp = ng * SSM_STATE
    grads['s5_c_re'] = _compact_c_t(dcc[:gp], ng)
    grads['s5_c_im'] = -_compact_c_t(dcc[gp:], ng)
    zoh_couts = (dabar[:, :gp].reshape(ng, SSM_STATE), dabar[:, gp:].reshape(ng, SSM_STATE),
                 _compact_b_t(dbc[:, :gp], ng), _compact_b_t(dbc[:, gp:], ng))
    grads['s5_a_re'], grads['s5_a_im'], grads['s5_log_dt'], db_re_t, db_im_t = _zoh_bwd(
        a_re, a_im, log_dt, b_re_t, b_im_t, zoh_couts)
    grads['s5_b_re'], grads['s5_b_im'] = db_re_t.transpose(0, 2, 1), db_im_t.transpose(0, 2, 1)
    dh2 = (dh2_glu, dh2_bu)

    dh1, ds0, dhid0, act0, dln_g[1], dln_b[1] = _mlp_bwd("mlp0_bwd", h1, s_mlp0, dh2, *mlp_w[0], 0,
                                                         ln_g[1], ln_b[1], tbmb)
    grads['mlp_w1'] = _mlp_weight_grad("mlp0_dw1", h1, dhid0, 0, DEPTH, "n", into=dw1)
    grads['mlp_w2'] = _mlp_weight_grad("mlp0_dw2", act0, ds0, 0, DEPTH, "m", into=dw2)
    ready_views = [grads[n].reshape(4, -1, grads[n].shape[-1]) for n in BIG_READY]
    (do, dr_p, dk2_p, dv_p, dgate, dx_post), post_g, ready_others = _stage_bwd(
        "post_bwd", _f_post, post_acts, post_params, (dh1,), tbb, proxied=(3,), halves_of=ready_views)
    grads['rw_lnx_g'], grads['rw_lnx_b'], grads['rw_r_k'], grads['rw_wo'], dln_g[0], dln_b[0] = post_g
    ready_sums = [_half_add(f"half_add_a{i}", v, o, core) for i, (v, o) in enumerate(zip(ready_views, ready_others))]
    rec_g, ready_lands = _rec_bwd(*rec_in, s0s, do, hg, ready_sums)
    reduced = dict(zip(BIG_READY, zip(ready_sums, ready_lands)))
    dr_r, dlw, dk2_r, dv_r, dan, dbb = rec_g
    dk2 = (dk2_p, dk2_r)
    (dx_l, dxp_l, dkraw_l), lora_g = _stage_bwd("lora_bwd", _f_lora, (x, xp, raw['k']), lora_params,
                                                (dlw, dk2, dan, dbb, dgate), tbb)
    (dmu_w, dmu_a, dmu_g, grads['rw_w0'], grads['rw_w1'], grads['rw_w2'], grads['rw_a0'], grads['rw_a1'], grads['rw_a2'],
     grads['rw_g1'], grads['rw_g2'], grads['rw_k_k'], grads['rw_k_a']) = lora_g
    dproj = {'r': (dr_p, dr_r), 'k': dkraw_l, 'v': (dv_p, dv_r)}
    dxs, dxps, dmu = [dx_post, dx_l], [dxp_l], {}
    for n in 'rkv':
        (dx_n, dxp_n), (dmu[n], grads['rw_w' + n]) = _stage_bwd("proj_bwd_" + n, _f_proj, (x, xp), proj_params[n],
                                                                 (dproj[n],), tb, proxied=(1,))
        dxs.append(dx_n)
        dxps.append(dxp_n)
    grads['rw_mu'] = jnp.concatenate([dmu['r'], dmu_w, dmu['k'], dmu['v'], dmu_a, dmu_g], axis=0)
    grads['ln_g'] = jnp.concatenate(dln_g, axis=0)
    grads['ln_b'] = jnp.concatenate(dln_b, axis=0)
    grad_x = _addn("grad_x", dxs + [_shift_up(_addn("dxp_sum", dxps))])
    late = [n for n in BIG if n not in BIG_READY]
    late_sums = dict(zip(late, _chip_sums("b", [grads[n] for n in late], core)))
    return loss_blk, grad_x, grads, reduced, late_sums


def _position():
    return lax.axis_index("x"), lax.axis_index("y"), lax.axis_index("c")


def _other_chips(x, y):
    return [(1 - x, y), (x, 1 - y), (1 - x, 1 - y)]


def _chip_slice(ref, axis, q, size):
    idx = [slice(None)] * len(ref.shape)
    idx[axis] = pl.ds(pl.multiple_of(q * size, size), size)
    return ref.at[tuple(idx)]


_HBM = pl.BlockSpec(memory_space=pltpu.HBM)


def _gather_small_phases(src, dst, axes, sems):
    n = len(src)
    send_sems, recv_sems, own_sems = sems
    x, y, c = _position()
    chips = _other_chips(x, y)
    sizes = [src[a].shape[axes[a]] for a in range(n)]

    def copy(a, k, q):
        return pltpu.make_async_remote_copy(
            src_ref=src[a], dst_ref=_chip_slice(dst[a], axes[a], q, sizes[a]), send_sem=send_sems.at[a, k],
            recv_sem=recv_sems.at[a, k], device_id=(*chips[k], c), device_id_type=MESH)

    def own(a):
        return pltpu.make_async_copy(src[a], _chip_slice(dst[a], axes[a], 2 * x + y, sizes[a]), own_sems.at[a])

    def start():
        for a in range(n):
            own(a).start()
            for k in range(3):
                copy(a, k, 2 * x + y).start()

    def finish():
        for a in range(n):
            for k, (cx, cy) in enumerate(chips):
                copy(a, k, 2 * cx + cy).wait_recv()
        for a in range(n):
            for k in range(3):
                copy(a, k, 2 * x + y).wait_send()
            own(a).wait()

    return start, finish


def _gather_early(big, small, axes):
    nb, ns = len(big), len(small)
    full_shapes = [tuple(s * 4 if i == ax else s for i, s in enumerate(a.shape)) for a, ax in zip(small, axes)]

    def body(*refs):
        src_b, src_s = refs[:nb], refs[nb:nb + ns]
        dst_b, dst_s = refs[nb + ns:2 * nb + ns], refs[2 * nb + ns:2 * (nb + ns)]
        sems = refs[2 * (nb + ns):]
        small_start, small_finish = _gather_small_phases(src_s, dst_s, axes, sems[5:])
        small_start()
        for phase in _gather_big_phases(src_b, dst_b, sems[:5]):
            phase()
        small_finish()

    outs = pl.pallas_call(
        body, name="gather_early", in_specs=[_HBM] * (nb + ns), out_specs=[_HBM] * (nb + ns),
        out_shape=[_sds((4,) + a.shape, a.dtype) for a in big] + [_sds(s, a.dtype) for s, a in zip(full_shapes, small)],
        scratch_shapes=_gather_big_sems(nb) + [pltpu.SemaphoreType.DMA((ns, 3)), pltpu.SemaphoreType.DMA((ns, 3)),
                                               pltpu.SemaphoreType.DMA((ns,))],
        compiler_params=_cparams(),
    )(*big, *small)
    return outs[:nb], outs[nb:]


def _scatter_pieces(fulls, axes, sums):
    n, nsum = len(fulls), len(sums)
    sizes = [a.shape[ax] // 4 for a, ax in zip(fulls, axes)]
    shard_shapes = [tuple(sz if i == ax else s for i, s in enumerate(a.shape)) for a, ax, sz in zip(fulls, axes, sizes)]

    def body(*refs):
        src, big_src = refs[:n], refs[n:n + nsum]
        land, big_land = refs[n + nsum:2 * n + nsum], refs[2 * n + nsum:2 * (n + nsum)]
        send_sems, recv_sems = refs[2 * (n + nsum):2 * (n + nsum) + 2]
        big_start, big_finish = _scatter_big_phases(big_src, big_land, refs[2 * (n + nsum) + 2:])
        big_start()
        x, y, c = _position()
        chips = _other_chips(x, y)

        def copy(a, k):
            cx, cy = chips[k]
            return pltpu.make_async_remote_copy(
                src_ref=_chip_slice(src[a], axes[a], 2 * cx + cy, sizes[a]), dst_ref=land[a].at[k],
                send_sem=send_sems.at[a, k], recv_sem=recv_sems.at[a, k], device_id=(cx, cy, c), device_id_type=MESH)

        for a in range(n):
            for k in range(3):
                copy(a, k).start()
        for a in range(n):
            for k in range(3):
                copy(a, k).wait_recv()
        for a in range(n):
            for k in range(3):
                copy(a, k).wait_send()
        big_finish()

    outs = pl.pallas_call(
        body, name="scatter_grads", in_specs=[_HBM] * (n + nsum), out_specs=[_HBM] * (n + nsum),
        out_shape=[_sds((3,) + s) for s in shard_shapes] + [_sds((3,) + s.shape[1:], s.dtype) for s in sums],
        scratch_shapes=[pltpu.SemaphoreType.DMA((n, 3)), pltpu.SemaphoreType.DMA((n, 3))] + _scatter_big_sems(nsum),
        compiler_params=_cparams(),
    )(*fulls, *sums)
    return outs[:n], outs[n:]


def _sibling_swap(name, arrs):
    n = len(arrs)

    def body(*refs):
        src, dst = refs[:n], refs[n:2 * n]
        send_sems, recv_sems = refs[2 * n:]
        x, y, c = _position()
        copies = [pltpu.make_async_remote_copy(src_ref=src[a], dst_ref=dst[a], send_sem=send_sems.at[a], recv_sem=recv_sems.at[a],
                                               device_id=(x, y, 1 - c), device_id_type=MESH) for a in range(n)]
        for cp in copies:
            cp.start()
        for cp in copies:
            cp.wait_recv()
        for cp in copies:
            cp.wait_send()

    return pl.pallas_call(
        body, name=name, in_specs=[_HBM] * n, out_specs=[_HBM] * n, out_shape=[_sds(a.shape) for a in arrs],
        scratch_shapes=[pltpu.SemaphoreType.DMA((n,)), pltpu.SemaphoreType.DMA((n,))],
        compiler_params=_cparams(),
    )(*arrs)


def _sum4(name, own, land):
    rows, cols = own.shape
    tb = _rows_tile(rows)

    def body(o_ref, l0, l1, l2, out_ref):
        out_ref[...] = ((o_ref[...] + l0[...]) + l1[...]) + l2[...]

    blk = pl.BlockSpec((tb, cols), lambda i: (i, 0))
    lands = [pl.BlockSpec((None, tb, cols), functools.partial(lambda k, i: (k, i, 0), k)) for k in range(3)]
    return pl.pallas_call(body, name=name, grid=(rows // tb,), in_specs=[blk] + lands, out_specs=blk,
                          out_shape=_sds((rows, cols)), compiler_params=_cparams(("parallel",)))(own, land, land, land)


def _allreduce_adamw_small(g, w, m, v):
    rows, lanes = g.shape

    def body(g_ref, w_ref, m_ref, v_ref, gs_ref, d_ref, mn_ref, vn_ref, land, send_sems, recv_sems):
        x, y, c = _position()
        me = 4 * x + 2 * y + c
        masks = [(bx, by, bc) for bx in (0, 1) for by in (0, 1) for bc in (0, 1)][1:]

        def peer(mask):
            return (x ^ mask[0], y ^ mask[1], c ^ mask[2])

        def copy(j, slot):
            return pltpu.make_async_remote_copy(src_ref=g_ref, dst_ref=land.at[slot], send_sem=send_sems.at[j],
                                                recv_sem=recv_sems.at[j], device_id=peer(masks[j]), device_id_type=MESH)

        for j in range(7):
            copy(j, me).start()
        land[me] = g_ref[...]
        for j in range(7):
            px, py, pc = peer(masks[j])
            copy(j, 4 * px + 2 * py + pc).wait_recv()
        for j in range(7):
            copy(j, me).wait_send()
        total = land[0]
        for dev in range(1, 8):
            total = total + land[dev]
        delta, mn, vn = _adamw_math(w_ref[...], total, m_ref[...], v_ref[...])
        gs_ref[...] = total
        d_ref[...] = delta
        mn_ref[...] = mn
        vn_ref[...] = vn

    vmem = pl.BlockSpec(memory_space=pltpu.VMEM)
    return pl.pallas_call(
        body, name="allreduce_adamw_small", in_specs=[vmem] * 4, out_specs=[vmem] * 4, out_shape=[_sds((rows, lanes))] * 4,
        scratch_shapes=[pltpu.VMEM((8, rows, lanes), F32), pltpu.SemaphoreType.DMA((7,)), pltpu.SemaphoreType.DMA((7,))],
        compiler_params=_cparams(),
    )(g, w, m, v)


def _row_half(ref, c):
    r2 = ref.shape[-2] // 2
    lead = (slice(None),) * (len(ref.shape) - 2)
    return ref.at[(*lead, pl.ds(pl.multiple_of(c * r2, r2), r2), slice(None))]


def _gather_big_phases(src, dst, sems):
    n = len(src)
    ici_send, ici_recv, d2d_send, d2d_recv, own_sems = sems
    x, y, c = _position()
    me = 2 * x + y
    chips = _other_chips(x, y)
    ids = [2 * cx + cy for cx, cy in chips]

    def ici(a, k, q):
        return pltpu.make_async_remote_copy(
            src_ref=_row_half(src[a], c), dst_ref=_row_half(dst[a].at[q], c), send_sem=ici_send.at[a, k],
            recv_sem=ici_recv.at[a, k], device_id=(*chips[k], c), device_id_type=MESH)

    def d2d(a, k, half):
        where = _row_half(dst[a].at[ids[k]], half)
        return pltpu.make_async_remote_copy(src_ref=where, dst_ref=where, send_sem=d2d_send.at[a, k], recv_sem=d2d_recv.at[a, k],
                                            device_id=(x, y, 1 - c), device_id_type=MESH)

    def own(a):
        return pltpu.make_async_copy(src[a], dst[a].at[me], own_sems.at[a])

    def start():
        for a in range(n):
            own(a).start()
            for k in range(3):
                ici(a, k, me).start()

    def forward():
        for a in range(n):
            for k in range(3):
                ici(a, k, ids[k]).wait_recv()
                d2d(a, k, c).start()

    def finish():
        for a in range(n):
            for k in range(3):
                d2d(a, k, 1 - c).wait_recv()
        for a in range(n):
            for k in range(3):
                ici(a, k, me).wait_send()
                d2d(a, k, c).wait_send()
            own(a).wait()

    return start, forward, finish


def _gather_big_sems(n):
    return [pltpu.SemaphoreType.DMA((n, 3))] * 4 + [pltpu.SemaphoreType.DMA((n,))]


def _chip_sums(tag, grads, core):
    views = [g.reshape(4, -1, g.shape[-1]) for g in grads]
    others = _sibling_halves("sibling_halves_" + tag, views)
    return [_half_add(f"half_add_{tag}{i}", v, o, core) for i, (v, o) in enumerate(zip(views, others))]


def _sibling_halves_phases(src, dst, sems):
    n = len(src)
    send_sems, recv_sems = sems
    x, y, c = _position()

    def copy(a):
        return pltpu.make_async_remote_copy(src_ref=_row_half(src[a], 1 - c), dst_ref=dst[a], send_sem=send_sems.at[a],
                                            recv_sem=recv_sems.at[a], device_id=(x, y, 1 - c), device_id_type=MESH)

    def start():
        for a in range(n):
            copy(a).start()

    def finish():
        for a in range(n):
            copy(a).wait_recv()
        for a in range(n):
            copy(a).wait_send()

    return start, finish


def _sibling_halves_sems(n):
    return [pltpu.SemaphoreType.DMA((n,)), pltpu.SemaphoreType.DMA((n,))]


def _sibling_halves_shapes(views):
    return [_sds((4, v.shape[1] // 2, v.shape[2])) for v in views]


def _sibling_halves(name, views):
    n = len(views)

    def body(*refs):
        for phase in _sibling_halves_phases(refs[:n], refs[n:2 * n], refs[2 * n:]):
            phase()

    return pl.pallas_call(
        body, name=name, in_specs=[_HBM] * n, out_specs=[_HBM] * n, out_shape=_sibling_halves_shapes(views),
        scratch_shapes=_sibling_halves_sems(n), compiler_params=_cparams(),
    )(*views)


def _rows_tile_capped(rows, cap=256):
    return min(_rows_tile(rows), cap)


def _half_add(name, view, other, core):
    _, r, k = view.shape
    r2 = r // 2
    tr = _rows_tile_capped(r2)
    per = r2 // tr

    def body(c_ref, v_ref, o_ref, out_ref):
        out_ref[...] = (v_ref[...] + o_ref[...]).astype(BF16)

    blk = pl.BlockSpec((None, tr, k), lambda q, i, c: (q, i, 0))
    return pl.pallas_call(
        body, name=name,
        grid_spec=pltpu.PrefetchScalarGridSpec(
            num_scalar_prefetch=1, grid=(4, per),
            in_specs=[pl.BlockSpec((None, tr, k), lambda q, i, c: (q, c[0] * per + i, 0)), blk], out_specs=blk),
        out_shape=_sds((4, r2, k), BF16), compiler_params=_cparams(("parallel", "parallel")),
    )(core, view, other)


def _scatter_big_phases(src, land, sems):
    n = len(src)
    send_sems, recv_sems = sems
    x, y, c = _position()
    chips = _other_chips(x, y)

    def copy(a, k):
        cx, cy = chips[k]
        return pltpu.make_async_remote_copy(src_ref=src[a].at[2 * cx + cy], dst_ref=land[a].at[k], send_sem=send_sems.at[a, k],
                                            recv_sem=recv_sems.at[a, k], device_id=(cx, cy, c), device_id_type=MESH)

    def start():
        for a in range(n):
            for k in range(3):
                copy(a, k).start()

    def finish():
        for a in range(n):
            for k in range(3):
                copy(a, k).wait_recv()
        for a in range(n):
            for k in range(3):
                copy(a, k).wait_send()

    return start, finish


def _scatter_big_sems(n):
    return [pltpu.SemaphoreType.DMA((n, 3)), pltpu.SemaphoreType.DMA((n, 3))]


def _sum4_big(name, sums, land, chip):
    _, r2, k = sums.shape
    tr = _rows_tile_capped(r2)

    def body(q_ref, s_ref, l0, l1, l2, out_ref):
        out_ref[...] = ((s_ref[...].astype(F32) + l0[...].astype(F32)) + l1[...].astype(F32)) + l2[...].astype(F32)

    lands = [pl.BlockSpec((None, tr, k), functools.partial(lambda j, i, q: (j, i, 0), j)) for j in range(3)]
    return pl.pallas_call(
        body, name=name,
        grid_spec=pltpu.PrefetchScalarGridSpec(
            num_scalar_prefetch=1, grid=(r2 // tr,),
            in_specs=[pl.BlockSpec((None, tr, k), lambda i, q: (q[0], i, 0))] + lands,
            out_specs=pl.BlockSpec((tr, k), lambda i, q: (i, 0))),
        out_shape=_sds((r2, k)), compiler_params=_cparams(("parallel",)),
    )(chip, sums, land, land, land)


def _adamw_halves(name, mine, theirs, w, m, v, core):
    r, k = w.shape
    r2 = r // 2
    tr = _rows_tile_capped(r2, 512)
    per = r2 // tr

    def body(c_ref, mine_ref, theirs_ref, w_ref, m_ref, v_ref, g_out, d_out, m_out, v_out):
        g = jnp.where(pl.program_id(0) == c_ref[0], mine_ref[...], theirs_ref[...])
        delta, mn, vn = _adamw_math(w_ref[...], g, m_ref[...], v_ref[...])
        g_out[...] = g
        d_out[...] = delta
        m_out[...] = mn
        v_out[...] = vn

    half = pl.BlockSpec((tr, k), lambda h, i, c: (i, 0))
    full = pl.BlockSpec((tr, k), lambda h, i, c: (h * per + i, 0))
    return pl.pallas_call(
        body, name=name,
        grid_spec=pltpu.PrefetchScalarGridSpec(num_scalar_prefetch=1, grid=(2, per), in_specs=[half, half, full, full, full],
                                               out_specs=[full] * 4),
        out_shape=[_sds((r, k))] * 4, compiler_params=_cparams(("parallel", "parallel")),
    )(core, mine, theirs, w, m, v)


def _drops_layer_axis(name):
    return not (name.startswith('mlp') or name == 's5_d')


def _work(name, arr):
    return arr.reshape(arr.shape[1:]) if _drops_layer_axis(name) else arr


def _work_axis(name):
    return SHARD_AXIS[name] - (1 if _drops_layer_axis(name) else 0)


def _as2d(a):
    return a.reshape(-1, a.shape[-1])


def _replicated_2d(name, arr):
    if name in ('ln_g', 'ln_b'):
        return arr
    if name == 'rw_r_k':
        return arr.reshape(1, -1)
    if name == 's5_log_dt':
        return arr.reshape(-1, 1)
    if name.startswith('s5_'):
        return arr.reshape(arr.shape[1:])
    return arr


def _pack(arrs):
    flat = []
    for a in arrs:
        f = a.reshape(-1)
        flat.append(jnp.pad(f, (0, -f.shape[0] % 128)))
    f = jnp.concatenate(flat)
    f = jnp.pad(f, (0, -f.shape[0] % 1024))
    return f.reshape(-1, 128)


def _unpack(packed, shapes):
    flat = packed.reshape(-1)
    out, at = [], 0
    for s in shapes:
        size = math.prod(s)
        out.append(flat[at:at + size].reshape(s))
        at += size + (-size % 128)
    return out


def kernel(x, ln_g, ln_b, rw_mu, rw_w0, rw_w1, rw_w2, rw_a0, rw_a1, rw_a2, rw_g1, rw_g2, rw_k_k, rw_k_a, rw_r_k, rw_wr, rw_wk, rw_wv, rw_wo, rw_lnx_g, rw_lnx_b, s5_a_re, s5_a_im, s5_log_dt, s5_b_re, s5_b_im, s5_c_re, s5_c_im, s5_d, s5_w_glu, mlp_w1, mlp_w2, loss_target, m_ln_g, m_ln_b, m_rw_mu, m_rw_w0, m_rw_w1, m_rw_w2, m_rw_a0, m_rw_a1, m_rw_a2, m_rw_g1, m_rw_g2, m_rw_k_k, m_rw_k_a, m_rw_r_k, m_rw_wr, m_rw_wk, m_rw_wv, m_rw_wo, m_rw_lnx_g, m_rw_lnx_b, m_s5_a_re, m_s5_a_im, m_s5_log_dt, m_s5_b_re, m_s5_b_im, m_s5_c_re, m_s5_c_im, m_s5_d, m_s5_w_glu, m_mlp_w1, m_mlp_w2, v_ln_g, v_ln_b, v_rw_mu, v_rw_w0, v_rw_w1, v_rw_w2, v_rw_a0, v_rw_a1, v_rw_a2, v_rw_g1, v_rw_g2, v_rw_k_k, v_rw_k_a, v_rw_r_k, v_rw_wr, v_rw_wk, v_rw_wv, v_rw_wo, v_rw_lnx_g, v_rw_lnx_b, v_s5_a_re, v_s5_a_im, v_s5_log_dt, v_s5_b_re, v_s5_b_im, v_s5_c_re, v_s5_c_im, v_s5_d, v_s5_w_glu, v_mlp_w1, v_mlp_w2):
    d = dict(locals())
    x_pos, y_pos, c_pos = _position()
    chip = 2 * x_pos + y_pos
    chip_arr = jnp.reshape(chip, (1,)).astype(jnp.int32)
    core_arr = jnp.reshape(c_pos, (1,)).astype(jnp.int32)

    small = [n for n in SHARD_AXIS if n not in BIG]
    axes = [_work_axis(n) for n in small]
    big_views, small_fulls = _gather_early([_as2d(d[n]).astype(BF16) for n in BIG_EARLY], [_work(n, d[n]) for n in small], axes)
    views = dict(zip(BIG_EARLY, big_views))
    fw = dict(zip(small, small_fulls))
    c_model = d['x'].shape[-1]
    for n in BIG_EARLY:
        fw[n] = views[n].reshape(c_model, c_model)
    w1_layers, w2_layers = d['mlp_w1'].astype(BF16), d['mlp_w2'].astype(BF16)
    fw['late_a'] = [_as2d(d['rw_wo']).astype(BF16), w1_layers[0], w2_layers[0]]
    fw['late_b'] = [w1_layers[1]]
    fw['late_c'] = [_as2d(d['s5_w_glu']).astype(BF16), w2_layers[1]]
    for n in REPLICATED:
        fw[n] = _replicated_2d(n, d[n])

    loss_blk, grad_x, grads, reduced, late_sums = _local_step(d['x'][0], d['loss_target'][0], fw, core_arr)
    loss = lax.psum(loss_blk[0, 0], ('x', 'y', 'c'))
    out = {}

    pieces = [grads[n] for n in small]
    lands, late_lands = _scatter_pieces(pieces, axes, list(late_sums.values()))
    reduced.update(zip(late_sums, zip(late_sums.values(), late_lands)))
    mine = [_sum4_big("sum4_" + n, *reduced[n], chip_arr) for n in BIG]
    for n, g, ax, land in zip(small, pieces, axes, lands):
        size = g.shape[ax] // 4
        mine.append(_sum4("sum4_" + n, lax.dynamic_slice_in_dim(g, chip * size, size, ax), land))
    theirs = _sibling_swap("swap_sums", mine)
    for n, a, b in zip(BIG + small, mine, theirs):
        w2d, m2d, v2d = _as2d(d[n]), _as2d(d['m_' + n]), _as2d(d['v_' + n])
        res = (_adamw_halves("adamw_" + n, a, b, w2d, m2d, v2d, core_arr) if n in BIG
               else _adamw("adamw_" + n, (a, b), w2d, m2d, v2d))
        out[n] = [r.reshape(d[n].shape) for r in res]

    rep_shapes = [d[n].shape for n in REPLICATED]
    packs = [_pack([grads[n] for n in REPLICATED])] + [_pack([d[p + n] for n in REPLICATED]) for p in ('', 'm_', 'v_')]
    res = [_unpack(p, rep_shapes) for p in _allreduce_adamw_small(*packs)]
    for i, n in enumerate(REPLICATED):
        out[n] = [r[i] for r in res]

    grad_x = grad_x.reshape(d['x'].shape)
    return (loss, grad_x, *[out[n][0] for n in WEIGHTS], *[out[n][1] for n in WEIGHTS],
            *[out[n][2] for n in WEIGHTS], *[out[n][3] for n in WEIGHTS])
```

```python
import functools
import math

import jax
import jax.numpy as jnp
from jax import lax
from jax.experimental import pallas as pl
from jax.experimental.pallas import tpu as pltpu

F32 = jnp.float32
BF16 = jnp.bfloat16
MESH = pl.DeviceIdType.MESH

HEAD = 64
SSM_GROUP = 16
SSM_STATE = 64
GN_EPS = 64e-5
LN_EPS = 1e-5
DEPTH = 2
DN_ALPHA = (2.0 * DEPTH) ** 0.25
ADAM_LR, ADAM_B1, ADAM_B2, ADAM_EPS, ADAM_WD, ADAM_STEP = 0.001, 0.9, 0.999, 1e-08, 0.01, 10
REC_CHUNK = 64
V7X_VMEM_BYTES = 64 * 2 ** 20
VMEM_LIMIT = V7X_VMEM_BYTES - 8 * 2 ** 20

WEIGHTS = ['ln_g', 'ln_b', 'rw_mu', 'rw_w0', 'rw_w1', 'rw_w2', 'rw_a0', 'rw_a1', 'rw_a2', 'rw_g1', 'rw_g2',
           'rw_k_k', 'rw_k_a', 'rw_r_k', 'rw_wr', 'rw_wk', 'rw_wv', 'rw_wo', 'rw_lnx_g', 'rw_lnx_b',
           's5_a_re', 's5_a_im', 's5_log_dt', 's5_b_re', 's5_b_im', 's5_c_re', 's5_c_im', 's5_d', 's5_w_glu',
           'mlp_w1', 'mlp_w2']
SHARD_AXIS = {'rw_mu': 2, 'rw_w1': 1, 'rw_w2': 2, 'rw_a1': 1, 'rw_a2': 2, 'rw_g1': 1, 'rw_g2': 2,
              'rw_wr': 1, 'rw_wk': 1, 'rw_wv': 1, 'rw_wo': 1, 's5_d': 1, 's5_w_glu': 2, 'mlp_w1': 2, 'mlp_w2': 1}
REPLICATED = [n for n in WEIGHTS if n not in SHARD_AXIS]
BIG_EARLY = ['rw_wr', 'rw_wk', 'rw_wv']
BIG_LATE = ['rw_wo', 's5_w_glu', 'mlp_w1', 'mlp_w2']
BIG = BIG_EARLY + BIG_LATE
BIG_READY = ['s5_w_glu', 'mlp_w1', 'mlp_w2']


def _sds(shape, dtype=F32):
    return jax.ShapeDtypeStruct(tuple(shape), dtype)


def _cparams(sem=None, **kw):
    if sem is not None:
        kw["dimension_semantics"] = sem
    return pltpu.CompilerParams(vmem_limit_bytes=VMEM_LIMIT, **kw)


def _mm_products(a, b, g):
    gb = g.astype(BF16)
    da = lax.dot_general(gb, b.astype(BF16), (((1,), (1,)), ((), ())), preferred_element_type=F32)
    db = lax.dot_general(a.astype(BF16), gb, (((0,), (0,)), ((), ())), preferred_element_type=F32)
    return da, db


@jax.custom_vjp
def _mm_plain(a, b):
    return jnp.dot(a.astype(BF16), b.astype(BF16), preferred_element_type=F32)


def _mm_plain_bwd(res, g):
    da, db = _mm_products(*res, g)
    return da.astype(res[0].dtype), db.astype(res[1].dtype)


_mm_plain.defvjp(lambda a, b: (_mm_plain(a, b), (a, b)), _mm_plain_bwd)


@jax.custom_vjp
def _mm_proxy(a, b, z):
    return jnp.dot(a.astype(BF16), b.astype(BF16), preferred_element_type=F32)


def _mm_proxy_bwd(res, g):
    da, db = _mm_products(*res, g)
    return da.astype(res[0].dtype), jnp.zeros_like(res[1]), db


_mm_proxy.defvjp(lambda a, b, z: (_mm_proxy(a, b, z), (a, b)), _mm_proxy_bwd)


def mm(a, b, z=None):
    return _mm_plain(a, b) if z is None else _mm_proxy(a, b, z)


def _split3(x):
    hi = x.astype(BF16)
    r1 = x - hi.astype(F32)
    mid = r1.astype(BF16)
    lo = (r1 - mid.astype(F32)).astype(BF16)
    return hi, mid, lo


def _head_sum_impl(x):
    c = x.shape[1]
    lanes = 128
    sel = (lax.broadcasted_iota(jnp.int32, (c, lanes), 0) // HEAD
           == lax.broadcasted_iota(jnp.int32, (c, lanes), 1)).astype(BF16)
    s = sum(jnp.dot(p, sel, preferred_element_type=F32) for p in _split3(x))
    return sum(lax.dot_general(p, sel, (((1,), (1,)), ((), ())), preferred_element_type=F32) for p in _split3(s))


@jax.custom_vjp
def head_sum(x):
    return _head_sum_impl(x)


head_sum.defvjp(lambda x: (_head_sum_impl(x), None), lambda _, g: (_head_sum_impl(g),))


def _ln(x, g, b):
    mu = jnp.mean(x, axis=-1, keepdims=True)
    xc = x - mu
    var = jnp.mean(xc * xc, axis=-1, keepdims=True)
    return xc * lax.rsqrt(var + LN_EPS) * g + b


def _f_proj(acts, params, proxies):
    x, xp = acts
    mu, w = params
    return (mm(x + (xp - x) * mu, w, proxies[1]),)


def _f_lora(acts, params, proxies):
    x, xp, kraw = acts
    mu_w, mu_a, mu_g, w0, w1, w2, a0, a1, a2, g1, g2, k_k, k_a = params
    xx = xp - x
    w_pre = w0 + mm(jnp.tanh(mm(x + xx * mu_w, w1)), w2)
    z = -w_pre
    softplus = jnp.maximum(z, 0.0) + jnp.log(1.0 + jnp.exp(-jnp.abs(z)))
    log_decay = -jnp.exp(-softplus - 0.5)
    a = jax.nn.sigmoid(a0 + mm(mm(x + xx * mu_a, a1), a2))
    g = mm(jax.nn.sigmoid(mm(x + xx * mu_g, g1)), g2)
    kk = kraw * k_k
    kkn = kk / jnp.maximum(jnp.sqrt(head_sum(kk * kk)), 1e-12)
    k2 = kraw * (1.0 + (a - 1.0) * k_a)
    return log_decay, k2, -kkn, kkn * a, g


def _f_post(acts, params, proxies):
    o, r, k2, v, g, x = acts
    lnx_g, lnx_b, r_k, wo, ln_g, ln_b = params
    om = head_sum(o) * (1.0 / HEAD)
    oc = o - om
    ov = head_sum(oc * oc) * (1.0 / HEAD)
    on = oc * lax.rsqrt(ov + GN_EPS) * lnx_g + lnx_b
    bonus = head_sum(r * k2 * r_k) * v
    y = mm((on + bonus) * g, wo, proxies[3])
    return (_ln(DN_ALPHA * x + y, ln_g, ln_b),)


def _f_glu(acts, params, proxies):
    ys, h = acts
    d, wv0, wv1, wg0, wg1, ln_g, ln_b = params
    y = jax.nn.gelu(ys + h * d)
    mix = jnp.concatenate([mm(y, wv0, proxies[1]) * jax.nn.sigmoid(mm(y, wg0, proxies[3])),
                           mm(y, wv1, proxies[2]) * jax.nn.sigmoid(mm(y, wg1, proxies[4]))], axis=1)
    return (_ln(DN_ALPHA * h + mix, ln_g, ln_b),)


def _f_zoh(a_re, a_im, log_dt, b_re_t, b_im_t):
    dt = jnp.exp(log_dt)
    lam_re = jnp.minimum(a_re, -1e-4)
    lam_im = a_im
    mag = jnp.exp(dt * lam_re)
    abar_re = mag * jnp.cos(dt * lam_im)
    abar_im = mag * jnp.sin(dt * lam_im)
    den = lam_re * lam_re + lam_im * lam_im
    nr, ni = abar_re - 1.0, abar_im
    coef_re = ((nr * lam_re + ni * lam_im) / den)[:, None, :]
    coef_im = ((ni * lam_re - nr * lam_im) / den)[:, None, :]
    return (abar_re, abar_im, coef_re * b_re_t - coef_im * b_im_t, coef_re * b_im_t + coef_im * b_re_t)


def _bdot16_raw(a, b, ca, cb):
    return lax.dot_general(a.astype(BF16), b.astype(BF16), (((ca,), (cb,)), ((0,), (0,))), preferred_element_type=F32)


@functools.partial(jax.custom_vjp, nondiff_argnums=(2, 3))
def _bdot16(a, b, ca, cb):
    return _bdot16_raw(a, b, ca, cb)


def _bdot16_bwd(ca, cb, res, g):
    a, b = res
    if (ca, cb) == (2, 1):
        return _bdot16_raw(g, b, 2, 2), _bdot16_raw(a, g, 1, 1)
    if (ca, cb) == (2, 2):
        return _bdot16_raw(g, b, 2, 1), _bdot16_raw(g, a, 1, 1)
    assert (ca, cb) == (1, 1)
    return _bdot16_raw(b, g, 2, 2), _bdot16_raw(a, g, 2, 1)


_bdot16.defvjp(lambda a, b, ca, cb: (_bdot16_raw(a, b, ca, cb), (a, b)), _bdot16_bwd)

def _time_sums(x, suffix):
    hg, ln, _ = x.shape
    row = lax.broadcasted_iota(jnp.int32, (hg, ln, ln), 1)
    col = lax.broadcasted_iota(jnp.int32, (hg, ln, ln), 2)
    tri = ((row <= col) if suffix else (row >= col)).astype(BF16)
    return sum(lax.dot_general(tri, p, (((2,), (1,)), ((0,), (0,))), preferred_element_type=F32) for p in _split3(x))


@jax.custom_vjp
def _time_cumsum(x):
    return _time_sums(x, False)


_time_cumsum.defvjp(lambda x: (_time_sums(x, False), None), lambda _, g: (_time_sums(g, True),))

_dot_score = _bdot16
_dot_inverse = _bdot16
_dot_value = _bdot16


def _rec_chunk(s0, r, lw, k, v, a, b):
    hg, ln, _ = r.shape
    row = lax.broadcasted_iota(jnp.int32, (hg, ln, ln), 1)
    col = lax.broadcasted_iota(jnp.int32, (hg, ln, ln), 2)
    incl, strict = row >= col, row > col
    cum = _time_cumsum(lw)
    total = jnp.sum(lw, axis=1, keepdims=True)
    e_cum, e_inv, e_prev, e_tail = jnp.exp(cum), jnp.exp(-cum), jnp.exp(cum - lw), jnp.exp(total - cum)
    rt, at, bt, kt = r * e_cum, a * e_prev, b * e_inv, k * e_inv
    ar = jnp.concatenate([at, rt], axis=1)
    on_b, on_k = _dot_score(ar, bt, 2, 2), _dot_score(ar, kt, 2, 2)
    aab, arb = jnp.where(strict, on_b[:, :ln], 0.0), jnp.where(incl, on_b[:, ln:], 0.0)
    aak, ark = jnp.where(strict, on_k[:, :ln], 0.0), jnp.where(incl, on_k[:, ln:], 0.0)
    p = (row == col).astype(F32) + aab
    m = aab
    for _ in range(int(math.log2(ln)) - 1):
        m = _dot_inverse(m, m, 2, 1)
        p = p + _dot_inverse(p, m, 2, 1)
    from_state = _dot_value(ar, s0, 2, 2)
    from_v = _dot_value(jnp.concatenate([aak, ark], axis=1), v, 2, 1)
    u = _dot_inverse(p, from_state[:, :ln] + from_v[:, :ln], 2, 1)
    o = from_state[:, ln:] + from_v[:, ln:] + _dot_value(arb, u, 2, 1)
    s1 = s0 * jnp.exp(total) + _dot_value(jnp.concatenate([u, v], axis=1),
                                          jnp.concatenate([b * e_tail, k * e_tail], axis=1), 1, 1)
    return o, s1


def _full_spec(shape):
    nd = len(shape)
    return pl.BlockSpec(tuple(shape), lambda *_: (0,) * nd)


def _stage_fwd(name, f, acts, params, out_dims, tb):
    t = acts[0].shape[0]
    na, npar = len(acts), len(params)

    def body(*refs):
        outs = f(tuple(r[...] for r in refs[:na]), tuple(r[...] for r in refs[na:na + npar]), (None,) * npar)
        for r, val in zip(refs[na + npar:], outs):
            r[...] = val

    return pl.pallas_call(
        body, name=name, grid=(t // tb,),
        in_specs=[pl.BlockSpec((tb, a.shape[1]), lambda i: (i, 0)) for a in acts] + [_full_spec(p.shape) for p in params],
        out_specs=[pl.BlockSpec((tb, d), lambda i: (i, 0)) for d in out_dims],
        out_shape=[_sds((t, d)) for d in out_dims],
        compiler_params=_cparams(("arbitrary",)),
    )(*acts, *params)


def _stage_bwd(name, f, acts, params, couts, tb, proxied=(), halves_of=()):
    nh = len(halves_of)
    t = acts[0].shape[0]
    groups = [c if isinstance(c, tuple) else (c,) for c in couts]
    couts = [term for grp in groups for term in grp]
    na, npar, nc = len(acts), len(params), len(couts)
    steps = t // tb

    def f_diff(act_vals, diff_vals, param_vals):
        real = tuple(param_vals[i] if i in proxied else diff_vals[i] for i in range(npar))
        proxies = tuple(diff_vals[i] if i in proxied else None for i in range(npar))
        return f(act_vals, real, proxies)

    def body(*refs):
        a_refs, p_hbm, c_refs = refs[:na], refs[na:na + npar], refs[na + npar:na + npar + nc]
        o = na + npar + nc
        half_src, o = refs[o:o + nh], o + nh
        da_refs, dp_hbm, half_dst = refs[o:o + na], refs[o + na:o + na + npar], refs[o + na + npar:o + na + npar + nh]
        o = o + na + npar + nh
        p_buf, acc, half_sems = refs[o:o + npar], refs[o + npar:o + 2 * npar], refs[o + 2 * npar:]
        i = pl.program_id(0)
        if nh:
            half_start, half_finish = _sibling_halves_phases(half_src, half_dst, half_sems)
            pl.when(i == 0)(half_start)

        @pl.when(i == 0)
        def _():
            for src, dst in zip(p_hbm, p_buf):
                pltpu.sync_copy(src, dst)
            for r in acc:
                r[...] = jnp.zeros_like(r)

        param_vals = tuple(r[...] for r in p_buf)
        diff_vals = tuple(jnp.zeros(v.shape, F32) if i in proxied else v for i, v in enumerate(param_vals))
        _, vjp = jax.vjp(functools.partial(f_diff, param_vals=param_vals), tuple(r[...] for r in a_refs), diff_vals)
        terms = iter(c_refs)
        d_acts, d_params = vjp(tuple(functools.reduce(jnp.add, [next(terms)[...] for _ in grp]) for grp in groups))
        for r, val in zip(da_refs, d_acts):
            r[...] = val
        for r, val in zip(acc, d_params):
            r[...] += val

        @pl.when(i == steps - 1)
        def _():
            for src, dst in zip(acc, dp_hbm):
                pltpu.sync_copy(src, dst)

        if nh:
            pl.when(i == steps - 1)(half_finish)

    hbm = pl.BlockSpec(memory_space=pltpu.HBM)
    outs = pl.pallas_call(
        body, name=name, grid=(steps,),
        in_specs=[pl.BlockSpec((tb, a.shape[1]), lambda i: (i, 0)) for a in acts] + [hbm] * npar
        + [pl.BlockSpec((tb, c.shape[1]), lambda i: (i, 0)) for c in couts] + [hbm] * nh,
        out_specs=[pl.BlockSpec((tb, a.shape[1]), lambda i: (i, 0)) for a in acts] + [hbm] * (npar + nh),
        out_shape=[_sds(a.shape) for a in acts] + [_sds(p.shape) for p in params] + _sibling_halves_shapes(halves_of),
        scratch_shapes=[pltpu.VMEM(p.shape, p.dtype) for p in params] + [pltpu.VMEM(p.shape, F32) for p in params]
        + (_sibling_halves_sems(nh) if nh else []),
        compiler_params=_cparams(("arbitrary",)),
    )(*acts, *params, *couts, *halves_of)
    if nh:
        return outs[:na], outs[na:na + npar], outs[na + npar:]
    return outs[:na], outs[na:]


def _tiled_matmul(name, a, b, mode, grid, a_spec, b_spec, o_spec, out_shape):
    nk = grid[2]
    dims = {"nn": ((1,), (0,)), "nt": ((1,), (1,)), "tn": ((0,), (0,))}[mode]

    def body(a_ref, b_ref, o_ref, acc):
        kk = pl.program_id(2)

        @pl.when(kk == 0)
        def _():
            acc[...] = jnp.zeros_like(acc)

        acc[...] += lax.dot_general(a_ref[...].astype(BF16), b_ref[...].astype(BF16), (dims, ((), ())),
                                    preferred_element_type=F32)

        @pl.when(kk == nk - 1)
        def _():
            o_ref[...] = acc[...]

    return pl.pallas_call(
        body, name=name, grid=grid, in_specs=[a_spec, b_spec], out_specs=o_spec, out_shape=_sds(out_shape),
        scratch_shapes=[pltpu.VMEM(o_spec.block_shape, F32)],
        compiler_params=_cparams(("parallel", "parallel", "arbitrary")),
    )(a, b)


def _mlp_weight_grad(name, a, b, layer, layers, split, into=None, tile=512):
    t, m = a.shape
    n = b.shape[1]
    tk = min(tile, t)
    tile = 2 * tile
    if split == "n":
        tm, tn = min(tile, m), min(tile, n // 4)
        per = n // 4 // tn
        shape = (4, layers, m, n // 4)
        o_idx = lambda i, j, k: (j // per, layer, i, j % per)
    else:
        tm, tn = min(tile, m // 4), min(tile, n)
        per = m // 4 // tm
        shape = (4, layers, m // 4, n)
        o_idx = lambda i, j, k: (i // per, layer, i % per, j)
    nk = t // tk

    def body(a_ref, b_ref, *rest):
        o_ref, acc = rest[-2:]
        kk = pl.program_id(2)

        @pl.when(kk == 0)
        def _():
            acc[...] = jnp.zeros_like(acc)

        acc[...] += lax.dot_general(a_ref[...].astype(BF16), b_ref[...].astype(BF16), (((0,), (0,)), ((), ())),
                                    preferred_element_type=F32)

        @pl.when(kk == nk - 1)
        def _():
            o_ref[...] = acc[...]

    in_specs = [pl.BlockSpec((tk, tm), lambda i, j, k: (k, i)), pl.BlockSpec((tk, tn), lambda i, j, k: (k, j))]
    operands = [a, b]
    aliases = {}
    if into is not None:
        in_specs.append(pl.BlockSpec(memory_space=pl.ANY))
        operands.append(into)
        aliases = {2: 0}
    return pl.pallas_call(
        body, name=name, grid=(m // tm, n // tn, nk), in_specs=in_specs,
        out_specs=pl.BlockSpec((None, None, tm, tn), o_idx), out_shape=_sds(shape), input_output_aliases=aliases,
        scratch_shapes=[pltpu.VMEM((tm, tn), F32)],
        compiler_params=_cparams(("parallel", "parallel", "arbitrary")),
    )(*operands)


S5_PACK = 8


def _s5_weight_grad(name, x, s, wide_rows, tk):
    t, c = x.shape
    wide = s.shape[1]
    kb, nb = S5_PACK * SSM_GROUP, S5_PACK * SSM_STATE
    nsb = c // kb
    x_spec = pl.BlockSpec((tk, kb), lambda i, j, k: (k, j % nsb))
    s_spec = pl.BlockSpec((tk, nb), lambda i, j, k: (k, j))
    if wide_rows:
        return _tiled_matmul(name, s, x, "tn", (1, wide // nb, t // tk), s_spec, x_spec,
                             pl.BlockSpec((nb, kb), lambda i, j, k: (j, 0)), (wide, kb))
    return _tiled_matmul(name, x, s, "tn", (1, wide // nb, t // tk), x_spec, s_spec,
                         pl.BlockSpec((kb, nb), lambda i, j, k: (0, j)), (kb, wide))


def _mlp_fwd(name, h, w1, w2, layer, ln_g, ln_b, tb, shards=()):
    t, c = h.shape
    nj, fc = w1.shape[0], w1.shape[3]
    nsh = len(shards)
    steps = (t // tb) * nj

    def body(h_ref, w1_ref, w2_ref, g_ref, b_ref, *rest):
        src, (out_ref, s_ref), dst = rest[:nsh], rest[nsh:nsh + 2], rest[nsh + 2:2 * nsh + 2]
        acc, sems = rest[2 * nsh + 2], rest[2 * nsh + 3:]
        j = pl.program_id(1)
        step = pl.program_id(0) * nj + j
        if nsh:
            start, forward, finish = _gather_big_phases(src, dst, sems)
            pl.when(step == 0)(start)

        @pl.when(j == 0)
        def _():
            acc[...] = jnp.zeros_like(acc)

        hid = jnp.dot(h_ref[...].astype(BF16), w1_ref[...].astype(BF16), preferred_element_type=F32)
        act = jnp.square(jnp.maximum(hid, 0.0))
        acc[...] += jnp.dot(act.astype(BF16), w2_ref[...].astype(BF16), preferred_element_type=F32)

        @pl.when(j == nj - 1)
        def _():
            s = DN_ALPHA * h_ref[...] + acc[...]
            s_ref[...] = s
            out_ref[...] = _ln(s, g_ref[...], b_ref[...])

        if nsh:
            pl.when(step == steps // 2)(forward)
            pl.when(step == steps - 1)(finish)

    row = pl.BlockSpec((tb, c), lambda i, j: (i, 0))
    vec = pl.BlockSpec((1, c), lambda i, j: (0, 0))
    outs = pl.pallas_call(
        body, name=name, grid=(t // tb, nj),
        in_specs=[row, pl.BlockSpec((None, None, c, fc), lambda i, j: (j, layer, 0, 0)),
                  pl.BlockSpec((None, None, fc, c), lambda i, j: (j, layer, 0, 0)), vec, vec] + [_HBM] * nsh,
        out_specs=[row, row] + [_HBM] * nsh,
        out_shape=[_sds((t, c)), _sds((t, c))] + [_sds((4,) + a.shape, a.dtype) for a in shards],
        scratch_shapes=[pltpu.VMEM((tb, c), F32)] + (_gather_big_sems(nsh) if nsh else []),
        compiler_params=_cparams(("arbitrary", "arbitrary")),
    )(h, w1, w2, ln_g, ln_b, *shards)
    return outs[0], outs[1], outs[2:]


def _mlp_bwd(name, h, s, dout, w1, w2, layer, ln_g, ln_b, tb):
    t, c = h.shape
    nj, fc = w1.shape[0], w1.shape[3]
    ff = nj * fc
    ni = t // tb
    nt = (((1,), (1,)), ((), ()))
    douts = dout if isinstance(dout, tuple) else (dout,)
    nd = len(douts)

    def body(h_ref, s_ref, *rest):
        dout_refs = rest[:nd]
        (w1_ref, w2_ref, g_ref, b_ref, dh_ref, ds_ref, dhid_ref, act_ref, dg_ref, db_ref,
         ds_scr, dh_acc, dg_acc, db_acc) = rest[nd:]
        i, j = pl.program_id(0), pl.program_id(1)

        @pl.when((i == 0) & (j == 0))
        def _():
            dg_acc[...] = jnp.zeros_like(dg_acc)
            db_acc[...] = jnp.zeros_like(db_acc)

        @pl.when(j == 0)
        def _():
            _, vjp = jax.vjp(_ln, s_ref[...], g_ref[...], b_ref[...])
            ds, dg, db = vjp(functools.reduce(jnp.add, [r[...] for r in dout_refs]))
            ds_scr[...] = ds
            ds_ref[...] = ds.astype(BF16)
            dh_acc[...] = DN_ALPHA * ds
            dg_acc[...] += dg
            db_acc[...] += db

        w1b, w2b = w1_ref[...].astype(BF16), w2_ref[...].astype(BF16)
        hid = jnp.dot(h_ref[...].astype(BF16), w1b, preferred_element_type=F32)
        rl = jnp.maximum(hid, 0.0)
        dact = lax.dot_general(ds_scr[...].astype(BF16), w2b, nt, preferred_element_type=F32)
        dhid = (dact * 2.0 * rl).astype(BF16)
        dh_acc[...] += lax.dot_general(dhid, w1b, nt, preferred_element_type=F32)
        dhid_ref[...] = dhid
        act_ref[...] = (rl * rl).astype(BF16)

        @pl.when(j == nj - 1)
        def _():
            dh_ref[...] = dh_acc[...]

        @pl.when((i == ni - 1) & (j == nj - 1))
        def _():
            dg_ref[...] = dg_acc[...]
            db_ref[...] = db_acc[...]

    row = pl.BlockSpec((tb, c), lambda i, j: (i, 0))
    vec = pl.BlockSpec((1, c), lambda i, j: (0, 0))
    wide = pl.BlockSpec((tb, fc), lambda i, j: (i, j))
    return pl.pallas_call(
        body, name=name, grid=(ni, nj),
        in_specs=[row, row] + [row] * nd + [pl.BlockSpec((None, None, c, fc), lambda i, j: (j, layer, 0, 0)),
                                            pl.BlockSpec((None, None, fc, c), lambda i, j: (j, layer, 0, 0)), vec, vec],
        out_specs=[row, row, wide, wide, vec, vec],
        out_shape=[_sds((t, c)), _sds((t, c), BF16), _sds((t, ff), BF16), _sds((t, ff), BF16), _sds((1, c)), _sds((1, c))],
        scratch_shapes=[pltpu.VMEM((tb, c), F32), pltpu.VMEM((tb, c), F32), pltpu.VMEM((1, c), F32), pltpu.VMEM((1, c), F32)],
        compiler_params=_cparams(("arbitrary", "arbitrary")),
    )(h, s, *douts, w1, w2, ln_g, ln_b)


def _load_heads(ref, hg):
    return jnp.stack([ref[:, h * HEAD:(h + 1) * HEAD] for h in range(hg)])


def _store_heads(ref, val):
    for h in range(val.shape[0]):
        ref[:, h * HEAD:(h + 1) * HEAD] = val[h]


def _rec_fwd(r, lw, k, v, a, b, hg, shards):
    t, c = r.shape
    n = HEAD
    nh = c // n
    ln = REC_CHUNK
    nck = t // ln
    ngrp = nh // hg
    nsh = len(shards)
    steps = ngrp * nck

    def body(r_ref, lw_ref, k_ref, v_ref, a_ref, b_ref, *rest):
        src, (o_ref, s0_ref), dst = rest[:nsh], rest[nsh:nsh + 2], rest[nsh + 2:2 * nsh + 2]
        state, sems = rest[2 * nsh + 2], rest[2 * nsh + 3:]
        step = pl.program_id(0) * nck + pl.program_id(1)
        start, forward, finish = _gather_big_phases(src, dst, sems)
        pl.when(step == 0)(start)

        @pl.when(pl.program_id(1) == 0)
        def _():
            state[...] = jnp.zeros_like(state)

        s0 = state[...]
        s0_ref[...] = s0
        o, s1 = _rec_chunk(s0, *(_load_heads(x, hg) for x in (r_ref, lw_ref, k_ref, v_ref, a_ref, b_ref)))
        _store_heads(o_ref, o)
        state[...] = s1
        pl.when(step == steps // 2)(forward)
        pl.when(step == steps - 1)(finish)

    seq = pl.BlockSpec((ln, hg * n), lambda g, i: (i, g))
    outs = pl.pallas_call(
        body, name="rec_fwd", grid=(ngrp, nck), in_specs=[seq] * 6 + [_HBM] * nsh,
        out_specs=[seq, pl.BlockSpec((None, hg, n, n), lambda g, i: (i, g, 0, 0))] + [_HBM] * nsh,
        out_shape=[_sds((t, c)), _sds((nck, nh, n, n))] + [_sds((4,) + s.shape, s.dtype) for s in shards],
        scratch_shapes=[pltpu.VMEM((hg, n, n), F32)] + _gather_big_sems(nsh),
        compiler_params=_cparams(("arbitrary", "arbitrary")),
    )(r, lw, k, v, a, b, *shards)
    return outs[0], outs[1], outs[2:]


def _rec_bwd(r, lw, k, v, a, b, s0s, do, hg, chip_sums):
    t, c = r.shape
    n = HEAD
    nh = c // n
    ln = REC_CHUNK
    nck = t // ln
    ngrp = nh // hg
    nsum = len(chip_sums)
    steps = ngrp * nck

    def body(r_ref, lw_ref, k_ref, v_ref, a_ref, b_ref, s0_ref, do_ref, *rest):
        src, grad_refs, land = rest[:nsum], rest[nsum:nsum + 6], rest[nsum + 6:2 * nsum + 6]
        dstate, sems = rest[2 * nsum + 6], rest[2 * nsum + 7:]
        step = pl.program_id(0) * nck + pl.program_id(1)
        start, finish = _scatter_big_phases(src, land, sems)
        pl.when(step == 0)(start)

        @pl.when(pl.program_id(1) == 0)
        def _():
            dstate[...] = jnp.zeros_like(dstate)

        _, vjp = jax.vjp(_rec_chunk, s0_ref[...], *(_load_heads(x, hg) for x in (r_ref, lw_ref, k_ref, v_ref, a_ref, b_ref)))
        ds0, *grads = vjp((_load_heads(do_ref, hg), dstate[...]))
        dstate[...] = ds0
        for ref, val in zip(grad_refs, grads):
            _store_heads(ref, val)
        pl.when(step == steps - 1)(finish)

    seq = pl.BlockSpec((ln, hg * n), lambda g, i: (nck - 1 - i, g))
    outs = pl.pallas_call(
        body, name="rec_bwd", grid=(ngrp, nck),
        in_specs=[seq] * 6 + [pl.BlockSpec((None, hg, n, n), lambda g, i: (nck - 1 - i, g, 0, 0)), seq] + [_HBM] * nsum,
        out_specs=[seq] * 6 + [_HBM] * nsum,
        out_shape=[_sds((t, c))] * 6 + [_sds((3,) + s.shape[1:], s.dtype) for s in chip_sums],
        scratch_shapes=[pltpu.VMEM((hg, n, n), F32)] + _scatter_big_sems(nsum),
        compiler_params=_cparams(("arbitrary", "arbitrary")),
    )(r, lw, k, v, a, b, s0s, do, *chip_sums)
    return outs[:6], outs[6:]


def _s5_blocks(c):
    kb, nb = S5_PACK * SSM_GROUP, S5_PACK * SSM_STATE
    return kb, nb, c // kb


def _s5_fwd(h, bc, abar, cc, tb, shards=()):
    t, c = h.shape
    w2 = bc.shape[1]
    w = w2 // 2
    kb, nb, nsb = _s5_blocks(c)

    nsh = len(shards)
    steps = t // tb

    def body(h_ref, bc_ref, a_ref, cc_ref, *rest):
        src, (s_ref, y_ref), dst = rest[:nsh], rest[nsh:nsh + 2], rest[nsh + 2:2 * nsh + 2]
        carry, rows, sems = rest[2 * nsh + 2], rest[2 * nsh + 3], rest[2 * nsh + 4:]
        if nsh:
            start, forward, finish = _gather_big_phases(src, dst, sems)
            pl.when(pl.program_id(0) == 0)(start)

        @pl.when(pl.program_id(0) == 0)
        def _():
            carry[...] = jnp.zeros_like(carry)

        for j in range(w2 // nb):
            ch = (j % nsb) * kb
            rows[:, j * nb:(j + 1) * nb] = jnp.dot(h_ref[:, ch:ch + kb].astype(BF16), bc_ref[:, j * nb:(j + 1) * nb],
                                                   preferred_element_type=F32)
        ar, ai = a_ref[:, :w], a_ref[:, w:]

        def step(i, state):
            hr, hi = state
            nr = ar * hr - ai * hi + rows[pl.ds(i, 1), :w]
            ni = ar * hi + ai * hr + rows[pl.ds(i, 1), w:]
            rows[pl.ds(i, 1), :w] = nr
            rows[pl.ds(i, 1), w:] = ni
            return nr, ni

        hr, hi = lax.fori_loop(0, tb, step, (carry[:, :w], carry[:, w:]))
        carry[:, :w] = hr
        carry[:, w:] = hi
        s_ref[...] = rows[...].astype(BF16)
        for j in range(nsb):
            re, im = j * nb, w + j * nb
            y_ref[:, j * kb:(j + 1) * kb] = (
                jnp.dot(s_ref[:, re:re + nb], cc_ref[re:re + nb, :], preferred_element_type=F32)
                + jnp.dot(s_ref[:, im:im + nb], cc_ref[im:im + nb, :], preferred_element_type=F32))
        if nsh:
            pl.when(pl.program_id(0) == steps // 2)(forward)
            pl.when(pl.program_id(0) == steps - 1)(finish)

    outs = pl.pallas_call(
        body, name="s5_fwd", grid=(steps,),
        in_specs=[pl.BlockSpec((tb, c), lambda i: (i, 0)), _full_spec(bc.shape), _full_spec(abar.shape), _full_spec(cc.shape)]
        + [_HBM] * nsh,
        out_specs=[pl.BlockSpec((tb, w2), lambda i: (i, 0)), pl.BlockSpec((tb, c), lambda i: (i, 0))] + [_HBM] * nsh,
        out_shape=[_sds((t, w2), BF16), _sds((t, c))] + [_sds((4,) + a.shape, a.dtype) for a in shards],
        scratch_shapes=[pltpu.VMEM((1, w2), F32), pltpu.VMEM((tb, w2), F32)] + (_gather_big_sems(nsh) if nsh else []),
        compiler_params=_cparams(("arbitrary",)),
    )(h, bc, abar, cc, *shards)
    return outs[0], outs[1], outs[2:]


def _s5_bwd(dy, s, abar, cc, bc, tb):
    t, c = dy.shape
    w2 = s.shape[1]
    w = w2 // 2
    kb, nb, nsb = _s5_blocks(c)
    nblk = t // tb
    pack = 16
    per = tb // pack
    nt = (((1,), (1,)), ((), ()))

    def body(dy_ref, s_ref, sprev_ref, a_ref, cc_ref, bc_ref, dbu_ref, dh_ref, da_ref, carry, da_acc, rows):
        i = pl.program_id(0)

        @pl.when(i == 0)
        def _():
            carry[...] = jnp.zeros_like(carry)
            da_acc[...] = jnp.zeros_like(da_acc)

        for j in range(w2 // nb):
            ch = (j % nsb) * kb
            rows[:, j * nb:(j + 1) * nb] = lax.dot_general(dy_ref[:, ch:ch + kb].astype(BF16), cc_ref[j * nb:(j + 1) * nb, :], nt,
                                                           preferred_element_type=F32)
        ar, ai = a_ref[:, :w], a_ref[:, w:]

        def step(n, state):
            gr, gi = state
            row = tb - 1 - n
            nr = rows[pl.ds(row, 1), :w] + ar * gr + ai * gi
            ni = rows[pl.ds(row, 1), w:] + ar * gi - ai * gr
            rows[pl.ds(row, 1), :w] = nr
            rows[pl.ds(row, 1), w:] = ni
            return nr, ni

        gr, gi = lax.fori_loop(0, tb, step, (carry[:, :w], carry[:, w:]))
        carry[:, :w] = gr
        carry[:, w:] = gi
        last = (lax.broadcasted_iota(jnp.int32, (pack, w2), 0) == pack - 1) & (i < nblk - 1)
        before = jnp.sum(jnp.where(last, sprev_ref[...].astype(F32), 0.0), axis=0, keepdims=True)
        rid = lax.broadcasted_iota(jnp.int32, (tb, w2), 0)
        sp = jnp.where(rid == 0, before, pltpu.roll(s_ref[...].astype(F32), 1, 0))
        g = rows[...]
        dbu_ref[...] = g.astype(BF16)
        spr, spi, g_r, g_i = sp[:, :w], sp[:, w:], g[:, :w], g[:, w:]
        da_acc[:, :w] += jnp.sum(spr * g_r + spi * g_i, axis=0, keepdims=True)
        da_acc[:, w:] += jnp.sum(spr * g_i - spi * g_r, axis=0, keepdims=True)
        for j in range(nsb):
            re, im = j * nb, w + j * nb
            dh_ref[:, j * kb:(j + 1) * kb] = (
                lax.dot_general(dbu_ref[:, re:re + nb], bc_ref[:, re:re + nb], nt, preferred_element_type=F32)
                + lax.dot_general(dbu_ref[:, im:im + nb], bc_ref[:, im:im + nb], nt, preferred_element_type=F32))

        @pl.when(i == nblk - 1)
        def _():
            da_ref[...] = da_acc[...]

    wide = pl.BlockSpec((tb, w2), lambda i: (nblk - 1 - i, 0))
    narrow = pl.BlockSpec((tb, c), lambda i: (nblk - 1 - i, 0))
    prev = pl.BlockSpec((pack, w2), lambda i: (jnp.maximum((nblk - 1 - i) * per - 1, 0), 0))
    return pl.pallas_call(
        body, name="s5_bwd", grid=(nblk,),
        in_specs=[narrow, wide, prev, _full_spec(abar.shape), _full_spec(cc.shape), _full_spec(bc.shape)],
        out_specs=[wide, narrow, pl.BlockSpec((1, w2), lambda i: (0, 0))],
        out_shape=[_sds((t, w2), BF16), _sds((t, c)), _sds((1, w2))],
        scratch_shapes=[pltpu.VMEM((1, w2), F32), pltpu.VMEM((1, w2), F32), pltpu.VMEM((tb, w2), F32)],
        compiler_params=_cparams(("arbitrary",)),
    )(dy, s, s, abar, cc, bc)


def _zoh_fwd(a_re, a_im, log_dt, b_re_t, b_im_t):
    def body(*refs):
        for r, val in zip(refs[5:], _f_zoh(*(x[...] for x in refs[:5]))):
            r[...] = val

    return pl.pallas_call(body, name="s5_zoh_fwd", out_shape=[_sds(a_re.shape)] * 2 + [_sds(b_re_t.shape)] * 2,
                          compiler_params=_cparams())(a_re, a_im, log_dt, b_re_t, b_im_t)


def _zoh_bwd(a_re, a_im, log_dt, b_re_t, b_im_t, couts):
    def body(*refs):
        _, vjp = jax.vjp(_f_zoh, *(x[...] for x in refs[:5]))
        for r, val in zip(refs[9:], vjp(tuple(x[...] for x in refs[5:9]))):
            r[...] = val

    ins = (a_re, a_im, log_dt, b_re_t, b_im_t)
    return pl.pallas_call(body, name="s5_zoh_bwd", out_shape=[_sds(x.shape) for x in ins],
                          compiler_params=_cparams())(*ins, *couts)


def _loss_head(h, target, tb):
    t, c = h.shape
    nb = t // tb

    def body(h_ref, t_ref, loss_ref, dh_ref, acc):
        i = pl.program_id(0)

        @pl.when(i == 0)
        def _():
            acc[...] = jnp.zeros_like(acc)

        d = h_ref[...] - t_ref[...]
        dh_ref[...] = d * (1.0 / c)
        acc[...] += 0.5 * jnp.sum(jnp.mean(d * d, axis=-1, keepdims=True), axis=0, keepdims=True)

        @pl.when(i == nb - 1)
        def _():
            loss_ref[...] = jnp.broadcast_to(acc[...], loss_ref.shape)

    row = pl.BlockSpec((tb, c), lambda i: (i, 0))
    return pl.pallas_call(
        body, name="loss_head", grid=(nb,), in_specs=[row, row],
        out_specs=[pl.BlockSpec((8, 128), lambda i: (0, 0)), row], out_shape=[_sds((8, 128)), _sds((t, c))],
        scratch_shapes=[pltpu.VMEM((1, 1), F32)], compiler_params=_cparams(("arbitrary",)),
    )(h, target)


def _rows_tile(rows):
    for cand in (512, 256, 128, 64, 32, 16, 8):
        if rows % cand == 0:
            return cand
    return rows


def _grad_x(here, from_next):
    rows, cols = here[0].shape
    tb = _rows_tile(rows)
    nb = rows // tb
    nh, nn = len(here), len(from_next)
    sub = 8

    def body(*refs):
        i = pl.program_id(0)
        total = functools.reduce(jnp.add, [r[...] for r in refs[:nh]])
        shifted = functools.reduce(jnp.add, [r[...] for r in refs[nh:nh + nn]])
        first_next = functools.reduce(jnp.add, [r[0:1, :] for r in refs[nh + nn:nh + 2 * nn]])
        first_next = jnp.where(i == nb - 1, 0.0, first_next)
        rid = lax.broadcasted_iota(jnp.int32, (tb, cols), 0)
        refs[-1][...] = total + jnp.where(rid == tb - 1, first_next, pltpu.roll(shifted, tb - 1, 0))

    blk = pl.BlockSpec((tb, cols), lambda i: (i, 0))
    nxt = pl.BlockSpec((sub, cols), lambda i: (jnp.minimum(i + 1, nb - 1) * (tb // sub), 0))
    return pl.pallas_call(body, name="grad_x", grid=(nb,), in_specs=[blk] * (nh + nn) + [nxt] * nn, out_specs=blk,
                          out_shape=_sds((rows, cols)), compiler_params=_cparams(("parallel",)))(*here, *from_next, *from_next)


def _adamw_math(w, g, m, v):
    m = ADAM_B1 * m + (1.0 - ADAM_B1) * g
    v = ADAM_B2 * v + (1.0 - ADAM_B2) * jnp.square(g)
    m_hat = m / (1.0 - ADAM_B1 ** ADAM_STEP)
    v_hat = v / (1.0 - ADAM_B2 ** ADAM_STEP)
    delta = -ADAM_LR * (m_hat / (jnp.sqrt(v_hat) + ADAM_EPS) + ADAM_WD * w)
    return delta, m, v


def _adamw(name, parts, w, m, v):
    rows, cols = w.shape
    tb = _rows_tile(rows)
    npart = len(parts)

    def body(*refs):
        g = refs[0][...]
        for r in refs[1:npart]:
            g = g + r[...]
        w_ref, m_ref, v_ref = refs[npart:npart + 3]
        g_out, d_out, m_out, v_out = refs[npart + 3:]
        delta, mn, vn = _adamw_math(w_ref[...], g, m_ref[...], v_ref[...])
        g_out[...] = g
        d_out[...] = delta
        m_out[...] = mn
        v_out[...] = vn

    blk = pl.BlockSpec((tb, cols), lambda i: (i, 0))
    return pl.pallas_call(body, name=name, grid=(rows // tb,), in_specs=[blk] * (npart + 3), out_specs=[blk] * 4,
                          out_shape=[_sds((rows, cols))] * 4, compiler_params=_cparams(("parallel",)))(*parts, w, m, v)


def _shift_down(a):
    return jnp.concatenate([jnp.zeros_like(a[:1]), a[:-1]], axis=0)


def _s5_pack_mask(g):
    return (jnp.arange(g)[None, :] % S5_PACK == jnp.arange(S5_PACK)[:, None]).astype(F32)


def _compact_b(bbar_t):
    g, s, p = bbar_t.shape
    return (_s5_pack_mask(g)[:, None, :, None] * bbar_t.transpose(1, 0, 2)[None]).reshape(S5_PACK * s, g * p)


def _compact_b_t(dense, g):
    s, p = dense.shape[0] // S5_PACK, dense.shape[1] // g
    return jnp.sum(dense.reshape(S5_PACK, s, g, p) * _s5_pack_mask(g)[:, None, :, None], axis=0).transpose(1, 0, 2)


def _compact_c(c_w):
    g, s, p = c_w.shape
    return (c_w.transpose(0, 2, 1)[:, :, None, :] * _s5_pack_mask(g).T[:, None, :, None]).reshape(g * p, S5_PACK * s)


def _compact_c_t(dense, g):
    p, s = dense.shape[0] // g, dense.shape[1] // S5_PACK
    return jnp.sum(dense.reshape(g, p, S5_PACK, s) * _s5_pack_mask(g).T[:, None, :, None], axis=2).transpose(0, 2, 1)


def _local_step(x, target, fw, core):
    t, c = x.shape
    nh = c // HEAD
    ng = c // SSM_GROUP
    tb = min(256, t)
    tbm = min(512, t)
    tbmb = min(512, t)
    tbs = min(256, t)
    tk5 = min(2048, t)
    hg = min(16, nh)
    mu = [fw['rw_mu'][i:i + 1] for i in range(6)]
    ln_g = [fw['ln_g'][i:i + 1] for i in range(4)]
    ln_b = [fw['ln_b'][i:i + 1] for i in range(4)]
    grads = {}

    xp = _shift_down(x)
    proj_params = {n: (mu[i], fw['rw_w' + n]) for n, i in (('r', 0), ('k', 2), ('v', 3))}
    raw = {n: _stage_fwd("proj_" + n, _f_proj, (x, xp), proj_params[n], (c,), tb)[0] for n in 'rkv'}
    lora_params = (mu[1], mu[4], mu[5], fw['rw_w0'], fw['rw_w1'], fw['rw_w2'], fw['rw_a0'], fw['rw_a1'], fw['rw_a2'],
                   fw['rw_g1'], fw['rw_g2'], fw['rw_k_k'], fw['rw_k_a'])
    lw, k2, an, bb, gate = _stage_fwd("lora", _f_lora, (x, xp, raw['k']), lora_params, (c,) * 5, tb)
    rec_in = (raw['r'], lw, k2, raw['v'], an, bb)
    o, s0s, (wo_view, w1_l0, w2_l0) = _rec_fwd(*rec_in, hg, fw['late_a'])
    fw = dict(fw, rw_wo=wo_view.reshape(c, c))
    mlp_w = [(w1_l0.reshape(4, 1, c, -1), w2_l0.reshape(4, 1, -1, c)), None]
    post_params = (fw['rw_lnx_g'], fw['rw_lnx_b'], fw['rw_r_k'], fw['rw_wo'], ln_g[0], ln_b[0])
    post_acts = (o, raw['r'], k2, raw['v'], gate, x)
    h1, = _stage_fwd("post", _f_post, post_acts, post_params, (c,), tb)
    h2, s_mlp0, (w1_l1,) = _mlp_fwd("mlp0_fwd", h1, *mlp_w[0], 0, ln_g[1], ln_b[1], tbm, fw['late_b'])

    a_re, a_im, log_dt = fw['s5_a_re'], fw['s5_a_im'], fw['s5_log_dt']
    b_re_t, b_im_t = fw['s5_b_re'].transpose(0, 2, 1), fw['s5_b_im'].transpose(0, 2, 1)
    abar_re, abar_im, bbar_re_t, bbar_im_t = _zoh_fwd(a_re, a_im, log_dt, b_re_t, b_im_t)
    abar = jnp.concatenate([abar_re.reshape(1, -1), abar_im.reshape(1, -1)], axis=1)
    bc = jnp.concatenate([_compact_b(bbar_re_t), _compact_b(bbar_im_t)], axis=1).astype(BF16)
    cc = jnp.concatenate([_compact_c(fw['s5_c_re']), -_compact_c(fw['s5_c_im'])], axis=0).astype(BF16)
    st, ys, (glu_view, w2_l1) = _s5_fwd(h2, bc, abar, cc, tbs, fw['late_c'])
    mlp_w[1] = (w1_l1.reshape(4, 1, c, -1), w2_l1.reshape(4, 1, -1, c))
    fw = dict(fw, s5_w_glu=tuple(glu_view[q] for q in range(4)))
    glu_params = (fw['s5_d'], *fw['s5_w_glu'], ln_g[2], ln_b[2])
    h3, = _stage_fwd("glu", _f_glu, (ys, h2), glu_params, (c,), tb)
    h4, s_mlp1, _ = _mlp_fwd("mlp1_fwd", h3, *mlp_w[1], 0, ln_g[3], ln_b[3], tbm)

    loss_blk, dh4 = _loss_head(h4, target, tb)

    dln_g, dln_b = [None] * 4, [None] * 4
    dh3, ds1, dhid1, act1, dln_g[3], dln_b[3] = _mlp_bwd("mlp1_bwd", h3, s_mlp1, dh4, *mlp_w[1], 0,
                                                         ln_g[3], ln_b[3], tbmb)
    dw1 = _mlp_weight_grad("mlp1_dw1", h3, dhid1, 1, DEPTH, "n")
    dw2 = _mlp_weight_grad("mlp1_dw2", act1, ds1, 1, DEPTH, "m")
    (dys, dh2_glu), (grads['s5_d'], *dglu, dln_g[2], dln_b[2]) = _stage_bwd(
        "glu_bwd", _f_glu, (ys, h2), glu_params, (dh3,), tb, proxied=(1, 2, 3, 4))
    grads['s5_w_glu'] = jnp.stack(dglu)
    dcc = _s5_weight_grad("s5_dcc", dys, st, True, tk5)
    dbu, dh2_bu, dabar = _s5_bwd(dys, st, abar, cc, bc, tbs)
    dbc = _s5_weight_grad("s5_dbc", h2, dbu, False, tk5)
    gp = ng * SSM_STATE
    grads['s5_c_re'] = _compact_c_t(dcc[:gp], ng)
    grads['s5_c_im'] = -_compact_c_t(dcc[gp:], ng)
    zoh_couts = (dabar[:, :gp].reshape(ng, SSM_STATE), dabar[:, gp:].reshape(ng, SSM_STATE),
                 _compact_b_t(dbc[:, :gp], ng), _compact_b_t(dbc[:, gp:], ng))
    grads['s5_a_re'], grads['s5_a_im'], grads['s5_log_dt'], db_re_t, db_im_t = _zoh_bwd(
        a_re, a_im, log_dt, b_re_t, b_im_t, zoh_couts)
    grads['s5_b_re'], grads['s5_b_im'] = db_re_t.transpose(0, 2, 1), db_im_t.transpose(0, 2, 1)
    dh2 = (dh2_glu, dh2_bu)

    dh1, ds0, dhid0, act0, dln_g[1], dln_b[1] = _mlp_bwd("mlp0_bwd", h1, s_mlp0, dh2, *mlp_w[0], 0,
                                                         ln_g[1], ln_b[1], tbmb)
    grads['mlp_w1'] = _mlp_weight_grad("mlp0_dw1", h1, dhid0, 0, DEPTH, "n", into=dw1)
    grads['mlp_w2'] = _mlp_weight_grad("mlp0_dw2", act0, ds0, 0, DEPTH, "m", into=dw2)
    ready_views = [grads[n].reshape(4, -1, grads[n].shape[-1]) for n in BIG_READY]
    (do, dr_p, dk2_p, dv_p, dgate, dx_post), post_g, ready_others = _stage_bwd(
        "post_bwd", _f_post, post_acts, post_params, (dh1,), tb, proxied=(3,), halves_of=ready_views)
    grads['rw_lnx_g'], grads['rw_lnx_b'], grads['rw_r_k'], grads['rw_wo'], dln_g[0], dln_b[0] = post_g
    ready_sums = [_half_add(f"half_add_a{i}", v, o, core) for i, (v, o) in enumerate(zip(ready_views, ready_others))]
    rec_g, ready_lands = _rec_bwd(*rec_in, s0s, do, hg, ready_sums)
    reduced = dict(zip(BIG_READY, zip(ready_sums, ready_lands)))
    dr_r, dlw, dk2_r, dv_r, dan, dbb = rec_g
    dk2 = (dk2_p, dk2_r)
    (dx_l, dxp_l, dkraw_l), lora_g = _stage_bwd("lora_bwd", _f_lora, (x, xp, raw['k']), lora_params,
                                                (dlw, dk2, dan, dbb, dgate), tb)
    (dmu_w, dmu_a, dmu_g, grads['rw_w0'], grads['rw_w1'], grads['rw_w2'], grads['rw_a0'], grads['rw_a1'], grads['rw_a2'],
     grads['rw_g1'], grads['rw_g2'], grads['rw_k_k'], grads['rw_k_a']) = lora_g
    dproj = {'r': (dr_p, dr_r), 'k': dkraw_l, 'v': (dv_p, dv_r)}
    dxs, dxps, dmu = [dx_post, dx_l], [dxp_l], {}
    for n in 'rkv':
        (dx_n, dxp_n), (dmu[n], grads['rw_w' + n]) = _stage_bwd("proj_bwd_" + n, _f_proj, (x, xp), proj_params[n],
                                                                 (dproj[n],), tb, proxied=(1,))
        dxs.append(dx_n)
        dxps.append(dxp_n)
    grads['rw_mu'] = jnp.concatenate([dmu['r'], dmu_w, dmu['k'], dmu['v'], dmu_a, dmu_g], axis=0)
    grads['ln_g'] = jnp.concatenate(dln_g, axis=0)
    grads['ln_b'] = jnp.concatenate(dln_b, axis=0)
    grad_x = _grad_x(dxs, dxps)
    late = [n for n in BIG if n not in BIG_READY]
    late_sums = dict(zip(late, _chip_sums("b", [grads[n] for n in late], core)))
    return loss_blk, grad_x, grads, reduced, late_sums


def _position():
    return lax.axis_index("x"), lax.axis_index("y"), lax.axis_index("c")


def _other_chips(x, y):
    return [(1 - x, y), (x, 1 - y), (1 - x, 1 - y)]


def _chip_slice(ref, axis, q, size):
    idx = [slice(None)] * len(ref.shape)
    idx[axis] = pl.ds(pl.multiple_of(q * size, size), size)
    return ref.at[tuple(idx)]


_HBM = pl.BlockSpec(memory_space=pltpu.HBM)


def _gather_small_phases(src, dst, axes, sems):
    n = len(src)
    send_sems, recv_sems, own_sems = sems
    x, y, c = _position()
    chips = _other_chips(x, y)
    sizes = [src[a].shape[axes[a]] for a in range(n)]

    def copy(a, k, q):
        return pltpu.make_async_remote_copy(
            src_ref=src[a], dst_ref=_chip_slice(dst[a], axes[a], q, sizes[a]), send_sem=send_sems.at[a, k],
            recv_sem=recv_sems.at[a, k], device_id=(*chips[k], c), device_id_type=MESH)

    def own(a):
        return pltpu.make_async_copy(src[a], _chip_slice(dst[a], axes[a], 2 * x + y, sizes[a]), own_sems.at[a])

    def start():
        for a in range(n):
            own(a).start()
            for k in range(3):
                copy(a, k, 2 * x + y).start()

    def finish():
        for a in range(n):
            for k, (cx, cy) in enumerate(chips):
                copy(a, k, 2 * cx + cy).wait_recv()
        for a in range(n):
            for k in range(3):
                copy(a, k, 2 * x + y).wait_send()
            own(a).wait()

    return start, finish


def _gather_early(big, small, axes):
    nb, ns = len(big), len(small)
    full_shapes = [tuple(s * 4 if i == ax else s for i, s in enumerate(a.shape)) for a, ax in zip(small, axes)]

    def body(*refs):
        src_b, src_s = refs[:nb], refs[nb:nb + ns]
        dst_b, dst_s = refs[nb + ns:2 * nb + ns], refs[2 * nb + ns:2 * (nb + ns)]
        sems = refs[2 * (nb + ns):]
        small_start, small_finish = _gather_small_phases(src_s, dst_s, axes, sems[5:])
        small_start()
        for phase in _gather_big_phases(src_b, dst_b, sems[:5]):
            phase()
        small_finish()

    outs = pl.pallas_call(
        body, name="gather_early", in_specs=[_HBM] * (nb + ns), out_specs=[_HBM] * (nb + ns),
        out_shape=[_sds((4,) + a.shape, a.dtype) for a in big] + [_sds(s, a.dtype) for s, a in zip(full_shapes, small)],
        scratch_shapes=_gather_big_sems(nb) + [pltpu.SemaphoreType.DMA((ns, 3)), pltpu.SemaphoreType.DMA((ns, 3)),
                                               pltpu.SemaphoreType.DMA((ns,))],
        compiler_params=_cparams(),
    )(*big, *small)
    return outs[:nb], outs[nb:]


def _scatter_pieces(fulls, axes, sums):
    n, nsum = len(fulls), len(sums)
    sizes = [a.shape[ax] // 4 for a, ax in zip(fulls, axes)]
    shard_shapes = [tuple(sz if i == ax else s for i, s in enumerate(a.shape)) for a, ax, sz in zip(fulls, axes, sizes)]

    def body(*refs):
        src, big_src = refs[:n], refs[n:n + nsum]
        land, big_land = refs[n + nsum:2 * n + nsum], refs[2 * n + nsum:2 * (n + nsum)]
        send_sems, recv_sems = refs[2 * (n + nsum):2 * (n + nsum) + 2]
        big_start, big_finish = _scatter_big_phases(big_src, big_land, refs[2 * (n + nsum) + 2:])
        big_start()
        x, y, c = _position()
        chips = _other_chips(x, y)

        def copy(a, k):
            cx, cy = chips[k]
            return pltpu.make_async_remote_copy(
                src_ref=_chip_slice(src[a], axes[a], 2 * cx + cy, sizes[a]), dst_ref=land[a].at[k],
                send_sem=send_sems.at[a, k], recv_sem=recv_sems.at[a, k], device_id=(cx, cy, c), device_id_type=MESH)

        for a in range(n):
            for k in range(3):
                copy(a, k).start()
        for a in range(n):
            for k in range(3):
                copy(a, k).wait_recv()
        for a in range(n):
            for k in range(3):
                copy(a, k).wait_send()
        big_finish()

    outs = pl.pallas_call(
        body, name="scatter_grads", in_specs=[_HBM] * (n + nsum), out_specs=[_HBM] * (n + nsum),
        out_shape=[_sds((3,) + s) for s in shard_shapes] + [_sds((3,) + s.shape[1:], s.dtype) for s in sums],
        scratch_shapes=[pltpu.SemaphoreType.DMA((n, 3)), pltpu.SemaphoreType.DMA((n, 3))] + _scatter_big_sems(nsum),
        compiler_params=_cparams(),
    )(*fulls, *sums)
    return outs[:n], outs[n:]


def _sibling_swap(name, arrs):
    n = len(arrs)

    def body(*refs):
        src, dst = refs[:n], refs[n:2 * n]
        send_sems, recv_sems = refs[2 * n:]
        x, y, c = _position()
        copies = [pltpu.make_async_remote_copy(src_ref=src[a], dst_ref=dst[a], send_sem=send_sems.at[a], recv_sem=recv_sems.at[a],
                                               device_id=(x, y, 1 - c), device_id_type=MESH) for a in range(n)]
        for cp in copies:
            cp.start()
        for cp in copies:
            cp.wait_recv()
        for cp in copies:
            cp.wait_send()

    return pl.pallas_call(
        body, name=name, in_specs=[_HBM] * n, out_specs=[_HBM] * n, out_shape=[_sds(a.shape) for a in arrs],
        scratch_shapes=[pltpu.SemaphoreType.DMA((n,)), pltpu.SemaphoreType.DMA((n,))],
        compiler_params=_cparams(),
    )(*arrs)


def _sum4(name, own, land):
    rows, cols = own.shape
    tb = _rows_tile(rows)

    def body(o_ref, l0, l1, l2, out_ref):
        out_ref[...] = ((o_ref[...] + l0[...]) + l1[...]) + l2[...]

    blk = pl.BlockSpec((tb, cols), lambda i: (i, 0))
    lands = [pl.BlockSpec((None, tb, cols), functools.partial(lambda k, i: (k, i, 0), k)) for k in range(3)]
    return pl.pallas_call(body, name=name, grid=(rows // tb,), in_specs=[blk] + lands, out_specs=blk,
                          out_shape=_sds((rows, cols)), compiler_params=_cparams(("parallel",)))(own, land, land, land)


def _allreduce_adamw_small(g, w, m, v):
    rows, lanes = g.shape

    def body(g_ref, w_ref, m_ref, v_ref, gs_ref, d_ref, mn_ref, vn_ref, land, send_sems, recv_sems):
        x, y, c = _position()
        me = 4 * x + 2 * y + c
        masks = [(bx, by, bc) for bx in (0, 1) for by in (0, 1) for bc in (0, 1)][1:]

        def peer(mask):
            return (x ^ mask[0], y ^ mask[1], c ^ mask[2])

        def copy(j, slot):
            return pltpu.make_async_remote_copy(src_ref=g_ref, dst_ref=land.at[slot], send_sem=send_sems.at[j],
                                                recv_sem=recv_sems.at[j], device_id=peer(masks[j]), device_id_type=MESH)

        for j in range(7):
            copy(j, me).start()
        land[me] = g_ref[...]
        for j in range(7):
            px, py, pc = peer(masks[j])
            copy(j, 4 * px + 2 * py + pc).wait_recv()
        for j in range(7):
            copy(j, me).wait_send()
        total = land[0]
        for dev in range(1, 8):
            total = total + land[dev]
        delta, mn, vn = _adamw_math(w_ref[...], total, m_ref[...], v_ref[...])
        gs_ref[...] = total
        d_ref[...] = delta
        mn_ref[...] = mn
        vn_ref[...] = vn

    vmem = pl.BlockSpec(memory_space=pltpu.VMEM)
    return pl.pallas_call(
        body, name="allreduce_adamw_small", in_specs=[vmem] * 4, out_specs=[vmem] * 4, out_shape=[_sds((rows, lanes))] * 4,
        scratch_shapes=[pltpu.VMEM((8, rows, lanes), F32), pltpu.SemaphoreType.DMA((7,)), pltpu.SemaphoreType.DMA((7,))],
        compiler_params=_cparams(),
    )(g, w, m, v)


def _row_half(ref, c):
    r2 = ref.shape[-2] // 2
    lead = (slice(None),) * (len(ref.shape) - 2)
    return ref.at[(*lead, pl.ds(pl.multiple_of(c * r2, r2), r2), slice(None))]


def _gather_big_phases(src, dst, sems):
    n = len(src)
    ici_send, ici_recv, d2d_send, d2d_recv, own_sems = sems
    x, y, c = _position()
    me = 2 * x + y
    chips = _other_chips(x, y)
    ids = [2 * cx + cy for cx, cy in chips]

    def ici(a, k, q):
        return pltpu.make_async_remote_copy(
            src_ref=_row_half(src[a], c), dst_ref=_row_half(dst[a].at[q], c), send_sem=ici_send.at[a, k],
            recv_sem=ici_recv.at[a, k], device_id=(*chips[k], c), device_id_type=MESH)

    def d2d(a, k, half):
        where = _row_half(dst[a].at[ids[k]], half)
        return pltpu.make_async_remote_copy(src_ref=where, dst_ref=where, send_sem=d2d_send.at[a, k], recv_sem=d2d_recv.at[a, k],
                                            device_id=(x, y, 1 - c), device_id_type=MESH)

    def own(a):
        return pltpu.make_async_copy(src[a], dst[a].at[me], own_sems.at[a])

    def start():
        for a in range(n):
            own(a).start()
            for k in range(3):
                ici(a, k, me).start()

    def forward():
        for a in range(n):
            for k in range(3):
                ici(a, k, ids[k]).wait_recv()
                d2d(a, k, c).start()

    def finish():
        for a in range(n):
            for k in range(3):
                d2d(a, k, 1 - c).wait_recv()
        for a in range(n):
            for k in range(3):
                ici(a, k, me).wait_send()
                d2d(a, k, c).wait_send()
            own(a).wait()

    return start, forward, finish


def _gather_big_sems(n):
    return [pltpu.SemaphoreType.DMA((n, 3))] * 4 + [pltpu.SemaphoreType.DMA((n,))]


def _chip_sums(tag, grads, core):
    views = [g.reshape(4, -1, g.shape[-1]) for g in grads]
    others = _sibling_halves("sibling_halves_" + tag, views)
    return [_half_add(f"half_add_{tag}{i}", v, o, core) for i, (v, o) in enumerate(zip(views, others))]


def _sibling_halves_phases(src, dst, sems):
    n = len(src)
    send_sems, recv_sems = sems
    x, y, c = _position()

    def copy(a):
        return pltpu.make_async_remote_copy(src_ref=_row_half(src[a], 1 - c), dst_ref=dst[a], send_sem=send_sems.at[a],
                                            recv_sem=recv_sems.at[a], device_id=(x, y, 1 - c), device_id_type=MESH)

    def start():
        for a in range(n):
            copy(a).start()

    def finish():
        for a in range(n):
            copy(a).wait_recv()
        for a in range(n):
            copy(a).wait_send()

    return start, finish


def _sibling_halves_sems(n):
    return [pltpu.SemaphoreType.DMA((n,)), pltpu.SemaphoreType.DMA((n,))]


def _sibling_halves_shapes(views):
    return [_sds((4, v.shape[1] // 2, v.shape[2])) for v in views]


def _sibling_halves(name, views):
    n = len(views)

    def body(*refs):
        for phase in _sibling_halves_phases(refs[:n], refs[n:2 * n], refs[2 * n:]):
            phase()

    return pl.pallas_call(
        body, name=name, in_specs=[_HBM] * n, out_specs=[_HBM] * n, out_shape=_sibling_halves_shapes(views),
        scratch_shapes=_sibling_halves_sems(n), compiler_params=_cparams(),
    )(*views)


def _rows_tile_capped(rows, cap=256):
    return min(_rows_tile(rows), cap)


def _half_add(name, view, other, core):
    _, r, k = view.shape
    r2 = r // 2
    tr = _rows_tile_capped(r2)
    per = r2 // tr

    def body(c_ref, v_ref, o_ref, out_ref):
        out_ref[...] = (v_ref[...] + o_ref[...]).astype(BF16)

    blk = pl.BlockSpec((None, tr, k), lambda q, i, c: (q, i, 0))
    return pl.pallas_call(
        body, name=name,
        grid_spec=pltpu.PrefetchScalarGridSpec(
            num_scalar_prefetch=1, grid=(4, per),
            in_specs=[pl.BlockSpec((None, tr, k), lambda q, i, c: (q, c[0] * per + i, 0)), blk], out_specs=blk),
        out_shape=_sds((4, r2, k), BF16), compiler_params=_cparams(("parallel", "parallel")),
    )(core, view, other)


def _scatter_big_phases(src, land, sems):
    n = len(src)
    send_sems, recv_sems = sems
    x, y, c = _position()
    chips = _other_chips(x, y)

    def copy(a, k):
        cx, cy = chips[k]
        return pltpu.make_async_remote_copy(src_ref=src[a].at[2 * cx + cy], dst_ref=land[a].at[k], send_sem=send_sems.at[a, k],
                                            recv_sem=recv_sems.at[a, k], device_id=(cx, cy, c), device_id_type=MESH)

    def start():
        for a in range(n):
            for k in range(3):
                copy(a, k).start()

    def finish():
        for a in range(n):
            for k in range(3):
                copy(a, k).wait_recv()
        for a in range(n):
            for k in range(3):
                copy(a, k).wait_send()

    return start, finish


def _scatter_big_sems(n):
    return [pltpu.SemaphoreType.DMA((n, 3)), pltpu.SemaphoreType.DMA((n, 3))]


def _sum4_big(name, sums, land, chip):
    _, r2, k = sums.shape
    tr = _rows_tile_capped(r2)

    def body(q_ref, s_ref, l0, l1, l2, out_ref):
        out_ref[...] = ((s_ref[...].astype(F32) + l0[...].astype(F32)) + l1[...].astype(F32)) + l2[...].astype(F32)

    lands = [pl.BlockSpec((None, tr, k), functools.partial(lambda j, i, q: (j, i, 0), j)) for j in range(3)]
    return pl.pallas_call(
        body, name=name,
        grid_spec=pltpu.PrefetchScalarGridSpec(
            num_scalar_prefetch=1, grid=(r2 // tr,),
            in_specs=[pl.BlockSpec((None, tr, k), lambda i, q: (q[0], i, 0))] + lands,
            out_specs=pl.BlockSpec((tr, k), lambda i, q: (i, 0))),
        out_shape=_sds((r2, k)), compiler_params=_cparams(("parallel",)),
    )(chip, sums, land, land, land)


def _adamw_halves(name, mine, theirs, w, m, v, core):
    r, k = w.shape
    r2 = r // 2
    tr = _rows_tile_capped(r2, 512)
    per = r2 // tr

    def body(c_ref, mine_ref, theirs_ref, w_ref, m_ref, v_ref, g_out, d_out, m_out, v_out):
        g = jnp.where(pl.program_id(0) == c_ref[0], mine_ref[...], theirs_ref[...])
        delta, mn, vn = _adamw_math(w_ref[...], g, m_ref[...], v_ref[...])
        g_out[...] = g
        d_out[...] = delta
        m_out[...] = mn
        v_out[...] = vn

    half = pl.BlockSpec((tr, k), lambda h, i, c: (i, 0))
    full = pl.BlockSpec((tr, k), lambda h, i, c: (h * per + i, 0))
    return pl.pallas_call(
        body, name=name,
        grid_spec=pltpu.PrefetchScalarGridSpec(num_scalar_prefetch=1, grid=(2, per), in_specs=[half, half, full, full, full],
                                               out_specs=[full] * 4),
        out_shape=[_sds((r, k))] * 4, compiler_params=_cparams(("parallel", "parallel")),
    )(core, mine, theirs, w, m, v)


def _drops_layer_axis(name):
    return not (name.startswith('mlp') or name == 's5_d')


def _work(name, arr):
    return arr.reshape(arr.shape[1:]) if _drops_layer_axis(name) else arr


def _work_axis(name):
    return SHARD_AXIS[name] - (1 if _drops_layer_axis(name) else 0)


def _as2d(a):
    return a.reshape(-1, a.shape[-1])


def _replicated_2d(name, arr):
    if name in ('ln_g', 'ln_b'):
        return arr
    if name == 'rw_r_k':
        return arr.reshape(1, -1)
    if name == 's5_log_dt':
        return arr.reshape(-1, 1)
    if name.startswith('s5_'):
        return arr.reshape(arr.shape[1:])
    return arr


def _pack(arrs):
    flat = []
    for a in arrs:
        f = a.reshape(-1)
        flat.append(jnp.pad(f, (0, -f.shape[0] % 128)))
    f = jnp.concatenate(flat)
    f = jnp.pad(f, (0, -f.shape[0] % 1024))
    return f.reshape(-1, 128)


def _unpack(packed, shapes):
    flat = packed.reshape(-1)
    out, at = [], 0
    for s in shapes:
        size = math.prod(s)
        out.append(flat[at:at + size].reshape(s))
        at += size + (-size % 128)
    return out


def kernel(x, ln_g, ln_b, rw_mu, rw_w0, rw_w1, rw_w2, rw_a0, rw_a1, rw_a2, rw_g1, rw_g2, rw_k_k, rw_k_a, rw_r_k, rw_wr, rw_wk, rw_wv, rw_wo, rw_lnx_g, rw_lnx_b, s5_a_re, s5_a_im, s5_log_dt, s5_b_re, s5_b_im, s5_c_re, s5_c_im, s5_d, s5_w_glu, mlp_w1, mlp_w2, loss_target, m_ln_g, m_ln_b, m_rw_mu, m_rw_w0, m_rw_w1, m_rw_w2, m_rw_a0, m_rw_a1, m_rw_a2, m_rw_g1, m_rw_g2, m_rw_k_k, m_rw_k_a, m_rw_r_k, m_rw_wr, m_rw_wk, m_rw_wv, m_rw_wo, m_rw_lnx_g, m_rw_lnx_b, m_s5_a_re, m_s5_a_im, m_s5_log_dt, m_s5_b_re, m_s5_b_im, m_s5_c_re, m_s5_c_im, m_s5_d, m_s5_w_glu, m_mlp_w1, m_mlp_w2, v_ln_g, v_ln_b, v_rw_mu, v_rw_w0, v_rw_w1, v_rw_w2, v_rw_a0, v_rw_a1, v_rw_a2, v_rw_g1, v_rw_g2, v_rw_k_k, v_rw_k_a, v_rw_r_k, v_rw_wr, v_rw_wk, v_rw_wv, v_rw_wo, v_rw_lnx_g, v_rw_lnx_b, v_s5_a_re, v_s5_a_im, v_s5_log_dt, v_s5_b_re, v_s5_b_im, v_s5_c_re, v_s5_c_im, v_s5_d, v_s5_w_glu, v_mlp_w1, v_mlp_w2):
    d = dict(locals())
    x_pos, y_pos, c_pos = _position()
    chip = 2 * x_pos + y_pos
    chip_arr = jnp.reshape(chip, (1,)).astype(jnp.int32)
    core_arr = jnp.reshape(c_pos, (1,)).astype(jnp.int32)

    small = [n for n in SHARD_AXIS if n not in BIG]
    axes = [_work_axis(n) for n in small]
    big_views, small_fulls = _gather_early([_as2d(d[n]).astype(BF16) for n in BIG_EARLY], [_work(n, d[n]) for n in small], axes)
    views = dict(zip(BIG_EARLY, big_views))
    fw = dict(zip(small, small_fulls))
    c_model = d['x'].shape[-1]
    for n in BIG_EARLY:
        fw[n] = views[n].reshape(c_model, c_model)
    w1_layers, w2_layers = d['mlp_w1'].astype(BF16), d['mlp_w2'].astype(BF16)
    fw['late_a'] = [_as2d(d['rw_wo']).astype(BF16), w1_layers[0], w2_layers[0]]
    fw['late_b'] = [w1_layers[1]]
    fw['late_c'] = [_as2d(d['s5_w_glu']).astype(BF16), w2_layers[1]]
    for n in REPLICATED:
        fw[n] = _replicated_2d(n, d[n])

    loss_blk, grad_x, grads, reduced, late_sums = _local_step(d['x'][0], d['loss_target'][0], fw, core_arr)
    loss = lax.psum(loss_blk[0, 0], ('x', 'y', 'c'))
    out = {}

    pieces = [grads[n] for n in small]
    lands, late_lands = _scatter_pieces(pieces, axes, list(late_sums.values()))
    reduced.update(zip(late_sums, zip(late_sums.values(), late_lands)))
    mine = [_sum4_big("sum4_" + n, *reduced[n], chip_arr) for n in BIG]
    for n, g, ax, land in zip(small, pieces, axes, lands):
        size = g.shape[ax] // 4
        mine.append(_sum4("sum4_" + n, lax.dynamic_slice_in_dim(g, chip * size, size, ax), land))
    theirs = _sibling_swap("swap_sums", mine)
    for n, a, b in zip(BIG + small, mine, theirs):
        w2d, m2d, v2d = _as2d(d[n]), _as2d(d['m_' + n]), _as2d(d['v_' + n])
        res = (_adamw_halves("adamw_" + n, a, b, w2d, m2d, v2d, core_arr) if n in BIG
               else _adamw("adamw_" + n, (a, b), w2d, m2d, v2d))
        out[n] = [r.reshape(d[n].shape) for r in res]

    rep_shapes = [d[n].shape for n in REPLICATED]
    packs = [_pack([grads[n] for n in REPLICATED])] + [_pack([d[p + n] for n in REPLICATED]) for p in ('', 'm_', 'v_')]
    res = [_unpack(p, rep_shapes) for p in _allreduce_adamw_small(*packs)]
    for i, n in enumerate(REPLICATED):
        out[n] = [r[i] for r in res]

    grad_x = grad_x.reshape(d['x'].shape)
    return (loss, grad_x, *[out[n][0] for n in WEIGHTS], *[out[n][1] for n in WEIGHTS],
            *[out[n][2] for n in WEIGHTS], *[out[n][3] for n in WEIGHTS])
```

```python
import functools
import math

import jax
import jax.numpy as jnp
from jax import lax
from jax.experimental import pallas as pl
from jax.experimental.pallas import tpu as pltpu

F32 = jnp.float32
BF16 = jnp.bfloat16
MESH = pl.DeviceIdType.MESH

HEAD = 64
SSM_GROUP = 16
SSM_STATE = 64
GN_EPS = 64e-5
LN_EPS = 1e-5
DEPTH = 2
DN_ALPHA = (2.0 * DEPTH) ** 0.25
ADAM_LR, ADAM_B1, ADAM_B2, ADAM_EPS, ADAM_WD, ADAM_STEP = 0.001, 0.9, 0.999, 1e-08, 0.01, 10
REC_CHUNK = 64
V7X_VMEM_BYTES = 64 * 2 ** 20
VMEM_LIMIT = V7X_VMEM_BYTES - 8 * 2 ** 20

WEIGHTS = ['ln_g', 'ln_b', 'rw_mu', 'rw_w0', 'rw_w1', 'rw_w2', 'rw_a0', 'rw_a1', 'rw_a2', 'rw_g1', 'rw_g2',
           'rw_k_k', 'rw_k_a', 'rw_r_k', 'rw_wr', 'rw_wk', 'rw_wv', 'rw_wo', 'rw_lnx_g', 'rw_lnx_b',
           's5_a_re', 's5_a_im', 's5_log_dt', 's5_b_re', 's5_b_im', 's5_c_re', 's5_c_im', 's5_d', 's5_w_glu',
           'mlp_w1', 'mlp_w2']
SHARD_AXIS = {'rw_mu': 2, 'rw_w1': 1, 'rw_w2': 2, 'rw_a1': 1, 'rw_a2': 2, 'rw_g1': 1, 'rw_g2': 2,
              'rw_wr': 1, 'rw_wk': 1, 'rw_wv': 1, 'rw_wo': 1, 's5_d': 1, 's5_w_glu': 2, 'mlp_w1': 2, 'mlp_w2': 1}
REPLICATED = [n for n in WEIGHTS if n not in SHARD_AXIS]
BIG_EARLY = ['rw_wr', 'rw_wk', 'rw_wv']
BIG_LATE = ['rw_wo', 's5_w_glu', 'mlp_w1', 'mlp_w2']
BIG = BIG_EARLY + BIG_LATE
BIG_READY = ['s5_w_glu', 'mlp_w1', 'mlp_w2']


def _sds(shape, dtype=F32):
    return jax.ShapeDtypeStruct(tuple(shape), dtype)


def _cparams(sem=None, **kw):
    if sem is not None:
        kw["dimension_semantics"] = sem
    return pltpu.CompilerParams(vmem_limit_bytes=VMEM_LIMIT, **kw)


def _mm_products(a, b, g):
    gb = g.astype(BF16)
    da = lax.dot_general(gb, b.astype(BF16), (((1,), (1,)), ((), ())), preferred_element_type=F32)
    db = lax.dot_general(a.astype(BF16), gb, (((0,), (0,)), ((), ())), preferred_element_type=F32)
    return da, db


@jax.custom_vjp
def _mm_plain(a, b):
    return jnp.dot(a.astype(BF16), b.astype(BF16), preferred_element_type=F32)


def _mm_plain_bwd(res, g):
    da, db = _mm_products(*res, g)
    return da.astype(res[0].dtype), db.astype(res[1].dtype)


_mm_plain.defvjp(lambda a, b: (_mm_plain(a, b), (a, b)), _mm_plain_bwd)


@jax.custom_vjp
def _mm_proxy(a, b, z):
    return jnp.dot(a.astype(BF16), b.astype(BF16), preferred_element_type=F32)


def _mm_proxy_bwd(res, g):
    da, db = _mm_products(*res, g)
    return da.astype(res[0].dtype), jnp.zeros_like(res[1]), db


_mm_proxy.defvjp(lambda a, b, z: (_mm_proxy(a, b, z), (a, b)), _mm_proxy_bwd)


def mm(a, b, z=None):
    return _mm_plain(a, b) if z is None else _mm_proxy(a, b, z)


def _split3(x):
    hi = x.astype(BF16)
    r1 = x - hi.astype(F32)
    mid = r1.astype(BF16)
    lo = (r1 - mid.astype(F32)).astype(BF16)
    return hi, mid, lo


def _head_sum_impl(x):
    c = x.shape[1]
    lanes = 128
    sel = (lax.broadcasted_iota(jnp.int32, (c, lanes), 0) // HEAD
           == lax.broadcasted_iota(jnp.int32, (c, lanes), 1)).astype(BF16)
    s = sum(jnp.dot(p, sel, preferred_element_type=F32) for p in _split3(x))
    return sum(lax.dot_general(p, sel, (((1,), (1,)), ((), ())), preferred_element_type=F32) for p in _split3(s))


@jax.custom_vjp
def head_sum(x):
    return _head_sum_impl(x)


head_sum.defvjp(lambda x: (_head_sum_impl(x), None), lambda _, g: (_head_sum_impl(g),))


def _ln(x, g, b):
    mu = jnp.mean(x, axis=-1, keepdims=True)
    xc = x - mu
    var = jnp.mean(xc * xc, axis=-1, keepdims=True)
    return xc * lax.rsqrt(var + LN_EPS) * g + b


def _f_proj(acts, params, proxies):
    x, xp = acts
    mu, w = params
    return (mm(x + (xp - x) * mu, w, proxies[1]),)


def _f_lora(acts, params, proxies):
    x, xp, kraw = acts
    mu_w, mu_a, mu_g, w0, w1, w2, a0, a1, a2, g1, g2, k_k, k_a = params
    xx = xp - x
    w_pre = w0 + mm(jnp.tanh(mm(x + xx * mu_w, w1)), w2)
    z = -w_pre
    softplus = jnp.maximum(z, 0.0) + jnp.log(1.0 + jnp.exp(-jnp.abs(z)))
    log_decay = -jnp.exp(-softplus - 0.5)
    a = jax.nn.sigmoid(a0 + mm(mm(x + xx * mu_a, a1), a2))
    g = mm(jax.nn.sigmoid(mm(x + xx * mu_g, g1)), g2)
    kk = kraw * k_k
    kkn = kk / jnp.maximum(jnp.sqrt(head_sum(kk * kk)), 1e-12)
    k2 = kraw * (1.0 + (a - 1.0) * k_a)
    return log_decay, k2, -kkn, kkn * a, g


def _f_post(acts, params, proxies):
    o, r, k2, v, g, x = acts
    lnx_g, lnx_b, r_k, wo, ln_g, ln_b = params
    om = head_sum(o) * (1.0 / HEAD)
    oc = o - om
    ov = head_sum(oc * oc) * (1.0 / HEAD)
    on = oc * lax.rsqrt(ov + GN_EPS) * lnx_g + lnx_b
    bonus = head_sum(r * k2 * r_k) * v
    y = mm((on + bonus) * g, wo, proxies[3])
    return (_ln(DN_ALPHA * x + y, ln_g, ln_b),)


def _f_glu(acts, params, proxies):
    ys, h = acts
    d, wv0, wv1, wg0, wg1, ln_g, ln_b = params
    y = jax.nn.gelu(ys + h * d)
    mix = jnp.concatenate([mm(y, wv0, proxies[1]) * jax.nn.sigmoid(mm(y, wg0, proxies[3])),
                           mm(y, wv1, proxies[2]) * jax.nn.sigmoid(mm(y, wg1, proxies[4]))], axis=1)
    return (_ln(DN_ALPHA * h + mix, ln_g, ln_b),)


def _f_zoh(a_re, a_im, log_dt, b_re_t, b_im_t):
    dt = jnp.exp(log_dt)
    lam_re = jnp.minimum(a_re, -1e-4)
    lam_im = a_im
    mag = jnp.exp(dt * lam_re)
    abar_re = mag * jnp.cos(dt * lam_im)
    abar_im = mag * jnp.sin(dt * lam_im)
    den = lam_re * lam_re + lam_im * lam_im
    nr, ni = abar_re - 1.0, abar_im
    coef_re = ((nr * lam_re + ni * lam_im) / den)[:, None, :]
    coef_im = ((ni * lam_re - nr * lam_im) / den)[:, None, :]
    return (abar_re, abar_im, coef_re * b_re_t - coef_im * b_im_t, coef_re * b_im_t + coef_im * b_re_t)


def _bdot16_raw(a, b, ca, cb):
    return lax.dot_general(a.astype(BF16), b.astype(BF16), (((ca,), (cb,)), ((0,), (0,))), preferred_element_type=F32)


@functools.partial(jax.custom_vjp, nondiff_argnums=(2, 3))
def _bdot16(a, b, ca, cb):
    return _bdot16_raw(a, b, ca, cb)


def _bdot16_bwd(ca, cb, res, g):
    a, b = res
    if (ca, cb) == (2, 1):
        return _bdot16_raw(g, b, 2, 2), _bdot16_raw(a, g, 1, 1)
    if (ca, cb) == (2, 2):
        return _bdot16_raw(g, b, 2, 1), _bdot16_raw(g, a, 1, 1)
    assert (ca, cb) == (1, 1)
    return _bdot16_raw(b, g, 2, 2), _bdot16_raw(a, g, 2, 1)


_bdot16.defvjp(lambda a, b, ca, cb: (_bdot16_raw(a, b, ca, cb), (a, b)), _bdot16_bwd)

def _time_sums(x, suffix):
    hg, ln, _ = x.shape
    row = lax.broadcasted_iota(jnp.int32, (hg, ln, ln), 1)
    col = lax.broadcasted_iota(jnp.int32, (hg, ln, ln), 2)
    tri = ((row <= col) if suffix else (row >= col)).astype(BF16)
    return sum(lax.dot_general(tri, p, (((2,), (1,)), ((0,), (0,))), preferred_element_type=F32) for p in _split3(x))


@jax.custom_vjp
def _time_cumsum(x):
    return _time_sums(x, False)


_time_cumsum.defvjp(lambda x: (_time_sums(x, False), None), lambda _, g: (_time_sums(g, True),))

_dot_score = _bdot16
_dot_inverse = _bdot16
_dot_value = _bdot16


def _rec_chunk(s0, r, lw, k, v, a, b):
    hg, ln, _ = r.shape
    row = lax.broadcasted_iota(jnp.int32, (hg, ln, ln), 1)
    col = lax.broadcasted_iota(jnp.int32, (hg, ln, ln), 2)
    incl, strict = row >= col, row > col
    cum = _time_cumsum(lw)
    total = jnp.sum(lw, axis=1, keepdims=True)
    e_cum, e_inv, e_prev, e_tail = jnp.exp(cum), jnp.exp(-cum), jnp.exp(cum - lw), jnp.exp(total - cum)
    rt, at, bt, kt = r * e_cum, a * e_prev, b * e_inv, k * e_inv
    ar = jnp.concatenate([at, rt], axis=1)
    on_b, on_k = _dot_score(ar, bt, 2, 2), _dot_score(ar, kt, 2, 2)
    aab, arb = jnp.where(strict, on_b[:, :ln], 0.0), jnp.where(incl, on_b[:, ln:], 0.0)
    aak, ark = jnp.where(strict, on_k[:, :ln], 0.0), jnp.where(incl, on_k[:, ln:], 0.0)
    p = (row == col).astype(F32) + aab
    m = aab
    for _ in range(int(math.log2(ln)) - 1):
        m = _dot_inverse(m, m, 2, 1)
        p = p + _dot_inverse(p, m, 2, 1)
    from_state = _dot_value(ar, s0, 2, 2)
    from_v = _dot_value(jnp.concatenate([aak, ark], axis=1), v, 2, 1)
    u = _dot_inverse(p, from_state[:, :ln] + from_v[:, :ln], 2, 1)
    o = from_state[:, ln:] + from_v[:, ln:] + _dot_value(arb, u, 2, 1)
    s1 = s0 * jnp.exp(total) + _dot_value(jnp.concatenate([u, v], axis=1),
                                          jnp.concatenate([b * e_tail, k * e_tail], axis=1), 1, 1)
    return o, s1


def _full_spec(shape):
    nd = len(shape)
    return pl.BlockSpec(tuple(shape), lambda *_: (0,) * nd)


def _stage_fwd(name, f, acts, params, out_dims, tb):
    t = acts[0].shape[0]
    na, npar = len(acts), len(params)

    def body(*refs):
        outs = f(tuple(r[...] for r in refs[:na]), tuple(r[...] for r in refs[na:na + npar]), (None,) * npar)
        for r, val in zip(refs[na + npar:], outs):
            r[...] = val

    return pl.pallas_call(
        body, name=name, grid=(t // tb,),
        in_specs=[pl.BlockSpec((tb, a.shape[1]), lambda i: (i, 0)) for a in acts] + [_full_spec(p.shape) for p in params],
        out_specs=[pl.BlockSpec((tb, d), lambda i: (i, 0)) for d in out_dims],
        out_shape=[_sds((t, d)) for d in out_dims],
        compiler_params=_cparams(("arbitrary",)),
    )(*acts, *params)


def _stage_bwd(name, f, acts, params, couts, tb, proxied=(), halves_of=()):
    nh = len(halves_of)
    t = acts[0].shape[0]
    groups = [c if isinstance(c, tuple) else (c,) for c in couts]
    couts = [term for grp in groups for term in grp]
    na, npar, nc = len(acts), len(params), len(couts)
    steps = t // tb

    def f_diff(act_vals, diff_vals, param_vals):
        real = tuple(param_vals[i] if i in proxied else diff_vals[i] for i in range(npar))
        proxies = tuple(diff_vals[i] if i in proxied else None for i in range(npar))
        return f(act_vals, real, proxies)

    def body(*refs):
        a_refs, p_hbm, c_refs = refs[:na], refs[na:na + npar], refs[na + npar:na + npar + nc]
        o = na + npar + nc
        half_src, o = refs[o:o + nh], o + nh
        da_refs, dp_hbm, half_dst = refs[o:o + na], refs[o + na:o + na + npar], refs[o + na + npar:o + na + npar + nh]
        o = o + na + npar + nh
        p_buf, acc, half_sems = refs[o:o + npar], refs[o + npar:o + 2 * npar], refs[o + 2 * npar:]
        i = pl.program_id(0)
        if nh:
            half_start, half_finish = _sibling_halves_phases(half_src, half_dst, half_sems)
            pl.when(i == 0)(half_start)

        @pl.when(i == 0)
        def _():
            for src, dst in zip(p_hbm, p_buf):
                pltpu.sync_copy(src, dst)
            for r in acc:
                r[...] = jnp.zeros_like(r)

        param_vals = tuple(r[...] for r in p_buf)
        diff_vals = tuple(jnp.zeros(v.shape, F32) if i in proxied else v for i, v in enumerate(param_vals))
        _, vjp = jax.vjp(functools.partial(f_diff, param_vals=param_vals), tuple(r[...] for r in a_refs), diff_vals)
        terms = iter(c_refs)
        d_acts, d_params = vjp(tuple(functools.reduce(jnp.add, [next(terms)[...] for _ in grp]) for grp in groups))
        for r, val in zip(da_refs, d_acts):
            r[...] = val
        for r, val in zip(acc, d_params):
            r[...] += val

        @pl.when(i == steps - 1)
        def _():
            for src, dst in zip(acc, dp_hbm):
                pltpu.sync_copy(src, dst)

        if nh:
            pl.when(i == steps - 1)(half_finish)

    hbm = pl.BlockSpec(memory_space=pltpu.HBM)
    outs = pl.pallas_call(
        body, name=name, grid=(steps,),
        in_specs=[pl.BlockSpec((tb, a.shape[1]), lambda i: (i, 0)) for a in acts] + [hbm] * npar
        + [pl.BlockSpec((tb, c.shape[1]), lambda i: (i, 0)) for c in couts] + [hbm] * nh,
        out_specs=[pl.BlockSpec((tb, a.shape[1]), lambda i: (i, 0)) for a in acts] + [hbm] * (npar + nh),
        out_shape=[_sds(a.shape) for a in acts] + [_sds(p.shape) for p in params] + _sibling_halves_shapes(halves_of),
        scratch_shapes=[pltpu.VMEM(p.shape, p.dtype) for p in params] + [pltpu.VMEM(p.shape, F32) for p in params]
        + (_sibling_halves_sems(nh) if nh else []),
        compiler_params=_cparams(("arbitrary",)),
    )(*acts, *params, *couts, *halves_of)
    if nh:
        return outs[:na], outs[na:na + npar], outs[na + npar:]
    return outs[:na], outs[na:]


def _tiled_matmul(name, a, b, mode, grid, a_spec, b_spec, o_spec, out_shape):
    nk = grid[2]
    dims = {"nn": ((1,), (0,)), "nt": ((1,), (1,)), "tn": ((0,), (0,))}[mode]

    def body(a_ref, b_ref, o_ref, acc):
        kk = pl.program_id(2)

        @pl.when(kk == 0)
        def _():
            acc[...] = jnp.zeros_like(acc)

        acc[...] += lax.dot_general(a_ref[...].astype(BF16), b_ref[...].astype(BF16), (dims, ((), ())),
                                    preferred_element_type=F32)

        @pl.when(kk == nk - 1)
        def _():
            o_ref[...] = acc[...]

    return pl.pallas_call(
        body, name=name, grid=grid, in_specs=[a_spec, b_spec], out_specs=o_spec, out_shape=_sds(out_shape),
        scratch_shapes=[pltpu.VMEM(o_spec.block_shape, F32)],
        compiler_params=_cparams(("parallel", "parallel", "arbitrary")),
    )(a, b)


def _mlp_weight_grad(name, a, b, layer, layers, split, into=None, tile=512):
    t, m = a.shape
    n = b.shape[1]
    tile = 2 * tile
    tk = min(tile, t)
    if split == "n":
        tm, tn = min(tile, m), min(tile, n // 4)
        per = n // 4 // tn
        shape = (4, layers, m, n // 4)
        o_idx = lambda i, j, k: (j // per, layer, i, j % per)
    else:
        tm, tn = min(tile, m // 4), min(tile, n)
        per = m // 4 // tm
        shape = (4, layers, m // 4, n)
        o_idx = lambda i, j, k: (i // per, layer, i % per, j)
    nk = t // tk

    def body(a_ref, b_ref, *rest):
        o_ref, acc = rest[-2:]
        kk = pl.program_id(2)

        @pl.when(kk == 0)
        def _():
            acc[...] = jnp.zeros_like(acc)

        acc[...] += lax.dot_general(a_ref[...].astype(BF16), b_ref[...].astype(BF16), (((0,), (0,)), ((), ())),
                                    preferred_element_type=F32)

        @pl.when(kk == nk - 1)
        def _():
            o_ref[...] = acc[...]

    in_specs = [pl.BlockSpec((tk, tm), lambda i, j, k: (k, i)), pl.BlockSpec((tk, tn), lambda i, j, k: (k, j))]
    operands = [a, b]
    aliases = {}
    if into is not None:
        in_specs.append(pl.BlockSpec(memory_space=pl.ANY))
        operands.append(into)
        aliases = {2: 0}
    return pl.pallas_call(
        body, name=name, grid=(m // tm, n // tn, nk), in_specs=in_specs,
        out_specs=pl.BlockSpec((None, None, tm, tn), o_idx), out_shape=_sds(shape), input_output_aliases=aliases,
        scratch_shapes=[pltpu.VMEM((tm, tn), F32)],
        compiler_params=_cparams(("parallel", "parallel", "arbitrary")),
    )(*operands)


S5_PACK = 8


def _s5_weight_grad(name, x, s, wide_rows, tk):
    t, c = x.shape
    wide = s.shape[1]
    kb, nb = S5_PACK * SSM_GROUP, S5_PACK * SSM_STATE
    nsb = c // kb
    x_spec = pl.BlockSpec((tk, kb), lambda i, j, k: (k, j % nsb))
    s_spec = pl.BlockSpec((tk, nb), lambda i, j, k: (k, j))
    if wide_rows:
        return _tiled_matmul(name, s, x, "tn", (1, wide // nb, t // tk), s_spec, x_spec,
                             pl.BlockSpec((nb, kb), lambda i, j, k: (j, 0)), (wide, kb))
    return _tiled_matmul(name, x, s, "tn", (1, wide // nb, t // tk), x_spec, s_spec,
                         pl.BlockSpec((kb, nb), lambda i, j, k: (0, j)), (kb, wide))


def _mlp_fwd(name, h, w1, w2, layer, ln_g, ln_b, tb, shards=()):
    t, c = h.shape
    nj, fc = w1.shape[0], w1.shape[3]
    nsh = len(shards)
    steps = (t // tb) * nj

    def body(h_ref, w1_ref, w2_ref, g_ref, b_ref, *rest):
        src, (out_ref, s_ref), dst = rest[:nsh], rest[nsh:nsh + 2], rest[nsh + 2:2 * nsh + 2]
        acc, sems = rest[2 * nsh + 2], rest[2 * nsh + 3:]
        j = pl.program_id(1)
        step = pl.program_id(0) * nj + j
        if nsh:
            start, forward, finish = _gather_big_phases(src, dst, sems)
            pl.when(step == 0)(start)

        @pl.when(j == 0)
        def _():
            acc[...] = jnp.zeros_like(acc)

        hid = jnp.dot(h_ref[...].astype(BF16), w1_ref[...].astype(BF16), preferred_element_type=F32)
        act = jnp.square(jnp.maximum(hid, 0.0))
        acc[...] += jnp.dot(act.astype(BF16), w2_ref[...].astype(BF16), preferred_element_type=F32)

        @pl.when(j == nj - 1)
        def _():
            s = DN_ALPHA * h_ref[...] + acc[...]
            s_ref[...] = s
            out_ref[...] = _ln(s, g_ref[...], b_ref[...])

        if nsh:
            pl.when(step == steps // 2)(forward)
            pl.when(step == steps - 1)(finish)

    row = pl.BlockSpec((tb, c), lambda i, j: (i, 0))
    vec = pl.BlockSpec((1, c), lambda i, j: (0, 0))
    outs = pl.pallas_call(
        body, name=name, grid=(t // tb, nj),
        in_specs=[row, pl.BlockSpec((None, None, c, fc), lambda i, j: (j, layer, 0, 0)),
                  pl.BlockSpec((None, None, fc, c), lambda i, j: (j, layer, 0, 0)), vec, vec] + [_HBM] * nsh,
        out_specs=[row, row] + [_HBM] * nsh,
        out_shape=[_sds((t, c)), _sds((t, c))] + [_sds((4,) + a.shape, a.dtype) for a in shards],
        scratch_shapes=[pltpu.VMEM((tb, c), F32)] + (_gather_big_sems(nsh) if nsh else []),
        compiler_params=_cparams(("arbitrary", "arbitrary")),
    )(h, w1, w2, ln_g, ln_b, *shards)
    return outs[0], outs[1], outs[2:]


def _mlp_bwd(name, h, s, dout, w1, w2, layer, ln_g, ln_b, tb):
    t, c = h.shape
    nj, fc = w1.shape[0], w1.shape[3]
    ff = nj * fc
    ni = t // tb
    nt = (((1,), (1,)), ((), ()))
    douts = dout if isinstance(dout, tuple) else (dout,)
    nd = len(douts)

    def body(h_ref, s_ref, *rest):
        dout_refs = rest[:nd]
        (w1_ref, w2_ref, g_ref, b_ref, dh_ref, ds_ref, dhid_ref, act_ref, dg_ref, db_ref,
         ds_scr, dh_acc, dg_acc, db_acc) = rest[nd:]
        i, j = pl.program_id(0), pl.program_id(1)

        @pl.when((i == 0) & (j == 0))
        def _():
            dg_acc[...] = jnp.zeros_like(dg_acc)
            db_acc[...] = jnp.zeros_like(db_acc)

        @pl.when(j == 0)
        def _():
            _, vjp = jax.vjp(_ln, s_ref[...], g_ref[...], b_ref[...])
            ds, dg, db = vjp(functools.reduce(jnp.add, [r[...] for r in dout_refs]))
            ds_scr[...] = ds
            ds_ref[...] = ds.astype(BF16)
            dh_acc[...] = DN_ALPHA * ds
            dg_acc[...] += dg
            db_acc[...] += db

        w1b, w2b = w1_ref[...].astype(BF16), w2_ref[...].astype(BF16)
        hid = jnp.dot(h_ref[...].astype(BF16), w1b, preferred_element_type=F32)
        rl = jnp.maximum(hid, 0.0)
        dact = lax.dot_general(ds_scr[...].astype(BF16), w2b, nt, preferred_element_type=F32)
        dhid = (dact * 2.0 * rl).astype(BF16)
        dh_acc[...] += lax.dot_general(dhid, w1b, nt, preferred_element_type=F32)
        dhid_ref[...] = dhid
        act_ref[...] = (rl * rl).astype(BF16)

        @pl.when(j == nj - 1)
        def _():
            dh_ref[...] = dh_acc[...]

        @pl.when((i == ni - 1) & (j == nj - 1))
        def _():
            dg_ref[...] = dg_acc[...]
            db_ref[...] = db_acc[...]

    row = pl.BlockSpec((tb, c), lambda i, j: (i, 0))
    vec = pl.BlockSpec((1, c), lambda i, j: (0, 0))
    wide = pl.BlockSpec((tb, fc), lambda i, j: (i, j))
    return pl.pallas_call(
        body, name=name, grid=(ni, nj),
        in_specs=[row, row] + [row] * nd + [pl.BlockSpec((None, None, c, fc), lambda i, j: (j, layer, 0, 0)),
                                            pl.BlockSpec((None, None, fc, c), lambda i, j: (j, layer, 0, 0)), vec, vec],
        out_specs=[row, row, wide, wide, vec, vec],
        out_shape=[_sds((t, c)), _sds((t, c), BF16), _sds((t, ff), BF16), _sds((t, ff), BF16), _sds((1, c)), _sds((1, c))],
        scratch_shapes=[pltpu.VMEM((tb, c), F32), pltpu.VMEM((tb, c), F32), pltpu.VMEM((1, c), F32), pltpu.VMEM((1, c), F32)],
        compiler_params=_cparams(("arbitrary", "arbitrary")),
    )(h, s, *douts, w1, w2, ln_g, ln_b)


def _load_heads(ref, hg):
    return jnp.stack([ref[:, h * HEAD:(h + 1) * HEAD] for h in range(hg)])


def _store_heads(ref, val):
    for h in range(val.shape[0]):
        ref[:, h * HEAD:(h + 1) * HEAD] = val[h]


def _rec_fwd(r, lw, k, v, a, b, hg, shards):
    t, c = r.shape
    n = HEAD
    nh = c // n
    ln = REC_CHUNK
    nck = t // ln
    ngrp = nh // hg
    nsh = len(shards)
    steps = ngrp * nck

    def body(r_ref, lw_ref, k_ref, v_ref, a_ref, b_ref, *rest):
        src, (o_ref, s0_ref), dst = rest[:nsh], rest[nsh:nsh + 2], rest[nsh + 2:2 * nsh + 2]
        state, sems = rest[2 * nsh + 2], rest[2 * nsh + 3:]
        step = pl.program_id(0) * nck + pl.program_id(1)
        start, forward, finish = _gather_big_phases(src, dst, sems)
        pl.when(step == 0)(start)

        @pl.when(pl.program_id(1) == 0)
        def _():
            state[...] = jnp.zeros_like(state)

        s0 = state[...]
        s0_ref[...] = s0
        o, s1 = _rec_chunk(s0, *(_load_heads(x, hg) for x in (r_ref, lw_ref, k_ref, v_ref, a_ref, b_ref)))
        _store_heads(o_ref, o)
        state[...] = s1
        pl.when(step == steps // 2)(forward)
        pl.when(step == steps - 1)(finish)

    seq = pl.BlockSpec((ln, hg * n), lambda g, i: (i, g))
    outs = pl.pallas_call(
        body, name="rec_fwd", grid=(ngrp, nck), in_specs=[seq] * 6 + [_HBM] * nsh,
        out_specs=[seq, pl.BlockSpec((None, hg, n, n), lambda g, i: (i, g, 0, 0))] + [_HBM] * nsh,
        out_shape=[_sds((t, c)), _sds((nck, nh, n, n))] + [_sds((4,) + s.shape, s.dtype) for s in shards],
        scratch_shapes=[pltpu.VMEM((hg, n, n), F32)] + _gather_big_sems(nsh),
        compiler_params=_cparams(("arbitrary", "arbitrary")),
    )(r, lw, k, v, a, b, *shards)
    return outs[0], outs[1], outs[2:]


def _rec_bwd(r, lw, k, v, a, b, s0s, do, hg, chip_sums):
    t, c = r.shape
    n = HEAD
    nh = c // n
    ln = REC_CHUNK
    nck = t // ln
    ngrp = nh // hg
    nsum = len(chip_sums)
    steps = ngrp * nck

    def body(r_ref, lw_ref, k_ref, v_ref, a_ref, b_ref, s0_ref, do_ref, *rest):
        src, grad_refs, land = rest[:nsum], rest[nsum:nsum + 6], rest[nsum + 6:2 * nsum + 6]
        dstate, sems = rest[2 * nsum + 6], rest[2 * nsum + 7:]
        step = pl.program_id(0) * nck + pl.program_id(1)
        start, finish = _scatter_big_phases(src, land, sems)
        pl.when(step == 0)(start)

        @pl.when(pl.program_id(1) == 0)
        def _():
            dstate[...] = jnp.zeros_like(dstate)

        _, vjp = jax.vjp(_rec_chunk, s0_ref[...], *(_load_heads(x, hg) for x in (r_ref, lw_ref, k_ref, v_ref, a_ref, b_ref)))
        ds0, *grads = vjp((_load_heads(do_ref, hg), dstate[...]))
        dstate[...] = ds0
        for ref, val in zip(grad_refs, grads):
            _store_heads(ref, val)
        pl.when(step == steps - 1)(finish)

    seq = pl.BlockSpec((ln, hg * n), lambda g, i: (nck - 1 - i, g))
    outs = pl.pallas_call(
        body, name="rec_bwd", grid=(ngrp, nck),
        in_specs=[seq] * 6 + [pl.BlockSpec((None, hg, n, n), lambda g, i: (nck - 1 - i, g, 0, 0)), seq] + [_HBM] * nsum,
        out_specs=[seq] * 6 + [_HBM] * nsum,
        out_shape=[_sds((t, c))] * 6 + [_sds((3,) + s.shape[1:], s.dtype) for s in chip_sums],
        scratch_shapes=[pltpu.VMEM((hg, n, n), F32)] + _scatter_big_sems(nsum),
        compiler_params=_cparams(("arbitrary", "arbitrary")),
    )(r, lw, k, v, a, b, s0s, do, *chip_sums)
    return outs[:6], outs[6:]


def _s5_blocks(c):
    kb, nb = S5_PACK * SSM_GROUP, S5_PACK * SSM_STATE
    return kb, nb, c // kb


def _s5_fwd(h, bc, abar, cc, tb, shards=()):
    t, c = h.shape
    w2 = bc.shape[1]
    w = w2 // 2
    kb, nb, nsb = _s5_blocks(c)

    nsh = len(shards)
    steps = t // tb

    def body(h_ref, bc_ref, a_ref, cc_ref, *rest):
        src, (s_ref, y_ref), dst = rest[:nsh], rest[nsh:nsh + 2], rest[nsh + 2:2 * nsh + 2]
        carry, rows, sems = rest[2 * nsh + 2], rest[2 * nsh + 3], rest[2 * nsh + 4:]
        if nsh:
            start, forward, finish = _gather_big_phases(src, dst, sems)
            pl.when(pl.program_id(0) == 0)(start)

        @pl.when(pl.program_id(0) == 0)
        def _():
            carry[...] = jnp.zeros_like(carry)

        for j in range(w2 // nb):
            ch = (j % nsb) * kb
            rows[:, j * nb:(j + 1) * nb] = jnp.dot(h_ref[:, ch:ch + kb].astype(BF16), bc_ref[:, j * nb:(j + 1) * nb],
                                                   preferred_element_type=F32)
        ar, ai = a_ref[:, :w], a_ref[:, w:]

        def step(i, state):
            hr, hi = state
            nr = ar * hr - ai * hi + rows[pl.ds(i, 1), :w]
            ni = ar * hi + ai * hr + rows[pl.ds(i, 1), w:]
            rows[pl.ds(i, 1), :w] = nr
            rows[pl.ds(i, 1), w:] = ni
            return nr, ni

        hr, hi = lax.fori_loop(0, tb, step, (carry[:, :w], carry[:, w:]))
        carry[:, :w] = hr
        carry[:, w:] = hi
        s_ref[...] = rows[...].astype(BF16)
        for j in range(nsb):
            re, im = j * nb, w + j * nb
            y_ref[:, j * kb:(j + 1) * kb] = (
                jnp.dot(s_ref[:, re:re + nb], cc_ref[re:re + nb, :], preferred_element_type=F32)
                + jnp.dot(s_ref[:, im:im + nb], cc_ref[im:im + nb, :], preferred_element_type=F32))
        if nsh:
            pl.when(pl.program_id(0) == steps // 2)(forward)
            pl.when(pl.program_id(0) == steps - 1)(finish)

    outs = pl.pallas_call(
        body, name="s5_fwd", grid=(steps,),
        in_specs=[pl.BlockSpec((tb, c), lambda i: (i, 0)), _full_spec(bc.shape), _full_spec(abar.shape), _full_spec(cc.shape)]
        + [_HBM] * nsh,
        out_specs=[pl.BlockSpec((tb, w2), lambda i: (i, 0)), pl.BlockSpec((tb, c), lambda i: (i, 0))] + [_HBM] * nsh,
        out_shape=[_sds((t, w2), BF16), _sds((t, c))] + [_sds((4,) + a.shape, a.dtype) for a in shards],
        scratch_shapes=[pltpu.VMEM((1, w2), F32), pltpu.VMEM((tb, w2), F32)] + (_gather_big_sems(nsh) if nsh else []),
        compiler_params=_cparams(("arbitrary",)),
    )(h, bc, abar, cc, *shards)
    return outs[0], outs[1], outs[2:]


def _s5_bwd(dy, s, abar, cc, bc, tb):
    t, c = dy.shape
    w2 = s.shape[1]
    w = w2 // 2
    kb, nb, nsb = _s5_blocks(c)
    nblk = t // tb
    pack = 16
    per = tb // pack
    nt = (((1,), (1,)), ((), ()))

    def body(dy_ref, s_ref, sprev_ref, a_ref, cc_ref, bc_ref, dbu_ref, dh_ref, da_ref, carry, da_acc, rows):
        i = pl.program_id(0)

        @pl.when(i == 0)
        def _():
            carry[...] = jnp.zeros_like(carry)
            da_acc[...] = jnp.zeros_like(da_acc)

        for j in range(w2 // nb):
            ch = (j % nsb) * kb
            rows[:, j * nb:(j + 1) * nb] = lax.dot_general(dy_ref[:, ch:ch + kb].astype(BF16), cc_ref[j * nb:(j + 1) * nb, :], nt,
                                                           preferred_element_type=F32)
        ar, ai = a_ref[:, :w], a_ref[:, w:]

        def step(n, state):
            gr, gi = state
            row = tb - 1 - n
            nr = rows[pl.ds(row, 1), :w] + ar * gr + ai * gi
            ni = rows[pl.ds(row, 1), w:] + ar * gi - ai * gr
            rows[pl.ds(row, 1), :w] = nr
            rows[pl.ds(row, 1), w:] = ni
            return nr, ni

        gr, gi = lax.fori_loop(0, tb, step, (carry[:, :w], carry[:, w:]))
        carry[:, :w] = gr
        carry[:, w:] = gi
        last = (lax.broadcasted_iota(jnp.int32, (pack, w2), 0) == pack - 1) & (i < nblk - 1)
        before = jnp.sum(jnp.where(last, sprev_ref[...].astype(F32), 0.0), axis=0, keepdims=True)
        rid = lax.broadcasted_iota(jnp.int32, (tb, w2), 0)
        sp = jnp.where(rid == 0, before, pltpu.roll(s_ref[...].astype(F32), 1, 0))
        g = rows[...]
        dbu_ref[...] = g.astype(BF16)
        spr, spi, g_r, g_i = sp[:, :w], sp[:, w:], g[:, :w], g[:, w:]
        da_acc[:, :w] += jnp.sum(spr * g_r + spi * g_i, axis=0, keepdims=True)
        da_acc[:, w:] += jnp.sum(spr * g_i - spi * g_r, axis=0, keepdims=True)
        for j in range(nsb):
            re, im = j * nb, w + j * nb
            dh_ref[:, j * kb:(j + 1) * kb] = (
                lax.dot_general(dbu_ref[:, re:re + nb], bc_ref[:, re:re + nb], nt, preferred_element_type=F32)
                + lax.dot_general(dbu_ref[:, im:im + nb], bc_ref[:, im:im + nb], nt, preferred_element_type=F32))

        @pl.when(i == nblk - 1)
        def _():
            da_ref[...] = da_acc[...]

    wide = pl.BlockSpec((tb, w2), lambda i: (nblk - 1 - i, 0))
    narrow = pl.BlockSpec((tb, c), lambda i: (nblk - 1 - i, 0))
    prev = pl.BlockSpec((pack, w2), lambda i: (jnp.maximum((nblk - 1 - i) * per - 1, 0), 0))
    return pl.pallas_call(
        body, name="s5_bwd", grid=(nblk,),
        in_specs=[narrow, wide, prev, _full_spec(abar.shape), _full_spec(cc.shape), _full_spec(bc.shape)],
        out_specs=[wide, narrow, pl.BlockSpec((1, w2), lambda i: (0, 0))],
        out_shape=[_sds((t, w2), BF16), _sds((t, c)), _sds((1, w2))],
        scratch_shapes=[pltpu.VMEM((1, w2), F32), pltpu.VMEM((1, w2), F32), pltpu.VMEM((tb, w2), F32)],
        compiler_params=_cparams(("arbitrary",)),
    )(dy, s, s, abar, cc, bc)


def _zoh_fwd(a_re, a_im, log_dt, b_re_t, b_im_t):
    def body(*refs):
        for r, val in zip(refs[5:], _f_zoh(*(x[...] for x in refs[:5]))):
            r[...] = val

    return pl.pallas_call(body, name="s5_zoh_fwd", out_shape=[_sds(a_re.shape)] * 2 + [_sds(b_re_t.shape)] * 2,
                          compiler_params=_cparams())(a_re, a_im, log_dt, b_re_t, b_im_t)


def _zoh_bwd(a_re, a_im, log_dt, b_re_t, b_im_t, couts):
    def body(*refs):
        _, vjp = jax.vjp(_f_zoh, *(x[...] for x in refs[:5]))
        for r, val in zip(refs[9:], vjp(tuple(x[...] for x in refs[5:9]))):
            r[...] = val

    ins = (a_re, a_im, log_dt, b_re_t, b_im_t)
    return pl.pallas_call(body, name="s5_zoh_bwd", out_shape=[_sds(x.shape) for x in ins],
                          compiler_params=_cparams())(*ins, *couts)


def _loss_head(h, target, tb):
    t, c = h.shape
    nb = t // tb

    def body(h_ref, t_ref, loss_ref, dh_ref, acc):
        i = pl.program_id(0)

        @pl.when(i == 0)
        def _():
            acc[...] = jnp.zeros_like(acc)

        d = h_ref[...] - t_ref[...]
        dh_ref[...] = d * (1.0 / c)
        acc[...] += 0.5 * jnp.sum(jnp.mean(d * d, axis=-1, keepdims=True), axis=0, keepdims=True)

        @pl.when(i == nb - 1)
        def _():
            loss_ref[...] = jnp.broadcast_to(acc[...], loss_ref.shape)

    row = pl.BlockSpec((tb, c), lambda i: (i, 0))
    return pl.pallas_call(
        body, name="loss_head", grid=(nb,), in_specs=[row, row],
        out_specs=[pl.BlockSpec((8, 128), lambda i: (0, 0)), row], out_shape=[_sds((8, 128)), _sds((t, c))],
        scratch_shapes=[pltpu.VMEM((1, 1), F32)], compiler_params=_cparams(("arbitrary",)),
    )(h, target)


def _rows_tile(rows):
    for cand in (512, 256, 128, 64, 32, 16, 8):
        if rows % cand == 0:
            return cand
    return rows


def _grad_x(here, from_next):
    rows, cols = here[0].shape
    tb = _rows_tile(rows)
    nb = rows // tb
    nh, nn = len(here), len(from_next)
    sub = 8

    def body(*refs):
        i = pl.program_id(0)
        total = functools.reduce(jnp.add, [r[...] for r in refs[:nh]])
        shifted = functools.reduce(jnp.add, [r[...] for r in refs[nh:nh + nn]])
        first_next = functools.reduce(jnp.add, [r[0:1, :] for r in refs[nh + nn:nh + 2 * nn]])
        first_next = jnp.where(i == nb - 1, 0.0, first_next)
        rid = lax.broadcasted_iota(jnp.int32, (tb, cols), 0)
        refs[-1][...] = total + jnp.where(rid == tb - 1, first_next, pltpu.roll(shifted, tb - 1, 0))

    blk = pl.BlockSpec((tb, cols), lambda i: (i, 0))
    nxt = pl.BlockSpec((sub, cols), lambda i: (jnp.minimum(i + 1, nb - 1) * (tb // sub), 0))
    return pl.pallas_call(body, name="grad_x", grid=(nb,), in_specs=[blk] * (nh + nn) + [nxt] * nn, out_specs=blk,
                          out_shape=_sds((rows, cols)), compiler_params=_cparams(("parallel",)))(*here, *from_next, *from_next)


def _adamw_math(w, g, m, v):
    m = ADAM_B1 * m + (1.0 - ADAM_B1) * g
    v = ADAM_B2 * v + (1.0 - ADAM_B2) * jnp.square(g)
    m_hat = m / (1.0 - ADAM_B1 ** ADAM_STEP)
    v_hat = v / (1.0 - ADAM_B2 ** ADAM_STEP)
    delta = -ADAM_LR * (m_hat / (jnp.sqrt(v_hat) + ADAM_EPS) + ADAM_WD * w)
    return delta, m, v


def _adamw(name, parts, w, m, v):
    rows, cols = w.shape
    tb = _rows_tile(rows)
    npart = len(parts)

    def body(*refs):
        g = refs[0][...]
        for r in refs[1:npart]:
            g = g + r[...]
        w_ref, m_ref, v_ref = refs[npart:npart + 3]
        g_out, d_out, m_out, v_out = refs[npart + 3:]
        delta, mn, vn = _adamw_math(w_ref[...], g, m_ref[...], v_ref[...])
        g_out[...] = g
        d_out[...] = delta
        m_out[...] = mn
        v_out[...] = vn

    blk = pl.BlockSpec((tb, cols), lambda i: (i, 0))
    return pl.pallas_call(body, name=name, grid=(rows // tb,), in_specs=[blk] * (npart + 3), out_specs=[blk] * 4,
                          out_shape=[_sds((rows, cols))] * 4, compiler_params=_cparams(("parallel",)))(*parts, w, m, v)


def _shift_down(a):
    return jnp.concatenate([jnp.zeros_like(a[:1]), a[:-1]], axis=0)


def _s5_pack_mask(g):
    return (jnp.arange(g)[None, :] % S5_PACK == jnp.arange(S5_PACK)[:, None]).astype(F32)


def _compact_b(bbar_t):
    g, s, p = bbar_t.shape
    return (_s5_pack_mask(g)[:, None, :, None] * bbar_t.transpose(1, 0, 2)[None]).reshape(S5_PACK * s, g * p)


def _compact_b_t(dense, g):
    s, p = dense.shape[0] // S5_PACK, dense.shape[1] // g
    return jnp.sum(dense.reshape(S5_PACK, s, g, p) * _s5_pack_mask(g)[:, None, :, None], axis=0).transpose(1, 0, 2)


def _compact_c(c_w):
    g, s, p = c_w.shape
    return (c_w.transpose(0, 2, 1)[:, :, None, :] * _s5_pack_mask(g).T[:, None, :, None]).reshape(g * p, S5_PACK * s)


def _compact_c_t(dense, g):
    p, s = dense.shape[0] // g, dense.shape[1] // S5_PACK
    return jnp.sum(dense.reshape(g, p, S5_PACK, s) * _s5_pack_mask(g).T[:, None, :, None], axis=2).transpose(0, 2, 1)


def _local_step(x, target, fw, core):
    t, c = x.shape
    nh = c // HEAD
    ng = c // SSM_GROUP
    tb = min(256, t)
    tbm = min(512, t)
    tbmb = min(512, t)
    tbs = min(256, t)
    tk5 = min(2048, t)
    hg = min(16, nh)
    mu = [fw['rw_mu'][i:i + 1] for i in range(6)]
    ln_g = [fw['ln_g'][i:i + 1] for i in range(4)]
    ln_b = [fw['ln_b'][i:i + 1] for i in range(4)]
    grads = {}

    xp = _shift_down(x)
    proj_params = {n: (mu[i], fw['rw_w' + n]) for n, i in (('r', 0), ('k', 2), ('v', 3))}
    raw = {n: _stage_fwd("proj_" + n, _f_proj, (x, xp), proj_params[n], (c,), tbm)[0] for n in 'rkv'}
    lora_params = (mu[1], mu[4], mu[5], fw['rw_w0'], fw['rw_w1'], fw['rw_w2'], fw['rw_a0'], fw['rw_a1'], fw['rw_a2'],
                   fw['rw_g1'], fw['rw_g2'], fw['rw_k_k'], fw['rw_k_a'])
    lw, k2, an, bb, gate = _stage_fwd("lora", _f_lora, (x, xp, raw['k']), lora_params, (c,) * 5, tbm)
    rec_in = (raw['r'], lw, k2, raw['v'], an, bb)
    o, s0s, (wo_view, w1_l0, w2_l0) = _rec_fwd(*rec_in, hg, fw['late_a'])
    fw = dict(fw, rw_wo=wo_view.reshape(c, c))
    mlp_w = [(w1_l0.reshape(4, 1, c, -1), w2_l0.reshape(4, 1, -1, c)), None]
    post_params = (fw['rw_lnx_g'], fw['rw_lnx_b'], fw['rw_r_k'], fw['rw_wo'], ln_g[0], ln_b[0])
    post_acts = (o, raw['r'], k2, raw['v'], gate, x)
    h1, = _stage_fwd("post", _f_post, post_acts, post_params, (c,), tbm)
    h2, s_mlp0, (w1_l1,) = _mlp_fwd("mlp0_fwd", h1, *mlp_w[0], 0, ln_g[1], ln_b[1], tbm, fw['late_b'])

    a_re, a_im, log_dt = fw['s5_a_re'], fw['s5_a_im'], fw['s5_log_dt']
    b_re_t, b_im_t = fw['s5_b_re'].transpose(0, 2, 1), fw['s5_b_im'].transpose(0, 2, 1)
    abar_re, abar_im, bbar_re_t, bbar_im_t = _zoh_fwd(a_re, a_im, log_dt, b_re_t, b_im_t)
    abar = jnp.concatenate([abar_re.reshape(1, -1), abar_im.reshape(1, -1)], axis=1)
    bc = jnp.concatenate([_compact_b(bbar_re_t), _compact_b(bbar_im_t)], axis=1).astype(BF16)
    cc = jnp.concatenate([_compact_c(fw['s5_c_re']), -_compact_c(fw['s5_c_im'])], axis=0).astype(BF16)
    st, ys, (glu_view, w2_l1) = _s5_fwd(h2, bc, abar, cc, tbs, fw['late_c'])
    mlp_w[1] = (w1_l1.reshape(4, 1, c, -1), w2_l1.reshape(4, 1, -1, c))
    fw = dict(fw, s5_w_glu=tuple(glu_view[q] for q in range(4)))
    glu_params = (fw['s5_d'], *fw['s5_w_glu'], ln_g[2], ln_b[2])
    h3, = _stage_fwd("glu", _f_glu, (ys, h2), glu_params, (c,), tbm)
    h4, s_mlp1, _ = _mlp_fwd("mlp1_fwd", h3, *mlp_w[1], 0, ln_g[3], ln_b[3], tbm)

    loss_blk, dh4 = _loss_head(h4, target, tb)

    dln_g, dln_b = [None] * 4, [None] * 4
    dh3, ds1, dhid1, act1, dln_g[3], dln_b[3] = _mlp_bwd("mlp1_bwd", h3, s_mlp1, dh4, *mlp_w[1], 0,
                                                         ln_g[3], ln_b[3], tbmb)
    dw1 = _mlp_weight_grad("mlp1_dw1", h3, dhid1, 1, DEPTH, "n")
    dw2 = _mlp_weight_grad("mlp1_dw2", act1, ds1, 1, DEPTH, "m")
    (dys, dh2_glu), (grads['s5_d'], *dglu, dln_g[2], dln_b[2]) = _stage_bwd(
        "glu_bwd", _f_glu, (ys, h2), glu_params, (dh3,), tb, proxied=(1, 2, 3, 4))
    grads['s5_w_glu'] = jnp.stack(dglu)
    dcc = _s5_weight_grad("s5_dcc", dys, st, True, tk5)
    dbu, dh2_bu, dabar = _s5_bwd(dys, st, abar, cc, bc, tbs)
    dbc = _s5_weight_grad("s5_dbc", h2, dbu, False, tk5)
    gp = ng * SSM_STATE
    grads['s5_c_re'] = _compact_c_t(dcc[:gp], ng)
    grads['s5_c_im'] = -_compact_c_t(dcc[gp:], ng)
    zoh_couts = (dabar[:, :gp].reshape(ng, SSM_STATE), dabar[:, gp:].reshape(ng, SSM_STATE),
                 _compact_b_t(dbc[:, :gp], ng), _compact_b_t(dbc[:, gp:], ng))
    grads['s5_a_re'], grads['s5_a_im'], grads['s5_log_dt'], db_re_t, db_im_t = _zoh_bwd(
        a_re, a_im, log_dt, b_re_t, b_im_t, zoh_couts)
    grads['s5_b_re'], grads['s5_b_im'] = db_re_t.transpose(0, 2, 1), db_im_t.transpose(0, 2, 1)
    dh2 = (dh2_glu, dh2_bu)

    dh1, ds0, dhid0, act0, dln_g[1], dln_b[1] = _mlp_bwd("mlp0_bwd", h1, s_mlp0, dh2, *mlp_w[0], 0,
                                                         ln_g[1], ln_b[1], tbmb)
    grads['mlp_w1'] = _mlp_weight_grad("mlp0_dw1", h1, dhid0, 0, DEPTH, "n", into=dw1)
    grads['mlp_w2'] = _mlp_weight_grad("mlp0_dw2", act0, ds0, 0, DEPTH, "m", into=dw2)
    ready_views = [grads[n].reshape(4, -1, grads[n].shape[-1]) for n in BIG_READY]
    (do, dr_p, dk2_p, dv_p, dgate, dx_post), post_g, ready_others = _stage_bwd(
        "post_bwd", _f_post, post_acts, post_params, (dh1,), tb, proxied=(3,), halves_of=ready_views)
    grads['rw_lnx_g'], grads['rw_lnx_b'], grads['rw_r_k'], grads['rw_wo'], dln_g[0], dln_b[0] = post_g
    ready_sums = [_half_add(f"half_add_a{i}", v, o, core) for i, (v, o) in enumerate(zip(ready_views, ready_others))]
    rec_g, ready_lands = _rec_bwd(*rec_in, s0s, do, hg, ready_sums)
    reduced = dict(zip(BIG_READY, zip(ready_sums, ready_lands)))
    dr_r, dlw, dk2_r, dv_r, dan, dbb = rec_g
    dk2 = (dk2_p, dk2_r)
    (dx_l, dxp_l, dkraw_l), lora_g = _stage_bwd("lora_bwd", _f_lora, (x, xp, raw['k']), lora_params,
                                                (dlw, dk2, dan, dbb, dgate), tb)
    (dmu_w, dmu_a, dmu_g, grads['rw_w0'], grads['rw_w1'], grads['rw_w2'], grads['rw_a0'], grads['rw_a1'], grads['rw_a2'],
     grads['rw_g1'], grads['rw_g2'], grads['rw_k_k'], grads['rw_k_a']) = lora_g
    dproj = {'r': (dr_p, dr_r), 'k': dkraw_l, 'v': (dv_p, dv_r)}
    dxs, dxps, dmu = [dx_post, dx_l], [dxp_l], {}
    for n in 'rkv':
        (dx_n, dxp_n), (dmu[n], grads['rw_w' + n]) = _stage_bwd("proj_bwd_" + n, _f_proj, (x, xp), proj_params[n],
                                                                 (dproj[n],), tbm, proxied=(1,))
        dxs.append(dx_n)
        dxps.append(dxp_n)
    grads['rw_mu'] = jnp.concatenate([dmu['r'], dmu_w, dmu['k'], dmu['v'], dmu_a, dmu_g], axis=0)
    grads['ln_g'] = jnp.concatenate(dln_g, axis=0)
    grads['ln_b'] = jnp.concatenate(dln_b, axis=0)
    grad_x = _grad_x(dxs, dxps)
    late = [n for n in BIG if n not in BIG_READY]
    late_sums = dict(zip(late, _chip_sums("b", [grads[n] for n in late], core)))
    return loss_blk, grad_x, grads, reduced, late_sums


def _position():
    return lax.axis_index("x"), lax.axis_index("y"), lax.axis_index("c")


def _other_chips(x, y):
    return [(1 - x, y), (x, 1 - y), (1 - x, 1 - y)]


def _chip_slice(ref, axis, q, size):
    idx = [slice(None)] * len(ref.shape)
    idx[axis] = pl.ds(pl.multiple_of(q * size, size), size)
    return ref.at[tuple(idx)]


_HBM = pl.BlockSpec(memory_space=pltpu.HBM)


def _gather_small_phases(src, dst, axes, sems):
    n = len(src)
    send_sems, recv_sems, own_sems = sems
    x, y, c = _position()
    chips = _other_chips(x, y)
    sizes = [src[a].shape[axes[a]] for a in range(n)]

    def copy(a, k, q):
        return pltpu.make_async_remote_copy(
            src_ref=src[a], dst_ref=_chip_slice(dst[a], axes[a], q, sizes[a]), send_sem=send_sems.at[a, k],
            recv_sem=recv_sems.at[a, k], device_id=(*chips[k], c), device_id_type=MESH)

    def own(a):
        return pltpu.make_async_copy(src[a], _chip_slice(dst[a], axes[a], 2 * x + y, sizes[a]), own_sems.at[a])

    def start():
        for a in range(n):
            own(a).start()
            for k in range(3):
                copy(a, k, 2 * x + y).start()

    def finish():
        for a in range(n):
            for k, (cx, cy) in enumerate(chips):
                copy(a, k, 2 * cx + cy).wait_recv()
        for a in range(n):
            for k in range(3):
                copy(a, k, 2 * x + y).wait_send()
            own(a).wait()

    return start, finish


def _gather_early(big, small, axes):
    nb, ns = len(big), len(small)
    full_shapes = [tuple(s * 4 if i == ax else s for i, s in enumerate(a.shape)) for a, ax in zip(small, axes)]

    def body(*refs):
        src_b, src_s = refs[:nb], refs[nb:nb + ns]
        dst_b, dst_s = refs[nb + ns:2 * nb + ns], refs[2 * nb + ns:2 * (nb + ns)]
        sems = refs[2 * (nb + ns):]
        small_start, small_finish = _gather_small_phases(src_s, dst_s, axes, sems[5:])
        small_start()
        for phase in _gather_big_phases(src_b, dst_b, sems[:5]):
            phase()
        small_finish()

    outs = pl.pallas_call(
        body, name="gather_early", in_specs=[_HBM] * (nb + ns), out_specs=[_HBM] * (nb + ns),
        out_shape=[_sds((4,) + a.shape, a.dtype) for a in big] + [_sds(s, a.dtype) for s, a in zip(full_shapes, small)],
        scratch_shapes=_gather_big_sems(nb) + [pltpu.SemaphoreType.DMA((ns, 3)), pltpu.SemaphoreType.DMA((ns, 3)),
                                               pltpu.SemaphoreType.DMA((ns,))],
        compiler_params=_cparams(),
    )(*big, *small)
    return outs[:nb], outs[nb:]


def _scatter_pieces(fulls, axes, sums):
    n, nsum = len(fulls), len(sums)
    sizes = [a.shape[ax] // 4 for a, ax in zip(fulls, axes)]
    shard_shapes = [tuple(sz if i == ax else s for i, s in enumerate(a.shape)) for a, ax, sz in zip(fulls, axes, sizes)]

    def body(*refs):
        src, big_src = refs[:n], refs[n:n + nsum]
        land, big_land = refs[n + nsum:2 * n + nsum], refs[2 * n + nsum:2 * (n + nsum)]
        send_sems, recv_sems = refs[2 * (n + nsum):2 * (n + nsum) + 2]
        big_start, big_finish = _scatter_big_phases(big_src, big_land, refs[2 * (n + nsum) + 2:])
        big_start()
        x, y, c = _position()
        chips = _other_chips(x, y)

        def copy(a, k):
            cx, cy = chips[k]
            return pltpu.make_async_remote_copy(
                src_ref=_chip_slice(src[a], axes[a], 2 * cx + cy, sizes[a]), dst_ref=land[a].at[k],
                send_sem=send_sems.at[a, k], recv_sem=recv_sems.at[a, k], device_id=(cx, cy, c), device_id_type=MESH)

        for a in range(n):
            for k in range(3):
                copy(a, k).start()
        for a in range(n):
            for k in range(3):
                copy(a, k).wait_recv()
        for a in range(n):
            for k in range(3):
                copy(a, k).wait_send()
        big_finish()

    outs = pl.pallas_call(
        body, name="scatter_grads", in_specs=[_HBM] * (n + nsum), out_specs=[_HBM] * (n + nsum),
        out_shape=[_sds((3,) + s) for s in shard_shapes] + [_sds((3,) + s.shape[1:], s.dtype) for s in sums],
        scratch_shapes=[pltpu.SemaphoreType.DMA((n, 3)), pltpu.SemaphoreType.DMA((n, 3))] + _scatter_big_sems(nsum),
        compiler_params=_cparams(),
    )(*fulls, *sums)
    return outs[:n], outs[n:]


def _sibling_swap(name, arrs):
    n = len(arrs)

    def body(*refs):
        src, dst = refs[:n], refs[n:2 * n]
        send_sems, recv_sems = refs[2 * n:]
        x, y, c = _position()
        copies = [pltpu.make_async_remote_copy(src_ref=src[a], dst_ref=dst[a], send_sem=send_sems.at[a], recv_sem=recv_sems.at[a],
                                               device_id=(x, y, 1 - c), device_id_type=MESH) for a in range(n)]
        for cp in copies:
            cp.start()
        for cp in copies:
            cp.wait_recv()
        for cp in copies:
            cp.wait_send()

    return pl.pallas_call(
        body, name=name, in_specs=[_HBM] * n, out_specs=[_HBM] * n, out_shape=[_sds(a.shape) for a in arrs],
        scratch_shapes=[pltpu.SemaphoreType.DMA((n,)), pltpu.SemaphoreType.DMA((n,))],
        compiler_params=_cparams(),
    )(*arrs)


def _sum4(name, own, land):
    rows, cols = own.shape
    tb = _rows_tile(rows)

    def body(o_ref, l0, l1, l2, out_ref):
        out_ref[...] = ((o_ref[...] + l0[...]) + l1[...]) + l2[...]

    blk = pl.BlockSpec((tb, cols), lambda i: (i, 0))
    lands = [pl.BlockSpec((None, tb, cols), functools.partial(lambda k, i: (k, i, 0), k)) for k in range(3)]
    return pl.pallas_call(body, name=name, grid=(rows // tb,), in_specs=[blk] + lands, out_specs=blk,
                          out_shape=_sds((rows, cols)), compiler_params=_cparams(("parallel",)))(own, land, land, land)


def _allreduce_adamw_small(g, w, m, v):
    rows, lanes = g.shape

    def body(g_ref, w_ref, m_ref, v_ref, gs_ref, d_ref, mn_ref, vn_ref, land, send_sems, recv_sems):
        x, y, c = _position()
        me = 4 * x + 2 * y + c
        masks = [(bx, by, bc) for bx in (0, 1) for by in (0, 1) for bc in (0, 1)][1:]

        def peer(mask):
            return (x ^ mask[0], y ^ mask[1], c ^ mask[2])

        def copy(j, slot):
            return pltpu.make_async_remote_copy(src_ref=g_ref, dst_ref=land.at[slot], send_sem=send_sems.at[j],
                                                recv_sem=recv_sems.at[j], device_id=peer(masks[j]), device_id_type=MESH)

        for j in range(7):
            copy(j, me).start()
        land[me] = g_ref[...]
        for j in range(7):
            px, py, pc = peer(masks[j])
            copy(j, 4 * px + 2 * py + pc).wait_recv()
        for j in range(7):
            copy(j, me).wait_send()
        total = land[0]
        for dev in range(1, 8):
            total = total + land[dev]
        delta, mn, vn = _adamw_math(w_ref[...], total, m_ref[...], v_ref[...])
        gs_ref[...] = total
        d_ref[...] = delta
        mn_ref[...] = mn
        vn_ref[...] = vn

    vmem = pl.BlockSpec(memory_space=pltpu.VMEM)
    return pl.pallas_call(
        body, name="allreduce_adamw_small", in_specs=[vmem] * 4, out_specs=[vmem] * 4, out_shape=[_sds((rows, lanes))] * 4,
        scratch_shapes=[pltpu.VMEM((8, rows, lanes), F32), pltpu.SemaphoreType.DMA((7,)), pltpu.SemaphoreType.DMA((7,))],
        compiler_params=_cparams(),
    )(g, w, m, v)


def _row_half(ref, c):
    r2 = ref.shape[-2] // 2
    lead = (slice(None),) * (len(ref.shape) - 2)
    return ref.at[(*lead, pl.ds(pl.multiple_of(c * r2, r2), r2), slice(None))]


def _gather_big_phases(src, dst, sems):
    n = len(src)
    ici_send, ici_recv, d2d_send, d2d_recv, own_sems = sems
    x, y, c = _position()
    me = 2 * x + y
    chips = _other_chips(x, y)
    ids = [2 * cx + cy for cx, cy in chips]

    def ici(a, k, q):
        return pltpu.make_async_remote_copy(
            src_ref=_row_half(src[a], c), dst_ref=_row_half(dst[a].at[q], c), send_sem=ici_send.at[a, k],
            recv_sem=ici_recv.at[a, k], device_id=(*chips[k], c), device_id_type=MESH)

    def d2d(a, k, half):
        where = _row_half(dst[a].at[ids[k]], half)
        return pltpu.make_async_remote_copy(src_ref=where, dst_ref=where, send_sem=d2d_send.at[a, k], recv_sem=d2d_recv.at[a, k],
                                            device_id=(x, y, 1 - c), device_id_type=MESH)

    def own(a):
        return pltpu.make_async_copy(src[a], dst[a].at[me], own_sems.at[a])

    def start():
        for a in range(n):
            own(a).start()
            for k in range(3):
                ici(a, k, me).start()

    def forward():
        for a in range(n):
            for k in range(3):
                ici(a, k, ids[k]).wait_recv()
                d2d(a, k, c).start()

    def finish():
        for a in range(n):
            for k in range(3):
                d2d(a, k, 1 - c).wait_recv()
        for a in range(n):
            for k in range(3):
                ici(a, k, me).wait_send()
                d2d(a, k, c).wait_send()
            own(a).wait()

    return start, forward, finish


def _gather_big_sems(n):
    return [pltpu.SemaphoreType.DMA((n, 3))] * 4 + [pltpu.SemaphoreType.DMA((n,))]


def _chip_sums(tag, grads, core):
    views = [g.reshape(4, -1, g.shape[-1]) for g in grads]
    others = _sibling_halves("sibling_halves_" + tag, views)
    return [_half_add(f"half_add_{tag}{i}", v, o, core) for i, (v, o) in enumerate(zip(views, others))]


def _sibling_halves_phases(src, dst, sems):
    n = len(src)
    send_sems, recv_sems = sems
    x, y, c = _position()

    def copy(a):
        return pltpu.make_async_remote_copy(src_ref=_row_half(src[a], 1 - c), dst_ref=dst[a], send_sem=send_sems.at[a],
                                            recv_sem=recv_sems.at[a], device_id=(x, y, 1 - c), device_id_type=MESH)

    def start():
        for a in range(n):
            copy(a).start()

    def finish():
        for a in range(n):
            copy(a).wait_recv()
        for a in range(n):
            copy(a).wait_send()

    return start, finish


def _sibling_halves_sems(n):
    return [pltpu.SemaphoreType.DMA((n,)), pltpu.SemaphoreType.DMA((n,))]


def _sibling_halves_shapes(views):
    return [_sds((4, v.shape[1] // 2, v.shape[2])) for v in views]


def _sibling_halves(name, views):
    n = len(views)

    def body(*refs):
        for phase in _sibling_halves_phases(refs[:n], refs[n:2 * n], refs[2 * n:]):
            phase()

    return pl.pallas_call(
        body, name=name, in_specs=[_HBM] * n, out_specs=[_HBM] * n, out_shape=_sibling_halves_shapes(views),
        scratch_shapes=_sibling_halves_sems(n), compiler_params=_cparams(),
    )(*views)


def _rows_tile_capped(rows, cap=256):
    return min(_rows_tile(rows), cap)


def _half_add(name, view, other, core):
    _, r, k = view.shape
    r2 = r // 2
    tr = _rows_tile_capped(r2)
    per = r2 // tr

    def body(c_ref, v_ref, o_ref, out_ref):
        out_ref[...] = (v_ref[...] + o_ref[...]).astype(BF16)

    blk = pl.BlockSpec((None, tr, k), lambda q, i, c: (q, i, 0))
    return pl.pallas_call(
        body, name=name,
        grid_spec=pltpu.PrefetchScalarGridSpec(
            num_scalar_prefetch=1, grid=(4, per),
            in_specs=[pl.BlockSpec((None, tr, k), lambda q, i, c: (q, c[0] * per + i, 0)), blk], out_specs=blk),
        out_shape=_sds((4, r2, k), BF16), compiler_params=_cparams(("parallel", "parallel")),
    )(core, view, other)


def _scatter_big_phases(src, land, sems):
    n = len(src)
    send_sems, recv_sems = sems
    x, y, c = _position()
    chips = _other_chips(x, y)

    def copy(a, k):
        cx, cy = chips[k]
        return pltpu.make_async_remote_copy(src_ref=src[a].at[2 * cx + cy], dst_ref=land[a].at[k], send_sem=send_sems.at[a, k],
                                            recv_sem=recv_sems.at[a, k], device_id=(cx, cy, c), device_id_type=MESH)

    def start():
        for a in range(n):
            for k in range(3):
                copy(a, k).start()

    def finish():
        for a in range(n):
            for k in range(3):
                copy(a, k).wait_recv()
        for a in range(n):
            for k in range(3):
                copy(a, k).wait_send()

    return start, finish


def _scatter_big_sems(n):
    return [pltpu.SemaphoreType.DMA((n, 3)), pltpu.SemaphoreType.DMA((n, 3))]


def _sum4_big(name, sums, land, chip):
    _, r2, k = sums.shape
    tr = _rows_tile_capped(r2)

    def body(q_ref, s_ref, l0, l1, l2, out_ref):
        out_ref[...] = ((s_ref[...].astype(F32) + l0[...].astype(F32)) + l1[...].astype(F32)) + l2[...].astype(F32)

    lands = [pl.BlockSpec((None, tr, k), functools.partial(lambda j, i, q: (j, i, 0), j)) for j in range(3)]
    return pl.pallas_call(
        body, name=name,
        grid_spec=pltpu.PrefetchScalarGridSpec(
            num_scalar_prefetch=1, grid=(r2 // tr,),
            in_specs=[pl.BlockSpec((None, tr, k), lambda i, q: (q[0], i, 0))] + lands,
            out_specs=pl.BlockSpec((tr, k), lambda i, q: (i, 0))),
        out_shape=_sds((r2, k)), compiler_params=_cparams(("parallel",)),
    )(chip, sums, land, land, land)


def _adamw_halves(name, mine, theirs, w, m, v, core):
    r, k = w.shape
    r2 = r // 2
    tr = _rows_tile_capped(r2, 512)
    per = r2 // tr

    def body(c_ref, mine_ref, theirs_ref, w_ref, m_ref, v_ref, g_out, d_out, m_out, v_out):
        g = jnp.where(pl.program_id(0) == c_ref[0], mine_ref[...], theirs_ref[...])
        delta, mn, vn = _adamw_math(w_ref[...], g, m_ref[...], v_ref[...])
        g_out[...] = g
        d_out[...] = delta
        m_out[...] = mn
        v_out[...] = vn

    half = pl.BlockSpec((tr, k), lambda h, i, c: (i, 0))
    full = pl.BlockSpec((tr, k), lambda h, i, c: (h * per + i, 0))
    return pl.pallas_call(
        body, name=name,
        grid_spec=pltpu.PrefetchScalarGridSpec(num_scalar_prefetch=1, grid=(2, per), in_specs=[half, half, full, full, full],
                                               out_specs=[full] * 4),
        out_shape=[_sds((r, k))] * 4, compiler_params=_cparams(("parallel", "parallel")),
    )(core, mine, theirs, w, m, v)


def _drops_layer_axis(name):
    return not (name.startswith('mlp') or name == 's5_d')


def _work(name, arr):
    return arr.reshape(arr.shape[1:]) if _drops_layer_axis(name) else arr


def _work_axis(name):
    return SHARD_AXIS[name] - (1 if _drops_layer_axis(name) else 0)


def _as2d(a):
    return a.reshape(-1, a.shape[-1])


def _replicated_2d(name, arr):
    if name in ('ln_g', 'ln_b'):
        return arr
    if name == 'rw_r_k':
        return arr.reshape(1, -1)
    if name == 's5_log_dt':
        return arr.reshape(-1, 1)
    if name.startswith('s5_'):
        return arr.reshape(arr.shape[1:])
    return arr


def _pack(arrs):
    flat = []
    for a in arrs:
        f = a.reshape(-1)
        flat.append(jnp.pad(f, (0, -f.shape[0] % 128)))
    f = jnp.concatenate(flat)
    f = jnp.pad(f, (0, -f.shape[0] % 1024))
    return f.reshape(-1, 128)


def _unpack(packed, shapes):
    flat = packed.reshape(-1)
    out, at = [], 0
    for s in shapes:
        size = math.prod(s)
        out.append(flat[at:at + size].reshape(s))
        at += size + (-size % 128)
    return out


def kernel(x, ln_g, ln_b, rw_mu, rw_w0, rw_w1, rw_w2, rw_a0, rw_a1, rw_a2, rw_g1, rw_g2, rw_k_k, rw_k_a, rw_r_k, rw_wr, rw_wk, rw_wv, rw_wo, rw_lnx_g, rw_lnx_b, s5_a_re, s5_a_im, s5_log_dt, s5_b_re, s5_b_im, s5_c_re, s5_c_im, s5_d, s5_w_glu, mlp_w1, mlp_w2, loss_target, m_ln_g, m_ln_b, m_rw_mu, m_rw_w0, m_rw_w1, m_rw_w2, m_rw_a0, m_rw_a1, m_rw_a2, m_rw_g1, m_rw_g2, m_rw_k_k, m_rw_k_a, m_rw_r_k, m_rw_wr, m_rw_wk, m_rw_wv, m_rw_wo, m_rw_lnx_g, m_rw_lnx_b, m_s5_a_re, m_s5_a_im, m_s5_log_dt, m_s5_b_re, m_s5_b_im, m_s5_c_re, m_s5_c_im, m_s5_d, m_s5_w_glu, m_mlp_w1, m_mlp_w2, v_ln_g, v_ln_b, v_rw_mu, v_rw_w0, v_rw_w1, v_rw_w2, v_rw_a0, v_rw_a1, v_rw_a2, v_rw_g1, v_rw_g2, v_rw_k_k, v_rw_k_a, v_rw_r_k, v_rw_wr, v_rw_wk, v_rw_wv, v_rw_wo, v_rw_lnx_g, v_rw_lnx_b, v_s5_a_re, v_s5_a_im, v_s5_log_dt, v_s5_b_re, v_s5_b_im, v_s5_c_re, v_s5_c_im, v_s5_d, v_s5_w_glu, v_mlp_w1, v_mlp_w2):
    d = dict(locals())
    x_pos, y_pos, c_pos = _position()
    chip = 2 * x_pos + y_pos
    chip_arr = jnp.reshape(chip, (1,)).astype(jnp.int32)
    core_arr = jnp.reshape(c_pos, (1,)).astype(jnp.int32)

    small = [n for n in SHARD_AXIS if n not in BIG]
    axes = [_work_axis(n) for n in small]
    big_views, small_fulls = _gather_early([_as2d(d[n]).astype(BF16) for n in BIG_EARLY], [_work(n, d[n]) for n in small], axes)
    views = dict(zip(BIG_EARLY, big_views))
    fw = dict(zip(small, small_fulls))
    c_model = d['x'].shape[-1]
    for n in BIG_EARLY:
        fw[n] = views[n].reshape(c_model, c_model)
    w1_layers, w2_layers = d['mlp_w1'].astype(BF16), d['mlp_w2'].astype(BF16)
    fw['late_a'] = [_as2d(d['rw_wo']).astype(BF16), w1_layers[0], w2_layers[0]]
    fw['late_b'] = [w1_layers[1]]
    fw['late_c'] = [_as2d(d['s5_w_glu']).astype(BF16), w2_layers[1]]
    for n in REPLICATED:
        fw[n] = _replicated_2d(n, d[n])

    loss_blk, grad_x, grads, reduced, late_sums = _local_step(d['x'][0], d['loss_target'][0], fw, core_arr)
    loss = lax.psum(loss_blk[0, 0], ('x', 'y', 'c'))
    out = {}

    pieces = [grads[n] for n in small]
    lands, late_lands = _scatter_pieces(pieces, axes, list(late_sums.values()))
    reduced.update(zip(late_sums, zip(late_sums.values(), late_lands)))
    mine = [_sum4_big("sum4_" + n, *reduced[n], chip_arr) for n in BIG]
    for n, g, ax, land in zip(small, pieces, axes, lands):
        size = g.shape[ax] // 4
        mine.append(_sum4("sum4_" + n, lax.dynamic_slice_in_dim(g, chip * size, size, ax), land))
    theirs = _sibling_swap("swap_sums", mine)
    for n, a, b in zip(BIG + small, mine, theirs):
        w2d, m2d, v2d = _as2d(d[n]), _as2d(d['m_' + n]), _as2d(d['v_' + n])
        res = (_adamw_halves("adamw_" + n, a, b, w2d, m2d, v2d, core_arr) if n in BIG
               else _adamw("adamw_" + n, (a, b), w2d, m2d, v2d))
        out[n] = [r.reshape(d[n].shape) for r in res]

    rep_shapes = [d[n].shape for n in REPLICATED]
    packs = [_pack([grads[n] for n in REPLICATED])] + [_pack([d[p + n] for n in REPLICATED]) for p in ('', 'm_', 'v_')]
    res = [_unpack(p, rep_shapes) for p in _allreduce_adamw_small(*packs)]
    for i, n in enumerate(REPLICATED):
        out[n] = [r[i] for r in res]

    grad_x = grad_x.reshape(d['x'].shape)
    return (loss, grad_x, *[out[n][0] for n in WEIGHTS], *[out[n][1] for n in WEIGHTS],
            *[out[n][2] for n in WEIGHTS], *[out[n][3] for n in WEIGHTS])
```

```python
import functools
import math

import jax
import jax.numpy as jnp
from jax import lax
from jax.experimental import pallas as pl
from jax.experimental.pallas import tpu as pltpu

F32 = jnp.float32
BF16 = jnp.bfloat16
MESH = pl.DeviceIdType.MESH

HEAD = 64
SSM_GROUP = 16
SSM_STATE = 64
GN_EPS = 64e-5
LN_EPS = 1e-5
DEPTH = 2
DN_ALPHA = (2.0 * DEPTH) ** 0.25
ADAM_LR, ADAM_B1, ADAM_B2, ADAM_EPS, ADAM_WD, ADAM_STEP = 0.001, 0.9, 0.999, 1e-08, 0.01, 10
REC_CHUNK = 64
V7X_VMEM_BYTES = 64 * 2 ** 20
VMEM_LIMIT = V7X_VMEM_BYTES - 8 * 2 ** 20

WEIGHTS = ['ln_g', 'ln_b', 'rw_mu', 'rw_w0', 'rw_w1', 'rw_w2', 'rw_a0', 'rw_a1', 'rw_a2', 'rw_g1', 'rw_g2',
           'rw_k_k', 'rw_k_a', 'rw_r_k', 'rw_wr', 'rw_wk', 'rw_wv', 'rw_wo', 'rw_lnx_g', 'rw_lnx_b',
           's5_a_re', 's5_a_im', 's5_log_dt', 's5_b_re', 's5_b_im', 's5_c_re', 's5_c_im', 's5_d', 's5_w_glu',
           'mlp_w1', 'mlp_w2']
SHARD_AXIS = {'rw_mu': 2, 'rw_w1': 1, 'rw_w2': 2, 'rw_a1': 1, 'rw_a2': 2, 'rw_g1': 1, 'rw_g2': 2,
              'rw_wr': 1, 'rw_wk': 1, 'rw_wv': 1, 'rw_wo': 1, 's5_d': 1, 's5_w_glu': 2, 'mlp_w1': 2, 'mlp_w2': 1}
REPLICATED = [n for n in WEIGHTS if n not in SHARD_AXIS]
BIG_EARLY = ['rw_wr', 'rw_wk', 'rw_wv']
BIG_LATE = ['rw_wo', 's5_w_glu', 'mlp_w1', 'mlp_w2']
BIG = BIG_EARLY + BIG_LATE
BIG_READY = ['s5_w_glu', 'mlp_w1', 'mlp_w2']


def _sds(shape, dtype=F32):
    return jax.ShapeDtypeStruct(tuple(shape), dtype)


def _cparams(sem=None, **kw):
    if sem is not None:
        kw["dimension_semantics"] = sem
    return pltpu.CompilerParams(vmem_limit_bytes=VMEM_LIMIT, **kw)


def _mm_products(a, b, g):
    gb = g.astype(BF16)
    da = lax.dot_general(gb, b.astype(BF16), (((1,), (1,)), ((), ())), preferred_element_type=F32)
    db = lax.dot_general(a.astype(BF16), gb, (((0,), (0,)), ((), ())), preferred_element_type=F32)
    return da, db


@jax.custom_vjp
def _mm_plain(a, b):
    return jnp.dot(a.astype(BF16), b.astype(BF16), preferred_element_type=F32)


def _mm_plain_bwd(res, g):
    da, db = _mm_products(*res, g)
    return da.astype(res[0].dtype), db.astype(res[1].dtype)


_mm_plain.defvjp(lambda a, b: (_mm_plain(a, b), (a, b)), _mm_plain_bwd)


@jax.custom_vjp
def _mm_proxy(a, b, z):
    return jnp.dot(a.astype(BF16), b.astype(BF16), preferred_element_type=F32)


def _mm_proxy_bwd(res, g):
    da, db = _mm_products(*res, g)
    return da.astype(res[0].dtype), jnp.zeros_like(res[1]), db


_mm_proxy.defvjp(lambda a, b, z: (_mm_proxy(a, b, z), (a, b)), _mm_proxy_bwd)


def mm(a, b, z=None):
    return _mm_plain(a, b) if z is None else _mm_proxy(a, b, z)


def _split3(x):
    hi = x.astype(BF16)
    r1 = x - hi.astype(F32)
    mid = r1.astype(BF16)
    lo = (r1 - mid.astype(F32)).astype(BF16)
    return hi, mid, lo


def _head_sum_impl(x):
    c = x.shape[1]
    lanes = 128
    sel = (lax.broadcasted_iota(jnp.int32, (c, lanes), 0) // HEAD
           == lax.broadcasted_iota(jnp.int32, (c, lanes), 1)).astype(BF16)
    s = sum(jnp.dot(p, sel, preferred_element_type=F32) for p in _split3(x))
    return sum(lax.dot_general(p, sel, (((1,), (1,)), ((), ())), preferred_element_type=F32) for p in _split3(s))


@jax.custom_vjp
def head_sum(x):
    return _head_sum_impl(x)


head_sum.defvjp(lambda x: (_head_sum_impl(x), None), lambda _, g: (_head_sum_impl(g),))


def _ln(x, g, b):
    mu = jnp.mean(x, axis=-1, keepdims=True)
    xc = x - mu
    var = jnp.mean(xc * xc, axis=-1, keepdims=True)
    return xc * lax.rsqrt(var + LN_EPS) * g + b


def _f_proj(acts, params, proxies):
    x, xp = acts
    mu, w = params
    return (mm(x + (xp - x) * mu, w, proxies[1]),)


def _f_lora(acts, params, proxies):
    x, xp, kraw = acts
    mu_w, mu_a, mu_g, w0, w1, w2, a0, a1, a2, g1, g2, k_k, k_a = params
    xx = xp - x
    w_pre = w0 + mm(jnp.tanh(mm(x + xx * mu_w, w1)), w2)
    z = -w_pre
    softplus = jnp.maximum(z, 0.0) + jnp.log(1.0 + jnp.exp(-jnp.abs(z)))
    log_decay = -jnp.exp(-softplus - 0.5)
    a = jax.nn.sigmoid(a0 + mm(mm(x + xx * mu_a, a1), a2))
    g = mm(jax.nn.sigmoid(mm(x + xx * mu_g, g1)), g2)
    kk = kraw * k_k
    kkn = kk / jnp.maximum(jnp.sqrt(head_sum(kk * kk)), 1e-12)
    k2 = kraw * (1.0 + (a - 1.0) * k_a)
    return log_decay, k2, -kkn, kkn * a, g


def _f_post(acts, params, proxies):
    o, r, k2, v, g, x = acts
    lnx_g, lnx_b, r_k, wo, ln_g, ln_b = params
    om = head_sum(o) * (1.0 / HEAD)
    oc = o - om
    ov = head_sum(oc * oc) * (1.0 / HEAD)
    on = oc * lax.rsqrt(ov + GN_EPS) * lnx_g + lnx_b
    bonus = head_sum(r * k2 * r_k) * v
    y = mm((on + bonus) * g, wo, proxies[3])
    return (_ln(DN_ALPHA * x + y, ln_g, ln_b),)


def _f_glu(acts, params, proxies):
    ys, h = acts
    d, wv0, wv1, wg0, wg1, ln_g, ln_b = params
    y = jax.nn.gelu(ys + h * d)
    mix = jnp.concatenate([mm(y, wv0, proxies[1]) * jax.nn.sigmoid(mm(y, wg0, proxies[3])),
                           mm(y, wv1, proxies[2]) * jax.nn.sigmoid(mm(y, wg1, proxies[4]))], axis=1)
    return (_ln(DN_ALPHA * h + mix, ln_g, ln_b),)


def _f_zoh(a_re, a_im, log_dt, b_re_t, b_im_t):
    dt = jnp.exp(log_dt)
    lam_re = jnp.minimum(a_re, -1e-4)
    lam_im = a_im
    mag = jnp.exp(dt * lam_re)
    abar_re = mag * jnp.cos(dt * lam_im)
    abar_im = mag * jnp.sin(dt * lam_im)
    den = lam_re * lam_re + lam_im * lam_im
    nr, ni = abar_re - 1.0, abar_im
    coef_re = ((nr * lam_re + ni * lam_im) / den)[:, None, :]
    coef_im = ((ni * lam_re - nr * lam_im) / den)[:, None, :]
    return (abar_re, abar_im, coef_re * b_re_t - coef_im * b_im_t, coef_re * b_im_t + coef_im * b_re_t)


def _bdot16_raw(a, b, ca, cb):
    return lax.dot_general(a.astype(BF16), b.astype(BF16), (((ca,), (cb,)), ((0,), (0,))), preferred_element_type=F32)


@functools.partial(jax.custom_vjp, nondiff_argnums=(2, 3))
def _bdot16(a, b, ca, cb):
    return _bdot16_raw(a, b, ca, cb)


def _bdot16_bwd(ca, cb, res, g):
    a, b = res
    if (ca, cb) == (2, 1):
        return _bdot16_raw(g, b, 2, 2), _bdot16_raw(a, g, 1, 1)
    if (ca, cb) == (2, 2):
        return _bdot16_raw(g, b, 2, 1), _bdot16_raw(g, a, 1, 1)
    assert (ca, cb) == (1, 1)
    return _bdot16_raw(b, g, 2, 2), _bdot16_raw(a, g, 2, 1)


_bdot16.defvjp(lambda a, b, ca, cb: (_bdot16_raw(a, b, ca, cb), (a, b)), _bdot16_bwd)

def _time_sums(x, suffix):
    hg, ln, _ = x.shape
    row = lax.broadcasted_iota(jnp.int32, (hg, ln, ln), 1)
    col = lax.broadcasted_iota(jnp.int32, (hg, ln, ln), 2)
    tri = ((row <= col) if suffix else (row >= col)).astype(BF16)
    return sum(lax.dot_general(tri, p, (((2,), (1,)), ((0,), (0,))), preferred_element_type=F32) for p in _split3(x))


@jax.custom_vjp
def _time_cumsum(x):
    return _time_sums(x, False)


_time_cumsum.defvjp(lambda x: (_time_sums(x, False), None), lambda _, g: (_time_sums(g, True),))

_dot_score = _bdot16
_dot_inverse = _bdot16
_dot_value = _bdot16


def _rec_chunk(s0, r, lw, k, v, a, b):
    hg, ln, _ = r.shape
    row = lax.broadcasted_iota(jnp.int32, (hg, ln, ln), 1)
    col = lax.broadcasted_iota(jnp.int32, (hg, ln, ln), 2)
    incl, strict = row >= col, row > col
    cum = _time_cumsum(lw)
    total = jnp.sum(lw, axis=1, keepdims=True)
    e_cum, e_inv, e_prev, e_tail = jnp.exp(cum), jnp.exp(-cum), jnp.exp(cum - lw), jnp.exp(total - cum)
    rt, at, bt, kt = r * e_cum, a * e_prev, b * e_inv, k * e_inv
    ar = jnp.concatenate([at, rt], axis=1)
    on_b, on_k = _dot_score(ar, bt, 2, 2), _dot_score(ar, kt, 2, 2)
    aab, arb = jnp.where(strict, on_b[:, :ln], 0.0), jnp.where(incl, on_b[:, ln:], 0.0)
    aak, ark = jnp.where(strict, on_k[:, :ln], 0.0), jnp.where(incl, on_k[:, ln:], 0.0)
    p = (row == col).astype(F32) + aab
    m = aab
    for _ in range(int(math.log2(ln)) - 1):
        m = _dot_inverse(m, m, 2, 1)
        p = p + _dot_inverse(p, m, 2, 1)
    from_state = _dot_value(ar, s0, 2, 2)
    from_v = _dot_value(jnp.concatenate([aak, ark], axis=1), v, 2, 1)
    u = _dot_inverse(p, from_state[:, :ln] + from_v[:, :ln], 2, 1)
    o = from_state[:, ln:] + from_v[:, ln:] + _dot_value(arb, u, 2, 1)
    s1 = s0 * jnp.exp(total) + _dot_value(jnp.concatenate([u, v], axis=1),
                                          jnp.concatenate([b * e_tail, k * e_tail], axis=1), 1, 1)
    return o, s1


def _full_spec(shape):
    nd = len(shape)
    return pl.BlockSpec(tuple(shape), lambda *_: (0,) * nd)


def _stage_fwd(name, f, acts, params, out_dims, tb):
    t = acts[0].shape[0]
    na, npar = len(acts), len(params)

    def body(*refs):
        outs = f(tuple(r[...] for r in refs[:na]), tuple(r[...] for r in refs[na:na + npar]), (None,) * npar)
        for r, val in zip(refs[na + npar:], outs):
            r[...] = val

    return pl.pallas_call(
        body, name=name, grid=(t // tb,),
        in_specs=[pl.BlockSpec((tb, a.shape[1]), lambda i: (i, 0)) for a in acts] + [_full_spec(p.shape) for p in params],
        out_specs=[pl.BlockSpec((tb, d), lambda i: (i, 0)) for d in out_dims],
        out_shape=[_sds((t, d)) for d in out_dims],
        compiler_params=_cparams(("arbitrary",)),
    )(*acts, *params)


def _stage_bwd(name, f, acts, params, couts, tb, proxied=(), halves_of=()):
    nh = len(halves_of)
    t = acts[0].shape[0]
    groups = [c if isinstance(c, tuple) else (c,) for c in couts]
    couts = [term for grp in groups for term in grp]
    na, npar, nc = len(acts), len(params), len(couts)
    steps = t // tb

    def f_diff(act_vals, diff_vals, param_vals):
        real = tuple(param_vals[i] if i in proxied else diff_vals[i] for i in range(npar))
        proxies = tuple(diff_vals[i] if i in proxied else None for i in range(npar))
        return f(act_vals, real, proxies)

    def body(*refs):
        a_refs, p_hbm, c_refs = refs[:na], refs[na:na + npar], refs[na + npar:na + npar + nc]
        o = na + npar + nc
        half_src, o = refs[o:o + nh], o + nh
        da_refs, dp_hbm, half_dst = refs[o:o + na], refs[o + na:o + na + npar], refs[o + na + npar:o + na + npar + nh]
        o = o + na + npar + nh
        p_buf, acc, half_sems = refs[o:o + npar], refs[o + npar:o + 2 * npar], refs[o + 2 * npar:]
        i = pl.program_id(0)
        if nh:
            half_start, half_finish = _sibling_halves_phases(half_src, half_dst, half_sems)
            pl.when(i == 0)(half_start)

        @pl.when(i == 0)
        def _():
            for src, dst in zip(p_hbm, p_buf):
                pltpu.sync_copy(src, dst)
            for r in acc:
                r[...] = jnp.zeros_like(r)

        param_vals = tuple(r[...] for r in p_buf)
        diff_vals = tuple(jnp.zeros(v.shape, F32) if i in proxied else v for i, v in enumerate(param_vals))
        _, vjp = jax.vjp(functools.partial(f_diff, param_vals=param_vals), tuple(r[...] for r in a_refs), diff_vals)
        terms = iter(c_refs)
        d_acts, d_params = vjp(tuple(functools.reduce(jnp.add, [next(terms)[...] for _ in grp]) for grp in groups))
        for r, val in zip(da_refs, d_acts):
            r[...] = val
        for r, val in zip(acc, d_params):
            r[...] += val

        @pl.when(i == steps - 1)
        def _():
            for src, dst in zip(acc, dp_hbm):
                pltpu.sync_copy(src, dst)

        if nh:
            pl.when(i == steps - 1)(half_finish)

    hbm = pl.BlockSpec(memory_space=pltpu.HBM)
    outs = pl.pallas_call(
        body, name=name, grid=(steps,),
        in_specs=[pl.BlockSpec((tb, a.shape[1]), lambda i: (i, 0)) for a in acts] + [hbm] * npar
        + [pl.BlockSpec((tb, c.shape[1]), lambda i: (i, 0)) for c in couts] + [hbm] * nh,
        out_specs=[pl.BlockSpec((tb, a.shape[1]), lambda i: (i, 0)) for a in acts] + [hbm] * (npar + nh),
        out_shape=[_sds(a.shape) for a in acts] + [_sds(p.shape) for p in params] + _sibling_halves_shapes(halves_of),
        scratch_shapes=[pltpu.VMEM(p.shape, p.dtype) for p in params] + [pltpu.VMEM(p.shape, F32) for p in params]
        + (_sibling_halves_sems(nh) if nh else []),
        compiler_params=_cparams(("arbitrary",)),
    )(*acts, *params, *couts, *halves_of)
    if nh:
        return outs[:na], outs[na:na + npar], outs[na + npar:]
    return outs[:na], outs[na:]


def _tiled_matmul(name, a, b, mode, grid, a_spec, b_spec, o_spec, out_shape):
    nk = grid[2]
    dims = {"nn": ((1,), (0,)), "nt": ((1,), (1,)), "tn": ((0,), (0,))}[mode]

    def body(a_ref, b_ref, o_ref, acc):
        kk = pl.program_id(2)

        @pl.when(kk == 0)
        def _():
            acc[...] = jnp.zeros_like(acc)

        acc[...] += lax.dot_general(a_ref[...].astype(BF16), b_ref[...].astype(BF16), (dims, ((), ())),
                                    preferred_element_type=F32)

        @pl.when(kk == nk - 1)
        def _():
            o_ref[...] = acc[...]

    return pl.pallas_call(
        body, name=name, grid=grid, in_specs=[a_spec, b_spec], out_specs=o_spec, out_shape=_sds(out_shape),
        scratch_shapes=[pltpu.VMEM(o_spec.block_shape, F32)],
        compiler_params=_cparams(("parallel", "parallel", "arbitrary")),
    )(a, b)


def _mlp_weight_grad(name, a, b, layer, layers, split, into=None, tile=512):
    t, m = a.shape
    n = b.shape[1]
    tile = 2 * tile
    tk = min(tile, t)
    if split == "n":
        tm, tn = min(tile, m), min(tile, n // 4)
        per = n // 4 // tn
        shape = (4, layers, m, n // 4)
        o_idx = lambda i, j, k: (j // per, layer, i, j % per)
    else:
        tm, tn = min(tile, m // 4), min(tile, n)
        per = m // 4 // tm
        shape = (4, layers, m // 4, n)
        o_idx = lambda i, j, k: (i // per, layer, i % per, j)
    nk = t // tk

    def body(a_ref, b_ref, *rest):
        o_ref, acc = rest[-2:]
        kk = pl.program_id(2)

        @pl.when(kk == 0)
        def _():
            acc[...] = jnp.zeros_like(acc)

        acc[...] += lax.dot_general(a_ref[...].astype(BF16), b_ref[...].astype(BF16), (((0,), (0,)), ((), ())),
                                    preferred_element_type=F32)

        @pl.when(kk == nk - 1)
        def _():
            o_ref[...] = acc[...]

    in_specs = [pl.BlockSpec((tk, tm), lambda i, j, k: (k, i)), pl.BlockSpec((tk, tn), lambda i, j, k: (k, j))]
    operands = [a, b]
    aliases = {}
    if into is not None:
        in_specs.append(pl.BlockSpec(memory_space=pl.ANY))
        operands.append(into)
        aliases = {2: 0}
    return pl.pallas_call(
        body, name=name, grid=(m // tm, n // tn, nk), in_specs=in_specs,
        out_specs=pl.BlockSpec((None, None, tm, tn), o_idx), out_shape=_sds(shape), input_output_aliases=aliases,
        scratch_shapes=[pltpu.VMEM((tm, tn), F32)],
        compiler_params=_cparams(("parallel", "parallel", "arbitrary")),
    )(*operands)


S5_PACK = 8


def _s5_weight_grad(name, x, s, wide_rows, tk):
    t, c = x.shape
    wide = s.shape[1]
    kb, nb = S5_PACK * SSM_GROUP, S5_PACK * SSM_STATE
    nsb = c // kb
    x_spec = pl.BlockSpec((tk, kb), lambda i, j, k: (k, j % nsb))
    s_spec = pl.BlockSpec((tk, nb), lambda i, j, k: (k, j))
    if wide_rows:
        return _tiled_matmul(name, s, x, "tn", (1, wide // nb, t // tk), s_spec, x_spec,
                             pl.BlockSpec((nb, kb), lambda i, j, k: (j, 0)), (wide, kb))
    return _tiled_matmul(name, x, s, "tn", (1, wide // nb, t // tk), x_spec, s_spec,
                         pl.BlockSpec((kb, nb), lambda i, j, k: (0, j)), (kb, wide))


def _mlp_fwd(name, h, w1, w2, layer, ln_g, ln_b, tb, shards=()):
    t, c = h.shape
    nj, fc = w1.shape[0], w1.shape[3]
    nsh = len(shards)
    steps = (t // tb) * nj

    def body(h_ref, w1_ref, w2_ref, g_ref, b_ref, *rest):
        src, (out_ref, s_ref), dst = rest[:nsh], rest[nsh:nsh + 2], rest[nsh + 2:2 * nsh + 2]
        acc, sems = rest[2 * nsh + 2], rest[2 * nsh + 3:]
        j = pl.program_id(1)
        step = pl.program_id(0) * nj + j
        if nsh:
            start, forward, finish = _gather_big_phases(src, dst, sems)
            pl.when(step == 0)(start)

        @pl.when(j == 0)
        def _():
            acc[...] = jnp.zeros_like(acc)

        hid = jnp.dot(h_ref[...].astype(BF16), w1_ref[...].astype(BF16), preferred_element_type=F32)
        act = jnp.square(jnp.maximum(hid, 0.0))
        acc[...] += jnp.dot(act.astype(BF16), w2_ref[...].astype(BF16), preferred_element_type=F32)

        @pl.when(j == nj - 1)
        def _():
            s = DN_ALPHA * h_ref[...] + acc[...]
            s_ref[...] = s
            out_ref[...] = _ln(s, g_ref[...], b_ref[...])

        if nsh:
            pl.when(step == steps // 2)(forward)
            pl.when(step == steps - 1)(finish)

    row = pl.BlockSpec((tb, c), lambda i, j: (i, 0))
    vec = pl.BlockSpec((1, c), lambda i, j: (0, 0))
    outs = pl.pallas_call(
        body, name=name, grid=(t // tb, nj),
        in_specs=[row, pl.BlockSpec((None, None, c, fc), lambda i, j: (j, layer, 0, 0)),
                  pl.BlockSpec((None, None, fc, c), lambda i, j: (j, layer, 0, 0)), vec, vec] + [_HBM] * nsh,
        out_specs=[row, row] + [_HBM] * nsh,
        out_shape=[_sds((t, c)), _sds((t, c))] + [_sds((4,) + a.shape, a.dtype) for a in shards],
        scratch_shapes=[pltpu.VMEM((tb, c), F32)] + (_gather_big_sems(nsh) if nsh else []),
        compiler_params=_cparams(("arbitrary", "arbitrary")),
    )(h, w1, w2, ln_g, ln_b, *shards)
    return outs[0], outs[1], outs[2:]


def _mlp_bwd(name, h, s, dout, w1, w2, layer, ln_g, ln_b, tb):
    t, c = h.shape
    nj, fc = w1.shape[0], w1.shape[3]
    ff = nj * fc
    ni = t // tb
    nt = (((1,), (1,)), ((), ()))
    douts = dout if isinstance(dout, tuple) else (dout,)
    nd = len(douts)

    def body(h_ref, s_ref, *rest):
        dout_refs = rest[:nd]
        (w1_ref, w2_ref, g_ref, b_ref, dh_ref, ds_ref, dhid_ref, act_ref, dg_ref, db_ref,
         ds_scr, dh_acc, dg_acc, db_acc) = rest[nd:]
        i, j = pl.program_id(0), pl.program_id(1)

        @pl.when((i == 0) & (j == 0))
        def _():
            dg_acc[...] = jnp.zeros_like(dg_acc)
            db_acc[...] = jnp.zeros_like(db_acc)

        @pl.when(j == 0)
        def _():
            _, vjp = jax.vjp(_ln, s_ref[...], g_ref[...], b_ref[...])
            ds, dg, db = vjp(functools.reduce(jnp.add, [r[...] for r in dout_refs]))
            ds_scr[...] = ds
            ds_ref[...] = ds.astype(BF16)
            dh_acc[...] = DN_ALPHA * ds
            dg_acc[...] += dg
            db_acc[...] += db

        w1b, w2b = w1_ref[...].astype(BF16), w2_ref[...].astype(BF16)
        hid = jnp.dot(h_ref[...].astype(BF16), w1b, preferred_element_type=F32)
        rl = jnp.maximum(hid, 0.0)
        dact = lax.dot_general(ds_scr[...].astype(BF16), w2b, nt, preferred_element_type=F32)
        dhid = (dact * 2.0 * rl).astype(BF16)
        dh_acc[...] += lax.dot_general(dhid, w1b, nt, preferred_element_type=F32)
        dhid_ref[...] = dhid
        act_ref[...] = (rl * rl).astype(BF16)

        @pl.when(j == nj - 1)
        def _():
            dh_ref[...] = dh_acc[...]

        @pl.when((i == ni - 1) & (j == nj - 1))
        def _():
            dg_ref[...] = dg_acc[...]
            db_ref[...] = db_acc[...]

    row = pl.BlockSpec((tb, c), lambda i, j: (i, 0))
    vec = pl.BlockSpec((1, c), lambda i, j: (0, 0))
    wide = pl.BlockSpec((tb, fc), lambda i, j: (i, j))
    return pl.pallas_call(
        body, name=name, grid=(ni, nj),
        in_specs=[row, row] + [row] * nd + [pl.BlockSpec((None, None, c, fc), lambda i, j: (j, layer, 0, 0)),
                                            pl.BlockSpec((None, None, fc, c), lambda i, j: (j, layer, 0, 0)), vec, vec],
        out_specs=[row, row, wide, wide, vec, vec],
        out_shape=[_sds((t, c)), _sds((t, c), BF16), _sds((t, ff), BF16), _sds((t, ff), BF16), _sds((1, c)), _sds((1, c))],
        scratch_shapes=[pltpu.VMEM((tb, c), F32), pltpu.VMEM((tb, c), F32), pltpu.VMEM((1, c), F32), pltpu.VMEM((1, c), F32)],
        compiler_params=_cparams(("arbitrary", "arbitrary")),
    )(h, s, *douts, w1, w2, ln_g, ln_b)


def _load_heads(ref, hg):
    return jnp.stack([ref[:, h * HEAD:(h + 1) * HEAD] for h in range(hg)])


def _store_heads(ref, val):
    for h in range(val.shape[0]):
        ref[:, h * HEAD:(h + 1) * HEAD] = val[h]


def _rec_fwd(r, lw, k, v, a, b, hg, shards):
    t, c = r.shape
    n = HEAD
    nh = c // n
    ln = REC_CHUNK
    nck = t // ln
    ngrp = nh // hg
    nsh = len(shards)
    steps = ngrp * nck

    def body(r_ref, lw_ref, k_ref, v_ref, a_ref, b_ref, *rest):
        src, (o_ref, s0_ref), dst = rest[:nsh], rest[nsh:nsh + 2], rest[nsh + 2:2 * nsh + 2]
        state, sems = rest[2 * nsh + 2], rest[2 * nsh + 3:]
        step = pl.program_id(0) * nck + pl.program_id(1)
        start, forward, finish = _gather_big_phases(src, dst, sems)
        pl.when(step == 0)(start)

        @pl.when(pl.program_id(1) == 0)
        def _():
            state[...] = jnp.zeros_like(state)

        s0 = state[...]
        s0_ref[...] = s0
        o, s1 = _rec_chunk(s0, *(_load_heads(x, hg) for x in (r_ref, lw_ref, k_ref, v_ref, a_ref, b_ref)))
        _store_heads(o_ref, o)
        state[...] = s1
        pl.when(step == steps // 2)(forward)
        pl.when(step == steps - 1)(finish)

    seq = pl.BlockSpec((ln, hg * n), lambda g, i: (i, g))
    outs = pl.pallas_call(
        body, name="rec_fwd", grid=(ngrp, nck), in_specs=[seq] * 6 + [_HBM] * nsh,
        out_specs=[seq, pl.BlockSpec((None, hg, n, n), lambda g, i: (i, g, 0, 0))] + [_HBM] * nsh,
        out_shape=[_sds((t, c)), _sds((nck, nh, n, n))] + [_sds((4,) + s.shape, s.dtype) for s in shards],
        scratch_shapes=[pltpu.VMEM((hg, n, n), F32)] + _gather_big_sems(nsh),
        compiler_params=_cparams(("arbitrary", "arbitrary")),
    )(r, lw, k, v, a, b, *shards)
    return outs[0], outs[1], outs[2:]


def _rec_bwd(r, lw, k, v, a, b, s0s, do, hg, chip_sums):
    t, c = r.shape
    n = HEAD
    nh = c // n
    ln = REC_CHUNK
    nck = t // ln
    ngrp = nh // hg
    nsum = len(chip_sums)
    steps = ngrp * nck

    def body(r_ref, lw_ref, k_ref, v_ref, a_ref, b_ref, s0_ref, do_ref, *rest):
        src, grad_refs, land = rest[:nsum], rest[nsum:nsum + 6], rest[nsum + 6:2 * nsum + 6]
        dstate, sems = rest[2 * nsum + 6], rest[2 * nsum + 7:]
        step = pl.program_id(0) * nck + pl.program_id(1)
        start, finish = _scatter_big_phases(src, land, sems)
        pl.when(step == 0)(start)

        @pl.when(pl.program_id(1) == 0)
        def _():
            dstate[...] = jnp.zeros_like(dstate)

        _, vjp = jax.vjp(_rec_chunk, s0_ref[...], *(_load_heads(x, hg) for x in (r_ref, lw_ref, k_ref, v_ref, a_ref, b_ref)))
        ds0, *grads = vjp((_load_heads(do_ref, hg), dstate[...]))
        dstate[...] = ds0
        for ref, val in zip(grad_refs, grads):
            _store_heads(ref, val)
        pl.when(step == steps - 1)(finish)

    seq = pl.BlockSpec((ln, hg * n), lambda g, i: (nck - 1 - i, g))
    outs = pl.pallas_call(
        body, name="rec_bwd", grid=(ngrp, nck),
        in_specs=[seq] * 6 + [pl.BlockSpec((None, hg, n, n), lambda g, i: (nck - 1 - i, g, 0, 0)), seq] + [_HBM] * nsum,
        out_specs=[seq] * 6 + [_HBM] * nsum,
        out_shape=[_sds((t, c))] * 6 + [_sds((3,) + s.shape[1:], s.dtype) for s in chip_sums],
        scratch_shapes=[pltpu.VMEM((hg, n, n), F32)] + _scatter_big_sems(nsum),
        compiler_params=_cparams(("arbitrary", "arbitrary")),
    )(r, lw, k, v, a, b, s0s, do, *chip_sums)
    return outs[:6], outs[6:]


def _s5_blocks(c):
    kb, nb = S5_PACK * SSM_GROUP, S5_PACK * SSM_STATE
    return kb, nb, c // kb


def _s5_fwd(h, bc, abar, cc, tb, shards=()):
    t, c = h.shape
    w2 = bc.shape[1]
    w = w2 // 2
    kb, nb, nsb = _s5_blocks(c)

    nsh = len(shards)
    steps = t // tb

    def body(h_ref, bc_ref, a_ref, cc_ref, *rest):
        src, (s_ref, y_ref), dst = rest[:nsh], rest[nsh:nsh + 2], rest[nsh + 2:2 * nsh + 2]
        carry, rows, sems = rest[2 * nsh + 2], rest[2 * nsh + 3], rest[2 * nsh + 4:]
        if nsh:
            start, forward, finish = _gather_big_phases(src, dst, sems)
            pl.when(pl.program_id(0) == 0)(start)

        @pl.when(pl.program_id(0) == 0)
        def _():
            carry[...] = jnp.zeros_like(carry)

        for j in range(w2 // nb):
            ch = (j % nsb) * kb
            rows[:, j * nb:(j + 1) * nb] = jnp.dot(h_ref[:, ch:ch + kb].astype(BF16), bc_ref[:, j * nb:(j + 1) * nb],
                                                   preferred_element_type=F32)
        ar, ai = a_ref[:, :w], a_ref[:, w:]

        def step(i, state):
            hr, hi = state
            nr = ar * hr - ai * hi + rows[pl.ds(i, 1), :w]
            ni = ar * hi + ai * hr + rows[pl.ds(i, 1), w:]
            rows[pl.ds(i, 1), :w] = nr
            rows[pl.ds(i, 1), w:] = ni
            return nr, ni

        hr, hi = lax.fori_loop(0, tb, step, (carry[:, :w], carry[:, w:]))
        carry[:, :w] = hr
        carry[:, w:] = hi
        s_ref[...] = rows[...].astype(BF16)
        for j in range(nsb):
            re, im = j * nb, w + j * nb
            y_ref[:, j * kb:(j + 1) * kb] = (
                jnp.dot(s_ref[:, re:re + nb], cc_ref[re:re + nb, :], preferred_element_type=F32)
                + jnp.dot(s_ref[:, im:im + nb], cc_ref[im:im + nb, :], preferred_element_type=F32))
        if nsh:
            pl.when(pl.program_id(0) == steps // 2)(forward)
            pl.when(pl.program_id(0) == steps - 1)(finish)

    outs = pl.pallas_call(
        body, name="s5_fwd", grid=(steps,),
        in_specs=[pl.BlockSpec((tb, c), lambda i: (i, 0)), _full_spec(bc.shape), _full_spec(abar.shape), _full_spec(cc.shape)]
        + [_HBM] * nsh,
        out_specs=[pl.BlockSpec((tb, w2), lambda i: (i, 0)), pl.BlockSpec((tb, c), lambda i: (i, 0))] + [_HBM] * nsh,
        out_shape=[_sds((t, w2), BF16), _sds((t, c))] + [_sds((4,) + a.shape, a.dtype) for a in shards],
        scratch_shapes=[pltpu.VMEM((1, w2), F32), pltpu.VMEM((tb, w2), F32)] + (_gather_big_sems(nsh) if nsh else []),
        compiler_params=_cparams(("arbitrary",)),
    )(h, bc, abar, cc, *shards)
    return outs[0], outs[1], outs[2:]


def _s5_bwd(dy, s, abar, cc, bc, tb):
    t, c = dy.shape
    w2 = s.shape[1]
    w = w2 // 2
    kb, nb, nsb = _s5_blocks(c)
    nblk = t // tb
    pack = 16
    per = tb // pack
    nt = (((1,), (1,)), ((), ()))

    def body(dy_ref, s_ref, sprev_ref, a_ref, cc_ref, bc_ref, dbu_ref, dh_ref, da_ref, carry, da_acc, rows):
        i = pl.program_id(0)

        @pl.when(i == 0)
        def _():
            carry[...] = jnp.zeros_like(carry)
            da_acc[...] = jnp.zeros_like(da_acc)

        for j in range(w2 // nb):
            ch = (j % nsb) * kb
            rows[:, j * nb:(j + 1) * nb] = lax.dot_general(dy_ref[:, ch:ch + kb].astype(BF16), cc_ref[j * nb:(j + 1) * nb, :], nt,
                                                           preferred_element_type=F32)
        ar, ai = a_ref[:, :w], a_ref[:, w:]

        def step(n, state):
            gr, gi = state
            row = tb - 1 - n
            nr = rows[pl.ds(row, 1), :w] + ar * gr + ai * gi
            ni = rows[pl.ds(row, 1), w:] + ar * gi - ai * gr
            rows[pl.ds(row, 1), :w] = nr
            rows[pl.ds(row, 1), w:] = ni
            return nr, ni

        gr, gi = lax.fori_loop(0, tb, step, (carry[:, :w], carry[:, w:]))
        carry[:, :w] = gr
        carry[:, w:] = gi
        last = (lax.broadcasted_iota(jnp.int32, (pack, w2), 0) == pack - 1) & (i < nblk - 1)
        before = jnp.sum(jnp.where(last, sprev_ref[...].astype(F32), 0.0), axis=0, keepdims=True)
        rid = lax.broadcasted_iota(jnp.int32, (tb, w2), 0)
        sp = jnp.where(rid == 0, before, pltpu.roll(s_ref[...].astype(F32), 1, 0))
        g = rows[...]
        dbu_ref[...] = g.astype(BF16)
        spr, spi, g_r, g_i = sp[:, :w], sp[:, w:], g[:, :w], g[:, w:]
        da_acc[:, :w] += jnp.sum(spr * g_r + spi * g_i, axis=0, keepdims=True)
        da_acc[:, w:] += jnp.sum(spr * g_i - spi * g_r, axis=0, keepdims=True)
        for j in range(nsb):
            re, im = j * nb, w + j * nb
            dh_ref[:, j * kb:(j + 1) * kb] = (
                lax.dot_general(dbu_ref[:, re:re + nb], bc_ref[:, re:re + nb], nt, preferred_element_type=F32)
                + lax.dot_general(dbu_ref[:, im:im + nb], bc_ref[:, im:im + nb], nt, preferred_element_type=F32))

        @pl.when(i == nblk - 1)
        def _():
            da_ref[...] = da_acc[...]

    wide = pl.BlockSpec((tb, w2), lambda i: (nblk - 1 - i, 0))
    narrow = pl.BlockSpec((tb, c), lambda i: (nblk - 1 - i, 0))
    prev = pl.BlockSpec((pack, w2), lambda i: (jnp.maximum((nblk - 1 - i) * per - 1, 0), 0))
    return pl.pallas_call(
        body, name="s5_bwd", grid=(nblk,),
        in_specs=[narrow, wide, prev, _full_spec(abar.shape), _full_spec(cc.shape), _full_spec(bc.shape)],
        out_specs=[wide, narrow, pl.BlockSpec((1, w2), lambda i: (0, 0))],
        out_shape=[_sds((t, w2), BF16), _sds((t, c)), _sds((1, w2))],
        scratch_shapes=[pltpu.VMEM((1, w2), F32), pltpu.VMEM((1, w2), F32), pltpu.VMEM((tb, w2), F32)],
        compiler_params=_cparams(("arbitrary",)),
    )(dy, s, s, abar, cc, bc)


def _zoh_fwd(a_re, a_im, log_dt, b_re_t, b_im_t):
    def body(*refs):
        for r, val in zip(refs[5:], _f_zoh(*(x[...] for x in refs[:5]))):
            r[...] = val

    return pl.pallas_call(body, name="s5_zoh_fwd", out_shape=[_sds(a_re.shape)] * 2 + [_sds(b_re_t.shape)] * 2,
                          compiler_params=_cparams())(a_re, a_im, log_dt, b_re_t, b_im_t)


def _zoh_bwd(a_re, a_im, log_dt, b_re_t, b_im_t, couts):
    def body(*refs):
        _, vjp = jax.vjp(_f_zoh, *(x[...] for x in refs[:5]))
        for r, val in zip(refs[9:], vjp(tuple(x[...] for x in refs[5:9]))):
            r[...] = val

    ins = (a_re, a_im, log_dt, b_re_t, b_im_t)
    return pl.pallas_call(body, name="s5_zoh_bwd", out_shape=[_sds(x.shape) for x in ins],
                          compiler_params=_cparams())(*ins, *couts)


def _loss_head(h, target, tb):
    t, c = h.shape
    nb = t // tb

    def body(h_ref, t_ref, loss_ref, dh_ref, acc):
        i = pl.program_id(0)

        @pl.when(i == 0)
        def _():
            acc[...] = jnp.zeros_like(acc)

        d = h_ref[...] - t_ref[...]
        dh_ref[...] = d * (1.0 / c)
        acc[...] += 0.5 * jnp.sum(jnp.mean(d * d, axis=-1, keepdims=True), axis=0, keepdims=True)

        @pl.when(i == nb - 1)
        def _():
            loss_ref[...] = jnp.broadcast_to(acc[...], loss_ref.shape)

    row = pl.BlockSpec((tb, c), lambda i: (i, 0))
    return pl.pallas_call(
        body, name="loss_head", grid=(nb,), in_specs=[row, row],
        out_specs=[pl.BlockSpec((8, 128), lambda i: (0, 0)), row], out_shape=[_sds((8, 128)), _sds((t, c))],
        scratch_shapes=[pltpu.VMEM((1, 1), F32)], compiler_params=_cparams(("arbitrary",)),
    )(h, target)


def _rows_tile(rows):
    for cand in (512, 256, 128, 64, 32, 16, 8):
        if rows % cand == 0:
            return cand
    return rows


def _grad_x(here, from_next):
    rows, cols = here[0].shape
    tb = _rows_tile(rows)
    nb = rows // tb
    nh, nn = len(here), len(from_next)
    sub = 8

    def body(*refs):
        i = pl.program_id(0)
        total = functools.reduce(jnp.add, [r[...] for r in refs[:nh]])
        shifted = functools.reduce(jnp.add, [r[...] for r in refs[nh:nh + nn]])
        first_next = functools.reduce(jnp.add, [r[0:1, :] for r in refs[nh + nn:nh + 2 * nn]])
        first_next = jnp.where(i == nb - 1, 0.0, first_next)
        rid = lax.broadcasted_iota(jnp.int32, (tb, cols), 0)
        refs[-1][...] = total + jnp.where(rid == tb - 1, first_next, pltpu.roll(shifted, tb - 1, 0))

    blk = pl.BlockSpec((tb, cols), lambda i: (i, 0))
    nxt = pl.BlockSpec((sub, cols), lambda i: (jnp.minimum(i + 1, nb - 1) * (tb // sub), 0))
    return pl.pallas_call(body, name="grad_x", grid=(nb,), in_specs=[blk] * (nh + nn) + [nxt] * nn, out_specs=blk,
                          out_shape=_sds((rows, cols)), compiler_params=_cparams(("parallel",)))(*here, *from_next, *from_next)


def _adamw_math(w, g, m, v):
    m = ADAM_B1 * m + (1.0 - ADAM_B1) * g
    v = ADAM_B2 * v + (1.0 - ADAM_B2) * jnp.square(g)
    m_hat = m / (1.0 - ADAM_B1 ** ADAM_STEP)
    v_hat = v / (1.0 - ADAM_B2 ** ADAM_STEP)
    delta = -ADAM_LR * (m_hat / (jnp.sqrt(v_hat) + ADAM_EPS) + ADAM_WD * w)
    return delta, m, v


def _adamw(name, parts, w, m, v):
    rows, cols = w.shape
    tb = _rows_tile(rows)
    npart = len(parts)

    def body(*refs):
        g = refs[0][...]
        for r in refs[1:npart]:
            g = g + r[...]
        w_ref, m_ref, v_ref = refs[npart:npart + 3]
        g_out, d_out, m_out, v_out = refs[npart + 3:]
        delta, mn, vn = _adamw_math(w_ref[...], g, m_ref[...], v_ref[...])
        g_out[...] = g
        d_out[...] = delta
        m_out[...] = mn
        v_out[...] = vn

    blk = pl.BlockSpec((tb, cols), lambda i: (i, 0))
    return pl.pallas_call(body, name=name, grid=(rows // tb,), in_specs=[blk] * (npart + 3), out_specs=[blk] * 4,
                          out_shape=[_sds((rows, cols))] * 4, compiler_params=_cparams(("parallel",)))(*parts, w, m, v)


def _shift_down(a):
    return jnp.concatenate([jnp.zeros_like(a[:1]), a[:-1]], axis=0)


def _s5_pack_mask(g):
    return (jnp.arange(g)[None, :] % S5_PACK == jnp.arange(S5_PACK)[:, None]).astype(F32)


def _compact_b(bbar_t):
    g, s, p = bbar_t.shape
    return (_s5_pack_mask(g)[:, None, :, None] * bbar_t.transpose(1, 0, 2)[None]).reshape(S5_PACK * s, g * p)


def _compact_b_t(dense, g):
    s, p = dense.shape[0] // S5_PACK, dense.shape[1] // g
    return jnp.sum(dense.reshape(S5_PACK, s, g, p) * _s5_pack_mask(g)[:, None, :, None], axis=0).transpose(1, 0, 2)


def _compact_c(c_w):
    g, s, p = c_w.shape
    return (c_w.transpose(0, 2, 1)[:, :, None, :] * _s5_pack_mask(g).T[:, None, :, None]).reshape(g * p, S5_PACK * s)


def _compact_c_t(dense, g):
    p, s = dense.shape[0] // g, dense.shape[1] // S5_PACK
    return jnp.sum(dense.reshape(g, p, S5_PACK, s) * _s5_pack_mask(g).T[:, None, :, None], axis=2).transpose(0, 2, 1)


def _local_step(x, target, fw, core):
    t, c = x.shape
    nh = c // HEAD
    ng = c // SSM_GROUP
    tb = min(256, t)
    tbm = min(512, t)
    tbmb = min(512, t)
    tbs = min(256, t)
    tk5 = min(2048, t)
    hg = min(16, nh)
    mu = [fw['rw_mu'][i:i + 1] for i in range(6)]
    ln_g = [fw['ln_g'][i:i + 1] for i in range(4)]
    ln_b = [fw['ln_b'][i:i + 1] for i in range(4)]
    grads = {}

    xp = _shift_down(x)
    proj_params = {n: (mu[i], fw['rw_w' + n]) for n, i in (('r', 0), ('k', 2), ('v', 3))}
    raw = {n: _stage_fwd("proj_" + n, _f_proj, (x, xp), proj_params[n], (c,), tbm)[0] for n in 'rkv'}
    lora_params = (mu[1], mu[4], mu[5], fw['rw_w0'], fw['rw_w1'], fw['rw_w2'], fw['rw_a0'], fw['rw_a1'], fw['rw_a2'],
                   fw['rw_g1'], fw['rw_g2'], fw['rw_k_k'], fw['rw_k_a'])
    lw, k2, an, bb, gate = _stage_fwd("lora", _f_lora, (x, xp, raw['k']), lora_params, (c,) * 5, tbm)
    rec_in = (raw['r'], lw, k2, raw['v'], an, bb)
    o, s0s, (wo_view, w1_l0, w2_l0) = _rec_fwd(*rec_in, hg, fw['late_a'])
    fw = dict(fw, rw_wo=wo_view.reshape(c, c))
    mlp_w = [(w1_l0.reshape(4, 1, c, -1), w2_l0.reshape(4, 1, -1, c)), None]
    post_params = (fw['rw_lnx_g'], fw['rw_lnx_b'], fw['rw_r_k'], fw['rw_wo'], ln_g[0], ln_b[0])
    post_acts = (o, raw['r'], k2, raw['v'], gate, x)
    h1, = _stage_fwd("post", _f_post, post_acts, post_params, (c,), tb)
    h2, s_mlp0, (w1_l1,) = _mlp_fwd("mlp0_fwd", h1, *mlp_w[0], 0, ln_g[1], ln_b[1], tbm, fw['late_b'])

    a_re, a_im, log_dt = fw['s5_a_re'], fw['s5_a_im'], fw['s5_log_dt']
    b_re_t, b_im_t = fw['s5_b_re'].transpose(0, 2, 1), fw['s5_b_im'].transpose(0, 2, 1)
    abar_re, abar_im, bbar_re_t, bbar_im_t = _zoh_fwd(a_re, a_im, log_dt, b_re_t, b_im_t)
    abar = jnp.concatenate([abar_re.reshape(1, -1), abar_im.reshape(1, -1)], axis=1)
    bc = jnp.concatenate([_compact_b(bbar_re_t), _compact_b(bbar_im_t)], axis=1).astype(BF16)
    cc = jnp.concatenate([_compact_c(fw['s5_c_re']), -_compact_c(fw['s5_c_im'])], axis=0).astype(BF16)
    st, ys, (glu_view, w2_l1) = _s5_fwd(h2, bc, abar, cc, tbs, fw['late_c'])
    mlp_w[1] = (w1_l1.reshape(4, 1, c, -1), w2_l1.reshape(4, 1, -1, c))
    fw = dict(fw, s5_w_glu=tuple(glu_view[q] for q in range(4)))
    glu_params = (fw['s5_d'], *fw['s5_w_glu'], ln_g[2], ln_b[2])
    h3, = _stage_fwd("glu", _f_glu, (ys, h2), glu_params, (c,), tbm)
    h4, s_mlp1, _ = _mlp_fwd("mlp1_fwd", h3, *mlp_w[1], 0, ln_g[3], ln_b[3], tbm)

    loss_blk, dh4 = _loss_head(h4, target, tb)

    dln_g, dln_b = [None] * 4, [None] * 4
    dh3, ds1, dhid1, act1, dln_g[3], dln_b[3] = _mlp_bwd("mlp1_bwd", h3, s_mlp1, dh4, *mlp_w[1], 0,
                                                         ln_g[3], ln_b[3], tbmb)
    dw1 = _mlp_weight_grad("mlp1_dw1", h3, dhid1, 1, DEPTH, "n")
    dw2 = _mlp_weight_grad("mlp1_dw2", act1, ds1, 1, DEPTH, "m")
    (dys, dh2_glu), (grads['s5_d'], *dglu, dln_g[2], dln_b[2]) = _stage_bwd(
        "glu_bwd", _f_glu, (ys, h2), glu_params, (dh3,), tb, proxied=(1, 2, 3, 4))
    grads['s5_w_glu'] = jnp.stack(dglu)
    dcc = _s5_weight_grad("s5_dcc", dys, st, True, tk5)
    dbu, dh2_bu, dabar = _s5_bwd(dys, st, abar, cc, bc, tbs)
    dbc = _s5_weight_grad("s5_dbc", h2, dbu, False, tk5)
    gp = ng * SSM_STATE
    grads['s5_c_re'] = _compact_c_t(dcc[:gp], ng)
    grads['s5_c_im'] = -_compact_c_t(dcc[gp:], ng)
    zoh_couts = (dabar[:, :gp].reshape(ng, SSM_STATE), dabar[:, gp:].reshape(ng, SSM_STATE),
                 _compact_b_t(dbc[:, :gp], ng), _compact_b_t(dbc[:, gp:], ng))
    grads['s5_a_re'], grads['s5_a_im'], grads['s5_log_dt'], db_re_t, db_im_t = _zoh_bwd(
        a_re, a_im, log_dt, b_re_t, b_im_t, zoh_couts)
    grads['s5_b_re'], grads['s5_b_im'] = db_re_t.transpose(0, 2, 1), db_im_t.transpose(0, 2, 1)
    dh2 = (dh2_glu, dh2_bu)

    dh1, ds0, dhid0, act0, dln_g[1], dln_b[1] = _mlp_bwd("mlp0_bwd", h1, s_mlp0, dh2, *mlp_w[0], 0,
                                                         ln_g[1], ln_b[1], tbmb)
    grads['mlp_w1'] = _mlp_weight_grad("mlp0_dw1", h1, dhid0, 0, DEPTH, "n", into=dw1)
    grads['mlp_w2'] = _mlp_weight_grad("mlp0_dw2", act0, ds0, 0, DEPTH, "m", into=dw2)
    ready_views = [grads[n].reshape(4, -1, grads[n].shape[-1]) for n in BIG_READY]
    (do, dr_p, dk2_p, dv_p, dgate, dx_post), post_g, ready_others = _stage_bwd(
        "post_bwd", _f_post, post_acts, post_params, (dh1,), tb, proxied=(3,), halves_of=ready_views)
    grads['rw_lnx_g'], grads['rw_lnx_b'], grads['rw_r_k'], grads['rw_wo'], dln_g[0], dln_b[0] = post_g
    ready_sums = [_half_add(f"half_add_a{i}", v, o, core) for i, (v, o) in enumerate(zip(ready_views, ready_others))]
    rec_g, ready_lands = _rec_bwd(*rec_in, s0s, do, hg, ready_sums)
    reduced = dict(zip(BIG_READY, zip(ready_sums, ready_lands)))
    dr_r, dlw, dk2_r, dv_r, dan, dbb = rec_g
    dk2 = (dk2_p, dk2_r)
    (dx_l, dxp_l, dkraw_l), lora_g = _stage_bwd("lora_bwd", _f_lora, (x, xp, raw['k']), lora_params,
                                                (dlw, dk2, dan, dbb, dgate), tb)
    (dmu_w, dmu_a, dmu_g, grads['rw_w0'], grads['rw_w1'], grads['rw_w2'], grads['rw_a0'], grads['rw_a1'], grads['rw_a2'],
     grads['rw_g1'], grads['rw_g2'], grads['rw_k_k'], grads['rw_k_a']) = lora_g
    dproj = {'r': (dr_p, dr_r), 'k': dkraw_l, 'v': (dv_p, dv_r)}
    dxs, dxps, dmu = [dx_post, dx_l], [dxp_l], {}
    for n in 'rkv':
        (dx_n, dxp_n), (dmu[n], grads['rw_w' + n]) = _stage_bwd("proj_bwd_" + n, _f_proj, (x, xp), proj_params[n],
                                                                 (dproj[n],), tbm, proxied=(1,))
        dxs.append(dx_n)
        dxps.append(dxp_n)
    grads['rw_mu'] = jnp.concatenate([dmu['r'], dmu_w, dmu['k'], dmu['v'], dmu_a, dmu_g], axis=0)
    grads['ln_g'] = jnp.concatenate(dln_g, axis=0)
    grads['ln_b'] = jnp.concatenate(dln_b, axis=0)
    grad_x = _grad_x(dxs, dxps)
    late = [n for n in BIG if n not in BIG_READY]
    late_sums = dict(zip(late, _chip_sums("b", [grads[n] for n in late], core)))
    return loss_blk, grad_x, grads, reduced, late_sums


def _position():
    return lax.axis_index("x"), lax.axis_index("y"), lax.axis_index("c")


def _other_chips(x, y):
    return [(1 - x, y), (x, 1 - y), (1 - x, 1 - y)]


def _chip_slice(ref, axis, q, size):
    idx = [slice(None)] * len(ref.shape)
    idx[axis] = pl.ds(pl.multiple_of(q * size, size), size)
    return ref.at[tuple(idx)]


_HBM = pl.BlockSpec(memory_space=pltpu.HBM)


def _gather_small_phases(src, dst, axes, sems):
    n = len(src)
    send_sems, recv_sems, own_sems = sems
    x, y, c = _position()
    chips = _other_chips(x, y)
    sizes = [src[a].shape[axes[a]] for a in range(n)]

    def copy(a, k, q):
        return pltpu.make_async_remote_copy(
            src_ref=src[a], dst_ref=_chip_slice(dst[a], axes[a], q, sizes[a]), send_sem=send_sems.at[a, k],
            recv_sem=recv_sems.at[a, k], device_id=(*chips[k], c), device_id_type=MESH)

    def own(a):
        return pltpu.make_async_copy(src[a], _chip_slice(dst[a], axes[a], 2 * x + y, sizes[a]), own_sems.at[a])

    def start():
        for a in range(n):
            own(a).start()
            for k in range(3):
                copy(a, k, 2 * x + y).start()

    def finish():
        for a in range(n):
            for k, (cx, cy) in enumerate(chips):
                copy(a, k, 2 * cx + cy).wait_recv()
        for a in range(n):
            for k in range(3):
                copy(a, k, 2 * x + y).wait_send()
            own(a).wait()

    return start, finish


def _gather_early(big, small, axes):
    nb, ns = len(big), len(small)
    full_shapes = [tuple(s * 4 if i == ax else s for i, s in enumerate(a.shape)) for a, ax in zip(small, axes)]

    def body(*refs):
        src_b, src_s = refs[:nb], refs[nb:nb + ns]
        dst_b, dst_s = refs[nb + ns:2 * nb + ns], refs[2 * nb + ns:2 * (nb + ns)]
        sems = refs[2 * (nb + ns):]
        small_start, small_finish = _gather_small_phases(src_s, dst_s, axes, sems[5:])
        small_start()
        for phase in _gather_big_phases(src_b, dst_b, sems[:5]):
            phase()
        small_finish()

    outs = pl.pallas_call(
        body, name="gather_early", in_specs=[_HBM] * (nb + ns), out_specs=[_HBM] * (nb + ns),
        out_shape=[_sds((4,) + a.shape, a.dtype) for a in big] + [_sds(s, a.dtype) for s, a in zip(full_shapes, small)],
        scratch_shapes=_gather_big_sems(nb) + [pltpu.SemaphoreType.DMA((ns, 3)), pltpu.SemaphoreType.DMA((ns, 3)),
                                               pltpu.SemaphoreType.DMA((ns,))],
        compiler_params=_cparams(),
    )(*big, *small)
    return outs[:nb], outs[nb:]


def _scatter_pieces(fulls, axes, sums):
    n, nsum = len(fulls), len(sums)
    sizes = [a.shape[ax] // 4 for a, ax in zip(fulls, axes)]
    shard_shapes = [tuple(sz if i == ax else s for i, s in enumerate(a.shape)) for a, ax, sz in zip(fulls, axes, sizes)]

    def body(*refs):
        src, big_src = refs[:n], refs[n:n + nsum]
        land, big_land = refs[n + nsum:2 * n + nsum], refs[2 * n + nsum:2 * (n + nsum)]
        send_sems, recv_sems = refs[2 * (n + nsum):2 * (n + nsum) + 2]
        big_start, big_finish = _scatter_big_phases(big_src, big_land, refs[2 * (n + nsum) + 2:])
        big_start()
        x, y, c = _position()
        chips = _other_chips(x, y)

        def copy(a, k):
            cx, cy = chips[k]
            return pltpu.make_async_remote_copy(
                src_ref=_chip_slice(src[a], axes[a], 2 * cx + cy, sizes[a]), dst_ref=land[a].at[k],
                send_sem=send_sems.at[a, k], recv_sem=recv_sems.at[a, k], device_id=(cx, cy, c), device_id_type=MESH)

        for a in range(n):
            for k in range(3):
                copy(a, k).start()
        for a in range(n):
            for k in range(3):
                copy(a, k).wait_recv()
        for a in range(n):
            for k in range(3):
                copy(a, k).wait_send()
        big_finish()

    outs = pl.pallas_call(
        body, name="scatter_grads", in_specs=[_HBM] * (n + nsum), out_specs=[_HBM] * (n + nsum),
        out_shape=[_sds((3,) + s) for s in shard_shapes] + [_sds((3,) + s.shape[1:], s.dtype) for s in sums],
        scratch_shapes=[pltpu.SemaphoreType.DMA((n, 3)), pltpu.SemaphoreType.DMA((n, 3))] + _scatter_big_sems(nsum),
        compiler_params=_cparams(),
    )(*fulls, *sums)
    return outs[:n], outs[n:]


def _sibling_swap(name, arrs):
    n = len(arrs)

    def body(*refs):
        src, dst = refs[:n], refs[n:2 * n]
        send_sems, recv_sems = refs[2 * n:]
        x, y, c = _position()
        copies = [pltpu.make_async_remote_copy(src_ref=src[a], dst_ref=dst[a], send_sem=send_sems.at[a], recv_sem=recv_sems.at[a],
                                               device_id=(x, y, 1 - c), device_id_type=MESH) for a in range(n)]
        for cp in copies:
            cp.start()
        for cp in copies:
            cp.wait_recv()
        for cp in copies:
            cp.wait_send()

    return pl.pallas_call(
        body, name=name, in_specs=[_HBM] * n, out_specs=[_HBM] * n, out_shape=[_sds(a.shape) for a in arrs],
        scratch_shapes=[pltpu.SemaphoreType.DMA((n,)), pltpu.SemaphoreType.DMA((n,))],
        compiler_params=_cparams(),
    )(*arrs)


def _sum4(name, own, land):
    rows, cols = own.shape
    tb = _rows_tile(rows)

    def body(o_ref, l0, l1, l2, out_ref):
        out_ref[...] = ((o_ref[...] + l0[...]) + l1[...]) + l2[...]

    blk = pl.BlockSpec((tb, cols), lambda i: (i, 0))
    lands = [pl.BlockSpec((None, tb, cols), functools.partial(lambda k, i: (k, i, 0), k)) for k in range(3)]
    return pl.pallas_call(body, name=name, grid=(rows // tb,), in_specs=[blk] + lands, out_specs=blk,
                          out_shape=_sds((rows, cols)), compiler_params=_cparams(("parallel",)))(own, land, land, land)


def _allreduce_adamw_small(g, w, m, v):
    rows, lanes = g.shape

    def body(g_ref, w_ref, m_ref, v_ref, gs_ref, d_ref, mn_ref, vn_ref, land, send_sems, recv_sems):
        x, y, c = _position()
        me = 4 * x + 2 * y + c
        chips = _other_chips(x, y)
        sibling = (x, y, 1 - c)

        def slot_of(cx, cy, cc):
            return 4 * cx + 2 * cy + cc

        def copy(j, slot, to, src=None):
            return pltpu.make_async_remote_copy(src_ref=g_ref if src is None else land.at[src], dst_ref=land.at[slot],
                                                send_sem=send_sems.at[j], recv_sem=recv_sems.at[j], device_id=to, device_id_type=MESH)

        copy(0, me, sibling).start()
        for k, chip in enumerate(chips):
            copy(1 + k, me, (*chip, c)).start()
        land[me] = g_ref[...]
        for k, chip in enumerate(chips):
            theirs = slot_of(*chip, c)
            copy(1 + k, theirs, (*chip, c)).wait_recv()
            copy(4 + k, theirs, sibling, src=theirs).start()
        copy(0, slot_of(*sibling), sibling).wait_recv()
        for k, chip in enumerate(chips):
            copy(4 + k, slot_of(*chip, 1 - c), sibling).wait_recv()
        copy(0, me, sibling).wait_send()
        for k, chip in enumerate(chips):
            copy(1 + k, me, (*chip, c)).wait_send()
            copy(4 + k, slot_of(*chip, c), sibling, src=slot_of(*chip, c)).wait_send()
        total = land[0]
        for dev in range(1, 8):
            total = total + land[dev]
        delta, mn, vn = _adamw_math(w_ref[...], total, m_ref[...], v_ref[...])
        gs_ref[...] = total
        d_ref[...] = delta
        mn_ref[...] = mn
        vn_ref[...] = vn

    vmem = pl.BlockSpec(memory_space=pltpu.VMEM)
    return pl.pallas_call(
        body, name="allreduce_adamw_small", in_specs=[vmem] * 4, out_specs=[vmem] * 4, out_shape=[_sds((rows, lanes))] * 4,
        scratch_shapes=[pltpu.VMEM((8, rows, lanes), F32), pltpu.SemaphoreType.DMA((7,)), pltpu.SemaphoreType.DMA((7,))],
        compiler_params=_cparams(),
    )(g, w, m, v)


def _row_half(ref, c):
    r2 = ref.shape[-2] // 2
    lead = (slice(None),) * (len(ref.shape) - 2)
    return ref.at[(*lead, pl.ds(pl.multiple_of(c * r2, r2), r2), slice(None))]


def _gather_big_phases(src, dst, sems):
    n = len(src)
    ici_send, ici_recv, d2d_send, d2d_recv, own_sems = sems
    x, y, c = _position()
    me = 2 * x + y
    chips = _other_chips(x, y)
    ids = [2 * cx + cy for cx, cy in chips]

    def ici(a, k, q):
        return pltpu.make_async_remote_copy(
            src_ref=_row_half(src[a], c), dst_ref=_row_half(dst[a].at[q], c), send_sem=ici_send.at[a, k],
            recv_sem=ici_recv.at[a, k], device_id=(*chips[k], c), device_id_type=MESH)

    def d2d(a, k, half):
        where = _row_half(dst[a].at[ids[k]], half)
        return pltpu.make_async_remote_copy(src_ref=where, dst_ref=where, send_sem=d2d_send.at[a, k], recv_sem=d2d_recv.at[a, k],
                                            device_id=(x, y, 1 - c), device_id_type=MESH)

    def own(a):
        return pltpu.make_async_copy(src[a], dst[a].at[me], own_sems.at[a])

    def start():
        for a in range(n):
            own(a).start()
            for k in range(3):
                ici(a, k, me).start()

    def forward():
        for a in range(n):
            for k in range(3):
                ici(a, k, ids[k]).wait_recv()
                d2d(a, k, c).start()

    def finish():
        for a in range(n):
            for k in range(3):
                d2d(a, k, 1 - c).wait_recv()
        for a in range(n):
            for k in range(3):
                ici(a, k, me).wait_send()
                d2d(a, k, c).wait_send()
            own(a).wait()

    return start, forward, finish


def _gather_big_sems(n):
    return [pltpu.SemaphoreType.DMA((n, 3))] * 4 + [pltpu.SemaphoreType.DMA((n,))]


def _chip_sums(tag, grads, core):
    views = [g.reshape(4, -1, g.shape[-1]) for g in grads]
    others = _sibling_halves("sibling_halves_" + tag, views)
    return [_half_add(f"half_add_{tag}{i}", v, o, core) for i, (v, o) in enumerate(zip(views, others))]


def _sibling_halves_phases(src, dst, sems):
    n = len(src)
    send_sems, recv_sems = sems
    x, y, c = _position()

    def copy(a):
        return pltpu.make_async_remote_copy(src_ref=_row_half(src[a], 1 - c), dst_ref=dst[a], send_sem=send_sems.at[a],
                                            recv_sem=recv_sems.at[a], device_id=(x, y, 1 - c), device_id_type=MESH)

    def start():
        for a in range(n):
            copy(a).start()

    def finish():
        for a in range(n):
            copy(a).wait_recv()
        for a in range(n):
            copy(a).wait_send()

    return start, finish


def _sibling_halves_sems(n):
    return [pltpu.SemaphoreType.DMA((n,)), pltpu.SemaphoreType.DMA((n,))]


def _sibling_halves_shapes(views):
    return [_sds((4, v.shape[1] // 2, v.shape[2])) for v in views]


def _sibling_halves(name, views):
    n = len(views)

    def body(*refs):
        for phase in _sibling_halves_phases(refs[:n], refs[n:2 * n], refs[2 * n:]):
            phase()

    return pl.pallas_call(
        body, name=name, in_specs=[_HBM] * n, out_specs=[_HBM] * n, out_shape=_sibling_halves_shapes(views),
        scratch_shapes=_sibling_halves_sems(n), compiler_params=_cparams(),
    )(*views)


def _rows_tile_capped(rows, cap=256):
    return min(_rows_tile(rows), cap)


def _half_add(name, view, other, core):
    _, r, k = view.shape
    r2 = r // 2
    tr = _rows_tile_capped(r2, 512)
    per = r2 // tr

    def body(c_ref, v_ref, o_ref, out_ref):
        out_ref[...] = (v_ref[...] + o_ref[...]).astype(BF16)

    blk = pl.BlockSpec((None, tr, k), lambda q, i, c: (q, i, 0))
    return pl.pallas_call(
        body, name=name,
        grid_spec=pltpu.PrefetchScalarGridSpec(
            num_scalar_prefetch=1, grid=(4, per),
            in_specs=[pl.BlockSpec((None, tr, k), lambda q, i, c: (q, c[0] * per + i, 0)), blk], out_specs=blk),
        out_shape=_sds((4, r2, k), BF16), compiler_params=_cparams(("parallel", "parallel")),
    )(core, view, other)


def _scatter_big_phases(src, land, sems):
    n = len(src)
    send_sems, recv_sems = sems
    x, y, c = _position()
    chips = _other_chips(x, y)

    def copy(a, k):
        cx, cy = chips[k]
        return pltpu.make_async_remote_copy(src_ref=src[a].at[2 * cx + cy], dst_ref=land[a].at[k], send_sem=send_sems.at[a, k],
                                            recv_sem=recv_sems.at[a, k], device_id=(cx, cy, c), device_id_type=MESH)

    def start():
        for a in range(n):
            for k in range(3):
                copy(a, k).start()

    def finish():
        for a in range(n):
            for k in range(3):
                copy(a, k).wait_recv()
        for a in range(n):
            for k in range(3):
                copy(a, k).wait_send()

    return start, finish


def _scatter_big_sems(n):
    return [pltpu.SemaphoreType.DMA((n, 3)), pltpu.SemaphoreType.DMA((n, 3))]


def _sum4_big(name, sums, land, chip):
    _, r2, k = sums.shape
    tr = _rows_tile_capped(r2, 512)

    def body(q_ref, s_ref, l0, l1, l2, out_ref):
        out_ref[...] = ((s_ref[...].astype(F32) + l0[...].astype(F32)) + l1[...].astype(F32)) + l2[...].astype(F32)

    lands = [pl.BlockSpec((None, tr, k), functools.partial(lambda j, i, q: (j, i, 0), j)) for j in range(3)]
    return pl.pallas_call(
        body, name=name,
        grid_spec=pltpu.PrefetchScalarGridSpec(
            num_scalar_prefetch=1, grid=(r2 // tr,),
            in_specs=[pl.BlockSpec((None, tr, k), lambda i, q: (q[0], i, 0))] + lands,
            out_specs=pl.BlockSpec((tr, k), lambda i, q: (i, 0))),
        out_shape=_sds((r2, k)), compiler_params=_cparams(("parallel",)),
    )(chip, sums, land, land, land)


def _adamw_halves(name, mine, theirs, w, m, v, core):
    r, k = w.shape
    r2 = r // 2
    tr = _rows_tile_capped(r2, 512)
    per = r2 // tr

    def body(c_ref, mine_ref, theirs_ref, w_ref, m_ref, v_ref, g_out, d_out, m_out, v_out):
        g = jnp.where(pl.program_id(0) == c_ref[0], mine_ref[...], theirs_ref[...])
        delta, mn, vn = _adamw_math(w_ref[...], g, m_ref[...], v_ref[...])
        g_out[...] = g
        d_out[...] = delta
        m_out[...] = mn
        v_out[...] = vn

    half = pl.BlockSpec((tr, k), lambda h, i, c: (i, 0))
    full = pl.BlockSpec((tr, k), lambda h, i, c: (h * per + i, 0))
    return pl.pallas_call(
        body, name=name,
        grid_spec=pltpu.PrefetchScalarGridSpec(num_scalar_prefetch=1, grid=(2, per), in_specs=[half, half, full, full, full],
                                               out_specs=[full] * 4),
        out_shape=[_sds((r, k))] * 4, compiler_params=_cparams(("parallel", "parallel")),
    )(core, mine, theirs, w, m, v)


def _drops_layer_axis(name):
    return not (name.startswith('mlp') or name == 's5_d')


def _work(name, arr):
    return arr.reshape(arr.shape[1:]) if _drops_layer_axis(name) else arr


def _work_axis(name):
    return SHARD_AXIS[name] - (1 if _drops_layer_axis(name) else 0)


def _as2d(a):
    return a.reshape(-1, a.shape[-1])


def _replicated_2d(name, arr):
    if name in ('ln_g', 'ln_b'):
        return arr
    if name == 'rw_r_k':
        return arr.reshape(1, -1)
    if name == 's5_log_dt':
        return arr.reshape(-1, 1)
    if name.startswith('s5_'):
        return arr.reshape(arr.shape[1:])
    return arr


def _pack(arrs):
    flat = []
    for a in arrs:
        f = a.reshape(-1)
        flat.append(jnp.pad(f, (0, -f.shape[0] % 128)))
    f = jnp.concatenate(flat)
    f = jnp.pad(f, (0, -f.shape[0] % 1024))
    return f.reshape(-1, 128)


def _unpack(packed, shapes):
    flat = packed.reshape(-1)
    out, at = [], 0
    for s in shapes:
        size = math.prod(s)
        out.append(flat[at:at + size].reshape(s))
        at += size + (-size % 128)
    return out


def kernel(x, ln_g, ln_b, rw_mu, rw_w0, rw_w1, rw_w2, rw_a0, rw_a1, rw_a2, rw_g1, rw_g2, rw_k_k, rw_k_a, rw_r_k, rw_wr, rw_wk, rw_wv, rw_wo, rw_lnx_g, rw_lnx_b, s5_a_re, s5_a_im, s5_log_dt, s5_b_re, s5_b_im, s5_c_re, s5_c_im, s5_d, s5_w_glu, mlp_w1, mlp_w2, loss_target, m_ln_g, m_ln_b, m_rw_mu, m_rw_w0, m_rw_w1, m_rw_w2, m_rw_a0, m_rw_a1, m_rw_a2, m_rw_g1, m_rw_g2, m_rw_k_k, m_rw_k_a, m_rw_r_k, m_rw_wr, m_rw_wk, m_rw_wv, m_rw_wo, m_rw_lnx_g, m_rw_lnx_b, m_s5_a_re, m_s5_a_im, m_s5_log_dt, m_s5_b_re, m_s5_b_im, m_s5_c_re, m_s5_c_im, m_s5_d, m_s5_w_glu, m_mlp_w1, m_mlp_w2, v_ln_g, v_ln_b, v_rw_mu, v_rw_w0, v_rw_w1, v_rw_w2, v_rw_a0, v_rw_a1, v_rw_a2, v_rw_g1, v_rw_g2, v_rw_k_k, v_rw_k_a, v_rw_r_k, v_rw_wr, v_rw_wk, v_rw_wv, v_rw_wo, v_rw_lnx_g, v_rw_lnx_b, v_s5_a_re, v_s5_a_im, v_s5_log_dt, v_s5_b_re, v_s5_b_im, v_s5_c_re, v_s5_c_im, v_s5_d, v_s5_w_glu, v_mlp_w1, v_mlp_w2):
    d = dict(locals())
    x_pos, y_pos, c_pos = _position()
    chip = 2 * x_pos + y_pos
    chip_arr = jnp.reshape(chip, (1,)).astype(jnp.int32)
    core_arr = jnp.reshape(c_pos, (1,)).astype(jnp.int32)

    small = [n for n in SHARD_AXIS if n not in BIG]
    axes = [_work_axis(n) for n in small]
    big_views, small_fulls = _gather_early([_as2d(d[n]).astype(BF16) for n in BIG_EARLY], [_work(n, d[n]) for n in small], axes)
    views = dict(zip(BIG_EARLY, big_views))
    fw = dict(zip(small, small_fulls))
    c_model = d['x'].shape[-1]
    for n in BIG_EARLY:
        fw[n] = views[n].reshape(c_model, c_model)
    w1_layers, w2_layers = d['mlp_w1'].astype(BF16), d['mlp_w2'].astype(BF16)
    fw['late_a'] = [_as2d(d['rw_wo']).astype(BF16), w1_layers[0], w2_layers[0]]
    fw['late_b'] = [w1_layers[1]]
    fw['late_c'] = [_as2d(d['s5_w_glu']).astype(BF16), w2_layers[1]]
    for n in REPLICATED:
        fw[n] = _replicated_2d(n, d[n])

    loss_blk, grad_x, grads, reduced, late_sums = _local_step(d['x'][0], d['loss_target'][0], fw, core_arr)
    loss = lax.psum(loss_blk[0, 0], ('x', 'y', 'c'))
    out = {}

    pieces = [grads[n] for n in small]
    lands, late_lands = _scatter_pieces(pieces, axes, list(late_sums.values()))
    reduced.update(zip(late_sums, zip(late_sums.values(), late_lands)))
    mine = [_sum4_big("sum4_" + n, *reduced[n], chip_arr) for n in BIG]
    for n, g, ax, land in zip(small, pieces, axes, lands):
        size = g.shape[ax] // 4
        mine.append(_sum4("sum4_" + n, lax.dynamic_slice_in_dim(g, chip * size, size, ax), land))
    theirs = _sibling_swap("swap_sums", mine)
    for n, a, b in zip(BIG + small, mine, theirs):
        w2d, m2d, v2d = _as2d(d[n]), _as2d(d['m_' + n]), _as2d(d['v_' + n])
        res = (_adamw_halves("adamw_" + n, a, b, w2d, m2d, v2d, core_arr) if n in BIG
               else _adamw("adamw_" + n, (a, b), w2d, m2d, v2d))
        out[n] = [r.reshape(d[n].shape) for r in res]

    rep_shapes = [d[n].shape for n in REPLICATED]
    packs = [_pack([grads[n] for n in REPLICATED])] + [_pack([d[p + n] for n in REPLICATED]) for p in ('', 'm_', 'v_')]
    res = [_unpack(p, rep_shapes) for p in _allreduce_adamw_small(*packs)]
    for i, n in enumerate(REPLICATED):
        out[n] = [r[i] for r in res]

    grad_x = grad_x.reshape(d['x'].shape)
    return (loss, grad_x, *[out[n][0] for n in WEIGHTS], *[out[n][1] for n in WEIGHTS],
            *[out[n][2] for n in WEIGHTS], *[out[n][3] for n in WEIGHTS])
```

```python
import functools
import math

import jax
import jax.numpy as jnp
from jax import lax
from jax.experimental import pallas as pl
from jax.experimental.pallas import tpu as pltpu

F32 = jnp.float32
BF16 = jnp.bfloat16
MESH = pl.DeviceIdType.MESH

HEAD = 64
SSM_GROUP = 16
SSM_STATE = 64
GN_EPS = 64e-5
LN_EPS = 1e-5
DEPTH = 2
DN_ALPHA = (2.0 * DEPTH) ** 0.25
ADAM_LR, ADAM_B1, ADAM_B2, ADAM_EPS, ADAM_WD, ADAM_STEP = 0.001, 0.9, 0.999, 1e-08, 0.01, 10
REC_CHUNK = 64
V7X_VMEM_BYTES = 64 * 2 ** 20
VMEM_LIMIT = V7X_VMEM_BYTES - 8 * 2 ** 20

WEIGHTS = ['ln_g', 'ln_b', 'rw_mu', 'rw_w0', 'rw_w1', 'rw_w2', 'rw_a0', 'rw_a1', 'rw_a2', 'rw_g1', 'rw_g2',
           'rw_k_k', 'rw_k_a', 'rw_r_k', 'rw_wr', 'rw_wk', 'rw_wv', 'rw_wo', 'rw_lnx_g', 'rw_lnx_b',
           's5_a_re', 's5_a_im', 's5_log_dt', 's5_b_re', 's5_b_im', 's5_c_re', 's5_c_im', 's5_d', 's5_w_glu',
           'mlp_w1', 'mlp_w2']
SHARD_AXIS = {'rw_mu': 2, 'rw_w1': 1, 'rw_w2': 2, 'rw_a1': 1, 'rw_a2': 2, 'rw_g1': 1, 'rw_g2': 2,
              'rw_wr': 1, 'rw_wk': 1, 'rw_wv': 1, 'rw_wo': 1, 's5_d': 1, 's5_w_glu': 2, 'mlp_w1': 2, 'mlp_w2': 1}
REPLICATED = [n for n in WEIGHTS if n not in SHARD_AXIS]
BIG_EARLY = ['rw_wr', 'rw_wk', 'rw_wv']
BIG_LATE = ['rw_wo', 's5_w_glu', 'mlp_w1', 'mlp_w2']
BIG = BIG_EARLY + BIG_LATE
BIG_READY = ['s5_w_glu', 'mlp_w1', 'mlp_w2']


def _sds(shape, dtype=F32):
    return jax.ShapeDtypeStruct(tuple(shape), dtype)


def _cparams(sem=None, **kw):
    if sem is not None:
        kw["dimension_semantics"] = sem
    return pltpu.CompilerParams(vmem_limit_bytes=VMEM_LIMIT, **kw)


def _mm_products(a, b, g):
    gb = g.astype(BF16)
    da = lax.dot_general(gb, b.astype(BF16), (((1,), (1,)), ((), ())), preferred_element_type=F32)
    db = lax.dot_general(a.astype(BF16), gb, (((0,), (0,)), ((), ())), preferred_element_type=F32)
    return da, db


@jax.custom_vjp
def _mm_plain(a, b):
    return jnp.dot(a.astype(BF16), b.astype(BF16), preferred_element_type=F32)


def _mm_plain_bwd(res, g):
    da, db = _mm_products(*res, g)
    return da.astype(res[0].dtype), db.astype(res[1].dtype)


_mm_plain.defvjp(lambda a, b: (_mm_plain(a, b), (a, b)), _mm_plain_bwd)


@jax.custom_vjp
def _mm_proxy(a, b, z):
    return jnp.dot(a.astype(BF16), b.astype(BF16), preferred_element_type=F32)


def _mm_proxy_bwd(res, g):
    da, db = _mm_products(*res, g)
    return da.astype(res[0].dtype), jnp.zeros_like(res[1]), db


_mm_proxy.defvjp(lambda a, b, z: (_mm_proxy(a, b, z), (a, b)), _mm_proxy_bwd)


def mm(a, b, z=None):
    return _mm_plain(a, b) if z is None else _mm_proxy(a, b, z)


def _split3(x):
    hi = x.astype(BF16)
    r1 = x - hi.astype(F32)
    mid = r1.astype(BF16)
    lo = (r1 - mid.astype(F32)).astype(BF16)
    return hi, mid, lo


def _head_sum_impl(x):
    c = x.shape[1]
    lanes = 128
    sel = (lax.broadcasted_iota(jnp.int32, (c, lanes), 0) // HEAD
           == lax.broadcasted_iota(jnp.int32, (c, lanes), 1)).astype(BF16)
    s = sum(jnp.dot(p, sel, preferred_element_type=F32) for p in _split3(x))
    return sum(lax.dot_general(p, sel, (((1,), (1,)), ((), ())), preferred_element_type=F32) for p in _split3(s))


@jax.custom_vjp
def head_sum(x):
    return _head_sum_impl(x)


head_sum.defvjp(lambda x: (_head_sum_impl(x), None), lambda _, g: (_head_sum_impl(g),))


def _ln(x, g, b):
    mu = jnp.mean(x, axis=-1, keepdims=True)
    xc = x - mu
    var = jnp.mean(xc * xc, axis=-1, keepdims=True)
    return xc * lax.rsqrt(var + LN_EPS) * g + b


def _f_proj(acts, params, proxies):
    x, xp = acts
    mu, w = params
    return (mm(x + (xp - x) * mu, w, proxies[1]),)


def _f_lora(acts, params, proxies):
    x, xp, kraw = acts
    mu_w, mu_a, mu_g, w0, w1, w2, a0, a1, a2, g1, g2, k_k, k_a = params
    xx = xp - x
    w_pre = w0 + mm(jnp.tanh(mm(x + xx * mu_w, w1)), w2)
    z = -w_pre
    softplus = jnp.maximum(z, 0.0) + jnp.log(1.0 + jnp.exp(-jnp.abs(z)))
    log_decay = -jnp.exp(-softplus - 0.5)
    a = jax.nn.sigmoid(a0 + mm(mm(x + xx * mu_a, a1), a2))
    g = mm(jax.nn.sigmoid(mm(x + xx * mu_g, g1)), g2)
    kk = kraw * k_k
    kkn = kk / jnp.maximum(jnp.sqrt(head_sum(kk * kk)), 1e-12)
    k2 = kraw * (1.0 + (a - 1.0) * k_a)
    return log_decay, k2, -kkn, kkn * a, g


def _f_post(acts, params, proxies):
    o, r, k2, v, g, x = acts
    lnx_g, lnx_b, r_k, wo, ln_g, ln_b = params
    om = head_sum(o) * (1.0 / HEAD)
    oc = o - om
    ov = head_sum(oc * oc) * (1.0 / HEAD)
    on = oc * lax.rsqrt(ov + GN_EPS) * lnx_g + lnx_b
    bonus = head_sum(r * k2 * r_k) * v
    y = mm((on + bonus) * g, wo, proxies[3])
    return (_ln(DN_ALPHA * x + y, ln_g, ln_b),)


def _f_glu(acts, params, proxies):
    ys, h = acts
    d, wv0, wv1, wg0, wg1, ln_g, ln_b = params
    y = jax.nn.gelu(ys + h * d)
    mix = jnp.concatenate([mm(y, wv0, proxies[1]) * jax.nn.sigmoid(mm(y, wg0, proxies[3])),
                           mm(y, wv1, proxies[2]) * jax.nn.sigmoid(mm(y, wg1, proxies[4]))], axis=1)
    return (_ln(DN_ALPHA * h + mix, ln_g, ln_b),)


def _f_zoh(a_re, a_im, log_dt, b_re_t, b_im_t):
    dt = jnp.exp(log_dt)
    lam_re = jnp.minimum(a_re, -1e-4)
    lam_im = a_im
    mag = jnp.exp(dt * lam_re)
    abar_re = mag * jnp.cos(dt * lam_im)
    abar_im = mag * jnp.sin(dt * lam_im)
    den = lam_re * lam_re + lam_im * lam_im
    nr, ni = abar_re - 1.0, abar_im
    coef_re = ((nr * lam_re + ni * lam_im) / den)[:, None, :]
    coef_im = ((ni * lam_re - nr * lam_im) / den)[:, None, :]
    return (abar_re, abar_im, coef_re * b_re_t - coef_im * b_im_t, coef_re * b_im_t + coef_im * b_re_t)


def _bdot16_raw(a, b, ca, cb):
    return lax.dot_general(a.astype(BF16), b.astype(BF16), (((ca,), (cb,)), ((0,), (0,))), preferred_element_type=F32)


@functools.partial(jax.custom_vjp, nondiff_argnums=(2, 3))
def _bdot16(a, b, ca, cb):
    return _bdot16_raw(a, b, ca, cb)


def _bdot16_bwd(ca, cb, res, g):
    a, b = res
    if (ca, cb) == (2, 1):
        return _bdot16_raw(g, b, 2, 2), _bdot16_raw(a, g, 1, 1)
    if (ca, cb) == (2, 2):
        return _bdot16_raw(g, b, 2, 1), _bdot16_raw(g, a, 1, 1)
    assert (ca, cb) == (1, 1)
    return _bdot16_raw(b, g, 2, 2), _bdot16_raw(a, g, 2, 1)


_bdot16.defvjp(lambda a, b, ca, cb: (_bdot16_raw(a, b, ca, cb), (a, b)), _bdot16_bwd)

def _time_sums(x, suffix):
    hg, ln, _ = x.shape
    row = lax.broadcasted_iota(jnp.int32, (hg, ln, ln), 1)
    col = lax.broadcasted_iota(jnp.int32, (hg, ln, ln), 2)
    tri = ((row <= col) if suffix else (row >= col)).astype(BF16)
    return sum(lax.dot_general(tri, p, (((2,), (1,)), ((0,), (0,))), preferred_element_type=F32) for p in _split3(x))


@jax.custom_vjp
def _time_cumsum(x):
    return _time_sums(x, False)


_time_cumsum.defvjp(lambda x: (_time_sums(x, False), None), lambda _, g: (_time_sums(g, True),))

_dot_score = _bdot16
_dot_inverse = _bdot16
_dot_value = _bdot16


def _rec_chunk(s0, r, lw, k, v, a, b):
    hg, ln, _ = r.shape
    row = lax.broadcasted_iota(jnp.int32, (hg, ln, ln), 1)
    col = lax.broadcasted_iota(jnp.int32, (hg, ln, ln), 2)
    incl, strict = row >= col, row > col
    cum = _time_cumsum(lw)
    total = jnp.sum(lw, axis=1, keepdims=True)
    e_cum, e_inv, e_prev, e_tail = jnp.exp(cum), jnp.exp(-cum), jnp.exp(cum - lw), jnp.exp(total - cum)
    rt, at, bt, kt = r * e_cum, a * e_prev, b * e_inv, k * e_inv
    ar = jnp.concatenate([at, rt], axis=1)
    on_b, on_k = _dot_score(ar, bt, 2, 2), _dot_score(ar, kt, 2, 2)
    aab, arb = jnp.where(strict, on_b[:, :ln], 0.0), jnp.where(incl, on_b[:, ln:], 0.0)
    aak, ark = jnp.where(strict, on_k[:, :ln], 0.0), jnp.where(incl, on_k[:, ln:], 0.0)
    p = (row == col).astype(F32) + aab
    m = aab
    for _ in range(int(math.log2(ln)) - 1):
        m = _dot_inverse(m, m, 2, 1)
        p = p + _dot_inverse(p, m, 2, 1)
    from_state = _dot_value(ar, s0, 2, 2)
    from_v = _dot_value(jnp.concatenate([aak, ark], axis=1), v, 2, 1)
    u = _dot_inverse(p, from_state[:, :ln] + from_v[:, :ln], 2, 1)
    o = from_state[:, ln:] + from_v[:, ln:] + _dot_value(arb, u, 2, 1)
    s1 = s0 * jnp.exp(total) + _dot_value(jnp.concatenate([u, v], axis=1),
                                          jnp.concatenate([b * e_tail, k * e_tail], axis=1), 1, 1)
    return o, s1


def _full_spec(shape):
    nd = len(shape)
    return pl.BlockSpec(tuple(shape), lambda *_: (0,) * nd)


def _stage_fwd(name, f, acts, params, out_dims, tb):
    t = acts[0].shape[0]
    na, npar = len(acts), len(params)

    def body(*refs):
        outs = f(tuple(r[...] for r in refs[:na]), tuple(r[...] for r in refs[na:na + npar]), (None,) * npar)
        for r, val in zip(refs[na + npar:], outs):
            r[...] = val

    return pl.pallas_call(
        body, name=name, grid=(t // tb,),
        in_specs=[pl.BlockSpec((tb, a.shape[1]), lambda i: (i, 0)) for a in acts] + [_full_spec(p.shape) for p in params],
        out_specs=[pl.BlockSpec((tb, d), lambda i: (i, 0)) for d in out_dims],
        out_shape=[_sds((t, d)) for d in out_dims],
        compiler_params=_cparams(("arbitrary",)),
    )(*acts, *params)


def _stage_bwd(name, f, acts, params, couts, tb, proxied=(), halves_of=()):
    nh = len(halves_of)
    t = acts[0].shape[0]
    groups = [c if isinstance(c, tuple) else (c,) for c in couts]
    couts = [term for grp in groups for term in grp]
    na, npar, nc = len(acts), len(params), len(couts)
    steps = t // tb

    def f_diff(act_vals, diff_vals, param_vals):
        real = tuple(param_vals[i] if i in proxied else diff_vals[i] for i in range(npar))
        proxies = tuple(diff_vals[i] if i in proxied else None for i in range(npar))
        return f(act_vals, real, proxies)

    def body(*refs):
        a_refs, p_hbm, c_refs = refs[:na], refs[na:na + npar], refs[na + npar:na + npar + nc]
        o = na + npar + nc
        half_src, o = refs[o:o + nh], o + nh
        da_refs, dp_hbm, half_dst = refs[o:o + na], refs[o + na:o + na + npar], refs[o + na + npar:o + na + npar + nh]
        o = o + na + npar + nh
        p_buf, acc, half_sems = refs[o:o + npar], refs[o + npar:o + 2 * npar], refs[o + 2 * npar:]
        i = pl.program_id(0)
        if nh:
            half_start, half_finish = _sibling_halves_phases(half_src, half_dst, half_sems)
            pl.when(i == 0)(half_start)

        @pl.when(i == 0)
        def _():
            for src, dst in zip(p_hbm, p_buf):
                pltpu.sync_copy(src, dst)
            for r in acc:
                r[...] = jnp.zeros_like(r)

        param_vals = tuple(r[...] for r in p_buf)
        diff_vals = tuple(jnp.zeros(v.shape, F32) if i in proxied else v for i, v in enumerate(param_vals))
        _, vjp = jax.vjp(functools.partial(f_diff, param_vals=param_vals), tuple(r[...] for r in a_refs), diff_vals)
        terms = iter(c_refs)
        d_acts, d_params = vjp(tuple(functools.reduce(jnp.add, [next(terms)[...] for _ in grp]) for grp in groups))
        for r, val in zip(da_refs, d_acts):
            r[...] = val
        for r, val in zip(acc, d_params):
            r[...] += val

        @pl.when(i == steps - 1)
        def _():
            for src, dst in zip(acc, dp_hbm):
                pltpu.sync_copy(src, dst)

        if nh:
            pl.when(i == steps - 1)(half_finish)

    hbm = pl.BlockSpec(memory_space=pltpu.HBM)
    outs = pl.pallas_call(
        body, name=name, grid=(steps,),
        in_specs=[pl.BlockSpec((tb, a.shape[1]), lambda i: (i, 0)) for a in acts] + [hbm] * npar
        + [pl.BlockSpec((tb, c.shape[1]), lambda i: (i, 0)) for c in couts] + [hbm] * nh,
        out_specs=[pl.BlockSpec((tb, a.shape[1]), lambda i: (i, 0)) for a in acts] + [hbm] * (npar + nh),
        out_shape=[_sds(a.shape) for a in acts] + [_sds(p.shape) for p in params] + _sibling_halves_shapes(halves_of),
        scratch_shapes=[pltpu.VMEM(p.shape, p.dtype) for p in params] + [pltpu.VMEM(p.shape, F32) for p in params]
        + (_sibling_halves_sems(nh) if nh else []),
        compiler_params=_cparams(("arbitrary",)),
    )(*acts, *params, *couts, *halves_of)
    if nh:
        return outs[:na], outs[na:na + npar], outs[na + npar:]
    return outs[:na], outs[na:]


def _tiled_matmul(name, a, b, mode, grid, a_spec, b_spec, o_spec, out_shape):
    nk = grid[2]
    dims = {"nn": ((1,), (0,)), "nt": ((1,), (1,)), "tn": ((0,), (0,))}[mode]

    def body(a_ref, b_ref, o_ref, acc):
        kk = pl.program_id(2)

        @pl.when(kk == 0)
        def _():
            acc[...] = jnp.zeros_like(acc)

        acc[...] += lax.dot_general(a_ref[...].astype(BF16), b_ref[...].astype(BF16), (dims, ((), ())),
                                    preferred_element_type=F32)

        @pl.when(kk == nk - 1)
        def _():
            o_ref[...] = acc[...]

    return pl.pallas_call(
        body, name=name, grid=grid, in_specs=[a_spec, b_spec], out_specs=o_spec, out_shape=_sds(out_shape),
        scratch_shapes=[pltpu.VMEM(o_spec.block_shape, F32)],
        compiler_params=_cparams(("parallel", "parallel", "arbitrary")),
    )(a, b)


def _mlp_weight_grad(name, a, b, layer, layers, split, into=None, tile=512):
    t, m = a.shape
    n = b.shape[1]
    tile = 2 * tile
    tk = min(2 * tile, t)
    if split == "n":
        tm, tn = min(tile, m), min(tile, n // 4)
        per = n // 4 // tn
        shape = (4, layers, m, n // 4)
        o_idx = lambda i, j, k: (j // per, layer, i, j % per)
    else:
        tm, tn = min(tile, m // 4), min(tile, n)
        per = m // 4 // tm
        shape = (4, layers, m // 4, n)
        o_idx = lambda i, j, k: (i // per, layer, i % per, j)
    nk = t // tk

    def body(a_ref, b_ref, *rest):
        o_ref, acc = rest[-2:]
        kk = pl.program_id(2)

        @pl.when(kk == 0)
        def _():
            acc[...] = jnp.zeros_like(acc)

        acc[...] += lax.dot_general(a_ref[...].astype(BF16), b_ref[...].astype(BF16), (((0,), (0,)), ((), ())),
                                    preferred_element_type=F32)

        @pl.when(kk == nk - 1)
        def _():
            o_ref[...] = acc[...]

    in_specs = [pl.BlockSpec((tk, tm), lambda i, j, k: (k, i)), pl.BlockSpec((tk, tn), lambda i, j, k: (k, j))]
    operands = [a, b]
    aliases = {}
    if into is not None:
        in_specs.append(pl.BlockSpec(memory_space=pl.ANY))
        operands.append(into)
        aliases = {2: 0}
    return pl.pallas_call(
        body, name=name, grid=(m // tm, n // tn, nk), in_specs=in_specs,
        out_specs=pl.BlockSpec((None, None, tm, tn), o_idx), out_shape=_sds(shape), input_output_aliases=aliases,
        scratch_shapes=[pltpu.VMEM((tm, tn), F32)],
        compiler_params=_cparams(("parallel", "parallel", "arbitrary")),
    )(*operands)


S5_PACK = 8


def _s5_weight_grad(name, x, s, wide_rows, tk):
    t, c = x.shape
    wide = s.shape[1]
    kb, nb = S5_PACK * SSM_GROUP, S5_PACK * SSM_STATE
    nsb = c // kb
    x_spec = pl.BlockSpec((tk, kb), lambda i, j, k: (k, j % nsb))
    s_spec = pl.BlockSpec((tk, nb), lambda i, j, k: (k, j))
    if wide_rows:
        return _tiled_matmul(name, s, x, "tn", (1, wide // nb, t // tk), s_spec, x_spec,
                             pl.BlockSpec((nb, kb), lambda i, j, k: (j, 0)), (wide, kb))
    return _tiled_matmul(name, x, s, "tn", (1, wide // nb, t // tk), x_spec, s_spec,
                         pl.BlockSpec((kb, nb), lambda i, j, k: (0, j)), (kb, wide))


def _mlp_fwd(name, h, w1, w2, layer, ln_g, ln_b, tb, shards=()):
    t, c = h.shape
    nj, fc = w1.shape[0], w1.shape[3]
    nsh = len(shards)
    steps = (t // tb) * nj

    def body(h_ref, w1_ref, w2_ref, g_ref, b_ref, *rest):
        src, (out_ref, s_ref), dst = rest[:nsh], rest[nsh:nsh + 2], rest[nsh + 2:2 * nsh + 2]
        acc, sems = rest[2 * nsh + 2], rest[2 * nsh + 3:]
        j = pl.program_id(1)
        step = pl.program_id(0) * nj + j
        if nsh:
            start, forward, finish = _gather_big_phases(src, dst, sems)
            pl.when(step == 0)(start)

        @pl.when(j == 0)
        def _():
            acc[...] = jnp.zeros_like(acc)

        hid = jnp.dot(h_ref[...].astype(BF16), w1_ref[...].astype(BF16), preferred_element_type=F32)
        act = jnp.square(jnp.maximum(hid, 0.0))
        acc[...] += jnp.dot(act.astype(BF16), w2_ref[...].astype(BF16), preferred_element_type=F32)

        @pl.when(j == nj - 1)
        def _():
            s = DN_ALPHA * h_ref[...] + acc[...]
            s_ref[...] = s
            out_ref[...] = _ln(s, g_ref[...], b_ref[...])

        if nsh:
            pl.when(step == steps // 2)(forward)
            pl.when(step == steps - 1)(finish)

    row = pl.BlockSpec((tb, c), lambda i, j: (i, 0))
    vec = pl.BlockSpec((1, c), lambda i, j: (0, 0))
    outs = pl.pallas_call(
        body, name=name, grid=(t // tb, nj),
        in_specs=[row, pl.BlockSpec((None, None, c, fc), lambda i, j: (j, layer, 0, 0)),
                  pl.BlockSpec((None, None, fc, c), lambda i, j: (j, layer, 0, 0)), vec, vec] + [_HBM] * nsh,
        out_specs=[row, row] + [_HBM] * nsh,
        out_shape=[_sds((t, c)), _sds((t, c))] + [_sds((4,) + a.shape, a.dtype) for a in shards],
        scratch_shapes=[pltpu.VMEM((tb, c), F32)] + (_gather_big_sems(nsh) if nsh else []),
        compiler_params=_cparams(("arbitrary", "arbitrary")),
    )(h, w1, w2, ln_g, ln_b, *shards)
    return outs[0], outs[1], outs[2:]


def _mlp_bwd(name, h, s, dout, w1, w2, layer, ln_g, ln_b, tb):
    t, c = h.shape
    nj, fc = w1.shape[0], w1.shape[3]
    ff = nj * fc
    ni = t // tb
    nt = (((1,), (1,)), ((), ()))
    douts = dout if isinstance(dout, tuple) else (dout,)
    nd = len(douts)

    def body(h_ref, s_ref, *rest):
        dout_refs = rest[:nd]
        (w1_ref, w2_ref, g_ref, b_ref, dh_ref, ds_ref, dhid_ref, act_ref, dg_ref, db_ref,
         ds_scr, dh_acc, dg_acc, db_acc) = rest[nd:]
        i, j = pl.program_id(0), pl.program_id(1)

        @pl.when((i == 0) & (j == 0))
        def _():
            dg_acc[...] = jnp.zeros_like(dg_acc)
            db_acc[...] = jnp.zeros_like(db_acc)

        @pl.when(j == 0)
        def _():
            _, vjp = jax.vjp(_ln, s_ref[...], g_ref[...], b_ref[...])
            ds, dg, db = vjp(functools.reduce(jnp.add, [r[...] for r in dout_refs]))
            ds_scr[...] = ds
            ds_ref[...] = ds.astype(BF16)
            dh_acc[...] = DN_ALPHA * ds
            dg_acc[...] += dg
            db_acc[...] += db

        w1b, w2b = w1_ref[...].astype(BF16), w2_ref[...].astype(BF16)
        hid = jnp.dot(h_ref[...].astype(BF16), w1b, preferred_element_type=F32)
        rl = jnp.maximum(hid, 0.0)
        dact = lax.dot_general(ds_scr[...].astype(BF16), w2b, nt, preferred_element_type=F32)
        dhid = (dact * 2.0 * rl).astype(BF16)
        dh_acc[...] += lax.dot_general(dhid, w1b, nt, preferred_element_type=F32)
        dhid_ref[...] = dhid
        act_ref[...] = (rl * rl).astype(BF16)

        @pl.when(j == nj - 1)
        def _():
            dh_ref[...] = dh_acc[...]

        @pl.when((i == ni - 1) & (j == nj - 1))
        def _():
            dg_ref[...] = dg_acc[...]
            db_ref[...] = db_acc[...]

    row = pl.BlockSpec((tb, c), lambda i, j: (i, 0))
    vec = pl.BlockSpec((1, c), lambda i, j: (0, 0))
    wide = pl.BlockSpec((tb, fc), lambda i, j: (i, j))
    return pl.pallas_call(
        body, name=name, grid=(ni, nj),
        in_specs=[row, row] + [row] * nd + [pl.BlockSpec((None, None, c, fc), lambda i, j: (j, layer, 0, 0)),
                                            pl.BlockSpec((None, None, fc, c), lambda i, j: (j, layer, 0, 0)), vec, vec],
        out_specs=[row, row, wide, wide, vec, vec],
        out_shape=[_sds((t, c)), _sds((t, c), BF16), _sds((t, ff), BF16), _sds((t, ff), BF16), _sds((1, c)), _sds((1, c))],
        scratch_shapes=[pltpu.VMEM((tb, c), F32), pltpu.VMEM((tb, c), F32), pltpu.VMEM((1, c), F32), pltpu.VMEM((1, c), F32)],
        compiler_params=_cparams(("arbitrary", "arbitrary")),
    )(h, s, *douts, w1, w2, ln_g, ln_b)


def _load_heads(ref, hg):
    return jnp.stack([ref[:, h * HEAD:(h + 1) * HEAD] for h in range(hg)])


def _store_heads(ref, val):
    for h in range(val.shape[0]):
        ref[:, h * HEAD:(h + 1) * HEAD] = val[h]


def _rec_fwd(r, lw, k, v, a, b, hg, shards):
    t, c = r.shape
    n = HEAD
    nh = c // n
    ln = REC_CHUNK
    nck = t // ln
    ngrp = nh // hg
    nsh = len(shards)
    steps = ngrp * nck

    def body(r_ref, lw_ref, k_ref, v_ref, a_ref, b_ref, *rest):
        src, (o_ref, s0_ref), dst = rest[:nsh], rest[nsh:nsh + 2], rest[nsh + 2:2 * nsh + 2]
        state, sems = rest[2 * nsh + 2], rest[2 * nsh + 3:]
        step = pl.program_id(0) * nck + pl.program_id(1)
        start, forward, finish = _gather_big_phases(src, dst, sems)
        pl.when(step == 0)(start)

        @pl.when(pl.program_id(1) == 0)
        def _():
            state[...] = jnp.zeros_like(state)

        s0 = state[...]
        s0_ref[...] = s0
        o, s1 = _rec_chunk(s0, *(_load_heads(x, hg) for x in (r_ref, lw_ref, k_ref, v_ref, a_ref, b_ref)))
        _store_heads(o_ref, o)
        state[...] = s1
        pl.when(step == steps // 2)(forward)
        pl.when(step == steps - 1)(finish)

    seq = pl.BlockSpec((ln, hg * n), lambda g, i: (i, g))
    outs = pl.pallas_call(
        body, name="rec_fwd", grid=(ngrp, nck), in_specs=[seq] * 6 + [_HBM] * nsh,
        out_specs=[seq, pl.BlockSpec((None, hg, n, n), lambda g, i: (i, g, 0, 0))] + [_HBM] * nsh,
        out_shape=[_sds((t, c)), _sds((nck, nh, n, n))] + [_sds((4,) + s.shape, s.dtype) for s in shards],
        scratch_shapes=[pltpu.VMEM((hg, n, n), F32)] + _gather_big_sems(nsh),
        compiler_params=_cparams(("arbitrary", "arbitrary")),
    )(r, lw, k, v, a, b, *shards)
    return outs[0], outs[1], outs[2:]


def _rec_bwd(r, lw, k, v, a, b, s0s, do, hg, chip_sums):
    t, c = r.shape
    n = HEAD
    nh = c // n
    ln = REC_CHUNK
    nck = t // ln
    ngrp = nh // hg
    nsum = len(chip_sums)
    steps = ngrp * nck

    def body(r_ref, lw_ref, k_ref, v_ref, a_ref, b_ref, s0_ref, do_ref, *rest):
        src, grad_refs, land = rest[:nsum], rest[nsum:nsum + 6], rest[nsum + 6:2 * nsum + 6]
        dstate, sems = rest[2 * nsum + 6], rest[2 * nsum + 7:]
        step = pl.program_id(0) * nck + pl.program_id(1)
        start, finish = _scatter_big_phases(src, land, sems)
        pl.when(step == 0)(start)

        @pl.when(pl.program_id(1) == 0)
        def _():
            dstate[...] = jnp.zeros_like(dstate)

        _, vjp = jax.vjp(_rec_chunk, s0_ref[...], *(_load_heads(x, hg) for x in (r_ref, lw_ref, k_ref, v_ref, a_ref, b_ref)))
        ds0, *grads = vjp((_load_heads(do_ref, hg), dstate[...]))
        dstate[...] = ds0
        for ref, val in zip(grad_refs, grads):
            _store_heads(ref, val)
        pl.when(step == steps - 1)(finish)

    seq = pl.BlockSpec((ln, hg * n), lambda g, i: (nck - 1 - i, g))
    outs = pl.pallas_call(
        body, name="rec_bwd", grid=(ngrp, nck),
        in_specs=[seq] * 6 + [pl.BlockSpec((None, hg, n, n), lambda g, i: (nck - 1 - i, g, 0, 0)), seq] + [_HBM] * nsum,
        out_specs=[seq] * 6 + [_HBM] * nsum,
        out_shape=[_sds((t, c))] * 6 + [_sds((3,) + s.shape[1:], s.dtype) for s in chip_sums],
        scratch_shapes=[pltpu.VMEM((hg, n, n), F32)] + _scatter_big_sems(nsum),
        compiler_params=_cparams(("arbitrary", "arbitrary")),
    )(r, lw, k, v, a, b, s0s, do, *chip_sums)
    return outs[:6], outs[6:]


def _s5_blocks(c):
    kb, nb = S5_PACK * SSM_GROUP, S5_PACK * SSM_STATE
    return kb, nb, c // kb


def _s5_fwd(h, bc, abar, cc, tb, shards=()):
    t, c = h.shape
    w2 = bc.shape[1]
    w = w2 // 2
    kb, nb, nsb = _s5_blocks(c)

    nsh = len(shards)
    steps = t // tb

    def body(h_ref, bc_ref, a_ref, cc_ref, *rest):
        src, (s_ref, y_ref), dst = rest[:nsh], rest[nsh:nsh + 2], rest[nsh + 2:2 * nsh + 2]
        carry, rows, sems = rest[2 * nsh + 2], rest[2 * nsh + 3], rest[2 * nsh + 4:]
        if nsh:
            start, forward, finish = _gather_big_phases(src, dst, sems)
            pl.when(pl.program_id(0) == 0)(start)

        @pl.when(pl.program_id(0) == 0)
        def _():
            carry[...] = jnp.zeros_like(carry)

        for j in range(w2 // nb):
            ch = (j % nsb) * kb
            rows[:, j * nb:(j + 1) * nb] = jnp.dot(h_ref[:, ch:ch + kb].astype(BF16), bc_ref[:, j * nb:(j + 1) * nb],
                                                   preferred_element_type=F32)
        ar, ai = a_ref[:, :w], a_ref[:, w:]

        def step(i, state):
            hr, hi = state
            nr = ar * hr - ai * hi + rows[pl.ds(i, 1), :w]
            ni = ar * hi + ai * hr + rows[pl.ds(i, 1), w:]
            rows[pl.ds(i, 1), :w] = nr
            rows[pl.ds(i, 1), w:] = ni
            return nr, ni

        hr, hi = lax.fori_loop(0, tb, step, (carry[:, :w], carry[:, w:]))
        carry[:, :w] = hr
        carry[:, w:] = hi
        s_ref[...] = rows[...].astype(BF16)
        for j in range(nsb):
            re, im = j * nb, w + j * nb
            y_ref[:, j * kb:(j + 1) * kb] = (
                jnp.dot(s_ref[:, re:re + nb], cc_ref[re:re + nb, :], preferred_element_type=F32)
                + jnp.dot(s_ref[:, im:im + nb], cc_ref[im:im + nb, :], preferred_element_type=F32))
        if nsh:
            pl.when(pl.program_id(0) == steps // 2)(forward)
            pl.when(pl.program_id(0) == steps - 1)(finish)

    outs = pl.pallas_call(
        body, name="s5_fwd", grid=(steps,),
        in_specs=[pl.BlockSpec((tb, c), lambda i: (i, 0)), _full_spec(bc.shape), _full_spec(abar.shape), _full_spec(cc.shape)]
        + [_HBM] * nsh,
        out_specs=[pl.BlockSpec((tb, w2), lambda i: (i, 0)), pl.BlockSpec((tb, c), lambda i: (i, 0))] + [_HBM] * nsh,
        out_shape=[_sds((t, w2), BF16), _sds((t, c))] + [_sds((4,) + a.shape, a.dtype) for a in shards],
        scratch_shapes=[pltpu.VMEM((1, w2), F32), pltpu.VMEM((tb, w2), F32)] + (_gather_big_sems(nsh) if nsh else []),
        compiler_params=_cparams(("arbitrary",)),
    )(h, bc, abar, cc, *shards)
    return outs[0], outs[1], outs[2:]


def _s5_bwd(dy, s, abar, cc, bc, tb):
    t, c = dy.shape
    w2 = s.shape[1]
    w = w2 // 2
    kb, nb, nsb = _s5_blocks(c)
    nblk = t // tb
    pack = 16
    per = tb // pack
    nt = (((1,), (1,)), ((), ()))

    def body(dy_ref, s_ref, sprev_ref, a_ref, cc_ref, bc_ref, dbu_ref, dh_ref, da_ref, carry, da_acc, rows):
        i = pl.program_id(0)

        @pl.when(i == 0)
        def _():
            carry[...] = jnp.zeros_like(carry)
            da_acc[...] = jnp.zeros_like(da_acc)

        for j in range(w2 // nb):
            ch = (j % nsb) * kb
            rows[:, j * nb:(j + 1) * nb] = lax.dot_general(dy_ref[:, ch:ch + kb].astype(BF16), cc_ref[j * nb:(j + 1) * nb, :], nt,
                                                           preferred_element_type=F32)
        ar, ai = a_ref[:, :w], a_ref[:, w:]

        def step(n, state):
            gr, gi = state
            row = tb - 1 - n
            nr = rows[pl.ds(row, 1), :w] + ar * gr + ai * gi
            ni = rows[pl.ds(row, 1), w:] + ar * gi - ai * gr
            rows[pl.ds(row, 1), :w] = nr
            rows[pl.ds(row, 1), w:] = ni
            return nr, ni

        gr, gi = lax.fori_loop(0, tb, step, (carry[:, :w], carry[:, w:]))
        carry[:, :w] = gr
        carry[:, w:] = gi
        last = (lax.broadcasted_iota(jnp.int32, (pack, w2), 0) == pack - 1) & (i < nblk - 1)
        before = jnp.sum(jnp.where(last, sprev_ref[...].astype(F32), 0.0), axis=0, keepdims=True)
        rid = lax.broadcasted_iota(jnp.int32, (tb, w2), 0)
        sp = jnp.where(rid == 0, before, pltpu.roll(s_ref[...].astype(F32), 1, 0))
        g = rows[...]
        dbu_ref[...] = g.astype(BF16)
        spr, spi, g_r, g_i = sp[:, :w], sp[:, w:], g[:, :w], g[:, w:]
        da_acc[:, :w] += jnp.sum(spr * g_r + spi * g_i, axis=0, keepdims=True)
        da_acc[:, w:] += jnp.sum(spr * g_i - spi * g_r, axis=0, keepdims=True)
        for j in range(nsb):
            re, im = j * nb, w + j * nb
            dh_ref[:, j * kb:(j + 1) * kb] = (
                lax.dot_general(dbu_ref[:, re:re + nb], bc_ref[:, re:re + nb], nt, preferred_element_type=F32)
                + lax.dot_general(dbu_ref[:, im:im + nb], bc_ref[:, im:im + nb], nt, preferred_element_type=F32))

        @pl.when(i == nblk - 1)
        def _():
            da_ref[...] = da_acc[...]

    wide = pl.BlockSpec((tb, w2), lambda i: (nblk - 1 - i, 0))
    narrow = pl.BlockSpec((tb, c), lambda i: (nblk - 1 - i, 0))
    prev = pl.BlockSpec((pack, w2), lambda i: (jnp.maximum((nblk - 1 - i) * per - 1, 0), 0))
    return pl.pallas_call(
        body, name="s5_bwd", grid=(nblk,),
        in_specs=[narrow, wide, prev, _full_spec(abar.shape), _full_spec(cc.shape), _full_spec(bc.shape)],
        out_specs=[wide, narrow, pl.BlockSpec((1, w2), lambda i: (0, 0))],
        out_shape=[_sds((t, w2), BF16), _sds((t, c)), _sds((1, w2))],
        scratch_shapes=[pltpu.VMEM((1, w2), F32), pltpu.VMEM((1, w2), F32), pltpu.VMEM((tb, w2), F32)],
        compiler_params=_cparams(("arbitrary",)),
    )(dy, s, s, abar, cc, bc)


def _zoh_fwd(a_re, a_im, log_dt, b_re_t, b_im_t):
    def body(*refs):
        for r, val in zip(refs[5:], _f_zoh(*(x[...] for x in refs[:5]))):
            r[...] = val

    return pl.pallas_call(body, name="s5_zoh_fwd", out_shape=[_sds(a_re.shape)] * 2 + [_sds(b_re_t.shape)] * 2,
                          compiler_params=_cparams())(a_re, a_im, log_dt, b_re_t, b_im_t)


def _zoh_bwd(a_re, a_im, log_dt, b_re_t, b_im_t, couts):
    def body(*refs):
        _, vjp = jax.vjp(_f_zoh, *(x[...] for x in refs[:5]))
        for r, val in zip(refs[9:], vjp(tuple(x[...] for x in refs[5:9]))):
            r[...] = val

    ins = (a_re, a_im, log_dt, b_re_t, b_im_t)
    return pl.pallas_call(body, name="s5_zoh_bwd", out_shape=[_sds(x.shape) for x in ins],
                          compiler_params=_cparams())(*ins, *couts)


def _loss_head(h, target, tb):
    t, c = h.shape
    nb = t // tb

    def body(h_ref, t_ref, loss_ref, dh_ref, acc):
        i = pl.program_id(0)

        @pl.when(i == 0)
        def _():
            acc[...] = jnp.zeros_like(acc)

        d = h_ref[...] - t_ref[...]
        dh_ref[...] = d * (1.0 / c)
        acc[...] += 0.5 * jnp.sum(jnp.mean(d * d, axis=-1, keepdims=True), axis=0, keepdims=True)

        @pl.when(i == nb - 1)
        def _():
            loss_ref[...] = jnp.broadcast_to(acc[...], loss_ref.shape)

    row = pl.BlockSpec((tb, c), lambda i: (i, 0))
    return pl.pallas_call(
        body, name="loss_head", grid=(nb,), in_specs=[row, row],
        out_specs=[pl.BlockSpec((8, 128), lambda i: (0, 0)), row], out_shape=[_sds((8, 128)), _sds((t, c))],
        scratch_shapes=[pltpu.VMEM((1, 1), F32)], compiler_params=_cparams(("arbitrary",)),
    )(h, target)


def _rows_tile(rows):
    for cand in (512, 256, 128, 64, 32, 16, 8):
        if rows % cand == 0:
            return cand
    return rows


def _grad_x(here, from_next):
    rows, cols = here[0].shape
    tb = _rows_tile(rows)
    nb = rows // tb
    nh, nn = len(here), len(from_next)
    sub = 8

    def body(*refs):
        i = pl.program_id(0)
        total = functools.reduce(jnp.add, [r[...] for r in refs[:nh]])
        shifted = functools.reduce(jnp.add, [r[...] for r in refs[nh:nh + nn]])
        first_next = functools.reduce(jnp.add, [r[0:1, :] for r in refs[nh + nn:nh + 2 * nn]])
        first_next = jnp.where(i == nb - 1, 0.0, first_next)
        rid = lax.broadcasted_iota(jnp.int32, (tb, cols), 0)
        refs[-1][...] = total + jnp.where(rid == tb - 1, first_next, pltpu.roll(shifted, tb - 1, 0))

    blk = pl.BlockSpec((tb, cols), lambda i: (i, 0))
    nxt = pl.BlockSpec((sub, cols), lambda i: (jnp.minimum(i + 1, nb - 1) * (tb // sub), 0))
    return pl.pallas_call(body, name="grad_x", grid=(nb,), in_specs=[blk] * (nh + nn) + [nxt] * nn, out_specs=blk,
                          out_shape=_sds((rows, cols)), compiler_params=_cparams(("parallel",)))(*here, *from_next, *from_next)


def _adamw_math(w, g, m, v):
    m = ADAM_B1 * m + (1.0 - ADAM_B1) * g
    v = ADAM_B2 * v + (1.0 - ADAM_B2) * jnp.square(g)
    m_hat = m / (1.0 - ADAM_B1 ** ADAM_STEP)
    v_hat = v / (1.0 - ADAM_B2 ** ADAM_STEP)
    delta = -ADAM_LR * (m_hat / (jnp.sqrt(v_hat) + ADAM_EPS) + ADAM_WD * w)
    return delta, m, v


def _adamw(name, parts, w, m, v):
    rows, cols = w.shape
    tb = _rows_tile(rows)
    npart = len(parts)

    def body(*refs):
        g = refs[0][...]
        for r in refs[1:npart]:
            g = g + r[...]
        w_ref, m_ref, v_ref = refs[npart:npart + 3]
        g_out, d_out, m_out, v_out = refs[npart + 3:]
        delta, mn, vn = _adamw_math(w_ref[...], g, m_ref[...], v_ref[...])
        g_out[...] = g
        d_out[...] = delta
        m_out[...] = mn
        v_out[...] = vn

    blk = pl.BlockSpec((tb, cols), lambda i: (i, 0))
    return pl.pallas_call(body, name=name, grid=(rows // tb,), in_specs=[blk] * (npart + 3), out_specs=[blk] * 4,
                          out_shape=[_sds((rows, cols))] * 4, compiler_params=_cparams(("parallel",)))(*parts, w, m, v)


def _shift_down(a):
    return jnp.concatenate([jnp.zeros_like(a[:1]), a[:-1]], axis=0)


def _s5_pack_mask(g):
    return (jnp.arange(g)[None, :] % S5_PACK == jnp.arange(S5_PACK)[:, None]).astype(F32)


def _compact_b(bbar_t):
    g, s, p = bbar_t.shape
    return (_s5_pack_mask(g)[:, None, :, None] * bbar_t.transpose(1, 0, 2)[None]).reshape(S5_PACK * s, g * p)


def _compact_b_t(dense, g):
    s, p = dense.shape[0] // S5_PACK, dense.shape[1] // g
    return jnp.sum(dense.reshape(S5_PACK, s, g, p) * _s5_pack_mask(g)[:, None, :, None], axis=0).transpose(1, 0, 2)


def _compact_c(c_w):
    g, s, p = c_w.shape
    return (c_w.transpose(0, 2, 1)[:, :, None, :] * _s5_pack_mask(g).T[:, None, :, None]).reshape(g * p, S5_PACK * s)


def _compact_c_t(dense, g):
    p, s = dense.shape[0] // g, dense.shape[1] // S5_PACK
    return jnp.sum(dense.reshape(g, p, S5_PACK, s) * _s5_pack_mask(g).T[:, None, :, None], axis=2).transpose(0, 2, 1)


def _local_step(x, target, fw, core):
    t, c = x.shape
    nh = c // HEAD
    ng = c // SSM_GROUP
    tb = min(256, t)
    tbm = min(512, t)
    tbmf = min(1024, t)
    tbmb = min(512, t)
    tbs = min(256, t)
    tk5 = min(2048, t)
    hg = min(16, nh)
    mu = [fw['rw_mu'][i:i + 1] for i in range(6)]
    ln_g = [fw['ln_g'][i:i + 1] for i in range(4)]
    ln_b = [fw['ln_b'][i:i + 1] for i in range(4)]
    grads = {}

    xp = _shift_down(x)
    proj_params = {n: (mu[i], fw['rw_w' + n]) for n, i in (('r', 0), ('k', 2), ('v', 3))}
    raw = {n: _stage_fwd("proj_" + n, _f_proj, (x, xp), proj_params[n], (c,), tbm)[0] for n in 'rkv'}
    lora_params = (mu[1], mu[4], mu[5], fw['rw_w0'], fw['rw_w1'], fw['rw_w2'], fw['rw_a0'], fw['rw_a1'], fw['rw_a2'],
                   fw['rw_g1'], fw['rw_g2'], fw['rw_k_k'], fw['rw_k_a'])
    lw, k2, an, bb, gate = _stage_fwd("lora", _f_lora, (x, xp, raw['k']), lora_params, (c,) * 5, tbm)
    rec_in = (raw['r'], lw, k2, raw['v'], an, bb)
    o, s0s, (wo_view, w1_l0, w2_l0) = _rec_fwd(*rec_in, hg, fw['late_a'])
    fw = dict(fw, rw_wo=wo_view.reshape(c, c))
    mlp_w = [(w1_l0.reshape(4, 1, c, -1), w2_l0.reshape(4, 1, -1, c)), None]
    post_params = (fw['rw_lnx_g'], fw['rw_lnx_b'], fw['rw_r_k'], fw['rw_wo'], ln_g[0], ln_b[0])
    post_acts = (o, raw['r'], k2, raw['v'], gate, x)
    h1, = _stage_fwd("post", _f_post, post_acts, post_params, (c,), tb)
    h2, s_mlp0, (w1_l1,) = _mlp_fwd("mlp0_fwd", h1, *mlp_w[0], 0, ln_g[1], ln_b[1], tbmf, fw['late_b'])

    a_re, a_im, log_dt = fw['s5_a_re'], fw['s5_a_im'], fw['s5_log_dt']
    b_re_t, b_im_t = fw['s5_b_re'].transpose(0, 2, 1), fw['s5_b_im'].transpose(0, 2, 1)
    abar_re, abar_im, bbar_re_t, bbar_im_t = _zoh_fwd(a_re, a_im, log_dt, b_re_t, b_im_t)
    abar = jnp.concatenate([abar_re.reshape(1, -1), abar_im.reshape(1, -1)], axis=1)
    bc = jnp.concatenate([_compact_b(bbar_re_t), _compact_b(bbar_im_t)], axis=1).astype(BF16)
    cc = jnp.concatenate([_compact_c(fw['s5_c_re']), -_compact_c(fw['s5_c_im'])], axis=0).astype(BF16)
    st, ys, (glu_view, w2_l1) = _s5_fwd(h2, bc, abar, cc, tbs, fw['late_c'])
    mlp_w[1] = (w1_l1.reshape(4, 1, c, -1), w2_l1.reshape(4, 1, -1, c))
    fw = dict(fw, s5_w_glu=tuple(glu_view[q] for q in range(4)))
    glu_params = (fw['s5_d'], *fw['s5_w_glu'], ln_g[2], ln_b[2])
    h3, = _stage_fwd("glu", _f_glu, (ys, h2), glu_params, (c,), tbm)
    h4, s_mlp1, _ = _mlp_fwd("mlp1_fwd", h3, *mlp_w[1], 0, ln_g[3], ln_b[3], tbmf)

    loss_blk, dh4 = _loss_head(h4, target, tb)

    dln_g, dln_b = [None] * 4, [None] * 4
    dh3, ds1, dhid1, act1, dln_g[3], dln_b[3] = _mlp_bwd("mlp1_bwd", h3, s_mlp1, dh4, *mlp_w[1], 0,
                                                         ln_g[3], ln_b[3], tbmb)
    dw1 = _mlp_weight_grad("mlp1_dw1", h3, dhid1, 1, DEPTH, "n")
    dw2 = _mlp_weight_grad("mlp1_dw2", act1, ds1, 1, DEPTH, "m")
    (dys, dh2_glu), (grads['s5_d'], *dglu, dln_g[2], dln_b[2]) = _stage_bwd(
        "glu_bwd", _f_glu, (ys, h2), glu_params, (dh3,), tb, proxied=(1, 2, 3, 4))
    grads['s5_w_glu'] = jnp.stack(dglu)
    dcc = _s5_weight_grad("s5_dcc", dys, st, True, tk5)
    dbu, dh2_bu, dabar = _s5_bwd(dys, st, abar, cc, bc, tbs)
    dbc = _s5_weight_grad("s5_dbc", h2, dbu, False, tk5)
    gp = ng * SSM_STATE
    grads['s5_c_re'] = _compact_c_t(dcc[:gp], ng)
    grads['s5_c_im'] = -_compact_c_t(dcc[gp:], ng)
    zoh_couts = (dabar[:, :gp].reshape(ng, SSM_STATE), dabar[:, gp:].reshape(ng, SSM_STATE),
                 _compact_b_t(dbc[:, :gp], ng), _compact_b_t(dbc[:, gp:], ng))
    grads['s5_a_re'], grads['s5_a_im'], grads['s5_log_dt'], db_re_t, db_im_t = _zoh_bwd(
        a_re, a_im, log_dt, b_re_t, b_im_t, zoh_couts)
    grads['s5_b_re'], grads['s5_b_im'] = db_re_t.transpose(0, 2, 1), db_im_t.transpose(0, 2, 1)
    dh2 = (dh2_glu, dh2_bu)

    dh1, ds0, dhid0, act0, dln_g[1], dln_b[1] = _mlp_bwd("mlp0_bwd", h1, s_mlp0, dh2, *mlp_w[0], 0,
                                                         ln_g[1], ln_b[1], tbmb)
    grads['mlp_w1'] = _mlp_weight_grad("mlp0_dw1", h1, dhid0, 0, DEPTH, "n", into=dw1)
    grads['mlp_w2'] = _mlp_weight_grad("mlp0_dw2", act0, ds0, 0, DEPTH, "m", into=dw2)
    ready_views = [grads[n].reshape(4, -1, grads[n].shape[-1]) for n in BIG_READY]
    (do, dr_p, dk2_p, dv_p, dgate, dx_post), post_g, ready_others = _stage_bwd(
        "post_bwd", _f_post, post_acts, post_params, (dh1,), tb, proxied=(3,), halves_of=ready_views)
    grads['rw_lnx_g'], grads['rw_lnx_b'], grads['rw_r_k'], grads['rw_wo'], dln_g[0], dln_b[0] = post_g
    ready_sums = [_half_add(f"half_add_a{i}", v, o, core) for i, (v, o) in enumerate(zip(ready_views, ready_others))]
    rec_g, ready_lands = _rec_bwd(*rec_in, s0s, do, hg, ready_sums)
    reduced = dict(zip(BIG_READY, zip(ready_sums, ready_lands)))
    dr_r, dlw, dk2_r, dv_r, dan, dbb = rec_g
    dk2 = (dk2_p, dk2_r)
    (dx_l, dxp_l, dkraw_l), lora_g = _stage_bwd("lora_bwd", _f_lora, (x, xp, raw['k']), lora_params,
                                                (dlw, dk2, dan, dbb, dgate), tb)
    (dmu_w, dmu_a, dmu_g, grads['rw_w0'], grads['rw_w1'], grads['rw_w2'], grads['rw_a0'], grads['rw_a1'], grads['rw_a2'],
     grads['rw_g1'], grads['rw_g2'], grads['rw_k_k'], grads['rw_k_a']) = lora_g
    dproj = {'r': (dr_p, dr_r), 'k': dkraw_l, 'v': (dv_p, dv_r)}
    dxs, dxps, dmu = [dx_post, dx_l], [dxp_l], {}
    for n in 'rkv':
        (dx_n, dxp_n), (dmu[n], grads['rw_w' + n]) = _stage_bwd("proj_bwd_" + n, _f_proj, (x, xp), proj_params[n],
                                                                 (dproj[n],), tbm, proxied=(1,))
        dxs.append(dx_n)
        dxps.append(dxp_n)
    grads['rw_mu'] = jnp.concatenate([dmu['r'], dmu_w, dmu['k'], dmu['v'], dmu_a, dmu_g], axis=0)
    grads['ln_g'] = jnp.concatenate(dln_g, axis=0)
    grads['ln_b'] = jnp.concatenate(dln_b, axis=0)
    grad_x = _grad_x(dxs, dxps)
    late = [n for n in BIG if n not in BIG_READY]
    late_sums = dict(zip(late, _chip_sums("b", [grads[n] for n in late], core)))
    return loss_blk, grad_x, grads, reduced, late_sums


def _position():
    return lax.axis_index("x"), lax.axis_index("y"), lax.axis_index("c")


def _other_chips(x, y):
    return [(1 - x, y), (x, 1 - y), (1 - x, 1 - y)]


def _chip_slice(ref, axis, q, size):
    idx = [slice(None)] * len(ref.shape)
    idx[axis] = pl.ds(pl.multiple_of(q * size, size), size)
    return ref.at[tuple(idx)]


_HBM = pl.BlockSpec(memory_space=pltpu.HBM)


def _gather_small_phases(src, dst, axes, sems):
    n = len(src)
    send_sems, recv_sems, own_sems = sems
    x, y, c = _position()
    chips = _other_chips(x, y)
    sizes = [src[a].shape[axes[a]] for a in range(n)]

    def copy(a, k, q):
        return pltpu.make_async_remote_copy(
            src_ref=src[a], dst_ref=_chip_slice(dst[a], axes[a], q, sizes[a]), send_sem=send_sems.at[a, k],
            recv_sem=recv_sems.at[a, k], device_id=(*chips[k], c), device_id_type=MESH)

    def own(a):
        return pltpu.make_async_copy(src[a], _chip_slice(dst[a], axes[a], 2 * x + y, sizes[a]), own_sems.at[a])

    def start():
        for a in range(n):
            own(a).start()
            for k in range(3):
                copy(a, k, 2 * x + y).start()

    def finish():
        for a in range(n):
            for k, (cx, cy) in enumerate(chips):
                copy(a, k, 2 * cx + cy).wait_recv()
        for a in range(n):
            for k in range(3):
                copy(a, k, 2 * x + y).wait_send()
            own(a).wait()

    return start, finish


def _gather_early(big, small, axes):
    nb, ns = len(big), len(small)
    full_shapes = [tuple(s * 4 if i == ax else s for i, s in enumerate(a.shape)) for a, ax in zip(small, axes)]

    def body(*refs):
        src_b, src_s = refs[:nb], refs[nb:nb + ns]
        dst_b, dst_s = refs[nb + ns:2 * nb + ns], refs[2 * nb + ns:2 * (nb + ns)]
        sems = refs[2 * (nb + ns):]
        small_start, small_finish = _gather_small_phases(src_s, dst_s, axes, sems[5:])
        small_start()
        for phase in _gather_big_phases(src_b, dst_b, sems[:5]):
            phase()
        small_finish()

    outs = pl.pallas_call(
        body, name="gather_early", in_specs=[_HBM] * (nb + ns), out_specs=[_HBM] * (nb + ns),
        out_shape=[_sds((4,) + a.shape, a.dtype) for a in big] + [_sds(s, a.dtype) for s, a in zip(full_shapes, small)],
        scratch_shapes=_gather_big_sems(nb) + [pltpu.SemaphoreType.DMA((ns, 3)), pltpu.SemaphoreType.DMA((ns, 3)),
                                               pltpu.SemaphoreType.DMA((ns,))],
        compiler_params=_cparams(),
    )(*big, *small)
    return outs[:nb], outs[nb:]


def _scatter_pieces(fulls, axes, sums):
    n, nsum = len(fulls), len(sums)
    sizes = [a.shape[ax] // 4 for a, ax in zip(fulls, axes)]
    shard_shapes = [tuple(sz if i == ax else s for i, s in enumerate(a.shape)) for a, ax, sz in zip(fulls, axes, sizes)]

    def body(*refs):
        src, big_src = refs[:n], refs[n:n + nsum]
        land, big_land = refs[n + nsum:2 * n + nsum], refs[2 * n + nsum:2 * (n + nsum)]
        send_sems, recv_sems = refs[2 * (n + nsum):2 * (n + nsum) + 2]
        big_start, big_finish = _scatter_big_phases(big_src, big_land, refs[2 * (n + nsum) + 2:])
        big_start()
        x, y, c = _position()
        chips = _other_chips(x, y)

        def copy(a, k):
            cx, cy = chips[k]
            return pltpu.make_async_remote_copy(
                src_ref=_chip_slice(src[a], axes[a], 2 * cx + cy, sizes[a]), dst_ref=land[a].at[k],
                send_sem=send_sems.at[a, k], recv_sem=recv_sems.at[a, k], device_id=(cx, cy, c), device_id_type=MESH)

        for a in range(n):
            for k in range(3):
                copy(a, k).start()
        for a in range(n):
            for k in range(3):
                copy(a, k).wait_recv()
        for a in range(n):
            for k in range(3):
                copy(a, k).wait_send()
        big_finish()

    outs = pl.pallas_call(
        body, name="scatter_grads", in_specs=[_HBM] * (n + nsum), out_specs=[_HBM] * (n + nsum),
        out_shape=[_sds((3,) + s) for s in shard_shapes] + [_sds((3,) + s.shape[1:], s.dtype) for s in sums],
        scratch_shapes=[pltpu.SemaphoreType.DMA((n, 3)), pltpu.SemaphoreType.DMA((n, 3))] + _scatter_big_sems(nsum),
        compiler_params=_cparams(),
    )(*fulls, *sums)
    return outs[:n], outs[n:]


def _sibling_swap(name, arrs):
    n = len(arrs)

    def body(*refs):
        src, dst = refs[:n], refs[n:2 * n]
        send_sems, recv_sems = refs[2 * n:]
        x, y, c = _position()
        copies = [pltpu.make_async_remote_copy(src_ref=src[a], dst_ref=dst[a], send_sem=send_sems.at[a], recv_sem=recv_sems.at[a],
                                               device_id=(x, y, 1 - c), device_id_type=MESH) for a in range(n)]
        for cp in copies:
            cp.start()
        for cp in copies:
            cp.wait_recv()
        for cp in copies:
            cp.wait_send()

    return pl.pallas_call(
        body, name=name, in_specs=[_HBM] * n, out_specs=[_HBM] * n, out_shape=[_sds(a.shape) for a in arrs],
        scratch_shapes=[pltpu.SemaphoreType.DMA((n,)), pltpu.SemaphoreType.DMA((n,))],
        compiler_params=_cparams(),
    )(*arrs)


def _sum4(name, own, land):
    rows, cols = own.shape
    tb = _rows_tile(rows)

    def body(o_ref, l0, l1, l2, out_ref):
        out_ref[...] = ((o_ref[...] + l0[...]) + l1[...]) + l2[...]

    blk = pl.BlockSpec((tb, cols), lambda i: (i, 0))
    lands = [pl.BlockSpec((None, tb, cols), functools.partial(lambda k, i: (k, i, 0), k)) for k in range(3)]
    return pl.pallas_call(body, name=name, grid=(rows // tb,), in_specs=[blk] + lands, out_specs=blk,
                          out_shape=_sds((rows, cols)), compiler_params=_cparams(("parallel",)))(own, land, land, land)


def _allreduce_adamw_small(g, w, m, v):
    rows, lanes = g.shape

    def body(g_ref, w_ref, m_ref, v_ref, gs_ref, d_ref, mn_ref, vn_ref, land, send_sems, recv_sems):
        x, y, c = _position()
        me = 4 * x + 2 * y + c
        chips = _other_chips(x, y)
        sibling = (x, y, 1 - c)

        def slot_of(cx, cy, cc):
            return 4 * cx + 2 * cy + cc

        def copy(j, slot, to, src=None):
            return pltpu.make_async_remote_copy(src_ref=g_ref if src is None else land.at[src], dst_ref=land.at[slot],
                                                send_sem=send_sems.at[j], recv_sem=recv_sems.at[j], device_id=to, device_id_type=MESH)

        copy(0, me, sibling).start()
        for k, chip in enumerate(chips):
            copy(1 + k, me, (*chip, c)).start()
        land[me] = g_ref[...]
        for k, chip in enumerate(chips):
            theirs = slot_of(*chip, c)
            copy(1 + k, theirs, (*chip, c)).wait_recv()
            copy(4 + k, theirs, sibling, src=theirs).start()
        copy(0, slot_of(*sibling), sibling).wait_recv()
        for k, chip in enumerate(chips):
            copy(4 + k, slot_of(*chip, 1 - c), sibling).wait_recv()
        copy(0, me, sibling).wait_send()
        for k, chip in enumerate(chips):
            copy(1 + k, me, (*chip, c)).wait_send()
            copy(4 + k, slot_of(*chip, c), sibling, src=slot_of(*chip, c)).wait_send()
        total = land[0]
        for dev in range(1, 8):
            total = total + land[dev]
        delta, mn, vn = _adamw_math(w_ref[...], total, m_ref[...], v_ref[...])
        gs_ref[...] = total
        d_ref[...] = delta
        mn_ref[...] = mn
        vn_ref[...] = vn

    vmem = pl.BlockSpec(memory_space=pltpu.VMEM)
    return pl.pallas_call(
        body, name="allreduce_adamw_small", in_specs=[vmem] * 4, out_specs=[vmem] * 4, out_shape=[_sds((rows, lanes))] * 4,
        scratch_shapes=[pltpu.VMEM((8, rows, lanes), F32), pltpu.SemaphoreType.DMA((7,)), pltpu.SemaphoreType.DMA((7,))],
        compiler_params=_cparams(),
    )(g, w, m, v)


def _row_half(ref, c):
    r2 = ref.shape[-2] // 2
    lead = (slice(None),) * (len(ref.shape) - 2)
    return ref.at[(*lead, pl.ds(pl.multiple_of(c * r2, r2), r2), slice(None))]


def _gather_big_phases(src, dst, sems):
    n = len(src)
    ici_send, ici_recv, d2d_send, d2d_recv, own_sems = sems
    x, y, c = _position()
    me = 2 * x + y
    chips = _other_chips(x, y)
    ids = [2 * cx + cy for cx, cy in chips]

    def ici(a, k, q):
        return pltpu.make_async_remote_copy(
            src_ref=_row_half(src[a], c), dst_ref=_row_half(dst[a].at[q], c), send_sem=ici_send.at[a, k],
            recv_sem=ici_recv.at[a, k], device_id=(*chips[k], c), device_id_type=MESH)

    def d2d(a, k, half):
        where = _row_half(dst[a].at[ids[k]], half)
        return pltpu.make_async_remote_copy(src_ref=where, dst_ref=where, send_sem=d2d_send.at[a, k], recv_sem=d2d_recv.at[a, k],
                                            device_id=(x, y, 1 - c), device_id_type=MESH)

    def own(a):
        return pltpu.make_async_copy(src[a], dst[a].at[me], own_sems.at[a])

    def start():
        for a in range(n):
            own(a).start()
            for k in range(3):
                ici(a, k, me).start()

    def forward():
        for a in range(n):
            for k in range(3):
                ici(a, k, ids[k]).wait_recv()
                d2d(a, k, c).start()

    def finish():
        for a in range(n):
            for k in range(3):
                d2d(a, k, 1 - c).wait_recv()
        for a in range(n):
            for k in range(3):
                ici(a, k, me).wait_send()
                d2d(a, k, c).wait_send()
            own(a).wait()

    return start, forward, finish


def _gather_big_sems(n):
    return [pltpu.SemaphoreType.DMA((n, 3))] * 4 + [pltpu.SemaphoreType.DMA((n,))]


def _chip_sums(tag, grads, core):
    views = [g.reshape(4, -1, g.shape[-1]) for g in grads]
    others = _sibling_halves("sibling_halves_" + tag, views)
    return [_half_add(f"half_add_{tag}{i}", v, o, core) for i, (v, o) in enumerate(zip(views, others))]


def _sibling_halves_phases(src, dst, sems):
    n = len(src)
    send_sems, recv_sems = sems
    x, y, c = _position()

    def copy(a):
        return pltpu.make_async_remote_copy(src_ref=_row_half(src[a], 1 - c), dst_ref=dst[a], send_sem=send_sems.at[a],
                                            recv_sem=recv_sems.at[a], device_id=(x, y, 1 - c), device_id_type=MESH)

    def start():
        for a in range(n):
            copy(a).start()

    def finish():
        for a in range(n):
            copy(a).wait_recv()
        for a in range(n):
            copy(a).wait_send()

    return start, finish


def _sibling_halves_sems(n):
    return [pltpu.SemaphoreType.DMA((n,)), pltpu.SemaphoreType.DMA((n,))]


def _sibling_halves_shapes(views):
    return [_sds((4, v.shape[1] // 2, v.shape[2])) for v in views]


def _sibling_halves(name, views):
    n = len(views)

    def body(*refs):
        for phase in _sibling_halves_phases(refs[:n], refs[n:2 * n], refs[2 * n:]):
            phase()

    return pl.pallas_call(
        body, name=name, in_specs=[_HBM] * n, out_specs=[_HBM] * n, out_shape=_sibling_halves_shapes(views),
        scratch_shapes=_sibling_halves_sems(n), compiler_params=_cparams(),
    )(*views)


def _rows_tile_capped(rows, cap=256):
    return min(_rows_tile(rows), cap)


def _half_add(name, view, other, core):
    _, r, k = view.shape
    r2 = r // 2
    tr = _rows_tile_capped(r2, 512)
    per = r2 // tr

    def body(c_ref, v_ref, o_ref, out_ref):
        out_ref[...] = (v_ref[...] + o_ref[...]).astype(BF16)

    blk = pl.BlockSpec((None, tr, k), lambda q, i, c: (q, i, 0))
    return pl.pallas_call(
        body, name=name,
        grid_spec=pltpu.PrefetchScalarGridSpec(
            num_scalar_prefetch=1, grid=(4, per),
            in_specs=[pl.BlockSpec((None, tr, k), lambda q, i, c: (q, c[0] * per + i, 0)), blk], out_specs=blk),
        out_shape=_sds((4, r2, k), BF16), compiler_params=_cparams(("parallel", "parallel")),
    )(core, view, other)


def _scatter_big_phases(src, land, sems):
    n = len(src)
    send_sems, recv_sems = sems
    x, y, c = _position()
    chips = _other_chips(x, y)

    def copy(a, k):
        cx, cy = chips[k]
        return pltpu.make_async_remote_copy(src_ref=src[a].at[2 * cx + cy], dst_ref=land[a].at[k], send_sem=send_sems.at[a, k],
                                            recv_sem=recv_sems.at[a, k], device_id=(cx, cy, c), device_id_type=MESH)

    def start():
        for a in range(n):
            for k in range(3):
                copy(a, k).start()

    def finish():
        for a in range(n):
            for k in range(3):
                copy(a, k).wait_recv()
        for a in range(n):
            for k in range(3):
                copy(a, k).wait_send()

    return start, finish


def _scatter_big_sems(n):
    return [pltpu.SemaphoreType.DMA((n, 3)), pltpu.SemaphoreType.DMA((n, 3))]


def _sum4_big(name, sums, land, chip):
    _, r2, k = sums.shape
    tr = _rows_tile_capped(r2, 512)

    def body(q_ref, s_ref, l0, l1, l2, out_ref):
        out_ref[...] = ((s_ref[...].astype(F32) + l0[...].astype(F32)) + l1[...].astype(F32)) + l2[...].astype(F32)

    lands = [pl.BlockSpec((None, tr, k), functools.partial(lambda j, i, q: (j, i, 0), j)) for j in range(3)]
    return pl.pallas_call(
        body, name=name,
        grid_spec=pltpu.PrefetchScalarGridSpec(
            num_scalar_prefetch=1, grid=(r2 // tr,),
            in_specs=[pl.BlockSpec((None, tr, k), lambda i, q: (q[0], i, 0))] + lands,
            out_specs=pl.BlockSpec((tr, k), lambda i, q: (i, 0))),
        out_shape=_sds((r2, k)), compiler_params=_cparams(("parallel",)),
    )(chip, sums, land, land, land)


def _adamw_halves(name, mine, theirs, w, m, v, core):
    r, k = w.shape
    r2 = r // 2
    tr = _rows_tile_capped(r2, 512)
    per = r2 // tr

    def body(c_ref, mine_ref, theirs_ref, w_ref, m_ref, v_ref, g_out, d_out, m_out, v_out):
        g = jnp.where(pl.program_id(0) == c_ref[0], mine_ref[...], theirs_ref[...])
        delta, mn, vn = _adamw_math(w_ref[...], g, m_ref[...], v_ref[...])
        g_out[...] = g
        d_out[...] = delta
        m_out[...] = mn
        v_out[...] = vn

    half = pl.BlockSpec((tr, k), lambda h, i, c: (i, 0))
    full = pl.BlockSpec((tr, k), lambda h, i, c: (h * per + i, 0))
    return pl.pallas_call(
        body, name=name,
        grid_spec=pltpu.PrefetchScalarGridSpec(num_scalar_prefetch=1, grid=(2, per), in_specs=[half, half, full, full, full],
                                               out_specs=[full] * 4),
        out_shape=[_sds((r, k))] * 4, compiler_params=_cparams(("parallel", "parallel")),
    )(core, mine, theirs, w, m, v)


def _drops_layer_axis(name):
    return not (name.startswith('mlp') or name == 's5_d')


def _work(name, arr):
    return arr.reshape(arr.shape[1:]) if _drops_layer_axis(name) else arr


def _work_axis(name):
    return SHARD_AXIS[name] - (1 if _drops_layer_axis(name) else 0)


def _as2d(a):
    return a.reshape(-1, a.shape[-1])


def _replicated_2d(name, arr):
    if name in ('ln_g', 'ln_b'):
        return arr
    if name == 'rw_r_k':
        return arr.reshape(1, -1)
    if name == 's5_log_dt':
        return arr.reshape(-1, 1)
    if name.startswith('s5_'):
        return arr.reshape(arr.shape[1:])
    return arr


def _pack(arrs):
    flat = []
    for a in arrs:
        f = a.reshape(-1)
        flat.append(jnp.pad(f, (0, -f.shape[0] % 128)))
    f = jnp.concatenate(flat)
    f = jnp.pad(f, (0, -f.shape[0] % 1024))
    return f.reshape(-1, 128)


def _unpack(packed, shapes):
    flat = packed.reshape(-1)
    out, at = [], 0
    for s in shapes:
        size = math.prod(s)
        out.append(flat[at:at + size].reshape(s))
        at += size + (-size % 128)
    return out


def kernel(x, ln_g, ln_b, rw_mu, rw_w0, rw_w1, rw_w2, rw_a0, rw_a1, rw_a2, rw_g1, rw_g2, rw_k_k, rw_k_a, rw_r_k, rw_wr, rw_wk, rw_wv, rw_wo, rw_lnx_g, rw_lnx_b, s5_a_re, s5_a_im, s5_log_dt, s5_b_re, s5_b_im, s5_c_re, s5_c_im, s5_d, s5_w_glu, mlp_w1, mlp_w2, loss_target, m_ln_g, m_ln_b, m_rw_mu, m_rw_w0, m_rw_w1, m_rw_w2, m_rw_a0, m_rw_a1, m_rw_a2, m_rw_g1, m_rw_g2, m_rw_k_k, m_rw_k_a, m_rw_r_k, m_rw_wr, m_rw_wk, m_rw_wv, m_rw_wo, m_rw_lnx_g, m_rw_lnx_b, m_s5_a_re, m_s5_a_im, m_s5_log_dt, m_s5_b_re, m_s5_b_im, m_s5_c_re, m_s5_c_im, m_s5_d, m_s5_w_glu, m_mlp_w1, m_mlp_w2, v_ln_g, v_ln_b, v_rw_mu, v_rw_w0, v_rw_w1, v_rw_w2, v_rw_a0, v_rw_a1, v_rw_a2, v_rw_g1, v_rw_g2, v_rw_k_k, v_rw_k_a, v_rw_r_k, v_rw_wr, v_rw_wk, v_rw_wv, v_rw_wo, v_rw_lnx_g, v_rw_lnx_b, v_s5_a_re, v_s5_a_im, v_s5_log_dt, v_s5_b_re, v_s5_b_im, v_s5_c_re, v_s5_c_im, v_s5_d, v_s5_w_glu, v_mlp_w1, v_mlp_w2):
    d = dict(locals())
    x_pos, y_pos, c_pos = _position()
    chip = 2 * x_pos + y_pos
    chip_arr = jnp.reshape(chip, (1,)).astype(jnp.int32)
    core_arr = jnp.reshape(c_pos, (1,)).astype(jnp.int32)

    small = [n for n in SHARD_AXIS if n not in BIG]
    axes = [_work_axis(n) for n in small]
    big_views, small_fulls = _gather_early([_as2d(d[n]).astype(BF16) for n in BIG_EARLY], [_work(n, d[n]) for n in small], axes)
    views = dict(zip(BIG_EARLY, big_views))
    fw = dict(zip(small, small_fulls))
    c_model = d['x'].shape[-1]
    for n in BIG_EARLY:
        fw[n] = views[n].reshape(c_model, c_model)
    w1_layers, w2_layers = d['mlp_w1'].astype(BF16), d['mlp_w2'].astype(BF16)
    fw['late_a'] = [_as2d(d['rw_wo']).astype(BF16), w1_layers[0], w2_layers[0]]
    fw['late_b'] = [w1_layers[1]]
    fw['late_c'] = [_as2d(d['s5_w_glu']).astype(BF16), w2_layers[1]]
    for n in REPLICATED:
        fw[n] = _replicated_2d(n, d[n])

    loss_blk, grad_x, grads, reduced, late_sums = _local_step(d['x'][0], d['loss_target'][0], fw, core_arr)
    loss = lax.psum(loss_blk[0, 0], ('x', 'y', 'c'))
    out = {}

    pieces = [grads[n] for n in small]
    lands, late_lands = _scatter_pieces(pieces, axes, list(late_sums.values()))
    reduced.update(zip(late_sums, zip(late_sums.values(), late_lands)))
    mine = [_sum4_big("sum4_" + n, *reduced[n], chip_arr) for n in BIG]
    for n, g, ax, land in zip(small, pieces, axes, lands):
        size = g.shape[ax] // 4
        mine.append(_sum4("sum4_" + n, lax.dynamic_slice_in_dim(g, chip * size, size, ax), land))
    theirs = _sibling_swap("swap_sums", mine)
    for n, a, b in zip(BIG + small, mine, theirs):
        w2d, m2d, v2d = _as2d(d[n]), _as2d(d['m_' + n]), _as2d(d['v_' + n])
        res = (_adamw_halves("adamw_" + n, a, b, w2d, m2d, v2d, core_arr) if n in BIG
               else _adamw("adamw_" + n, (a, b), w2d, m2d, v2d))
        out[n] = [r.reshape(d[n].shape) for r in res]

    rep_shapes = [d[n].shape for n in REPLICATED]
    packs = [_pack([grads[n] for n in REPLICATED])] + [_pack([d[p + n] for n in REPLICATED]) for p in ('', 'm_', 'v_')]
    res = [_unpack(p, rep_shapes) for p in _allreduce_adamw_small(*packs)]
    for i, n in enumerate(REPLICATED):
        out[n] = [r[i] for r in res]

    grad_x = grad_x.reshape(d['x'].shape)
    return (loss, grad_x, *[out[n][0] for n in WEIGHTS], *[out[n][1] for n in WEIGHTS],
            *[out[n][2] for n in WEIGHTS], *[out[n][3] for n in WEIGHTS])
```

```python
import functools
import math

import jax
import jax.numpy as jnp
from jax import lax
from jax.experimental import pallas as pl
from jax.experimental.pallas import tpu as pltpu

F32 = jnp.float32
BF16 = jnp.bfloat16
MESH = pl.DeviceIdType.MESH

HEAD = 64
SSM_GROUP = 16
SSM_STATE = 64
GN_EPS = 64e-5
LN_EPS = 1e-5
DEPTH = 2
DN_ALPHA = (2.0 * DEPTH) ** 0.25
ADAM_LR, ADAM_B1, ADAM_B2, ADAM_EPS, ADAM_WD, ADAM_STEP = 0.001, 0.9, 0.999, 1e-08, 0.01, 10
REC_CHUNK = 64
V7X_VMEM_BYTES = 64 * 2 ** 20
VMEM_LIMIT = V7X_VMEM_BYTES - 8 * 2 ** 20

WEIGHTS = ['ln_g', 'ln_b', 'rw_mu', 'rw_w0', 'rw_w1', 'rw_w2', 'rw_a0', 'rw_a1', 'rw_a2', 'rw_g1', 'rw_g2',
           'rw_k_k', 'rw_k_a', 'rw_r_k', 'rw_wr', 'rw_wk', 'rw_wv', 'rw_wo', 'rw_lnx_g', 'rw_lnx_b',
           's5_a_re', 's5_a_im', 's5_log_dt', 's5_b_re', 's5_b_im', 's5_c_re', 's5_c_im', 's5_d', 's5_w_glu',
           'mlp_w1', 'mlp_w2']
SHARD_AXIS = {'rw_mu': 2, 'rw_w1': 1, 'rw_w2': 2, 'rw_a1': 1, 'rw_a2': 2, 'rw_g1': 1, 'rw_g2': 2,
              'rw_wr': 1, 'rw_wk': 1, 'rw_wv': 1, 'rw_wo': 1, 's5_d': 1, 's5_w_glu': 2, 'mlp_w1': 2, 'mlp_w2': 1}
REPLICATED = [n for n in WEIGHTS if n not in SHARD_AXIS]
BIG_EARLY = ['rw_wr', 'rw_wk', 'rw_wv']
BIG_LATE = ['rw_wo', 's5_w_glu', 'mlp_w1', 'mlp_w2']
BIG = BIG_EARLY + BIG_LATE
BIG_READY = ['s5_w_glu', 'mlp_w1', 'mlp_w2']


def _sds(shape, dtype=F32):
    return jax.ShapeDtypeStruct(tuple(shape), dtype)


def _cparams(sem=None, **kw):
    if sem is not None:
        kw["dimension_semantics"] = sem
    return pltpu.CompilerParams(vmem_limit_bytes=VMEM_LIMIT, **kw)


def _mm_products(a, b, g):
    gb = g.astype(BF16)
    da = lax.dot_general(gb, b.astype(BF16), (((1,), (1,)), ((), ())), preferred_element_type=F32)
    db = lax.dot_general(a.astype(BF16), gb, (((0,), (0,)), ((), ())), preferred_element_type=F32)
    return da, db


@jax.custom_vjp
def _mm_plain(a, b):
    return jnp.dot(a.astype(BF16), b.astype(BF16), preferred_element_type=F32)


def _mm_plain_bwd(res, g):
    da, db = _mm_products(*res, g)
    return da.astype(res[0].dtype), db.astype(res[1].dtype)


_mm_plain.defvjp(lambda a, b: (_mm_plain(a, b), (a, b)), _mm_plain_bwd)


@jax.custom_vjp
def _mm_proxy(a, b, z):
    return jnp.dot(a.astype(BF16), b.astype(BF16), preferred_element_type=F32)


def _mm_proxy_bwd(res, g):
    da, db = _mm_products(*res, g)
    return da.astype(res[0].dtype), jnp.zeros_like(res[1]), db


_mm_proxy.defvjp(lambda a, b, z: (_mm_proxy(a, b, z), (a, b)), _mm_proxy_bwd)


def mm(a, b, z=None):
    return _mm_plain(a, b) if z is None else _mm_proxy(a, b, z)


def _split3(x):
    hi = x.astype(BF16)
    r1 = x - hi.astype(F32)
    mid = r1.astype(BF16)
    lo = (r1 - mid.astype(F32)).astype(BF16)
    return hi, mid, lo


def _head_sum_impl(x):
    c = x.shape[1]
    lanes = 128
    sel = (lax.broadcasted_iota(jnp.int32, (c, lanes), 0) // HEAD
           == lax.broadcasted_iota(jnp.int32, (c, lanes), 1)).astype(BF16)
    s = sum(jnp.dot(p, sel, preferred_element_type=F32) for p in _split3(x))
    return sum(lax.dot_general(p, sel, (((1,), (1,)), ((), ())), preferred_element_type=F32) for p in _split3(s))


@jax.custom_vjp
def head_sum(x):
    return _head_sum_impl(x)


head_sum.defvjp(lambda x: (_head_sum_impl(x), None), lambda _, g: (_head_sum_impl(g),))


def _ln(x, g, b):
    mu = jnp.mean(x, axis=-1, keepdims=True)
    xc = x - mu
    var = jnp.mean(xc * xc, axis=-1, keepdims=True)
    return xc * lax.rsqrt(var + LN_EPS) * g + b


def _f_proj(acts, params, proxies):
    x, xp = acts
    mu, w = params
    return (mm(x + (xp - x) * mu, w, proxies[1]),)


def _f_lora(acts, params, proxies):
    x, xp, kraw = acts
    mu_w, mu_a, mu_g, w0, w1, w2, a0, a1, a2, g1, g2, k_k, k_a = params
    xx = xp - x
    w_pre = w0 + mm(jnp.tanh(mm(x + xx * mu_w, w1)), w2)
    z = -w_pre
    softplus = jnp.maximum(z, 0.0) + jnp.log(1.0 + jnp.exp(-jnp.abs(z)))
    log_decay = -jnp.exp(-softplus - 0.5)
    a = jax.nn.sigmoid(a0 + mm(mm(x + xx * mu_a, a1), a2))
    g = mm(jax.nn.sigmoid(mm(x + xx * mu_g, g1)), g2)
    kk = kraw * k_k
    kkn = kk / jnp.maximum(jnp.sqrt(head_sum(kk * kk)), 1e-12)
    k2 = kraw * (1.0 + (a - 1.0) * k_a)
    return log_decay, k2, -kkn, kkn * a, g


def _f_post(acts, params, proxies):
    o, r, k2, v, g, x = acts
    lnx_g, lnx_b, r_k, wo, ln_g, ln_b = params
    om = head_sum(o) * (1.0 / HEAD)
    oc = o - om
    ov = head_sum(oc * oc) * (1.0 / HEAD)
    on = oc * lax.rsqrt(ov + GN_EPS) * lnx_g + lnx_b
    bonus = head_sum(r * k2 * r_k) * v
    y = mm((on + bonus) * g, wo, proxies[3])
    return (_ln(DN_ALPHA * x + y, ln_g, ln_b),)


def _f_glu(acts, params, proxies):
    ys, h = acts
    d, wv0, wv1, wg0, wg1, ln_g, ln_b = params
    y = jax.nn.gelu(ys + h * d)
    mix = jnp.concatenate([mm(y, wv0, proxies[1]) * jax.nn.sigmoid(mm(y, wg0, proxies[3])),
                           mm(y, wv1, proxies[2]) * jax.nn.sigmoid(mm(y, wg1, proxies[4]))], axis=1)
    return (_ln(DN_ALPHA * h + mix, ln_g, ln_b),)


def _f_zoh(a_re, a_im, log_dt, b_re_t, b_im_t):
    dt = jnp.exp(log_dt)
    lam_re = jnp.minimum(a_re, -1e-4)
    lam_im = a_im
    mag = jnp.exp(dt * lam_re)
    abar_re = mag * jnp.cos(dt * lam_im)
    abar_im = mag * jnp.sin(dt * lam_im)
    den = lam_re * lam_re + lam_im * lam_im
    nr, ni = abar_re - 1.0, abar_im
    coef_re = ((nr * lam_re + ni * lam_im) / den)[:, None, :]
    coef_im = ((ni * lam_re - nr * lam_im) / den)[:, None, :]
    return (abar_re, abar_im, coef_re * b_re_t - coef_im * b_im_t, coef_re * b_im_t + coef_im * b_re_t)


def _bdot16_raw(a, b, ca, cb):
    return lax.dot_general(a.astype(BF16), b.astype(BF16), (((ca,), (cb,)), ((0,), (0,))), preferred_element_type=F32)


@functools.partial(jax.custom_vjp, nondiff_argnums=(2, 3))
def _bdot16(a, b, ca, cb):
    return _bdot16_raw(a, b, ca, cb)


def _bdot16_bwd(ca, cb, res, g):
    a, b = res
    if (ca, cb) == (2, 1):
        return _bdot16_raw(g, b, 2, 2), _bdot16_raw(a, g, 1, 1)
    if (ca, cb) == (2, 2):
        return _bdot16_raw(g, b, 2, 1), _bdot16_raw(g, a, 1, 1)
    assert (ca, cb) == (1, 1)
    return _bdot16_raw(b, g, 2, 2), _bdot16_raw(a, g, 2, 1)


_bdot16.defvjp(lambda a, b, ca, cb: (_bdot16_raw(a, b, ca, cb), (a, b)), _bdot16_bwd)

def _time_sums(x, suffix):
    hg, ln, _ = x.shape
    row = lax.broadcasted_iota(jnp.int32, (hg, ln, ln), 1)
    col = lax.broadcasted_iota(jnp.int32, (hg, ln, ln), 2)
    tri = ((row <= col) if suffix else (row >= col)).astype(BF16)
    return sum(lax.dot_general(tri, p, (((2,), (1,)), ((0,), (0,))), preferred_element_type=F32) for p in _split3(x))


@jax.custom_vjp
def _time_cumsum(x):
    return _time_sums(x, False)


_time_cumsum.defvjp(lambda x: (_time_sums(x, False), None), lambda _, g: (_time_sums(g, True),))

_dot_score = _bdot16
_dot_inverse = _bdot16
_dot_value = _bdot16


def _rec_chunk(s0, r, lw, k, v, a, b):
    hg, ln, _ = r.shape
    row = lax.broadcasted_iota(jnp.int32, (hg, ln, ln), 1)
    col = lax.broadcasted_iota(jnp.int32, (hg, ln, ln), 2)
    incl, strict = row >= col, row > col
    cum = _time_cumsum(lw)
    total = jnp.sum(lw, axis=1, keepdims=True)
    e_cum, e_inv, e_prev, e_tail = jnp.exp(cum), jnp.exp(-cum), jnp.exp(cum - lw), jnp.exp(total - cum)
    rt, at, bt, kt = r * e_cum, a * e_prev, b * e_inv, k * e_inv
    ar = jnp.concatenate([at, rt], axis=1)
    on_b, on_k = _dot_score(ar, bt, 2, 2), _dot_score(ar, kt, 2, 2)
    aab, arb = jnp.where(strict, on_b[:, :ln], 0.0), jnp.where(incl, on_b[:, ln:], 0.0)
    aak, ark = jnp.where(strict, on_k[:, :ln], 0.0), jnp.where(incl, on_k[:, ln:], 0.0)
    p = (row == col).astype(F32) + aab
    m = aab
    for _ in range(int(math.log2(ln)) - 1):
        m = _dot_inverse(m, m, 2, 1)
        p = p + _dot_inverse(p, m, 2, 1)
    from_state = _dot_value(ar, s0, 2, 2)
    from_v = _dot_value(jnp.concatenate([aak, ark], axis=1), v, 2, 1)
    u = _dot_inverse(p, from_state[:, :ln] + from_v[:, :ln], 2, 1)
    o = from_state[:, ln:] + from_v[:, ln:] + _dot_value(arb, u, 2, 1)
    s1 = s0 * jnp.exp(total) + _dot_value(jnp.concatenate([u, v], axis=1),
                                          jnp.concatenate([b * e_tail, k * e_tail], axis=1), 1, 1)
    return o, s1


def _full_spec(shape):
    nd = len(shape)
    return pl.BlockSpec(tuple(shape), lambda *_: (0,) * nd)


def _stage_fwd(name, f, acts, params, out_dims, tb):
    t = acts[0].shape[0]
    na, npar = len(acts), len(params)

    def body(*refs):
        outs = f(tuple(r[...] for r in refs[:na]), tuple(r[...] for r in refs[na:na + npar]), (None,) * npar)
        for r, val in zip(refs[na + npar:], outs):
            r[...] = val

    return pl.pallas_call(
        body, name=name, grid=(t // tb,),
        in_specs=[pl.BlockSpec((tb, a.shape[1]), lambda i: (i, 0)) for a in acts] + [_full_spec(p.shape) for p in params],
        out_specs=[pl.BlockSpec((tb, d), lambda i: (i, 0)) for d in out_dims],
        out_shape=[_sds((t, d)) for d in out_dims],
        compiler_params=_cparams(("arbitrary",)),
    )(*acts, *params)


def _stage_bwd(name, f, acts, params, couts, tb, proxied=(), halves_of=()):
    nh = len(halves_of)
    t = acts[0].shape[0]
    groups = [c if isinstance(c, tuple) else (c,) for c in couts]
    couts = [term for grp in groups for term in grp]
    na, npar, nc = len(acts), len(params), len(couts)
    steps = t // tb

    def f_diff(act_vals, diff_vals, param_vals):
        real = tuple(param_vals[i] if i in proxied else diff_vals[i] for i in range(npar))
        proxies = tuple(diff_vals[i] if i in proxied else None for i in range(npar))
        return f(act_vals, real, proxies)

    def body(*refs):
        a_refs, p_hbm, c_refs = refs[:na], refs[na:na + npar], refs[na + npar:na + npar + nc]
        o = na + npar + nc
        half_src, o = refs[o:o + nh], o + nh
        da_refs, dp_hbm, half_dst = refs[o:o + na], refs[o + na:o + na + npar], refs[o + na + npar:o + na + npar + nh]
        o = o + na + npar + nh
        p_buf, acc, half_sems = refs[o:o + npar], refs[o + npar:o + 2 * npar], refs[o + 2 * npar:]
        i = pl.program_id(0)
        if nh:
            half_start, half_finish = _sibling_halves_phases(half_src, half_dst, half_sems)
            pl.when(i == 0)(half_start)

        @pl.when(i == 0)
        def _():
            for src, dst in zip(p_hbm, p_buf):
                pltpu.sync_copy(src, dst)
            for r in acc:
                r[...] = jnp.zeros_like(r)

        param_vals = tuple(r[...] for r in p_buf)
        diff_vals = tuple(jnp.zeros(v.shape, F32) if i in proxied else v for i, v in enumerate(param_vals))
        _, vjp = jax.vjp(functools.partial(f_diff, param_vals=param_vals), tuple(r[...] for r in a_refs), diff_vals)
        terms = iter(c_refs)
        d_acts, d_params = vjp(tuple(functools.reduce(jnp.add, [next(terms)[...] for _ in grp]) for grp in groups))
        for r, val in zip(da_refs, d_acts):
            r[...] = val
        for r, val in zip(acc, d_params):
            r[...] += val

        @pl.when(i == steps - 1)
        def _():
            for src, dst in zip(acc, dp_hbm):
                pltpu.sync_copy(src, dst)

        if nh:
            pl.when(i == steps - 1)(half_finish)

    hbm = pl.BlockSpec(memory_space=pltpu.HBM)
    outs = pl.pallas_call(
        body, name=name, grid=(steps,),
        in_specs=[pl.BlockSpec((tb, a.shape[1]), lambda i: (i, 0)) for a in acts] + [hbm] * npar
        + [pl.BlockSpec((tb, c.shape[1]), lambda i: (i, 0)) for c in couts] + [hbm] * nh,
        out_specs=[pl.BlockSpec((tb, a.shape[1]), lambda i: (i, 0)) for a in acts] + [hbm] * (npar + nh),
        out_shape=[_sds(a.shape) for a in acts] + [_sds(p.shape) for p in params] + _sibling_halves_shapes(halves_of),
        scratch_shapes=[pltpu.VMEM(p.shape, p.dtype) for p in params] + [pltpu.VMEM(p.shape, F32) for p in params]
        + (_sibling_halves_sems(nh) if nh else []),
        compiler_params=_cparams(("arbitrary",)),
    )(*acts, *params, *couts, *halves_of)
    if nh:
        return outs[:na], outs[na:na + npar], outs[na + npar:]
    return outs[:na], outs[na:]


def _tiled_matmul(name, a, b, mode, grid, a_spec, b_spec, o_spec, out_shape):
    nk = grid[2]
    dims = {"nn": ((1,), (0,)), "nt": ((1,), (1,)), "tn": ((0,), (0,))}[mode]

    def body(a_ref, b_ref, o_ref, acc):
        kk = pl.program_id(2)

        @pl.when(kk == 0)
        def _():
            acc[...] = jnp.zeros_like(acc)

        acc[...] += lax.dot_general(a_ref[...].astype(BF16), b_ref[...].astype(BF16), (dims, ((), ())),
                                    preferred_element_type=F32)

        @pl.when(kk == nk - 1)
        def _():
            o_ref[...] = acc[...]

    return pl.pallas_call(
        body, name=name, grid=grid, in_specs=[a_spec, b_spec], out_specs=o_spec, out_shape=_sds(out_shape),
        scratch_shapes=[pltpu.VMEM(o_spec.block_shape, F32)],
        compiler_params=_cparams(("parallel", "parallel", "arbitrary")),
    )(a, b)


def _mlp_weight_grad(name, a, b, layer, layers, split, into=None, tile=512):
    t, m = a.shape
    n = b.shape[1]
    tile = 2 * tile
    tk = min(2 * tile, t)
    if split == "n":
        tm, tn = min(tile, m), min(tile, n // 4)
        per = n // 4 // tn
        shape = (4, layers, m, n // 4)
        o_idx = lambda i, j, k: (j // per, layer, i, j % per)
    else:
        tm, tn = min(tile, m // 4), min(tile, n)
        per = m // 4 // tm
        shape = (4, layers, m // 4, n)
        o_idx = lambda i, j, k: (i // per, layer, i % per, j)
    nk = t // tk

    def body(a_ref, b_ref, *rest):
        o_ref, acc = rest[-2:]
        kk = pl.program_id(2)

        @pl.when(kk == 0)
        def _():
            acc[...] = jnp.zeros_like(acc)

        acc[...] += lax.dot_general(a_ref[...].astype(BF16), b_ref[...].astype(BF16), (((0,), (0,)), ((), ())),
                                    preferred_element_type=F32)

        @pl.when(kk == nk - 1)
        def _():
            o_ref[...] = acc[...]

    in_specs = [pl.BlockSpec((tk, tm), lambda i, j, k: (k, i)), pl.BlockSpec((tk, tn), lambda i, j, k: (k, j))]
    operands = [a, b]
    aliases = {}
    if into is not None:
        in_specs.append(pl.BlockSpec(memory_space=pl.ANY))
        operands.append(into)
        aliases = {2: 0}
    return pl.pallas_call(
        body, name=name, grid=(m // tm, n // tn, nk), in_specs=in_specs,
        out_specs=pl.BlockSpec((None, None, tm, tn), o_idx), out_shape=_sds(shape), input_output_aliases=aliases,
        scratch_shapes=[pltpu.VMEM((tm, tn), F32)],
        compiler_params=_cparams(("parallel", "parallel", "arbitrary")),
    )(*operands)


S5_PACK = 8


def _s5_weight_grad(name, x, s, wide_rows, tk):
    t, c = x.shape
    wide = s.shape[1]
    kb, nb = S5_PACK * SSM_GROUP, S5_PACK * SSM_STATE
    nsb = c // kb
    x_spec = pl.BlockSpec((tk, kb), lambda i, j, k: (k, j % nsb))
    s_spec = pl.BlockSpec((tk, nb), lambda i, j, k: (k, j))
    if wide_rows:
        return _tiled_matmul(name, s, x, "tn", (1, wide // nb, t // tk), s_spec, x_spec,
                             pl.BlockSpec((nb, kb), lambda i, j, k: (j, 0)), (wide, kb))
    return _tiled_matmul(name, x, s, "tn", (1, wide // nb, t // tk), x_spec, s_spec,
                         pl.BlockSpec((kb, nb), lambda i, j, k: (0, j)), (kb, wide))


def _mlp_fwd(name, h, w1, w2, layer, ln_g, ln_b, tb, shards=()):
    t, c = h.shape
    nj, fc = w1.shape[0], w1.shape[3]
    nsh = len(shards)
    steps = (t // tb) * nj

    def body(h_ref, w1_ref, w2_ref, g_ref, b_ref, *rest):
        src, (out_ref, s_ref), dst = rest[:nsh], rest[nsh:nsh + 2], rest[nsh + 2:2 * nsh + 2]
        acc, sems = rest[2 * nsh + 2], rest[2 * nsh + 3:]
        j = pl.program_id(1)
        step = pl.program_id(0) * nj + j
        if nsh:
            start, forward, finish = _gather_big_phases(src, dst, sems)
            pl.when(step == 0)(start)

        @pl.when(j == 0)
        def _():
            acc[...] = jnp.zeros_like(acc)

        hid = jnp.dot(h_ref[...].astype(BF16), w1_ref[...].astype(BF16), preferred_element_type=F32)
        act = jnp.square(jnp.maximum(hid, 0.0))
        acc[...] += jnp.dot(act.astype(BF16), w2_ref[...].astype(BF16), preferred_element_type=F32)

        @pl.when(j == nj - 1)
        def _():
            s = DN_ALPHA * h_ref[...] + acc[...]
            s_ref[...] = s
            out_ref[...] = _ln(s, g_ref[...], b_ref[...])

        if nsh:
            pl.when(step == steps // 2)(forward)
            pl.when(step == steps - 1)(finish)

    row = pl.BlockSpec((tb, c), lambda i, j: (i, 0))
    vec = pl.BlockSpec((1, c), lambda i, j: (0, 0))
    outs = pl.pallas_call(
        body, name=name, grid=(t // tb, nj),
        in_specs=[row, pl.BlockSpec((None, None, c, fc), lambda i, j: (j, layer, 0, 0)),
                  pl.BlockSpec((None, None, fc, c), lambda i, j: (j, layer, 0, 0)), vec, vec] + [_HBM] * nsh,
        out_specs=[row, row] + [_HBM] * nsh,
        out_shape=[_sds((t, c)), _sds((t, c))] + [_sds((4,) + a.shape, a.dtype) for a in shards],
        scratch_shapes=[pltpu.VMEM((tb, c), F32)] + (_gather_big_sems(nsh) if nsh else []),
        compiler_params=_cparams(("arbitrary", "arbitrary")),
    )(h, w1, w2, ln_g, ln_b, *shards)
    return outs[0], outs[1], outs[2:]


def _mlp_bwd(name, h, s, dout, w1, w2, layer, ln_g, ln_b, tb):
    t, c = h.shape
    nj, fc = w1.shape[0], w1.shape[3]
    ff = nj * fc
    ni = t // tb
    nt = (((1,), (1,)), ((), ()))
    douts = dout if isinstance(dout, tuple) else (dout,)
    nd = len(douts)

    def body(h_ref, s_ref, *rest):
        dout_refs = rest[:nd]
        (w1_ref, w2_ref, g_ref, b_ref, dh_ref, ds_ref, dhid_ref, act_ref, dg_ref, db_ref,
         ds_scr, dh_acc, dg_acc, db_acc) = rest[nd:]
        i, j = pl.program_id(0), pl.program_id(1)

        @pl.when((i == 0) & (j == 0))
        def _():
            dg_acc[...] = jnp.zeros_like(dg_acc)
            db_acc[...] = jnp.zeros_like(db_acc)

        @pl.when(j == 0)
        def _():
            _, vjp = jax.vjp(_ln, s_ref[...], g_ref[...], b_ref[...])
            ds, dg, db = vjp(functools.reduce(jnp.add, [r[...] for r in dout_refs]))
            ds_scr[...] = ds
            ds_ref[...] = ds.astype(BF16)
            dh_acc[...] = DN_ALPHA * ds
            dg_acc[...] += dg
            db_acc[...] += db

        w1b, w2b = w1_ref[...].astype(BF16), w2_ref[...].astype(BF16)
        hid = jnp.dot(h_ref[...].astype(BF16), w1b, preferred_element_type=F32)
        rl = jnp.maximum(hid, 0.0)
        dact = lax.dot_general(ds_scr[...].astype(BF16), w2b, nt, preferred_element_type=F32)
        dhid = (dact * 2.0 * rl).astype(BF16)
        dh_acc[...] += lax.dot_general(dhid, w1b, nt, preferred_element_type=F32)
        dhid_ref[...] = dhid
        act_ref[...] = (rl * rl).astype(BF16)

        @pl.when(j == nj - 1)
        def _():
            dh_ref[...] = dh_acc[...]

        @pl.when((i == ni - 1) & (j == nj - 1))
        def _():
            dg_ref[...] = dg_acc[...]
            db_ref[...] = db_acc[...]

    row = pl.BlockSpec((tb, c), lambda i, j: (i, 0))
    vec = pl.BlockSpec((1, c), lambda i, j: (0, 0))
    wide = pl.BlockSpec((tb, fc), lambda i, j: (i, j))
    return pl.pallas_call(
        body, name=name, grid=(ni, nj),
        in_specs=[row, row] + [row] * nd + [pl.BlockSpec((None, None, c, fc), lambda i, j: (j, layer, 0, 0)),
                                            pl.BlockSpec((None, None, fc, c), lambda i, j: (j, layer, 0, 0)), vec, vec],
        out_specs=[row, row, wide, wide, vec, vec],
        out_shape=[_sds((t, c)), _sds((t, c), BF16), _sds((t, ff), BF16), _sds((t, ff), BF16), _sds((1, c)), _sds((1, c))],
        scratch_shapes=[pltpu.VMEM((tb, c), F32), pltpu.VMEM((tb, c), F32), pltpu.VMEM((1, c), F32), pltpu.VMEM((1, c), F32)],
        compiler_params=_cparams(("arbitrary", "arbitrary")),
    )(h, s, *douts, w1, w2, ln_g, ln_b)


def _load_heads(ref, hg):
    return jnp.stack([ref[:, h * HEAD:(h + 1) * HEAD] for h in range(hg)])


def _store_heads(ref, val):
    for h in range(val.shape[0]):
        ref[:, h * HEAD:(h + 1) * HEAD] = val[h]


def _rec_fwd(r, lw, k, v, a, b, hg, shards):
    t, c = r.shape
    n = HEAD
    nh = c // n
    ln = REC_CHUNK
    nck = t // ln
    ngrp = nh // hg
    nsh = len(shards)
    steps = ngrp * nck

    def body(r_ref, lw_ref, k_ref, v_ref, a_ref, b_ref, *rest):
        src, (o_ref, s0_ref), dst = rest[:nsh], rest[nsh:nsh + 2], rest[nsh + 2:2 * nsh + 2]
        state, sems = rest[2 * nsh + 2], rest[2 * nsh + 3:]
        step = pl.program_id(0) * nck + pl.program_id(1)
        start, forward, finish = _gather_big_phases(src, dst, sems)
        pl.when(step == 0)(start)

        @pl.when(pl.program_id(1) == 0)
        def _():
            state[...] = jnp.zeros_like(state)

        s0 = state[...]
        s0_ref[...] = s0
        o, s1 = _rec_chunk(s0, *(_load_heads(x, hg) for x in (r_ref, lw_ref, k_ref, v_ref, a_ref, b_ref)))
        _store_heads(o_ref, o)
        state[...] = s1
        pl.when(step == steps // 2)(forward)
        pl.when(step == steps - 1)(finish)

    seq = pl.BlockSpec((ln, hg * n), lambda g, i: (i, g))
    outs = pl.pallas_call(
        body, name="rec_fwd", grid=(ngrp, nck), in_specs=[seq] * 6 + [_HBM] * nsh,
        out_specs=[seq, pl.BlockSpec((None, hg, n, n), lambda g, i: (i, g, 0, 0))] + [_HBM] * nsh,
        out_shape=[_sds((t, c)), _sds((nck, nh, n, n))] + [_sds((4,) + s.shape, s.dtype) for s in shards],
        scratch_shapes=[pltpu.VMEM((hg, n, n), F32)] + _gather_big_sems(nsh),
        compiler_params=_cparams(("arbitrary", "arbitrary")),
    )(r, lw, k, v, a, b, *shards)
    return outs[0], outs[1], outs[2:]


def _rec_bwd(r, lw, k, v, a, b, s0s, do, hg, chip_sums):
    t, c = r.shape
    n = HEAD
    nh = c // n
    ln = REC_CHUNK
    nck = t // ln
    ngrp = nh // hg
    nsum = len(chip_sums)
    steps = ngrp * nck

    def body(r_ref, lw_ref, k_ref, v_ref, a_ref, b_ref, s0_ref, do_ref, *rest):
        src, grad_refs, land = rest[:nsum], rest[nsum:nsum + 6], rest[nsum + 6:2 * nsum + 6]
        dstate, sems = rest[2 * nsum + 6], rest[2 * nsum + 7:]
        step = pl.program_id(0) * nck + pl.program_id(1)
        start, finish = _scatter_big_phases(src, land, sems)
        pl.when(step == 0)(start)

        @pl.when(pl.program_id(1) == 0)
        def _():
            dstate[...] = jnp.zeros_like(dstate)

        _, vjp = jax.vjp(_rec_chunk, s0_ref[...], *(_load_heads(x, hg) for x in (r_ref, lw_ref, k_ref, v_ref, a_ref, b_ref)))
        ds0, *grads = vjp((_load_heads(do_ref, hg), dstate[...]))
        dstate[...] = ds0
        for ref, val in zip(grad_refs, grads):
            _store_heads(ref, val)
        pl.when(step == steps - 1)(finish)

    seq = pl.BlockSpec((ln, hg * n), lambda g, i: (nck - 1 - i, g))
    outs = pl.pallas_call(
        body, name="rec_bwd", grid=(ngrp, nck),
        in_specs=[seq] * 6 + [pl.BlockSpec((None, hg, n, n), lambda g, i: (nck - 1 - i, g, 0, 0)), seq] + [_HBM] * nsum,
        out_specs=[seq] * 6 + [_HBM] * nsum,
        out_shape=[_sds((t, c))] * 6 + [_sds((3,) + s.shape[1:], s.dtype) for s in chip_sums],
        scratch_shapes=[pltpu.VMEM((hg, n, n), F32)] + _scatter_big_sems(nsum),
        compiler_params=_cparams(("arbitrary", "arbitrary")),
    )(r, lw, k, v, a, b, s0s, do, *chip_sums)
    return outs[:6], outs[6:]


def _s5_blocks(c):
    kb, nb = S5_PACK * SSM_GROUP, S5_PACK * SSM_STATE
    return kb, nb, c // kb


def _s5_fwd(h, bc, abar, cc, tb, shards=()):
    t, c = h.shape
    w2 = bc.shape[1]
    w = w2 // 2
    kb, nb, nsb = _s5_blocks(c)

    nsh = len(shards)
    steps = t // tb

    def body(h_ref, bc_ref, a_ref, cc_ref, *rest):
        src, (s_ref, y_ref), dst = rest[:nsh], rest[nsh:nsh + 2], rest[nsh + 2:2 * nsh + 2]
        carry, rows, sems = rest[2 * nsh + 2], rest[2 * nsh + 3], rest[2 * nsh + 4:]
        if nsh:
            start, forward, finish = _gather_big_phases(src, dst, sems)
            pl.when(pl.program_id(0) == 0)(start)

        @pl.when(pl.program_id(0) == 0)
        def _():
            carry[...] = jnp.zeros_like(carry)

        for j in range(w2 // nb):
            ch = (j % nsb) * kb
            rows[:, j * nb:(j + 1) * nb] = jnp.dot(h_ref[:, ch:ch + kb].astype(BF16), bc_ref[:, j * nb:(j + 1) * nb],
                                                   preferred_element_type=F32)
        ar, ai = a_ref[:, :w], a_ref[:, w:]

        def step(i, state):
            hr, hi = state
            nr = ar * hr - ai * hi + rows[pl.ds(i, 1), :w]
            ni = ar * hi + ai * hr + rows[pl.ds(i, 1), w:]
            rows[pl.ds(i, 1), :w] = nr
            rows[pl.ds(i, 1), w:] = ni
            return nr, ni

        hr, hi = lax.fori_loop(0, tb, step, (carry[:, :w], carry[:, w:]))
        carry[:, :w] = hr
        carry[:, w:] = hi
        s_ref[...] = rows[...].astype(BF16)
        for j in range(nsb):
            re, im = j * nb, w + j * nb
            y_ref[:, j * kb:(j + 1) * kb] = (
                jnp.dot(s_ref[:, re:re + nb], cc_ref[re:re + nb, :], preferred_element_type=F32)
                + jnp.dot(s_ref[:, im:im + nb], cc_ref[im:im + nb, :], preferred_element_type=F32))
        if nsh:
            pl.when(pl.program_id(0) == steps // 2)(forward)
            pl.when(pl.program_id(0) == steps - 1)(finish)

    outs = pl.pallas_call(
        body, name="s5_fwd", grid=(steps,),
        in_specs=[pl.BlockSpec((tb, c), lambda i: (i, 0)), _full_spec(bc.shape), _full_spec(abar.shape), _full_spec(cc.shape)]
        + [_HBM] * nsh,
        out_specs=[pl.BlockSpec((tb, w2), lambda i: (i, 0)), pl.BlockSpec((tb, c), lambda i: (i, 0))] + [_HBM] * nsh,
        out_shape=[_sds((t, w2), BF16), _sds((t, c))] + [_sds((4,) + a.shape, a.dtype) for a in shards],
        scratch_shapes=[pltpu.VMEM((1, w2), F32), pltpu.VMEM((tb, w2), F32)] + (_gather_big_sems(nsh) if nsh else []),
        compiler_params=_cparams(("arbitrary",)),
    )(h, bc, abar, cc, *shards)
    return outs[0], outs[1], outs[2:]


def _s5_bwd(dy, s, abar, cc, bc, tb):
    t, c = dy.shape
    w2 = s.shape[1]
    w = w2 // 2
    kb, nb, nsb = _s5_blocks(c)
    nblk = t // tb
    pack = 16
    per = tb // pack
    nt = (((1,), (1,)), ((), ()))

    def body(dy_ref, s_ref, sprev_ref, a_ref, cc_ref, bc_ref, dbu_ref, dh_ref, da_ref, carry, da_acc, rows):
        i = pl.program_id(0)

        @pl.when(i == 0)
        def _():
            carry[...] = jnp.zeros_like(carry)
            da_acc[...] = jnp.zeros_like(da_acc)

        for j in range(w2 // nb):
            ch = (j % nsb) * kb
            rows[:, j * nb:(j + 1) * nb] = lax.dot_general(dy_ref[:, ch:ch + kb].astype(BF16), cc_ref[j * nb:(j + 1) * nb, :], nt,
                                                           preferred_element_type=F32)
        ar, ai = a_ref[:, :w], a_ref[:, w:]

        def step(n, state):
            gr, gi = state
            row = tb - 1 - n
            nr = rows[pl.ds(row, 1), :w] + ar * gr + ai * gi
            ni = rows[pl.ds(row, 1), w:] + ar * gi - ai * gr
            rows[pl.ds(row, 1), :w] = nr
            rows[pl.ds(row, 1), w:] = ni
            return nr, ni

        gr, gi = lax.fori_loop(0, tb, step, (carry[:, :w], carry[:, w:]))
        carry[:, :w] = gr
        carry[:, w:] = gi
        last = (lax.broadcasted_iota(jnp.int32, (pack, w2), 0) == pack - 1) & (i < nblk - 1)
        before = jnp.sum(jnp.where(last, sprev_ref[...].astype(F32), 0.0), axis=0, keepdims=True)
        rid = lax.broadcasted_iota(jnp.int32, (tb, w2), 0)
        sp = jnp.where(rid == 0, before, pltpu.roll(s_ref[...].astype(F32), 1, 0))
        g = rows[...]
        dbu_ref[...] = g.astype(BF16)
        spr, spi, g_r, g_i = sp[:, :w], sp[:, w:], g[:, :w], g[:, w:]
        da_acc[:, :w] += jnp.sum(spr * g_r + spi * g_i, axis=0, keepdims=True)
        da_acc[:, w:] += jnp.sum(spr * g_i - spi * g_r, axis=0, keepdims=True)
        for j in range(nsb):
            re, im = j * nb, w + j * nb
            dh_ref[:, j * kb:(j + 1) * kb] = (
                lax.dot_general(dbu_ref[:, re:re + nb], bc_ref[:, re:re + nb], nt, preferred_element_type=F32)
                + lax.dot_general(dbu_ref[:, im:im + nb], bc_ref[:, im:im + nb], nt, preferred_element_type=F32))

        @pl.when(i == nblk - 1)
        def _():
            da_ref[...] = da_acc[...]

    wide = pl.BlockSpec((tb, w2), lambda i: (nblk - 1 - i, 0))
    narrow = pl.BlockSpec((tb, c), lambda i: (nblk - 1 - i, 0))
    prev = pl.BlockSpec((pack, w2), lambda i: (jnp.maximum((nblk - 1 - i) * per - 1, 0), 0))
    return pl.pallas_call(
        body, name="s5_bwd", grid=(nblk,),
        in_specs=[narrow, wide, prev, _full_spec(abar.shape), _full_spec(cc.shape), _full_spec(bc.shape)],
        out_specs=[wide, narrow, pl.BlockSpec((1, w2), lambda i: (0, 0))],
        out_shape=[_sds((t, w2), BF16), _sds((t, c)), _sds((1, w2))],
        scratch_shapes=[pltpu.VMEM((1, w2), F32), pltpu.VMEM((1, w2), F32), pltpu.VMEM((tb, w2), F32)],
        compiler_params=_cparams(("arbitrary",)),
    )(dy, s, s, abar, cc, bc)


def _zoh_fwd(a_re, a_im, log_dt, b_re_t, b_im_t):
    def body(*refs):
        for r, val in zip(refs[5:], _f_zoh(*(x[...] for x in refs[:5]))):
            r[...] = val

    return pl.pallas_call(body, name="s5_zoh_fwd", out_shape=[_sds(a_re.shape)] * 2 + [_sds(b_re_t.shape)] * 2,
                          compiler_params=_cparams())(a_re, a_im, log_dt, b_re_t, b_im_t)


def _zoh_bwd(a_re, a_im, log_dt, b_re_t, b_im_t, couts):
    def body(*refs):
        _, vjp = jax.vjp(_f_zoh, *(x[...] for x in refs[:5]))
        for r, val in zip(refs[9:], vjp(tuple(x[...] for x in refs[5:9]))):
            r[...] = val

    ins = (a_re, a_im, log_dt, b_re_t, b_im_t)
    return pl.pallas_call(body, name="s5_zoh_bwd", out_shape=[_sds(x.shape) for x in ins],
                          compiler_params=_cparams())(*ins, *couts)


def _loss_head(h, target, tb):
    t, c = h.shape
    nb = t // tb

    def body(h_ref, t_ref, loss_ref, dh_ref, acc):
        i = pl.program_id(0)

        @pl.when(i == 0)
        def _():
            acc[...] = jnp.zeros_like(acc)

        d = h_ref[...] - t_ref[...]
        dh_ref[...] = d * (1.0 / c)
        acc[...] += 0.5 * jnp.sum(jnp.mean(d * d, axis=-1, keepdims=True), axis=0, keepdims=True)

        @pl.when(i == nb - 1)
        def _():
            loss_ref[...] = jnp.broadcast_to(acc[...], loss_ref.shape)

    row = pl.BlockSpec((tb, c), lambda i: (i, 0))
    return pl.pallas_call(
        body, name="loss_head", grid=(nb,), in_specs=[row, row],
        out_specs=[pl.BlockSpec((8, 128), lambda i: (0, 0)), row], out_shape=[_sds((8, 128)), _sds((t, c))],
        scratch_shapes=[pltpu.VMEM((1, 1), F32)], compiler_params=_cparams(("arbitrary",)),
    )(h, target)


def _rows_tile(rows):
    for cand in (512, 256, 128, 64, 32, 16, 8):
        if rows % cand == 0:
            return cand
    return rows


def _grad_x(here, from_next):
    rows, cols = here[0].shape
    tb = _rows_tile(rows)
    nb = rows // tb
    nh, nn = len(here), len(from_next)
    sub = 8

    def body(*refs):
        i = pl.program_id(0)
        total = functools.reduce(jnp.add, [r[...] for r in refs[:nh]])
        shifted = functools.reduce(jnp.add, [r[...] for r in refs[nh:nh + nn]])
        first_next = functools.reduce(jnp.add, [r[0:1, :] for r in refs[nh + nn:nh + 2 * nn]])
        first_next = jnp.where(i == nb - 1, 0.0, first_next)
        rid = lax.broadcasted_iota(jnp.int32, (tb, cols), 0)
        refs[-1][...] = total + jnp.where(rid == tb - 1, first_next, pltpu.roll(shifted, tb - 1, 0))

    blk = pl.BlockSpec((tb, cols), lambda i: (i, 0))
    nxt = pl.BlockSpec((sub, cols), lambda i: (jnp.minimum(i + 1, nb - 1) * (tb // sub), 0))
    return pl.pallas_call(body, name="grad_x", grid=(nb,), in_specs=[blk] * (nh + nn) + [nxt] * nn, out_specs=blk,
                          out_shape=_sds((rows, cols)), compiler_params=_cparams(("parallel",)))(*here, *from_next, *from_next)


def _adamw_math(w, g, m, v):
    m = ADAM_B1 * m + (1.0 - ADAM_B1) * g
    v = ADAM_B2 * v + (1.0 - ADAM_B2) * jnp.square(g)
    m_hat = m / (1.0 - ADAM_B1 ** ADAM_STEP)
    v_hat = v / (1.0 - ADAM_B2 ** ADAM_STEP)
    delta = -ADAM_LR * (m_hat / (jnp.sqrt(v_hat) + ADAM_EPS) + ADAM_WD * w)
    return delta, m, v


def _adamw(name, parts, w, m, v):
    rows, cols = w.shape
    tb = _rows_tile(rows)
    npart = len(parts)

    def body(*refs):
        g = refs[0][...]
        for r in refs[1:npart]:
            g = g + r[...]
        w_ref, m_ref, v_ref = refs[npart:npart + 3]
        g_out, d_out, m_out, v_out = refs[npart + 3:]
        delta, mn, vn = _adamw_math(w_ref[...], g, m_ref[...], v_ref[...])
        g_out[...] = g
        d_out[...] = delta
        m_out[...] = mn
        v_out[...] = vn

    blk = pl.BlockSpec((tb, cols), lambda i: (i, 0))
    return pl.pallas_call(body, name=name, grid=(rows // tb,), in_specs=[blk] * (npart + 3), out_specs=[blk] * 4,
                          out_shape=[_sds((rows, cols))] * 4, compiler_params=_cparams(("parallel",)))(*parts, w, m, v)


def _shift_down(a):
    return jnp.concatenate([jnp.zeros_like(a[:1]), a[:-1]], axis=0)


def _s5_pack_mask(g):
    return (jnp.arange(g)[None, :] % S5_PACK == jnp.arange(S5_PACK)[:, None]).astype(F32)


def _compact_b(bbar_t):
    g, s, p = bbar_t.shape
    return (_s5_pack_mask(g)[:, None, :, None] * bbar_t.transpose(1, 0, 2)[None]).reshape(S5_PACK * s, g * p)


def _compact_b_t(dense, g):
    s, p = dense.shape[0] // S5_PACK, dense.shape[1] // g
    return jnp.sum(dense.reshape(S5_PACK, s, g, p) * _s5_pack_mask(g)[:, None, :, None], axis=0).transpose(1, 0, 2)


def _compact_c(c_w):
    g, s, p = c_w.shape
    return (c_w.transpose(0, 2, 1)[:, :, None, :] * _s5_pack_mask(g).T[:, None, :, None]).reshape(g * p, S5_PACK * s)


def _compact_c_t(dense, g):
    p, s = dense.shape[0] // g, dense.shape[1] // S5_PACK
    return jnp.sum(dense.reshape(g, p, S5_PACK, s) * _s5_pack_mask(g).T[:, None, :, None], axis=2).transpose(0, 2, 1)


def _local_step(x, target, fw, core):
    t, c = x.shape
    nh = c // HEAD
    ng = c // SSM_GROUP
    tb = min(256, t)
    tbm = min(512, t)
    tbmf = min(1024, t)
    tbmb = min(512, t)
    tbs = min(256, t)
    tk5 = min(4096, t)
    hg = min(16, nh)
    mu = [fw['rw_mu'][i:i + 1] for i in range(6)]
    ln_g = [fw['ln_g'][i:i + 1] for i in range(4)]
    ln_b = [fw['ln_b'][i:i + 1] for i in range(4)]
    grads = {}

    xp = _shift_down(x)
    proj_params = {n: (mu[i], fw['rw_w' + n]) for n, i in (('r', 0), ('k', 2), ('v', 3))}
    raw = {n: _stage_fwd("proj_" + n, _f_proj, (x, xp), proj_params[n], (c,), tbm)[0] for n in 'rkv'}
    lora_params = (mu[1], mu[4], mu[5], fw['rw_w0'], fw['rw_w1'], fw['rw_w2'], fw['rw_a0'], fw['rw_a1'], fw['rw_a2'],
                   fw['rw_g1'], fw['rw_g2'], fw['rw_k_k'], fw['rw_k_a'])
    lw, k2, an, bb, gate = _stage_fwd("lora", _f_lora, (x, xp, raw['k']), lora_params, (c,) * 5, tbm)
    rec_in = (raw['r'], lw, k2, raw['v'], an, bb)
    o, s0s, (wo_view, w1_l0, w2_l0) = _rec_fwd(*rec_in, hg, fw['late_a'])
    fw = dict(fw, rw_wo=wo_view.reshape(c, c))
    mlp_w = [(w1_l0.reshape(4, 1, c, -1), w2_l0.reshape(4, 1, -1, c)), None]
    post_params = (fw['rw_lnx_g'], fw['rw_lnx_b'], fw['rw_r_k'], fw['rw_wo'], ln_g[0], ln_b[0])
    post_acts = (o, raw['r'], k2, raw['v'], gate, x)
    h1, = _stage_fwd("post", _f_post, post_acts, post_params, (c,), tb)
    h2, s_mlp0, (w1_l1,) = _mlp_fwd("mlp0_fwd", h1, *mlp_w[0], 0, ln_g[1], ln_b[1], tbmf, fw['late_b'])

    a_re, a_im, log_dt = fw['s5_a_re'], fw['s5_a_im'], fw['s5_log_dt']
    b_re_t, b_im_t = fw['s5_b_re'].transpose(0, 2, 1), fw['s5_b_im'].transpose(0, 2, 1)
    abar_re, abar_im, bbar_re_t, bbar_im_t = _zoh_fwd(a_re, a_im, log_dt, b_re_t, b_im_t)
    abar = jnp.concatenate([abar_re.reshape(1, -1), abar_im.reshape(1, -1)], axis=1)
    bc = jnp.concatenate([_compact_b(bbar_re_t), _compact_b(bbar_im_t)], axis=1).astype(BF16)
    cc = jnp.concatenate([_compact_c(fw['s5_c_re']), -_compact_c(fw['s5_c_im'])], axis=0).astype(BF16)
    st, ys, (glu_view, w2_l1) = _s5_fwd(h2, bc, abar, cc, tbs, fw['late_c'])
    mlp_w[1] = (w1_l1.reshape(4, 1, c, -1), w2_l1.reshape(4, 1, -1, c))
    fw = dict(fw, s5_w_glu=tuple(glu_view[q] for q in range(4)))
    glu_params = (fw['s5_d'], *fw['s5_w_glu'], ln_g[2], ln_b[2])
    h3, = _stage_fwd("glu", _f_glu, (ys, h2), glu_params, (c,), tbm)
    h4, s_mlp1, _ = _mlp_fwd("mlp1_fwd", h3, *mlp_w[1], 0, ln_g[3], ln_b[3], tbmf)

    loss_blk, dh4 = _loss_head(h4, target, tb)

    dln_g, dln_b = [None] * 4, [None] * 4
    dh3, ds1, dhid1, act1, dln_g[3], dln_b[3] = _mlp_bwd("mlp1_bwd", h3, s_mlp1, dh4, *mlp_w[1], 0,
                                                         ln_g[3], ln_b[3], tbmb)
    dw1 = _mlp_weight_grad("mlp1_dw1", h3, dhid1, 1, DEPTH, "n")
    dw2 = _mlp_weight_grad("mlp1_dw2", act1, ds1, 1, DEPTH, "m")
    (dys, dh2_glu), (grads['s5_d'], *dglu, dln_g[2], dln_b[2]) = _stage_bwd(
        "glu_bwd", _f_glu, (ys, h2), glu_params, (dh3,), tb, proxied=(1, 2, 3, 4))
    grads['s5_w_glu'] = jnp.stack(dglu)
    dcc = _s5_weight_grad("s5_dcc", dys, st, True, tk5)
    dbu, dh2_bu, dabar = _s5_bwd(dys, st, abar, cc, bc, tbs)
    dbc = _s5_weight_grad("s5_dbc", h2, dbu, False, tk5)
    gp = ng * SSM_STATE
    grads['s5_c_re'] = _compact_c_t(dcc[:gp], ng)
    grads['s5_c_im'] = -_compact_c_t(dcc[gp:], ng)
    zoh_couts = (dabar[:, :gp].reshape(ng, SSM_STATE), dabar[:, gp:].reshape(ng, SSM_STATE),
                 _compact_b_t(dbc[:, :gp], ng), _compact_b_t(dbc[:, gp:], ng))
    grads['s5_a_re'], grads['s5_a_im'], grads['s5_log_dt'], db_re_t, db_im_t = _zoh_bwd(
        a_re, a_im, log_dt, b_re_t, b_im_t, zoh_couts)
    grads['s5_b_re'], grads['s5_b_im'] = db_re_t.transpose(0, 2, 1), db_im_t.transpose(0, 2, 1)
    dh2 = (dh2_glu, dh2_bu)

    dh1, ds0, dhid0, act0, dln_g[1], dln_b[1] = _mlp_bwd("mlp0_bwd", h1, s_mlp0, dh2, *mlp_w[0], 0,
                                                         ln_g[1], ln_b[1], tbmb)
    grads['mlp_w1'] = _mlp_weight_grad("mlp0_dw1", h1, dhid0, 0, DEPTH, "n", into=dw1)
    grads['mlp_w2'] = _mlp_weight_grad("mlp0_dw2", act0, ds0, 0, DEPTH, "m", into=dw2)
    ready_views = [grads[n].reshape(4, -1, grads[n].shape[-1]) for n in BIG_READY]
    (do, dr_p, dk2_p, dv_p, dgate, dx_post), post_g, ready_others = _stage_bwd(
        "post_bwd", _f_post, post_acts, post_params, (dh1,), tb, proxied=(3,), halves_of=ready_views)
    grads['rw_lnx_g'], grads['rw_lnx_b'], grads['rw_r_k'], grads['rw_wo'], dln_g[0], dln_b[0] = post_g
    ready_sums = [_half_add(f"half_add_a{i}", v, o, core) for i, (v, o) in enumerate(zip(ready_views, ready_others))]
    rec_g, ready_lands = _rec_bwd(*rec_in, s0s, do, hg, ready_sums)
    reduced = dict(zip(BIG_READY, zip(ready_sums, ready_lands)))
    dr_r, dlw, dk2_r, dv_r, dan, dbb = rec_g
    dk2 = (dk2_p, dk2_r)
    (dx_l, dxp_l, dkraw_l), lora_g = _stage_bwd("lora_bwd", _f_lora, (x, xp, raw['k']), lora_params,
                                                (dlw, dk2, dan, dbb, dgate), tb)
    (dmu_w, dmu_a, dmu_g, grads['rw_w0'], grads['rw_w1'], grads['rw_w2'], grads['rw_a0'], grads['rw_a1'], grads['rw_a2'],
     grads['rw_g1'], grads['rw_g2'], grads['rw_k_k'], grads['rw_k_a']) = lora_g
    dproj = {'r': (dr_p, dr_r), 'k': dkraw_l, 'v': (dv_p, dv_r)}
    dxs, dxps, dmu = [dx_post, dx_l], [dxp_l], {}
    for n in 'rkv':
        (dx_n, dxp_n), (dmu[n], grads['rw_w' + n]) = _stage_bwd("proj_bwd_" + n, _f_proj, (x, xp), proj_params[n],
                                                                 (dproj[n],), tbm, proxied=(1,))
        dxs.append(dx_n)
        dxps.append(dxp_n)
    grads['rw_mu'] = jnp.concatenate([dmu['r'], dmu_w, dmu['k'], dmu['v'], dmu_a, dmu_g], axis=0)
    grads['ln_g'] = jnp.concatenate(dln_g, axis=0)
    grads['ln_b'] = jnp.concatenate(dln_b, axis=0)
    grad_x = _grad_x(dxs, dxps)
    late = [n for n in BIG if n not in BIG_READY]
    late_sums = dict(zip(late, _chip_sums("b", [grads[n] for n in late], core)))
    return loss_blk, grad_x, grads, reduced, late_sums


def _position():
    return lax.axis_index("x"), lax.axis_index("y"), lax.axis_index("c")


def _other_chips(x, y):
    return [(1 - x, y), (x, 1 - y), (1 - x, 1 - y)]


def _chip_slice(ref, axis, q, size):
    idx = [slice(None)] * len(ref.shape)
    idx[axis] = pl.ds(pl.multiple_of(q * size, size), size)
    return ref.at[tuple(idx)]


_HBM = pl.BlockSpec(memory_space=pltpu.HBM)


def _gather_small_phases(src, dst, axes, sems):
    n = len(src)
    send_sems, recv_sems, own_sems = sems
    x, y, c = _position()
    chips = _other_chips(x, y)
    sizes = [src[a].shape[axes[a]] for a in range(n)]

    def copy(a, k, q):
        return pltpu.make_async_remote_copy(
            src_ref=src[a], dst_ref=_chip_slice(dst[a], axes[a], q, sizes[a]), send_sem=send_sems.at[a, k],
            recv_sem=recv_sems.at[a, k], device_id=(*chips[k], c), device_id_type=MESH)

    def own(a):
        return pltpu.make_async_copy(src[a], _chip_slice(dst[a], axes[a], 2 * x + y, sizes[a]), own_sems.at[a])

    def start():
        for a in range(n):
            own(a).start()
            for k in range(3):
                copy(a, k, 2 * x + y).start()

    def finish():
        for a in range(n):
            for k, (cx, cy) in enumerate(chips):
                copy(a, k, 2 * cx + cy).wait_recv()
        for a in range(n):
            for k in range(3):
                copy(a, k, 2 * x + y).wait_send()
            own(a).wait()

    return start, finish


def _gather_early(big, small, axes):
    nb, ns = len(big), len(small)
    full_shapes = [tuple(s * 4 if i == ax else s for i, s in enumerate(a.shape)) for a, ax in zip(small, axes)]

    def body(*refs):
        src_b, src_s = refs[:nb], refs[nb:nb + ns]
        dst_b, dst_s = refs[nb + ns:2 * nb + ns], refs[2 * nb + ns:2 * (nb + ns)]
        sems = refs[2 * (nb + ns):]
        small_start, small_finish = _gather_small_phases(src_s, dst_s, axes, sems[5:])
        small_start()
        for phase in _gather_big_phases(src_b, dst_b, sems[:5]):
            phase()
        small_finish()

    outs = pl.pallas_call(
        body, name="gather_early", in_specs=[_HBM] * (nb + ns), out_specs=[_HBM] * (nb + ns),
        out_shape=[_sds((4,) + a.shape, a.dtype) for a in big] + [_sds(s, a.dtype) for s, a in zip(full_shapes, small)],
        scratch_shapes=_gather_big_sems(nb) + [pltpu.SemaphoreType.DMA((ns, 3)), pltpu.SemaphoreType.DMA((ns, 3)),
                                               pltpu.SemaphoreType.DMA((ns,))],
        compiler_params=_cparams(),
    )(*big, *small)
    return outs[:nb], outs[nb:]


def _scatter_pieces(fulls, axes, sums):
    n, nsum = len(fulls), len(sums)
    sizes = [a.shape[ax] // 4 for a, ax in zip(fulls, axes)]
    shard_shapes = [tuple(sz if i == ax else s for i, s in enumerate(a.shape)) for a, ax, sz in zip(fulls, axes, sizes)]

    def body(*refs):
        src, big_src = refs[:n], refs[n:n + nsum]
        land, big_land = refs[n + nsum:2 * n + nsum], refs[2 * n + nsum:2 * (n + nsum)]
        send_sems, recv_sems = refs[2 * (n + nsum):2 * (n + nsum) + 2]
        big_start, big_finish = _scatter_big_phases(big_src, big_land, refs[2 * (n + nsum) + 2:])
        big_start()
        x, y, c = _position()
        chips = _other_chips(x, y)

        def copy(a, k):
            cx, cy = chips[k]
            return pltpu.make_async_remote_copy(
                src_ref=_chip_slice(src[a], axes[a], 2 * cx + cy, sizes[a]), dst_ref=land[a].at[k],
                send_sem=send_sems.at[a, k], recv_sem=recv_sems.at[a, k], device_id=(cx, cy, c), device_id_type=MESH)

        for a in range(n):
            for k in range(3):
                copy(a, k).start()
        for a in range(n):
            for k in range(3):
                copy(a, k).wait_recv()
        for a in range(n):
            for k in range(3):
                copy(a, k).wait_send()
        big_finish()

    outs = pl.pallas_call(
        body, name="scatter_grads", in_specs=[_HBM] * (n + nsum), out_specs=[_HBM] * (n + nsum),
        out_shape=[_sds((3,) + s) for s in shard_shapes] + [_sds((3,) + s.shape[1:], s.dtype) for s in sums],
        scratch_shapes=[pltpu.SemaphoreType.DMA((n, 3)), pltpu.SemaphoreType.DMA((n, 3))] + _scatter_big_sems(nsum),
        compiler_params=_cparams(),
    )(*fulls, *sums)
    return outs[:n], outs[n:]


def _sibling_swap(name, arrs):
    n = len(arrs)

    def body(*refs):
        src, dst = refs[:n], refs[n:2 * n]
        send_sems, recv_sems = refs[2 * n:]
        x, y, c = _position()
        copies = [pltpu.make_async_remote_copy(src_ref=src[a], dst_ref=dst[a], send_sem=send_sems.at[a], recv_sem=recv_sems.at[a],
                                               device_id=(x, y, 1 - c), device_id_type=MESH) for a in range(n)]
        for cp in copies:
            cp.start()
        for cp in copies:
            cp.wait_recv()
        for cp in copies:
            cp.wait_send()

    return pl.pallas_call(
        body, name=name, in_specs=[_HBM] * n, out_specs=[_HBM] * n, out_shape=[_sds(a.shape) for a in arrs],
        scratch_shapes=[pltpu.SemaphoreType.DMA((n,)), pltpu.SemaphoreType.DMA((n,))],
        compiler_params=_cparams(),
    )(*arrs)


def _sum4(name, own, land):
    rows, cols = own.shape
    tb = _rows_tile(rows)

    def body(o_ref, l0, l1, l2, out_ref):
        out_ref[...] = ((o_ref[...] + l0[...]) + l1[...]) + l2[...]

    blk = pl.BlockSpec((tb, cols), lambda i: (i, 0))
    lands = [pl.BlockSpec((None, tb, cols), functools.partial(lambda k, i: (k, i, 0), k)) for k in range(3)]
    return pl.pallas_call(body, name=name, grid=(rows // tb,), in_specs=[blk] + lands, out_specs=blk,
                          out_shape=_sds((rows, cols)), compiler_params=_cparams(("parallel",)))(own, land, land, land)


def _allreduce_adamw_small(g, w, m, v):
    rows, lanes = g.shape

    def body(g_ref, w_ref, m_ref, v_ref, gs_ref, d_ref, mn_ref, vn_ref, land, send_sems, recv_sems):
        x, y, c = _position()
        me = 4 * x + 2 * y + c
        chips = _other_chips(x, y)
        sibling = (x, y, 1 - c)

        def slot_of(cx, cy, cc):
            return 4 * cx + 2 * cy + cc

        def copy(j, slot, to, src=None):
            return pltpu.make_async_remote_copy(src_ref=g_ref if src is None else land.at[src], dst_ref=land.at[slot],
                                                send_sem=send_sems.at[j], recv_sem=recv_sems.at[j], device_id=to, device_id_type=MESH)

        copy(0, me, sibling).start()
        for k, chip in enumerate(chips):
            copy(1 + k, me, (*chip, c)).start()
        land[me] = g_ref[...]
        for k, chip in enumerate(chips):
            theirs = slot_of(*chip, c)
            copy(1 + k, theirs, (*chip, c)).wait_recv()
            copy(4 + k, theirs, sibling, src=theirs).start()
        copy(0, slot_of(*sibling), sibling).wait_recv()
        for k, chip in enumerate(chips):
            copy(4 + k, slot_of(*chip, 1 - c), sibling).wait_recv()
        copy(0, me, sibling).wait_send()
        for k, chip in enumerate(chips):
            copy(1 + k, me, (*chip, c)).wait_send()
            copy(4 + k, slot_of(*chip, c), sibling, src=slot_of(*chip, c)).wait_send()
        total = land[0]
        for dev in range(1, 8):
            total = total + land[dev]
        delta, mn, vn = _adamw_math(w_ref[...], total, m_ref[...], v_ref[...])
        gs_ref[...] = total
        d_ref[...] = delta
        mn_ref[...] = mn
        vn_ref[...] = vn

    vmem = pl.BlockSpec(memory_space=pltpu.VMEM)
    return pl.pallas_call(
        body, name="allreduce_adamw_small", in_specs=[vmem] * 4, out_specs=[vmem] * 4, out_shape=[_sds((rows, lanes))] * 4,
        scratch_shapes=[pltpu.VMEM((8, rows, lanes), F32), pltpu.SemaphoreType.DMA((7,)), pltpu.SemaphoreType.DMA((7,))],
        compiler_params=_cparams(),
    )(g, w, m, v)


def _row_half(ref, c):
    r2 = ref.shape[-2] // 2
    lead = (slice(None),) * (len(ref.shape) - 2)
    return ref.at[(*lead, pl.ds(pl.multiple_of(c * r2, r2), r2), slice(None))]


def _gather_big_phases(src, dst, sems):
    n = len(src)
    ici_send, ici_recv, d2d_send, d2d_recv, own_sems = sems
    x, y, c = _position()
    me = 2 * x + y
    chips = _other_chips(x, y)
    ids = [2 * cx + cy for cx, cy in chips]

    def ici(a, k, q):
        return pltpu.make_async_remote_copy(
            src_ref=_row_half(src[a], c), dst_ref=_row_half(dst[a].at[q], c), send_sem=ici_send.at[a, k],
            recv_sem=ici_recv.at[a, k], device_id=(*chips[k], c), device_id_type=MESH)

    def d2d(a, k, half):
        where = _row_half(dst[a].at[ids[k]], half)
        return pltpu.make_async_remote_copy(src_ref=where, dst_ref=where, send_sem=d2d_send.at[a, k], recv_sem=d2d_recv.at[a, k],
                                            device_id=(x, y, 1 - c), device_id_type=MESH)

    def own(a):
        return pltpu.make_async_copy(src[a], dst[a].at[me], own_sems.at[a])

    def start():
        for a in range(n):
            own(a).start()
            for k in range(3):
                ici(a, k, me).start()

    def forward():
        for a in range(n):
            for k in range(3):
                ici(a, k, ids[k]).wait_recv()
                d2d(a, k, c).start()

    def finish():
        for a in range(n):
            for k in range(3):
                d2d(a, k, 1 - c).wait_recv()
        for a in range(n):
            for k in range(3):
                ici(a, k, me).wait_send()
                d2d(a, k, c).wait_send()
            own(a).wait()

    return start, forward, finish


def _gather_big_sems(n):
    return [pltpu.SemaphoreType.DMA((n, 3))] * 4 + [pltpu.SemaphoreType.DMA((n,))]


def _chip_sums(tag, grads, core):
    views = [g.reshape(4, -1, g.shape[-1]) for g in grads]
    others = _sibling_halves("sibling_halves_" + tag, views)
    return [_half_add(f"half_add_{tag}{i}", v, o, core) for i, (v, o) in enumerate(zip(views, others))]


def _sibling_halves_phases(src, dst, sems):
    n = len(src)
    send_sems, recv_sems = sems
    x, y, c = _position()

    def copy(a):
        return pltpu.make_async_remote_copy(src_ref=_row_half(src[a], 1 - c), dst_ref=dst[a], send_sem=send_sems.at[a],
                                            recv_sem=recv_sems.at[a], device_id=(x, y, 1 - c), device_id_type=MESH)

    def start():
        for a in range(n):
            copy(a).start()

    def finish():
        for a in range(n):
            copy(a).wait_recv()
        for a in range(n):
            copy(a).wait_send()

    return start, finish


def _sibling_halves_sems(n):
    return [pltpu.SemaphoreType.DMA((n,)), pltpu.SemaphoreType.DMA((n,))]


def _sibling_halves_shapes(views):
    return [_sds((4, v.shape[1] // 2, v.shape[2])) for v in views]


def _sibling_halves(name, views):
    n = len(views)

    def body(*refs):
        for phase in _sibling_halves_phases(refs[:n], refs[n:2 * n], refs[2 * n:]):
            phase()

    return pl.pallas_call(
        body, name=name, in_specs=[_HBM] * n, out_specs=[_HBM] * n, out_shape=_sibling_halves_shapes(views),
        scratch_shapes=_sibling_halves_sems(n), compiler_params=_cparams(),
    )(*views)


def _rows_tile_capped(rows, cap=256):
    return min(_rows_tile(rows), cap)


def _half_add(name, view, other, core):
    _, r, k = view.shape
    r2 = r // 2
    tr = _rows_tile_capped(r2, 512)
    per = r2 // tr

    def body(c_ref, v_ref, o_ref, out_ref):
        out_ref[...] = (v_ref[...] + o_ref[...]).astype(BF16)

    blk = pl.BlockSpec((None, tr, k), lambda q, i, c: (q, i, 0))
    return pl.pallas_call(
        body, name=name,
        grid_spec=pltpu.PrefetchScalarGridSpec(
            num_scalar_prefetch=1, grid=(4, per),
            in_specs=[pl.BlockSpec((None, tr, k), lambda q, i, c: (q, c[0] * per + i, 0)), blk], out_specs=blk),
        out_shape=_sds((4, r2, k), BF16), compiler_params=_cparams(("parallel", "parallel")),
    )(core, view, other)


def _scatter_big_phases(src, land, sems):
    n = len(src)
    send_sems, recv_sems = sems
    x, y, c = _position()
    chips = _other_chips(x, y)

    def copy(a, k):
        cx, cy = chips[k]
        return pltpu.make_async_remote_copy(src_ref=src[a].at[2 * cx + cy], dst_ref=land[a].at[k], send_sem=send_sems.at[a, k],
                                            recv_sem=recv_sems.at[a, k], device_id=(cx, cy, c), device_id_type=MESH)

    def start():
        for a in range(n):
            for k in range(3):
                copy(a, k).start()

    def finish():
        for a in range(n):
            for k in range(3):
                copy(a, k).wait_recv()
        for a in range(n):
            for k in range(3):
                copy(a, k).wait_send()

    return start, finish


def _scatter_big_sems(n):
    return [pltpu.SemaphoreType.DMA((n, 3)), pltpu.SemaphoreType.DMA((n, 3))]


def _sum4_big(name, sums, land, chip):
    _, r2, k = sums.shape
    tr = _rows_tile_capped(r2, 512)

    def body(q_ref, s_ref, l0, l1, l2, out_ref):
        out_ref[...] = ((s_ref[...].astype(F32) + l0[...].astype(F32)) + l1[...].astype(F32)) + l2[...].astype(F32)

    lands = [pl.BlockSpec((None, tr, k), functools.partial(lambda j, i, q: (j, i, 0), j)) for j in range(3)]
    return pl.pallas_call(
        body, name=name,
        grid_spec=pltpu.PrefetchScalarGridSpec(
            num_scalar_prefetch=1, grid=(r2 // tr,),
            in_specs=[pl.BlockSpec((None, tr, k), lambda i, q: (q[0], i, 0))] + lands,
            out_specs=pl.BlockSpec((tr, k), lambda i, q: (i, 0))),
        out_shape=_sds((r2, k)), compiler_params=_cparams(("parallel",)),
    )(chip, sums, land, land, land)


def _adamw_halves(name, mine, theirs, w, m, v, core):
    r, k = w.shape
    r2 = r // 2
    tr = _rows_tile_capped(r2, 512)
    per = r2 // tr

    def body(c_ref, mine_ref, theirs_ref, w_ref, m_ref, v_ref, g_out, d_out, m_out, v_out):
        g = jnp.where(pl.program_id(0) == c_ref[0], mine_ref[...], theirs_ref[...])
        delta, mn, vn = _adamw_math(w_ref[...], g, m_ref[...], v_ref[...])
        g_out[...] = g
        d_out[...] = delta
        m_out[...] = mn
        v_out[...] = vn

    half = pl.BlockSpec((tr, k), lambda h, i, c: (i, 0))
    full = pl.BlockSpec((tr, k), lambda h, i, c: (h * per + i, 0))
    return pl.pallas_call(
        body, name=name,
        grid_spec=pltpu.PrefetchScalarGridSpec(num_scalar_prefetch=1, grid=(2, per), in_specs=[half, half, full, full, full],
                                               out_specs=[full] * 4),
        out_shape=[_sds((r, k))] * 4, compiler_params=_cparams(("parallel", "parallel")),
    )(core, mine, theirs, w, m, v)


def _drops_layer_axis(name):
    return not (name.startswith('mlp') or name == 's5_d')


def _work(name, arr):
    return arr.reshape(arr.shape[1:]) if _drops_layer_axis(name) else arr


def _work_axis(name):
    return SHARD_AXIS[name] - (1 if _drops_layer_axis(name) else 0)


def _as2d(a):
    return a.reshape(-1, a.shape[-1])


def _replicated_2d(name, arr):
    if name in ('ln_g', 'ln_b'):
        return arr
    if name == 'rw_r_k':
        return arr.reshape(1, -1)
    if name == 's5_log_dt':
        return arr.reshape(-1, 1)
    if name.startswith('s5_'):
        return arr.reshape(arr.shape[1:])
    return arr


def _pack(arrs):
    flat = []
    for a in arrs:
        f = a.reshape(-1)
        flat.append(jnp.pad(f, (0, -f.shape[0] % 128)))
    f = jnp.concatenate(flat)
    f = jnp.pad(f, (0, -f.shape[0] % 1024))
    return f.reshape(-1, 128)


def _unpack(packed, shapes):
    flat = packed.reshape(-1)
    out, at = [], 0
    for s in shapes:
        size = math.prod(s)
        out.append(flat[at:at + size].reshape(s))
        at += size + (-size % 128)
    return out


def kernel(x, ln_g, ln_b, rw_mu, rw_w0, rw_w1, rw_w2, rw_a0, rw_a1, rw_a2, rw_g1, rw_g2, rw_k_k, rw_k_a, rw_r_k, rw_wr, rw_wk, rw_wv, rw_wo, rw_lnx_g, rw_lnx_b, s5_a_re, s5_a_im, s5_log_dt, s5_b_re, s5_b_im, s5_c_re, s5_c_im, s5_d, s5_w_glu, mlp_w1, mlp_w2, loss_target, m_ln_g, m_ln_b, m_rw_mu, m_rw_w0, m_rw_w1, m_rw_w2, m_rw_a0, m_rw_a1, m_rw_a2, m_rw_g1, m_rw_g2, m_rw_k_k, m_rw_k_a, m_rw_r_k, m_rw_wr, m_rw_wk, m_rw_wv, m_rw_wo, m_rw_lnx_g, m_rw_lnx_b, m_s5_a_re, m_s5_a_im, m_s5_log_dt, m_s5_b_re, m_s5_b_im, m_s5_c_re, m_s5_c_im, m_s5_d, m_s5_w_glu, m_mlp_w1, m_mlp_w2, v_ln_g, v_ln_b, v_rw_mu, v_rw_w0, v_rw_w1, v_rw_w2, v_rw_a0, v_rw_a1, v_rw_a2, v_rw_g1, v_rw_g2, v_rw_k_k, v_rw_k_a, v_rw_r_k, v_rw_wr, v_rw_wk, v_rw_wv, v_rw_wo, v_rw_lnx_g, v_rw_lnx_b, v_s5_a_re, v_s5_a_im, v_s5_log_dt, v_s5_b_re, v_s5_b_im, v_s5_c_re, v_s5_c_im, v_s5_d, v_s5_w_glu, v_mlp_w1, v_mlp_w2):
    d = dict(locals())
    x_pos, y_pos, c_pos = _position()
    chip = 2 * x_pos + y_pos
    chip_arr = jnp.reshape(chip, (1,)).astype(jnp.int32)
    core_arr = jnp.reshape(c_pos, (1,)).astype(jnp.int32)

    small = [n for n in SHARD_AXIS if n not in BIG]
    axes = [_work_axis(n) for n in small]
    big_views, small_fulls = _gather_early([_as2d(d[n]).astype(BF16) for n in BIG_EARLY], [_work(n, d[n]) for n in small], axes)
    views = dict(zip(BIG_EARLY, big_views))
    fw = dict(zip(small, small_fulls))
    c_model = d['x'].shape[-1]
    for n in BIG_EARLY:
        fw[n] = views[n].reshape(c_model, c_model)
    w1_layers, w2_layers = d['mlp_w1'].astype(BF16), d['mlp_w2'].astype(BF16)
    fw['late_a'] = [_as2d(d['rw_wo']).astype(BF16), w1_layers[0], w2_layers[0]]
    fw['late_b'] = [w1_layers[1]]
    fw['late_c'] = [_as2d(d['s5_w_glu']).astype(BF16), w2_layers[1]]
    for n in REPLICATED:
        fw[n] = _replicated_2d(n, d[n])

    loss_blk, grad_x, grads, reduced, late_sums = _local_step(d['x'][0], d['loss_target'][0], fw, core_arr)
    loss = lax.psum(loss_blk[0, 0], ('x', 'y', 'c'))
    out = {}

    pieces = [grads[n] for n in small]
    lands, late_lands = _scatter_pieces(pieces, axes, list(late_sums.values()))
    reduced.update(zip(late_sums, zip(late_sums.values(), late_lands)))
    mine = [_sum4_big("sum4_" + n, *reduced[n], chip_arr) for n in BIG]
    for n, g, ax, land in zip(small, pieces, axes, lands):
        size = g.shape[ax] // 4
        mine.append(_sum4("sum4_" + n, lax.dynamic_slice_in_dim(g, chip * size, size, ax), land))
    theirs = _sibling_swap("swap_sums", mine)
    for n, a, b in zip(BIG + small, mine, theirs):
        w2d, m2d, v2d = _as2d(d[n]), _as2d(d['m_' + n]), _as2d(d['v_' + n])
        res = (_adamw_halves("adamw_" + n, a, b, w2d, m2d, v2d, core_arr) if n in BIG
               else _adamw("adamw_" + n, (a, b), w2d, m2d, v2d))
        out[n] = [r.reshape(d[n].shape) for r in res]

    rep_shapes = [d[n].shape for n in REPLICATED]
    packs = [_pack([grads[n] for n in REPLICATED])] + [_pack([d[p + n] for n in REPLICATED]) for p in ('', 'm_', 'v_')]
    res = [_unpack(p, rep_shapes) for p in _allreduce_adamw_small(*packs)]
    for i, n in enumerate(REPLICATED):
        out[n] = [r[i] for r in res]

    grad_x = grad_x.reshape(d['x'].shape)
    return (loss, grad_x, *[out[n][0] for n in WEIGHTS], *[out[n][1] for n in WEIGHTS],
            *[out[n][2] for n in WEIGHTS], *[out[n][3] for n in WEIGHTS])
```

```python
import functools
import math

import jax
import jax.numpy as jnp
from jax import lax
from jax.experimental import pallas as pl
from jax.experimental.pallas import tpu as pltpu

F32 = jnp.float32
BF16 = jnp.bfloat16
MESH = pl.DeviceIdType.MESH

HEAD = 64
SSM_GROUP = 16
SSM_STATE = 64
GN_EPS = 64e-5
LN_EPS = 1e-5
DEPTH = 2
DN_ALPHA = (2.0 * DEPTH) ** 0.25
ADAM_LR, ADAM_B1, ADAM_B2, ADAM_EPS, ADAM_WD, ADAM_STEP = 0.001, 0.9, 0.999, 1e-08, 0.01, 10
REC_CHUNK = 64
V7X_VMEM_BYTES = 64 * 2 ** 20
VMEM_LIMIT = V7X_VMEM_BYTES - 8 * 2 ** 20

WEIGHTS = ['ln_g', 'ln_b', 'rw_mu', 'rw_w0', 'rw_w1', 'rw_w2', 'rw_a0', 'rw_a1', 'rw_a2', 'rw_g1', 'rw_g2',
           'rw_k_k', 'rw_k_a', 'rw_r_k', 'rw_wr', 'rw_wk', 'rw_wv', 'rw_wo', 'rw_lnx_g', 'rw_lnx_b',
           's5_a_re', 's5_a_im', 's5_log_dt', 's5_b_re', 's5_b_im', 's5_c_re', 's5_c_im', 's5_d', 's5_w_glu',
           'mlp_w1', 'mlp_w2']
SHARD_AXIS = {'rw_mu': 2, 'rw_w1': 1, 'rw_w2': 2, 'rw_a1': 1, 'rw_a2': 2, 'rw_g1': 1, 'rw_g2': 2,
              'rw_wr': 1, 'rw_wk': 1, 'rw_wv': 1, 'rw_wo': 1, 's5_d': 1, 's5_w_glu': 2, 'mlp_w1': 2, 'mlp_w2': 1}
REPLICATED = [n for n in WEIGHTS if n not in SHARD_AXIS]
BIG_EARLY = ['rw_wr', 'rw_wk', 'rw_wv']
BIG_LATE = ['rw_wo', 's5_w_glu', 'mlp_w1', 'mlp_w2']
BIG = BIG_EARLY + BIG_LATE
BIG_READY = ['s5_w_glu', 'mlp_w1', 'mlp_w2']


def _sds(shape, dtype=F32):
    return jax.ShapeDtypeStruct(tuple(shape), dtype)


def _cparams(sem=None, **kw):
    if sem is not None:
        kw["dimension_semantics"] = sem
    return pltpu.CompilerParams(vmem_limit_bytes=VMEM_LIMIT, **kw)


def _mm_products(a, b, g):
    gb = g.astype(BF16)
    da = lax.dot_general(gb, b.astype(BF16), (((1,), (1,)), ((), ())), preferred_element_type=F32)
    db = lax.dot_general(a.astype(BF16), gb, (((0,), (0,)), ((), ())), preferred_element_type=F32)
    return da, db


@jax.custom_vjp
def _mm_plain(a, b):
    return jnp.dot(a.astype(BF16), b.astype(BF16), preferred_element_type=F32)


def _mm_plain_bwd(res, g):
    da, db = _mm_products(*res, g)
    return da.astype(res[0].dtype), db.astype(res[1].dtype)


_mm_plain.defvjp(lambda a, b: (_mm_plain(a, b), (a, b)), _mm_plain_bwd)


@jax.custom_vjp
def _mm_proxy(a, b, z):
    return jnp.dot(a.astype(BF16), b.astype(BF16), preferred_element_type=F32)


def _mm_proxy_bwd(res, g):
    da, db = _mm_products(*res, g)
    return da.astype(res[0].dtype), jnp.zeros_like(res[1]), db


_mm_proxy.defvjp(lambda a, b, z: (_mm_proxy(a, b, z), (a, b)), _mm_proxy_bwd)


def mm(a, b, z=None):
    return _mm_plain(a, b) if z is None else _mm_proxy(a, b, z)


def _split3(x):
    hi = x.astype(BF16)
    r1 = x - hi.astype(F32)
    mid = r1.astype(BF16)
    lo = (r1 - mid.astype(F32)).astype(BF16)
    return hi, mid, lo


def _head_sum_impl(x):
    c = x.shape[1]
    lanes = 128
    sel = (lax.broadcasted_iota(jnp.int32, (c, lanes), 0) // HEAD
           == lax.broadcasted_iota(jnp.int32, (c, lanes), 1)).astype(BF16)
    s = sum(jnp.dot(p, sel, preferred_element_type=F32) for p in _split3(x))
    return sum(lax.dot_general(p, sel, (((1,), (1,)), ((), ())), preferred_element_type=F32) for p in _split3(s))


@jax.custom_vjp
def head_sum(x):
    return _head_sum_impl(x)


head_sum.defvjp(lambda x: (_head_sum_impl(x), None), lambda _, g: (_head_sum_impl(g),))


def _ln(x, g, b):
    mu = jnp.mean(x, axis=-1, keepdims=True)
    xc = x - mu
    var = jnp.mean(xc * xc, axis=-1, keepdims=True)
    return xc * lax.rsqrt(var + LN_EPS) * g + b


def _f_proj(acts, params, proxies):
    x, xp = acts
    mu, w = params
    return (mm(x + (xp - x) * mu, w, proxies[1]),)


def _f_lora(acts, params, proxies):
    x, xp, kraw = acts
    mu_w, mu_a, mu_g, w0, w1, w2, a0, a1, a2, g1, g2, k_k, k_a = params
    xx = xp - x
    w_pre = w0 + mm(jnp.tanh(mm(x + xx * mu_w, w1)), w2)
    z = -w_pre
    softplus = jnp.maximum(z, 0.0) + jnp.log(1.0 + jnp.exp(-jnp.abs(z)))
    log_decay = -jnp.exp(-softplus - 0.5)
    a = jax.nn.sigmoid(a0 + mm(mm(x + xx * mu_a, a1), a2))
    g = mm(jax.nn.sigmoid(mm(x + xx * mu_g, g1)), g2)
    kk = kraw * k_k
    kkn = kk / jnp.maximum(jnp.sqrt(head_sum(kk * kk)), 1e-12)
    k2 = kraw * (1.0 + (a - 1.0) * k_a)
    return log_decay, k2, -kkn, kkn * a, g


def _f_post(acts, params, proxies):
    o, r, k2, v, g, x = acts
    lnx_g, lnx_b, r_k, wo, ln_g, ln_b = params
    om = head_sum(o) * (1.0 / HEAD)
    oc = o - om
    ov = head_sum(oc * oc) * (1.0 / HEAD)
    on = oc * lax.rsqrt(ov + GN_EPS) * lnx_g + lnx_b
    bonus = head_sum(r * k2 * r_k) * v
    y = mm((on + bonus) * g, wo, proxies[3])
    return (_ln(DN_ALPHA * x + y, ln_g, ln_b),)


def _f_glu(acts, params, proxies):
    ys, h = acts
    d, wv0, wv1, wg0, wg1, ln_g, ln_b = params
    y = jax.nn.gelu(ys + h * d)
    mix = jnp.concatenate([mm(y, wv0, proxies[1]) * jax.nn.sigmoid(mm(y, wg0, proxies[3])),
                           mm(y, wv1, proxies[2]) * jax.nn.sigmoid(mm(y, wg1, proxies[4]))], axis=1)
    return (_ln(DN_ALPHA * h + mix, ln_g, ln_b),)


def _f_zoh(a_re, a_im, log_dt, b_re_t, b_im_t):
    dt = jnp.exp(log_dt)
    lam_re = jnp.minimum(a_re, -1e-4)
    lam_im = a_im
    mag = jnp.exp(dt * lam_re)
    abar_re = mag * jnp.cos(dt * lam_im)
    abar_im = mag * jnp.sin(dt * lam_im)
    den = lam_re * lam_re + lam_im * lam_im
    nr, ni = abar_re - 1.0, abar_im
    coef_re = ((nr * lam_re + ni * lam_im) / den)[:, None, :]
    coef_im = ((ni * lam_re - nr * lam_im) / den)[:, None, :]
    return (abar_re, abar_im, coef_re * b_re_t - coef_im * b_im_t, coef_re * b_im_t + coef_im * b_re_t)


def _bdot16_raw(a, b, ca, cb):
    return lax.dot_general(a.astype(BF16), b.astype(BF16), (((ca,), (cb,)), ((0,), (0,))), preferred_element_type=F32)


@functools.partial(jax.custom_vjp, nondiff_argnums=(2, 3))
def _bdot16(a, b, ca, cb):
    return _bdot16_raw(a, b, ca, cb)


def _bdot16_bwd(ca, cb, res, g):
    a, b = res
    if (ca, cb) == (2, 1):
        return _bdot16_raw(g, b, 2, 2), _bdot16_raw(a, g, 1, 1)
    if (ca, cb) == (2, 2):
        return _bdot16_raw(g, b, 2, 1), _bdot16_raw(g, a, 1, 1)
    assert (ca, cb) == (1, 1)
    return _bdot16_raw(b, g, 2, 2), _bdot16_raw(a, g, 2, 1)


_bdot16.defvjp(lambda a, b, ca, cb: (_bdot16_raw(a, b, ca, cb), (a, b)), _bdot16_bwd)

def _time_sums(x, suffix):
    hg, ln, _ = x.shape
    row = lax.broadcasted_iota(jnp.int32, (hg, ln, ln), 1)
    col = lax.broadcasted_iota(jnp.int32, (hg, ln, ln), 2)
    tri = ((row <= col) if suffix else (row >= col)).astype(BF16)
    return sum(lax.dot_general(tri, p, (((2,), (1,)), ((0,), (0,))), preferred_element_type=F32) for p in _split3(x))


@jax.custom_vjp
def _time_cumsum(x):
    return _time_sums(x, False)


_time_cumsum.defvjp(lambda x: (_time_sums(x, False), None), lambda _, g: (_time_sums(g, True),))

_dot_score = _bdot16
_dot_inverse = _bdot16
_dot_value = _bdot16


def _rec_chunk(s0, r, lw, k, v, a, b):
    hg, ln, _ = r.shape
    row = lax.broadcasted_iota(jnp.int32, (hg, ln, ln), 1)
    col = lax.broadcasted_iota(jnp.int32, (hg, ln, ln), 2)
    incl, strict = row >= col, row > col
    cum = _time_cumsum(lw)
    total = jnp.sum(lw, axis=1, keepdims=True)
    e_cum, e_inv, e_prev, e_tail = jnp.exp(cum), jnp.exp(-cum), jnp.exp(cum - lw), jnp.exp(total - cum)
    rt, at, bt, kt = r * e_cum, a * e_prev, b * e_inv, k * e_inv
    ar = jnp.concatenate([at, rt], axis=1)
    on_b, on_k = _dot_score(ar, bt, 2, 2), _dot_score(ar, kt, 2, 2)
    aab, arb = jnp.where(strict, on_b[:, :ln], 0.0), jnp.where(incl, on_b[:, ln:], 0.0)
    aak, ark = jnp.where(strict, on_k[:, :ln], 0.0), jnp.where(incl, on_k[:, ln:], 0.0)
    p = (row == col).astype(F32) + aab
    m = aab
    for _ in range(int(math.log2(ln)) - 1):
        m = _dot_inverse(m, m, 2, 1)
        p = p + _dot_inverse(p, m, 2, 1)
    from_state = _dot_value(ar, s0, 2, 2)
    from_v = _dot_value(jnp.concatenate([aak, ark], axis=1), v, 2, 1)
    u = _dot_inverse(p, from_state[:, :ln] + from_v[:, :ln], 2, 1)
    o = from_state[:, ln:] + from_v[:, ln:] + _dot_value(arb, u, 2, 1)
    s1 = s0 * jnp.exp(total) + _dot_value(jnp.concatenate([u, v], axis=1),
                                          jnp.concatenate([b * e_tail, k * e_tail], axis=1), 1, 1)
    return o, s1


def _full_spec(shape):
    nd = len(shape)
    return pl.BlockSpec(tuple(shape), lambda *_: (0,) * nd)


def _stage_fwd(name, f, acts, params, out_dims, tb):
    t = acts[0].shape[0]
    na, npar = len(acts), len(params)

    def body(*refs):
        outs = f(tuple(r[...] for r in refs[:na]), tuple(r[...] for r in refs[na:na + npar]), (None,) * npar)
        for r, val in zip(refs[na + npar:], outs):
            r[...] = val

    return pl.pallas_call(
        body, name=name, grid=(t // tb,),
        in_specs=[pl.BlockSpec((tb, a.shape[1]), lambda i: (i, 0)) for a in acts] + [_full_spec(p.shape) for p in params],
        out_specs=[pl.BlockSpec((tb, d), lambda i: (i, 0)) for d in out_dims],
        out_shape=[_sds((t, d)) for d in out_dims],
        compiler_params=_cparams(("arbitrary",)),
    )(*acts, *params)


def _stage_bwd(name, f, acts, params, couts, tb, proxied=(), halves_of=()):
    nh = len(halves_of)
    t = acts[0].shape[0]
    groups = [c if isinstance(c, tuple) else (c,) for c in couts]
    couts = [term for grp in groups for term in grp]
    na, npar, nc = len(acts), len(params), len(couts)
    steps = t // tb

    def f_diff(act_vals, diff_vals, param_vals):
        real = tuple(param_vals[i] if i in proxied else diff_vals[i] for i in range(npar))
        proxies = tuple(diff_vals[i] if i in proxied else None for i in range(npar))
        return f(act_vals, real, proxies)

    def body(*refs):
        a_refs, p_hbm, c_refs = refs[:na], refs[na:na + npar], refs[na + npar:na + npar + nc]
        o = na + npar + nc
        half_src, o = refs[o:o + nh], o + nh
        da_refs, dp_hbm, half_dst = refs[o:o + na], refs[o + na:o + na + npar], refs[o + na + npar:o + na + npar + nh]
        o = o + na + npar + nh
        p_buf, acc, half_sems = refs[o:o + npar], refs[o + npar:o + 2 * npar], refs[o + 2 * npar:]
        i = pl.program_id(0)
        if nh:
            half_start, half_finish = _sibling_halves_phases(half_src, half_dst, half_sems)
            pl.when(i == 0)(half_start)

        @pl.when(i == 0)
        def _():
            for src, dst in zip(p_hbm, p_buf):
                pltpu.sync_copy(src, dst)
            for r in acc:
                r[...] = jnp.zeros_like(r)

        param_vals = tuple(r[...] for r in p_buf)
        diff_vals = tuple(jnp.zeros(v.shape, F32) if i in proxied else v for i, v in enumerate(param_vals))
        _, vjp = jax.vjp(functools.partial(f_diff, param_vals=param_vals), tuple(r[...] for r in a_refs), diff_vals)
        terms = iter(c_refs)
        d_acts, d_params = vjp(tuple(functools.reduce(jnp.add, [next(terms)[...] for _ in grp]) for grp in groups))
        for r, val in zip(da_refs, d_acts):
            r[...] = val
        for r, val in zip(acc, d_params):
            r[...] += val

        @pl.when(i == steps - 1)
        def _():
            for src, dst in zip(acc, dp_hbm):
                pltpu.sync_copy(src, dst)

        if nh:
            pl.when(i == steps - 1)(half_finish)

    hbm = pl.BlockSpec(memory_space=pltpu.HBM)
    outs = pl.pallas_call(
        body, name=name, grid=(steps,),
        in_specs=[pl.BlockSpec((tb, a.shape[1]), lambda i: (i, 0)) for a in acts] + [hbm] * npar
        + [pl.BlockSpec((tb, c.shape[1]), lambda i: (i, 0)) for c in couts] + [hbm] * nh,
        out_specs=[pl.BlockSpec((tb, a.shape[1]), lambda i: (i, 0)) for a in acts] + [hbm] * (npar + nh),
        out_shape=[_sds(a.shape) for a in acts] + [_sds(p.shape) for p in params] + _sibling_halves_shapes(halves_of),
        scratch_shapes=[pltpu.VMEM(p.shape, p.dtype) for p in params] + [pltpu.VMEM(p.shape, F32) for p in params]
        + (_sibling_halves_sems(nh) if nh else []),
        compiler_params=_cparams(("arbitrary",)),
    )(*acts, *params, *couts, *halves_of)
    if nh:
        return outs[:na], outs[na:na + npar], outs[na + npar:]
    return outs[:na], outs[na:]


def _tiled_matmul(name, a, b, mode, grid, a_spec, b_spec, o_spec, out_shape):
    nk = grid[2]
    dims = {"nn": ((1,), (0,)), "nt": ((1,), (1,)), "tn": ((0,), (0,))}[mode]

    def body(a_ref, b_ref, o_ref, acc):
        kk = pl.program_id(2)

        @pl.when(kk == 0)
        def _():
            acc[...] = jnp.zeros_like(acc)

        acc[...] += lax.dot_general(a_ref[...].astype(BF16), b_ref[...].astype(BF16), (dims, ((), ())),
                                    preferred_element_type=F32)

        @pl.when(kk == nk - 1)
        def _():
            o_ref[...] = acc[...]

    return pl.pallas_call(
        body, name=name, grid=grid, in_specs=[a_spec, b_spec], out_specs=o_spec, out_shape=_sds(out_shape),
        scratch_shapes=[pltpu.VMEM(o_spec.block_shape, F32)],
        compiler_params=_cparams(("parallel", "parallel", "arbitrary")),
    )(a, b)


def _mlp_weight_grad(name, a, b, layer, layers, split, into=None, tile=512):
    t, m = a.shape
    n = b.shape[1]
    tile = 2 * tile
    tk = min(2 * tile, t)
    if split == "n":
        tm, tn = min(tile, m), min(tile, n // 4)
        per = n // 4 // tn
        shape = (4, layers, m, n // 4)
        o_idx = lambda i, j, k: (j // per, layer, i, j % per)
    else:
        tm, tn = min(tile, m // 4), min(tile, n)
        per = m // 4 // tm
        shape = (4, layers, m // 4, n)
        o_idx = lambda i, j, k: (i // per, layer, i % per, j)
    nk = t // tk

    def body(a_ref, b_ref, *rest):
        o_ref, acc = rest[-2:]
        kk = pl.program_id(2)

        @pl.when(kk == 0)
        def _():
            acc[...] = jnp.zeros_like(acc)

        acc[...] += lax.dot_general(a_ref[...].astype(BF16), b_ref[...].astype(BF16), (((0,), (0,)), ((), ())),
                                    preferred_element_type=F32)

        @pl.when(kk == nk - 1)
        def _():
            o_ref[...] = acc[...]

    in_specs = [pl.BlockSpec((tk, tm), lambda i, j, k: (k, i)), pl.BlockSpec((tk, tn), lambda i, j, k: (k, j))]
    operands = [a, b]
    aliases = {}
    if into is not None:
        in_specs.append(pl.BlockSpec(memory_space=pl.ANY))
        operands.append(into)
        aliases = {2: 0}
    return pl.pallas_call(
        body, name=name, grid=(m // tm, n // tn, nk), in_specs=in_specs,
        out_specs=pl.BlockSpec((None, None, tm, tn), o_idx), out_shape=_sds(shape), input_output_aliases=aliases,
        scratch_shapes=[pltpu.VMEM((tm, tn), F32)],
        compiler_params=_cparams(("parallel", "parallel", "arbitrary")),
    )(*operands)


S5_PACK = 8


def _s5_weight_grad(name, x, s, wide_rows, tk):
    t, c = x.shape
    wide = s.shape[1]
    kb, nb = S5_PACK * SSM_GROUP, S5_PACK * SSM_STATE
    nsb = c // kb
    x_spec = pl.BlockSpec((tk, kb), lambda i, j, k: (k, j % nsb))
    s_spec = pl.BlockSpec((tk, nb), lambda i, j, k: (k, j))
    if wide_rows:
        return _tiled_matmul(name, s, x, "tn", (1, wide // nb, t // tk), s_spec, x_spec,
                             pl.BlockSpec((nb, kb), lambda i, j, k: (j, 0)), (wide, kb))
    return _tiled_matmul(name, x, s, "tn", (1, wide // nb, t // tk), x_spec, s_spec,
                         pl.BlockSpec((kb, nb), lambda i, j, k: (0, j)), (kb, wide))


def _mlp_fwd(name, h, w1, w2, layer, ln_g, ln_b, tb, shards=()):
    t, c = h.shape
    nj, fc = w1.shape[0], w1.shape[3]
    nsh = len(shards)
    steps = (t // tb) * nj

    def body(h_ref, w1_ref, w2_ref, g_ref, b_ref, *rest):
        src, (out_ref, s_ref), dst = rest[:nsh], rest[nsh:nsh + 2], rest[nsh + 2:2 * nsh + 2]
        acc, sems = rest[2 * nsh + 2], rest[2 * nsh + 3:]
        j = pl.program_id(1)
        step = pl.program_id(0) * nj + j
        if nsh:
            start, forward, finish = _gather_big_phases(src, dst, sems)
            pl.when(step == 0)(start)

        @pl.when(j == 0)
        def _():
            acc[...] = jnp.zeros_like(acc)

        hid = jnp.dot(h_ref[...].astype(BF16), w1_ref[...].astype(BF16), preferred_element_type=F32)
        act = jnp.square(jnp.maximum(hid, 0.0))
        acc[...] += jnp.dot(act.astype(BF16), w2_ref[...].astype(BF16), preferred_element_type=F32)

        @pl.when(j == nj - 1)
        def _():
            s = DN_ALPHA * h_ref[...] + acc[...]
            s_ref[...] = s
            out_ref[...] = _ln(s, g_ref[...], b_ref[...])

        if nsh:
            pl.when(step == steps // 2)(forward)
            pl.when(step == steps - 1)(finish)

    row = pl.BlockSpec((tb, c), lambda i, j: (i, 0))
    vec = pl.BlockSpec((1, c), lambda i, j: (0, 0))
    outs = pl.pallas_call(
        body, name=name, grid=(t // tb, nj),
        in_specs=[row, pl.BlockSpec((None, None, c, fc), lambda i, j: (j, layer, 0, 0)),
                  pl.BlockSpec((None, None, fc, c), lambda i, j: (j, layer, 0, 0)), vec, vec] + [_HBM] * nsh,
        out_specs=[row, row] + [_HBM] * nsh,
        out_shape=[_sds((t, c)), _sds((t, c))] + [_sds((4,) + a.shape, a.dtype) for a in shards],
        scratch_shapes=[pltpu.VMEM((tb, c), F32)] + (_gather_big_sems(nsh) if nsh else []),
        compiler_params=_cparams(("arbitrary", "arbitrary")),
    )(h, w1, w2, ln_g, ln_b, *shards)
    return outs[0], outs[1], outs[2:]


def _mlp_bwd(name, h, s, dout, w1, w2, layer, ln_g, ln_b, tb):
    t, c = h.shape
    nj, fc = w1.shape[0], w1.shape[3]
    ff = nj * fc
    ni = t // tb
    nt = (((1,), (1,)), ((), ()))
    douts = dout if isinstance(dout, tuple) else (dout,)
    nd = len(douts)

    def body(h_ref, s_ref, *rest):
        dout_refs = rest[:nd]
        (w1_ref, w2_ref, g_ref, b_ref, dh_ref, ds_ref, dhid_ref, act_ref, dg_ref, db_ref,
         ds_scr, dh_acc, dg_acc, db_acc) = rest[nd:]
        i, j = pl.program_id(0), pl.program_id(1)

        @pl.when((i == 0) & (j == 0))
        def _():
            dg_acc[...] = jnp.zeros_like(dg_acc)
            db_acc[...] = jnp.zeros_like(db_acc)

        @pl.when(j == 0)
        def _():
            _, vjp = jax.vjp(_ln, s_ref[...], g_ref[...], b_ref[...])
            ds, dg, db = vjp(functools.reduce(jnp.add, [r[...] for r in dout_refs]))
            ds_scr[...] = ds
            ds_ref[...] = ds.astype(BF16)
            dh_acc[...] = DN_ALPHA * ds
            dg_acc[...] += dg
            db_acc[...] += db

        w1b, w2b = w1_ref[...].astype(BF16), w2_ref[...].astype(BF16)
        hid = jnp.dot(h_ref[...].astype(BF16), w1b, preferred_element_type=F32)
        rl = jnp.maximum(hid, 0.0)
        dact = lax.dot_general(ds_scr[...].astype(BF16), w2b, nt, preferred_element_type=F32)
        dhid = (dact * 2.0 * rl).astype(BF16)
        dh_acc[...] += lax.dot_general(dhid, w1b, nt, preferred_element_type=F32)
        dhid_ref[...] = dhid
        act_ref[...] = (rl * rl).astype(BF16)

        @pl.when(j == nj - 1)
        def _():
            dh_ref[...] = dh_acc[...]

        @pl.when((i == ni - 1) & (j == nj - 1))
        def _():
            dg_ref[...] = dg_acc[...]
            db_ref[...] = db_acc[...]

    row = pl.BlockSpec((tb, c), lambda i, j: (i, 0))
    vec = pl.BlockSpec((1, c), lambda i, j: (0, 0))
    wide = pl.BlockSpec((tb, fc), lambda i, j: (i, j))
    return pl.pallas_call(
        body, name=name, grid=(ni, nj),
        in_specs=[row, row] + [row] * nd + [pl.BlockSpec((None, None, c, fc), lambda i, j: (j, layer, 0, 0)),
                                            pl.BlockSpec((None, None, fc, c), lambda i, j: (j, layer, 0, 0)), vec, vec],
        out_specs=[row, row, wide, wide, vec, vec],
        out_shape=[_sds((t, c)), _sds((t, c), BF16), _sds((t, ff), BF16), _sds((t, ff), BF16), _sds((1, c)), _sds((1, c))],
        scratch_shapes=[pltpu.VMEM((tb, c), F32), pltpu.VMEM((tb, c), F32), pltpu.VMEM((1, c), F32), pltpu.VMEM((1, c), F32)],
        compiler_params=_cparams(("arbitrary", "arbitrary")),
    )(h, s, *douts, w1, w2, ln_g, ln_b)


def _load_heads(ref, hg):
    return jnp.stack([ref[:, h * HEAD:(h + 1) * HEAD] for h in range(hg)])


def _store_heads(ref, val):
    for h in range(val.shape[0]):
        ref[:, h * HEAD:(h + 1) * HEAD] = val[h]


def _rec_fwd(r, lw, k, v, a, b, hg, shards):
    t, c = r.shape
    n = HEAD
    nh = c // n
    ln = REC_CHUNK
    nck = t // ln
    ngrp = nh // hg
    nsh = len(shards)
    steps = ngrp * nck

    def body(r_ref, lw_ref, k_ref, v_ref, a_ref, b_ref, *rest):
        src, (o_ref, s0_ref), dst = rest[:nsh], rest[nsh:nsh + 2], rest[nsh + 2:2 * nsh + 2]
        state, sems = rest[2 * nsh + 2], rest[2 * nsh + 3:]
        step = pl.program_id(0) * nck + pl.program_id(1)
        start, forward, finish = _gather_big_phases(src, dst, sems)
        pl.when(step == 0)(start)

        @pl.when(pl.program_id(1) == 0)
        def _():
            state[...] = jnp.zeros_like(state)

        s0 = state[...]
        s0_ref[...] = s0
        o, s1 = _rec_chunk(s0, *(_load_heads(x, hg) for x in (r_ref, lw_ref, k_ref, v_ref, a_ref, b_ref)))
        _store_heads(o_ref, o)
        state[...] = s1
        pl.when(step == steps // 2)(forward)
        pl.when(step == steps - 1)(finish)

    seq = pl.BlockSpec((ln, hg * n), lambda g, i: (i, g))
    outs = pl.pallas_call(
        body, name="rec_fwd", grid=(ngrp, nck), in_specs=[seq] * 6 + [_HBM] * nsh,
        out_specs=[seq, pl.BlockSpec((None, hg, n, n), lambda g, i: (i, g, 0, 0))] + [_HBM] * nsh,
        out_shape=[_sds((t, c)), _sds((nck, nh, n, n))] + [_sds((4,) + s.shape, s.dtype) for s in shards],
        scratch_shapes=[pltpu.VMEM((hg, n, n), F32)] + _gather_big_sems(nsh),
        compiler_params=_cparams(("arbitrary", "arbitrary")),
    )(r, lw, k, v, a, b, *shards)
    return outs[0], outs[1], outs[2:]


def _rec_bwd(r, lw, k, v, a, b, s0s, do, hg, chip_sums):
    t, c = r.shape
    n = HEAD
    nh = c // n
    ln = REC_CHUNK
    nck = t // ln
    ngrp = nh // hg
    nsum = len(chip_sums)
    steps = ngrp * nck

    def body(r_ref, lw_ref, k_ref, v_ref, a_ref, b_ref, s0_ref, do_ref, *rest):
        src, grad_refs, land = rest[:nsum], rest[nsum:nsum + 6], rest[nsum + 6:2 * nsum + 6]
        dstate, sems = rest[2 * nsum + 6], rest[2 * nsum + 7:]
        step = pl.program_id(0) * nck + pl.program_id(1)
        start, finish = _scatter_big_phases(src, land, sems)
        pl.when(step == 0)(start)

        @pl.when(pl.program_id(1) == 0)
        def _():
            dstate[...] = jnp.zeros_like(dstate)

        _, vjp = jax.vjp(_rec_chunk, s0_ref[...], *(_load_heads(x, hg) for x in (r_ref, lw_ref, k_ref, v_ref, a_ref, b_ref)))
        ds0, *grads = vjp((_load_heads(do_ref, hg), dstate[...]))
        dstate[...] = ds0
        for ref, val in zip(grad_refs, grads):
            _store_heads(ref, val)
        pl.when(step == steps - 1)(finish)

    seq = pl.BlockSpec((ln, hg * n), lambda g, i: (nck - 1 - i, g))
    outs = pl.pallas_call(
        body, name="rec_bwd", grid=(ngrp, nck),
        in_specs=[seq] * 6 + [pl.BlockSpec((None, hg, n, n), lambda g, i: (nck - 1 - i, g, 0, 0)), seq] + [_HBM] * nsum,
        out_specs=[seq] * 6 + [_HBM] * nsum,
        out_shape=[_sds((t, c))] * 6 + [_sds((3,) + s.shape[1:], s.dtype) for s in chip_sums],
        scratch_shapes=[pltpu.VMEM((hg, n, n), F32)] + _scatter_big_sems(nsum),
        compiler_params=_cparams(("arbitrary", "arbitrary")),
    )(r, lw, k, v, a, b, s0s, do, *chip_sums)
    return outs[:6], outs[6:]


def _s5_blocks(c):
    kb, nb = S5_PACK * SSM_GROUP, S5_PACK * SSM_STATE
    return kb, nb, c // kb


def _s5_fwd(h, bc, abar, cc, tb, shards=()):
    t, c = h.shape
    w2 = bc.shape[1]
    w = w2 // 2
    kb, nb, nsb = _s5_blocks(c)

    nsh = len(shards)
    steps = t // tb

    def body(h_ref, bc_ref, a_ref, cc_ref, *rest):
        src, (s_ref, y_ref), dst = rest[:nsh], rest[nsh:nsh + 2], rest[nsh + 2:2 * nsh + 2]
        carry, rows, sems = rest[2 * nsh + 2], rest[2 * nsh + 3], rest[2 * nsh + 4:]
        if nsh:
            start, forward, finish = _gather_big_phases(src, dst, sems)
            pl.when(pl.program_id(0) == 0)(start)

        @pl.when(pl.program_id(0) == 0)
        def _():
            carry[...] = jnp.zeros_like(carry)

        for j in range(w2 // nb):
            ch = (j % nsb) * kb
            rows[:, j * nb:(j + 1) * nb] = jnp.dot(h_ref[:, ch:ch + kb].astype(BF16), bc_ref[:, j * nb:(j + 1) * nb],
                                                   preferred_element_type=F32)
        ar, ai = a_ref[:, :w], a_ref[:, w:]

        def step(i, state):
            hr, hi = state
            nr = ar * hr - ai * hi + rows[pl.ds(i, 1), :w]
            ni = ar * hi + ai * hr + rows[pl.ds(i, 1), w:]
            rows[pl.ds(i, 1), :w] = nr
            rows[pl.ds(i, 1), w:] = ni
            return nr, ni

        hr, hi = lax.fori_loop(0, tb, step, (carry[:, :w], carry[:, w:]))
        carry[:, :w] = hr
        carry[:, w:] = hi
        s_ref[...] = rows[...].astype(BF16)
        for j in range(nsb):
            re, im = j * nb, w + j * nb
            y_ref[:, j * kb:(j + 1) * kb] = (
                jnp.dot(s_ref[:, re:re + nb], cc_ref[re:re + nb, :], preferred_element_type=F32)
                + jnp.dot(s_ref[:, im:im + nb], cc_ref[im:im + nb, :], preferred_element_type=F32))
        if nsh:
            pl.when(pl.program_id(0) == steps // 2)(forward)
            pl.when(pl.program_id(0) == steps - 1)(finish)

    outs = pl.pallas_call(
        body, name="s5_fwd", grid=(steps,),
        in_specs=[pl.BlockSpec((tb, c), lambda i: (i, 0)), _full_spec(bc.shape), _full_spec(abar.shape), _full_spec(cc.shape)]
        + [_HBM] * nsh,
        out_specs=[pl.BlockSpec((tb, w2), lambda i: (i, 0)), pl.BlockSpec((tb, c), lambda i: (i, 0))] + [_HBM] * nsh,
        out_shape=[_sds((t, w2), BF16), _sds((t, c))] + [_sds((4,) + a.shape, a.dtype) for a in shards],
        scratch_shapes=[pltpu.VMEM((1, w2), F32), pltpu.VMEM((tb, w2), F32)] + (_gather_big_sems(nsh) if nsh else []),
        compiler_params=_cparams(("arbitrary",)),
    )(h, bc, abar, cc, *shards)
    return outs[0], outs[1], outs[2:]


def _s5_bwd(dy, s, abar, cc, bc, tb):
    t, c = dy.shape
    w2 = s.shape[1]
    w = w2 // 2
    kb, nb, nsb = _s5_blocks(c)
    nblk = t // tb
    pack = 16
    per = tb // pack
    nt = (((1,), (1,)), ((), ()))

    def body(dy_ref, s_ref, sprev_ref, a_ref, cc_ref, bc_ref, dbu_ref, dh_ref, da_ref, carry, da_acc, rows):
        i = pl.program_id(0)

        @pl.when(i == 0)
        def _():
            carry[...] = jnp.zeros_like(carry)
            da_acc[...] = jnp.zeros_like(da_acc)

        for j in range(w2 // nb):
            ch = (j % nsb) * kb
            rows[:, j * nb:(j + 1) * nb] = lax.dot_general(dy_ref[:, ch:ch + kb].astype(BF16), cc_ref[j * nb:(j + 1) * nb, :], nt,
                                                           preferred_element_type=F32)
        ar, ai = a_ref[:, :w], a_ref[:, w:]

        def step(n, state):
            gr, gi = state
            row = tb - 1 - n
            nr = rows[pl.ds(row, 1), :w] + ar * gr + ai * gi
            ni = rows[pl.ds(row, 1), w:] + ar * gi - ai * gr
            rows[pl.ds(row, 1), :w] = nr
            rows[pl.ds(row, 1), w:] = ni
            return nr, ni

        gr, gi = lax.fori_loop(0, tb, step, (carry[:, :w], carry[:, w:]))
        carry[:, :w] = gr
        carry[:, w:] = gi
        last = (lax.broadcasted_iota(jnp.int32, (pack, w2), 0) == pack - 1) & (i < nblk - 1)
        before = jnp.sum(jnp.where(last, sprev_ref[...].astype(F32), 0.0), axis=0, keepdims=True)
        rid = lax.broadcasted_iota(jnp.int32, (tb, w2), 0)
        sp = jnp.where(rid == 0, before, pltpu.roll(s_ref[...].astype(F32), 1, 0))
        g = rows[...]
        dbu_ref[...] = g.astype(BF16)
        spr, spi, g_r, g_i = sp[:, :w], sp[:, w:], g[:, :w], g[:, w:]
        da_acc[:, :w] += jnp.sum(spr * g_r + spi * g_i, axis=0, keepdims=True)
        da_acc[:, w:] += jnp.sum(spr * g_i - spi * g_r, axis=0, keepdims=True)
        for j in range(nsb):
            re, im = j * nb, w + j * nb
            dh_ref[:, j * kb:(j + 1) * kb] = (
                lax.dot_general(dbu_ref[:, re:re + nb], bc_ref[:, re:re + nb], nt, preferred_element_type=F32)
                + lax.dot_general(dbu_ref[:, im:im + nb], bc_ref[:, im:im + nb], nt, preferred_element_type=F32))

        @pl.when(i == nblk - 1)
        def _():
            da_ref[...] = da_acc[...]

    wide = pl.BlockSpec((tb, w2), lambda i: (nblk - 1 - i, 0))
    narrow = pl.BlockSpec((tb, c), lambda i: (nblk - 1 - i, 0))
    prev = pl.BlockSpec((pack, w2), lambda i: (jnp.maximum((nblk - 1 - i) * per - 1, 0), 0))
    return pl.pallas_call(
        body, name="s5_bwd", grid=(nblk,),
        in_specs=[narrow, wide, prev, _full_spec(abar.shape), _full_spec(cc.shape), _full_spec(bc.shape)],
        out_specs=[wide, narrow, pl.BlockSpec((1, w2), lambda i: (0, 0))],
        out_shape=[_sds((t, w2), BF16), _sds((t, c)), _sds((1, w2))],
        scratch_shapes=[pltpu.VMEM((1, w2), F32), pltpu.VMEM((1, w2), F32), pltpu.VMEM((tb, w2), F32)],
        compiler_params=_cparams(("arbitrary",)),
    )(dy, s, s, abar, cc, bc)


def _zoh_fwd(a_re, a_im, log_dt, b_re_t, b_im_t):
    def body(*refs):
        for r, val in zip(refs[5:], _f_zoh(*(x[...] for x in refs[:5]))):
            r[...] = val

    return pl.pallas_call(body, name="s5_zoh_fwd", out_shape=[_sds(a_re.shape)] * 2 + [_sds(b_re_t.shape)] * 2,
                          compiler_params=_cparams())(a_re, a_im, log_dt, b_re_t, b_im_t)


def _zoh_bwd(a_re, a_im, log_dt, b_re_t, b_im_t, couts):
    def body(*refs):
        _, vjp = jax.vjp(_f_zoh, *(x[...] for x in refs[:5]))
        for r, val in zip(refs[9:], vjp(tuple(x[...] for x in refs[5:9]))):
            r[...] = val

    ins = (a_re, a_im, log_dt, b_re_t, b_im_t)
    return pl.pallas_call(body, name="s5_zoh_bwd", out_shape=[_sds(x.shape) for x in ins],
                          compiler_params=_cparams())(*ins, *couts)


def _loss_head(h, target, tb):
    t, c = h.shape
    nb = t // tb

    def body(h_ref, t_ref, loss_ref, dh_ref, acc):
        i = pl.program_id(0)

        @pl.when(i == 0)
        def _():
            acc[...] = jnp.zeros_like(acc)

        d = h_ref[...] - t_ref[...]
        dh_ref[...] = d * (1.0 / c)
        acc[...] += 0.5 * jnp.sum(jnp.mean(d * d, axis=-1, keepdims=True), axis=0, keepdims=True)

        @pl.when(i == nb - 1)
        def _():
            loss_ref[...] = jnp.broadcast_to(acc[...], loss_ref.shape)

    row = pl.BlockSpec((tb, c), lambda i: (i, 0))
    return pl.pallas_call(
        body, name="loss_head", grid=(nb,), in_specs=[row, row],
        out_specs=[pl.BlockSpec((8, 128), lambda i: (0, 0)), row], out_shape=[_sds((8, 128)), _sds((t, c))],
        scratch_shapes=[pltpu.VMEM((1, 1), F32)], compiler_params=_cparams(("arbitrary",)),
    )(h, target)


def _rows_tile(rows):
    for cand in (512, 256, 128, 64, 32, 16, 8):
        if rows % cand == 0:
            return cand
    return rows


def _grad_x(here, from_next):
    rows, cols = here[0].shape
    tb = _rows_tile(rows)
    nb = rows // tb
    nh, nn = len(here), len(from_next)
    sub = 8

    def body(*refs):
        i = pl.program_id(0)
        total = functools.reduce(jnp.add, [r[...] for r in refs[:nh]])
        shifted = functools.reduce(jnp.add, [r[...] for r in refs[nh:nh + nn]])
        first_next = functools.reduce(jnp.add, [r[0:1, :] for r in refs[nh + nn:nh + 2 * nn]])
        first_next = jnp.where(i == nb - 1, 0.0, first_next)
        rid = lax.broadcasted_iota(jnp.int32, (tb, cols), 0)
        refs[-1][...] = total + jnp.where(rid == tb - 1, first_next, pltpu.roll(shifted, tb - 1, 0))

    blk = pl.BlockSpec((tb, cols), lambda i: (i, 0))
    nxt = pl.BlockSpec((sub, cols), lambda i: (jnp.minimum(i + 1, nb - 1) * (tb // sub), 0))
    return pl.pallas_call(body, name="grad_x", grid=(nb,), in_specs=[blk] * (nh + nn) + [nxt] * nn, out_specs=blk,
                          out_shape=_sds((rows, cols)), compiler_params=_cparams(("parallel",)))(*here, *from_next, *from_next)


def _adamw_math(w, g, m, v):
    m = ADAM_B1 * m + (1.0 - ADAM_B1) * g
    v = ADAM_B2 * v + (1.0 - ADAM_B2) * jnp.square(g)
    m_hat = m / (1.0 - ADAM_B1 ** ADAM_STEP)
    v_hat = v / (1.0 - ADAM_B2 ** ADAM_STEP)
    delta = -ADAM_LR * (m_hat / (jnp.sqrt(v_hat) + ADAM_EPS) + ADAM_WD * w)
    return delta, m, v


def _adamw(name, parts, w, m, v):
    rows, cols = w.shape
    tb = _rows_tile(rows)
    npart = len(parts)

    def body(*refs):
        g = refs[0][...]
        for r in refs[1:npart]:
            g = g + r[...]
        w_ref, m_ref, v_ref = refs[npart:npart + 3]
        g_out, d_out, m_out, v_out = refs[npart + 3:]
        delta, mn, vn = _adamw_math(w_ref[...], g, m_ref[...], v_ref[...])
        g_out[...] = g
        d_out[...] = delta
        m_out[...] = mn
        v_out[...] = vn

    blk = pl.BlockSpec((tb, cols), lambda i: (i, 0))
    return pl.pallas_call(body, name=name, grid=(rows // tb,), in_specs=[blk] * (npart + 3), out_specs=[blk] * 4,
                          out_shape=[_sds((rows, cols))] * 4, compiler_params=_cparams(("parallel",)))(*parts, w, m, v)


def _shift_down(a):
    return jnp.concatenate([jnp.zeros_like(a[:1]), a[:-1]], axis=0)


def _s5_pack_mask(g):
    return (jnp.arange(g)[None, :] % S5_PACK == jnp.arange(S5_PACK)[:, None]).astype(F32)


def _compact_b(bbar_t):
    g, s, p = bbar_t.shape
    return (_s5_pack_mask(g)[:, None, :, None] * bbar_t.transpose(1, 0, 2)[None]).reshape(S5_PACK * s, g * p)


def _compact_b_t(dense, g):
    s, p = dense.shape[0] // S5_PACK, dense.shape[1] // g
    return jnp.sum(dense.reshape(S5_PACK, s, g, p) * _s5_pack_mask(g)[:, None, :, None], axis=0).transpose(1, 0, 2)


def _compact_c(c_w):
    g, s, p = c_w.shape
    return (c_w.transpose(0, 2, 1)[:, :, None, :] * _s5_pack_mask(g).T[:, None, :, None]).reshape(g * p, S5_PACK * s)


def _compact_c_t(dense, g):
    p, s = dense.shape[0] // g, dense.shape[1] // S5_PACK
    return jnp.sum(dense.reshape(g, p, S5_PACK, s) * _s5_pack_mask(g).T[:, None, :, None], axis=2).transpose(0, 2, 1)


def _local_step(x, target, fw, core):
    t, c = x.shape
    nh = c // HEAD
    ng = c // SSM_GROUP
    tb = min(256, t)
    tbm = min(512, t)
    tbmf = min(1024, t)
    tbmb = min(512, t)
    tbs = min(256, t)
    tk5 = min(4096, t)
    hg = min(16, nh)
    mu = [fw['rw_mu'][i:i + 1] for i in range(6)]
    ln_g = [fw['ln_g'][i:i + 1] for i in range(4)]
    ln_b = [fw['ln_b'][i:i + 1] for i in range(4)]
    grads = {}

    xp = _shift_down(x)
    proj_params = {n: (mu[i], fw['rw_w' + n]) for n, i in (('r', 0), ('k', 2), ('v', 3))}
    raw = {n: _stage_fwd("proj_" + n, _f_proj, (x, xp), proj_params[n], (c,), tbm)[0] for n in 'rkv'}
    lora_params = (mu[1], mu[4], mu[5], fw['rw_w0'], fw['rw_w1'], fw['rw_w2'], fw['rw_a0'], fw['rw_a1'], fw['rw_a2'],
                   fw['rw_g1'], fw['rw_g2'], fw['rw_k_k'], fw['rw_k_a'])
    lw, k2, an, bb, gate = _stage_fwd("lora", _f_lora, (x, xp, raw['k']), lora_params, (c,) * 5, tbm)
    rec_in = (raw['r'], lw, k2, raw['v'], an, bb)
    o, s0s, (wo_view, w1_l0, w2_l0) = _rec_fwd(*rec_in, hg, fw['late_a'])
    fw = dict(fw, rw_wo=wo_view.reshape(c, c))
    mlp_w = [(w1_l0.reshape(4, 1, c, -1), w2_l0.reshape(4, 1, -1, c)), None]
    post_params = (fw['rw_lnx_g'], fw['rw_lnx_b'], fw['rw_r_k'], fw['rw_wo'], ln_g[0], ln_b[0])
    post_acts = (o, raw['r'], k2, raw['v'], gate, x)
    h1, = _stage_fwd("post", _f_post, post_acts, post_params, (c,), tb)
    h2, s_mlp0, (w1_l1,) = _mlp_fwd("mlp0_fwd", h1, *mlp_w[0], 0, ln_g[1], ln_b[1], tbmf, fw['late_b'])

    a_re, a_im, log_dt = fw['s5_a_re'], fw['s5_a_im'], fw['s5_log_dt']
    b_re_t, b_im_t = fw['s5_b_re'].transpose(0, 2, 1), fw['s5_b_im'].transpose(0, 2, 1)
    abar_re, abar_im, bbar_re_t, bbar_im_t = _zoh_fwd(a_re, a_im, log_dt, b_re_t, b_im_t)
    abar = jnp.concatenate([abar_re.reshape(1, -1), abar_im.reshape(1, -1)], axis=1)
    bc = jnp.concatenate([_compact_b(bbar_re_t), _compact_b(bbar_im_t)], axis=1).astype(BF16)
    cc = jnp.concatenate([_compact_c(fw['s5_c_re']), -_compact_c(fw['s5_c_im'])], axis=0).astype(BF16)
    st, ys, (glu_view, w2_l1) = _s5_fwd(h2, bc, abar, cc, tbs, fw['late_c'])
    mlp_w[1] = (w1_l1.reshape(4, 1, c, -1), w2_l1.reshape(4, 1, -1, c))
    fw = dict(fw, s5_w_glu=tuple(glu_view[q] for q in range(4)))
    glu_params = (fw['s5_d'], *fw['s5_w_glu'], ln_g[2], ln_b[2])
    h3, = _stage_fwd("glu", _f_glu, (ys, h2), glu_params, (c,), tbm)
    h4, s_mlp1, _ = _mlp_fwd("mlp1_fwd", h3, *mlp_w[1], 0, ln_g[3], ln_b[3], tbmf)

    loss_blk, dh4 = _loss_head(h4, target, tb)

    dln_g, dln_b = [None] * 4, [None] * 4
    dh3, ds1, dhid1, act1, dln_g[3], dln_b[3] = _mlp_bwd("mlp1_bwd", h3, s_mlp1, dh4, *mlp_w[1], 0,
                                                         ln_g[3], ln_b[3], tbmb)
    dw1 = _mlp_weight_grad("mlp1_dw1", h3, dhid1, 1, DEPTH, "n")
    dw2 = _mlp_weight_grad("mlp1_dw2", act1, ds1, 1, DEPTH, "m")
    (dys, dh2_glu), (grads['s5_d'], *dglu, dln_g[2], dln_b[2]) = _stage_bwd(
        "glu_bwd", _f_glu, (ys, h2), glu_params, (dh3,), tbm, proxied=(1, 2, 3, 4))
    grads['s5_w_glu'] = jnp.stack(dglu)
    dcc = _s5_weight_grad("s5_dcc", dys, st, True, tk5)
    dbu, dh2_bu, dabar = _s5_bwd(dys, st, abar, cc, bc, tbs)
    dbc = _s5_weight_grad("s5_dbc", h2, dbu, False, tk5)
    gp = ng * SSM_STATE
    grads['s5_c_re'] = _compact_c_t(dcc[:gp], ng)
    grads['s5_c_im'] = -_compact_c_t(dcc[gp:], ng)
    zoh_couts = (dabar[:, :gp].reshape(ng, SSM_STATE), dabar[:, gp:].reshape(ng, SSM_STATE),
                 _compact_b_t(dbc[:, :gp], ng), _compact_b_t(dbc[:, gp:], ng))
    grads['s5_a_re'], grads['s5_a_im'], grads['s5_log_dt'], db_re_t, db_im_t = _zoh_bwd(
        a_re, a_im, log_dt, b_re_t, b_im_t, zoh_couts)
    grads['s5_b_re'], grads['s5_b_im'] = db_re_t.transpose(0, 2, 1), db_im_t.transpose(0, 2, 1)
    dh2 = (dh2_glu, dh2_bu)

    dh1, ds0, dhid0, act0, dln_g[1], dln_b[1] = _mlp_bwd("mlp0_bwd", h1, s_mlp0, dh2, *mlp_w[0], 0,
                                                         ln_g[1], ln_b[1], tbmb)
    grads['mlp_w1'] = _mlp_weight_grad("mlp0_dw1", h1, dhid0, 0, DEPTH, "n", into=dw1)
    grads['mlp_w2'] = _mlp_weight_grad("mlp0_dw2", act0, ds0, 0, DEPTH, "m", into=dw2)
    ready_views = [grads[n].reshape(4, -1, grads[n].shape[-1]) for n in BIG_READY]
    (do, dr_p, dk2_p, dv_p, dgate, dx_post), post_g, ready_others = _stage_bwd(
        "post_bwd", _f_post, post_acts, post_params, (dh1,), tb, proxied=(3,), halves_of=ready_views)
    grads['rw_lnx_g'], grads['rw_lnx_b'], grads['rw_r_k'], grads['rw_wo'], dln_g[0], dln_b[0] = post_g
    ready_sums = [_half_add(f"half_add_a{i}", v, o, core) for i, (v, o) in enumerate(zip(ready_views, ready_others))]
    rec_g, ready_lands = _rec_bwd(*rec_in, s0s, do, hg, ready_sums)
    reduced = dict(zip(BIG_READY, zip(ready_sums, ready_lands)))
    dr_r, dlw, dk2_r, dv_r, dan, dbb = rec_g
    dk2 = (dk2_p, dk2_r)
    (dx_l, dxp_l, dkraw_l), lora_g = _stage_bwd("lora_bwd", _f_lora, (x, xp, raw['k']), lora_params,
                                                (dlw, dk2, dan, dbb, dgate), tb)
    (dmu_w, dmu_a, dmu_g, grads['rw_w0'], grads['rw_w1'], grads['rw_w2'], grads['rw_a0'], grads['rw_a1'], grads['rw_a2'],
     grads['rw_g1'], grads['rw_g2'], grads['rw_k_k'], grads['rw_k_a']) = lora_g
    dproj = {'r': (dr_p, dr_r), 'k': dkraw_l, 'v': (dv_p, dv_r)}
    dxs, dxps, dmu = [dx_post, dx_l], [dxp_l], {}
    for n in 'rkv':
        (dx_n, dxp_n), (dmu[n], grads['rw_w' + n]) = _stage_bwd("proj_bwd_" + n, _f_proj, (x, xp), proj_params[n],
                                                                 (dproj[n],), tbm, proxied=(1,))
        dxs.append(dx_n)
        dxps.append(dxp_n)
    grads['rw_mu'] = jnp.concatenate([dmu['r'], dmu_w, dmu['k'], dmu['v'], dmu_a, dmu_g], axis=0)
    grads['ln_g'] = jnp.concatenate(dln_g, axis=0)
    grads['ln_b'] = jnp.concatenate(dln_b, axis=0)
    grad_x = _grad_x(dxs, dxps)
    late = [n for n in BIG if n not in BIG_READY]
    late_sums = dict(zip(late, _chip_sums("b", [grads[n] for n in late], core)))
    return loss_blk, grad_x, grads, reduced, late_sums


def _position():
    return lax.axis_index("x"), lax.axis_index("y"), lax.axis_index("c")


def _other_chips(x, y):
    return [(1 - x, y), (x, 1 - y), (1 - x, 1 - y)]


def _chip_slice(ref, axis, q, size):
    idx = [slice(None)] * len(ref.shape)
    idx[axis] = pl.ds(pl.multiple_of(q * size, size), size)
    return ref.at[tuple(idx)]


_HBM = pl.BlockSpec(memory_space=pltpu.HBM)


def _gather_small_phases(src, dst, axes, sems):
    n = len(src)
    send_sems, recv_sems, own_sems = sems
    x, y, c = _position()
    chips = _other_chips(x, y)
    sizes = [src[a].shape[axes[a]] for a in range(n)]

    def copy(a, k, q):
        return pltpu.make_async_remote_copy(
            src_ref=src[a], dst_ref=_chip_slice(dst[a], axes[a], q, sizes[a]), send_sem=send_sems.at[a, k],
            recv_sem=recv_sems.at[a, k], device_id=(*chips[k], c), device_id_type=MESH)

    def own(a):
        return pltpu.make_async_copy(src[a], _chip_slice(dst[a], axes[a], 2 * x + y, sizes[a]), own_sems.at[a])

    def start():
        for a in range(n):
            own(a).start()
            for k in range(3):
                copy(a, k, 2 * x + y).start()

    def finish():
        for a in range(n):
            for k, (cx, cy) in enumerate(chips):
                copy(a, k, 2 * cx + cy).wait_recv()
        for a in range(n):
            for k in range(3):
                copy(a, k, 2 * x + y).wait_send()
            own(a).wait()

    return start, finish


def _gather_early(big, small, axes):
    nb, ns = len(big), len(small)
    full_shapes = [tuple(s * 4 if i == ax else s for i, s in enumerate(a.shape)) for a, ax in zip(small, axes)]

    def body(*refs):
        src_b, src_s = refs[:nb], refs[nb:nb + ns]
        dst_b, dst_s = refs[nb + ns:2 * nb + ns], refs[2 * nb + ns:2 * (nb + ns)]
        sems = refs[2 * (nb + ns):]
        small_start, small_finish = _gather_small_phases(src_s, dst_s, axes, sems[5:])
        small_start()
        for phase in _gather_big_phases(src_b, dst_b, sems[:5]):
            phase()
        small_finish()

    outs = pl.pallas_call(
        body, name="gather_early", in_specs=[_HBM] * (nb + ns), out_specs=[_HBM] * (nb + ns),
        out_shape=[_sds((4,) + a.shape, a.dtype) for a in big] + [_sds(s, a.dtype) for s, a in zip(full_shapes, small)],
        scratch_shapes=_gather_big_sems(nb) + [pltpu.SemaphoreType.DMA((ns, 3)), pltpu.SemaphoreType.DMA((ns, 3)),
                                               pltpu.SemaphoreType.DMA((ns,))],
        compiler_params=_cparams(),
    )(*big, *small)
    return outs[:nb], outs[nb:]


def _scatter_pieces(fulls, axes, sums):
    n, nsum = len(fulls), len(sums)
    sizes = [a.shape[ax] // 4 for a, ax in zip(fulls, axes)]
    shard_shapes = [tuple(sz if i == ax else s for i, s in enumerate(a.shape)) for a, ax, sz in zip(fulls, axes, sizes)]

    def body(*refs):
        src, big_src = refs[:n], refs[n:n + nsum]
        land, big_land = refs[n + nsum:2 * n + nsum], refs[2 * n + nsum:2 * (n + nsum)]
        send_sems, recv_sems = refs[2 * (n + nsum):2 * (n + nsum) + 2]
        big_start, big_finish = _scatter_big_phases(big_src, big_land, refs[2 * (n + nsum) + 2:])
        big_start()
        x, y, c = _position()
        chips = _other_chips(x, y)

        def copy(a, k):
            cx, cy = chips[k]
            return pltpu.make_async_remote_copy(
                src_ref=_chip_slice(src[a], axes[a], 2 * cx + cy, sizes[a]), dst_ref=land[a].at[k],
                send_sem=send_sems.at[a, k], recv_sem=recv_sems.at[a, k], device_id=(cx, cy, c), device_id_type=MESH)

        for a in range(n):
            for k in range(3):
                copy(a, k).start()
        for a in range(n):
            for k in range(3):
                copy(a, k).wait_recv()
        for a in range(n):
            for k in range(3):
                copy(a, k).wait_send()
        big_finish()

    outs = pl.pallas_call(
        body, name="scatter_grads", in_specs=[_HBM] * (n + nsum), out_specs=[_HBM] * (n + nsum),
        out_shape=[_sds((3,) + s) for s in shard_shapes] + [_sds((3,) + s.shape[1:], s.dtype) for s in sums],
        scratch_shapes=[pltpu.SemaphoreType.DMA((n, 3)), pltpu.SemaphoreType.DMA((n, 3))] + _scatter_big_sems(nsum),
        compiler_params=_cparams(),
    )(*fulls, *sums)
    return outs[:n], outs[n:]


def _sibling_swap(name, arrs):
    n = len(arrs)

    def body(*refs):
        src, dst = refs[:n], refs[n:2 * n]
        send_sems, recv_sems = refs[2 * n:]
        x, y, c = _position()
        copies = [pltpu.make_async_remote_copy(src_ref=src[a], dst_ref=dst[a], send_sem=send_sems.at[a], recv_sem=recv_sems.at[a],
                                               device_id=(x, y, 1 - c), device_id_type=MESH) for a in range(n)]
        for cp in copies:
            cp.start()
        for cp in copies:
            cp.wait_recv()
        for cp in copies:
            cp.wait_send()

    return pl.pallas_call(
        body, name=name, in_specs=[_HBM] * n, out_specs=[_HBM] * n, out_shape=[_sds(a.shape) for a in arrs],
        scratch_shapes=[pltpu.SemaphoreType.DMA((n,)), pltpu.SemaphoreType.DMA((n,))],
        compiler_params=_cparams(),
    )(*arrs)


def _sum4(name, own, land):
    rows, cols = own.shape
    tb = _rows_tile(rows)

    def body(o_ref, l0, l1, l2, out_ref):
        out_ref[...] = ((o_ref[...] + l0[...]) + l1[...]) + l2[...]

    blk = pl.BlockSpec((tb, cols), lambda i: (i, 0))
    lands = [pl.BlockSpec((None, tb, cols), functools.partial(lambda k, i: (k, i, 0), k)) for k in range(3)]
    return pl.pallas_call(body, name=name, grid=(rows // tb,), in_specs=[blk] + lands, out_specs=blk,
                          out_shape=_sds((rows, cols)), compiler_params=_cparams(("parallel",)))(own, land, land, land)


def _allreduce_adamw_small(g, w, m, v):
    rows, lanes = g.shape

    def body(g_ref, w_ref, m_ref, v_ref, gs_ref, d_ref, mn_ref, vn_ref, land, send_sems, recv_sems):
        x, y, c = _position()
        me = 4 * x + 2 * y + c
        chips = _other_chips(x, y)
        sibling = (x, y, 1 - c)

        def slot_of(cx, cy, cc):
            return 4 * cx + 2 * cy + cc

        def copy(j, slot, to, src=None):
            return pltpu.make_async_remote_copy(src_ref=g_ref if src is None else land.at[src], dst_ref=land.at[slot],
                                                send_sem=send_sems.at[j], recv_sem=recv_sems.at[j], device_id=to, device_id_type=MESH)

        copy(0, me, sibling).start()
        for k, chip in enumerate(chips):
            copy(1 + k, me, (*chip, c)).start()
        land[me] = g_ref[...]
        for k, chip in enumerate(chips):
            theirs = slot_of(*chip, c)
            copy(1 + k, theirs, (*chip, c)).wait_recv()
            copy(4 + k, theirs, sibling, src=theirs).start()
        copy(0, slot_of(*sibling), sibling).wait_recv()
        for k, chip in enumerate(chips):
            copy(4 + k, slot_of(*chip, 1 - c), sibling).wait_recv()
        copy(0, me, sibling).wait_send()
        for k, chip in enumerate(chips):
            copy(1 + k, me, (*chip, c)).wait_send()
            copy(4 + k, slot_of(*chip, c), sibling, src=slot_of(*chip, c)).wait_send()
        total = land[0]
        for dev in range(1, 8):
            total = total + land[dev]
        delta, mn, vn = _adamw_math(w_ref[...], total, m_ref[...], v_ref[...])
        gs_ref[...] = total
        d_ref[...] = delta
        mn_ref[...] = mn
        vn_ref[...] = vn

    vmem = pl.BlockSpec(memory_space=pltpu.VMEM)
    return pl.pallas_call(
        body, name="allreduce_adamw_small", in_specs=[vmem] * 4, out_specs=[vmem] * 4, out_shape=[_sds((rows, lanes))] * 4,
        scratch_shapes=[pltpu.VMEM((8, rows, lanes), F32), pltpu.SemaphoreType.DMA((7,)), pltpu.SemaphoreType.DMA((7,))],
        compiler_params=_cparams(),
    )(g, w, m, v)


def _row_half(ref, c):
    r2 = ref.shape[-2] // 2
    lead = (slice(None),) * (len(ref.shape) - 2)
    return ref.at[(*lead, pl.ds(pl.multiple_of(c * r2, r2), r2), slice(None))]


def _gather_big_phases(src, dst, sems):
    n = len(src)
    ici_send, ici_recv, d2d_send, d2d_recv, own_sems = sems
    x, y, c = _position()
    me = 2 * x + y
    chips = _other_chips(x, y)
    ids = [2 * cx + cy for cx, cy in chips]

    def ici(a, k, q):
        return pltpu.make_async_remote_copy(
            src_ref=_row_half(src[a], c), dst_ref=_row_half(dst[a].at[q], c), send_sem=ici_send.at[a, k],
            recv_sem=ici_recv.at[a, k], device_id=(*chips[k], c), device_id_type=MESH)

    def d2d(a, k, half):
        where = _row_half(dst[a].at[ids[k]], half)
        return pltpu.make_async_remote_copy(src_ref=where, dst_ref=where, send_sem=d2d_send.at[a, k], recv_sem=d2d_recv.at[a, k],
                                            device_id=(x, y, 1 - c), device_id_type=MESH)

    def own(a):
        return pltpu.make_async_copy(src[a], dst[a].at[me], own_sems.at[a])

    def start():
        for a in range(n):
            own(a).start()
            for k in range(3):
                ici(a, k, me).start()

    def forward():
        for a in range(n):
            for k in range(3):
                ici(a, k, ids[k]).wait_recv()
                d2d(a, k, c).start()

    def finish():
        for a in range(n):
            for k in range(3):
                d2d(a, k, 1 - c).wait_recv()
        for a in range(n):
            for k in range(3):
                ici(a, k, me).wait_send()
                d2d(a, k, c).wait_send()
            own(a).wait()

    return start, forward, finish


def _gather_big_sems(n):
    return [pltpu.SemaphoreType.DMA((n, 3))] * 4 + [pltpu.SemaphoreType.DMA((n,))]


def _chip_sums(tag, grads, core):
    views = [g.reshape(4, -1, g.shape[-1]) for g in grads]
    others = _sibling_halves("sibling_halves_" + tag, views)
    return [_half_add(f"half_add_{tag}{i}", v, o, core) for i, (v, o) in enumerate(zip(views, others))]


def _sibling_halves_phases(src, dst, sems):
    n = len(src)
    send_sems, recv_sems = sems
    x, y, c = _position()

    def copy(a):
        return pltpu.make_async_remote_copy(src_ref=_row_half(src[a], 1 - c), dst_ref=dst[a], send_sem=send_sems.at[a],
                                            recv_sem=recv_sems.at[a], device_id=(x, y, 1 - c), device_id_type=MESH)

    def start():
        for a in range(n):
            copy(a).start()

    def finish():
        for a in range(n):
            copy(a).wait_recv()
        for a in range(n):
            copy(a).wait_send()

    return start, finish


def _sibling_halves_sems(n):
    return [pltpu.SemaphoreType.DMA((n,)), pltpu.SemaphoreType.DMA((n,))]


def _sibling_halves_shapes(views):
    return [_sds((4, v.shape[1] // 2, v.shape[2])) for v in views]


def _sibling_halves(name, views):
    n = len(views)

    def body(*refs):
        for phase in _sibling_halves_phases(refs[:n], refs[n:2 * n], refs[2 * n:]):
            phase()

    return pl.pallas_call(
        body, name=name, in_specs=[_HBM] * n, out_specs=[_HBM] * n, out_shape=_sibling_halves_shapes(views),
        scratch_shapes=_sibling_halves_sems(n), compiler_params=_cparams(),
    )(*views)


def _rows_tile_capped(rows, cap=256):
    return min(_rows_tile(rows), cap)


def _half_add(name, view, other, core):
    _, r, k = view.shape
    r2 = r // 2
    tr = _rows_tile_capped(r2, 512)
    per = r2 // tr

    def body(c_ref, v_ref, o_ref, out_ref):
        out_ref[...] = (v_ref[...] + o_ref[...]).astype(BF16)

    blk = pl.BlockSpec((None, tr, k), lambda q, i, c: (q, i, 0))
    return pl.pallas_call(
        body, name=name,
        grid_spec=pltpu.PrefetchScalarGridSpec(
            num_scalar_prefetch=1, grid=(4, per),
            in_specs=[pl.BlockSpec((None, tr, k), lambda q, i, c: (q, c[0] * per + i, 0)), blk], out_specs=blk),
        out_shape=_sds((4, r2, k), BF16), compiler_params=_cparams(("parallel", "parallel")),
    )(core, view, other)


def _scatter_big_phases(src, land, sems):
    n = len(src)
    send_sems, recv_sems = sems
    x, y, c = _position()
    chips = _other_chips(x, y)

    def copy(a, k):
        cx, cy = chips[k]
        return pltpu.make_async_remote_copy(src_ref=src[a].at[2 * cx + cy], dst_ref=land[a].at[k], send_sem=send_sems.at[a, k],
                                            recv_sem=recv_sems.at[a, k], device_id=(cx, cy, c), device_id_type=MESH)

    def start():
        for a in range(n):
            for k in range(3):
                copy(a, k).start()

    def finish():
        for a in range(n):
            for k in range(3):
                copy(a, k).wait_recv()
        for a in range(n):
            for k in range(3):
                copy(a, k).wait_send()

    return start, finish


def _scatter_big_sems(n):
    return [pltpu.SemaphoreType.DMA((n, 3)), pltpu.SemaphoreType.DMA((n, 3))]


def _sum4_big(name, sums, land, chip):
    _, r2, k = sums.shape
    tr = _rows_tile_capped(r2, 512)

    def body(q_ref, s_ref, l0, l1, l2, out_ref):
        out_ref[...] = ((s_ref[...].astype(F32) + l0[...].astype(F32)) + l1[...].astype(F32)) + l2[...].astype(F32)

    lands = [pl.BlockSpec((None, tr, k), functools.partial(lambda j, i, q: (j, i, 0), j)) for j in range(3)]
    return pl.pallas_call(
        body, name=name,
        grid_spec=pltpu.PrefetchScalarGridSpec(
            num_scalar_prefetch=1, grid=(r2 // tr,),
            in_specs=[pl.BlockSpec((None, tr, k), lambda i, q: (q[0], i, 0))] + lands,
            out_specs=pl.BlockSpec((tr, k), lambda i, q: (i, 0))),
        out_shape=_sds((r2, k)), compiler_params=_cparams(("parallel",)),
    )(chip, sums, land, land, land)


def _adamw_halves(name, mine, theirs, w, m, v, core):
    r, k = w.shape
    r2 = r // 2
    tr = _rows_tile_capped(r2, 512)
    per = r2 // tr

    def body(c_ref, mine_ref, theirs_ref, w_ref, m_ref, v_ref, g_out, d_out, m_out, v_out):
        g = jnp.where(pl.program_id(0) == c_ref[0], mine_ref[...], theirs_ref[...])
        delta, mn, vn = _adamw_math(w_ref[...], g, m_ref[...], v_ref[...])
        g_out[...] = g
        d_out[...] = delta
        m_out[...] = mn
        v_out[...] = vn

    half = pl.BlockSpec((tr, k), lambda h, i, c: (i, 0))
    full = pl.BlockSpec((tr, k), lambda h, i, c: (h * per + i, 0))
    return pl.pallas_call(
        body, name=name,
        grid_spec=pltpu.PrefetchScalarGridSpec(num_scalar_prefetch=1, grid=(2, per), in_specs=[half, half, full, full, full],
                                               out_specs=[full] * 4),
        out_shape=[_sds((r, k))] * 4, compiler_params=_cparams(("parallel", "parallel")),
    )(core, mine, theirs, w, m, v)


def _drops_layer_axis(name):
    return not (name.startswith('mlp') or name == 's5_d')


def _work(name, arr):
    return arr.reshape(arr.shape[1:]) if _drops_layer_axis(name) else arr


def _work_axis(name):
    return SHARD_AXIS[name] - (1 if _drops_layer_axis(name) else 0)


def _as2d(a):
    return a.reshape(-1, a.shape[-1])


def _replicated_2d(name, arr):
    if name in ('ln_g', 'ln_b'):
        return arr
    if name == 'rw_r_k':
        return arr.reshape(1, -1)
    if name == 's5_log_dt':
        return arr.reshape(-1, 1)
    if name.startswith('s5_'):
        return arr.reshape(arr.shape[1:])
    return arr


def _pack(arrs):
    flat = []
    for a in arrs:
        f = a.reshape(-1)
        flat.append(jnp.pad(f, (0, -f.shape[0] % 128)))
    f = jnp.concatenate(flat)
    f = jnp.pad(f, (0, -f.shape[0] % 1024))
    return f.reshape(-1, 128)


def _unpack(packed, shapes):
    flat = packed.reshape(-1)
    out, at = [], 0
    for s in shapes:
        size = math.prod(s)
        out.append(flat[at:at + size].reshape(s))
        at += size + (-size % 128)
    return out


def kernel(x, ln_g, ln_b, rw_mu, rw_w0, rw_w1, rw_w2, rw_a0, rw_a1, rw_a2, rw_g1, rw_g2, rw_k_k, rw_k_a, rw_r_k, rw_wr, rw_wk, rw_wv, rw_wo, rw_lnx_g, rw_lnx_b, s5_a_re, s5_a_im, s5_log_dt, s5_b_re, s5_b_im, s5_c_re, s5_c_im, s5_d, s5_w_glu, mlp_w1, mlp_w2, loss_target, m_ln_g, m_ln_b, m_rw_mu, m_rw_w0, m_rw_w1, m_rw_w2, m_rw_a0, m_rw_a1, m_rw_a2, m_rw_g1, m_rw_g2, m_rw_k_k, m_rw_k_a, m_rw_r_k, m_rw_wr, m_rw_wk, m_rw_wv, m_rw_wo, m_rw_lnx_g, m_rw_lnx_b, m_s5_a_re, m_s5_a_im, m_s5_log_dt, m_s5_b_re, m_s5_b_im, m_s5_c_re, m_s5_c_im, m_s5_d, m_s5_w_glu, m_mlp_w1, m_mlp_w2, v_ln_g, v_ln_b, v_rw_mu, v_rw_w0, v_rw_w1, v_rw_w2, v_rw_a0, v_rw_a1, v_rw_a2, v_rw_g1, v_rw_g2, v_rw_k_k, v_rw_k_a, v_rw_r_k, v_rw_wr, v_rw_wk, v_rw_wv, v_rw_wo, v_rw_lnx_g, v_rw_lnx_b, v_s5_a_re, v_s5_a_im, v_s5_log_dt, v_s5_b_re, v_s5_b_im, v_s5_c_re, v_s5_c_im, v_s5_d, v_s5_w_glu, v_mlp_w1, v_mlp_w2):
    d = dict(locals())
    x_pos, y_pos, c_pos = _position()
    chip = 2 * x_pos + y_pos
    chip_arr = jnp.reshape(chip, (1,)).astype(jnp.int32)
    core_arr = jnp.reshape(c_pos, (1,)).astype(jnp.int32)

    small = [n for n in SHARD_AXIS if n not in BIG]
    axes = [_work_axis(n) for n in small]
    big_views, small_fulls = _gather_early([_as2d(d[n]).astype(BF16) for n in BIG_EARLY], [_work(n, d[n]) for n in small], axes)
    views = dict(zip(BIG_EARLY, big_views))
    fw = dict(zip(small, small_fulls))
    c_model = d['x'].shape[-1]
    for n in BIG_EARLY:
        fw[n] = views[n].reshape(c_model, c_model)
    w1_layers, w2_layers = d['mlp_w1'].astype(BF16), d['mlp_w2'].astype(BF16)
    fw['late_a'] = [_as2d(d['rw_wo']).astype(BF16), w1_layers[0], w2_layers[0]]
    fw['late_b'] = [w1_layers[1]]
    fw['late_c'] = [_as2d(d['s5_w_glu']).astype(BF16), w2_layers[1]]
    for n in REPLICATED:
        fw[n] = _replicated_2d(n, d[n])

    loss_blk, grad_x, grads, reduced, late_sums = _local_step(d['x'][0], d['loss_target'][0], fw, core_arr)
    loss = lax.psum(loss_blk[0, 0], ('x', 'y', 'c'))
    out = {}

    pieces = [grads[n] for n in small]
    lands, late_lands = _scatter_pieces(pieces, axes, list(late_sums.values()))
    reduced.update(zip(late_sums, zip(late_sums.values(), late_lands)))
    mine = [_sum4_big("sum4_" + n, *reduced[n], chip_arr) for n in BIG]
    for n, g, ax, land in zip(small, pieces, axes, lands):
        size = g.shape[ax] // 4
        mine.append(_sum4("sum4_" + n, lax.dynamic_slice_in_dim(g, chip * size, size, ax), land))
    theirs = _sibling_swap("swap_sums", mine)
    for n, a, b in zip(BIG + small, mine, theirs):
        w2d, m2d, v2d = _as2d(d[n]), _as2d(d['m_' + n]), _as2d(d['v_' + n])
        res = (_adamw_halves("adamw_" + n, a, b, w2d, m2d, v2d, core_arr) if n in BIG
               else _adamw("adamw_" + n, (a, b), w2d, m2d, v2d))
        out[n] = [r.reshape(d[n].shape) for r in res]

    rep_shapes = [d[n].shape for n in REPLICATED]
    packs = [_pack([grads[n] for n in REPLICATED])] + [_pack([d[p + n] for n in REPLICATED]) for p in ('', 'm_', 'v_')]
    res = [_unpack(p, rep_shapes) for p in _allreduce_adamw_small(*packs)]
    for i, n in enumerate(REPLICATED):
        out[n] = [r[i] for r in res]

    grad_x = grad_x.reshape(d['x'].shape)
    return (loss, grad_x, *[out[n][0] for n in WEIGHTS], *[out[n][1] for n in WEIGHTS],
            *[out[n][2] for n in WEIGHTS], *[out[n][3] for n in WEIGHTS])
```

```python
import functools
import math

import jax
import jax.numpy as jnp
from jax import lax
from jax.experimental import pallas as pl
from jax.experimental.pallas import tpu as pltpu

F32 = jnp.float32
BF16 = jnp.bfloat16
MESH = pl.DeviceIdType.MESH

HEAD = 64
SSM_GROUP = 16
SSM_STATE = 64
GN_EPS = 64e-5
LN_EPS = 1e-5
DEPTH = 2
DN_ALPHA = (2.0 * DEPTH) ** 0.25
ADAM_LR, ADAM_B1, ADAM_B2, ADAM_EPS, ADAM_WD, ADAM_STEP = 0.001, 0.9, 0.999, 1e-08, 0.01, 10
REC_CHUNK = 64
V7X_VMEM_BYTES = 64 * 2 ** 20
VMEM_LIMIT = V7X_VMEM_BYTES - 8 * 2 ** 20

WEIGHTS = ['ln_g', 'ln_b', 'rw_mu', 'rw_w0', 'rw_w1', 'rw_w2', 'rw_a0', 'rw_a1', 'rw_a2', 'rw_g1', 'rw_g2',
           'rw_k_k', 'rw_k_a', 'rw_r_k', 'rw_wr', 'rw_wk', 'rw_wv', 'rw_wo', 'rw_lnx_g', 'rw_lnx_b',
           's5_a_re', 's5_a_im', 's5_log_dt', 's5_b_re', 's5_b_im', 's5_c_re', 's5_c_im', 's5_d', 's5_w_glu',
           'mlp_w1', 'mlp_w2']
SHARD_AXIS = {'rw_mu': 2, 'rw_w1': 1, 'rw_w2': 2, 'rw_a1': 1, 'rw_a2': 2, 'rw_g1': 1, 'rw_g2': 2,
              'rw_wr': 1, 'rw_wk': 1, 'rw_wv': 1, 'rw_wo': 1, 's5_d': 1, 's5_w_glu': 2, 'mlp_w1': 2, 'mlp_w2': 1}
REPLICATED = [n for n in WEIGHTS if n not in SHARD_AXIS]
BIG_EARLY = ['rw_wr', 'rw_wk', 'rw_wv']
BIG_LATE = ['rw_wo', 's5_w_glu', 'mlp_w1', 'mlp_w2']
BIG = BIG_EARLY + BIG_LATE
BIG_READY = ['s5_w_glu', 'mlp_w1', 'mlp_w2']


def _sds(shape, dtype=F32):
    return jax.ShapeDtypeStruct(tuple(shape), dtype)


def _cparams(sem=None, **kw):
    if sem is not None:
        kw["dimension_semantics"] = sem
    return pltpu.CompilerParams(vmem_limit_bytes=VMEM_LIMIT, **kw)


def _mm_products(a, b, g):
    gb = g.astype(BF16)
    da = lax.dot_general(gb, b.astype(BF16), (((1,), (1,)), ((), ())), preferred_element_type=F32)
    db = lax.dot_general(a.astype(BF16), gb, (((0,), (0,)), ((), ())), preferred_element_type=F32)
    return da, db


@jax.custom_vjp
def _mm_plain(a, b):
    return jnp.dot(a.astype(BF16), b.astype(BF16), preferred_element_type=F32)


def _mm_plain_bwd(res, g):
    da, db = _mm_products(*res, g)
    return da.astype(res[0].dtype), db.astype(res[1].dtype)


_mm_plain.defvjp(lambda a, b: (_mm_plain(a, b), (a, b)), _mm_plain_bwd)


@jax.custom_vjp
def _mm_proxy(a, b, z):
    return jnp.dot(a.astype(BF16), b.astype(BF16), preferred_element_type=F32)


def _mm_proxy_bwd(res, g):
    da, db = _mm_products(*res, g)
    return da.astype(res[0].dtype), jnp.zeros_like(res[1]), db


_mm_proxy.defvjp(lambda a, b, z: (_mm_proxy(a, b, z), (a, b)), _mm_proxy_bwd)


def mm(a, b, z=None):
    return _mm_plain(a, b) if z is None else _mm_proxy(a, b, z)


def _split3(x):
    hi = x.astype(BF16)
    r1 = x - hi.astype(F32)
    mid = r1.astype(BF16)
    lo = (r1 - mid.astype(F32)).astype(BF16)
    return hi, mid, lo


def _head_sum_impl(x):
    c = x.shape[1]
    lanes = 128
    sel = (lax.broadcasted_iota(jnp.int32, (c, lanes), 0) // HEAD
           == lax.broadcasted_iota(jnp.int32, (c, lanes), 1)).astype(BF16)
    s = sum(jnp.dot(p, sel, preferred_element_type=F32) for p in _split3(x))
    return sum(lax.dot_general(p, sel, (((1,), (1,)), ((), ())), preferred_element_type=F32) for p in _split3(s))


@jax.custom_vjp
def head_sum(x):
    return _head_sum_impl(x)


head_sum.defvjp(lambda x: (_head_sum_impl(x), None), lambda _, g: (_head_sum_impl(g),))


def _ln(x, g, b):
    mu = jnp.mean(x, axis=-1, keepdims=True)
    xc = x - mu
    var = jnp.mean(xc * xc, axis=-1, keepdims=True)
    return xc * lax.rsqrt(var + LN_EPS) * g + b


def _f_proj(acts, params, proxies):
    x, xp = acts
    mu, w = params
    return (mm(x + (xp - x) * mu, w, proxies[1]),)


def _f_lora(acts, params, proxies):
    x, xp, kraw = acts
    mu_w, mu_a, mu_g, w0, w1, w2, a0, a1, a2, g1, g2, k_k, k_a = params
    xx = xp - x
    w_pre = w0 + mm(jnp.tanh(mm(x + xx * mu_w, w1)), w2)
    z = -w_pre
    softplus = jnp.maximum(z, 0.0) + jnp.log(1.0 + jnp.exp(-jnp.abs(z)))
    log_decay = -jnp.exp(-softplus - 0.5)
    a = jax.nn.sigmoid(a0 + mm(mm(x + xx * mu_a, a1), a2))
    g = mm(jax.nn.sigmoid(mm(x + xx * mu_g, g1)), g2)
    kk = kraw * k_k
    kkn = kk / jnp.maximum(jnp.sqrt(head_sum(kk * kk)), 1e-12)
    k2 = kraw * (1.0 + (a - 1.0) * k_a)
    return log_decay, k2, -kkn, kkn * a, g


def _f_post(acts, params, proxies):
    o, r, k2, v, g, x = acts
    lnx_g, lnx_b, r_k, wo, ln_g, ln_b = params
    om = head_sum(o) * (1.0 / HEAD)
    oc = o - om
    ov = head_sum(oc * oc) * (1.0 / HEAD)
    on = oc * lax.rsqrt(ov + GN_EPS) * lnx_g + lnx_b
    bonus = head_sum(r * k2 * r_k) * v
    y = mm((on + bonus) * g, wo, proxies[3])
    return (_ln(DN_ALPHA * x + y, ln_g, ln_b),)


def _f_glu(acts, params, proxies):
    ys, h = acts
    d, wv0, wv1, wg0, wg1, ln_g, ln_b = params
    y = jax.nn.gelu(ys + h * d)
    mix = jnp.concatenate([mm(y, wv0, proxies[1]) * jax.nn.sigmoid(mm(y, wg0, proxies[3])),
                           mm(y, wv1, proxies[2]) * jax.nn.sigmoid(mm(y, wg1, proxies[4]))], axis=1)
    return (_ln(DN_ALPHA * h + mix, ln_g, ln_b),)


def _f_zoh(a_re, a_im, log_dt, b_re_t, b_im_t):
    dt = jnp.exp(log_dt)
    lam_re = jnp.minimum(a_re, -1e-4)
    lam_im = a_im
    mag = jnp.exp(dt * lam_re)
    abar_re = mag * jnp.cos(dt * lam_im)
    abar_im = mag * jnp.sin(dt * lam_im)
    den = lam_re * lam_re + lam_im * lam_im
    nr, ni = abar_re - 1.0, abar_im
    coef_re = ((nr * lam_re + ni * lam_im) / den)[:, None, :]
    coef_im = ((ni * lam_re - nr * lam_im) / den)[:, None, :]
    return (abar_re, abar_im, coef_re * b_re_t - coef_im * b_im_t, coef_re * b_im_t + coef_im * b_re_t)


def _bdot16_raw(a, b, ca, cb):
    return lax.dot_general(a.astype(BF16), b.astype(BF16), (((ca,), (cb,)), ((0,), (0,))), preferred_element_type=F32)


@functools.partial(jax.custom_vjp, nondiff_argnums=(2, 3))
def _bdot16(a, b, ca, cb):
    return _bdot16_raw(a, b, ca, cb)


def _bdot16_bwd(ca, cb, res, g):
    a, b = res
    if (ca, cb) == (2, 1):
        return _bdot16_raw(g, b, 2, 2), _bdot16_raw(a, g, 1, 1)
    if (ca, cb) == (2, 2):
        return _bdot16_raw(g, b, 2, 1), _bdot16_raw(g, a, 1, 1)
    assert (ca, cb) == (1, 1)
    return _bdot16_raw(b, g, 2, 2), _bdot16_raw(a, g, 2, 1)


_bdot16.defvjp(lambda a, b, ca, cb: (_bdot16_raw(a, b, ca, cb), (a, b)), _bdot16_bwd)

def _time_sums(x, suffix):
    hg, ln, _ = x.shape
    row = lax.broadcasted_iota(jnp.int32, (hg, ln, ln), 1)
    col = lax.broadcasted_iota(jnp.int32, (hg, ln, ln), 2)
    tri = ((row <= col) if suffix else (row >= col)).astype(BF16)
    return sum(lax.dot_general(tri, p, (((2,), (1,)), ((0,), (0,))), preferred_element_type=F32) for p in _split3(x))


@jax.custom_vjp
def _time_cumsum(x):
    return _time_sums(x, False)


_time_cumsum.defvjp(lambda x: (_time_sums(x, False), None), lambda _, g: (_time_sums(g, True),))

_dot_score = _bdot16
_dot_inverse = _bdot16
_dot_value = _bdot16


def _rec_chunk(s0, r, lw, k, v, a, b):
    hg, ln, _ = r.shape
    row = lax.broadcasted_iota(jnp.int32, (hg, ln, ln), 1)
    col = lax.broadcasted_iota(jnp.int32, (hg, ln, ln), 2)
    incl, strict = row >= col, row > col
    cum = _time_cumsum(lw)
    total = jnp.sum(lw, axis=1, keepdims=True)
    e_cum, e_inv, e_prev, e_tail = jnp.exp(cum), jnp.exp(-cum), jnp.exp(cum - lw), jnp.exp(total - cum)
    rt, at, bt, kt = r * e_cum, a * e_prev, b * e_inv, k * e_inv
    ar = jnp.concatenate([at, rt], axis=1)
    on_b, on_k = _dot_score(ar, bt, 2, 2), _dot_score(ar, kt, 2, 2)
    aab, arb = jnp.where(strict, on_b[:, :ln], 0.0), jnp.where(incl, on_b[:, ln:], 0.0)
    aak, ark = jnp.where(strict, on_k[:, :ln], 0.0), jnp.where(incl, on_k[:, ln:], 0.0)
    p = (row == col).astype(F32) + aab
    m = aab
    for _ in range(int(math.log2(ln)) - 1):
        m = _dot_inverse(m, m, 2, 1)
        p = p + _dot_inverse(p, m, 2, 1)
    from_state = _dot_value(ar, s0, 2, 2)
    from_v = _dot_value(jnp.concatenate([aak, ark], axis=1), v, 2, 1)
    u = _dot_inverse(p, from_state[:, :ln] + from_v[:, :ln], 2, 1)
    o = from_state[:, ln:] + from_v[:, ln:] + _dot_value(arb, u, 2, 1)
    s1 = s0 * jnp.exp(total) + _dot_value(jnp.concatenate([u, v], axis=1),
                                          jnp.concatenate([b * e_tail, k * e_tail], axis=1), 1, 1)
    return o, s1


def _full_spec(shape):
    nd = len(shape)
    return pl.BlockSpec(tuple(shape), lambda *_: (0,) * nd)


def _stage_fwd(name, f, acts, params, out_dims, tb, small=(), axes=()):
    t = acts[0].shape[0]
    na, npar, nout, ns = len(acts), len(params), len(out_dims), len(small)
    steps = t // tb
    full_shapes = [tuple(s * 4 if i == ax else s for i, s in enumerate(a.shape)) for a, ax in zip(small, axes)]

    def body(*refs):
        src = refs[na + npar:na + npar + ns]
        o = na + npar + ns
        dst, sems = refs[o + nout:o + nout + ns], refs[o + nout + ns:]
        if ns:
            start, finish = _gather_small_phases(src, dst, axes, sems)
            pl.when(pl.program_id(0) == 0)(start)
        outs = f(tuple(r[...] for r in refs[:na]), tuple(r[...] for r in refs[na:na + npar]), (None,) * npar)
        for r, val in zip(refs[o:o + nout], outs):
            r[...] = val
        if ns:
            pl.when(pl.program_id(0) == steps - 1)(finish)

    outs = pl.pallas_call(
        body, name=name, grid=(steps,),
        in_specs=[pl.BlockSpec((tb, a.shape[1]), lambda i: (i, 0)) for a in acts] + [_full_spec(p.shape) for p in params]
        + [_HBM] * ns,
        out_specs=[pl.BlockSpec((tb, d), lambda i: (i, 0)) for d in out_dims] + [_HBM] * ns,
        out_shape=[_sds((t, d)) for d in out_dims] + [_sds(s, a.dtype) for s, a in zip(full_shapes, small)],
        scratch_shapes=([pltpu.SemaphoreType.DMA((ns, 3)), pltpu.SemaphoreType.DMA((ns, 3)), pltpu.SemaphoreType.DMA((ns,))]
                        if ns else []),
        compiler_params=_cparams(("arbitrary",)),
    )(*acts, *params, *small)
    return (outs[:nout], outs[nout:]) if ns else outs


def _stage_bwd(name, f, acts, params, couts, tb, proxied=(), halves_of=()):
    nh = len(halves_of)
    t = acts[0].shape[0]
    groups = [c if isinstance(c, tuple) else (c,) for c in couts]
    couts = [term for grp in groups for term in grp]
    na, npar, nc = len(acts), len(params), len(couts)
    steps = t // tb

    def f_diff(act_vals, diff_vals, param_vals):
        real = tuple(param_vals[i] if i in proxied else diff_vals[i] for i in range(npar))
        proxies = tuple(diff_vals[i] if i in proxied else None for i in range(npar))
        return f(act_vals, real, proxies)

    def body(*refs):
        a_refs, p_hbm, c_refs = refs[:na], refs[na:na + npar], refs[na + npar:na + npar + nc]
        o = na + npar + nc
        half_src, o = refs[o:o + nh], o + nh
        da_refs, dp_hbm, half_dst = refs[o:o + na], refs[o + na:o + na + npar], refs[o + na + npar:o + na + npar + nh]
        o = o + na + npar + nh
        p_buf, acc, half_sems = refs[o:o + npar], refs[o + npar:o + 2 * npar], refs[o + 2 * npar:]
        i = pl.program_id(0)
        if nh:
            half_start, half_finish = _sibling_halves_phases(half_src, half_dst, half_sems)
            pl.when(i == 0)(half_start)

        @pl.when(i == 0)
        def _():
            for src, dst in zip(p_hbm, p_buf):
                pltpu.sync_copy(src, dst)
            for r in acc:
                r[...] = jnp.zeros_like(r)

        param_vals = tuple(r[...] for r in p_buf)
        diff_vals = tuple(jnp.zeros(v.shape, F32) if i in proxied else v for i, v in enumerate(param_vals))
        _, vjp = jax.vjp(functools.partial(f_diff, param_vals=param_vals), tuple(r[...] for r in a_refs), diff_vals)
        terms = iter(c_refs)
        d_acts, d_params = vjp(tuple(functools.reduce(jnp.add, [next(terms)[...] for _ in grp]) for grp in groups))
        for r, val in zip(da_refs, d_acts):
            r[...] = val
        for r, val in zip(acc, d_params):
            r[...] += val

        @pl.when(i == steps - 1)
        def _():
            for src, dst in zip(acc, dp_hbm):
                pltpu.sync_copy(src, dst)

        if nh:
            pl.when(i == steps - 1)(half_finish)

    hbm = pl.BlockSpec(memory_space=pltpu.HBM)
    outs = pl.pallas_call(
        body, name=name, grid=(steps,),
        in_specs=[pl.BlockSpec((tb, a.shape[1]), lambda i: (i, 0)) for a in acts] + [hbm] * npar
        + [pl.BlockSpec((tb, c.shape[1]), lambda i: (i, 0)) for c in couts] + [hbm] * nh,
        out_specs=[pl.BlockSpec((tb, a.shape[1]), lambda i: (i, 0)) for a in acts] + [hbm] * (npar + nh),
        out_shape=[_sds(a.shape) for a in acts] + [_sds(p.shape) for p in params] + _sibling_halves_shapes(halves_of),
        scratch_shapes=[pltpu.VMEM(p.shape, p.dtype) for p in params] + [pltpu.VMEM(p.shape, F32) for p in params]
        + (_sibling_halves_sems(nh) if nh else []),
        compiler_params=_cparams(("arbitrary",)),
    )(*acts, *params, *couts, *halves_of)
    if nh:
        return outs[:na], outs[na:na + npar], outs[na + npar:]
    return outs[:na], outs[na:]


def _tiled_matmul(name, a, b, mode, grid, a_spec, b_spec, o_spec, out_shape):
    nk = grid[2]
    dims = {"nn": ((1,), (0,)), "nt": ((1,), (1,)), "tn": ((0,), (0,))}[mode]

    def body(a_ref, b_ref, o_ref, acc):
        kk = pl.program_id(2)

        @pl.when(kk == 0)
        def _():
            acc[...] = jnp.zeros_like(acc)

        acc[...] += lax.dot_general(a_ref[...].astype(BF16), b_ref[...].astype(BF16), (dims, ((), ())),
                                    preferred_element_type=F32)

        @pl.when(kk == nk - 1)
        def _():
            o_ref[...] = acc[...]

    return pl.pallas_call(
        body, name=name, grid=grid, in_specs=[a_spec, b_spec], out_specs=o_spec, out_shape=_sds(out_shape),
        scratch_shapes=[pltpu.VMEM(o_spec.block_shape, F32)],
        compiler_params=_cparams(("parallel", "parallel", "arbitrary")),
    )(a, b)


def _mlp_weight_grad(name, a, b, layer, layers, split, into=None, tile=512):
    t, m = a.shape
    n = b.shape[1]
    tile = 2 * tile
    tk = min(2 * tile, t)
    if split == "n":
        tm, tn = min(tile, m), min(tile, n // 4)
        per = n // 4 // tn
        shape = (4, layers, m, n // 4)
        o_idx = lambda i, j, k: (j // per, layer, i, j % per)
    else:
        tm, tn = min(tile, m // 4), min(tile, n)
        per = m // 4 // tm
        shape = (4, layers, m // 4, n)
        o_idx = lambda i, j, k: (i // per, layer, i % per, j)
    nk = t // tk

    def body(a_ref, b_ref, *rest):
        o_ref, acc = rest[-2:]
        kk = pl.program_id(2)

        @pl.when(kk == 0)
        def _():
            acc[...] = jnp.zeros_like(acc)

        acc[...] += lax.dot_general(a_ref[...].astype(BF16), b_ref[...].astype(BF16), (((0,), (0,)), ((), ())),
                                    preferred_element_type=F32)

        @pl.when(kk == nk - 1)
        def _():
            o_ref[...] = acc[...]

    in_specs = [pl.BlockSpec((tk, tm), lambda i, j, k: (k, i)), pl.BlockSpec((tk, tn), lambda i, j, k: (k, j))]
    operands = [a, b]
    aliases = {}
    if into is not None:
        in_specs.append(pl.BlockSpec(memory_space=pl.ANY))
        operands.append(into)
        aliases = {2: 0}
    return pl.pallas_call(
        body, name=name, grid=(m // tm, n // tn, nk), in_specs=in_specs,
        out_specs=pl.BlockSpec((None, None, tm, tn), o_idx), out_shape=_sds(shape), input_output_aliases=aliases,
        scratch_shapes=[pltpu.VMEM((tm, tn), F32)],
        compiler_params=_cparams(("parallel", "parallel", "arbitrary")),
    )(*operands)


S5_PACK = 8


def _s5_weight_grad(name, x, s, wide_rows, tk):
    t, c = x.shape
    wide = s.shape[1]
    kb, nb = S5_PACK * SSM_GROUP, S5_PACK * SSM_STATE
    nsb = c // kb
    x_spec = pl.BlockSpec((tk, kb), lambda i, j, k: (k, j % nsb))
    s_spec = pl.BlockSpec((tk, nb), lambda i, j, k: (k, j))
    if wide_rows:
        return _tiled_matmul(name, s, x, "tn", (1, wide // nb, t // tk), s_spec, x_spec,
                             pl.BlockSpec((nb, kb), lambda i, j, k: (j, 0)), (wide, kb))
    return _tiled_matmul(name, x, s, "tn", (1, wide // nb, t // tk), x_spec, s_spec,
                         pl.BlockSpec((kb, nb), lambda i, j, k: (0, j)), (kb, wide))


def _mlp_fwd(name, h, w1, w2, layer, ln_g, ln_b, tb, shards=()):
    t, c = h.shape
    nj, fc = w1.shape[0], w1.shape[3]
    nsh = len(shards)
    steps = (t // tb) * nj

    def body(h_ref, w1_ref, w2_ref, g_ref, b_ref, *rest):
        src, (out_ref, s_ref), dst = rest[:nsh], rest[nsh:nsh + 2], rest[nsh + 2:2 * nsh + 2]
        acc, sems = rest[2 * nsh + 2], rest[2 * nsh + 3:]
        j = pl.program_id(1)
        step = pl.program_id(0) * nj + j
        if nsh:
            start, forward, finish = _gather_big_phases(src, dst, sems)
            pl.when(step == 0)(start)

        @pl.when(j == 0)
        def _():
            acc[...] = jnp.zeros_like(acc)

        hid = jnp.dot(h_ref[...].astype(BF16), w1_ref[...].astype(BF16), preferred_element_type=F32)
        act = jnp.square(jnp.maximum(hid, 0.0))
        acc[...] += jnp.dot(act.astype(BF16), w2_ref[...].astype(BF16), preferred_element_type=F32)

        @pl.when(j == nj - 1)
        def _():
            s = DN_ALPHA * h_ref[...] + acc[...]
            s_ref[...] = s
            out_ref[...] = _ln(s, g_ref[...], b_ref[...])

        if nsh:
            pl.when(step == steps // 2)(forward)
            pl.when(step == steps - 1)(finish)

    row = pl.BlockSpec((tb, c), lambda i, j: (i, 0))
    vec = pl.BlockSpec((1, c), lambda i, j: (0, 0))
    outs = pl.pallas_call(
        body, name=name, grid=(t // tb, nj),
        in_specs=[row, pl.BlockSpec((None, None, c, fc), lambda i, j: (j, layer, 0, 0)),
                  pl.BlockSpec((None, None, fc, c), lambda i, j: (j, layer, 0, 0)), vec, vec] + [_HBM] * nsh,
        out_specs=[row, row] + [_HBM] * nsh,
        out_shape=[_sds((t, c)), _sds((t, c))] + [_sds((4,) + a.shape, a.dtype) for a in shards],
        scratch_shapes=[pltpu.VMEM((tb, c), F32)] + (_gather_big_sems(nsh) if nsh else []),
        compiler_params=_cparams(("arbitrary", "arbitrary")),
    )(h, w1, w2, ln_g, ln_b, *shards)
    return outs[0], outs[1], outs[2:]


def _mlp_bwd(name, h, s, dout, w1, w2, layer, ln_g, ln_b, tb):
    t, c = h.shape
    nj, fc = w1.shape[0], w1.shape[3]
    ff = nj * fc
    ni = t // tb
    nt = (((1,), (1,)), ((), ()))
    douts = dout if isinstance(dout, tuple) else (dout,)
    nd = len(douts)

    def body(h_ref, s_ref, *rest):
        dout_refs = rest[:nd]
        (w1_ref, w2_ref, g_ref, b_ref, dh_ref, ds_ref, dhid_ref, act_ref, dg_ref, db_ref,
         ds_scr, dh_acc, dg_acc, db_acc) = rest[nd:]
        i, j = pl.program_id(0), pl.program_id(1)

        @pl.when((i == 0) & (j == 0))
        def _():
            dg_acc[...] = jnp.zeros_like(dg_acc)
            db_acc[...] = jnp.zeros_like(db_acc)

        @pl.when(j == 0)
        def _():
            _, vjp = jax.vjp(_ln, s_ref[...], g_ref[...], b_ref[...])
            ds, dg, db = vjp(functools.reduce(jnp.add, [r[...] for r in dout_refs]))
            ds_scr[...] = ds
            ds_ref[...] = ds.astype(BF16)
            dh_acc[...] = DN_ALPHA * ds
            dg_acc[...] += dg
            db_acc[...] += db

        w1b, w2b = w1_ref[...].astype(BF16), w2_ref[...].astype(BF16)
        hid = jnp.dot(h_ref[...].astype(BF16), w1b, preferred_element_type=F32)
        rl = jnp.maximum(hid, 0.0)
        dact = lax.dot_general(ds_scr[...].astype(BF16), w2b, nt, preferred_element_type=F32)
        dhid = (dact * 2.0 * rl).astype(BF16)
        dh_acc[...] += lax.dot_general(dhid, w1b, nt, preferred_element_type=F32)
        dhid_ref[...] = dhid
        act_ref[...] = (rl * rl).astype(BF16)

        @pl.when(j == nj - 1)
        def _():
            dh_ref[...] = dh_acc[...]

        @pl.when((i == ni - 1) & (j == nj - 1))
        def _():
            dg_ref[...] = dg_acc[...]
            db_ref[...] = db_acc[...]

    row = pl.BlockSpec((tb, c), lambda i, j: (i, 0))
    vec = pl.BlockSpec((1, c), lambda i, j: (0, 0))
    wide = pl.BlockSpec((tb, fc), lambda i, j: (i, j))
    return pl.pallas_call(
        body, name=name, grid=(ni, nj),
        in_specs=[row, row] + [row] * nd + [pl.BlockSpec((None, None, c, fc), lambda i, j: (j, layer, 0, 0)),
                                            pl.BlockSpec((None, None, fc, c), lambda i, j: (j, layer, 0, 0)), vec, vec],
        out_specs=[row, row, wide, wide, vec, vec],
        out_shape=[_sds((t, c)), _sds((t, c), BF16), _sds((t, ff), BF16), _sds((t, ff), BF16), _sds((1, c)), _sds((1, c))],
        scratch_shapes=[pltpu.VMEM((tb, c), F32), pltpu.VMEM((tb, c), F32), pltpu.VMEM((1, c), F32), pltpu.VMEM((1, c), F32)],
        compiler_params=_cparams(("arbitrary", "arbitrary")),
    )(h, s, *douts, w1, w2, ln_g, ln_b)


def _load_heads(ref, hg):
    return jnp.stack([ref[:, h * HEAD:(h + 1) * HEAD] for h in range(hg)])


def _store_heads(ref, val):
    for h in range(val.shape[0]):
        ref[:, h * HEAD:(h + 1) * HEAD] = val[h]


def _rec_fwd(r, lw, k, v, a, b, hg, shards):
    t, c = r.shape
    n = HEAD
    nh = c // n
    ln = REC_CHUNK
    nck = t // ln
    ngrp = nh // hg
    nsh = len(shards)
    steps = ngrp * nck

    def body(r_ref, lw_ref, k_ref, v_ref, a_ref, b_ref, *rest):
        src, (o_ref, s0_ref), dst = rest[:nsh], rest[nsh:nsh + 2], rest[nsh + 2:2 * nsh + 2]
        state, sems = rest[2 * nsh + 2], rest[2 * nsh + 3:]
        step = pl.program_id(0) * nck + pl.program_id(1)
        start, forward, finish = _gather_big_phases(src, dst, sems)
        pl.when(step == 0)(start)

        @pl.when(pl.program_id(1) == 0)
        def _():
            state[...] = jnp.zeros_like(state)

        s0 = state[...]
        s0_ref[...] = s0
        o, s1 = _rec_chunk(s0, *(_load_heads(x, hg) for x in (r_ref, lw_ref, k_ref, v_ref, a_ref, b_ref)))
        _store_heads(o_ref, o)
        state[...] = s1
        pl.when(step == steps // 2)(forward)
        pl.when(step == steps - 1)(finish)

    seq = pl.BlockSpec((ln, hg * n), lambda g, i: (i, g))
    outs = pl.pallas_call(
        body, name="rec_fwd", grid=(ngrp, nck), in_specs=[seq] * 6 + [_HBM] * nsh,
        out_specs=[seq, pl.BlockSpec((None, hg, n, n), lambda g, i: (i, g, 0, 0))] + [_HBM] * nsh,
        out_shape=[_sds((t, c)), _sds((nck, nh, n, n))] + [_sds((4,) + s.shape, s.dtype) for s in shards],
        scratch_shapes=[pltpu.VMEM((hg, n, n), F32)] + _gather_big_sems(nsh),
        compiler_params=_cparams(("arbitrary", "arbitrary")),
    )(r, lw, k, v, a, b, *shards)
    return outs[0], outs[1], outs[2:]


def _rec_bwd(r, lw, k, v, a, b, s0s, do, hg, chip_sums):
    t, c = r.shape
    n = HEAD
    nh = c // n
    ln = REC_CHUNK
    nck = t // ln
    ngrp = nh // hg
    nsum = len(chip_sums)
    steps = ngrp * nck

    def body(r_ref, lw_ref, k_ref, v_ref, a_ref, b_ref, s0_ref, do_ref, *rest):
        src, grad_refs, land = rest[:nsum], rest[nsum:nsum + 6], rest[nsum + 6:2 * nsum + 6]
        dstate, sems = rest[2 * nsum + 6], rest[2 * nsum + 7:]
        step = pl.program_id(0) * nck + pl.program_id(1)
        start, finish = _scatter_big_phases(src, land, sems)
        pl.when(step == 0)(start)

        @pl.when(pl.program_id(1) == 0)
        def _():
            dstate[...] = jnp.zeros_like(dstate)

        _, vjp = jax.vjp(_rec_chunk, s0_ref[...], *(_load_heads(x, hg) for x in (r_ref, lw_ref, k_ref, v_ref, a_ref, b_ref)))
        ds0, *grads = vjp((_load_heads(do_ref, hg), dstate[...]))
        dstate[...] = ds0
        for ref, val in zip(grad_refs, grads):
            _store_heads(ref, val)
        pl.when(step == steps - 1)(finish)

    seq = pl.BlockSpec((ln, hg * n), lambda g, i: (nck - 1 - i, g))
    outs = pl.pallas_call(
        body, name="rec_bwd", grid=(ngrp, nck),
        in_specs=[seq] * 6 + [pl.BlockSpec((None, hg, n, n), lambda g, i: (nck - 1 - i, g, 0, 0)), seq] + [_HBM] * nsum,
        out_specs=[seq] * 6 + [_HBM] * nsum,
        out_shape=[_sds((t, c))] * 6 + [_sds((3,) + s.shape[1:], s.dtype) for s in chip_sums],
        scratch_shapes=[pltpu.VMEM((hg, n, n), F32)] + _scatter_big_sems(nsum),
        compiler_params=_cparams(("arbitrary", "arbitrary")),
    )(r, lw, k, v, a, b, s0s, do, *chip_sums)
    return outs[:6], outs[6:]


def _s5_blocks(c):
    kb, nb = S5_PACK * SSM_GROUP, S5_PACK * SSM_STATE
    return kb, nb, c // kb


def _s5_fwd(h, bc, abar, cc, tb, shards=()):
    t, c = h.shape
    w2 = bc.shape[1]
    w = w2 // 2
    kb, nb, nsb = _s5_blocks(c)

    nsh = len(shards)
    steps = t // tb

    def body(h_ref, bc_ref, a_ref, cc_ref, *rest):
        src, (s_ref, y_ref), dst = rest[:nsh], rest[nsh:nsh + 2], rest[nsh + 2:2 * nsh + 2]
        carry, rows, sems = rest[2 * nsh + 2], rest[2 * nsh + 3], rest[2 * nsh + 4:]
        if nsh:
            start, forward, finish = _gather_big_phases(src, dst, sems)
            pl.when(pl.program_id(0) == 0)(start)

        @pl.when(pl.program_id(0) == 0)
        def _():
            carry[...] = jnp.zeros_like(carry)

        for j in range(w2 // nb):
            ch = (j % nsb) * kb
            rows[:, j * nb:(j + 1) * nb] = jnp.dot(h_ref[:, ch:ch + kb].astype(BF16), bc_ref[:, j * nb:(j + 1) * nb],
                                                   preferred_element_type=F32)
        ar, ai = a_ref[:, :w], a_ref[:, w:]

        def step(i, state):
            hr, hi = state
            nr = ar * hr - ai * hi + rows[pl.ds(i, 1), :w]
            ni = ar * hi + ai * hr + rows[pl.ds(i, 1), w:]
            rows[pl.ds(i, 1), :w] = nr
            rows[pl.ds(i, 1), w:] = ni
            return nr, ni

        hr, hi = lax.fori_loop(0, tb, step, (carry[:, :w], carry[:, w:]))
        carry[:, :w] = hr
        carry[:, w:] = hi
        s_ref[...] = rows[...].astype(BF16)
        for j in range(nsb):
            re, im = j * nb, w + j * nb
            y_ref[:, j * kb:(j + 1) * kb] = (
                jnp.dot(s_ref[:, re:re + nb], cc_ref[re:re + nb, :], preferred_element_type=F32)
                + jnp.dot(s_ref[:, im:im + nb], cc_ref[im:im + nb, :], preferred_element_type=F32))
        if nsh:
            pl.when(pl.program_id(0) == steps // 2)(forward)
            pl.when(pl.program_id(0) == steps - 1)(finish)

    outs = pl.pallas_call(
        body, name="s5_fwd", grid=(steps,),
        in_specs=[pl.BlockSpec((tb, c), lambda i: (i, 0)), _full_spec(bc.shape), _full_spec(abar.shape), _full_spec(cc.shape)]
        + [_HBM] * nsh,
        out_specs=[pl.BlockSpec((tb, w2), lambda i: (i, 0)), pl.BlockSpec((tb, c), lambda i: (i, 0))] + [_HBM] * nsh,
        out_shape=[_sds((t, w2), BF16), _sds((t, c))] + [_sds((4,) + a.shape, a.dtype) for a in shards],
        scratch_shapes=[pltpu.VMEM((1, w2), F32), pltpu.VMEM((tb, w2), F32)] + (_gather_big_sems(nsh) if nsh else []),
        compiler_params=_cparams(("arbitrary",)),
    )(h, bc, abar, cc, *shards)
    return outs[0], outs[1], outs[2:]


def _s5_bwd(dy, s, abar, cc, bc, tb):
    t, c = dy.shape
    w2 = s.shape[1]
    w = w2 // 2
    kb, nb, nsb = _s5_blocks(c)
    nblk = t // tb
    pack = 16
    per = tb // pack
    nt = (((1,), (1,)), ((), ()))

    def body(dy_ref, s_ref, sprev_ref, a_ref, cc_ref, bc_ref, dbu_ref, dh_ref, da_ref, carry, da_acc, rows):
        i = pl.program_id(0)

        @pl.when(i == 0)
        def _():
            carry[...] = jnp.zeros_like(carry)
            da_acc[...] = jnp.zeros_like(da_acc)

        for j in range(w2 // nb):
            ch = (j % nsb) * kb
            rows[:, j * nb:(j + 1) * nb] = lax.dot_general(dy_ref[:, ch:ch + kb].astype(BF16), cc_ref[j * nb:(j + 1) * nb, :], nt,
                                                           preferred_element_type=F32)
        ar, ai = a_ref[:, :w], a_ref[:, w:]

        def step(n, state):
            gr, gi = state
            row = tb - 1 - n
            nr = rows[pl.ds(row, 1), :w] + ar * gr + ai * gi
            ni = rows[pl.ds(row, 1), w:] + ar * gi - ai * gr
            rows[pl.ds(row, 1), :w] = nr
            rows[pl.ds(row, 1), w:] = ni
            return nr, ni

        gr, gi = lax.fori_loop(0, tb, step, (carry[:, :w], carry[:, w:]))
        carry[:, :w] = gr
        carry[:, w:] = gi
        last = (lax.broadcasted_iota(jnp.int32, (pack, w2), 0) == pack - 1) & (i < nblk - 1)
        before = jnp.sum(jnp.where(last, sprev_ref[...].astype(F32), 0.0), axis=0, keepdims=True)
        rid = lax.broadcasted_iota(jnp.int32, (tb, w2), 0)
        sp = jnp.where(rid == 0, before, pltpu.roll(s_ref[...].astype(F32), 1, 0))
        g = rows[...]
        dbu_ref[...] = g.astype(BF16)
        spr, spi, g_r, g_i = sp[:, :w], sp[:, w:], g[:, :w], g[:, w:]
        da_acc[:, :w] += jnp.sum(spr * g_r + spi * g_i, axis=0, keepdims=True)
        da_acc[:, w:] += jnp.sum(spr * g_i - spi * g_r, axis=0, keepdims=True)
        for j in range(nsb):
            re, im = j * nb, w + j * nb
            dh_ref[:, j * kb:(j + 1) * kb] = (
                lax.dot_general(dbu_ref[:, re:re + nb], bc_ref[:, re:re + nb], nt, preferred_element_type=F32)
                + lax.dot_general(dbu_ref[:, im:im + nb], bc_ref[:, im:im + nb], nt, preferred_element_type=F32))

        @pl.when(i == nblk - 1)
        def _():
            da_ref[...] = da_acc[...]

    wide = pl.BlockSpec((tb, w2), lambda i: (nblk - 1 - i, 0))
    narrow = pl.BlockSpec((tb, c), lambda i: (nblk - 1 - i, 0))
    prev = pl.BlockSpec((pack, w2), lambda i: (jnp.maximum((nblk - 1 - i) * per - 1, 0), 0))
    return pl.pallas_call(
        body, name="s5_bwd", grid=(nblk,),
        in_specs=[narrow, wide, prev, _full_spec(abar.shape), _full_spec(cc.shape), _full_spec(bc.shape)],
        out_specs=[wide, narrow, pl.BlockSpec((1, w2), lambda i: (0, 0))],
        out_shape=[_sds((t, w2), BF16), _sds((t, c)), _sds((1, w2))],
        scratch_shapes=[pltpu.VMEM((1, w2), F32), pltpu.VMEM((1, w2), F32), pltpu.VMEM((tb, w2), F32)],
        compiler_params=_cparams(("arbitrary",)),
    )(dy, s, s, abar, cc, bc)


def _zoh_fwd(a_re, a_im, log_dt, b_re_t, b_im_t):
    def body(*refs):
        for r, val in zip(refs[5:], _f_zoh(*(x[...] for x in refs[:5]))):
            r[...] = val

    return pl.pallas_call(body, name="s5_zoh_fwd", out_shape=[_sds(a_re.shape)] * 2 + [_sds(b_re_t.shape)] * 2,
                          compiler_params=_cparams())(a_re, a_im, log_dt, b_re_t, b_im_t)


def _zoh_bwd(a_re, a_im, log_dt, b_re_t, b_im_t, couts):
    def body(*refs):
        _, vjp = jax.vjp(_f_zoh, *(x[...] for x in refs[:5]))
        for r, val in zip(refs[9:], vjp(tuple(x[...] for x in refs[5:9]))):
            r[...] = val

    ins = (a_re, a_im, log_dt, b_re_t, b_im_t)
    return pl.pallas_call(body, name="s5_zoh_bwd", out_shape=[_sds(x.shape) for x in ins],
                          compiler_params=_cparams())(*ins, *couts)


def _loss_head(h, target, tb):
    t, c = h.shape
    nb = t // tb

    def body(h_ref, t_ref, loss_ref, dh_ref, acc):
        i = pl.program_id(0)

        @pl.when(i == 0)
        def _():
            acc[...] = jnp.zeros_like(acc)

        d = h_ref[...] - t_ref[...]
        dh_ref[...] = d * (1.0 / c)
        acc[...] += 0.5 * jnp.sum(jnp.mean(d * d, axis=-1, keepdims=True), axis=0, keepdims=True)

        @pl.when(i == nb - 1)
        def _():
            loss_ref[...] = jnp.broadcast_to(acc[...], loss_ref.shape)

    row = pl.BlockSpec((tb, c), lambda i: (i, 0))
    return pl.pallas_call(
        body, name="loss_head", grid=(nb,), in_specs=[row, row],
        out_specs=[pl.BlockSpec((8, 128), lambda i: (0, 0)), row], out_shape=[_sds((8, 128)), _sds((t, c))],
        scratch_shapes=[pltpu.VMEM((1, 1), F32)], compiler_params=_cparams(("arbitrary",)),
    )(h, target)


def _rows_tile(rows):
    for cand in (512, 256, 128, 64, 32, 16, 8):
        if rows % cand == 0:
            return cand
    return rows


def _grad_x(here, from_next):
    rows, cols = here[0].shape
    tb = _rows_tile(rows)
    nb = rows // tb
    nh, nn = len(here), len(from_next)
    sub = 8

    def body(*refs):
        i = pl.program_id(0)
        total = functools.reduce(jnp.add, [r[...] for r in refs[:nh]])
        shifted = functools.reduce(jnp.add, [r[...] for r in refs[nh:nh + nn]])
        first_next = functools.reduce(jnp.add, [r[0:1, :] for r in refs[nh + nn:nh + 2 * nn]])
        first_next = jnp.where(i == nb - 1, 0.0, first_next)
        rid = lax.broadcasted_iota(jnp.int32, (tb, cols), 0)
        refs[-1][...] = total + jnp.where(rid == tb - 1, first_next, pltpu.roll(shifted, tb - 1, 0))

    blk = pl.BlockSpec((tb, cols), lambda i: (i, 0))
    nxt = pl.BlockSpec((sub, cols), lambda i: (jnp.minimum(i + 1, nb - 1) * (tb // sub), 0))
    return pl.pallas_call(body, name="grad_x", grid=(nb,), in_specs=[blk] * (nh + nn) + [nxt] * nn, out_specs=blk,
                          out_shape=_sds((rows, cols)), compiler_params=_cparams(("parallel",)))(*here, *from_next, *from_next)


def _adamw_math(w, g, m, v):
    m = ADAM_B1 * m + (1.0 - ADAM_B1) * g
    v = ADAM_B2 * v + (1.0 - ADAM_B2) * jnp.square(g)
    m_hat = m / (1.0 - ADAM_B1 ** ADAM_STEP)
    v_hat = v / (1.0 - ADAM_B2 ** ADAM_STEP)
    delta = -ADAM_LR * (m_hat / (jnp.sqrt(v_hat) + ADAM_EPS) + ADAM_WD * w)
    return delta, m, v


def _adamw(name, parts, w, m, v):
    rows, cols = w.shape
    tb = _rows_tile(rows)
    npart = len(parts)

    def body(*refs):
        g = refs[0][...]
        for r in refs[1:npart]:
            g = g + r[...]
        w_ref, m_ref, v_ref = refs[npart:npart + 3]
        g_out, d_out, m_out, v_out = refs[npart + 3:]
        delta, mn, vn = _adamw_math(w_ref[...], g, m_ref[...], v_ref[...])
        g_out[...] = g
        d_out[...] = delta
        m_out[...] = mn
        v_out[...] = vn

    blk = pl.BlockSpec((tb, cols), lambda i: (i, 0))
    return pl.pallas_call(body, name=name, grid=(rows // tb,), in_specs=[blk] * (npart + 3), out_specs=[blk] * 4,
                          out_shape=[_sds((rows, cols))] * 4, compiler_params=_cparams(("parallel",)))(*parts, w, m, v)


def _shift_down(a):
    return jnp.concatenate([jnp.zeros_like(a[:1]), a[:-1]], axis=0)


def _s5_pack_mask(g):
    return (jnp.arange(g)[None, :] % S5_PACK == jnp.arange(S5_PACK)[:, None]).astype(F32)


def _compact_b(bbar_t):
    g, s, p = bbar_t.shape
    return (_s5_pack_mask(g)[:, None, :, None] * bbar_t.transpose(1, 0, 2)[None]).reshape(S5_PACK * s, g * p)


def _compact_b_t(dense, g):
    s, p = dense.shape[0] // S5_PACK, dense.shape[1] // g
    return jnp.sum(dense.reshape(S5_PACK, s, g, p) * _s5_pack_mask(g)[:, None, :, None], axis=0).transpose(1, 0, 2)


def _compact_c(c_w):
    g, s, p = c_w.shape
    return (c_w.transpose(0, 2, 1)[:, :, None, :] * _s5_pack_mask(g).T[:, None, :, None]).reshape(g * p, S5_PACK * s)


def _compact_c_t(dense, g):
    p, s = dense.shape[0] // g, dense.shape[1] // S5_PACK
    return jnp.sum(dense.reshape(g, p, S5_PACK, s) * _s5_pack_mask(g).T[:, None, :, None], axis=2).transpose(0, 2, 1)


def _local_step(x, target, fw, core):
    t, c = x.shape
    nh = c // HEAD
    ng = c // SSM_GROUP
    tb = min(256, t)
    tbm = min(512, t)
    tbmf = min(1024, t)
    tbmb = min(512, t)
    tbs = min(256, t)
    tk5 = min(4096, t)
    hg = min(16, nh)
    mu = [fw['rw_mu'][i:i + 1] for i in range(6)]
    ln_g = [fw['ln_g'][i:i + 1] for i in range(4)]
    ln_b = [fw['ln_b'][i:i + 1] for i in range(4)]
    grads = {}

    xp = _shift_down(x)
    proj_params = {n: (mu[i], fw['rw_w' + n]) for n, i in (('r', 0), ('k', 2), ('v', 3))}
    late_names, late_shards, late_axes = fw['small_late']
    (r_raw,), late_fulls = _stage_fwd("proj_r", _f_proj, (x, xp), proj_params['r'], (c,), tbm, late_shards, late_axes)
    fw = dict(fw, **dict(zip(late_names, late_fulls)))
    raw = {'r': r_raw, **{n: _stage_fwd("proj_" + n, _f_proj, (x, xp), proj_params[n], (c,), tbm)[0] for n in 'kv'}}
    lora_params = (mu[1], mu[4], mu[5], fw['rw_w0'], fw['rw_w1'], fw['rw_w2'], fw['rw_a0'], fw['rw_a1'], fw['rw_a2'],
                   fw['rw_g1'], fw['rw_g2'], fw['rw_k_k'], fw['rw_k_a'])
    lw, k2, an, bb, gate = _stage_fwd("lora", _f_lora, (x, xp, raw['k']), lora_params, (c,) * 5, tbm)
    rec_in = (raw['r'], lw, k2, raw['v'], an, bb)
    o, s0s, (wo_view, w1_l0, w2_l0) = _rec_fwd(*rec_in, hg, fw['late_a'])
    fw = dict(fw, rw_wo=wo_view.reshape(c, c))
    mlp_w = [(w1_l0.reshape(4, 1, c, -1), w2_l0.reshape(4, 1, -1, c)), None]
    post_params = (fw['rw_lnx_g'], fw['rw_lnx_b'], fw['rw_r_k'], fw['rw_wo'], ln_g[0], ln_b[0])
    post_acts = (o, raw['r'], k2, raw['v'], gate, x)
    h1, = _stage_fwd("post", _f_post, post_acts, post_params, (c,), tb)
    h2, s_mlp0, (w1_l1,) = _mlp_fwd("mlp0_fwd", h1, *mlp_w[0], 0, ln_g[1], ln_b[1], tbmf, fw['late_b'])

    a_re, a_im, log_dt = fw['s5_a_re'], fw['s5_a_im'], fw['s5_log_dt']
    b_re_t, b_im_t = fw['s5_b_re'].transpose(0, 2, 1), fw['s5_b_im'].transpose(0, 2, 1)
    abar_re, abar_im, bbar_re_t, bbar_im_t = _zoh_fwd(a_re, a_im, log_dt, b_re_t, b_im_t)
    abar = jnp.concatenate([abar_re.reshape(1, -1), abar_im.reshape(1, -1)], axis=1)
    bc = jnp.concatenate([_compact_b(bbar_re_t), _compact_b(bbar_im_t)], axis=1).astype(BF16)
    cc = jnp.concatenate([_compact_c(fw['s5_c_re']), -_compact_c(fw['s5_c_im'])], axis=0).astype(BF16)
    st, ys, (glu_view, w2_l1) = _s5_fwd(h2, bc, abar, cc, tbs, fw['late_c'])
    mlp_w[1] = (w1_l1.reshape(4, 1, c, -1), w2_l1.reshape(4, 1, -1, c))
    fw = dict(fw, s5_w_glu=tuple(glu_view[q] for q in range(4)))
    glu_params = (fw['s5_d'], *fw['s5_w_glu'], ln_g[2], ln_b[2])
    h3, = _stage_fwd("glu", _f_glu, (ys, h2), glu_params, (c,), tbm)
    h4, s_mlp1, _ = _mlp_fwd("mlp1_fwd", h3, *mlp_w[1], 0, ln_g[3], ln_b[3], tbmf)

    loss_blk, dh4 = _loss_head(h4, target, tb)

    dln_g, dln_b = [None] * 4, [None] * 4
    dh3, ds1, dhid1, act1, dln_g[3], dln_b[3] = _mlp_bwd("mlp1_bwd", h3, s_mlp1, dh4, *mlp_w[1], 0,
                                                         ln_g[3], ln_b[3], tbmb)
    dw1 = _mlp_weight_grad("mlp1_dw1", h3, dhid1, 1, DEPTH, "n")
    dw2 = _mlp_weight_grad("mlp1_dw2", act1, ds1, 1, DEPTH, "m")
    (dys, dh2_glu), (grads['s5_d'], *dglu, dln_g[2], dln_b[2]) = _stage_bwd(
        "glu_bwd", _f_glu, (ys, h2), glu_params, (dh3,), tbm, proxied=(1, 2, 3, 4))
    grads['s5_w_glu'] = jnp.stack(dglu)
    dcc = _s5_weight_grad("s5_dcc", dys, st, True, tk5)
    dbu, dh2_bu, dabar = _s5_bwd(dys, st, abar, cc, bc, tbs)
    dbc = _s5_weight_grad("s5_dbc", h2, dbu, False, tk5)
    gp = ng * SSM_STATE
    grads['s5_c_re'] = _compact_c_t(dcc[:gp], ng)
    grads['s5_c_im'] = -_compact_c_t(dcc[gp:], ng)
    zoh_couts = (dabar[:, :gp].reshape(ng, SSM_STATE), dabar[:, gp:].reshape(ng, SSM_STATE),
                 _compact_b_t(dbc[:, :gp], ng), _compact_b_t(dbc[:, gp:], ng))
    grads['s5_a_re'], grads['s5_a_im'], grads['s5_log_dt'], db_re_t, db_im_t = _zoh_bwd(
        a_re, a_im, log_dt, b_re_t, b_im_t, zoh_couts)
    grads['s5_b_re'], grads['s5_b_im'] = db_re_t.transpose(0, 2, 1), db_im_t.transpose(0, 2, 1)
    dh2 = (dh2_glu, dh2_bu)

    dh1, ds0, dhid0, act0, dln_g[1], dln_b[1] = _mlp_bwd("mlp0_bwd", h1, s_mlp0, dh2, *mlp_w[0], 0,
                                                         ln_g[1], ln_b[1], tbmb)
    grads['mlp_w1'] = _mlp_weight_grad("mlp0_dw1", h1, dhid0, 0, DEPTH, "n", into=dw1)
    grads['mlp_w2'] = _mlp_weight_grad("mlp0_dw2", act0, ds0, 0, DEPTH, "m", into=dw2)
    ready_views = [grads[n].reshape(4, -1, grads[n].shape[-1]) for n in BIG_READY]
    (do, dr_p, dk2_p, dv_p, dgate, dx_post), post_g, ready_others = _stage_bwd(
        "post_bwd", _f_post, post_acts, post_params, (dh1,), tb, proxied=(3,), halves_of=ready_views)
    grads['rw_lnx_g'], grads['rw_lnx_b'], grads['rw_r_k'], grads['rw_wo'], dln_g[0], dln_b[0] = post_g
    ready_sums = [_half_add(f"half_add_a{i}", v, o, core) for i, (v, o) in enumerate(zip(ready_views, ready_others))]
    rec_g, ready_lands = _rec_bwd(*rec_in, s0s, do, hg, ready_sums)
    reduced = dict(zip(BIG_READY, zip(ready_sums, ready_lands)))
    dr_r, dlw, dk2_r, dv_r, dan, dbb = rec_g
    dk2 = (dk2_p, dk2_r)
    (dx_l, dxp_l, dkraw_l), lora_g = _stage_bwd("lora_bwd", _f_lora, (x, xp, raw['k']), lora_params,
                                                (dlw, dk2, dan, dbb, dgate), tb)
    (dmu_w, dmu_a, dmu_g, grads['rw_w0'], grads['rw_w1'], grads['rw_w2'], grads['rw_a0'], grads['rw_a1'], grads['rw_a2'],
     grads['rw_g1'], grads['rw_g2'], grads['rw_k_k'], grads['rw_k_a']) = lora_g
    dproj = {'r': (dr_p, dr_r), 'k': dkraw_l, 'v': (dv_p, dv_r)}
    dxs, dxps, dmu = [dx_post, dx_l], [dxp_l], {}
    for n in 'rkv':
        (dx_n, dxp_n), (dmu[n], grads['rw_w' + n]) = _stage_bwd("proj_bwd_" + n, _f_proj, (x, xp), proj_params[n],
                                                                 (dproj[n],), tbm, proxied=(1,))
        dxs.append(dx_n)
        dxps.append(dxp_n)
    grads['rw_mu'] = jnp.concatenate([dmu['r'], dmu_w, dmu['k'], dmu['v'], dmu_a, dmu_g], axis=0)
    grads['ln_g'] = jnp.concatenate(dln_g, axis=0)
    grads['ln_b'] = jnp.concatenate(dln_b, axis=0)
    grad_x = _grad_x(dxs, dxps)
    late = [n for n in BIG if n not in BIG_READY]
    late_sums = dict(zip(late, _chip_sums("b", [grads[n] for n in late], core)))
    return loss_blk, grad_x, grads, reduced, late_sums


def _position():
    return lax.axis_index("x"), lax.axis_index("y"), lax.axis_index("c")


def _other_chips(x, y):
    return [(1 - x, y), (x, 1 - y), (1 - x, 1 - y)]


def _chip_slice(ref, axis, q, size):
    idx = [slice(None)] * len(ref.shape)
    idx[axis] = pl.ds(pl.multiple_of(q * size, size), size)
    return ref.at[tuple(idx)]


_HBM = pl.BlockSpec(memory_space=pltpu.HBM)


def _gather_small_phases(src, dst, axes, sems):
    n = len(src)
    send_sems, recv_sems, own_sems = sems
    x, y, c = _position()
    chips = _other_chips(x, y)
    sizes = [src[a].shape[axes[a]] for a in range(n)]

    def copy(a, k, q):
        return pltpu.make_async_remote_copy(
            src_ref=src[a], dst_ref=_chip_slice(dst[a], axes[a], q, sizes[a]), send_sem=send_sems.at[a, k],
            recv_sem=recv_sems.at[a, k], device_id=(*chips[k], c), device_id_type=MESH)

    def own(a):
        return pltpu.make_async_copy(src[a], _chip_slice(dst[a], axes[a], 2 * x + y, sizes[a]), own_sems.at[a])

    def start():
        for a in range(n):
            own(a).start()
            for k in range(3):
                copy(a, k, 2 * x + y).start()

    def finish():
        for a in range(n):
            for k, (cx, cy) in enumerate(chips):
                copy(a, k, 2 * cx + cy).wait_recv()
        for a in range(n):
            for k in range(3):
                copy(a, k, 2 * x + y).wait_send()
            own(a).wait()

    return start, finish


def _gather_early(big, small, axes):
    nb, ns = len(big), len(small)
    full_shapes = [tuple(s * 4 if i == ax else s for i, s in enumerate(a.shape)) for a, ax in zip(small, axes)]

    def body(*refs):
        src_b, src_s = refs[:nb], refs[nb:nb + ns]
        dst_b, dst_s = refs[nb + ns:2 * nb + ns], refs[2 * nb + ns:2 * (nb + ns)]
        sems = refs[2 * (nb + ns):]
        small_start, small_finish = _gather_small_phases(src_s, dst_s, axes, sems[5:])
        small_start()
        for phase in _gather_big_phases(src_b, dst_b, sems[:5]):
            phase()
        small_finish()

    outs = pl.pallas_call(
        body, name="gather_early", in_specs=[_HBM] * (nb + ns), out_specs=[_HBM] * (nb + ns),
        out_shape=[_sds((4,) + a.shape, a.dtype) for a in big] + [_sds(s, a.dtype) for s, a in zip(full_shapes, small)],
        scratch_shapes=_gather_big_sems(nb) + [pltpu.SemaphoreType.DMA((ns, 3)), pltpu.SemaphoreType.DMA((ns, 3)),
                                               pltpu.SemaphoreType.DMA((ns,))],
        compiler_params=_cparams(),
    )(*big, *small)
    return outs[:nb], outs[nb:]


def _scatter_pieces(fulls, axes, sums):
    n, nsum = len(fulls), len(sums)
    sizes = [a.shape[ax] // 4 for a, ax in zip(fulls, axes)]
    shard_shapes = [tuple(sz if i == ax else s for i, s in enumerate(a.shape)) for a, ax, sz in zip(fulls, axes, sizes)]

    def body(*refs):
        src, big_src = refs[:n], refs[n:n + nsum]
        land, big_land = refs[n + nsum:2 * n + nsum], refs[2 * n + nsum:2 * (n + nsum)]
        send_sems, recv_sems = refs[2 * (n + nsum):2 * (n + nsum) + 2]
        big_start, big_finish = _scatter_big_phases(big_src, big_land, refs[2 * (n + nsum) + 2:])
        big_start()
        x, y, c = _position()
        chips = _other_chips(x, y)

        def copy(a, k):
            cx, cy = chips[k]
            return pltpu.make_async_remote_copy(
                src_ref=_chip_slice(src[a], axes[a], 2 * cx + cy, sizes[a]), dst_ref=land[a].at[k],
                send_sem=send_sems.at[a, k], recv_sem=recv_sems.at[a, k], device_id=(cx, cy, c), device_id_type=MESH)

        for a in range(n):
            for k in range(3):
                copy(a, k).start()
        for a in range(n):
            for k in range(3):
                copy(a, k).wait_recv()
        for a in range(n):
            for k in range(3):
                copy(a, k).wait_send()
        big_finish()

    outs = pl.pallas_call(
        body, name="scatter_grads", in_specs=[_HBM] * (n + nsum), out_specs=[_HBM] * (n + nsum),
        out_shape=[_sds((3,) + s) for s in shard_shapes] + [_sds((3,) + s.shape[1:], s.dtype) for s in sums],
        scratch_shapes=[pltpu.SemaphoreType.DMA((n, 3)), pltpu.SemaphoreType.DMA((n, 3))] + _scatter_big_sems(nsum),
        compiler_params=_cparams(),
    )(*fulls, *sums)
    return outs[:n], outs[n:]


def _sibling_swap(name, arrs):
    n = len(arrs)

    def body(*refs):
        src, dst = refs[:n], refs[n:2 * n]
        send_sems, recv_sems = refs[2 * n:]
        x, y, c = _position()
        copies = [pltpu.make_async_remote_copy(src_ref=src[a], dst_ref=dst[a], send_sem=send_sems.at[a], recv_sem=recv_sems.at[a],
                                               device_id=(x, y, 1 - c), device_id_type=MESH) for a in range(n)]
        for cp in copies:
            cp.start()
        for cp in copies:
            cp.wait_recv()
        for cp in copies:
            cp.wait_send()

    return pl.pallas_call(
        body, name=name, in_specs=[_HBM] * n, out_specs=[_HBM] * n, out_shape=[_sds(a.shape) for a in arrs],
        scratch_shapes=[pltpu.SemaphoreType.DMA((n,)), pltpu.SemaphoreType.DMA((n,))],
        compiler_params=_cparams(),
    )(*arrs)


def _sum4(name, own, land):
    rows, cols = own.shape
    tb = _rows_tile(rows)

    def body(o_ref, l0, l1, l2, out_ref):
        out_ref[...] = ((o_ref[...] + l0[...]) + l1[...]) + l2[...]

    blk = pl.BlockSpec((tb, cols), lambda i: (i, 0))
    lands = [pl.BlockSpec((None, tb, cols), functools.partial(lambda k, i: (k, i, 0), k)) for k in range(3)]
    return pl.pallas_call(body, name=name, grid=(rows // tb,), in_specs=[blk] + lands, out_specs=blk,
                          out_shape=_sds((rows, cols)), compiler_params=_cparams(("parallel",)))(own, land, land, land)


def _allreduce_adamw_small(g, w, m, v):
    rows, lanes = g.shape

    def body(g_ref, w_ref, m_ref, v_ref, gs_ref, d_ref, mn_ref, vn_ref, land, send_sems, recv_sems):
        x, y, c = _position()
        me = 4 * x + 2 * y + c
        chips = _other_chips(x, y)
        sibling = (x, y, 1 - c)

        def slot_of(cx, cy, cc):
            return 4 * cx + 2 * cy + cc

        def copy(j, slot, to, src=None):
            return pltpu.make_async_remote_copy(src_ref=g_ref if src is None else land.at[src], dst_ref=land.at[slot],
                                                send_sem=send_sems.at[j], recv_sem=recv_sems.at[j], device_id=to, device_id_type=MESH)

        copy(0, me, sibling).start()
        for k, chip in enumerate(chips):
            copy(1 + k, me, (*chip, c)).start()
        land[me] = g_ref[...]
        for k, chip in enumerate(chips):
            theirs = slot_of(*chip, c)
            copy(1 + k, theirs, (*chip, c)).wait_recv()
            copy(4 + k, theirs, sibling, src=theirs).start()
        copy(0, slot_of(*sibling), sibling).wait_recv()
        for k, chip in enumerate(chips):
            copy(4 + k, slot_of(*chip, 1 - c), sibling).wait_recv()
        copy(0, me, sibling).wait_send()
        for k, chip in enumerate(chips):
            copy(1 + k, me, (*chip, c)).wait_send()
            copy(4 + k, slot_of(*chip, c), sibling, src=slot_of(*chip, c)).wait_send()
        total = land[0]
        for dev in range(1, 8):
            total = total + land[dev]
        delta, mn, vn = _adamw_math(w_ref[...], total, m_ref[...], v_ref[...])
        gs_ref[...] = total
        d_ref[...] = delta
        mn_ref[...] = mn
        vn_ref[...] = vn

    vmem = pl.BlockSpec(memory_space=pltpu.VMEM)
    return pl.pallas_call(
        body, name="allreduce_adamw_small", in_specs=[vmem] * 4, out_specs=[vmem] * 4, out_shape=[_sds((rows, lanes))] * 4,
        scratch_shapes=[pltpu.VMEM((8, rows, lanes), F32), pltpu.SemaphoreType.DMA((7,)), pltpu.SemaphoreType.DMA((7,))],
        compiler_params=_cparams(),
    )(g, w, m, v)


def _row_half(ref, c):
    r2 = ref.shape[-2] // 2
    lead = (slice(None),) * (len(ref.shape) - 2)
    return ref.at[(*lead, pl.ds(pl.multiple_of(c * r2, r2), r2), slice(None))]


def _gather_big_phases(src, dst, sems):
    n = len(src)
    ici_send, ici_recv, d2d_send, d2d_recv, own_sems = sems
    x, y, c = _position()
    me = 2 * x + y
    chips = _other_chips(x, y)
    ids = [2 * cx + cy for cx, cy in chips]

    def ici(a, k, q):
        return pltpu.make_async_remote_copy(
            src_ref=_row_half(src[a], c), dst_ref=_row_half(dst[a].at[q], c), send_sem=ici_send.at[a, k],
            recv_sem=ici_recv.at[a, k], device_id=(*chips[k], c), device_id_type=MESH)

    def d2d(a, k, half):
        where = _row_half(dst[a].at[ids[k]], half)
        return pltpu.make_async_remote_copy(src_ref=where, dst_ref=where, send_sem=d2d_send.at[a, k], recv_sem=d2d_recv.at[a, k],
                                            device_id=(x, y, 1 - c), device_id_type=MESH)

    def own(a):
        return pltpu.make_async_copy(src[a], dst[a].at[me], own_sems.at[a])

    def start():
        for a in range(n):
            own(a).start()
            for k in range(3):
                ici(a, k, me).start()

    def forward():
        for a in range(n):
            for k in range(3):
                ici(a, k, ids[k]).wait_recv()
                d2d(a, k, c).start()

    def finish():
        for a in range(n):
            for k in range(3):
                d2d(a, k, 1 - c).wait_recv()
        for a in range(n):
            for k in range(3):
                ici(a, k, me).wait_send()
                d2d(a, k, c).wait_send()
            own(a).wait()

    return start, forward, finish


def _gather_big_sems(n):
    return [pltpu.SemaphoreType.DMA((n, 3))] * 4 + [pltpu.SemaphoreType.DMA((n,))]


def _chip_sums(tag, grads, core):
    views = [g.reshape(4, -1, g.shape[-1]) for g in grads]
    others = _sibling_halves("sibling_halves_" + tag, views)
    return [_half_add(f"half_add_{tag}{i}", v, o, core) for i, (v, o) in enumerate(zip(views, others))]


def _sibling_halves_phases(src, dst, sems):
    n = len(src)
    send_sems, recv_sems = sems
    x, y, c = _position()

    def copy(a):
        return pltpu.make_async_remote_copy(src_ref=_row_half(src[a], 1 - c), dst_ref=dst[a], send_sem=send_sems.at[a],
                                            recv_sem=recv_sems.at[a], device_id=(x, y, 1 - c), device_id_type=MESH)

    def start():
        for a in range(n):
            copy(a).start()

    def finish():
        for a in range(n):
            copy(a).wait_recv()
        for a in range(n):
            copy(a).wait_send()

    return start, finish


def _sibling_halves_sems(n):
    return [pltpu.SemaphoreType.DMA((n,)), pltpu.SemaphoreType.DMA((n,))]


def _sibling_halves_shapes(views):
    return [_sds((4, v.shape[1] // 2, v.shape[2])) for v in views]


def _sibling_halves(name, views):
    n = len(views)

    def body(*refs):
        for phase in _sibling_halves_phases(refs[:n], refs[n:2 * n], refs[2 * n:]):
            phase()

    return pl.pallas_call(
        body, name=name, in_specs=[_HBM] * n, out_specs=[_HBM] * n, out_shape=_sibling_halves_shapes(views),
        scratch_shapes=_sibling_halves_sems(n), compiler_params=_cparams(),
    )(*views)


def _rows_tile_capped(rows, cap=256):
    return min(_rows_tile(rows), cap)


def _half_add(name, view, other, core):
    _, r, k = view.shape
    r2 = r // 2
    tr = _rows_tile_capped(r2, 512)
    per = r2 // tr

    def body(c_ref, v_ref, o_ref, out_ref):
        out_ref[...] = (v_ref[...] + o_ref[...]).astype(BF16)

    blk = pl.BlockSpec((None, tr, k), lambda q, i, c: (q, i, 0))
    return pl.pallas_call(
        body, name=name,
        grid_spec=pltpu.PrefetchScalarGridSpec(
            num_scalar_prefetch=1, grid=(4, per),
            in_specs=[pl.BlockSpec((None, tr, k), lambda q, i, c: (q, c[0] * per + i, 0)), blk], out_specs=blk),
        out_shape=_sds((4, r2, k), BF16), compiler_params=_cparams(("parallel", "parallel")),
    )(core, view, other)


def _scatter_big_phases(src, land, sems):
    n = len(src)
    send_sems, recv_sems = sems
    x, y, c = _position()
    chips = _other_chips(x, y)

    def copy(a, k):
        cx, cy = chips[k]
        return pltpu.make_async_remote_copy(src_ref=src[a].at[2 * cx + cy], dst_ref=land[a].at[k], send_sem=send_sems.at[a, k],
                                            recv_sem=recv_sems.at[a, k], device_id=(cx, cy, c), device_id_type=MESH)

    def start():
        for a in range(n):
            for k in range(3):
                copy(a, k).start()

    def finish():
        for a in range(n):
            for k in range(3):
                copy(a, k).wait_recv()
        for a in range(n):
            for k in range(3):
                copy(a, k).wait_send()

    return start, finish


def _scatter_big_sems(n):
    return [pltpu.SemaphoreType.DMA((n, 3)), pltpu.SemaphoreType.DMA((n, 3))]


def _sum4_big(name, sums, land, chip):
    _, r2, k = sums.shape
    tr = _rows_tile_capped(r2, 512)

    def body(q_ref, s_ref, l0, l1, l2, out_ref):
        out_ref[...] = ((s_ref[...].astype(F32) + l0[...].astype(F32)) + l1[...].astype(F32)) + l2[...].astype(F32)

    lands = [pl.BlockSpec((None, tr, k), functools.partial(lambda j, i, q: (j, i, 0), j)) for j in range(3)]
    return pl.pallas_call(
        body, name=name,
        grid_spec=pltpu.PrefetchScalarGridSpec(
            num_scalar_prefetch=1, grid=(r2 // tr,),
            in_specs=[pl.BlockSpec((None, tr, k), lambda i, q: (q[0], i, 0))] + lands,
            out_specs=pl.BlockSpec((tr, k), lambda i, q: (i, 0))),
        out_shape=_sds((r2, k)), compiler_params=_cparams(("parallel",)),
    )(chip, sums, land, land, land)


def _adamw_halves(name, mine, theirs, w, m, v, core):
    r, k = w.shape
    r2 = r // 2
    tr = _rows_tile_capped(r2, 512)
    per = r2 // tr

    def body(c_ref, mine_ref, theirs_ref, w_ref, m_ref, v_ref, g_out, d_out, m_out, v_out):
        g = jnp.where(pl.program_id(0) == c_ref[0], mine_ref[...], theirs_ref[...])
        delta, mn, vn = _adamw_math(w_ref[...], g, m_ref[...], v_ref[...])
        g_out[...] = g
        d_out[...] = delta
        m_out[...] = mn
        v_out[...] = vn

    half = pl.BlockSpec((tr, k), lambda h, i, c: (i, 0))
    full = pl.BlockSpec((tr, k), lambda h, i, c: (h * per + i, 0))
    return pl.pallas_call(
        body, name=name,
        grid_spec=pltpu.PrefetchScalarGridSpec(num_scalar_prefetch=1, grid=(2, per), in_specs=[half, half, full, full, full],
                                               out_specs=[full] * 4),
        out_shape=[_sds((r, k))] * 4, compiler_params=_cparams(("parallel", "parallel")),
    )(core, mine, theirs, w, m, v)


def _drops_layer_axis(name):
    return not (name.startswith('mlp') or name == 's5_d')


def _work(name, arr):
    return arr.reshape(arr.shape[1:]) if _drops_layer_axis(name) else arr


def _work_axis(name):
    return SHARD_AXIS[name] - (1 if _drops_layer_axis(name) else 0)


def _as2d(a):
    return a.reshape(-1, a.shape[-1])


def _replicated_2d(name, arr):
    if name in ('ln_g', 'ln_b'):
        return arr
    if name == 'rw_r_k':
        return arr.reshape(1, -1)
    if name == 's5_log_dt':
        return arr.reshape(-1, 1)
    if name.startswith('s5_'):
        return arr.reshape(arr.shape[1:])
    return arr


def _pack(arrs):
    flat = []
    for a in arrs:
        f = a.reshape(-1)
        flat.append(jnp.pad(f, (0, -f.shape[0] % 128)))
    f = jnp.concatenate(flat)
    f = jnp.pad(f, (0, -f.shape[0] % 1024))
    return f.reshape(-1, 128)


def _unpack(packed, shapes):
    flat = packed.reshape(-1)
    out, at = [], 0
    for s in shapes:
        size = math.prod(s)
        out.append(flat[at:at + size].reshape(s))
        at += size + (-size % 128)
    return out


def kernel(x, ln_g, ln_b, rw_mu, rw_w0, rw_w1, rw_w2, rw_a0, rw_a1, rw_a2, rw_g1, rw_g2, rw_k_k, rw_k_a, rw_r_k, rw_wr, rw_wk, rw_wv, rw_wo, rw_lnx_g, rw_lnx_b, s5_a_re, s5_a_im, s5_log_dt, s5_b_re, s5_b_im, s5_c_re, s5_c_im, s5_d, s5_w_glu, mlp_w1, mlp_w2, loss_target, m_ln_g, m_ln_b, m_rw_mu, m_rw_w0, m_rw_w1, m_rw_w2, m_rw_a0, m_rw_a1, m_rw_a2, m_rw_g1, m_rw_g2, m_rw_k_k, m_rw_k_a, m_rw_r_k, m_rw_wr, m_rw_wk, m_rw_wv, m_rw_wo, m_rw_lnx_g, m_rw_lnx_b, m_s5_a_re, m_s5_a_im, m_s5_log_dt, m_s5_b_re, m_s5_b_im, m_s5_c_re, m_s5_c_im, m_s5_d, m_s5_w_glu, m_mlp_w1, m_mlp_w2, v_ln_g, v_ln_b, v_rw_mu, v_rw_w0, v_rw_w1, v_rw_w2, v_rw_a0, v_rw_a1, v_rw_a2, v_rw_g1, v_rw_g2, v_rw_k_k, v_rw_k_a, v_rw_r_k, v_rw_wr, v_rw_wk, v_rw_wv, v_rw_wo, v_rw_lnx_g, v_rw_lnx_b, v_s5_a_re, v_s5_a_im, v_s5_log_dt, v_s5_b_re, v_s5_b_im, v_s5_c_re, v_s5_c_im, v_s5_d, v_s5_w_glu, v_mlp_w1, v_mlp_w2):
    d = dict(locals())
    x_pos, y_pos, c_pos = _position()
    chip = 2 * x_pos + y_pos
    chip_arr = jnp.reshape(chip, (1,)).astype(jnp.int32)
    core_arr = jnp.reshape(c_pos, (1,)).astype(jnp.int32)

    small = [n for n in SHARD_AXIS if n not in BIG]
    axes = [_work_axis(n) for n in small]
    first = ['rw_mu']
    later = [n for n in small if n not in first]
    big_views, first_fulls = _gather_early([_as2d(d[n]).astype(BF16) for n in BIG_EARLY], [_work(n, d[n]) for n in first],
                                           [_work_axis(n) for n in first])
    views = dict(zip(BIG_EARLY, big_views))
    fw = dict(zip(first, first_fulls))
    fw['small_late'] = (later, [_work(n, d[n]) for n in later], [_work_axis(n) for n in later])
    c_model = d['x'].shape[-1]
    for n in BIG_EARLY:
        fw[n] = views[n].reshape(c_model, c_model)
    w1_layers, w2_layers = d['mlp_w1'].astype(BF16), d['mlp_w2'].astype(BF16)
    fw['late_a'] = [_as2d(d['rw_wo']).astype(BF16), w1_layers[0], w2_layers[0]]
    fw['late_b'] = [w1_layers[1]]
    fw['late_c'] = [_as2d(d['s5_w_glu']).astype(BF16), w2_layers[1]]
    for n in REPLICATED:
        fw[n] = _replicated_2d(n, d[n])

    loss_blk, grad_x, grads, reduced, late_sums = _local_step(d['x'][0], d['loss_target'][0], fw, core_arr)
    loss = lax.psum(loss_blk[0, 0], ('x', 'y', 'c'))
    out = {}

    pieces = [grads[n] for n in small]
    lands, late_lands = _scatter_pieces(pieces, axes, list(late_sums.values()))
    reduced.update(zip(late_sums, zip(late_sums.values(), late_lands)))
    mine = [_sum4_big("sum4_" + n, *reduced[n], chip_arr) for n in BIG]
    for n, g, ax, land in zip(small, pieces, axes, lands):
        size = g.shape[ax] // 4
        mine.append(_sum4("sum4_" + n, lax.dynamic_slice_in_dim(g, chip * size, size, ax), land))
    theirs = _sibling_swap("swap_sums", mine)
    for n, a, b in zip(BIG + small, mine, theirs):
        w2d, m2d, v2d = _as2d(d[n]), _as2d(d['m_' + n]), _as2d(d['v_' + n])
        res = (_adamw_halves("adamw_" + n, a, b, w2d, m2d, v2d, core_arr) if n in BIG
               else _adamw("adamw_" + n, (a, b), w2d, m2d, v2d))
        out[n] = [r.reshape(d[n].shape) for r in res]

    rep_shapes = [d[n].shape for n in REPLICATED]
    packs = [_pack([grads[n] for n in REPLICATED])] + [_pack([d[p + n] for n in REPLICATED]) for p in ('', 'm_', 'v_')]
    res = [_unpack(p, rep_shapes) for p in _allreduce_adamw_small(*packs)]
    for i, n in enumerate(REPLICATED):
        out[n] = [r[i] for r in res]

    grad_x = grad_x.reshape(d['x'].shape)
    return (loss, grad_x, *[out[n][0] for n in WEIGHTS], *[out[n][1] for n in WEIGHTS],
            *[out[n][2] for n in WEIGHTS], *[out[n][3] for n in WEIGHTS])
```
